```python
import math
import jax, jax.numpy as jnp
from jax import lax
import numpy as np

D_MODEL = 1024
BATCH = 16
SEQ = 4096
DEPTH = 4

PLE_DIM = 256
HEAD_DIM = 64
N_Q_HEADS = 8
N_KV_HEADS = 2
GQA_GROUP = N_Q_HEADS // N_KV_HEADS
WINDOW = 128
BLOCK = 128
ROPE_THETA = 500000.0
ROPE_DIM = HEAD_DIM // 4
Q_WIDTH = N_Q_HEADS * HEAD_DIM
KV_WIDTH = N_KV_HEADS * HEAD_DIM
SSM_WIDTH = D_MODEL // 4
SSM_GROUP = 16
SSM_GROUPS = SSM_WIDTH // SSM_GROUP
SSM_STATE = 64
CONV_WIDTH = D_MODEL // 4
CONV_K = 31
FFN_HIDDEN = ((-(-8 * D_MODEL // 3) + 255) // 256) * 256
N_BRANCH = 3
IN_WIDTHS = [Q_WIDTH, KV_WIDTH, KV_WIDTH, SSM_WIDTH, 2 * CONV_WIDTH, N_BRANCH * D_MODEL]
IN_WIDTH = sum(IN_WIDTHS)
SPLIT_POINTS = [int(v) for v in np.cumsum(IN_WIDTHS)[:-1]]
EPS = 1e-6
NEG_INF = -1e30

kernel_name = "hybrid_gated_swa_s5_conformer_block"


def rms_norm(t, g):
    t32 = t.astype(jnp.float32)
    out = t32 * lax.rsqrt(jnp.mean(t32 * t32, axis=-1, keepdims=True) + EPS) * g.astype(jnp.float32)
    return out.astype(t.dtype)


def layer_norm(t, g, b):
    t32 = t.astype(jnp.float32)
    mu = jnp.mean(t32, axis=-1, keepdims=True)
    var = jnp.mean(jnp.square(t32 - mu), axis=-1, keepdims=True)
    out = (t32 - mu) * lax.rsqrt(var + EPS) * g.astype(jnp.float32) + b.astype(jnp.float32)
    return out.astype(t.dtype)


def partial_rope(t, cos, sin):
    half = ROPE_DIM // 2
    t1 = t[..., :half]
    t2 = t[..., half:ROPE_DIM]
    return jnp.concatenate([t1 * cos - t2 * sin, t2 * cos + t1 * sin, t[..., ROPE_DIM:]], axis=-1)


def sliding_window_attention(q, k, v, sinks):
    b, s = q.shape[0], q.shape[1]
    nb = s // BLOCK
    qb = q.reshape(b, nb, BLOCK, N_KV_HEADS, GQA_GROUP, HEAD_DIM)

    def band(t):
        tb = t.reshape(b, nb, BLOCK, N_KV_HEADS, HEAD_DIM)
        prev = jnp.pad(tb, ((0, 0), (1, 0), (0, 0), (0, 0), (0, 0)))[:, :-1]
        return jnp.concatenate([prev, tb], axis=2)

    kb, vb = band(k), band(v)
    scores = jnp.einsum('bnqkgd,bnjkd->bnkgqj', qb, kb) * (HEAD_DIM ** -0.5)
    qi = jnp.arange(BLOCK)[:, None]
    kj = jnp.arange(2 * BLOCK)[None, :]
    dist = qi + BLOCK - kj
    in_window = (dist >= 0) & (dist < WINDOW)
    has_prev = (jnp.arange(nb)[:, None, None] > 0) | (kj[None] >= BLOCK)
    mask = in_window[None] & has_prev
    scores = jnp.where(mask[None, :, None, None], scores, NEG_INF)
    sink = sinks.astype(jnp.float32).reshape(N_KV_HEADS, GQA_GROUP)[None, None, :, :, None, None]
    m = jnp.maximum(jnp.max(scores, axis=-1, keepdims=True), sink)
    pr = jnp.exp(scores - m)
    denom = jnp.sum(pr, axis=-1, keepdims=True) + jnp.exp(sink - m)
    out = jnp.einsum('bnkgqj,bnjkd->bnqkgd', pr / denom, vb)
    return out.reshape(b, s, Q_WIDTH)


def s5_ssm(u, lam_re, lam_im, log_dt, b_re, b_im, c_re, c_im, d_skip):
    b, s, _ = u.shape
    ug = u.reshape(b, s, SSM_GROUPS, SSM_GROUP)
    lr = jnp.minimum(lam_re.astype(jnp.float32), -1e-4)
    li = lam_im.astype(jnp.float32)
    dt = jnp.exp(log_dt.astype(jnp.float32))[:, None]
    mag = jnp.exp(lr * dt)
    a_re = mag * jnp.cos(li * dt)
    a_im = mag * jnp.sin(li * dt)
    den = lr * lr + li * li
    x_re, x_im = a_re - 1.0, a_im
    f_re = (x_re * lr + x_im * li) / den
    f_im = (x_im * lr - x_re * li) / den
    br = b_re.astype(jnp.float32)
    bi = b_im.astype(jnp.float32)
    bb_re = f_re[..., None] * br - f_im[..., None] * bi
    bb_im = f_re[..., None] * bi + f_im[..., None] * br
    bu_re = jnp.einsum('bsgh,gnh->bsgn', ug, bb_re)
    bu_im = jnp.einsum('bsgh,gnh->bsgn', ug, bb_im)
    a_re_t = jnp.broadcast_to(a_re, (1, s) + a_re.shape)
    a_im_t = jnp.broadcast_to(a_im, (1, s) + a_im.shape)

    def combine(e1, e2):
        a1r, a1i, b1r, b1i = e1
        a2r, a2i, b2r, b2i = e2
        return (a2r * a1r - a2i * a1i, a2r * a1i + a2i * a1r,
                a2r * b1r - a2i * b1i + b2r, a2r * b1i + a2i * b1r + b2i)

    _, _, st_re, st_im = lax.associative_scan(combine, (a_re_t, a_im_t, bu_re, bu_im), axis=1)
    y = (jnp.einsum('bsgn,ghn->bsgh', st_re, c_re.astype(jnp.float32))
         - jnp.einsum('bsgn,ghn->bsgh', st_im, c_im.astype(jnp.float32)))
    return y.reshape(b, s, SSM_WIDTH) + d_skip.astype(jnp.float32) * u


def conformer_conv(c_in, dw_w, dw_b, ln_g, ln_b, w_pw_out):
    a, g = jnp.split(c_in, 2, axis=-1)
    u = a * jax.nn.sigmoid(g)
    u = lax.conv_general_dilated(u, dw_w[:, None, :].astype(u.dtype), window_strides=(1,),
                                 padding=[(CONV_K - 1, 0)],
                                 dimension_numbers=('NWC', 'WIO', 'NWC'),
                                 feature_group_count=CONV_WIDTH) + dw_b
    u = jax.nn.silu(layer_norm(u, ln_g, ln_b))
    return u @ w_pw_out


def _fwd_setup_inputs(seed: int = 0) -> dict:
    key = jax.random.key(seed)
    ks = iter(jax.random.split(key, 40))
    f32 = jnp.float32

    def nrm(shape, scale):
        return jax.random.normal(next(ks), shape, f32) * scale

    L, D = DEPTH, D_MODEL
    x = nrm((BATCH, SEQ, D), 1.0)
    p = nrm((DEPTH, BATCH, SEQ, PLE_DIM), 1.0)
    positions = (jnp.arange(SEQ, dtype=jnp.int32)[None, :]
                 + jax.random.randint(next(ks), (BATCH, 1), 0, 1024, dtype=jnp.int32))
    n_idx = jnp.arange(SSM_STATE, dtype=f32)
    return {
        'x': x,
        'p': p,
        'positions': positions,
        'mix_norm_g': 1.0 + nrm((L, D), 0.02),
        'w_in': nrm((L, D, IN_WIDTH), D ** -0.5),
        'b_gate': nrm((L, N_BRANCH * D), 0.02),
        'attn_sinks': nrm((L, N_Q_HEADS), 0.5),
        'w_attn_out': nrm((L, Q_WIDTH, D), Q_WIDTH ** -0.5),
        'ssm_lambda_re': -0.5 + nrm((L, SSM_GROUPS, SSM_STATE), 0.01),
        'ssm_lambda_im': math.pi * n_idx + nrm((L, SSM_GROUPS, SSM_STATE), 0.01),
        'ssm_log_dt': jax.random.uniform(next(ks), (L, SSM_GROUPS), f32, math.log(1e-3), math.log(1e-1)),
        'ssm_b_re': nrm((L, SSM_GROUPS, SSM_STATE, SSM_GROUP), (2 * SSM_GROUP) ** -0.5),
        'ssm_b_im': nrm((L, SSM_GROUPS, SSM_STATE, SSM_GROUP), (2 * SSM_GROUP) ** -0.5),
        'ssm_c_re': nrm((L, SSM_GROUPS, SSM_GROUP, SSM_STATE), (2 * SSM_STATE) ** -0.5),
        'ssm_c_im': nrm((L, SSM_GROUPS, SSM_GROUP, SSM_STATE), (2 * SSM_STATE) ** -0.5),
        'ssm_d': nrm((L, SSM_WIDTH), 1.0),
        'w_ssm_glu': nrm((L, SSM_WIDTH, 2 * D), SSM_WIDTH ** -0.5),
        'b_ssm_glu': nrm((L, 2 * D), 0.02),
        'conv_dw_w': nrm((L, CONV_K, CONV_WIDTH), CONV_K ** -0.5),
        'conv_dw_b': nrm((L, CONV_WIDTH), 0.02),
        'conv_norm_g': 1.0 + nrm((L, CONV_WIDTH), 0.02),
        'conv_norm_b': nrm((L, CONV_WIDTH), 0.02),
        'w_conv_out': nrm((L, CONV_WIDTH, D), CONV_WIDTH ** -0.5),
        'w_mix_out': nrm((L, D, D), D ** -0.5),
        'ffn_norm_g': 1.0 + nrm((L, D), 0.02),
        'w_ffn_in': nrm((L, D, 2 * FFN_HIDDEN), D ** -0.5),
        'w_ffn_out': nrm((L, FFN_HIDDEN, D), FFN_HIDDEN ** -0.5),
        'w_ple_in': nrm((L, PLE_DIM, D), PLE_DIM ** -0.5),
        'ple_norm_g': 1.0 + nrm((L, D), 0.02),
        'w_ple_gate': nrm((L, D, D), D ** -0.5),
        'final_norm_g': 1.0 + nrm((D,), 0.02),
    }


def _fwd_reference(x, p, positions, mix_norm_g, w_in, b_gate, attn_sinks, w_attn_out,
              ssm_lambda_re, ssm_lambda_im, ssm_log_dt, ssm_b_re, ssm_b_im, ssm_c_re, ssm_c_im,
              ssm_d, w_ssm_glu, b_ssm_glu, conv_dw_w, conv_dw_b, conv_norm_g, conv_norm_b,
              w_conv_out, w_mix_out, ffn_norm_g, w_ffn_in, w_ffn_out, w_ple_in, ple_norm_g,
              w_ple_gate, final_norm_g):
    f32 = jnp.float32
    b, s, d = x.shape
    inv_freq = ROPE_THETA ** (-jnp.arange(0, ROPE_DIM, 2, dtype=f32) / ROPE_DIM)
    ang = positions.astype(f32)[..., None] * inv_freq
    cos = jnp.cos(ang)[:, :, None, :]
    sin = jnp.sin(ang)[:, :, None, :]

    for i in range(DEPTH):
        h = rms_norm(x, mix_norm_g[i])
        z = h @ w_in[i]
        q, k, v, s_in, c_in, g_in = jnp.split(z, SPLIT_POINTS, axis=-1)

        qh = partial_rope(q.astype(f32).reshape(b, s, N_Q_HEADS, HEAD_DIM), cos, sin)
        kh = partial_rope(k.astype(f32).reshape(b, s, N_KV_HEADS, HEAD_DIM), cos, sin)
        vh = v.astype(f32).reshape(b, s, N_KV_HEADS, HEAD_DIM)
        y_attn = sliding_window_attention(qh, kh, vh, attn_sinks[i]).astype(x.dtype) @ w_attn_out[i]

        y_s = s5_ssm(s_in.astype(f32), ssm_lambda_re[i], ssm_lambda_im[i], ssm_log_dt[i],
                     ssm_b_re[i], ssm_b_im[i], ssm_c_re[i], ssm_c_im[i], ssm_d[i])
        glu_a, glu_b = jnp.split(jax.nn.gelu(y_s).astype(x.dtype) @ w_ssm_glu[i] + b_ssm_glu[i], 2, axis=-1)
        y_ssm = glu_a * jax.nn.sigmoid(glu_b)

        y_conv = conformer_conv(c_in, conv_dw_w[i], conv_dw_b[i], conv_norm_g[i], conv_norm_b[i], w_conv_out[i])

        gates = jax.nn.sigmoid(g_in + b_gate[i]).reshape(b, s, N_BRANCH, d)
        merged = gates[:, :, 0] * y_attn + gates[:, :, 1] * y_ssm + gates[:, :, 2] * y_conv
        x = x + merged @ w_mix_out[i]

        hf = rms_norm(x, ffn_norm_g[i])
        f_gate, f_up = jnp.split(hf @ w_ffn_in[i], 2, axis=-1)
        x = x + (jax.nn.silu(f_gate) * f_up) @ w_ffn_out[i]

        e = p[i] @ w_ple_in[i]
        g_ple = jax.nn.sigmoid(rms_norm(x, ple_norm_g[i]) @ w_ple_gate[i])
        x = x + g_ple * e

    return rms_norm(x, final_norm_g)


import jax as _jax
import jax.numpy as _jnp

TWIN_FORMAT = 'train_step'
FWD_PARAMS = ['x', 'p', 'positions', 'mix_norm_g', 'w_in', 'b_gate', 'attn_sinks', 'w_attn_out', 'ssm_lambda_re', 'ssm_lambda_im', 'ssm_log_dt', 'ssm_b_re', 'ssm_b_im', 'ssm_c_re', 'ssm_c_im', 'ssm_d', 'w_ssm_glu', 'b_ssm_glu', 'conv_dw_w', 'conv_dw_b', 'conv_norm_g', 'conv_norm_b', 'w_conv_out', 'w_mix_out', 'ffn_norm_g', 'w_ffn_in', 'w_ffn_out', 'w_ple_in', 'ple_norm_g', 'w_ple_gate', 'final_norm_g']
TWIN_WEIGHTS = ['mix_norm_g', 'w_in', 'b_gate', 'attn_sinks', 'w_attn_out', 'ssm_lambda_re', 'ssm_lambda_im', 'ssm_log_dt', 'ssm_b_re', 'ssm_b_im', 'ssm_c_re', 'ssm_c_im', 'ssm_d', 'w_ssm_glu', 'b_ssm_glu', 'conv_dw_w', 'conv_dw_b', 'conv_norm_g', 'conv_norm_b', 'w_conv_out', 'w_mix_out', 'ffn_norm_g', 'w_ffn_in', 'w_ffn_out', 'w_ple_in', 'ple_norm_g', 'w_ple_gate', 'final_norm_g']
TWIN_DIFF_INPUT = 'x'
TWIN_INPUTS = ['x', 'p', 'positions', 'mix_norm_g', 'w_in', 'b_gate', 'attn_sinks', 'w_attn_out', 'ssm_lambda_re', 'ssm_lambda_im', 'ssm_log_dt', 'ssm_b_re', 'ssm_b_im', 'ssm_c_re', 'ssm_c_im', 'ssm_d', 'w_ssm_glu', 'b_ssm_glu', 'conv_dw_w', 'conv_dw_b', 'conv_norm_g', 'conv_norm_b', 'w_conv_out', 'w_mix_out', 'ffn_norm_g', 'w_ffn_in', 'w_ffn_out', 'w_ple_in', 'ple_norm_g', 'w_ple_gate', 'final_norm_g', 'loss_target', 'm_mix_norm_g', 'm_w_in', 'm_b_gate', 'm_attn_sinks', 'm_w_attn_out', 'm_ssm_lambda_re', 'm_ssm_lambda_im', 'm_ssm_log_dt', 'm_ssm_b_re', 'm_ssm_b_im', 'm_ssm_c_re', 'm_ssm_c_im', 'm_ssm_d', 'm_w_ssm_glu', 'm_b_ssm_glu', 'm_conv_dw_w', 'm_conv_dw_b', 'm_conv_norm_g', 'm_conv_norm_b', 'm_w_conv_out', 'm_w_mix_out', 'm_ffn_norm_g', 'm_w_ffn_in', 'm_w_ffn_out', 'm_w_ple_in', 'm_ple_norm_g', 'm_w_ple_gate', 'm_final_norm_g', 'v_mix_norm_g', 'v_w_in', 'v_b_gate', 'v_attn_sinks', 'v_w_attn_out', 'v_ssm_lambda_re', 'v_ssm_lambda_im', 'v_ssm_log_dt', 'v_ssm_b_re', 'v_ssm_b_im', 'v_ssm_c_re', 'v_ssm_c_im', 'v_ssm_d', 'v_w_ssm_glu', 'v_b_ssm_glu', 'v_conv_dw_w', 'v_conv_dw_b', 'v_conv_norm_g', 'v_conv_norm_b', 'v_w_conv_out', 'v_w_mix_out', 'v_ffn_norm_g', 'v_w_ffn_in', 'v_w_ffn_out', 'v_w_ple_in', 'v_ple_norm_g', 'v_w_ple_gate', 'v_final_norm_g']
TWIN_OUTPUTS = ['loss', 'grad_x', 'grad_mix_norm_g', 'grad_w_in', 'grad_b_gate', 'grad_attn_sinks', 'grad_w_attn_out', 'grad_ssm_lambda_re', 'grad_ssm_lambda_im', 'grad_ssm_log_dt', 'grad_ssm_b_re', 'grad_ssm_b_im', 'grad_ssm_c_re', 'grad_ssm_c_im', 'grad_ssm_d', 'grad_w_ssm_glu', 'grad_b_ssm_glu', 'grad_conv_dw_w', 'grad_conv_dw_b', 'grad_conv_norm_g', 'grad_conv_norm_b', 'grad_w_conv_out', 'grad_w_mix_out', 'grad_ffn_norm_g', 'grad_w_ffn_in', 'grad_w_ffn_out', 'grad_w_ple_in', 'grad_ple_norm_g', 'grad_w_ple_gate', 'grad_final_norm_g', 'delta_mix_norm_g', 'delta_w_in', 'delta_b_gate', 'delta_attn_sinks', 'delta_w_attn_out', 'delta_ssm_lambda_re', 'delta_ssm_lambda_im', 'delta_ssm_log_dt', 'delta_ssm_b_re', 'delta_ssm_b_im', 'delta_ssm_c_re', 'delta_ssm_c_im', 'delta_ssm_d', 'delta_w_ssm_glu', 'delta_b_ssm_glu', 'delta_conv_dw_w', 'delta_conv_dw_b', 'delta_conv_norm_g', 'delta_conv_norm_b', 'delta_w_conv_out', 'delta_w_mix_out', 'delta_ffn_norm_g', 'delta_w_ffn_in', 'delta_w_ffn_out', 'delta_w_ple_in', 'delta_ple_norm_g', 'delta_w_ple_gate', 'delta_final_norm_g', 'new_m_mix_norm_g', 'new_m_w_in', 'new_m_b_gate', 'new_m_attn_sinks', 'new_m_w_attn_out', 'new_m_ssm_lambda_re', 'new_m_ssm_lambda_im', 'new_m_ssm_log_dt', 'new_m_ssm_b_re', 'new_m_ssm_b_im', 'new_m_ssm_c_re', 'new_m_ssm_c_im', 'new_m_ssm_d', 'new_m_w_ssm_glu', 'new_m_b_ssm_glu', 'new_m_conv_dw_w', 'new_m_conv_dw_b', 'new_m_conv_norm_g', 'new_m_conv_norm_b', 'new_m_w_conv_out', 'new_m_w_mix_out', 'new_m_ffn_norm_g', 'new_m_w_ffn_in', 'new_m_w_ffn_out', 'new_m_w_ple_in', 'new_m_ple_norm_g', 'new_m_w_ple_gate', 'new_m_final_norm_g', 'new_v_mix_norm_g', 'new_v_w_in', 'new_v_b_gate', 'new_v_attn_sinks', 'new_v_w_attn_out', 'new_v_ssm_lambda_re', 'new_v_ssm_lambda_im', 'new_v_ssm_log_dt', 'new_v_ssm_b_re', 'new_v_ssm_b_im', 'new_v_ssm_c_re', 'new_v_ssm_c_im', 'new_v_ssm_d', 'new_v_w_ssm_glu', 'new_v_b_ssm_glu', 'new_v_conv_dw_w', 'new_v_conv_dw_b', 'new_v_conv_norm_g', 'new_v_conv_norm_b', 'new_v_w_conv_out', 'new_v_w_mix_out', 'new_v_ffn_norm_g', 'new_v_w_ffn_in', 'new_v_w_ffn_out', 'new_v_w_ple_in', 'new_v_ple_norm_g', 'new_v_w_ple_gate', 'new_v_final_norm_g']
TWIN_LEAF_KINDS = {'loss': 'loss', 'grad_x': 'grad_x', 'grad_mix_norm_g': 'grad_w', 'grad_w_in': 'grad_w', 'grad_b_gate': 'grad_w', 'grad_attn_sinks': 'grad_w', 'grad_w_attn_out': 'grad_w', 'grad_ssm_lambda_re': 'grad_w', 'grad_ssm_lambda_im': 'grad_w', 'grad_ssm_log_dt': 'grad_w', 'grad_ssm_b_re': 'grad_w', 'grad_ssm_b_im': 'grad_w', 'grad_ssm_c_re': 'grad_w', 'grad_ssm_c_im': 'grad_w', 'grad_ssm_d': 'grad_w', 'grad_w_ssm_glu': 'grad_w', 'grad_b_ssm_glu': 'grad_w', 'grad_conv_dw_w': 'grad_w', 'grad_conv_dw_b': 'grad_w', 'grad_conv_norm_g': 'grad_w', 'grad_conv_norm_b': 'grad_w', 'grad_w_conv_out': 'grad_w', 'grad_w_mix_out': 'grad_w', 'grad_ffn_norm_g': 'grad_w', 'grad_w_ffn_in': 'grad_w', 'grad_w_ffn_out': 'grad_w', 'grad_w_ple_in': 'grad_w', 'grad_ple_norm_g': 'grad_w', 'grad_w_ple_gate': 'grad_w', 'grad_final_norm_g': 'grad_w', 'delta_mix_norm_g': 'delta_w', 'delta_w_in': 'delta_w', 'delta_b_gate': 'delta_w', 'delta_attn_sinks': 'delta_w', 'delta_w_attn_out': 'delta_w', 'delta_ssm_lambda_re': 'delta_w', 'delta_ssm_lambda_im': 'delta_w', 'delta_ssm_log_dt': 'delta_w', 'delta_ssm_b_re': 'delta_w', 'delta_ssm_b_im': 'delta_w', 'delta_ssm_c_re': 'delta_w', 'delta_ssm_c_im': 'delta_w', 'delta_ssm_d': 'delta_w', 'delta_w_ssm_glu': 'delta_w', 'delta_b_ssm_glu': 'delta_w', 'delta_conv_dw_w': 'delta_w', 'delta_conv_dw_b': 'delta_w', 'delta_conv_norm_g': 'delta_w', 'delta_conv_norm_b': 'delta_w', 'delta_w_conv_out': 'delta_w', 'delta_w_mix_out': 'delta_w', 'delta_ffn_norm_g': 'delta_w', 'delta_w_ffn_in': 'delta_w', 'delta_w_ffn_out': 'delta_w', 'delta_w_ple_in': 'delta_w', 'delta_ple_norm_g': 'delta_w', 'delta_w_ple_gate': 'delta_w', 'delta_final_norm_g': 'delta_w', 'new_m_mix_norm_g': 'new_m', 'new_m_w_in': 'new_m', 'new_m_b_gate': 'new_m', 'new_m_attn_sinks': 'new_m', 'new_m_w_attn_out': 'new_m', 'new_m_ssm_lambda_re': 'new_m', 'new_m_ssm_lambda_im': 'new_m', 'new_m_ssm_log_dt': 'new_m', 'new_m_ssm_b_re': 'new_m', 'new_m_ssm_b_im': 'new_m', 'new_m_ssm_c_re': 'new_m', 'new_m_ssm_c_im': 'new_m', 'new_m_ssm_d': 'new_m', 'new_m_w_ssm_glu': 'new_m', 'new_m_b_ssm_glu': 'new_m', 'new_m_conv_dw_w': 'new_m', 'new_m_conv_dw_b': 'new_m', 'new_m_conv_norm_g': 'new_m', 'new_m_conv_norm_b': 'new_m', 'new_m_w_conv_out': 'new_m', 'new_m_w_mix_out': 'new_m', 'new_m_ffn_norm_g': 'new_m', 'new_m_w_ffn_in': 'new_m', 'new_m_w_ffn_out': 'new_m', 'new_m_w_ple_in': 'new_m', 'new_m_ple_norm_g': 'new_m', 'new_m_w_ple_gate': 'new_m', 'new_m_final_norm_g': 'new_m', 'new_v_mix_norm_g': 'new_v', 'new_v_w_in': 'new_v', 'new_v_b_gate': 'new_v', 'new_v_attn_sinks': 'new_v', 'new_v_w_attn_out': 'new_v', 'new_v_ssm_lambda_re': 'new_v', 'new_v_ssm_lambda_im': 'new_v', 'new_v_ssm_log_dt': 'new_v', 'new_v_ssm_b_re': 'new_v', 'new_v_ssm_b_im': 'new_v', 'new_v_ssm_c_re': 'new_v', 'new_v_ssm_c_im': 'new_v', 'new_v_ssm_d': 'new_v', 'new_v_w_ssm_glu': 'new_v', 'new_v_b_ssm_glu': 'new_v', 'new_v_conv_dw_w': 'new_v', 'new_v_conv_dw_b': 'new_v', 'new_v_conv_norm_g': 'new_v', 'new_v_conv_norm_b': 'new_v', 'new_v_w_conv_out': 'new_v', 'new_v_w_mix_out': 'new_v', 'new_v_ffn_norm_g': 'new_v', 'new_v_w_ffn_in': 'new_v', 'new_v_w_ffn_out': 'new_v', 'new_v_w_ple_in': 'new_v', 'new_v_ple_norm_g': 'new_v', 'new_v_w_ple_gate': 'new_v', 'new_v_final_norm_g': 'new_v'}


def _forward(args):
    return _fwd_reference(*[args[k] for k in FWD_PARAMS])


def _output_shape():
    out = _jax.eval_shape(lambda: _forward(_fwd_setup_inputs(0)))
    return out.shape, out.dtype

N_MICROBATCH = 1
ADAM_LR = 0.001
ADAM_B1 = 0.9
ADAM_B2 = 0.999
ADAM_EPS = 1e-08
ADAM_WD = 0.01
ADAM_STEP = 10
PER_EXAMPLE_BATCH_AXIS = {'x': 0, 'p': 1, 'positions': 0, 'loss_target': 0}
SHARED_INPUTS = []
_WEIGHT_DTYPES = {'mix_norm_g': _jnp.float32, 'w_in': _jnp.float32, 'b_gate': _jnp.float32, 'attn_sinks': _jnp.float32, 'w_attn_out': _jnp.float32, 'ssm_lambda_re': _jnp.float32, 'ssm_lambda_im': _jnp.float32, 'ssm_log_dt': _jnp.float32, 'ssm_b_re': _jnp.float32, 'ssm_b_im': _jnp.float32, 'ssm_c_re': _jnp.float32, 'ssm_c_im': _jnp.float32, 'ssm_d': _jnp.float32, 'w_ssm_glu': _jnp.float32, 'b_ssm_glu': _jnp.float32, 'conv_dw_w': _jnp.float32, 'conv_dw_b': _jnp.float32, 'conv_norm_g': _jnp.float32, 'conv_norm_b': _jnp.float32, 'w_conv_out': _jnp.float32, 'w_mix_out': _jnp.float32, 'ffn_norm_g': _jnp.float32, 'w_ffn_in': _jnp.float32, 'w_ffn_out': _jnp.float32, 'w_ple_in': _jnp.float32, 'ple_norm_g': _jnp.float32, 'w_ple_gate': _jnp.float32, 'final_norm_g': _jnp.float32}
MOMENT_SCALE = {'mix_norm_g': 9.491069e-02, 'w_in': 4.477470e-02, 'b_gate': 1.786269e-02, 'attn_sinks': 2.280321e-02, 'w_attn_out': 2.440131e-02, 'ssm_lambda_re': 5.297718e-03, 'ssm_lambda_im': 5.968549e-03, 'ssm_log_dt': 2.665299e+00, 'ssm_b_re': 2.709006e-03, 'ssm_b_im': 2.590986e-03, 'ssm_c_re': 5.353552e-03, 'ssm_c_im': 5.433514e-03, 'ssm_d': 8.261680e-02, 'w_ssm_glu': 2.866039e-02, 'b_ssm_glu': 4.295012e-02, 'conv_dw_w': 1.348645e-01, 'conv_dw_b': 3.094826e-01, 'conv_norm_g': 1.715667e-01, 'conv_norm_b': 1.449920e-01, 'w_conv_out': 6.466797e-02, 'w_mix_out': 7.856502e-02, 'ffn_norm_g': 1.445819e-01, 'w_ffn_in': 5.971357e-02, 'w_ffn_out': 9.747751e-02, 'w_ple_in': 8.730355e-02, 'ple_norm_g': 3.439145e-02, 'w_ple_gate': 3.422838e-02, 'final_norm_g': 6.396457e+01}


def _to_microbatches(a, axis):
    t = _jnp.moveaxis(a, axis, 0)
    t = t.reshape((N_MICROBATCH, t.shape[0] // N_MICROBATCH) + t.shape[1:])
    return _jnp.moveaxis(t, 1, axis + 1)


def setup_inputs(seed: int = 0) -> dict:
    inp = _fwd_setup_inputs(seed)
    key = _jax.random.fold_in(_jax.random.key(seed), 7919)
    shape, _ = _output_shape()
    out = dict(inp)
    out["loss_target"] = _jax.random.normal(_jax.random.fold_in(key, 0), shape, _jnp.float32)
    for i, name in enumerate(TWIN_WEIGHTS):
        w = inp[name].astype(_jnp.float32)
        if MOMENT_SCALE is None:
            s = _jnp.sqrt(_jnp.mean(_jnp.square(w)) + 1e-30)
        else:
            s = MOMENT_SCALE[name]
        km, kv = _jax.random.split(_jax.random.fold_in(key, i + 1))
        out[name] = w
        out["m_" + name] = s * _jax.random.normal(km, w.shape, _jnp.float32)
        out["v_" + name] = (s * s) * _jax.random.uniform(kv, w.shape, _jnp.float32, 0.5, 1.5)
    if N_MICROBATCH > 1:
        for name, axis in PER_EXAMPLE_BATCH_AXIS.items():
            out[name] = _to_microbatches(out[name], axis)
    return {'x': out['x'], 'p': out['p'], 'positions': out['positions'], 'mix_norm_g': out['mix_norm_g'], 'w_in': out['w_in'], 'b_gate': out['b_gate'], 'attn_sinks': out['attn_sinks'], 'w_attn_out': out['w_attn_out'], 'ssm_lambda_re': out['ssm_lambda_re'], 'ssm_lambda_im': out['ssm_lambda_im'], 'ssm_log_dt': out['ssm_log_dt'], 'ssm_b_re': out['ssm_b_re'], 'ssm_b_im': out['ssm_b_im'], 'ssm_c_re': out['ssm_c_re'], 'ssm_c_im': out['ssm_c_im'], 'ssm_d': out['ssm_d'], 'w_ssm_glu': out['w_ssm_glu'], 'b_ssm_glu': out['b_ssm_glu'], 'conv_dw_w': out['conv_dw_w'], 'conv_dw_b': out['conv_dw_b'], 'conv_norm_g': out['conv_norm_g'], 'conv_norm_b': out['conv_norm_b'], 'w_conv_out': out['w_conv_out'], 'w_mix_out': out['w_mix_out'], 'ffn_norm_g': out['ffn_norm_g'], 'w_ffn_in': out['w_ffn_in'], 'w_ffn_out': out['w_ffn_out'], 'w_ple_in': out['w_ple_in'], 'ple_norm_g': out['ple_norm_g'], 'w_ple_gate': out['w_ple_gate'], 'final_norm_g': out['final_norm_g'], 'loss_target': out['loss_target'], 'm_mix_norm_g': out['m_mix_norm_g'], 'm_w_in': out['m_w_in'], 'm_b_gate': out['m_b_gate'], 'm_attn_sinks': out['m_attn_sinks'], 'm_w_attn_out': out['m_w_attn_out'], 'm_ssm_lambda_re': out['m_ssm_lambda_re'], 'm_ssm_lambda_im': out['m_ssm_lambda_im'], 'm_ssm_log_dt': out['m_ssm_log_dt'], 'm_ssm_b_re': out['m_ssm_b_re'], 'm_ssm_b_im': out['m_ssm_b_im'], 'm_ssm_c_re': out['m_ssm_c_re'], 'm_ssm_c_im': out['m_ssm_c_im'], 'm_ssm_d': out['m_ssm_d'], 'm_w_ssm_glu': out['m_w_ssm_glu'], 'm_b_ssm_glu': out['m_b_ssm_glu'], 'm_conv_dw_w': out['m_conv_dw_w'], 'm_conv_dw_b': out['m_conv_dw_b'], 'm_conv_norm_g': out['m_conv_norm_g'], 'm_conv_norm_b': out['m_conv_norm_b'], 'm_w_conv_out': out['m_w_conv_out'], 'm_w_mix_out': out['m_w_mix_out'], 'm_ffn_norm_g': out['m_ffn_norm_g'], 'm_w_ffn_in': out['m_w_ffn_in'], 'm_w_ffn_out': out['m_w_ffn_out'], 'm_w_ple_in': out['m_w_ple_in'], 'm_ple_norm_g': out['m_ple_norm_g'], 'm_w_ple_gate': out['m_w_ple_gate'], 'm_final_norm_g': out['m_final_norm_g'], 'v_mix_norm_g': out['v_mix_norm_g'], 'v_w_in': out['v_w_in'], 'v_b_gate': out['v_b_gate'], 'v_attn_sinks': out['v_attn_sinks'], 'v_w_attn_out': out['v_w_attn_out'], 'v_ssm_lambda_re': out['v_ssm_lambda_re'], 'v_ssm_lambda_im': out['v_ssm_lambda_im'], 'v_ssm_log_dt': out['v_ssm_log_dt'], 'v_ssm_b_re': out['v_ssm_b_re'], 'v_ssm_b_im': out['v_ssm_b_im'], 'v_ssm_c_re': out['v_ssm_c_re'], 'v_ssm_c_im': out['v_ssm_c_im'], 'v_ssm_d': out['v_ssm_d'], 'v_w_ssm_glu': out['v_w_ssm_glu'], 'v_b_ssm_glu': out['v_b_ssm_glu'], 'v_conv_dw_w': out['v_conv_dw_w'], 'v_conv_dw_b': out['v_conv_dw_b'], 'v_conv_norm_g': out['v_conv_norm_g'], 'v_conv_norm_b': out['v_conv_norm_b'], 'v_w_conv_out': out['v_w_conv_out'], 'v_w_mix_out': out['v_w_mix_out'], 'v_ffn_norm_g': out['v_ffn_norm_g'], 'v_w_ffn_in': out['v_w_ffn_in'], 'v_w_ffn_out': out['v_w_ffn_out'], 'v_w_ple_in': out['v_w_ple_in'], 'v_ple_norm_g': out['v_ple_norm_g'], 'v_w_ple_gate': out['v_w_ple_gate'], 'v_final_norm_g': out['v_final_norm_g']}


def _loss(weights, diff, rest, loss_target):
    with _jax.named_scope("forward"):
        args = {**rest, TWIN_DIFF_INPUT: diff, **{k: w.astype(_WEIGHT_DTYPES[k]) for k, w in weights.items()}}
        y = _forward(args)
    with _jax.named_scope("loss_head"):
        err = _jnp.square(y.astype(_jnp.float32) - loss_target)
        return 0.5 * _jnp.sum(_jnp.mean(err, axis=-1)) if err.ndim else 0.5 * err


def _adamw(w, g, m, v):
    m = ADAM_B1 * m + (1.0 - ADAM_B1) * g
    v = ADAM_B2 * v + (1.0 - ADAM_B2) * _jnp.square(g)
    m_hat = m / (1.0 - ADAM_B1 ** ADAM_STEP)
    v_hat = v / (1.0 - ADAM_B2 ** ADAM_STEP)
    delta = -ADAM_LR * (m_hat / (_jnp.sqrt(v_hat) + ADAM_EPS) + ADAM_WD * w)
    return delta, m, v


def reference(x, p, positions, mix_norm_g, w_in, b_gate, attn_sinks, w_attn_out, ssm_lambda_re, ssm_lambda_im, ssm_log_dt, ssm_b_re, ssm_b_im, ssm_c_re, ssm_c_im, ssm_d, w_ssm_glu, b_ssm_glu, conv_dw_w, conv_dw_b, conv_norm_g, conv_norm_b, w_conv_out, w_mix_out, ffn_norm_g, w_ffn_in, w_ffn_out, w_ple_in, ple_norm_g, w_ple_gate, final_norm_g, loss_target, m_mix_norm_g, m_w_in, m_b_gate, m_attn_sinks, m_w_attn_out, m_ssm_lambda_re, m_ssm_lambda_im, m_ssm_log_dt, m_ssm_b_re, m_ssm_b_im, m_ssm_c_re, m_ssm_c_im, m_ssm_d, m_w_ssm_glu, m_b_ssm_glu, m_conv_dw_w, m_conv_dw_b, m_conv_norm_g, m_conv_norm_b, m_w_conv_out, m_w_mix_out, m_ffn_norm_g, m_w_ffn_in, m_w_ffn_out, m_w_ple_in, m_ple_norm_g, m_w_ple_gate, m_final_norm_g, v_mix_norm_g, v_w_in, v_b_gate, v_attn_sinks, v_w_attn_out, v_ssm_lambda_re, v_ssm_lambda_im, v_ssm_log_dt, v_ssm_b_re, v_ssm_b_im, v_ssm_c_re, v_ssm_c_im, v_ssm_d, v_w_ssm_glu, v_b_ssm_glu, v_conv_dw_w, v_conv_dw_b, v_conv_norm_g, v_conv_norm_b, v_w_conv_out, v_w_mix_out, v_ffn_norm_g, v_w_ffn_in, v_w_ffn_out, v_w_ple_in, v_ple_norm_g, v_w_ple_gate, v_final_norm_g):
    given = dict(x=x, p=p, positions=positions, mix_norm_g=mix_norm_g, w_in=w_in, b_gate=b_gate, attn_sinks=attn_sinks, w_attn_out=w_attn_out, ssm_lambda_re=ssm_lambda_re, ssm_lambda_im=ssm_lambda_im, ssm_log_dt=ssm_log_dt, ssm_b_re=ssm_b_re, ssm_b_im=ssm_b_im, ssm_c_re=ssm_c_re, ssm_c_im=ssm_c_im, ssm_d=ssm_d, w_ssm_glu=w_ssm_glu, b_ssm_glu=b_ssm_glu, conv_dw_w=conv_dw_w, conv_dw_b=conv_dw_b, conv_norm_g=conv_norm_g, conv_norm_b=conv_norm_b, w_conv_out=w_conv_out, w_mix_out=w_mix_out, ffn_norm_g=ffn_norm_g, w_ffn_in=w_ffn_in, w_ffn_out=w_ffn_out, w_ple_in=w_ple_in, ple_norm_g=ple_norm_g, w_ple_gate=w_ple_gate, final_norm_g=final_norm_g, loss_target=loss_target, m_mix_norm_g=m_mix_norm_g, m_w_in=m_w_in, m_b_gate=m_b_gate, m_attn_sinks=m_attn_sinks, m_w_attn_out=m_w_attn_out, m_ssm_lambda_re=m_ssm_lambda_re, m_ssm_lambda_im=m_ssm_lambda_im, m_ssm_log_dt=m_ssm_log_dt, m_ssm_b_re=m_ssm_b_re, m_ssm_b_im=m_ssm_b_im, m_ssm_c_re=m_ssm_c_re, m_ssm_c_im=m_ssm_c_im, m_ssm_d=m_ssm_d, m_w_ssm_glu=m_w_ssm_glu, m_b_ssm_glu=m_b_ssm_glu, m_conv_dw_w=m_conv_dw_w, m_conv_dw_b=m_conv_dw_b, m_conv_norm_g=m_conv_norm_g, m_conv_norm_b=m_conv_norm_b, m_w_conv_out=m_w_conv_out, m_w_mix_out=m_w_mix_out, m_ffn_norm_g=m_ffn_norm_g, m_w_ffn_in=m_w_ffn_in, m_w_ffn_out=m_w_ffn_out, m_w_ple_in=m_w_ple_in, m_ple_norm_g=m_ple_norm_g, m_w_ple_gate=m_w_ple_gate, m_final_norm_g=m_final_norm_g, v_mix_norm_g=v_mix_norm_g, v_w_in=v_w_in, v_b_gate=v_b_gate, v_attn_sinks=v_attn_sinks, v_w_attn_out=v_w_attn_out, v_ssm_lambda_re=v_ssm_lambda_re, v_ssm_lambda_im=v_ssm_lambda_im, v_ssm_log_dt=v_ssm_log_dt, v_ssm_b_re=v_ssm_b_re, v_ssm_b_im=v_ssm_b_im, v_ssm_c_re=v_ssm_c_re, v_ssm_c_im=v_ssm_c_im, v_ssm_d=v_ssm_d, v_w_ssm_glu=v_w_ssm_glu, v_b_ssm_glu=v_b_ssm_glu, v_conv_dw_w=v_conv_dw_w, v_conv_dw_b=v_conv_dw_b, v_conv_norm_g=v_conv_norm_g, v_conv_norm_b=v_conv_norm_b, v_w_conv_out=v_w_conv_out, v_w_mix_out=v_w_mix_out, v_ffn_norm_g=v_ffn_norm_g, v_w_ffn_in=v_w_ffn_in, v_w_ffn_out=v_w_ffn_out, v_w_ple_in=v_w_ple_in, v_ple_norm_g=v_ple_norm_g, v_w_ple_gate=v_w_ple_gate, v_final_norm_g=v_final_norm_g)
    weights = {n: given[n] for n in TWIN_WEIGHTS}
    shared = {n: given[n] for n in SHARED_INPUTS}
    per_example = {n: given[n] for n in ['x', 'p', 'positions']}
    grad_fn = _jax.value_and_grad(_loss, argnums=(0, 1))

    def one_microbatch(ex, loss_target):
        ex = dict(ex)
        diff = ex.pop(TWIN_DIFF_INPUT)
        return grad_fn(weights, diff, {**shared, **ex}, loss_target)

    if N_MICROBATCH == 1:
        loss, (grad_w, grad_x) = one_microbatch(per_example, given["loss_target"])
    else:
        def body(carry, xs):
            loss_sum, grad_sum = carry
            l_k, (gw_k, gx_k) = one_microbatch(xs[0], xs[1])
            with _jax.named_scope("update"):
                return (loss_sum + l_k, _jax.tree.map(_jnp.add, grad_sum, gw_k)), gx_k

        init = (_jnp.zeros((), _jnp.float32), _jax.tree.map(_jnp.zeros_like, weights))
        (loss, grad_w), grad_x = _jax.lax.scan(body, init, (per_example, given["loss_target"]))
    with _jax.named_scope("update"):
        delta_w, new_m, new_v = {}, {}, {}
        for n in TWIN_WEIGHTS:
            delta_w[n], new_m[n], new_v[n] = _adamw(weights[n], grad_w[n], given["m_" + n], given["v_" + n])
    return (loss, grad_x, *[grad_w[n] for n in TWIN_WEIGHTS], *[delta_w[n] for n in TWIN_WEIGHTS],
            *[new_m[n] for n in TWIN_WEIGHTS], *[new_v[n] for n in TWIN_WEIGHTS])
```

```python
import functools
import math

import numpy as np
import jax
import jax.numpy as jnp
from jax import lax
from jax.experimental import pallas as pl
from jax.experimental.pallas import tpu as pltpu

F32 = jnp.float32
BF16 = jnp.bfloat16
MXU_DTYPE = jnp.bfloat16
VMEM_LIMIT_BYTES = 56 * 2 ** 20
N_DEV = 8
LANES = 128

HEAD_DIM = 64
N_Q_HEADS = 8
N_KV_HEADS = 2
GQA_GROUP = 4
BLOCK = 128
ROPE_THETA = 500000.0
ROPE_DIM = 16
Q_WIDTH = 512
KV_WIDTH = 128
SSM_WIDTH = 256
SSM_GROUP = 16
SSM_GROUPS = 16
SSM_STATE = 64
SSM_LANES = SSM_GROUPS * SSM_STATE
CONV_WIDTH = 256
CONV_K = 31
CONV_HALO = 32
EPS = 1e-6
NEG_INF = -1e30
ADAM_LR, ADAM_B1, ADAM_B2, ADAM_EPS, ADAM_WD, ADAM_STEP = 0.001, 0.9, 0.999, 1e-08, 0.01, 10

ZG_W, ZQ_W, ZKV_W, ZS_W, ZC_W = 3072, 512, 256, 256, 512
ZQ_BLK, ZKV_BLK, ZS_BLK, ZC_BLK = 3072 // 512, 3584 // 256, 3840 // 256, 4096 // 512
Z_WIDTH = 4608
Z_SPLIT = 1536

SHARDED = (("w_in", 1024, 4608, 1), ("w_attn_out", 512, 1024, 1), ("w_ssm_glu", 256, 2048, 1),
           ("conv_dw_w", 31, 256, 1), ("w_conv_out", 256, 1024, 1), ("w_mix_out", 1024, 1024, 0),
           ("w_ffn_in", 1024, 5632, 1), ("w_ffn_out", 2816, 1024, 0), ("w_ple_in", 256, 1024, 1),
           ("w_ple_gate", 1024, 1024, 0))
PACK_ALIGN = 16 * LANES
FLAT_ROW_ALIGN = 1024
REPLICATED = ("mix_norm_g", "b_gate", "attn_sinks", "ssm_lambda_re", "ssm_lambda_im", "ssm_log_dt", "ssm_b_re",
              "ssm_b_im", "ssm_c_re", "ssm_c_im", "ssm_d", "b_ssm_glu", "conv_dw_b", "conv_norm_g", "conv_norm_b",
              "ffn_norm_g", "ple_norm_g", "final_norm_g")
WEIGHT_ORDER = ("mix_norm_g", "w_in", "b_gate", "attn_sinks", "w_attn_out", "ssm_lambda_re", "ssm_lambda_im",
                "ssm_log_dt", "ssm_b_re", "ssm_b_im", "ssm_c_re", "ssm_c_im", "ssm_d", "w_ssm_glu", "b_ssm_glu",
                "conv_dw_w", "conv_dw_b", "conv_norm_g", "conv_norm_b", "w_conv_out", "w_mix_out", "ffn_norm_g",
                "w_ffn_in", "w_ffn_out", "w_ple_in", "ple_norm_g", "w_ple_gate", "final_norm_g")


def _params(sem=None):
    return pltpu.CompilerParams(dimension_semantics=sem, vmem_limit_bytes=VMEM_LIMIT_BYTES)


def _pick(n, cands):
    for c in cands:
        if n % c == 0:
            return c
    return n


def _dot(a, b, dims):
    return lax.dot_general(a.astype(MXU_DTYPE), b.astype(MXU_DTYPE), (dims, ((), ())), preferred_element_type=F32)


def _dot_nn(a, b):
    return _dot(a, b, ((1,), (0,)))


def _dot_nt(a, b):
    return _dot(a, b, ((1,), (1,)))


def _dot_tn(a, b):
    return _dot(a, b, ((0,), (0,)))


@jax.custom_vjp
def _mm(x, w):
    return _dot_nn(x, w)


def _mm_f(x, w):
    return _dot_nn(x, w), (x, w)


def _mm_b(res, dy):
    x, w = res
    return _dot_nt(dy, w).astype(x.dtype), _dot_tn(x, dy).astype(w.dtype)


_mm.defvjp(_mm_f, _mm_b)


def _rms(x, g):
    return x * lax.rsqrt(jnp.mean(x * x, axis=-1, keepdims=True) + EPS) * g


def _matmul_nn(name, a, b, out_dtype):
    t, k = a.shape
    n = b.shape[1]
    tm, tn = _pick(t, (512, 256, 128)), _pick(n, (512, 256, 128))

    def body(a_ref, b_ref, o_ref):
        o_ref[...] = _dot_nn(a_ref[...], b_ref[...]).astype(o_ref.dtype)

    return pl.pallas_call(
        body, name=name, grid=(t // tm, n // tn), out_shape=jax.ShapeDtypeStruct((t, n), out_dtype),
        in_specs=[pl.BlockSpec((tm, k), lambda i, j: (i, 0)), pl.BlockSpec((k, tn), lambda i, j: (0, j))],
        out_specs=pl.BlockSpec((tm, tn), lambda i, j: (i, j)),
        compiler_params=_params(("parallel", "parallel")))(a, b)


def _matmul_nt(name, a, b, out_dtype):
    t, n = a.shape
    k = b.shape[0]
    tm, tk = _pick(t, (512, 256, 128)), _pick(k, (512, 256, 128))

    def body(a_ref, b_ref, o_ref):
        o_ref[...] = _dot_nt(a_ref[...], b_ref[...]).astype(o_ref.dtype)

    return pl.pallas_call(
        body, name=name, grid=(t // tm, k // tk), out_shape=jax.ShapeDtypeStruct((t, k), out_dtype),
        in_specs=[pl.BlockSpec((tm, n), lambda i, j: (i, 0)), pl.BlockSpec((tk, n), lambda i, j: (j, 0))],
        out_specs=pl.BlockSpec((tm, tk), lambda i, j: (i, j)),
        compiler_params=_params(("parallel", "parallel")))(a, b)


def _matmul_tn(name, a, b):
    t, m = a.shape
    n = b.shape[1]
    tm, tn, tt = _pick(m, (1024, 512, 256, 128)), _pick(n, (512, 256, 128)), _pick(t, (1024, 512, 256, 128))

    def body(a_ref, b_ref, o_ref):
        @pl.when(pl.program_id(2) == 0)
        def _():
            o_ref[...] = jnp.zeros_like(o_ref)

        o_ref[...] += _dot_tn(a_ref[...], b_ref[...])

    return pl.pallas_call(
        body, name=name, grid=(m // tm, n // tn, t // tt), out_shape=jax.ShapeDtypeStruct((m, n), F32),
        in_specs=[pl.BlockSpec((tt, tm), lambda i, j, s: (s, i)), pl.BlockSpec((tt, tn), lambda i, j, s: (s, j))],
        out_specs=pl.BlockSpec((tm, tn), lambda i, j, s: (i, j)),
        compiler_params=_params(("parallel", "parallel", "arbitrary")))(a, b)


def _token_call(name, fn, tile, tok_ins, consts, tok_outs, acc_outs):
    n_rows = tok_ins[0][0].shape[0]
    tile = min(tile, n_rows)
    n_ti, n_c, n_to = len(tok_ins), len(consts), len(tok_outs)

    def body(*refs):
        ins = [r[...] for r in refs[:n_ti + n_c]]
        outs, accs = fn(*ins)
        for r, v in zip(refs[n_ti + n_c:n_ti + n_c + n_to], outs, strict=True):
            r[...] = v.astype(r.dtype)
        first = pl.program_id(0) == 0
        for r, v in zip(refs[n_ti + n_c + n_to:], accs, strict=True):
            @pl.when(first)
            def _(r=r):
                r[...] = jnp.zeros_like(r)

            r[...] += jnp.broadcast_to(v, r.shape).astype(F32)

    in_specs = [pl.BlockSpec((tile, w), functools.partial(lambda i, c: (i, c), c=cb)) for _, w, cb in tok_ins]
    in_specs += [pl.BlockSpec(c.shape, lambda i: (0, 0)) for c in consts]
    out_shape = [jax.ShapeDtypeStruct((n_rows, w), dt) for w, dt in tok_outs]
    out_shape += [jax.ShapeDtypeStruct(s, F32) for s in acc_outs]
    out_specs = [pl.BlockSpec((tile, w), lambda i: (i, 0)) for w, _ in tok_outs]
    out_specs += [pl.BlockSpec(s, lambda i: (0, 0)) for s in acc_outs]
    res = pl.pallas_call(
        body, name=name, grid=(n_rows // tile,), out_shape=out_shape, in_specs=in_specs, out_specs=out_specs,
        compiler_params=_params(("arbitrary",)))(*[a for a, _, _ in tok_ins], *consts)
    return res[:n_to], res[n_to:]


def _whole(a):
    return (a, a.shape[1], 0)


def _norm_in_tile(x, g):
    return _rms(x, g)


def _merge_tile(x, ya, ys, uc, gin, w_ao, w_sg, b_sg, w_co, b_gate, w_mo, g_ffn):
    d = x.shape[1]
    y_attn = _mm(ya, w_ao)
    pre = _mm(jax.nn.gelu(ys), w_sg) + b_sg
    y_ssm = pre[:, :d] * jax.nn.sigmoid(pre[:, d:])
    y_conv = _mm(uc, w_co)
    gates = jax.nn.sigmoid(gin + b_gate)
    merged = gates[:, :d] * y_attn + gates[:, d:2 * d] * y_ssm + gates[:, 2 * d:] * y_conv
    x1 = x + _mm(merged, w_mo)
    return x1, _rms(x1, g_ffn)


def _act_tile(hid):
    f = hid.shape[1] // 2
    return jax.nn.silu(hid[:, :f]) * hid[:, f:]


def _ple_tile(x2, p, w_pi, g_ple, w_pg):
    return x2 + jax.nn.sigmoid(_mm(_rms(x2, g_ple), w_pg)) * _mm(p, w_pi)


def _f32s(vals):
    return [v.astype(F32) for v in vals]


def _rope_tables(positions, inv_lane):
    def fn(pos, inv):
        ang = pos.astype(F32) * inv
        j = lax.broadcasted_iota(jnp.int32, ang.shape, 1) % HEAD_DIM
        c = jnp.where(j < ROPE_DIM, jnp.cos(ang), 1.0)
        s = jnp.sin(ang)
        s = jnp.where(j < ROPE_DIM // 2, -s, jnp.where(j < ROPE_DIM, s, 0.0))
        return [c, s], []

    (c, s), _ = _token_call("rope_tables", fn, 1024, [_whole(positions.reshape(-1, 1))], [inv_lane],
                            [(LANES, F32), (LANES, F32)], [])
    return c, s


def _swap_halves(t):
    n = t.shape[1]
    j = lax.broadcasted_iota(jnp.int32, t.shape, 1) % HEAD_DIM
    lower = pltpu.roll(t, n - ROPE_DIM // 2, 1)
    upper = jnp.where(j < ROPE_DIM, pltpu.roll(t, ROPE_DIM // 2, 1), 0.0)
    return jnp.where(j < ROPE_DIM // 2, lower, upper)


def _rope(t, c, s):
    return t * c + _swap_halves(t) * s


def _rope_t(dt, c, s):
    return dt * c + _swap_halves(dt * s)


def _tile4(a):
    return jnp.concatenate([a] * (Q_WIDTH // LANES), axis=1)


def _attn_mask(n):
    qi = lax.broadcasted_iota(jnp.int32, (BLOCK, 2 * BLOCK), 0)
    kj = lax.broadcasted_iota(jnp.int32, (BLOCK, 2 * BLOCK), 1)
    dist = qi + BLOCK - kj
    return (dist >= 0) & (dist < BLOCK) & ((n > 0) | (kj >= BLOCK))


def _attn_specs(nb):
    row = lambda b, n: b * nb + n
    prev = lambda b, n: b * nb + jnp.maximum(n - 1, 0)
    return [pl.BlockSpec((BLOCK, ZQ_W), lambda b, n: (row(b, n), ZQ_BLK)),
            pl.BlockSpec((BLOCK, ZKV_W), lambda b, n: (row(b, n), ZKV_BLK)),
            pl.BlockSpec((BLOCK, ZKV_W), lambda b, n: (prev(b, n), ZKV_BLK)),
            pl.BlockSpec((BLOCK, LANES), lambda b, n: (row(b, n), 0)),
            pl.BlockSpec((BLOCK, LANES), lambda b, n: (row(b, n), 0)),
            pl.BlockSpec((BLOCK, LANES), lambda b, n: (prev(b, n), 0)),
            pl.BlockSpec((BLOCK, LANES), lambda b, n: (prev(b, n), 0)),
            pl.BlockSpec((1, N_Q_HEADS), lambda b, n: (0, 0))]


def _attn_band(q_ref, kv_ref, kvp_ref, c_ref, s_ref, cp_ref, sp_ref):
    c, s = c_ref[...], s_ref[...]
    q = _rope(q_ref[...], _tile4(c), _tile4(s))
    kv, kvp = kv_ref[...], kvp_ref[...]
    k = _rope(kv[:, :KV_WIDTH], c, s)
    kp = _rope(kvp[:, :KV_WIDTH], cp_ref[...], sp_ref[...])
    kb = jnp.concatenate([kp, k], axis=0)
    vb = jnp.concatenate([kvp[:, KV_WIDTH:], kv[:, KV_WIDTH:]], axis=0)
    return q, kb, vb


def _attention_fwd(z, ctab, stab, sinks, n_seq):
    t = z.shape[0]
    nb = t // n_seq // BLOCK

    def body(q_ref, kv_ref, kvp_ref, c_ref, s_ref, cp_ref, sp_ref, sink_ref, o_ref, lse_ref):
        q, kb, vb = _attn_band(q_ref, kv_ref, kvp_ref, c_ref, s_ref, cp_ref, sp_ref)
        mask = _attn_mask(pl.program_id(1))
        sink = sink_ref[...]
        lane = lax.broadcasted_iota(jnp.int32, (BLOCK, N_Q_HEADS), 1)
        lse_all = jnp.zeros((BLOCK, N_Q_HEADS), F32)
        for h in range(N_Q_HEADS):
            kh = h // GQA_GROUP
            sc = _dot_nt(q[:, h * HEAD_DIM:(h + 1) * HEAD_DIM], kb[:, kh * HEAD_DIM:(kh + 1) * HEAD_DIM]) * (HEAD_DIM ** -0.5)
            sc = jnp.where(mask, sc, NEG_INF)
            sk = sink[:, h:h + 1]
            m = jnp.maximum(jnp.max(sc, axis=-1, keepdims=True), sk)
            pr = jnp.exp(sc - m)
            den = jnp.sum(pr, axis=-1, keepdims=True) + jnp.exp(sk - m)
            out = _dot_nn(pr / den, vb[:, kh * HEAD_DIM:(kh + 1) * HEAD_DIM])
            o_ref[:, h * HEAD_DIM:(h + 1) * HEAD_DIM] = out.astype(o_ref.dtype)
            lse_all = jnp.where(lane == h, m + jnp.log(den), lse_all)
        lse_ref[...] = lse_all

    return pl.pallas_call(
        body, name="attn_fwd", grid=(n_seq, nb),
        out_shape=[jax.ShapeDtypeStruct((t, Q_WIDTH), MXU_DTYPE), jax.ShapeDtypeStruct((t, N_Q_HEADS), F32)],
        in_specs=_attn_specs(nb),
        out_specs=[pl.BlockSpec((BLOCK, Q_WIDTH), lambda b, n: (b * nb + n, 0)),
                   pl.BlockSpec((BLOCK, N_Q_HEADS), lambda b, n: (b * nb + n, 0))],
        compiler_params=_params(("parallel", "parallel")))(z, z, z, ctab, stab, ctab, stab, sinks)


def _attention_bwd(z, ctab, stab, sinks, ya, lse, dya, n_seq):
    t = z.shape[0]
    nb = t // n_seq // BLOCK

    def body(q_ref, kv_ref, kvp_ref, c_ref, s_ref, cp_ref, sp_ref, sink_ref, o_ref, lse_ref, do_ref,
             dq_ref, dkv_ref, dkvp_ref, dsink_ref):
        q, kb, vb = _attn_band(q_ref, kv_ref, kvp_ref, c_ref, s_ref, cp_ref, sp_ref)
        mask = _attn_mask(pl.program_id(1))
        sink = sink_ref[...]
        lse_all = lse_ref[...]
        o = o_ref[...].astype(F32)
        do = do_ref[...].astype(F32)
        lane = lax.broadcasted_iota(jnp.int32, (1, N_Q_HEADS), 1)
        dsink = jnp.zeros((1, N_Q_HEADS), F32)
        dq_parts = []
        dk_parts, dv_parts = [], []
        for kh in range(N_KV_HEADS):
            kbh = kb[:, kh * HEAD_DIM:(kh + 1) * HEAD_DIM]
            vbh = vb[:, kh * HEAD_DIM:(kh + 1) * HEAD_DIM]
            dkb = jnp.zeros((2 * BLOCK, HEAD_DIM), F32)
            dvb = jnp.zeros((2 * BLOCK, HEAD_DIM), F32)
            for g in range(GQA_GROUP):
                h = kh * GQA_GROUP + g
                qh = q[:, h * HEAD_DIM:(h + 1) * HEAD_DIM]
                doh = do[:, h * HEAD_DIM:(h + 1) * HEAD_DIM]
                lse_h = lse_all[:, h:h + 1]
                sc = jnp.where(mask, _dot_nt(qh, kbh) * (HEAD_DIM ** -0.5), NEG_INF)
                pr = jnp.exp(sc - lse_h)
                delta = jnp.sum(doh * o[:, h * HEAD_DIM:(h + 1) * HEAD_DIM], axis=-1, keepdims=True)
                dp = _dot_nt(doh, vbh)
                ds = pr * (dp - delta) * (HEAD_DIM ** -0.5)
                dq_parts.append(_dot_nn(ds, kbh))
                dkb = dkb + _dot_tn(ds, qh)
                dvb = dvb + _dot_tn(pr, doh)
                ps = jnp.exp(sink[:, h:h + 1] - lse_h)
                dsink = dsink + jnp.where(lane == h, -jnp.sum(ps * delta), 0.0)
            dk_parts.append(dkb)
            dv_parts.append(dvb)
        c, s = c_ref[...], s_ref[...]
        dq_ref[...] = _rope_t(jnp.concatenate(dq_parts, axis=1), _tile4(c), _tile4(s)).astype(dq_ref.dtype)
        dk = jnp.concatenate(dk_parts, axis=1)
        dv = jnp.concatenate(dv_parts, axis=1)
        dkv_ref[:, :KV_WIDTH] = _rope_t(dk[BLOCK:], c, s)
        dkv_ref[:, KV_WIDTH:] = dv[BLOCK:]
        dkvp_ref[:, :KV_WIDTH] = _rope_t(dk[:BLOCK], cp_ref[...], sp_ref[...])
        dkvp_ref[:, KV_WIDTH:] = dv[:BLOCK]

        @pl.when((pl.program_id(0) == 0) & (pl.program_id(1) == 0))
        def _():
            dsink_ref[...] = jnp.zeros_like(dsink_ref)

        dsink_ref[...] += dsink

    row_spec = lambda w: pl.BlockSpec((BLOCK, w), lambda b, n: (b * nb + n, 0))
    return pl.pallas_call(
        body, name="attn_bwd", grid=(n_seq, nb),
        out_shape=[jax.ShapeDtypeStruct((t, Q_WIDTH), MXU_DTYPE), jax.ShapeDtypeStruct((t, ZKV_W), F32),
                   jax.ShapeDtypeStruct((t, ZKV_W), F32), jax.ShapeDtypeStruct((1, N_Q_HEADS), F32)],
        in_specs=_attn_specs(nb) + [row_spec(Q_WIDTH), row_spec(N_Q_HEADS), row_spec(Q_WIDTH)],
        out_specs=[row_spec(Q_WIDTH), row_spec(ZKV_W), row_spec(ZKV_W), pl.BlockSpec((1, N_Q_HEADS), lambda b, n: (0, 0))],
        compiler_params=_params(("arbitrary", "arbitrary")))(z, z, z, ctab, stab, ctab, stab, sinks, ya, lse, dya)


def _assemble_dz(dgin, dq, dkv, dkvp, ds, dc, n_seq):
    t = dq.shape[0]
    nb = t // n_seq // BLOCK

    def body(dg_ref, dq_ref, dkv_ref, dkvn_ref, ds_ref, dc_ref, o_ref):
        last = pl.program_id(1) == nb - 1
        o_ref[:, :ZG_W] = dg_ref[...]
        o_ref[:, ZG_W:ZG_W + ZQ_W] = dq_ref[...]
        dkv_sum = dkv_ref[...] + jnp.where(last, 0.0, dkvn_ref[...])
        o_ref[:, ZG_W + ZQ_W:ZG_W + ZQ_W + ZKV_W] = dkv_sum.astype(o_ref.dtype)
        o_ref[:, ZG_W + ZQ_W + ZKV_W:ZG_W + ZQ_W + ZKV_W + ZS_W] = ds_ref[...]
        o_ref[:, ZG_W + ZQ_W + ZKV_W + ZS_W:] = dc_ref[...]

    row_spec = lambda w: pl.BlockSpec((BLOCK, w), lambda b, n: (b * nb + n, 0))
    nxt = pl.BlockSpec((BLOCK, ZKV_W), lambda b, n: (b * nb + jnp.minimum(n + 1, nb - 1), 0))
    return pl.pallas_call(
        body, name="assemble_dz", grid=(n_seq, nb), out_shape=jax.ShapeDtypeStruct((t, Z_WIDTH), MXU_DTYPE),
        in_specs=[row_spec(ZG_W), row_spec(ZQ_W), row_spec(ZKV_W), nxt, row_spec(ZS_W), row_spec(ZC_W)],
        out_specs=row_spec(Z_WIDTH), compiler_params=_params(("parallel", "parallel")))(dgin, dq, dkv, dkvp, ds, dc)


def _ssm_coeff_tile(lam_re, lam_im, log_dt):
    lr = jnp.minimum(lam_re, -1e-4)
    dt = jnp.exp(log_dt)
    mag = jnp.exp(lr * dt)
    a_re = mag * jnp.cos(lam_im * dt)
    a_im = mag * jnp.sin(lam_im * dt)
    den = lr * lr + lam_im * lam_im
    x_re = a_re - 1.0
    f_re = (x_re * lr + a_im * lam_im) / den
    f_im = (a_im * lr - x_re * lam_im) / den
    return a_re, a_im, f_re, f_im


def _ssm_coeffs(lam_re, lam_im, log_dt):
    def body(lr_ref, li_ref, dt_ref, *o_refs):
        for r, v in zip(o_refs, _ssm_coeff_tile(lr_ref[...], li_ref[...], dt_ref[...]), strict=True):
            r[...] = v

    return pl.pallas_call(body, name="ssm_coeffs", out_shape=[jax.ShapeDtypeStruct(lam_re.shape, F32)] * 4)(
        lam_re, lam_im, log_dt)


def _ssm_coeffs_bwd(lam_re, lam_im, log_dt, cts):
    def body(lr_ref, li_ref, dt_ref, c0, c1, c2, c3, dlr_ref, dli_ref, ddt_ref):
        _, vjp = jax.vjp(_ssm_coeff_tile, lr_ref[...], li_ref[...], dt_ref[...])
        dlr, dli, ddt = vjp((c0[...], c1[...], c2[...], c3[...]))
        dlr_ref[...] = dlr
        dli_ref[...] = dli
        ddt_ref[...] = ddt

    return pl.pallas_call(
        body, name="ssm_coeffs_bwd",
        out_shape=[jax.ShapeDtypeStruct(lam_re.shape, F32)] * 2 + [jax.ShapeDtypeStruct(log_dt.shape, F32)])(
        lam_re, lam_im, log_dt, *cts)


def _ssm_chunk(t):
    return _pick(t, (256, 128))


def _ssm_fwd(z, bmat, a_row, f_row, cmat, d_row, n_seq):
    t = z.shape[0]
    seq = t // n_seq
    lc = _ssm_chunk(seq)
    nc = seq // lc
    n2 = 2 * SSM_LANES

    def body(u_ref, b_ref, a_ref, f_ref, c_ref, d_ref, y_ref, s_ref, bu_ref, st_ref):
        @pl.when(pl.program_id(1) == 0)
        def _():
            st_ref[...] = jnp.zeros_like(st_ref)

        u = u_ref[...]
        proj = _dot_nn(u, b_ref[...])
        fr, fi = f_ref[:, :SSM_LANES], f_ref[:, SSM_LANES:]
        pr, pi = proj[:, :SSM_LANES], proj[:, SSM_LANES:]
        bu_ref[:, :SSM_LANES] = fr * pr - fi * pi
        bu_ref[:, SSM_LANES:] = fr * pi + fi * pr
        ar, ai = a_ref[:, :SSM_LANES], a_ref[:, SSM_LANES:]

        def step(i, carry):
            sr, si = carry
            nr = ar * sr - ai * si + bu_ref[pl.ds(i, 1), pl.ds(0, SSM_LANES)]
            ni = ar * si + ai * sr + bu_ref[pl.ds(i, 1), pl.ds(SSM_LANES, SSM_LANES)]
            s_ref[pl.ds(i, 1), pl.ds(0, SSM_LANES)] = nr
            s_ref[pl.ds(i, 1), pl.ds(SSM_LANES, SSM_LANES)] = ni
            return nr, ni

        sr, si = lax.fori_loop(0, lc, step, (st_ref[0:1, :SSM_LANES], st_ref[0:1, SSM_LANES:]), unroll=8)
        st_ref[0:1, :SSM_LANES] = sr
        st_ref[0:1, SSM_LANES:] = si
        y_ref[...] = _dot_nn(s_ref[...], c_ref[...]) + d_ref[...] * u

    const = lambda shape: pl.BlockSpec(shape, lambda b, c: (0, 0))
    return pl.pallas_call(
        body, name="ssm_fwd", grid=(n_seq, nc),
        out_shape=[jax.ShapeDtypeStruct((t, SSM_WIDTH), F32), jax.ShapeDtypeStruct((t, n2), F32)],
        in_specs=[pl.BlockSpec((lc, ZS_W), lambda b, c: (b * nc + c, ZS_BLK)), const((SSM_WIDTH, n2)), const((1, n2)),
                  const((1, n2)), const((n2, SSM_WIDTH)), const((1, SSM_WIDTH))],
        out_specs=[pl.BlockSpec((lc, SSM_WIDTH), lambda b, c: (b * nc + c, 0)),
                   pl.BlockSpec((lc, n2), lambda b, c: (b * nc + c, 0))],
        scratch_shapes=[pltpu.VMEM((lc, n2), F32), pltpu.VMEM((8, n2), F32)],
        compiler_params=_params(("arbitrary", "arbitrary")))(z, bmat, a_row, f_row, cmat, d_row)


def _ssm_bwd(z, states, dy, bmat, a_row, f_row, cmat, d_row, n_seq):
    t = z.shape[0]
    seq = t // n_seq
    lc = _ssm_chunk(seq)
    nc = seq // lc
    n2 = 2 * SSM_LANES

    def body(dy_ref, u_ref, s_ref, b_ref, a_ref, f_ref, c_ref, d_ref,
             du_ref, db_ref, dc_ref, da_ref, df_ref, dd_ref, g_ref, carry_ref):
        @pl.when((pl.program_id(0) == 0) & (pl.program_id(1) == 0))
        def _():
            for r in (db_ref, dc_ref, da_ref, df_ref, dd_ref):
                r[...] = jnp.zeros_like(r)

        @pl.when(pl.program_id(1) == 0)
        def _():
            carry_ref[...] = jnp.zeros_like(carry_ref)

        dy, u, st = dy_ref[...], u_ref[...], s_ref[...]
        g_ref[0:lc, :] = _dot_nt(dy, c_ref[...])
        g_ref[lc:lc + 8, :] = carry_ref[...]
        dc_ref[...] += _dot_tn(st, dy)
        dd_ref[...] += jnp.sum(dy * u, axis=0, keepdims=True)
        ar, ai = a_ref[:, :SSM_LANES], a_ref[:, SSM_LANES:]

        def step(i, carry):
            gr, gi = carry
            r = lc - 1 - i
            nr = g_ref[pl.ds(r, 1), pl.ds(0, SSM_LANES)] + ar * gr + ai * gi
            ni = g_ref[pl.ds(r, 1), pl.ds(SSM_LANES, SSM_LANES)] - ai * gr + ar * gi
            g_ref[pl.ds(r, 1), pl.ds(0, SSM_LANES)] = nr
            g_ref[pl.ds(r, 1), pl.ds(SSM_LANES, SSM_LANES)] = ni
            return nr, ni

        gr, gi = lax.fori_loop(0, lc, step, (carry_ref[0:1, :SSM_LANES], carry_ref[0:1, SSM_LANES:]), unroll=8)
        carry_ref[0:1, :SSM_LANES] = gr
        carry_ref[0:1, SSM_LANES:] = gi
        sr, si = st[:, :SSM_LANES], st[:, SSM_LANES:]
        gnr, gni = g_ref[pl.ds(1, lc), pl.ds(0, SSM_LANES)], g_ref[pl.ds(1, lc), pl.ds(SSM_LANES, SSM_LANES)]
        da_ref[:, :SSM_LANES] += jnp.sum(gnr * sr + gni * si, axis=0, keepdims=True)
        da_ref[:, SSM_LANES:] += jnp.sum(gni * sr - gnr * si, axis=0, keepdims=True)
        gr_all, gi_all = g_ref[0:lc, :SSM_LANES], g_ref[0:lc, SSM_LANES:]
        proj = _dot_nn(u, b_ref[...])
        pr, pi = proj[:, :SSM_LANES], proj[:, SSM_LANES:]
        df_ref[:, :SSM_LANES] += jnp.sum(gr_all * pr + gi_all * pi, axis=0, keepdims=True)
        df_ref[:, SSM_LANES:] += jnp.sum(gi_all * pr - gr_all * pi, axis=0, keepdims=True)
        fr, fi = f_ref[:, :SSM_LANES], f_ref[:, SSM_LANES:]
        dproj = jnp.concatenate([fr * gr_all + fi * gi_all, fr * gi_all - fi * gr_all], axis=1).astype(MXU_DTYPE)
        du_ref[...] = (_dot_nt(dproj, b_ref[...]) + d_ref[...] * dy).astype(du_ref.dtype)
        db_ref[...] += _dot_tn(u, dproj)

    const = lambda shape: pl.BlockSpec(shape, lambda b, c: (0, 0))
    rows = lambda w, cb: pl.BlockSpec((lc, w), functools.partial(lambda b, c, cb: (b * nc + nc - 1 - c, cb), cb=cb))
    return pl.pallas_call(
        body, name="ssm_bwd", grid=(n_seq, nc),
        out_shape=[jax.ShapeDtypeStruct((t, SSM_WIDTH), MXU_DTYPE), jax.ShapeDtypeStruct((SSM_WIDTH, n2), F32),
                   jax.ShapeDtypeStruct((n2, SSM_WIDTH), F32), jax.ShapeDtypeStruct((1, n2), F32),
                   jax.ShapeDtypeStruct((1, n2), F32), jax.ShapeDtypeStruct((1, SSM_WIDTH), F32)],
        in_specs=[rows(SSM_WIDTH, 0), rows(ZS_W, ZS_BLK), rows(n2, 0), const((SSM_WIDTH, n2)), const((1, n2)),
                  const((1, n2)), const((n2, SSM_WIDTH)), const((1, SSM_WIDTH))],
        out_specs=[rows(SSM_WIDTH, 0), const((SSM_WIDTH, n2)), const((n2, SSM_WIDTH)), const((1, n2)), const((1, n2)),
                   const((1, SSM_WIDTH))],
        scratch_shapes=[pltpu.VMEM((lc + 8, n2), F32), pltpu.VMEM((8, n2), F32)],
        compiler_params=_params(("arbitrary", "arbitrary")))(dy, z, states, bmat, a_row, f_row, cmat, d_row)


def _conv_chunk(t):
    return _pick(t, (512, 256, 128))


def _glu(c):
    return c[:, :CONV_WIDTH] * jax.nn.sigmoid(c[:, CONV_WIDTH:])


def _conv_post_tile(v, g, b):
    mu = jnp.mean(v, axis=-1, keepdims=True)
    var = jnp.mean(jnp.square(v - mu), axis=-1, keepdims=True)
    return jax.nn.silu((v - mu) * lax.rsqrt(var + EPS) * g + b)


def _conv_specs(lc, nc):
    per = lc // CONV_HALO
    return [pl.BlockSpec((lc, ZC_W), lambda b, c: (b * nc + c, ZC_BLK)),
            pl.BlockSpec((CONV_HALO, ZC_W), lambda b, c: (jnp.maximum((b * nc + c) * per - 1, 0), ZC_BLK))]


def _conv_fill(c_ref, cp_ref, ue_ref, lc):
    ue_ref[0:CONV_HALO, :] = jnp.where(pl.program_id(1) > 0, _glu(cp_ref[...]), 0.0)
    ue_ref[CONV_HALO:CONV_HALO + lc, :] = _glu(c_ref[...])


def _conv_apply(ue_ref, w_ref, b_ref, lc):
    acc = jnp.zeros((lc, CONV_WIDTH), F32) + b_ref[...]
    for k in range(CONV_K):
        acc = acc + w_ref[k:k + 1, :] * ue_ref[pl.ds(k + CONV_HALO - CONV_K + 1, lc), :]
    return acc


def _conv_fwd(z, dw_w, dw_b, ln_g, ln_b, n_seq):
    t = z.shape[0]
    seq = t // n_seq
    lc = _conv_chunk(seq)
    nc = seq // lc

    def body(c_ref, cp_ref, w_ref, b_ref, g_ref, lb_ref, o_ref, ue_ref):
        _conv_fill(c_ref, cp_ref, ue_ref, lc)
        o_ref[...] = _conv_post_tile(_conv_apply(ue_ref, w_ref, b_ref, lc), g_ref[...], lb_ref[...]).astype(o_ref.dtype)

    const = lambda a: pl.BlockSpec(a.shape, lambda b, c: (0, 0))
    return pl.pallas_call(
        body, name="conv_fwd", grid=(n_seq, nc), out_shape=jax.ShapeDtypeStruct((t, CONV_WIDTH), MXU_DTYPE),
        in_specs=_conv_specs(lc, nc) + [const(dw_w), const(dw_b), const(ln_g), const(ln_b)],
        out_specs=pl.BlockSpec((lc, CONV_WIDTH), lambda b, c: (b * nc + c, 0)),
        scratch_shapes=[pltpu.VMEM((CONV_HALO + lc, CONV_WIDTH), F32)],
        compiler_params=_params(("parallel", "parallel")))(z, z, dw_w, dw_b, ln_g, ln_b)


def _conv_bwd_post(z, duc, dw_w, dw_b, ln_g, ln_b, n_seq):
    t = z.shape[0]
    seq = t // n_seq
    lc = _conv_chunk(seq)
    nc = seq // lc

    def body(c_ref, cp_ref, duc_ref, w_ref, b_ref, g_ref, lb_ref, dv_ref, dg_ref, dlb_ref, db_ref, ue_ref):
        @pl.when((pl.program_id(0) == 0) & (pl.program_id(1) == 0))
        def _():
            for r in (dg_ref, dlb_ref, db_ref):
                r[...] = jnp.zeros_like(r)

        _conv_fill(c_ref, cp_ref, ue_ref, lc)
        _, vjp = jax.vjp(_conv_post_tile, _conv_apply(ue_ref, w_ref, b_ref, lc), g_ref[...], lb_ref[...])
        dv, dg, dlb = vjp(duc_ref[...])
        dv_ref[...] = dv
        dg_ref[...] += dg
        dlb_ref[...] += dlb
        db_ref[...] += jnp.sum(dv, axis=0, keepdims=True)

    const = lambda a: pl.BlockSpec(a.shape, lambda b, c: (0, 0))
    vec = jax.ShapeDtypeStruct((1, CONV_WIDTH), F32)
    return pl.pallas_call(
        body, name="conv_bwd_post", grid=(n_seq, nc), out_shape=[jax.ShapeDtypeStruct((t, CONV_WIDTH), F32), vec, vec, vec],
        in_specs=_conv_specs(lc, nc) + [pl.BlockSpec((lc, CONV_WIDTH), lambda b, c: (b * nc + c, 0)),
                                       const(dw_w), const(dw_b), const(ln_g), const(ln_b)],
        out_specs=[pl.BlockSpec((lc, CONV_WIDTH), lambda b, c: (b * nc + c, 0))] + [const(dw_b)] * 3,
        scratch_shapes=[pltpu.VMEM((CONV_HALO + lc, CONV_WIDTH), F32)],
        compiler_params=_params(("arbitrary", "arbitrary")))(z, z, duc, dw_w, dw_b, ln_g, ln_b)


def _conv_bwd_taps(z, dv, dw_w, n_seq):
    t = z.shape[0]
    seq = t // n_seq
    lc = _conv_chunk(seq)
    nc = seq // lc
    per = lc // CONV_HALO
    n_halo = t // CONV_HALO

    def body(c_ref, cp_ref, dv_ref, dvn_ref, w_ref, dc_ref, dw_ref, ue_ref, dve_ref):
        @pl.when((pl.program_id(0) == 0) & (pl.program_id(1) == 0))
        def _():
            dw_ref[...] = jnp.zeros_like(dw_ref)

        _conv_fill(c_ref, cp_ref, ue_ref, lc)
        dv = dv_ref[...]
        dve_ref[0:lc, :] = dv
        dve_ref[lc:lc + CONV_HALO, :] = jnp.where(pl.program_id(1) < nc - 1, dvn_ref[...], 0.0)
        du = jnp.zeros((lc, CONV_WIDTH), F32)
        for k in range(CONV_K):
            du = du + w_ref[k:k + 1, :] * dve_ref[pl.ds(CONV_K - 1 - k, lc), :]
            dw_ref[k:k + 1, :] += jnp.sum(dv * ue_ref[pl.ds(k + CONV_HALO - CONV_K + 1, lc), :], axis=0, keepdims=True)
        c = c_ref[...]
        a, sg = c[:, :CONV_WIDTH], jax.nn.sigmoid(c[:, CONV_WIDTH:])
        dc_ref[:, :CONV_WIDTH] = (du * sg).astype(dc_ref.dtype)
        dc_ref[:, CONV_WIDTH:] = (du * a * sg * (1.0 - sg)).astype(dc_ref.dtype)

    return pl.pallas_call(
        body, name="conv_bwd_taps", grid=(n_seq, nc),
        out_shape=[jax.ShapeDtypeStruct((t, ZC_W), MXU_DTYPE), jax.ShapeDtypeStruct((CONV_HALO, CONV_WIDTH), F32)],
        in_specs=_conv_specs(lc, nc) + [
            pl.BlockSpec((lc, CONV_WIDTH), lambda b, c: (b * nc + c, 0)),
            pl.BlockSpec((CONV_HALO, CONV_WIDTH), lambda b, c: (jnp.minimum((b * nc + c + 1) * per, n_halo - 1), 0)),
            pl.BlockSpec(dw_w.shape, lambda b, c: (0, 0))],
        out_specs=[pl.BlockSpec((lc, ZC_W), lambda b, c: (b * nc + c, 0)),
                   pl.BlockSpec((CONV_HALO, CONV_WIDTH), lambda b, c: (0, 0))],
        scratch_shapes=[pltpu.VMEM((CONV_HALO + lc, CONV_WIDTH), F32), pltpu.VMEM((lc + CONV_HALO, CONV_WIDTH), F32)],
        compiler_params=_params(("arbitrary", "arbitrary")))(z, z, dv, dv, dw_w)


def _row(v):
    return v.reshape(1, -1)


def _ssm_mats(b_re, b_im, c_re, c_im):
    eye = jnp.eye(SSM_GROUPS, dtype=F32)
    bm = jnp.stack([b_re, b_im])
    bmat = jnp.einsum("pgnh,gk->ghpkn", bm, eye).reshape(SSM_WIDTH, 2 * SSM_LANES)
    cm = jnp.stack([c_re, -c_im])
    cmat = jnp.einsum("pghn,gk->pgnkh", cm, eye).reshape(2 * SSM_LANES, SSM_WIDTH)
    return bmat.astype(MXU_DTYPE), cmat.astype(MXU_DTYPE)


def _ssm_mats_t(dbmat, dcmat):
    db = jnp.einsum("ghpgn->pgnh", dbmat.reshape(SSM_GROUPS, SSM_GROUP, 2, SSM_GROUPS, SSM_STATE))
    dc = jnp.einsum("pgngh->pghn", dcmat.reshape(2, SSM_GROUPS, SSM_STATE, SSM_GROUPS, SSM_GROUP))
    return db[0], db[1], dc[0], -dc[1]


def _layer_fwd(x, p, w, sp, ctab, stab, n_seq):
    (h,), _ = _token_call("norm_in", lambda x, g: ([_norm_in_tile(x, g)], []), 512, [_whole(x)], [sp["mix_norm_g"]],
                          [(x.shape[1], MXU_DTYPE)], [])
    z = _matmul_nn("mm_in", h, w["w_in"], F32)
    ya, lse = _attention_fwd(z, ctab, stab, sp["attn_sinks"], n_seq)
    ys, states = _ssm_fwd(z, sp["bmat"], sp["a_row"], sp["f_row"], sp["cmat"], sp["ssm_d"], n_seq)
    uc = _conv_fwd(z, w["conv_dw_w"], sp["conv_dw_b"], sp["conv_norm_g"], sp["conv_norm_b"], n_seq)
    merge_consts = [w["w_attn_out"], w["w_ssm_glu"], sp["b_ssm_glu"], w["w_conv_out"], sp["b_gate"], w["w_mix_out"],
                    sp["ffn_norm_g"]]
    (x1, hf), _ = _token_call("merge", lambda *a: (list(_merge_tile(*_f32s(a))), []), 256,
                              [_whole(x), _whole(ya), _whole(ys), _whole(uc), (z, ZG_W, 0)], merge_consts,
                              [(x.shape[1], F32), (x.shape[1], MXU_DTYPE)], [])
    hid = _matmul_nn("mm_ffn_in", hf, w["w_ffn_in"], F32)
    (act,), _ = _token_call("ffn_act", lambda hid: ([_act_tile(hid)], []), 256, [_whole(hid)], [],
                            [(hid.shape[1] // 2, MXU_DTYPE)], [])
    ffn = _matmul_nn("mm_ffn_out", act, w["w_ffn_out"], F32)

    def ple_fn(x1, ffn, p, w_pi, g_ple, w_pg):
        x2 = x1 + ffn
        return [x2, _ple_tile(x2, p, w_pi.astype(F32), g_ple, w_pg.astype(F32))], []

    (x2, x3), _ = _token_call("ple", ple_fn, 512, [_whole(x1), _whole(ffn), _whole(p)],
                              [w["w_ple_in"], sp["ple_norm_g"], w["w_ple_gate"]], [(x.shape[1], F32)] * 2, [])
    saved = dict(x=x, h=h, z=z, ya=ya, lse=lse, ys=ys, states=states, uc=uc, hf=hf, hid=hid, act=act, x2=x2, p=p)
    return x3, saved


def _layer_bwd(dx3, sv, w, sp, ctab, stab, n_seq):
    d = dx3.shape[1]
    gw, gs = {}, {}

    def ple_bwd(x2, p, dx3, w_pi, g_ple, w_pg):
        _, vjp = jax.vjp(lambda x2, w_pi, g_ple, w_pg: _ple_tile(x2, p, w_pi, g_ple, w_pg), x2, w_pi.astype(F32), g_ple,
                         w_pg.astype(F32))
        dx2, dw_pi, dg_ple, dw_pg = vjp(dx3)
        return [dx2, dx2], [dw_pi, dg_ple, dw_pg]

    (dx2, dffn), (gw["w_ple_in"], gs["ple_norm_g"], gw["w_ple_gate"]) = _token_call(
        "ple_bwd", ple_bwd, 256, [_whole(sv["x2"]), _whole(sv["p"]), _whole(dx3)],
        [w["w_ple_in"], sp["ple_norm_g"], w["w_ple_gate"]], [(d, F32), (d, MXU_DTYPE)],
        [w["w_ple_in"].shape, (1, d), w["w_ple_gate"].shape])

    dact = _matmul_nt("mm_ffn_out_dx", dffn, w["w_ffn_out"], F32)
    gw["w_ffn_out"] = _matmul_tn("mm_ffn_out_dw", sv["act"], dffn)

    def act_bwd(hid, dact):
        _, vjp = jax.vjp(_act_tile, hid)
        return [vjp(dact)[0]], []

    (dhid,), _ = _token_call("ffn_act_bwd", act_bwd, 256, [_whole(sv["hid"]), _whole(dact)], [],
                             [(sv["hid"].shape[1], MXU_DTYPE)], [])
    dhf = _matmul_nt("mm_ffn_in_dx", dhid, w["w_ffn_in"], F32)
    gw["w_ffn_in"] = _matmul_tn("mm_ffn_in_dw", sv["hf"], dhid)

    def merge_bwd(x, ya, ys, uc, gin, dx1, dhf, *consts):
        consts = _f32s(consts)
        _, vjp = jax.vjp(_merge_tile, *_f32s((x, ya, ys, uc, gin)), *consts)
        g = vjp((dx1, dhf))
        return list(g[:5]), list(g[5:])

    merge_consts = [w["w_attn_out"], w["w_ssm_glu"], sp["b_ssm_glu"], w["w_conv_out"], sp["b_gate"], w["w_mix_out"],
                    sp["ffn_norm_g"]]
    (dx_res, dya, dys, duc, dgin), macc = _token_call(
        "merge_bwd", merge_bwd, 256,
        [_whole(sv["x"]), _whole(sv["ya"]), _whole(sv["ys"]), _whole(sv["uc"]), (sv["z"], ZG_W, 0), _whole(dx2), _whole(dhf)],
        merge_consts, [(d, F32), (Q_WIDTH, MXU_DTYPE), (SSM_WIDTH, F32), (CONV_WIDTH, F32), (ZG_W, MXU_DTYPE)],
        [c.shape for c in merge_consts])
    gw["w_attn_out"], gw["w_ssm_glu"], gs["b_ssm_glu"], gw["w_conv_out"], gs["b_gate"], gw["w_mix_out"], gs["ffn_norm_g"] = macc

    dv, gs["conv_norm_g"], gs["conv_norm_b"], gs["conv_dw_b"] = _conv_bwd_post(
        sv["z"], duc, w["conv_dw_w"], sp["conv_dw_b"], sp["conv_norm_g"], sp["conv_norm_b"], n_seq)
    dzc, dw_taps = _conv_bwd_taps(sv["z"], dv, w["conv_dw_w"], n_seq)
    gw["conv_dw_w"] = dw_taps[:CONV_K]

    dzs, gs["bmat"], gs["cmat"], gs["a_row"], gs["f_row"], gs["ssm_d"] = _ssm_bwd(
        sv["z"], sv["states"], dys, sp["bmat"], sp["a_row"], sp["f_row"], sp["cmat"], sp["ssm_d"], n_seq)

    dzq, dkv, dkvp, gs["attn_sinks"] = _attention_bwd(sv["z"], ctab, stab, sp["attn_sinks"], sv["ya"], sv["lse"], dya, n_seq)
    dz = _assemble_dz(dgin, dzq, dkv, dkvp, dzs, dzc, n_seq)
    dh = _matmul_nt("mm_in_dx", dz, w["w_in"], F32)
    gw["w_in"] = _matmul_tn("mm_in_dw", sv["h"], dz)

    def norm_bwd(x, dh, dx_res, g):
        _, vjp = jax.vjp(_norm_in_tile, x, g)
        dx, dg = vjp(dh)
        return [dx + dx_res], [dg]

    (dx,), (gs["mix_norm_g"],) = _token_call("norm_in_bwd", norm_bwd, 512, [_whole(sv["x"]), _whole(dh), _whole(dx_res)],
                                             [sp["mix_norm_g"]], [(d, F32)], [(1, d)])
    return dx, gw, gs


def _loss_and_grad(x, target, g):
    def fn(x, tgt, g):
        def f(x, g):
            err = _rms(x, g) - tgt
            return 0.5 * jnp.mean(err * err, axis=-1, keepdims=True)

        per_token, vjp = jax.vjp(f, x, g)
        dx, dg = vjp(jnp.ones_like(per_token))
        return [dx], [jnp.sum(per_token, axis=0, keepdims=True), dg]

    (dx,), (loss, dg) = _token_call("loss", fn, 512, [_whole(x), _whole(target)], [g], [(x.shape[1], F32)],
                                    [(8, LANES), (1, x.shape[1])])
    return loss[0, 0], dx, dg


def _mesh_place():
    return lax.axis_index("x"), lax.axis_index("y"), lax.axis_index("c")


def _flip(v, bit):
    return 1 - v if bit else v


_ANY = pl.BlockSpec(memory_space=pl.ANY)
_MESH = pl.DeviceIdType.MESH


def _all_gather(name, x):
    def body(x_ref, out_ref, send_sems, recv_sems, local_sem):
        mx, my, mc = _mesh_place()
        me, sibling = (mx, my, mc), (mx, my, 1 - mc)
        chips = [(1 - mx, my), (mx, 1 - my), (1 - mx, 1 - my)]

        def slot(px, py, pc):
            return out_ref.at[4 * px + 2 * py + pc]

        def copy(k, block, to, src=None):
            return pltpu.make_async_remote_copy(
                src_ref=slot(*block) if src is None else src, dst_ref=slot(*block), send_sem=send_sems.at[k],
                recv_sem=recv_sems.at[k], device_id=to, device_id_type=_MESH)

        mine = pltpu.make_async_copy(x_ref, slot(*me), local_sem)
        mine.start()
        first = [copy(0, me, sibling, src=x_ref)]
        first += [copy(1 + j, me, (*chip, mc), src=x_ref) for j, chip in enumerate(chips)]
        for cp in first:
            cp.start()
        passed = [copy(4 + j, (*chip, mc), sibling) for j, chip in enumerate(chips)]
        for j, chip in enumerate(chips):
            copy(1 + j, (*chip, mc), me).wait_recv()
            passed[j].start()
        copy(0, sibling, me).wait_recv()
        for j, chip in enumerate(chips):
            copy(4 + j, (*chip, 1 - mc), me).wait_recv()
        for cp in first + passed:
            cp.wait_send()
        mine.wait()

    return pl.pallas_call(
        body, name=name, out_shape=jax.ShapeDtypeStruct((N_DEV,) + x.shape, x.dtype), in_specs=[_ANY], out_specs=_ANY,
        scratch_shapes=[pltpu.SemaphoreType.DMA((7,)), pltpu.SemaphoreType.DMA((7,)), pltpu.SemaphoreType.DMA])(x)


def _exchange(name, parts):
    def body(p_ref, land_ref, send_sems, recv_sems, local_sem):
        mx, my, mc = _mesh_place()
        me = 4 * mx + 2 * my + mc
        mine = pltpu.make_async_copy(p_ref.at[me], land_ref.at[me], local_sem)
        mine.start()
        copies = []
        for rel in range(1, N_DEV):
            px, py, pc = _flip(mx, rel & 4), _flip(my, rel & 2), _flip(mc, rel & 1)
            copies.append(pltpu.make_async_remote_copy(
                src_ref=p_ref.at[4 * px + 2 * py + pc], dst_ref=land_ref.at[me], send_sem=send_sems.at[rel - 1],
                recv_sem=recv_sems.at[rel - 1], device_id=(px, py, pc), device_id_type=_MESH))
        for cp in copies:
            cp.start()
        for cp in copies:
            cp.wait()
        mine.wait()

    return pl.pallas_call(
        body, name=name, out_shape=jax.ShapeDtypeStruct(parts.shape, parts.dtype), in_specs=[_ANY], out_specs=_ANY,
        scratch_shapes=[pltpu.SemaphoreType.DMA((7,)), pltpu.SemaphoreType.DMA((7,)), pltpu.SemaphoreType.DMA])(parts)


def _adamw(name, parts, w, m, v):
    r = w.shape[0]
    tile = _pick(r, (1024, 512, 256, 128, 8))

    def body(p_ref, w_ref, m_ref, v_ref, g_ref, d_ref, m2_ref, v2_ref):
        g = p_ref[0].astype(F32)
        for j in range(1, N_DEV):
            g = g + p_ref[j].astype(F32)
        m2 = ADAM_B1 * m_ref[...] + (1.0 - ADAM_B1) * g
        v2 = ADAM_B2 * v_ref[...] + (1.0 - ADAM_B2) * jnp.square(g)
        m_hat = m2 / (1.0 - ADAM_B1 ** ADAM_STEP)
        v_hat = v2 / (1.0 - ADAM_B2 ** ADAM_STEP)
        g_ref[...] = g
        d_ref[...] = -ADAM_LR * (m_hat / (jnp.sqrt(v_hat) + ADAM_EPS) + ADAM_WD * w_ref[...])
        m2_ref[...] = m2
        v2_ref[...] = v2

    flat = pl.BlockSpec((tile, LANES), lambda i: (i, 0))
    return pl.pallas_call(
        body, name=name, grid=(r // tile,), out_shape=[jax.ShapeDtypeStruct((r, LANES), F32)] * 4,
        in_specs=[pl.BlockSpec((N_DEV, tile, LANES), lambda i: (0, i, 0)), flat, flat, flat], out_specs=[flat] * 4,
        compiler_params=_params(("parallel",)))(parts, w, m, v)


CONV_W_PIECES = 3


def _pad_to(n, align):
    return -(-n // align) * align


def _segments(depth):
    segs = []
    for layer in range(depth):
        for name, rows, cols, axis in SHARDED:
            n = rows * cols // N_DEV
            for piece in range(CONV_W_PIECES if name == "conv_dw_w" else 1):
                segs.append((layer, name, rows, cols, axis, piece, n, _pad_to(n, PACK_ALIGN)))
    return segs


def _flat_rows(depth):
    return _pad_to(sum(s[-1] for s in _segments(depth)), FLAT_ROW_ALIGN * LANES) // LANES


def _pack(depth, piece_fn, lead=()):
    out, dtype = [], None
    for layer, name, _, _, _, piece, n, padded in _segments(depth):
        a = piece_fn(layer, name, piece)
        if a is None:
            out.append((None, padded))
            continue
        dtype = a.dtype
        out.append((jnp.pad(a, [(0, 0)] * len(lead) + [(0, padded - n)]), padded))
    total = _flat_rows(depth) * LANES
    used = sum(p for _, p in out)
    cols = [jnp.zeros(lead + (p,), dtype) if a is None else a for a, p in out]
    if total > used:
        cols.append(jnp.zeros(lead + (total - used,), dtype))
    return jnp.concatenate(cols, axis=-1).reshape(lead + (total // LANES, LANES))


def _unpack(depth, flat, lead=()):
    flat = flat.reshape(lead + (-1,))
    res, off = {}, 0
    for layer, name, _, _, _, piece, n, padded in _segments(depth):
        res[(layer, name, piece)] = flat[..., off:off + n]
        off += padded
    return res


def _to_shards(name, full):
    _, rows, cols, axis = next(s for s in SHARDED if s[0] == name)
    if axis == 1:
        return full.reshape(rows, N_DEV, cols // N_DEV).transpose(1, 0, 2).reshape(N_DEV, -1)
    return full.reshape(N_DEV, -1)


def _from_shards(name, shards):
    _, rows, cols, axis = next(s for s in SHARDED if s[0] == name)
    if axis == 1:
        return shards.reshape(N_DEV, rows, cols // N_DEV).transpose(1, 0, 2).reshape(rows, cols)
    return shards.reshape(rows, cols)


def _split3(a):
    hi = a.astype(BF16)
    r1 = a - hi.astype(F32)
    mid = r1.astype(BF16)
    return hi, mid, (r1 - mid.astype(F32)).astype(BF16)


def _pack_small(arrs, lead=()):
    flat = jnp.concatenate([a.reshape(lead + (-1,)) for a in arrs], axis=-1)
    total = _pad_to(flat.shape[-1], 512 * LANES)
    flat = jnp.pad(flat, [(0, 0)] * len(lead) + [(0, total - flat.shape[-1])])
    return flat.reshape(lead + (total // LANES, LANES))


def _unpack_small(flat, shapes):
    flat = flat.reshape(-1)
    res, off = [], 0
    for s in shapes:
        n = int(np.prod(s))
        res.append(flat[off:off + n].reshape(s))
        off += n
    return res


def _small_rows(a, depth):
    n16 = depth * SSM_GROUPS
    a_re, a_im, f_re, f_im = _ssm_coeffs(a["ssm_lambda_re"].reshape(n16, SSM_STATE), a["ssm_lambda_im"].reshape(n16, SSM_STATE),
                                         a["ssm_log_dt"].reshape(n16, 1))
    rows = []
    for l in range(depth):
        sp = {k: _row(a[k][l]) for k in ("mix_norm_g", "b_gate", "attn_sinks", "ssm_d", "b_ssm_glu", "conv_dw_b",
                                         "conv_norm_g", "conv_norm_b", "ffn_norm_g", "ple_norm_g")}
        g = slice(l * SSM_GROUPS, (l + 1) * SSM_GROUPS)
        sp["a_row"] = jnp.concatenate([a_re[g].reshape(1, -1), a_im[g].reshape(1, -1)], axis=1)
        sp["f_row"] = jnp.concatenate([f_re[g].reshape(1, -1), f_im[g].reshape(1, -1)], axis=1)
        sp["bmat"], sp["cmat"] = _ssm_mats(a["ssm_b_re"][l], a["ssm_b_im"][l], a["ssm_c_re"][l], a["ssm_c_im"][l])
        rows.append(sp)
    return rows


def _local_step(a, weights, depth):
    n_seq, seq, d = a["x"].shape
    t = n_seq * seq
    inv = ROPE_THETA ** (-jnp.arange(0, ROPE_DIM, 2, dtype=F32) / ROPE_DIM)
    lane = np.arange(LANES) % HEAD_DIM
    inv_lane = jnp.where(lane < ROPE_DIM, jnp.tile(inv, LANES // (ROPE_DIM // 2)), 0.0).reshape(1, LANES)
    ctab, stab = _rope_tables(a["positions"].reshape(t), inv_lane)
    small = _small_rows(a, depth)

    x = a["x"].reshape(t, d)
    saved = []
    for l in range(depth):
        x, sv = _layer_fwd(x, a["p"][l].reshape(t, -1), weights[l], small[l], ctab, stab, n_seq)
        saved.append(sv)
    loss, dx, d_final = _loss_and_grad(x, a["loss_target"].reshape(t, d), _row(a["final_norm_g"]))
    gws, gss = [None] * depth, [None] * depth
    for l in reversed(range(depth)):
        dx, gws[l], gss[l] = _layer_bwd(dx, saved[l], weights[l], small[l], ctab, stab, n_seq)

    n16 = depth * SSM_GROUPS
    halves = lambda k, h: jnp.concatenate([gss[l][k][:, h * SSM_LANES:(h + 1) * SSM_LANES].reshape(SSM_GROUPS, SSM_STATE)
                                           for l in range(depth)], axis=0)
    dlr, dli, ddt = _ssm_coeffs_bwd(a["ssm_lambda_re"].reshape(n16, SSM_STATE), a["ssm_lambda_im"].reshape(n16, SSM_STATE),
                                    a["ssm_log_dt"].reshape(n16, 1),
                                    (halves("a_row", 0), halves("a_row", 1), halves("f_row", 0), halves("f_row", 1)))
    bc = [_ssm_mats_t(gss[l]["bmat"], gss[l]["cmat"]) for l in range(depth)]
    gsmall = {k: jnp.stack([gss[l][k].reshape(a[k].shape[1:]) for l in range(depth)])
              for k in ("mix_norm_g", "b_gate", "attn_sinks", "ssm_d", "b_ssm_glu", "conv_dw_b", "conv_norm_g", "conv_norm_b",
                        "ffn_norm_g", "ple_norm_g")}
    gsmall["ssm_lambda_re"] = dlr.reshape(a["ssm_lambda_re"].shape)
    gsmall["ssm_lambda_im"] = dli.reshape(a["ssm_lambda_im"].shape)
    gsmall["ssm_log_dt"] = ddt.reshape(a["ssm_log_dt"].shape)
    for i, k in enumerate(("ssm_b_re", "ssm_b_im", "ssm_c_re", "ssm_c_im")):
        gsmall[k] = jnp.stack([bc[l][i] for l in range(depth)])
    gsmall["final_norm_g"] = d_final.reshape(a["final_norm_g"].shape)
    return loss, dx.reshape(n_seq, seq, d), gws, gsmall


def _permute_in(w):
    return jnp.concatenate([w[:, Z_SPLIT:], w[:, :Z_SPLIT]], axis=1)


def _unpermute_in(w):
    return jnp.concatenate([w[:, Z_WIDTH - Z_SPLIT:], w[:, :Z_WIDTH - Z_SPLIT]], axis=1)


def kernel(x, p, positions, mix_norm_g, w_in, b_gate, attn_sinks, w_attn_out, ssm_lambda_re, ssm_lambda_im, ssm_log_dt, ssm_b_re, ssm_b_im, ssm_c_re, ssm_c_im, ssm_d, w_ssm_glu, b_ssm_glu, conv_dw_w, conv_dw_b, conv_norm_g, conv_norm_b, w_conv_out, w_mix_out, ffn_norm_g, w_ffn_in, w_ffn_out, w_ple_in, ple_norm_g, w_ple_gate, final_norm_g, loss_target, m_mix_norm_g, m_w_in, m_b_gate, m_attn_sinks, m_w_attn_out, m_ssm_lambda_re, m_ssm_lambda_im, m_ssm_log_dt, m_ssm_b_re, m_ssm_b_im, m_ssm_c_re, m_ssm_c_im, m_ssm_d, m_w_ssm_glu, m_b_ssm_glu, m_conv_dw_w, m_conv_dw_b, m_conv_norm_g, m_conv_norm_b, m_w_conv_out, m_w_mix_out, m_ffn_norm_g, m_w_ffn_in, m_w_ffn_out, m_w_ple_in, m_ple_norm_g, m_w_ple_gate, m_final_norm_g, v_mix_norm_g, v_w_in, v_b_gate, v_attn_sinks, v_w_attn_out, v_ssm_lambda_re, v_ssm_lambda_im, v_ssm_log_dt, v_ssm_b_re, v_ssm_b_im, v_ssm_c_re, v_ssm_c_im, v_ssm_d, v_w_ssm_glu, v_b_ssm_glu, v_conv_dw_w, v_conv_dw_b, v_conv_norm_g, v_conv_norm_b, v_w_conv_out, v_w_mix_out, v_ffn_norm_g, v_w_ffn_in, v_w_ffn_out, v_w_ple_in, v_ple_norm_g, v_w_ple_gate, v_final_norm_g):
    a = dict(locals())
    depth = w_in.shape[0]
    sharded_names = [s[0] for s in SHARDED]

    def weight_piece(layer, name, piece):
        shard = a[name][layer].reshape(-1)
        return _split3(shard)[piece] if name == "conv_dw_w" else shard.astype(BF16)

    gathered = _unpack(depth, _all_gather("gather_weights", _pack(depth, weight_piece)), lead=(N_DEV,))
    weights = []
    for l in range(depth):
        w = {n: _from_shards(n, gathered[(l, n, 0)]) for n in sharded_names if n != "conv_dw_w"}
        w["conv_dw_w"] = sum(_from_shards("conv_dw_w", gathered[(l, "conv_dw_w", i)]).astype(F32) for i in range(CONV_W_PIECES))
        w["w_in"] = _permute_in(w["w_in"])
        weights.append(w)

    loss, grad_x, gws, gsmall = _local_step(a, weights, depth)
    loss = lax.psum(loss, ("x", "y", "c"))

    def grad_piece(layer, name, piece):
        if piece:
            return None
        g = _unpermute_in(gws[layer][name]) if name == "w_in" else gws[layer][name]
        return _to_shards(name, g).astype(BF16)

    landed = _exchange("exchange_grads", _pack(depth, grad_piece, lead=(N_DEV,)))
    state = [_pack(depth, lambda layer, name, piece, pre=pre: None if piece else a[pre + name][layer].reshape(-1))
             for pre in ("", "m_", "v_")]
    big = [_unpack(depth, o) for o in _adamw("adamw_sharded", landed, *state)]

    shapes = [a[k].shape for k in REPLICATED]
    parts = _all_gather("gather_small_grads", _pack_small([gsmall[k] for k in REPLICATED]))
    small_state = [_pack_small([a[pre + k] for k in REPLICATED]) for pre in ("", "m_", "v_")]
    small = [dict(zip(REPLICATED, _unpack_small(o, shapes), strict=True)) for o in _adamw("adamw_replicated", parts, *small_state)]

    def result(kind, name):
        if name in REPLICATED:
            return small[kind][name]
        return jnp.stack([big[kind][(l, name, 0)] for l in range(depth)]).reshape(a[name].shape)

    return (loss, grad_x, *[result(kind, n) for kind in range(4) for n in WEIGHT_ORDER])
```

```python
import functools
import math

import numpy as np
import jax
import jax.numpy as jnp
from jax import lax
from jax.experimental import pallas as pl
from jax.experimental.pallas import tpu as pltpu

F32 = jnp.float32
BF16 = jnp.bfloat16
MXU_DTYPE = jnp.bfloat16
VMEM_LIMIT_BYTES = 56 * 2 ** 20
N_DEV = 8
LANES = 128

HEAD_DIM = 64
N_Q_HEADS = 8
N_KV_HEADS = 2
GQA_GROUP = 4
BLOCK = 128
ROPE_THETA = 500000.0
ROPE_DIM = 16
Q_WIDTH = 512
KV_WIDTH = 128
SSM_WIDTH = 256
SSM_GROUP = 16
SSM_GROUPS = 16
SSM_STATE = 64
SSM_LANES = SSM_GROUPS * SSM_STATE
CONV_WIDTH = 256
CONV_K = 31
CONV_HALO = 32
EPS = 1e-6
NEG_INF = -1e30
ADAM_LR, ADAM_B1, ADAM_B2, ADAM_EPS, ADAM_WD, ADAM_STEP = 0.001, 0.9, 0.999, 1e-08, 0.01, 10

ZG_W, ZQ_W, ZKV_W, ZS_W, ZC_W = 3072, 512, 256, 256, 512
ZQ_BLK, ZKV_BLK, ZS_BLK, ZC_BLK = 3072 // 512, 3584 // 256, 3840 // 256, 4096 // 512
Z_WIDTH = 4608
Z_SPLIT = 1536

SHARDED = (("w_in", 1024, 4608, 1), ("w_attn_out", 512, 1024, 1), ("w_ssm_glu", 256, 2048, 1),
           ("conv_dw_w", 31, 256, 1), ("w_conv_out", 256, 1024, 1), ("w_mix_out", 1024, 1024, 0),
           ("w_ffn_in", 1024, 5632, 1), ("w_ffn_out", 2816, 1024, 0), ("w_ple_in", 256, 1024, 1),
           ("w_ple_gate", 1024, 1024, 0))
PACK_ALIGN = 16 * LANES
FLAT_ROW_ALIGN = 1024
REPLICATED = ("mix_norm_g", "b_gate", "attn_sinks", "ssm_lambda_re", "ssm_lambda_im", "ssm_log_dt", "ssm_b_re",
              "ssm_b_im", "ssm_c_re", "ssm_c_im", "ssm_d", "b_ssm_glu", "conv_dw_b", "conv_norm_g", "conv_norm_b",
              "ffn_norm_g", "ple_norm_g", "final_norm_g")
WEIGHT_ORDER = ("mix_norm_g", "w_in", "b_gate", "attn_sinks", "w_attn_out", "ssm_lambda_re", "ssm_lambda_im",
                "ssm_log_dt", "ssm_b_re", "ssm_b_im", "ssm_c_re", "ssm_c_im", "ssm_d", "w_ssm_glu", "b_ssm_glu",
                "conv_dw_w", "conv_dw_b", "conv_norm_g", "conv_norm_b", "w_conv_out", "w_mix_out", "ffn_norm_g",
                "w_ffn_in", "w_ffn_out", "w_ple_in", "ple_norm_g", "w_ple_gate", "final_norm_g")


def _params(sem=None):
    return pltpu.CompilerParams(dimension_semantics=sem, vmem_limit_bytes=VMEM_LIMIT_BYTES)


def _pick(n, cands):
    for c in cands:
        if n % c == 0:
            return c
    return n


def _dot(a, b, dims):
    return lax.dot_general(a.astype(MXU_DTYPE), b.astype(MXU_DTYPE), (dims, ((), ())), preferred_element_type=F32)


def _dot_nn(a, b):
    return _dot(a, b, ((1,), (0,)))


def _dot_nt(a, b):
    return _dot(a, b, ((1,), (1,)))


def _dot_tn(a, b):
    return _dot(a, b, ((0,), (0,)))


@jax.custom_vjp
def _mm(x, w):
    return _dot_nn(x, w)


def _mm_f(x, w):
    return _dot_nn(x, w), (x, w)


def _mm_b(res, dy):
    x, w = res
    return _dot_nt(dy, w).astype(x.dtype), _dot_tn(x, dy).astype(w.dtype)


_mm.defvjp(_mm_f, _mm_b)


def _rms(x, g):
    return x * lax.rsqrt(jnp.mean(x * x, axis=-1, keepdims=True) + EPS) * g


ROW_TILES = (1024, 512, 256, 128)
COL_TILES = (1536, 1408, 1024, 512, 256, 128)


def _matmul_nn(name, a, b, out_dtype):
    t, k = a.shape
    n = b.shape[1]
    tm, tn = _pick(t, ROW_TILES), _pick(n, COL_TILES)

    def body(a_ref, b_ref, o_ref):
        o_ref[...] = _dot_nn(a_ref[...], b_ref[...]).astype(o_ref.dtype)

    return pl.pallas_call(
        body, name=name, grid=(t // tm, n // tn), out_shape=jax.ShapeDtypeStruct((t, n), out_dtype),
        in_specs=[pl.BlockSpec((tm, k), lambda i, j: (i, 0)), pl.BlockSpec((k, tn), lambda i, j: (0, j))],
        out_specs=pl.BlockSpec((tm, tn), lambda i, j: (i, j)),
        compiler_params=_params(("parallel", "parallel")))(a, b)


def _matmul_nt(name, a, b, out_dtype):
    t, n = a.shape
    k = b.shape[0]
    tm, tk = _pick(t, ROW_TILES[1:]), _pick(k, COL_TILES)

    def body(a_ref, b_ref, o_ref):
        o_ref[...] = _dot_nt(a_ref[...], b_ref[...]).astype(o_ref.dtype)

    return pl.pallas_call(
        body, name=name, grid=(t // tm, k // tk), out_shape=jax.ShapeDtypeStruct((t, k), out_dtype),
        in_specs=[pl.BlockSpec((tm, n), lambda i, j: (i, 0)), pl.BlockSpec((tk, n), lambda i, j: (j, 0))],
        out_specs=pl.BlockSpec((tm, tk), lambda i, j: (i, j)),
        compiler_params=_params(("parallel", "parallel")))(a, b)


def _matmul_tn(name, a, b):
    t, m = a.shape
    n = b.shape[1]
    tm, tn, tt = _pick(m, COL_TILES[1:]), _pick(n, COL_TILES), _pick(t, ROW_TILES)

    def body(a_ref, b_ref, o_ref):
        @pl.when(pl.program_id(2) == 0)
        def _():
            o_ref[...] = jnp.zeros_like(o_ref)

        o_ref[...] += _dot_tn(a_ref[...], b_ref[...])

    return pl.pallas_call(
        body, name=name, grid=(m // tm, n // tn, t // tt), out_shape=jax.ShapeDtypeStruct((m, n), F32),
        in_specs=[pl.BlockSpec((tt, tm), lambda i, j, s: (s, i)), pl.BlockSpec((tt, tn), lambda i, j, s: (s, j))],
        out_specs=pl.BlockSpec((tm, tn), lambda i, j, s: (i, j)),
        compiler_params=_params(("parallel", "parallel", "arbitrary")))(a, b)


def _ffn_in_act(hf, w_fi):
    t, k = hf.shape
    f = w_fi.shape[1] // 2
    tm, tf = _pick(t, ROW_TILES[1:]), _pick(f, COL_TILES)
    nf = f // tf

    def body(a_ref, wg_ref, wu_ref, g_ref, u_ref, act_ref):
        a = a_ref[...]
        g, u = _dot_nn(a, wg_ref[...]), _dot_nn(a, wu_ref[...])
        g_ref[...] = g.astype(g_ref.dtype)
        u_ref[...] = u.astype(u_ref.dtype)
        act_ref[...] = (jax.nn.silu(g) * u).astype(act_ref.dtype)

    out = pl.BlockSpec((tm, tf), lambda i, j: (i, j))
    return pl.pallas_call(
        body, name="ffn_in_act", grid=(t // tm, nf), out_shape=[jax.ShapeDtypeStruct((t, f), MXU_DTYPE)] * 3,
        in_specs=[pl.BlockSpec((tm, k), lambda i, j: (i, 0)), pl.BlockSpec((k, tf), lambda i, j: (0, j)),
                  pl.BlockSpec((k, tf), lambda i, j: (0, j + nf))],
        out_specs=[out, out, out], compiler_params=_params(("parallel", "parallel")))(hf, w_fi, w_fi)


def _ffn_mid_bwd(dffn, w_fo, gate, up):
    t, d = dffn.shape
    f = w_fo.shape[0]
    tm, tf = _pick(t, ROW_TILES[1:]), _pick(f, COL_TILES)

    def body(a_ref, w_ref, g_ref, u_ref, dg_ref, du_ref):
        dact = _dot_nt(a_ref[...], w_ref[...])
        g, u = g_ref[...].astype(F32), u_ref[...].astype(F32)
        sg = jax.nn.sigmoid(g)
        dg_ref[...] = (dact * u * sg * (1.0 + g * (1.0 - sg))).astype(dg_ref.dtype)
        du_ref[...] = (dact * g * sg).astype(du_ref.dtype)

    blk = pl.BlockSpec((tm, tf), lambda i, j: (i, j))
    return pl.pallas_call(
        body, name="ffn_mid_bwd", grid=(t // tm, f // tf), out_shape=[jax.ShapeDtypeStruct((t, f), MXU_DTYPE)] * 2,
        in_specs=[pl.BlockSpec((tm, d), lambda i, j: (i, 0)), pl.BlockSpec((tf, d), lambda i, j: (j, 0)), blk, blk],
        out_specs=[blk, blk], compiler_params=_params(("parallel", "parallel")))(dffn, w_fo, gate, up)


def _ffn_in_dx(dgate, dup, w_fi):
    t, f = dgate.shape
    d = w_fi.shape[0]
    tm = _pick(t, ROW_TILES[1:])

    def body(g_ref, u_ref, w_ref, o_ref):
        o_ref[...] = _dot_nt(g_ref[...], w_ref[:, :f]) + _dot_nt(u_ref[...], w_ref[:, f:])

    blk = pl.BlockSpec((tm, f), lambda i: (i, 0))
    return pl.pallas_call(
        body, name="ffn_in_dx", grid=(t // tm,), out_shape=jax.ShapeDtypeStruct((t, d), F32),
        in_specs=[blk, blk, pl.BlockSpec(w_fi.shape, lambda i: (0, 0))], out_specs=pl.BlockSpec((tm, d), lambda i: (i, 0)),
        compiler_params=_params(("parallel",)))(dgate, dup, w_fi)


def _token_call(name, fn, tile, tok_ins, consts, tok_outs, acc_outs):
    n_rows = tok_ins[0][0].shape[0]
    tile = min(tile, n_rows)
    n_ti, n_c, n_to = len(tok_ins), len(consts), len(tok_outs)

    def body(*refs):
        ins = [r[...] for r in refs[:n_ti + n_c]]
        outs, accs = fn(*ins)
        for r, v in zip(refs[n_ti + n_c:n_ti + n_c + n_to], outs, strict=True):
            r[...] = v.astype(r.dtype)
        first = pl.program_id(0) == 0
        for r, v in zip(refs[n_ti + n_c + n_to:], accs, strict=True):
            @pl.when(first)
            def _(r=r):
                r[...] = jnp.zeros_like(r)

            r[...] += jnp.broadcast_to(v, r.shape).astype(F32)

    in_specs = [pl.BlockSpec((tile, w), functools.partial(lambda i, c: (i, c), c=cb)) for _, w, cb in tok_ins]
    in_specs += [pl.BlockSpec(c.shape, lambda i: (0, 0)) for c in consts]
    out_shape = [jax.ShapeDtypeStruct((n_rows, w), dt) for w, dt in tok_outs]
    out_shape += [jax.ShapeDtypeStruct(s, F32) for s in acc_outs]
    out_specs = [pl.BlockSpec((tile, w), lambda i: (i, 0)) for w, _ in tok_outs]
    out_specs += [pl.BlockSpec(s, lambda i: (0, 0)) for s in acc_outs]
    res = pl.pallas_call(
        body, name=name, grid=(n_rows // tile,), out_shape=out_shape, in_specs=in_specs, out_specs=out_specs,
        compiler_params=_params(("arbitrary",)))(*[a for a, _, _ in tok_ins], *consts)
    return res[:n_to], res[n_to:]


def _whole(a):
    return (a, a.shape[1], 0)


def _norm_in_tile(x, g):
    return _rms(x, g)


def _merge_tile(x, ya, ys, uc, gin, w_ao, w_sg, b_sg, w_co, b_gate, w_mo, g_ffn):
    d = x.shape[1]
    y_attn = _mm(ya, w_ao)
    pre = _mm(jax.nn.gelu(ys), w_sg) + b_sg
    y_ssm = pre[:, :d] * jax.nn.sigmoid(pre[:, d:])
    y_conv = _mm(uc, w_co)
    gates = jax.nn.sigmoid(gin + b_gate)
    merged = gates[:, :d] * y_attn + gates[:, d:2 * d] * y_ssm + gates[:, 2 * d:] * y_conv
    x1 = x + _mm(merged, w_mo)
    return x1, _rms(x1, g_ffn)


def _ple_tile(x2, p, w_pi, g_ple, w_pg):
    return x2 + jax.nn.sigmoid(_mm(_rms(x2, g_ple), w_pg)) * _mm(p, w_pi)


def _f32s(vals):
    return [v.astype(F32) for v in vals]


def _rope_tables(positions, inv_lane):
    def fn(pos, inv):
        ang = pos.astype(F32) * inv
        j = lax.broadcasted_iota(jnp.int32, ang.shape, 1) % HEAD_DIM
        c = jnp.where(j < ROPE_DIM, jnp.cos(ang), 1.0)
        s = jnp.sin(ang)
        s = jnp.where(j < ROPE_DIM // 2, -s, jnp.where(j < ROPE_DIM, s, 0.0))
        return [c, s], []

    (c, s), _ = _token_call("rope_tables", fn, 1024, [_whole(positions.reshape(-1, 1))], [inv_lane],
                            [(LANES, F32), (LANES, F32)], [])
    return c, s


def _swap_halves(t):
    n = t.shape[1]
    j = lax.broadcasted_iota(jnp.int32, t.shape, 1) % HEAD_DIM
    lower = pltpu.roll(t, n - ROPE_DIM // 2, 1)
    upper = jnp.where(j < ROPE_DIM, pltpu.roll(t, ROPE_DIM // 2, 1), 0.0)
    return jnp.where(j < ROPE_DIM // 2, lower, upper)


def _rope(t, c, s):
    return t * c + _swap_halves(t) * s


def _rope_t(dt, c, s):
    return dt * c + _swap_halves(dt * s)


def _tile4(a):
    return jnp.concatenate([a] * (Q_WIDTH // LANES), axis=1)


def _attn_mask(n):
    qi = lax.broadcasted_iota(jnp.int32, (GQA_GROUP * BLOCK, 2 * BLOCK), 0) % BLOCK
    kj = lax.broadcasted_iota(jnp.int32, (GQA_GROUP * BLOCK, 2 * BLOCK), 1)
    dist = qi + BLOCK - kj
    return (dist >= 0) & (dist < BLOCK) & ((n > 0) | (kj >= BLOCK))


def _attn_specs(nb):
    row = lambda b, n: b * nb + n
    prev = lambda b, n: b * nb + jnp.maximum(n - 1, 0)
    return [pl.BlockSpec((BLOCK, ZQ_W), lambda b, n: (row(b, n), ZQ_BLK)),
            pl.BlockSpec((BLOCK, ZKV_W), lambda b, n: (row(b, n), ZKV_BLK)),
            pl.BlockSpec((BLOCK, ZKV_W), lambda b, n: (prev(b, n), ZKV_BLK)),
            pl.BlockSpec((BLOCK, LANES), lambda b, n: (row(b, n), 0)),
            pl.BlockSpec((BLOCK, LANES), lambda b, n: (row(b, n), 0)),
            pl.BlockSpec((BLOCK, LANES), lambda b, n: (prev(b, n), 0)),
            pl.BlockSpec((BLOCK, LANES), lambda b, n: (prev(b, n), 0)),
            pl.BlockSpec((1, N_Q_HEADS), lambda b, n: (0, 0))]


ATTN_SCALE = HEAD_DIM ** -0.5


def _stack_heads(t, kh):
    return jnp.concatenate([t[:, (kh * GQA_GROUP + g) * HEAD_DIM:(kh * GQA_GROUP + g + 1) * HEAD_DIM]
                            for g in range(GQA_GROUP)], axis=0)


def _stack_sinks(sink, kh):
    return jnp.concatenate([jnp.broadcast_to(sink[:, kh * GQA_GROUP + g:kh * GQA_GROUP + g + 1], (BLOCK, 1))
                            for g in range(GQA_GROUP)], axis=0)


def _attn_band(q_ref, kv_ref, kvp_ref, c_ref, s_ref, cp_ref, sp_ref):
    c, s = c_ref[...], s_ref[...]
    q = _rope(q_ref[...], _tile4(c), _tile4(s)) * ATTN_SCALE
    kv, kvp = kv_ref[...], kvp_ref[...]
    k = _rope(kv[:, :KV_WIDTH], c, s)
    kp = _rope(kvp[:, :KV_WIDTH], cp_ref[...], sp_ref[...])
    kb = jnp.concatenate([kp, k], axis=0)
    vb = jnp.concatenate([kvp[:, KV_WIDTH:], kv[:, KV_WIDTH:]], axis=0)
    return q, kb, vb


def _attention_fwd(z, ctab, stab, sinks, n_seq):
    t = z.shape[0]
    nb = t // n_seq // BLOCK

    def body(q_ref, kv_ref, kvp_ref, c_ref, s_ref, cp_ref, sp_ref, sink_ref, o_ref, lse_ref):
        q, kb, vb = _attn_band(q_ref, kv_ref, kvp_ref, c_ref, s_ref, cp_ref, sp_ref)
        mask = _attn_mask(pl.program_id(1))
        sink = sink_ref[...]
        lane = lax.broadcasted_iota(jnp.int32, (BLOCK, N_Q_HEADS), 1)
        lse_all = jnp.zeros((BLOCK, N_Q_HEADS), F32)
        for kh in range(N_KV_HEADS):
            sc = jnp.where(mask, _dot_nt(_stack_heads(q, kh), kb[:, kh * HEAD_DIM:(kh + 1) * HEAD_DIM]), NEG_INF)
            sk = _stack_sinks(sink, kh)
            m = jnp.maximum(jnp.max(sc, axis=-1, keepdims=True), sk)
            pr = jnp.exp(sc - m)
            den = jnp.sum(pr, axis=-1, keepdims=True) + jnp.exp(sk - m)
            out = _dot_nn(pr * (1.0 / den), vb[:, kh * HEAD_DIM:(kh + 1) * HEAD_DIM])
            lse = m + jnp.log(den)
            for g in range(GQA_GROUP):
                h = kh * GQA_GROUP + g
                o_ref[:, h * HEAD_DIM:(h + 1) * HEAD_DIM] = out[g * BLOCK:(g + 1) * BLOCK].astype(o_ref.dtype)
                lse_all = jnp.where(lane == h, lse[g * BLOCK:(g + 1) * BLOCK], lse_all)
        lse_ref[...] = lse_all

    return pl.pallas_call(
        body, name="attn_fwd", grid=(n_seq, nb),
        out_shape=[jax.ShapeDtypeStruct((t, Q_WIDTH), MXU_DTYPE), jax.ShapeDtypeStruct((t, N_Q_HEADS), F32)],
        in_specs=_attn_specs(nb),
        out_specs=[pl.BlockSpec((BLOCK, Q_WIDTH), lambda b, n: (b * nb + n, 0)),
                   pl.BlockSpec((BLOCK, N_Q_HEADS), lambda b, n: (b * nb + n, 0))],
        compiler_params=_params(("parallel", "parallel")))(z, z, z, ctab, stab, ctab, stab, sinks)


def _attention_bwd(z, ctab, stab, sinks, ya, lse, dya, n_seq):
    t = z.shape[0]
    nb = t // n_seq // BLOCK

    def body(q_ref, kv_ref, kvp_ref, c_ref, s_ref, cp_ref, sp_ref, sink_ref, o_ref, lse_ref, do_ref,
             dq_ref, dkv_ref, dkvp_ref, dsink_ref):
        q, kb, vb = _attn_band(q_ref, kv_ref, kvp_ref, c_ref, s_ref, cp_ref, sp_ref)
        mask = _attn_mask(pl.program_id(1))
        sink = sink_ref[...]
        lse_all = lse_ref[...]
        o = o_ref[...].astype(F32)
        do = do_ref[...].astype(F32)
        lane = lax.broadcasted_iota(jnp.int32, (1, N_Q_HEADS), 1)
        dsink = jnp.zeros((1, N_Q_HEADS), F32)
        dq_parts = []
        dk_parts, dv_parts = [], []
        for kh in range(N_KV_HEADS):
            kbh = kb[:, kh * HEAD_DIM:(kh + 1) * HEAD_DIM]
            vbh = vb[:, kh * HEAD_DIM:(kh + 1) * HEAD_DIM]
            qs, dos = _stack_heads(q, kh), _stack_heads(do, kh)
            lse = jnp.concatenate([lse_all[:, kh * GQA_GROUP + g:kh * GQA_GROUP + g + 1] for g in range(GQA_GROUP)], axis=0)
            pr = jnp.exp(jnp.where(mask, _dot_nt(qs, kbh), NEG_INF) - lse)
            delta = jnp.sum(dos * _stack_heads(o, kh), axis=-1, keepdims=True)
            ds = pr * (_dot_nt(dos, vbh) - delta)
            dqs = _dot_nn(ds, kbh)
            dq_parts += [dqs[g * BLOCK:(g + 1) * BLOCK] for g in range(GQA_GROUP)]
            dk_parts.append(_dot_tn(ds, qs))
            dv_parts.append(_dot_tn(pr, dos))
            dsk = jnp.exp(_stack_sinks(sink, kh) - lse) * delta
            for g in range(GQA_GROUP):
                dsink = dsink + jnp.where(lane == kh * GQA_GROUP + g, -jnp.sum(dsk[g * BLOCK:(g + 1) * BLOCK]), 0.0)
        c, s = c_ref[...], s_ref[...]
        dq_ref[...] = _rope_t(jnp.concatenate(dq_parts, axis=1) * ATTN_SCALE, _tile4(c), _tile4(s)).astype(dq_ref.dtype)
        dk = jnp.concatenate(dk_parts, axis=1)
        dv = jnp.concatenate(dv_parts, axis=1)
        dkv_ref[:, :KV_WIDTH] = _rope_t(dk[BLOCK:], c, s)
        dkv_ref[:, KV_WIDTH:] = dv[BLOCK:]
        dkvp_ref[:, :KV_WIDTH] = _rope_t(dk[:BLOCK], cp_ref[...], sp_ref[...])
        dkvp_ref[:, KV_WIDTH:] = dv[:BLOCK]

        @pl.when((pl.program_id(0) == 0) & (pl.program_id(1) == 0))
        def _():
            dsink_ref[...] = jnp.zeros_like(dsink_ref)

        dsink_ref[...] += dsink

    row_spec = lambda w: pl.BlockSpec((BLOCK, w), lambda b, n: (b * nb + n, 0))
    return pl.pallas_call(
        body, name="attn_bwd", grid=(n_seq, nb),
        out_shape=[jax.ShapeDtypeStruct((t, Q_WIDTH), MXU_DTYPE), jax.ShapeDtypeStruct((t, ZKV_W), F32),
                   jax.ShapeDtypeStruct((t, ZKV_W), F32), jax.ShapeDtypeStruct((1, N_Q_HEADS), F32)],
        in_specs=_attn_specs(nb) + [row_spec(Q_WIDTH), row_spec(N_Q_HEADS), row_spec(Q_WIDTH)],
        out_specs=[row_spec(Q_WIDTH), row_spec(ZKV_W), row_spec(ZKV_W), pl.BlockSpec((1, N_Q_HEADS), lambda b, n: (0, 0))],
        compiler_params=_params(("arbitrary", "arbitrary")))(z, z, z, ctab, stab, ctab, stab, sinks, ya, lse, dya)


def _assemble_dz(dgin, dq, dkv, dkvp, ds, dc, n_seq):
    t = dq.shape[0]
    nb = t // n_seq // BLOCK

    def body(dg_ref, dq_ref, dkv_ref, dkvn_ref, ds_ref, dc_ref, o_ref):
        last = pl.program_id(1) == nb - 1
        o_ref[:, :ZG_W] = dg_ref[...]
        o_ref[:, ZG_W:ZG_W + ZQ_W] = dq_ref[...]
        dkv_sum = dkv_ref[...] + jnp.where(last, 0.0, dkvn_ref[...])
        o_ref[:, ZG_W + ZQ_W:ZG_W + ZQ_W + ZKV_W] = dkv_sum.astype(o_ref.dtype)
        o_ref[:, ZG_W + ZQ_W + ZKV_W:ZG_W + ZQ_W + ZKV_W + ZS_W] = ds_ref[...]
        o_ref[:, ZG_W + ZQ_W + ZKV_W + ZS_W:] = dc_ref[...]

    row_spec = lambda w: pl.BlockSpec((BLOCK, w), lambda b, n: (b * nb + n, 0))
    nxt = pl.BlockSpec((BLOCK, ZKV_W), lambda b, n: (b * nb + jnp.minimum(n + 1, nb - 1), 0))
    return pl.pallas_call(
        body, name="assemble_dz", grid=(n_seq, nb), out_shape=jax.ShapeDtypeStruct((t, Z_WIDTH), MXU_DTYPE),
        in_specs=[row_spec(ZG_W), row_spec(ZQ_W), row_spec(ZKV_W), nxt, row_spec(ZS_W), row_spec(ZC_W)],
        out_specs=row_spec(Z_WIDTH), compiler_params=_params(("parallel", "parallel")))(dgin, dq, dkv, dkvp, ds, dc)


def _ssm_coeff_tile(lam_re, lam_im, log_dt):
    lr = jnp.minimum(lam_re, -1e-4)
    dt = jnp.exp(log_dt)
    mag = jnp.exp(lr * dt)
    a_re = mag * jnp.cos(lam_im * dt)
    a_im = mag * jnp.sin(lam_im * dt)
    den = lr * lr + lam_im * lam_im
    x_re = a_re - 1.0
    f_re = (x_re * lr + a_im * lam_im) / den
    f_im = (a_im * lr - x_re * lam_im) / den
    return a_re, a_im, f_re, f_im


def _ssm_coeffs(lam_re, lam_im, log_dt):
    def body(lr_ref, li_ref, dt_ref, *o_refs):
        for r, v in zip(o_refs, _ssm_coeff_tile(lr_ref[...], li_ref[...], dt_ref[...]), strict=True):
            r[...] = v

    return pl.pallas_call(body, name="ssm_coeffs", out_shape=[jax.ShapeDtypeStruct(lam_re.shape, F32)] * 4)(
        lam_re, lam_im, log_dt)


def _ssm_coeffs_bwd(lam_re, lam_im, log_dt, cts):
    def body(lr_ref, li_ref, dt_ref, c0, c1, c2, c3, dlr_ref, dli_ref, ddt_ref):
        _, vjp = jax.vjp(_ssm_coeff_tile, lr_ref[...], li_ref[...], dt_ref[...])
        dlr, dli, ddt = vjp((c0[...], c1[...], c2[...], c3[...]))
        dlr_ref[...] = dlr
        dli_ref[...] = dli
        ddt_ref[...] = ddt

    return pl.pallas_call(
        body, name="ssm_coeffs_bwd",
        out_shape=[jax.ShapeDtypeStruct(lam_re.shape, F32)] * 2 + [jax.ShapeDtypeStruct(log_dt.shape, F32)])(
        lam_re, lam_im, log_dt, *cts)


def _ssm_chunk(t):
    return _pick(t, (256, 128))


def _ssm_fwd(z, bmat, a_row, f_row, cmat, d_row, n_seq):
    t = z.shape[0]
    seq = t // n_seq
    lc = _ssm_chunk(seq)
    nc = seq // lc
    n2 = 2 * SSM_LANES

    def body(u_ref, b_ref, a_ref, f_ref, c_ref, d_ref, y_ref, s_ref, bu_ref, st_ref):
        @pl.when(pl.program_id(1) == 0)
        def _():
            st_ref[...] = jnp.zeros_like(st_ref)

        u = u_ref[...]
        proj = _dot_nn(u, b_ref[...])
        fr, fi = f_ref[:, :SSM_LANES], f_ref[:, SSM_LANES:]
        pr, pi = proj[:, :SSM_LANES], proj[:, SSM_LANES:]
        bu_ref[:, :SSM_LANES] = fr * pr - fi * pi
        bu_ref[:, SSM_LANES:] = fr * pi + fi * pr
        ar, ai = a_ref[:, :SSM_LANES], a_ref[:, SSM_LANES:]

        def step(i, carry):
            sr, si = carry
            nr = ar * sr - ai * si + bu_ref[pl.ds(i, 1), pl.ds(0, SSM_LANES)]
            ni = ar * si + ai * sr + bu_ref[pl.ds(i, 1), pl.ds(SSM_LANES, SSM_LANES)]
            s_ref[pl.ds(i, 1), pl.ds(0, SSM_LANES)] = nr
            s_ref[pl.ds(i, 1), pl.ds(SSM_LANES, SSM_LANES)] = ni
            return nr, ni

        sr, si = lax.fori_loop(0, lc, step, (st_ref[0:1, :SSM_LANES], st_ref[0:1, SSM_LANES:]), unroll=8)
        st_ref[0:1, :SSM_LANES] = sr
        st_ref[0:1, SSM_LANES:] = si
        y_ref[...] = _dot_nn(s_ref[...], c_ref[...]) + d_ref[...] * u

    const = lambda shape: pl.BlockSpec(shape, lambda b, c: (0, 0))
    return pl.pallas_call(
        body, name="ssm_fwd", grid=(n_seq, nc),
        out_shape=[jax.ShapeDtypeStruct((t, SSM_WIDTH), F32), jax.ShapeDtypeStruct((t, n2), F32)],
        in_specs=[pl.BlockSpec((lc, ZS_W), lambda b, c: (b * nc + c, ZS_BLK)), const((SSM_WIDTH, n2)), const((1, n2)),
                  const((1, n2)), const((n2, SSM_WIDTH)), const((1, SSM_WIDTH))],
        out_specs=[pl.BlockSpec((lc, SSM_WIDTH), lambda b, c: (b * nc + c, 0)),
                   pl.BlockSpec((lc, n2), lambda b, c: (b * nc + c, 0))],
        scratch_shapes=[pltpu.VMEM((lc, n2), F32), pltpu.VMEM((8, n2), F32)],
        compiler_params=_params(("arbitrary", "arbitrary")))(z, bmat, a_row, f_row, cmat, d_row)


def _ssm_bwd(z, states, dy, bmat, a_row, f_row, cmat, d_row, n_seq):
    t = z.shape[0]
    seq = t // n_seq
    lc = _ssm_chunk(seq)
    nc = seq // lc
    n2 = 2 * SSM_LANES

    def body(dy_ref, u_ref, s_ref, b_ref, a_ref, f_ref, c_ref, d_ref,
             du_ref, db_ref, dc_ref, da_ref, df_ref, dd_ref, g_ref, carry_ref):
        @pl.when((pl.program_id(0) == 0) & (pl.program_id(1) == 0))
        def _():
            for r in (db_ref, dc_ref, da_ref, df_ref, dd_ref):
                r[...] = jnp.zeros_like(r)

        @pl.when(pl.program_id(1) == 0)
        def _():
            carry_ref[...] = jnp.zeros_like(carry_ref)

        dy, u, st = dy_ref[...], u_ref[...], s_ref[...]
        g_ref[0:lc, :] = _dot_nt(dy, c_ref[...])
        g_ref[lc:lc + 8, :] = carry_ref[...]
        dc_ref[...] += _dot_tn(st, dy)
        dd_ref[...] += jnp.sum(dy * u, axis=0, keepdims=True)
        ar, ai = a_ref[:, :SSM_LANES], a_ref[:, SSM_LANES:]

        def step(i, carry):
            gr, gi = carry
            r = lc - 1 - i
            nr = g_ref[pl.ds(r, 1), pl.ds(0, SSM_LANES)] + ar * gr + ai * gi
            ni = g_ref[pl.ds(r, 1), pl.ds(SSM_LANES, SSM_LANES)] - ai * gr + ar * gi
            g_ref[pl.ds(r, 1), pl.ds(0, SSM_LANES)] = nr
            g_ref[pl.ds(r, 1), pl.ds(SSM_LANES, SSM_LANES)] = ni
            return nr, ni

        gr, gi = lax.fori_loop(0, lc, step, (carry_ref[0:1, :SSM_LANES], carry_ref[0:1, SSM_LANES:]), unroll=8)
        carry_ref[0:1, :SSM_LANES] = gr
        carry_ref[0:1, SSM_LANES:] = gi
        sr, si = st[:, :SSM_LANES], st[:, SSM_LANES:]
        gnr, gni = g_ref[pl.ds(1, lc), pl.ds(0, SSM_LANES)], g_ref[pl.ds(1, lc), pl.ds(SSM_LANES, SSM_LANES)]
        da_ref[:, :SSM_LANES] += jnp.sum(gnr * sr + gni * si, axis=0, keepdims=True)
        da_ref[:, SSM_LANES:] += jnp.sum(gni * sr - gnr * si, axis=0, keepdims=True)
        gr_all, gi_all = g_ref[0:lc, :SSM_LANES], g_ref[0:lc, SSM_LANES:]
        proj = _dot_nn(u, b_ref[...])
        pr, pi = proj[:, :SSM_LANES], proj[:, SSM_LANES:]
        df_ref[:, :SSM_LANES] += jnp.sum(gr_all * pr + gi_all * pi, axis=0, keepdims=True)
        df_ref[:, SSM_LANES:] += jnp.sum(gi_all * pr - gr_all * pi, axis=0, keepdims=True)
        fr, fi = f_ref[:, :SSM_LANES], f_ref[:, SSM_LANES:]
        dproj = jnp.concatenate([fr * gr_all + fi * gi_all, fr * gi_all - fi * gr_all], axis=1).astype(MXU_DTYPE)
        du_ref[...] = (_dot_nt(dproj, b_ref[...]) + d_ref[...] * dy).astype(du_ref.dtype)
        db_ref[...] += _dot_tn(u, dproj)

    const = lambda shape: pl.BlockSpec(shape, lambda b, c: (0, 0))
    rows = lambda w, cb: pl.BlockSpec((lc, w), functools.partial(lambda b, c, cb: (b * nc + nc - 1 - c, cb), cb=cb))
    return pl.pallas_call(
        body, name="ssm_bwd", grid=(n_seq, nc),
        out_shape=[jax.ShapeDtypeStruct((t, SSM_WIDTH), MXU_DTYPE), jax.ShapeDtypeStruct((SSM_WIDTH, n2), F32),
                   jax.ShapeDtypeStruct((n2, SSM_WIDTH), F32), jax.ShapeDtypeStruct((1, n2), F32),
                   jax.ShapeDtypeStruct((1, n2), F32), jax.ShapeDtypeStruct((1, SSM_WIDTH), F32)],
        in_specs=[rows(SSM_WIDTH, 0), rows(ZS_W, ZS_BLK), rows(n2, 0), const((SSM_WIDTH, n2)), const((1, n2)),
                  const((1, n2)), const((n2, SSM_WIDTH)), const((1, SSM_WIDTH))],
        out_specs=[rows(SSM_WIDTH, 0), const((SSM_WIDTH, n2)), const((n2, SSM_WIDTH)), const((1, n2)), const((1, n2)),
                   const((1, SSM_WIDTH))],
        scratch_shapes=[pltpu.VMEM((lc + 8, n2), F32), pltpu.VMEM((8, n2), F32)],
        compiler_params=_params(("arbitrary", "arbitrary")))(dy, z, states, bmat, a_row, f_row, cmat, d_row)


def _conv_chunk(t):
    return _pick(t, (512, 256, 128))


def _glu(c):
    return c[:, :CONV_WIDTH] * jax.nn.sigmoid(c[:, CONV_WIDTH:])


def _conv_post_tile(v, g, b):
    mu = jnp.mean(v, axis=-1, keepdims=True)
    var = jnp.mean(jnp.square(v - mu), axis=-1, keepdims=True)
    return jax.nn.silu((v - mu) * lax.rsqrt(var + EPS) * g + b)


def _conv_specs(lc, nc):
    per = lc // CONV_HALO
    return [pl.BlockSpec((lc, ZC_W), lambda b, c: (b * nc + c, ZC_BLK)),
            pl.BlockSpec((CONV_HALO, ZC_W), lambda b, c: (jnp.maximum((b * nc + c) * per - 1, 0), ZC_BLK))]


def _conv_fill(c_ref, cp_ref, ue_ref, lc):
    ue_ref[0:CONV_HALO, :] = jnp.where(pl.program_id(1) > 0, _glu(cp_ref[...]), 0.0)
    ue_ref[CONV_HALO:CONV_HALO + lc, :] = _glu(c_ref[...])


def _conv_apply(ue_ref, w_ref, b_ref, lc):
    acc = jnp.zeros((lc, CONV_WIDTH), F32) + b_ref[...]
    for k in range(CONV_K):
        acc = acc + w_ref[k:k + 1, :] * ue_ref[pl.ds(k + CONV_HALO - CONV_K + 1, lc), :]
    return acc


def _conv_fwd(z, dw_w, dw_b, ln_g, ln_b, n_seq):
    t = z.shape[0]
    seq = t // n_seq
    lc = _conv_chunk(seq)
    nc = seq // lc

    def body(c_ref, cp_ref, w_ref, b_ref, g_ref, lb_ref, o_ref, ue_ref):
        _conv_fill(c_ref, cp_ref, ue_ref, lc)
        o_ref[...] = _conv_post_tile(_conv_apply(ue_ref, w_ref, b_ref, lc), g_ref[...], lb_ref[...]).astype(o_ref.dtype)

    const = lambda a: pl.BlockSpec(a.shape, lambda b, c: (0, 0))
    return pl.pallas_call(
        body, name="conv_fwd", grid=(n_seq, nc), out_shape=jax.ShapeDtypeStruct((t, CONV_WIDTH), MXU_DTYPE),
        in_specs=_conv_specs(lc, nc) + [const(dw_w), const(dw_b), const(ln_g), const(ln_b)],
        out_specs=pl.BlockSpec((lc, CONV_WIDTH), lambda b, c: (b * nc + c, 0)),
        scratch_shapes=[pltpu.VMEM((CONV_HALO + lc, CONV_WIDTH), F32)],
        compiler_params=_params(("parallel", "parallel")))(z, z, dw_w, dw_b, ln_g, ln_b)


def _conv_bwd_post(z, duc, dw_w, dw_b, ln_g, ln_b, n_seq):
    t = z.shape[0]
    seq = t // n_seq
    lc = _conv_chunk(seq)
    nc = seq // lc

    def body(c_ref, cp_ref, duc_ref, w_ref, b_ref, g_ref, lb_ref, dv_ref, dg_ref, dlb_ref, db_ref, ue_ref):
        @pl.when((pl.program_id(0) == 0) & (pl.program_id(1) == 0))
        def _():
            for r in (dg_ref, dlb_ref, db_ref):
                r[...] = jnp.zeros_like(r)

        _conv_fill(c_ref, cp_ref, ue_ref, lc)
        _, vjp = jax.vjp(_conv_post_tile, _conv_apply(ue_ref, w_ref, b_ref, lc), g_ref[...], lb_ref[...])
        dv, dg, dlb = vjp(duc_ref[...])
        dv_ref[...] = dv
        dg_ref[...] += dg
        dlb_ref[...] += dlb
        db_ref[...] += jnp.sum(dv, axis=0, keepdims=True)

    const = lambda a: pl.BlockSpec(a.shape, lambda b, c: (0, 0))
    vec = jax.ShapeDtypeStruct((1, CONV_WIDTH), F32)
    return pl.pallas_call(
        body, name="conv_bwd_post", grid=(n_seq, nc), out_shape=[jax.ShapeDtypeStruct((t, CONV_WIDTH), F32), vec, vec, vec],
        in_specs=_conv_specs(lc, nc) + [pl.BlockSpec((lc, CONV_WIDTH), lambda b, c: (b * nc + c, 0)),
                                       const(dw_w), const(dw_b), const(ln_g), const(ln_b)],
        out_specs=[pl.BlockSpec((lc, CONV_WIDTH), lambda b, c: (b * nc + c, 0))] + [const(dw_b)] * 3,
        scratch_shapes=[pltpu.VMEM((CONV_HALO + lc, CONV_WIDTH), F32)],
        compiler_params=_params(("arbitrary", "arbitrary")))(z, z, duc, dw_w, dw_b, ln_g, ln_b)


def _conv_bwd_taps(z, dv, dw_w, n_seq):
    t = z.shape[0]
    seq = t // n_seq
    lc = _conv_chunk(seq)
    nc = seq // lc
    per = lc // CONV_HALO
    n_halo = t // CONV_HALO

    def body(c_ref, cp_ref, dv_ref, dvn_ref, w_ref, dc_ref, dw_ref, ue_ref, dve_ref):
        @pl.when((pl.program_id(0) == 0) & (pl.program_id(1) == 0))
        def _():
            dw_ref[...] = jnp.zeros_like(dw_ref)

        _conv_fill(c_ref, cp_ref, ue_ref, lc)
        dv = dv_ref[...]
        dve_ref[0:lc, :] = dv
        dve_ref[lc:lc + CONV_HALO, :] = jnp.where(pl.program_id(1) < nc - 1, dvn_ref[...], 0.0)
        du = jnp.zeros((lc, CONV_WIDTH), F32)
        for k in range(CONV_K):
            du = du + w_ref[k:k + 1, :] * dve_ref[pl.ds(CONV_K - 1 - k, lc), :]
            dw_ref[k:k + 1, :] += jnp.sum(dv * ue_ref[pl.ds(k + CONV_HALO - CONV_K + 1, lc), :], axis=0, keepdims=True)
        c = c_ref[...]
        a, sg = c[:, :CONV_WIDTH], jax.nn.sigmoid(c[:, CONV_WIDTH:])
        dc_ref[:, :CONV_WIDTH] = (du * sg).astype(dc_ref.dtype)
        dc_ref[:, CONV_WIDTH:] = (du * a * sg * (1.0 - sg)).astype(dc_ref.dtype)

    return pl.pallas_call(
        body, name="conv_bwd_taps", grid=(n_seq, nc),
        out_shape=[jax.ShapeDtypeStruct((t, ZC_W), MXU_DTYPE), jax.ShapeDtypeStruct((CONV_HALO, CONV_WIDTH), F32)],
        in_specs=_conv_specs(lc, nc) + [
            pl.BlockSpec((lc, CONV_WIDTH), lambda b, c: (b * nc + c, 0)),
            pl.BlockSpec((CONV_HALO, CONV_WIDTH), lambda b, c: (jnp.minimum((b * nc + c + 1) * per, n_halo - 1), 0)),
            pl.BlockSpec(dw_w.shape, lambda b, c: (0, 0))],
        out_specs=[pl.BlockSpec((lc, ZC_W), lambda b, c: (b * nc + c, 0)),
                   pl.BlockSpec((CONV_HALO, CONV_WIDTH), lambda b, c: (0, 0))],
        scratch_shapes=[pltpu.VMEM((CONV_HALO + lc, CONV_WIDTH), F32), pltpu.VMEM((lc + CONV_HALO, CONV_WIDTH), F32)],
        compiler_params=_params(("arbitrary", "arbitrary")))(z, z, dv, dv, dw_w)


def _row(v):
    return v.reshape(1, -1)


def _ssm_mats(b_re, b_im, c_re, c_im):
    eye = jnp.eye(SSM_GROUPS, dtype=bool)
    bm = jnp.stack([b_re, b_im]).transpose(1, 3, 0, 2)[:, :, :, None, :]
    bmat = jnp.where(eye[:, None, None, :, None], bm, 0.0).reshape(SSM_WIDTH, 2 * SSM_LANES)
    cm = jnp.stack([c_re, -c_im]).transpose(0, 1, 3, 2)[:, :, :, None, :]
    cmat = jnp.where(eye[None, :, None, :, None], cm, 0.0).reshape(2 * SSM_LANES, SSM_WIDTH)
    return bmat.astype(MXU_DTYPE), cmat.astype(MXU_DTYPE)


def _ssm_mats_t(dbmat, dcmat):
    eye = jnp.eye(SSM_GROUPS, dtype=bool)
    db = dbmat.reshape(SSM_GROUPS, SSM_GROUP, 2, SSM_GROUPS, SSM_STATE)
    db = jnp.sum(jnp.where(eye[:, None, None, :, None], db, 0.0), axis=3).transpose(2, 0, 3, 1)
    dc = dcmat.reshape(2, SSM_GROUPS, SSM_STATE, SSM_GROUPS, SSM_GROUP)
    dc = jnp.sum(jnp.where(eye[None, :, None, :, None], dc, 0.0), axis=3).transpose(0, 1, 3, 2)
    return db[0], db[1], dc[0], -dc[1]


def _layer_fwd(x, p, w, sp, ctab, stab, n_seq):
    (h,), _ = _token_call("norm_in", lambda x, g: ([_norm_in_tile(x, g)], []), 512, [_whole(x)], [sp["mix_norm_g"]],
                          [(x.shape[1], MXU_DTYPE)], [])
    z = _matmul_nn("mm_in", h, w["w_in"], F32)
    ya, lse = _attention_fwd(z, ctab, stab, sp["attn_sinks"], n_seq)
    ys, states = _ssm_fwd(z, sp["bmat"], sp["a_row"], sp["f_row"], sp["cmat"], sp["ssm_d"], n_seq)
    uc = _conv_fwd(z, w["conv_dw_w"], sp["conv_dw_b"], sp["conv_norm_g"], sp["conv_norm_b"], n_seq)
    merge_consts = [w["w_attn_out"], w["w_ssm_glu"], sp["b_ssm_glu"], w["w_conv_out"], sp["b_gate"], w["w_mix_out"],
                    sp["ffn_norm_g"]]
    (x1, hf), _ = _token_call("merge", lambda *a: (list(_merge_tile(*_f32s(a))), []), 256,
                              [_whole(x), _whole(ya), _whole(ys), _whole(uc), (z, ZG_W, 0)], merge_consts,
                              [(x.shape[1], F32), (x.shape[1], MXU_DTYPE)], [])
    gate, up, act = _ffn_in_act(hf, w["w_ffn_in"])
    ffn = _matmul_nn("mm_ffn_out", act, w["w_ffn_out"], F32)

    def ple_fn(x1, ffn, p, w_pi, g_ple, w_pg):
        x2 = x1 + ffn
        return [x2, _ple_tile(x2, p, w_pi.astype(F32), g_ple, w_pg.astype(F32))], []

    (x2, x3), _ = _token_call("ple", ple_fn, 512, [_whole(x1), _whole(ffn), _whole(p)],
                              [w["w_ple_in"], sp["ple_norm_g"], w["w_ple_gate"]], [(x.shape[1], F32)] * 2, [])
    saved = dict(x=x, h=h, z=z, ya=ya, lse=lse, ys=ys, states=states, uc=uc, hf=hf, gate=gate, up=up, act=act, x2=x2, p=p)
    return x3, saved


def _layer_bwd(dx3, sv, w, sp, ctab, stab, n_seq):
    d = dx3.shape[1]
    gw, gs = {}, {}

    def ple_bwd(x2, p, dx3, w_pi, g_ple, w_pg):
        _, vjp = jax.vjp(lambda x2, w_pi, g_ple, w_pg: _ple_tile(x2, p, w_pi, g_ple, w_pg), x2, w_pi.astype(F32), g_ple,
                         w_pg.astype(F32))
        dx2, dw_pi, dg_ple, dw_pg = vjp(dx3)
        return [dx2, dx2], [dw_pi, dg_ple, dw_pg]

    (dx2, dffn), (gw["w_ple_in"], gs["ple_norm_g"], gw["w_ple_gate"]) = _token_call(
        "ple_bwd", ple_bwd, 256, [_whole(sv["x2"]), _whole(sv["p"]), _whole(dx3)],
        [w["w_ple_in"], sp["ple_norm_g"], w["w_ple_gate"]], [(d, F32), (d, MXU_DTYPE)],
        [w["w_ple_in"].shape, (1, d), w["w_ple_gate"].shape])

    dgate, dup = _ffn_mid_bwd(dffn, w["w_ffn_out"], sv["gate"], sv["up"])
    gw["w_ffn_out"] = _matmul_tn("mm_ffn_out_dw", sv["act"], dffn)
    dhf = _ffn_in_dx(dgate, dup, w["w_ffn_in"])
    gw["w_ffn_in"] = jnp.concatenate([_matmul_tn("mm_ffn_gate_dw", sv["hf"], dgate), _matmul_tn("mm_ffn_up_dw", sv["hf"], dup)],
                                     axis=1)

    def merge_bwd(x, ya, ys, uc, gin, dx1, dhf, *consts):
        consts = _f32s(consts)
        _, vjp = jax.vjp(_merge_tile, *_f32s((x, ya, ys, uc, gin)), *consts)
        g = vjp((dx1, dhf))
        return list(g[:5]), list(g[5:])

    merge_consts = [w["w_attn_out"], w["w_ssm_glu"], sp["b_ssm_glu"], w["w_conv_out"], sp["b_gate"], w["w_mix_out"],
                    sp["ffn_norm_g"]]
    (dx_res, dya, dys, duc, dgin), macc = _token_call(
        "merge_bwd", merge_bwd, 256,
        [_whole(sv["x"]), _whole(sv["ya"]), _whole(sv["ys"]), _whole(sv["uc"]), (sv["z"], ZG_W, 0), _whole(dx2), _whole(dhf)],
        merge_consts, [(d, F32), (Q_WIDTH, MXU_DTYPE), (SSM_WIDTH, F32), (CONV_WIDTH, F32), (ZG_W, MXU_DTYPE)],
        [c.shape for c in merge_consts])
    gw["w_attn_out"], gw["w_ssm_glu"], gs["b_ssm_glu"], gw["w_conv_out"], gs["b_gate"], gw["w_mix_out"], gs["ffn_norm_g"] = macc

    dv, gs["conv_norm_g"], gs["conv_norm_b"], gs["conv_dw_b"] = _conv_bwd_post(
        sv["z"], duc, w["conv_dw_w"], sp["conv_dw_b"], sp["conv_norm_g"], sp["conv_norm_b"], n_seq)
    dzc, dw_taps = _conv_bwd_taps(sv["z"], dv, w["conv_dw_w"], n_seq)
    gw["conv_dw_w"] = dw_taps[:CONV_K]

    dzs, gs["bmat"], gs["cmat"], gs["a_row"], gs["f_row"], gs["ssm_d"] = _ssm_bwd(
        sv["z"], sv["states"], dys, sp["bmat"], sp["a_row"], sp["f_row"], sp["cmat"], sp["ssm_d"], n_seq)

    dzq, dkv, dkvp, gs["attn_sinks"] = _attention_bwd(sv["z"], ctab, stab, sp["attn_sinks"], sv["ya"], sv["lse"], dya, n_seq)
    dz = _assemble_dz(dgin, dzq, dkv, dkvp, dzs, dzc, n_seq)
    dh = _matmul_nt("mm_in_dx", dz, w["w_in"], F32)
    gw["w_in"] = _matmul_tn("mm_in_dw", sv["h"], dz)

    def norm_bwd(x, dh, dx_res, g):
        _, vjp = jax.vjp(_norm_in_tile, x, g)
        dx, dg = vjp(dh)
        return [dx + dx_res], [dg]

    (dx,), (gs["mix_norm_g"],) = _token_call("norm_in_bwd", norm_bwd, 512, [_whole(sv["x"]), _whole(dh), _whole(dx_res)],
                                             [sp["mix_norm_g"]], [(d, F32)], [(1, d)])
    return dx, gw, gs


def _loss_and_grad(x, target, g):
    def fn(x, tgt, g):
        def f(x, g):
            err = _rms(x, g) - tgt
            return 0.5 * jnp.mean(err * err, axis=-1, keepdims=True)

        per_token, vjp = jax.vjp(f, x, g)
        dx, dg = vjp(jnp.ones_like(per_token))
        return [dx], [jnp.sum(per_token, axis=0, keepdims=True), dg]

    (dx,), (loss, dg) = _token_call("loss", fn, 512, [_whole(x), _whole(target)], [g], [(x.shape[1], F32)],
                                    [(8, LANES), (1, x.shape[1])])
    return loss[0, 0], dx, dg


def _mesh_place():
    return lax.axis_index("x"), lax.axis_index("y"), lax.axis_index("c")


def _flip(v, bit):
    return 1 - v if bit else v


_ANY = pl.BlockSpec(memory_space=pl.ANY)
_MESH = pl.DeviceIdType.MESH


def _all_gather(name, x):
    def body(x_ref, out_ref, send_sems, recv_sems, local_sem):
        mx, my, mc = _mesh_place()
        me, sibling = (mx, my, mc), (mx, my, 1 - mc)
        chips = [(1 - mx, my), (mx, 1 - my), (1 - mx, 1 - my)]

        def slot(px, py, pc):
            return out_ref.at[4 * px + 2 * py + pc]

        def copy(k, block, to, src=None):
            return pltpu.make_async_remote_copy(
                src_ref=slot(*block) if src is None else src, dst_ref=slot(*block), send_sem=send_sems.at[k],
                recv_sem=recv_sems.at[k], device_id=to, device_id_type=_MESH)

        mine = pltpu.make_async_copy(x_ref, slot(*me), local_sem)
        mine.start()
        first = [copy(0, me, sibling, src=x_ref)]
        first += [copy(1 + j, me, (*chip, mc), src=x_ref) for j, chip in enumerate(chips)]
        for cp in first:
            cp.start()
        passed = [copy(4 + j, (*chip, mc), sibling) for j, chip in enumerate(chips)]
        for j, chip in enumerate(chips):
            copy(1 + j, (*chip, mc), me).wait_recv()
            passed[j].start()
        copy(0, sibling, me).wait_recv()
        for j, chip in enumerate(chips):
            copy(4 + j, (*chip, 1 - mc), me).wait_recv()
        for cp in first + passed:
            cp.wait_send()
        mine.wait()

    return pl.pallas_call(
        body, name=name, out_shape=jax.ShapeDtypeStruct((N_DEV,) + x.shape, x.dtype), in_specs=[_ANY], out_specs=_ANY,
        scratch_shapes=[pltpu.SemaphoreType.DMA((7,)), pltpu.SemaphoreType.DMA((7,)), pltpu.SemaphoreType.DMA])(x)


def _exchange(name, parts):
    def body(p_ref, land_ref, send_sems, recv_sems, local_sem):
        mx, my, mc = _mesh_place()
        me = 4 * mx + 2 * my + mc
        mine = pltpu.make_async_copy(p_ref.at[me], land_ref.at[me], local_sem)
        mine.start()
        copies = []
        for rel in range(1, N_DEV):
            px, py, pc = _flip(mx, rel & 4), _flip(my, rel & 2), _flip(mc, rel & 1)
            copies.append(pltpu.make_async_remote_copy(
                src_ref=p_ref.at[4 * px + 2 * py + pc], dst_ref=land_ref.at[me], send_sem=send_sems.at[rel - 1],
                recv_sem=recv_sems.at[rel - 1], device_id=(px, py, pc), device_id_type=_MESH))
        for cp in copies:
            cp.start()
        for cp in copies:
            cp.wait()
        mine.wait()

    return pl.pallas_call(
        body, name=name, out_shape=jax.ShapeDtypeStruct(parts.shape, parts.dtype), in_specs=[_ANY], out_specs=_ANY,
        scratch_shapes=[pltpu.SemaphoreType.DMA((7,)), pltpu.SemaphoreType.DMA((7,)), pltpu.SemaphoreType.DMA])(parts)


def _adamw(name, parts, w, m, v):
    r = w.shape[0]
    tile = _pick(r, (1024, 512, 256, 128, 8))

    def body(p_ref, w_ref, m_ref, v_ref, g_ref, d_ref, m2_ref, v2_ref):
        g = p_ref[0].astype(F32)
        for j in range(1, N_DEV):
            g = g + p_ref[j].astype(F32)
        m2 = ADAM_B1 * m_ref[...] + (1.0 - ADAM_B1) * g
        v2 = ADAM_B2 * v_ref[...] + (1.0 - ADAM_B2) * jnp.square(g)
        m_hat = m2 / (1.0 - ADAM_B1 ** ADAM_STEP)
        v_hat = v2 / (1.0 - ADAM_B2 ** ADAM_STEP)
        g_ref[...] = g
        d_ref[...] = -ADAM_LR * (m_hat / (jnp.sqrt(v_hat) + ADAM_EPS) + ADAM_WD * w_ref[...])
        m2_ref[...] = m2
        v2_ref[...] = v2

    flat = pl.BlockSpec((tile, LANES), lambda i: (i, 0))
    return pl.pallas_call(
        body, name=name, grid=(r // tile,), out_shape=[jax.ShapeDtypeStruct((r, LANES), F32)] * 4,
        in_specs=[pl.BlockSpec((N_DEV, tile, LANES), lambda i: (0, i, 0)), flat, flat, flat], out_specs=[flat] * 4,
        compiler_params=_params(("parallel",)))(parts, w, m, v)


CONV_W_PIECES = 3


def _pad_to(n, align):
    return -(-n // align) * align


def _segments(depth):
    segs = []
    for layer in range(depth):
        for name, rows, cols, axis in SHARDED:
            n = rows * cols // N_DEV
            for piece in range(CONV_W_PIECES if name == "conv_dw_w" else 1):
                segs.append((layer, name, rows, cols, axis, piece, n, _pad_to(n, PACK_ALIGN)))
    return segs


def _flat_rows(depth):
    return _pad_to(sum(s[-1] for s in _segments(depth)), FLAT_ROW_ALIGN * LANES) // LANES


def _pack(depth, piece_fn, lead=()):
    out, dtype = [], None
    for layer, name, _, _, _, piece, n, padded in _segments(depth):
        a = piece_fn(layer, name, piece)
        if a is None:
            out.append((None, padded))
            continue
        dtype = a.dtype
        out.append((jnp.pad(a, [(0, 0)] * len(lead) + [(0, padded - n)]), padded))
    total = _flat_rows(depth) * LANES
    used = sum(p for _, p in out)
    cols = [jnp.zeros(lead + (p,), dtype) if a is None else a for a, p in out]
    if total > used:
        cols.append(jnp.zeros(lead + (total - used,), dtype))
    return jnp.concatenate(cols, axis=-1).reshape(lead + (total // LANES, LANES))


def _unpack(depth, flat, lead=()):
    flat = flat.reshape(lead + (-1,))
    res, off = {}, 0
    for layer, name, _, _, _, piece, n, padded in _segments(depth):
        res[(layer, name, piece)] = flat[..., off:off + n]
        off += padded
    return res


def _to_shards(name, full):
    _, rows, cols, axis = next(s for s in SHARDED if s[0] == name)
    if axis == 1:
        return full.reshape(rows, N_DEV, cols // N_DEV).transpose(1, 0, 2).reshape(N_DEV, -1)
    return full.reshape(N_DEV, -1)


def _from_shards(name, shards):
    _, rows, cols, axis = next(s for s in SHARDED if s[0] == name)
    if axis == 1:
        return shards.reshape(N_DEV, rows, cols // N_DEV).transpose(1, 0, 2).reshape(rows, cols)
    return shards.reshape(rows, cols)


def _split3(a):
    hi = a.astype(BF16)
    r1 = a - hi.astype(F32)
    mid = r1.astype(BF16)
    return hi, mid, (r1 - mid.astype(F32)).astype(BF16)


def _pack_small(arrs, lead=()):
    flat = jnp.concatenate([a.reshape(lead + (-1,)) for a in arrs], axis=-1)
    total = _pad_to(flat.shape[-1], 512 * LANES)
    flat = jnp.pad(flat, [(0, 0)] * len(lead) + [(0, total - flat.shape[-1])])
    return flat.reshape(lead + (total // LANES, LANES))


def _unpack_small(flat, shapes):
    flat = flat.reshape(-1)
    res, off = [], 0
    for s in shapes:
        n = int(np.prod(s))
        res.append(flat[off:off + n].reshape(s))
        off += n
    return res


def _small_rows(a, depth):
    n16 = depth * SSM_GROUPS
    a_re, a_im, f_re, f_im = _ssm_coeffs(a["ssm_lambda_re"].reshape(n16, SSM_STATE), a["ssm_lambda_im"].reshape(n16, SSM_STATE),
                                         a["ssm_log_dt"].reshape(n16, 1))
    rows = []
    for l in range(depth):
        sp = {k: _row(a[k][l]) for k in ("mix_norm_g", "b_gate", "attn_sinks", "ssm_d", "b_ssm_glu", "conv_dw_b",
                                         "conv_norm_g", "conv_norm_b", "ffn_norm_g", "ple_norm_g")}
        g = slice(l * SSM_GROUPS, (l + 1) * SSM_GROUPS)
        sp["a_row"] = jnp.concatenate([a_re[g].reshape(1, -1), a_im[g].reshape(1, -1)], axis=1)
        sp["f_row"] = jnp.concatenate([f_re[g].reshape(1, -1), f_im[g].reshape(1, -1)], axis=1)
        sp["bmat"], sp["cmat"] = _ssm_mats(a["ssm_b_re"][l], a["ssm_b_im"][l], a["ssm_c_re"][l], a["ssm_c_im"][l])
        rows.append(sp)
    return rows


def _local_step(a, weights, depth):
    n_seq, seq, d = a["x"].shape
    t = n_seq * seq
    inv = ROPE_THETA ** (-jnp.arange(0, ROPE_DIM, 2, dtype=F32) / ROPE_DIM)
    lane = np.arange(LANES) % HEAD_DIM
    inv_lane = jnp.where(lane < ROPE_DIM, jnp.tile(inv, LANES // (ROPE_DIM // 2)), 0.0).reshape(1, LANES)
    ctab, stab = _rope_tables(a["positions"].reshape(t), inv_lane)
    small = _small_rows(a, depth)

    x = a["x"].reshape(t, d)
    saved = []
    for l in range(depth):
        x, sv = _layer_fwd(x, a["p"][l].reshape(t, -1), weights[l], small[l], ctab, stab, n_seq)
        saved.append(sv)
    loss, dx, d_final = _loss_and_grad(x, a["loss_target"].reshape(t, d), _row(a["final_norm_g"]))
    gws, gss = [None] * depth, [None] * depth
    for l in reversed(range(depth)):
        dx, gws[l], gss[l] = _layer_bwd(dx, saved[l], weights[l], small[l], ctab, stab, n_seq)

    n16 = depth * SSM_GROUPS
    halves = lambda k, h: jnp.concatenate([gss[l][k][:, h * SSM_LANES:(h + 1) * SSM_LANES].reshape(SSM_GROUPS, SSM_STATE)
                                           for l in range(depth)], axis=0)
    dlr, dli, ddt = _ssm_coeffs_bwd(a["ssm_lambda_re"].reshape(n16, SSM_STATE), a["ssm_lambda_im"].reshape(n16, SSM_STATE),
                                    a["ssm_log_dt"].reshape(n16, 1),
                                    (halves("a_row", 0), halves("a_row", 1), halves("f_row", 0), halves("f_row", 1)))
    bc = [_ssm_mats_t(gss[l]["bmat"], gss[l]["cmat"]) for l in range(depth)]
    gsmall = {k: jnp.stack([gss[l][k].reshape(a[k].shape[1:]) for l in range(depth)])
              for k in ("mix_norm_g", "b_gate", "attn_sinks", "ssm_d", "b_ssm_glu", "conv_dw_b", "conv_norm_g", "conv_norm_b",
                        "ffn_norm_g", "ple_norm_g")}
    gsmall["ssm_lambda_re"] = dlr.reshape(a["ssm_lambda_re"].shape)
    gsmall["ssm_lambda_im"] = dli.reshape(a["ssm_lambda_im"].shape)
    gsmall["ssm_log_dt"] = ddt.reshape(a["ssm_log_dt"].shape)
    for i, k in enumerate(("ssm_b_re", "ssm_b_im", "ssm_c_re", "ssm_c_im")):
        gsmall[k] = jnp.stack([bc[l][i] for l in range(depth)])
    gsmall["final_norm_g"] = d_final.reshape(a["final_norm_g"].shape)
    return loss, dx.reshape(n_seq, seq, d), gws, gsmall


def _permute_in(w):
    return jnp.concatenate([w[:, Z_SPLIT:], w[:, :Z_SPLIT]], axis=1)


def _unpermute_in(w):
    return jnp.concatenate([w[:, Z_WIDTH - Z_SPLIT:], w[:, :Z_WIDTH - Z_SPLIT]], axis=1)


def kernel(x, p, positions, mix_norm_g, w_in, b_gate, attn_sinks, w_attn_out, ssm_lambda_re, ssm_lambda_im, ssm_log_dt, ssm_b_re, ssm_b_im, ssm_c_re, ssm_c_im, ssm_d, w_ssm_glu, b_ssm_glu, conv_dw_w, conv_dw_b, conv_norm_g, conv_norm_b, w_conv_out, w_mix_out, ffn_norm_g, w_ffn_in, w_ffn_out, w_ple_in, ple_norm_g, w_ple_gate, final_norm_g, loss_target, m_mix_norm_g, m_w_in, m_b_gate, m_attn_sinks, m_w_attn_out, m_ssm_lambda_re, m_ssm_lambda_im, m_ssm_log_dt, m_ssm_b_re, m_ssm_b_im, m_ssm_c_re, m_ssm_c_im, m_ssm_d, m_w_ssm_glu, m_b_ssm_glu, m_conv_dw_w, m_conv_dw_b, m_conv_norm_g, m_conv_norm_b, m_w_conv_out, m_w_mix_out, m_ffn_norm_g, m_w_ffn_in, m_w_ffn_out, m_w_ple_in, m_ple_norm_g, m_w_ple_gate, m_final_norm_g, v_mix_norm_g, v_w_in, v_b_gate, v_attn_sinks, v_w_attn_out, v_ssm_lambda_re, v_ssm_lambda_im, v_ssm_log_dt, v_ssm_b_re, v_ssm_b_im, v_ssm_c_re, v_ssm_c_im, v_ssm_d, v_w_ssm_glu, v_b_ssm_glu, v_conv_dw_w, v_conv_dw_b, v_conv_norm_g, v_conv_norm_b, v_w_conv_out, v_w_mix_out, v_ffn_norm_g, v_w_ffn_in, v_w_ffn_out, v_w_ple_in, v_ple_norm_g, v_w_ple_gate, v_final_norm_g):
    a = dict(locals())
    depth = w_in.shape[0]
    sharded_names = [s[0] for s in SHARDED]

    def weight_piece(layer, name, piece):
        shard = a[name][layer].reshape(-1)
        return _split3(shard)[piece] if name == "conv_dw_w" else shard.astype(BF16)

    gathered = _unpack(depth, _all_gather("gather_weights", _pack(depth, weight_piece)), lead=(N_DEV,))
    weights = []
    for l in range(depth):
        w = {n: _from_shards(n, gathered[(l, n, 0)]) for n in sharded_names if n != "conv_dw_w"}
        w["conv_dw_w"] = sum(_from_shards("conv_dw_w", gathered[(l, "conv_dw_w", i)]).astype(F32) for i in range(CONV_W_PIECES))
        w["w_in"] = _permute_in(w["w_in"])
        weights.append(w)

    loss, grad_x, gws, gsmall = _local_step(a, weights, depth)
    loss = lax.psum(loss, ("x", "y", "c"))

    def grad_piece(layer, name, piece):
        if piece:
            return None
        g = _unpermute_in(gws[layer][name]) if name == "w_in" else gws[layer][name]
        return _to_shards(name, g).astype(BF16)

    landed = _exchange("exchange_grads", _pack(depth, grad_piece, lead=(N_DEV,)))
    state = [_pack(depth, lambda layer, name, piece, pre=pre: None if piece else a[pre + name][layer].reshape(-1))
             for pre in ("", "m_", "v_")]
    big = [_unpack(depth, o) for o in _adamw("adamw_sharded", landed, *state)]

    shapes = [a[k].shape for k in REPLICATED]
    parts = _all_gather("gather_small_grads", _pack_small([gsmall[k] for k in REPLICATED]))
    small_state = [_pack_small([a[pre + k] for k in REPLICATED]) for pre in ("", "m_", "v_")]
    small = [dict(zip(REPLICATED, _unpack_small(o, shapes), strict=True)) for o in _adamw("adamw_replicated", parts, *small_state)]

    def result(kind, name):
        if name in REPLICATED:
            return small[kind][name]
        return jnp.stack([big[kind][(l, name, 0)] for l in range(depth)]).reshape(a[name].shape)

    return (loss, grad_x, *[result(kind, n) for kind in range(4) for n in WEIGHT_ORDER])
```

```python
import functools
import math

import numpy as np
import jax
import jax.numpy as jnp
from jax import lax
from jax.experimental import pallas as pl
from jax.experimental.pallas import tpu as pltpu

F32 = jnp.float32
BF16 = jnp.bfloat16
MXU_DTYPE = jnp.bfloat16
VMEM_LIMIT_BYTES = 56 * 2 ** 20
N_DEV = 8
LANES = 128

HEAD_DIM = 64
N_Q_HEADS = 8
N_KV_HEADS = 2
GQA_GROUP = 4
BLOCK = 128
ROPE_THETA = 500000.0
ROPE_DIM = 16
Q_WIDTH = 512
KV_WIDTH = 128
SSM_WIDTH = 256
SSM_GROUP = 16
SSM_GROUPS = 16
SSM_STATE = 64
SSM_LANES = SSM_GROUPS * SSM_STATE
CONV_WIDTH = 256
CONV_K = 31
CONV_HALO = 32
EPS = 1e-6
NEG_INF = -1e30
ADAM_LR, ADAM_B1, ADAM_B2, ADAM_EPS, ADAM_WD, ADAM_STEP = 0.001, 0.9, 0.999, 1e-08, 0.01, 10

ZG_W, ZQ_W, ZKV_W, ZS_W, ZC_W = 3072, 512, 256, 256, 512
ZQ_BLK, ZKV_BLK, ZS_BLK, ZC_BLK = 3072 // 512, 3584 // 256, 3840 // 256, 4096 // 512
Z_WIDTH = 4608
Z_SPLIT = 1536

SHARDED = (("w_in", 1024, 4608, 1), ("w_attn_out", 512, 1024, 1), ("w_ssm_glu", 256, 2048, 1),
           ("conv_dw_w", 31, 256, 1), ("w_conv_out", 256, 1024, 1), ("w_mix_out", 1024, 1024, 0),
           ("w_ffn_in", 1024, 5632, 1), ("w_ffn_out", 2816, 1024, 0), ("w_ple_in", 256, 1024, 1),
           ("w_ple_gate", 1024, 1024, 0))
COL_SHARDED = ("w_in", "w_ffn_in", "w_attn_out", "w_ssm_glu", "w_conv_out", "w_ple_in")
ROW_SHARDED = ("w_ffn_out", "w_mix_out", "w_ple_gate")
SLAB_TILE = 256
FLAT_ROW_ALIGN = 1024
REPLICATED = ("mix_norm_g", "b_gate", "attn_sinks", "ssm_lambda_re", "ssm_lambda_im", "ssm_log_dt", "ssm_b_re",
              "ssm_b_im", "ssm_c_re", "ssm_c_im", "ssm_d", "b_ssm_glu", "conv_dw_b", "conv_norm_g", "conv_norm_b",
              "ffn_norm_g", "ple_norm_g", "final_norm_g")
WEIGHT_ORDER = ("mix_norm_g", "w_in", "b_gate", "attn_sinks", "w_attn_out", "ssm_lambda_re", "ssm_lambda_im",
                "ssm_log_dt", "ssm_b_re", "ssm_b_im", "ssm_c_re", "ssm_c_im", "ssm_d", "w_ssm_glu", "b_ssm_glu",
                "conv_dw_w", "conv_dw_b", "conv_norm_g", "conv_norm_b", "w_conv_out", "w_mix_out", "ffn_norm_g",
                "w_ffn_in", "w_ffn_out", "w_ple_in", "ple_norm_g", "w_ple_gate", "final_norm_g")


def _params(sem=None):
    return pltpu.CompilerParams(dimension_semantics=sem, vmem_limit_bytes=VMEM_LIMIT_BYTES)


def _pick(n, cands):
    for c in cands:
        if n % c == 0:
            return c
    return n


def _dot(a, b, dims):
    return lax.dot_general(a.astype(MXU_DTYPE), b.astype(MXU_DTYPE), (dims, ((), ())), preferred_element_type=F32)


def _dot_nn(a, b):
    return _dot(a, b, ((1,), (0,)))


def _dot_nt(a, b):
    return _dot(a, b, ((1,), (1,)))


def _dot_tn(a, b):
    return _dot(a, b, ((0,), (0,)))


@jax.custom_vjp
def _mm(x, w):
    return _dot_nn(x, w)


def _mm_f(x, w):
    return _dot_nn(x, w), (x, w)


def _mm_b(res, dy):
    x, w = res
    return _dot_nt(dy, w).astype(x.dtype), _dot_tn(x, dy).astype(w.dtype)


_mm.defvjp(_mm_f, _mm_b)


def _rms(x, g):
    return x * lax.rsqrt(jnp.mean(x * x, axis=-1, keepdims=True) + EPS) * g


ROW_TILES = (1024, 512, 256, 128)
COL_TILES = (1536, 1408, 1024, 512, 256, 128)


def _matmul_nn(name, a, b, out_dtype):
    t, k = a.shape
    n = b.shape[1]
    tm, tn = _pick(t, ROW_TILES), _pick(n, COL_TILES)

    def body(a_ref, b_ref, o_ref):
        o_ref[...] = _dot_nn(a_ref[...], b_ref[...]).astype(o_ref.dtype)

    return pl.pallas_call(
        body, name=name, grid=(t // tm, n // tn), out_shape=jax.ShapeDtypeStruct((t, n), out_dtype),
        in_specs=[pl.BlockSpec((tm, k), lambda i, j: (i, 0)), pl.BlockSpec((k, tn), lambda i, j: (0, j))],
        out_specs=pl.BlockSpec((tm, tn), lambda i, j: (i, j)),
        compiler_params=_params(("parallel", "parallel")))(a, b)


def _matmul_nt(name, a, b, out_dtype):
    t, n = a.shape
    k = b.shape[0]
    tm, tk = _pick(t, ROW_TILES[1:]), _pick(k, COL_TILES)

    def body(a_ref, b_ref, o_ref):
        o_ref[...] = _dot_nt(a_ref[...], b_ref[...]).astype(o_ref.dtype)

    return pl.pallas_call(
        body, name=name, grid=(t // tm, k // tk), out_shape=jax.ShapeDtypeStruct((t, k), out_dtype),
        in_specs=[pl.BlockSpec((tm, n), lambda i, j: (i, 0)), pl.BlockSpec((tk, n), lambda i, j: (j, 0))],
        out_specs=pl.BlockSpec((tm, tk), lambda i, j: (i, j)),
        compiler_params=_params(("parallel", "parallel")))(a, b)


def _matmul_tn(name, a, b):
    t, m = a.shape
    n = b.shape[1]
    tm, tn, tt = _pick(m, COL_TILES[1:]), _pick(n, COL_TILES), _pick(t, ROW_TILES)

    def body(a_ref, b_ref, o_ref):
        @pl.when(pl.program_id(2) == 0)
        def _():
            o_ref[...] = jnp.zeros_like(o_ref)

        o_ref[...] += _dot_tn(a_ref[...], b_ref[...])

    return pl.pallas_call(
        body, name=name, grid=(m // tm, n // tn, t // tt), out_shape=jax.ShapeDtypeStruct((m, n), F32),
        in_specs=[pl.BlockSpec((tt, tm), lambda i, j, s: (s, i)), pl.BlockSpec((tt, tn), lambda i, j, s: (s, j))],
        out_specs=pl.BlockSpec((tm, tn), lambda i, j, s: (i, j)),
        compiler_params=_params(("parallel", "parallel", "arbitrary")))(a, b)


def _ffn_in_act(hf, w_fi):
    t, k = hf.shape
    f = w_fi.shape[1] // 2
    tm, tf = _pick(t, ROW_TILES[1:]), _pick(f, COL_TILES)
    nf = f // tf

    def body(a_ref, wg_ref, wu_ref, g_ref, u_ref, act_ref):
        a = a_ref[...]
        g, u = _dot_nn(a, wg_ref[...]), _dot_nn(a, wu_ref[...])
        g_ref[...] = g.astype(g_ref.dtype)
        u_ref[...] = u.astype(u_ref.dtype)
        act_ref[...] = (jax.nn.silu(g) * u).astype(act_ref.dtype)

    out = pl.BlockSpec((tm, tf), lambda i, j: (i, j))
    return pl.pallas_call(
        body, name="ffn_in_act", grid=(t // tm, nf), out_shape=[jax.ShapeDtypeStruct((t, f), MXU_DTYPE)] * 3,
        in_specs=[pl.BlockSpec((tm, k), lambda i, j: (i, 0)), pl.BlockSpec((k, tf), lambda i, j: (0, j)),
                  pl.BlockSpec((k, tf), lambda i, j: (0, j + nf))],
        out_specs=[out, out, out], compiler_params=_params(("parallel", "parallel")))(hf, w_fi, w_fi)


def _ffn_mid_bwd(dffn, w_fo, gate, up):
    t, d = dffn.shape
    f = w_fo.shape[0]
    tm, tf = _pick(t, ROW_TILES[1:]), _pick(f, COL_TILES)

    def body(a_ref, w_ref, g_ref, u_ref, dg_ref, du_ref):
        dact = _dot_nt(a_ref[...], w_ref[...])
        g, u = g_ref[...].astype(F32), u_ref[...].astype(F32)
        sg = jax.nn.sigmoid(g)
        dg_ref[...] = (dact * u * sg * (1.0 + g * (1.0 - sg))).astype(dg_ref.dtype)
        du_ref[...] = (dact * g * sg).astype(du_ref.dtype)

    blk = pl.BlockSpec((tm, tf), lambda i, j: (i, j))
    return pl.pallas_call(
        body, name="ffn_mid_bwd", grid=(t // tm, f // tf), out_shape=[jax.ShapeDtypeStruct((t, f), MXU_DTYPE)] * 2,
        in_specs=[pl.BlockSpec((tm, d), lambda i, j: (i, 0)), pl.BlockSpec((tf, d), lambda i, j: (j, 0)), blk, blk],
        out_specs=[blk, blk], compiler_params=_params(("parallel", "parallel")))(dffn, w_fo, gate, up)


def _ffn_in_dx(dgate, dup, w_fi):
    t, f = dgate.shape
    d = w_fi.shape[0]
    tm = _pick(t, ROW_TILES[1:])

    def body(g_ref, u_ref, w_ref, o_ref):
        o_ref[...] = _dot_nt(g_ref[...], w_ref[:, :f]) + _dot_nt(u_ref[...], w_ref[:, f:])

    blk = pl.BlockSpec((tm, f), lambda i: (i, 0))
    return pl.pallas_call(
        body, name="ffn_in_dx", grid=(t // tm,), out_shape=jax.ShapeDtypeStruct((t, d), F32),
        in_specs=[blk, blk, pl.BlockSpec(w_fi.shape, lambda i: (0, 0))], out_specs=pl.BlockSpec((tm, d), lambda i: (i, 0)),
        compiler_params=_params(("parallel",)))(dgate, dup, w_fi)


def _token_call(name, fn, tile, tok_ins, consts, tok_outs, acc_outs):
    n_rows = tok_ins[0][0].shape[0]
    tile = min(tile, n_rows)
    n_ti, n_c, n_to = len(tok_ins), len(consts), len(tok_outs)

    def body(*refs):
        ins = [r[...] for r in refs[:n_ti + n_c]]
        outs, accs = fn(*ins)
        for r, v in zip(refs[n_ti + n_c:n_ti + n_c + n_to], outs, strict=True):
            r[...] = v.astype(r.dtype)
        first = pl.program_id(0) == 0
        for r, v in zip(refs[n_ti + n_c + n_to:], accs, strict=True):
            @pl.when(first)
            def _(r=r):
                r[...] = jnp.zeros_like(r)

            r[...] += jnp.broadcast_to(v, r.shape).astype(F32)

    in_specs = [pl.BlockSpec((tile, w), functools.partial(lambda i, c: (i, c), c=cb)) for _, w, cb in tok_ins]
    in_specs += [pl.BlockSpec(c.shape, lambda i: (0, 0)) for c in consts]
    out_shape = [jax.ShapeDtypeStruct((n_rows, w), dt) for w, dt in tok_outs]
    out_shape += [jax.ShapeDtypeStruct(s, F32) for s in acc_outs]
    out_specs = [pl.BlockSpec((tile, w), lambda i: (i, 0)) for w, _ in tok_outs]
    out_specs += [pl.BlockSpec(s, lambda i: (0, 0)) for s in acc_outs]
    res = pl.pallas_call(
        body, name=name, grid=(n_rows // tile,), out_shape=out_shape, in_specs=in_specs, out_specs=out_specs,
        compiler_params=_params(("arbitrary",)))(*[a for a, _, _ in tok_ins], *consts)
    return res[:n_to], res[n_to:]


def _whole(a):
    return (a, a.shape[1], 0)


def _norm_in_tile(x, g):
    return _rms(x, g)


def _merge_tile(x, ya, ys, uc, gin, w_ao, w_sg, b_sg, w_co, b_gate, w_mo, g_ffn):
    d = x.shape[1]
    y_attn = _mm(ya, w_ao)
    pre = _mm(jax.nn.gelu(ys), w_sg) + b_sg
    y_ssm = pre[:, :d] * jax.nn.sigmoid(pre[:, d:])
    y_conv = _mm(uc, w_co)
    gates = jax.nn.sigmoid(gin + b_gate)
    merged = gates[:, :d] * y_attn + gates[:, d:2 * d] * y_ssm + gates[:, 2 * d:] * y_conv
    x1 = x + _mm(merged, w_mo)
    return x1, _rms(x1, g_ffn)


def _ple_tile(x2, p, w_pi, g_ple, w_pg):
    return x2 + jax.nn.sigmoid(_mm(_rms(x2, g_ple), w_pg)) * _mm(p, w_pi)


def _f32s(vals):
    return [v.astype(F32) for v in vals]


def _rope_tables(positions, inv_lane):
    def fn(pos, inv):
        ang = pos.astype(F32) * inv
        j = lax.broadcasted_iota(jnp.int32, ang.shape, 1) % HEAD_DIM
        c = jnp.where(j < ROPE_DIM, jnp.cos(ang), 1.0)
        s = jnp.sin(ang)
        s = jnp.where(j < ROPE_DIM // 2, -s, jnp.where(j < ROPE_DIM, s, 0.0))
        return [c, s], []

    (c, s), _ = _token_call("rope_tables", fn, 1024, [_whole(positions.reshape(-1, 1))], [inv_lane],
                            [(LANES, F32), (LANES, F32)], [])
    return c, s


def _swap_halves(t):
    n = t.shape[1]
    j = lax.broadcasted_iota(jnp.int32, t.shape, 1) % HEAD_DIM
    lower = pltpu.roll(t, n - ROPE_DIM // 2, 1)
    upper = jnp.where(j < ROPE_DIM, pltpu.roll(t, ROPE_DIM // 2, 1), 0.0)
    return jnp.where(j < ROPE_DIM // 2, lower, upper)


def _rope(t, c, s):
    return t * c + _swap_halves(t) * s


def _rope_t(dt, c, s):
    return dt * c + _swap_halves(dt * s)


def _tile4(a):
    return jnp.concatenate([a] * (Q_WIDTH // LANES), axis=1)


def _attn_mask(n):
    qi = lax.broadcasted_iota(jnp.int32, (GQA_GROUP * BLOCK, 2 * BLOCK), 0) % BLOCK
    kj = lax.broadcasted_iota(jnp.int32, (GQA_GROUP * BLOCK, 2 * BLOCK), 1)
    dist = qi + BLOCK - kj
    return (dist >= 0) & (dist < BLOCK) & ((n > 0) | (kj >= BLOCK))


def _attn_specs(nb):
    row = lambda b, n: b * nb + n
    prev = lambda b, n: b * nb + jnp.maximum(n - 1, 0)
    return [pl.BlockSpec((BLOCK, ZQ_W), lambda b, n: (row(b, n), ZQ_BLK)),
            pl.BlockSpec((BLOCK, ZKV_W), lambda b, n: (row(b, n), ZKV_BLK)),
            pl.BlockSpec((BLOCK, ZKV_W), lambda b, n: (prev(b, n), ZKV_BLK)),
            pl.BlockSpec((BLOCK, LANES), lambda b, n: (row(b, n), 0)),
            pl.BlockSpec((BLOCK, LANES), lambda b, n: (row(b, n), 0)),
            pl.BlockSpec((BLOCK, LANES), lambda b, n: (prev(b, n), 0)),
            pl.BlockSpec((BLOCK, LANES), lambda b, n: (prev(b, n), 0)),
            pl.BlockSpec((1, N_Q_HEADS), lambda b, n: (0, 0))]


ATTN_SCALE = HEAD_DIM ** -0.5


def _stack_heads(t, kh):
    return jnp.concatenate([t[:, (kh * GQA_GROUP + g) * HEAD_DIM:(kh * GQA_GROUP + g + 1) * HEAD_DIM]
                            for g in range(GQA_GROUP)], axis=0)


def _stack_sinks(sink, kh):
    return jnp.concatenate([jnp.broadcast_to(sink[:, kh * GQA_GROUP + g:kh * GQA_GROUP + g + 1], (BLOCK, 1))
                            for g in range(GQA_GROUP)], axis=0)


def _attn_band(q_ref, kv_ref, kvp_ref, c_ref, s_ref, cp_ref, sp_ref):
    c, s = c_ref[...], s_ref[...]
    q = _rope(q_ref[...], _tile4(c), _tile4(s)) * ATTN_SCALE
    kv, kvp = kv_ref[...], kvp_ref[...]
    k = _rope(kv[:, :KV_WIDTH], c, s)
    kp = _rope(kvp[:, :KV_WIDTH], cp_ref[...], sp_ref[...])
    kb = jnp.concatenate([kp, k], axis=0)
    vb = jnp.concatenate([kvp[:, KV_WIDTH:], kv[:, KV_WIDTH:]], axis=0)
    return q, kb, vb


def _attention_fwd(z, ctab, stab, sinks, n_seq):
    t = z.shape[0]
    nb = t // n_seq // BLOCK

    def body(q_ref, kv_ref, kvp_ref, c_ref, s_ref, cp_ref, sp_ref, sink_ref, o_ref, lse_ref):
        q, kb, vb = _attn_band(q_ref, kv_ref, kvp_ref, c_ref, s_ref, cp_ref, sp_ref)
        mask = _attn_mask(pl.program_id(1))
        sink = sink_ref[...]
        lane = lax.broadcasted_iota(jnp.int32, (BLOCK, N_Q_HEADS), 1)
        lse_all = jnp.zeros((BLOCK, N_Q_HEADS), F32)
        for kh in range(N_KV_HEADS):
            sc = jnp.where(mask, _dot_nt(_stack_heads(q, kh), kb[:, kh * HEAD_DIM:(kh + 1) * HEAD_DIM]), NEG_INF)
            sk = _stack_sinks(sink, kh)
            m = jnp.maximum(jnp.max(sc, axis=-1, keepdims=True), sk)
            pr = jnp.exp(sc - m)
            den = jnp.sum(pr, axis=-1, keepdims=True) + jnp.exp(sk - m)
            out = _dot_nn(pr * (1.0 / den), vb[:, kh * HEAD_DIM:(kh + 1) * HEAD_DIM])
            lse = m + jnp.log(den)
            for g in range(GQA_GROUP):
                h = kh * GQA_GROUP + g
                o_ref[:, h * HEAD_DIM:(h + 1) * HEAD_DIM] = out[g * BLOCK:(g + 1) * BLOCK].astype(o_ref.dtype)
                lse_all = jnp.where(lane == h, lse[g * BLOCK:(g + 1) * BLOCK], lse_all)
        lse_ref[...] = lse_all

    return pl.pallas_call(
        body, name="attn_fwd", grid=(n_seq, nb),
        out_shape=[jax.ShapeDtypeStruct((t, Q_WIDTH), MXU_DTYPE), jax.ShapeDtypeStruct((t, N_Q_HEADS), F32)],
        in_specs=_attn_specs(nb),
        out_specs=[pl.BlockSpec((BLOCK, Q_WIDTH), lambda b, n: (b * nb + n, 0)),
                   pl.BlockSpec((BLOCK, N_Q_HEADS), lambda b, n: (b * nb + n, 0))],
        compiler_params=_params(("parallel", "parallel")))(z, z, z, ctab, stab, ctab, stab, sinks)


def _attention_bwd(z, ctab, stab, sinks, ya, lse, dya, n_seq):
    t = z.shape[0]
    nb = t // n_seq // BLOCK

    def body(q_ref, kv_ref, kvp_ref, c_ref, s_ref, cp_ref, sp_ref, sink_ref, o_ref, lse_ref, do_ref,
             dq_ref, dkv_ref, dkvp_ref, dsink_ref):
        q, kb, vb = _attn_band(q_ref, kv_ref, kvp_ref, c_ref, s_ref, cp_ref, sp_ref)
        mask = _attn_mask(pl.program_id(1))
        sink = sink_ref[...]
        lse_all = lse_ref[...]
        o = o_ref[...].astype(F32)
        do = do_ref[...].astype(F32)
        lane = lax.broadcasted_iota(jnp.int32, (1, N_Q_HEADS), 1)
        dsink = jnp.zeros((1, N_Q_HEADS), F32)
        dq_parts = []
        dk_parts, dv_parts = [], []
        for kh in range(N_KV_HEADS):
            kbh = kb[:, kh * HEAD_DIM:(kh + 1) * HEAD_DIM]
            vbh = vb[:, kh * HEAD_DIM:(kh + 1) * HEAD_DIM]
            qs, dos = _stack_heads(q, kh), _stack_heads(do, kh)
            lse = jnp.concatenate([lse_all[:, kh * GQA_GROUP + g:kh * GQA_GROUP + g + 1] for g in range(GQA_GROUP)], axis=0)
            pr = jnp.exp(jnp.where(mask, _dot_nt(qs, kbh), NEG_INF) - lse)
            delta = jnp.sum(dos * _stack_heads(o, kh), axis=-1, keepdims=True)
            ds = pr * (_dot_nt(dos, vbh) - delta)
            dqs = _dot_nn(ds, kbh)
            dq_parts += [dqs[g * BLOCK:(g + 1) * BLOCK] for g in range(GQA_GROUP)]
            dk_parts.append(_dot_tn(ds, qs))
            dv_parts.append(_dot_tn(pr, dos))
            dsk = jnp.exp(_stack_sinks(sink, kh) - lse) * delta
            for g in range(GQA_GROUP):
                dsink = dsink + jnp.where(lane == kh * GQA_GROUP + g, -jnp.sum(dsk[g * BLOCK:(g + 1) * BLOCK]), 0.0)
        c, s = c_ref[...], s_ref[...]
        dq_ref[...] = _rope_t(jnp.concatenate(dq_parts, axis=1) * ATTN_SCALE, _tile4(c), _tile4(s)).astype(dq_ref.dtype)
        dk = jnp.concatenate(dk_parts, axis=1)
        dv = jnp.concatenate(dv_parts, axis=1)
        dkv_ref[:, :KV_WIDTH] = _rope_t(dk[BLOCK:], c, s)
        dkv_ref[:, KV_WIDTH:] = dv[BLOCK:]
        dkvp_ref[:, :KV_WIDTH] = _rope_t(dk[:BLOCK], cp_ref[...], sp_ref[...])
        dkvp_ref[:, KV_WIDTH:] = dv[:BLOCK]

        @pl.when((pl.program_id(0) == 0) & (pl.program_id(1) == 0))
        def _():
            dsink_ref[...] = jnp.zeros_like(dsink_ref)

        dsink_ref[...] += dsink

    row_spec = lambda w: pl.BlockSpec((BLOCK, w), lambda b, n: (b * nb + n, 0))
    return pl.pallas_call(
        body, name="attn_bwd", grid=(n_seq, nb),
        out_shape=[jax.ShapeDtypeStruct((t, Q_WIDTH), MXU_DTYPE), jax.ShapeDtypeStruct((t, ZKV_W), F32),
                   jax.ShapeDtypeStruct((t, ZKV_W), F32), jax.ShapeDtypeStruct((1, N_Q_HEADS), F32)],
        in_specs=_attn_specs(nb) + [row_spec(Q_WIDTH), row_spec(N_Q_HEADS), row_spec(Q_WIDTH)],
        out_specs=[row_spec(Q_WIDTH), row_spec(ZKV_W), row_spec(ZKV_W), pl.BlockSpec((1, N_Q_HEADS), lambda b, n: (0, 0))],
        compiler_params=_params(("arbitrary", "arbitrary")))(z, z, z, ctab, stab, ctab, stab, sinks, ya, lse, dya)


def _assemble_dz(dgin, dq, dkv, dkvp, ds, dc, n_seq):
    t = dq.shape[0]
    nb = t // n_seq // BLOCK

    def body(dg_ref, dq_ref, dkv_ref, dkvn_ref, ds_ref, dc_ref, o_ref):
        last = pl.program_id(1) == nb - 1
        o_ref[:, :ZG_W] = dg_ref[...]
        o_ref[:, ZG_W:ZG_W + ZQ_W] = dq_ref[...]
        dkv_sum = dkv_ref[...] + jnp.where(last, 0.0, dkvn_ref[...])
        o_ref[:, ZG_W + ZQ_W:ZG_W + ZQ_W + ZKV_W] = dkv_sum.astype(o_ref.dtype)
        o_ref[:, ZG_W + ZQ_W + ZKV_W:ZG_W + ZQ_W + ZKV_W + ZS_W] = ds_ref[...]
        o_ref[:, ZG_W + ZQ_W + ZKV_W + ZS_W:] = dc_ref[...]

    row_spec = lambda w: pl.BlockSpec((BLOCK, w), lambda b, n: (b * nb + n, 0))
    nxt = pl.BlockSpec((BLOCK, ZKV_W), lambda b, n: (b * nb + jnp.minimum(n + 1, nb - 1), 0))
    return pl.pallas_call(
        body, name="assemble_dz", grid=(n_seq, nb), out_shape=jax.ShapeDtypeStruct((t, Z_WIDTH), MXU_DTYPE),
        in_specs=[row_spec(ZG_W), row_spec(ZQ_W), row_spec(ZKV_W), nxt, row_spec(ZS_W), row_spec(ZC_W)],
        out_specs=row_spec(Z_WIDTH), compiler_params=_params(("parallel", "parallel")))(dgin, dq, dkv, dkvp, ds, dc)


def _ssm_coeff_tile(lam_re, lam_im, log_dt):
    lr = jnp.minimum(lam_re, -1e-4)
    dt = jnp.exp(log_dt)
    mag = jnp.exp(lr * dt)
    a_re = mag * jnp.cos(lam_im * dt)
    a_im = mag * jnp.sin(lam_im * dt)
    den = lr * lr + lam_im * lam_im
    x_re = a_re - 1.0
    f_re = (x_re * lr + a_im * lam_im) / den
    f_im = (a_im * lr - x_re * lam_im) / den
    return a_re, a_im, f_re, f_im


def _ssm_coeffs(lam_re, lam_im, log_dt):
    def body(lr_ref, li_ref, dt_ref, *o_refs):
        for r, v in zip(o_refs, _ssm_coeff_tile(lr_ref[...], li_ref[...], dt_ref[...]), strict=True):
            r[...] = v

    return pl.pallas_call(body, name="ssm_coeffs", out_shape=[jax.ShapeDtypeStruct(lam_re.shape, F32)] * 4)(
        lam_re, lam_im, log_dt)


def _ssm_coeffs_bwd(lam_re, lam_im, log_dt, cts):
    def body(lr_ref, li_ref, dt_ref, c0, c1, c2, c3, dlr_ref, dli_ref, ddt_ref):
        _, vjp = jax.vjp(_ssm_coeff_tile, lr_ref[...], li_ref[...], dt_ref[...])
        dlr, dli, ddt = vjp((c0[...], c1[...], c2[...], c3[...]))
        dlr_ref[...] = dlr
        dli_ref[...] = dli
        ddt_ref[...] = ddt

    return pl.pallas_call(
        body, name="ssm_coeffs_bwd",
        out_shape=[jax.ShapeDtypeStruct(lam_re.shape, F32)] * 2 + [jax.ShapeDtypeStruct(log_dt.shape, F32)])(
        lam_re, lam_im, log_dt, *cts)


def _ssm_chunk(t):
    return _pick(t, (256, 128))


def _ssm_fwd(z, bmat, a_row, f_row, cmat, d_row, n_seq):
    t = z.shape[0]
    seq = t // n_seq
    lc = _ssm_chunk(seq)
    nc = seq // lc
    n2 = 2 * SSM_LANES

    def body(u_ref, b_ref, a_ref, f_ref, c_ref, d_ref, y_ref, s_ref, bu_ref, st_ref):
        @pl.when(pl.program_id(1) == 0)
        def _():
            st_ref[...] = jnp.zeros_like(st_ref)

        u = u_ref[...]
        proj = _dot_nn(u, b_ref[...])
        fr, fi = f_ref[:, :SSM_LANES], f_ref[:, SSM_LANES:]
        pr, pi = proj[:, :SSM_LANES], proj[:, SSM_LANES:]
        bu_ref[:, :SSM_LANES] = fr * pr - fi * pi
        bu_ref[:, SSM_LANES:] = fr * pi + fi * pr
        ar, ai = a_ref[:, :SSM_LANES], a_ref[:, SSM_LANES:]

        def step(i, carry):
            sr, si = carry
            nr = ar * sr - ai * si + bu_ref[pl.ds(i, 1), pl.ds(0, SSM_LANES)]
            ni = ar * si + ai * sr + bu_ref[pl.ds(i, 1), pl.ds(SSM_LANES, SSM_LANES)]
            s_ref[pl.ds(i, 1), pl.ds(0, SSM_LANES)] = nr
            s_ref[pl.ds(i, 1), pl.ds(SSM_LANES, SSM_LANES)] = ni
            return nr, ni

        sr, si = lax.fori_loop(0, lc, step, (st_ref[0:1, :SSM_LANES], st_ref[0:1, SSM_LANES:]), unroll=8)
        st_ref[0:1, :SSM_LANES] = sr
        st_ref[0:1, SSM_LANES:] = si
        y_ref[...] = _dot_nn(s_ref[...], c_ref[...]) + d_ref[...] * u

    const = lambda shape: pl.BlockSpec(shape, lambda b, c: (0, 0))
    return pl.pallas_call(
        body, name="ssm_fwd", grid=(n_seq, nc),
        out_shape=[jax.ShapeDtypeStruct((t, SSM_WIDTH), F32), jax.ShapeDtypeStruct((t, n2), F32)],
        in_specs=[pl.BlockSpec((lc, ZS_W), lambda b, c: (b * nc + c, ZS_BLK)), const((SSM_WIDTH, n2)), const((1, n2)),
                  const((1, n2)), const((n2, SSM_WIDTH)), const((1, SSM_WIDTH))],
        out_specs=[pl.BlockSpec((lc, SSM_WIDTH), lambda b, c: (b * nc + c, 0)),
                   pl.BlockSpec((lc, n2), lambda b, c: (b * nc + c, 0))],
        scratch_shapes=[pltpu.VMEM((lc, n2), F32), pltpu.VMEM((8, n2), F32)],
        compiler_params=_params(("arbitrary", "arbitrary")))(z, bmat, a_row, f_row, cmat, d_row)


def _ssm_bwd(z, states, dy, bmat, a_row, f_row, cmat, d_row, n_seq):
    t = z.shape[0]
    seq = t // n_seq
    lc = _ssm_chunk(seq)
    nc = seq // lc
    n2 = 2 * SSM_LANES

    def body(dy_ref, u_ref, s_ref, b_ref, a_ref, f_ref, c_ref, d_ref,
             du_ref, db_ref, dc_ref, da_ref, df_ref, dd_ref, g_ref, carry_ref):
        @pl.when((pl.program_id(0) == 0) & (pl.program_id(1) == 0))
        def _():
            for r in (db_ref, dc_ref, da_ref, df_ref, dd_ref):
                r[...] = jnp.zeros_like(r)

        @pl.when(pl.program_id(1) == 0)
        def _():
            carry_ref[...] = jnp.zeros_like(carry_ref)

        dy, u, st = dy_ref[...], u_ref[...], s_ref[...]
        g_ref[0:lc, :] = _dot_nt(dy, c_ref[...])
        g_ref[lc:lc + 8, :] = carry_ref[...]
        dc_ref[...] += _dot_tn(st, dy)
        dd_ref[...] += jnp.sum(dy * u, axis=0, keepdims=True)
        ar, ai = a_ref[:, :SSM_LANES], a_ref[:, SSM_LANES:]

        def step(i, carry):
            gr, gi = carry
            r = lc - 1 - i
            nr = g_ref[pl.ds(r, 1), pl.ds(0, SSM_LANES)] + ar * gr + ai * gi
            ni = g_ref[pl.ds(r, 1), pl.ds(SSM_LANES, SSM_LANES)] - ai * gr + ar * gi
            g_ref[pl.ds(r, 1), pl.ds(0, SSM_LANES)] = nr
            g_ref[pl.ds(r, 1), pl.ds(SSM_LANES, SSM_LANES)] = ni
            return nr, ni

        gr, gi = lax.fori_loop(0, lc, step, (carry_ref[0:1, :SSM_LANES], carry_ref[0:1, SSM_LANES:]), unroll=8)
        carry_ref[0:1, :SSM_LANES] = gr
        carry_ref[0:1, SSM_LANES:] = gi
        sr, si = st[:, :SSM_LANES], st[:, SSM_LANES:]
        gnr, gni = g_ref[pl.ds(1, lc), pl.ds(0, SSM_LANES)], g_ref[pl.ds(1, lc), pl.ds(SSM_LANES, SSM_LANES)]
        da_ref[:, :SSM_LANES] += jnp.sum(gnr * sr + gni * si, axis=0, keepdims=True)
        da_ref[:, SSM_LANES:] += jnp.sum(gni * sr - gnr * si, axis=0, keepdims=True)
        gr_all, gi_all = g_ref[0:lc, :SSM_LANES], g_ref[0:lc, SSM_LANES:]
        proj = _dot_nn(u, b_ref[...])
        pr, pi = proj[:, :SSM_LANES], proj[:, SSM_LANES:]
        df_ref[:, :SSM_LANES] += jnp.sum(gr_all * pr + gi_all * pi, axis=0, keepdims=True)
        df_ref[:, SSM_LANES:] += jnp.sum(gi_all * pr - gr_all * pi, axis=0, keepdims=True)
        fr, fi = f_ref[:, :SSM_LANES], f_ref[:, SSM_LANES:]
        dproj = jnp.concatenate([fr * gr_all + fi * gi_all, fr * gi_all - fi * gr_all], axis=1).astype(MXU_DTYPE)
        du_ref[...] = (_dot_nt(dproj, b_ref[...]) + d_ref[...] * dy).astype(du_ref.dtype)
        db_ref[...] += _dot_tn(u, dproj)

    const = lambda shape: pl.BlockSpec(shape, lambda b, c: (0, 0))
    rows = lambda w, cb: pl.BlockSpec((lc, w), functools.partial(lambda b, c, cb: (b * nc + nc - 1 - c, cb), cb=cb))
    return pl.pallas_call(
        body, name="ssm_bwd", grid=(n_seq, nc),
        out_shape=[jax.ShapeDtypeStruct((t, SSM_WIDTH), MXU_DTYPE), jax.ShapeDtypeStruct((SSM_WIDTH, n2), F32),
                   jax.ShapeDtypeStruct((n2, SSM_WIDTH), F32), jax.ShapeDtypeStruct((1, n2), F32),
                   jax.ShapeDtypeStruct((1, n2), F32), jax.ShapeDtypeStruct((1, SSM_WIDTH), F32)],
        in_specs=[rows(SSM_WIDTH, 0), rows(ZS_W, ZS_BLK), rows(n2, 0), const((SSM_WIDTH, n2)), const((1, n2)),
                  const((1, n2)), const((n2, SSM_WIDTH)), const((1, SSM_WIDTH))],
        out_specs=[rows(SSM_WIDTH, 0), const((SSM_WIDTH, n2)), const((n2, SSM_WIDTH)), const((1, n2)), const((1, n2)),
                   const((1, SSM_WIDTH))],
        scratch_shapes=[pltpu.VMEM((lc + 8, n2), F32), pltpu.VMEM((8, n2), F32)],
        compiler_params=_params(("arbitrary", "arbitrary")))(dy, z, states, bmat, a_row, f_row, cmat, d_row)


def _conv_chunk(t):
    return _pick(t, (512, 256, 128))


def _glu(c):
    return c[:, :CONV_WIDTH] * jax.nn.sigmoid(c[:, CONV_WIDTH:])


def _conv_post_tile(v, g, b):
    mu = jnp.mean(v, axis=-1, keepdims=True)
    var = jnp.mean(jnp.square(v - mu), axis=-1, keepdims=True)
    return jax.nn.silu((v - mu) * lax.rsqrt(var + EPS) * g + b)


def _conv_specs(lc, nc):
    per = lc // CONV_HALO
    return [pl.BlockSpec((lc, ZC_W), lambda b, c: (b * nc + c, ZC_BLK)),
            pl.BlockSpec((CONV_HALO, ZC_W), lambda b, c: (jnp.maximum((b * nc + c) * per - 1, 0), ZC_BLK))]


def _conv_fill(c_ref, cp_ref, ue_ref, lc):
    ue_ref[0:CONV_HALO, :] = jnp.where(pl.program_id(1) > 0, _glu(cp_ref[...]), 0.0)
    ue_ref[CONV_HALO:CONV_HALO + lc, :] = _glu(c_ref[...])


def _conv_apply(ue_ref, w_ref, b_ref, lc):
    acc = jnp.zeros((lc, CONV_WIDTH), F32) + b_ref[...]
    for k in range(CONV_K):
        acc = acc + w_ref[k:k + 1, :] * ue_ref[pl.ds(k + CONV_HALO - CONV_K + 1, lc), :]
    return acc


def _conv_fwd(z, dw_w, dw_b, ln_g, ln_b, n_seq):
    t = z.shape[0]
    seq = t // n_seq
    lc = _conv_chunk(seq)
    nc = seq // lc

    def body(c_ref, cp_ref, w_ref, b_ref, g_ref, lb_ref, o_ref, ue_ref):
        _conv_fill(c_ref, cp_ref, ue_ref, lc)
        o_ref[...] = _conv_post_tile(_conv_apply(ue_ref, w_ref, b_ref, lc), g_ref[...], lb_ref[...]).astype(o_ref.dtype)

    const = lambda a: pl.BlockSpec(a.shape, lambda b, c: (0, 0))
    return pl.pallas_call(
        body, name="conv_fwd", grid=(n_seq, nc), out_shape=jax.ShapeDtypeStruct((t, CONV_WIDTH), MXU_DTYPE),
        in_specs=_conv_specs(lc, nc) + [const(dw_w), const(dw_b), const(ln_g), const(ln_b)],
        out_specs=pl.BlockSpec((lc, CONV_WIDTH), lambda b, c: (b * nc + c, 0)),
        scratch_shapes=[pltpu.VMEM((CONV_HALO + lc, CONV_WIDTH), F32)],
        compiler_params=_params(("parallel", "parallel")))(z, z, dw_w, dw_b, ln_g, ln_b)


def _conv_bwd_post(z, duc, dw_w, dw_b, ln_g, ln_b, n_seq):
    t = z.shape[0]
    seq = t // n_seq
    lc = _conv_chunk(seq)
    nc = seq // lc

    def body(c_ref, cp_ref, duc_ref, w_ref, b_ref, g_ref, lb_ref, dv_ref, dg_ref, dlb_ref, db_ref, ue_ref):
        @pl.when((pl.program_id(0) == 0) & (pl.program_id(1) == 0))
        def _():
            for r in (dg_ref, dlb_ref, db_ref):
                r[...] = jnp.zeros_like(r)

        _conv_fill(c_ref, cp_ref, ue_ref, lc)
        _, vjp = jax.vjp(_conv_post_tile, _conv_apply(ue_ref, w_ref, b_ref, lc), g_ref[...], lb_ref[...])
        dv, dg, dlb = vjp(duc_ref[...])
        dv_ref[...] = dv
        dg_ref[...] += dg
        dlb_ref[...] += dlb
        db_ref[...] += jnp.sum(dv, axis=0, keepdims=True)

    const = lambda a: pl.BlockSpec(a.shape, lambda b, c: (0, 0))
    vec = jax.ShapeDtypeStruct((1, CONV_WIDTH), F32)
    return pl.pallas_call(
        body, name="conv_bwd_post", grid=(n_seq, nc), out_shape=[jax.ShapeDtypeStruct((t, CONV_WIDTH), F32), vec, vec, vec],
        in_specs=_conv_specs(lc, nc) + [pl.BlockSpec((lc, CONV_WIDTH), lambda b, c: (b * nc + c, 0)),
                                       const(dw_w), const(dw_b), const(ln_g), const(ln_b)],
        out_specs=[pl.BlockSpec((lc, CONV_WIDTH), lambda b, c: (b * nc + c, 0))] + [const(dw_b)] * 3,
        scratch_shapes=[pltpu.VMEM((CONV_HALO + lc, CONV_WIDTH), F32)],
        compiler_params=_params(("arbitrary", "arbitrary")))(z, z, duc, dw_w, dw_b, ln_g, ln_b)


def _conv_bwd_taps(z, dv, dw_w, n_seq):
    t = z.shape[0]
    seq = t // n_seq
    lc = _conv_chunk(seq)
    nc = seq // lc
    per = lc // CONV_HALO
    n_halo = t // CONV_HALO

    def body(c_ref, cp_ref, dv_ref, dvn_ref, w_ref, dc_ref, dw_ref, ue_ref, dve_ref):
        @pl.when((pl.program_id(0) == 0) & (pl.program_id(1) == 0))
        def _():
            dw_ref[...] = jnp.zeros_like(dw_ref)

        _conv_fill(c_ref, cp_ref, ue_ref, lc)
        dv = dv_ref[...]
        dve_ref[0:lc, :] = dv
        dve_ref[lc:lc + CONV_HALO, :] = jnp.where(pl.program_id(1) < nc - 1, dvn_ref[...], 0.0)
        du = jnp.zeros((lc, CONV_WIDTH), F32)
        for k in range(CONV_K):
            du = du + w_ref[k:k + 1, :] * dve_ref[pl.ds(CONV_K - 1 - k, lc), :]
            dw_ref[k:k + 1, :] += jnp.sum(dv * ue_ref[pl.ds(k + CONV_HALO - CONV_K + 1, lc), :], axis=0, keepdims=True)
        c = c_ref[...]
        a, sg = c[:, :CONV_WIDTH], jax.nn.sigmoid(c[:, CONV_WIDTH:])
        dc_ref[:, :CONV_WIDTH] = (du * sg).astype(dc_ref.dtype)
        dc_ref[:, CONV_WIDTH:] = (du * a * sg * (1.0 - sg)).astype(dc_ref.dtype)

    return pl.pallas_call(
        body, name="conv_bwd_taps", grid=(n_seq, nc),
        out_shape=[jax.ShapeDtypeStruct((t, ZC_W), MXU_DTYPE), jax.ShapeDtypeStruct((CONV_HALO, CONV_WIDTH), F32)],
        in_specs=_conv_specs(lc, nc) + [
            pl.BlockSpec((lc, CONV_WIDTH), lambda b, c: (b * nc + c, 0)),
            pl.BlockSpec((CONV_HALO, CONV_WIDTH), lambda b, c: (jnp.minimum((b * nc + c + 1) * per, n_halo - 1), 0)),
            pl.BlockSpec(dw_w.shape, lambda b, c: (0, 0))],
        out_specs=[pl.BlockSpec((lc, ZC_W), lambda b, c: (b * nc + c, 0)),
                   pl.BlockSpec((CONV_HALO, CONV_WIDTH), lambda b, c: (0, 0))],
        scratch_shapes=[pltpu.VMEM((CONV_HALO + lc, CONV_WIDTH), F32), pltpu.VMEM((lc + CONV_HALO, CONV_WIDTH), F32)],
        compiler_params=_params(("arbitrary", "arbitrary")))(z, z, dv, dv, dw_w)


def _row(v):
    return v.reshape(1, -1)


def _ssm_mats(b_re, b_im, c_re, c_im):
    eye = jnp.eye(SSM_GROUPS, dtype=bool)
    bm = jnp.stack([b_re, b_im]).transpose(1, 3, 0, 2)[:, :, :, None, :]
    bmat = jnp.where(eye[:, None, None, :, None], bm, 0.0).reshape(SSM_WIDTH, 2 * SSM_LANES)
    cm = jnp.stack([c_re, -c_im]).transpose(0, 1, 3, 2)[:, :, :, None, :]
    cmat = jnp.where(eye[None, :, None, :, None], cm, 0.0).reshape(2 * SSM_LANES, SSM_WIDTH)
    return bmat.astype(MXU_DTYPE), cmat.astype(MXU_DTYPE)


def _ssm_mats_t(dbmat, dcmat):
    eye = jnp.eye(SSM_GROUPS, dtype=bool)
    db = dbmat.reshape(SSM_GROUPS, SSM_GROUP, 2, SSM_GROUPS, SSM_STATE)
    db = jnp.sum(jnp.where(eye[:, None, None, :, None], db, 0.0), axis=3).transpose(2, 0, 3, 1)
    dc = dcmat.reshape(2, SSM_GROUPS, SSM_STATE, SSM_GROUPS, SSM_GROUP)
    dc = jnp.sum(jnp.where(eye[None, :, None, :, None], dc, 0.0), axis=3).transpose(0, 1, 3, 2)
    return db[0], db[1], dc[0], -dc[1]


def _layer_fwd(x, p, w, sp, ctab, stab, n_seq):
    (h,), _ = _token_call("norm_in", lambda x, g: ([_norm_in_tile(x, g)], []), 512, [_whole(x)], [sp["mix_norm_g"]],
                          [(x.shape[1], MXU_DTYPE)], [])
    z = _matmul_nn("mm_in", h, w["w_in"], F32)
    ya, lse = _attention_fwd(z, ctab, stab, sp["attn_sinks"], n_seq)
    ys, states = _ssm_fwd(z, sp["bmat"], sp["a_row"], sp["f_row"], sp["cmat"], sp["ssm_d"], n_seq)
    uc = _conv_fwd(z, w["conv_dw_w"], sp["conv_dw_b"], sp["conv_norm_g"], sp["conv_norm_b"], n_seq)
    merge_consts = [w["w_attn_out"], w["w_ssm_glu"], sp["b_ssm_glu"], w["w_conv_out"], sp["b_gate"], w["w_mix_out"],
                    sp["ffn_norm_g"]]
    (x1, hf), _ = _token_call("merge", lambda *a: (list(_merge_tile(*_f32s(a))), []), 256,
                              [_whole(x), _whole(ya), _whole(ys), _whole(uc), (z, ZG_W, 0)], merge_consts,
                              [(x.shape[1], F32), (x.shape[1], MXU_DTYPE)], [])
    gate, up, act = _ffn_in_act(hf, w["w_ffn_in"])
    ffn = _matmul_nn("mm_ffn_out", act, w["w_ffn_out"], F32)

    def ple_fn(x1, ffn, p, w_pi, g_ple, w_pg):
        x2 = x1 + ffn
        return [x2, _ple_tile(x2, p, w_pi.astype(F32), g_ple, w_pg.astype(F32))], []

    (x2, x3), _ = _token_call("ple", ple_fn, 512, [_whole(x1), _whole(ffn), _whole(p)],
                              [w["w_ple_in"], sp["ple_norm_g"], w["w_ple_gate"]], [(x.shape[1], F32)] * 2, [])
    saved = dict(x=x, h=h, z=z, ya=ya, lse=lse, ys=ys, states=states, uc=uc, hf=hf, gate=gate, up=up, act=act, x2=x2, p=p)
    return x3, saved


def _layer_bwd(dx3, sv, w, sp, ctab, stab, n_seq):
    d = dx3.shape[1]
    gw, gs = {}, {}

    def ple_bwd(x2, p, dx3, w_pi, g_ple, w_pg):
        _, vjp = jax.vjp(lambda x2, w_pi, g_ple, w_pg: _ple_tile(x2, p, w_pi, g_ple, w_pg), x2, w_pi.astype(F32), g_ple,
                         w_pg.astype(F32))
        dx2, dw_pi, dg_ple, dw_pg = vjp(dx3)
        return [dx2, dx2], [dw_pi, dg_ple, dw_pg]

    (dx2, dffn), (gw["w_ple_in"], gs["ple_norm_g"], gw["w_ple_gate"]) = _token_call(
        "ple_bwd", ple_bwd, 256, [_whole(sv["x2"]), _whole(sv["p"]), _whole(dx3)],
        [w["w_ple_in"], sp["ple_norm_g"], w["w_ple_gate"]], [(d, F32), (d, MXU_DTYPE)],
        [w["w_ple_in"].shape, (1, d), w["w_ple_gate"].shape])

    dgate, dup = _ffn_mid_bwd(dffn, w["w_ffn_out"], sv["gate"], sv["up"])
    gw["w_ffn_out"] = _matmul_tn("mm_ffn_out_dw", sv["act"], dffn)
    dhf = _ffn_in_dx(dgate, dup, w["w_ffn_in"])
    gw["w_ffn_in"] = jnp.concatenate([_matmul_tn("mm_ffn_gate_dw", sv["hf"], dgate), _matmul_tn("mm_ffn_up_dw", sv["hf"], dup)],
                                     axis=1)

    def merge_bwd(x, ya, ys, uc, gin, dx1, dhf, *consts):
        consts = _f32s(consts)
        _, vjp = jax.vjp(_merge_tile, *_f32s((x, ya, ys, uc, gin)), *consts)
        g = vjp((dx1, dhf))
        return list(g[:5]), list(g[5:])

    merge_consts = [w["w_attn_out"], w["w_ssm_glu"], sp["b_ssm_glu"], w["w_conv_out"], sp["b_gate"], w["w_mix_out"],
                    sp["ffn_norm_g"]]
    (dx_res, dya, dys, duc, dgin), macc = _token_call(
        "merge_bwd", merge_bwd, 256,
        [_whole(sv["x"]), _whole(sv["ya"]), _whole(sv["ys"]), _whole(sv["uc"]), (sv["z"], ZG_W, 0), _whole(dx2), _whole(dhf)],
        merge_consts, [(d, F32), (Q_WIDTH, MXU_DTYPE), (SSM_WIDTH, F32), (CONV_WIDTH, F32), (ZG_W, MXU_DTYPE)],
        [c.shape for c in merge_consts])
    gw["w_attn_out"], gw["w_ssm_glu"], gs["b_ssm_glu"], gw["w_conv_out"], gs["b_gate"], gw["w_mix_out"], gs["ffn_norm_g"] = macc

    dv, gs["conv_norm_g"], gs["conv_norm_b"], gs["conv_dw_b"] = _conv_bwd_post(
        sv["z"], duc, w["conv_dw_w"], sp["conv_dw_b"], sp["conv_norm_g"], sp["conv_norm_b"], n_seq)
    dzc, dw_taps = _conv_bwd_taps(sv["z"], dv, w["conv_dw_w"], n_seq)
    gw["conv_dw_w"] = dw_taps[:CONV_K]

    dzs, gs["bmat"], gs["cmat"], gs["a_row"], gs["f_row"], gs["ssm_d"] = _ssm_bwd(
        sv["z"], sv["states"], dys, sp["bmat"], sp["a_row"], sp["f_row"], sp["cmat"], sp["ssm_d"], n_seq)

    dzq, dkv, dkvp, gs["attn_sinks"] = _attention_bwd(sv["z"], ctab, stab, sp["attn_sinks"], sv["ya"], sv["lse"], dya, n_seq)
    dz = _assemble_dz(dgin, dzq, dkv, dkvp, dzs, dzc, n_seq)
    dh = _matmul_nt("mm_in_dx", dz, w["w_in"], F32)
    gw["w_in"] = _matmul_tn("mm_in_dw", sv["h"], dz)

    def norm_bwd(x, dh, dx_res, g):
        _, vjp = jax.vjp(_norm_in_tile, x, g)
        dx, dg = vjp(dh)
        return [dx + dx_res], [dg]

    (dx,), (gs["mix_norm_g"],) = _token_call("norm_in_bwd", norm_bwd, 512, [_whole(sv["x"]), _whole(dh), _whole(dx_res)],
                                             [sp["mix_norm_g"]], [(d, F32)], [(1, d)])
    return dx, gw, gs


def _loss_and_grad(x, target, g):
    def fn(x, tgt, g):
        def f(x, g):
            err = _rms(x, g) - tgt
            return 0.5 * jnp.mean(err * err, axis=-1, keepdims=True)

        per_token, vjp = jax.vjp(f, x, g)
        dx, dg = vjp(jnp.ones_like(per_token))
        return [dx], [jnp.sum(per_token, axis=0, keepdims=True), dg]

    (dx,), (loss, dg) = _token_call("loss", fn, 512, [_whole(x), _whole(target)], [g], [(x.shape[1], F32)],
                                    [(8, LANES), (1, x.shape[1])])
    return loss[0, 0], dx, dg


def _mesh_place():
    return lax.axis_index("x"), lax.axis_index("y"), lax.axis_index("c")


def _flip(v, bit):
    return 1 - v if bit else v


_ANY = pl.BlockSpec(memory_space=pl.ANY)
_MESH = pl.DeviceIdType.MESH


def _all_gather(name, xs):
    n = len(xs)

    def body(*refs):
        x_refs, out_refs = refs[:n], refs[n:2 * n]
        send_sems, recv_sems, local_sems = refs[2 * n:]
        mx, my, mc = _mesh_place()
        me, sibling = (mx, my, mc), (mx, my, 1 - mc)
        chips = [(1 - mx, my), (mx, 1 - my), (1 - mx, 1 - my)]

        def slot(a, px, py, pc):
            return out_refs[a].at[4 * px + 2 * py + pc]

        def copy(a, k, block, to, src=None):
            return pltpu.make_async_remote_copy(
                src_ref=slot(a, *block) if src is None else src, dst_ref=slot(a, *block), send_sem=send_sems.at[7 * a + k],
                recv_sem=recv_sems.at[7 * a + k], device_id=to, device_id_type=_MESH)

        mine = [pltpu.make_async_copy(x_refs[a], slot(a, *me), local_sems.at[a]) for a in range(n)]
        for cp in mine:
            cp.start()
        first = [copy(a, 0, me, sibling, src=x_refs[a]) for a in range(n)]
        first += [copy(a, 1 + j, me, (*chip, mc), src=x_refs[a]) for j, chip in enumerate(chips) for a in range(n)]
        for cp in first:
            cp.start()
        passed = []
        for j, chip in enumerate(chips):
            for a in range(n):
                copy(a, 1 + j, (*chip, mc), me).wait_recv()
                passed.append(copy(a, 4 + j, (*chip, mc), sibling))
                passed[-1].start()
        for a in range(n):
            copy(a, 0, sibling, me).wait_recv()
            for j, chip in enumerate(chips):
                copy(a, 4 + j, (*chip, 1 - mc), me).wait_recv()
        for cp in first + passed:
            cp.wait_send()
        for cp in mine:
            cp.wait()

    return pl.pallas_call(
        body, name=name, out_shape=[jax.ShapeDtypeStruct((N_DEV,) + x.shape, x.dtype) for x in xs], in_specs=[_ANY] * n,
        out_specs=[_ANY] * n,
        scratch_shapes=[pltpu.SemaphoreType.DMA((7 * n,)), pltpu.SemaphoreType.DMA((7 * n,)), pltpu.SemaphoreType.DMA((n,))])(*xs)


def _exchange(name, parts):
    n = len(parts)

    def body(*refs):
        p_refs, land_refs = refs[:n], refs[n:2 * n]
        send_sems, recv_sems, local_sems = refs[2 * n:]
        mx, my, mc = _mesh_place()
        me = 4 * mx + 2 * my + mc
        mine = [pltpu.make_async_copy(p_refs[a].at[me], land_refs[a].at[me], local_sems.at[a]) for a in range(n)]
        for cp in mine:
            cp.start()
        copies = []
        for rel in range(1, N_DEV):
            px, py, pc = _flip(mx, rel & 4), _flip(my, rel & 2), _flip(mc, rel & 1)
            for a in range(n):
                copies.append(pltpu.make_async_remote_copy(
                    src_ref=p_refs[a].at[4 * px + 2 * py + pc], dst_ref=land_refs[a].at[me],
                    send_sem=send_sems.at[7 * a + rel - 1], recv_sem=recv_sems.at[7 * a + rel - 1], device_id=(px, py, pc),
                    device_id_type=_MESH))
        for cp in copies:
            cp.start()
        for cp in copies:
            cp.wait()
        for cp in mine:
            cp.wait()

    return pl.pallas_call(
        body, name=name, out_shape=[jax.ShapeDtypeStruct(p.shape, p.dtype) for p in parts], in_specs=[_ANY] * n,
        out_specs=[_ANY] * n,
        scratch_shapes=[pltpu.SemaphoreType.DMA((7 * n,)), pltpu.SemaphoreType.DMA((7 * n,)), pltpu.SemaphoreType.DMA((n,))])(*parts)


def _adamw_math(g, w, m, v):
    m2 = ADAM_B1 * m + (1.0 - ADAM_B1) * g
    v2 = ADAM_B2 * v + (1.0 - ADAM_B2) * jnp.square(g)
    m_hat = m2 / (1.0 - ADAM_B1 ** ADAM_STEP)
    v_hat = v2 / (1.0 - ADAM_B2 ** ADAM_STEP)
    return g, -ADAM_LR * (m_hat / (jnp.sqrt(v_hat) + ADAM_EPS) + ADAM_WD * w), m2, v2


def _sum_blocks(ref):
    g = ref[0].astype(F32)
    for j in range(1, N_DEV):
        g = g + ref[j].astype(F32)
    return g


def _adamw_flat(name, parts, w, m, v):
    r = w.shape[0]
    tile = _pick(r, (1024, 512, 256, 128, 8))

    def body(p_ref, w_ref, m_ref, v_ref, *o_refs):
        for o, val in zip(o_refs, _adamw_math(_sum_blocks(p_ref), w_ref[...], m_ref[...], v_ref[...]), strict=True):
            o[...] = val

    flat = pl.BlockSpec((tile, LANES), lambda i: (i, 0))
    return pl.pallas_call(
        body, name=name, grid=(r // tile,), out_shape=[jax.ShapeDtypeStruct((r, LANES), F32)] * 4,
        in_specs=[pl.BlockSpec((N_DEV, tile, LANES), lambda i: (0, i, 0)), flat, flat, flat], out_specs=[flat] * 4,
        compiler_params=_params(("parallel",)))(parts, w, m, v)


def _adamw_cols(name, landed, base, w, m, v):
    depth, rows, cs = w.shape
    n_slab = -(-cs // LANES)
    tr = _pick(rows, (SLAB_TILE,))

    def body(*refs):
        slabs, (w_ref, m_ref, v_ref), o_refs = refs[:n_slab], refs[n_slab:n_slab + 3], refs[n_slab + 3:]
        g = jnp.concatenate([_sum_blocks(s)[:, :min(LANES, cs - LANES * k)] for k, s in enumerate(slabs)], axis=1)
        for o, val in zip(o_refs, _adamw_math(g, w_ref[...], m_ref[...], v_ref[...]), strict=True):
            o[...] = val

    slab = lambda k: pl.BlockSpec((N_DEV, tr, LANES), lambda l, i: (0, (base + (l * n_slab + k) * rows) // tr + i, 0))
    nat = pl.BlockSpec((None, tr, cs), lambda l, i: (l, i, 0))
    return pl.pallas_call(
        body, name=name, grid=(depth, rows // tr), out_shape=[jax.ShapeDtypeStruct(w.shape, F32)] * 4,
        in_specs=[slab(k) for k in range(n_slab)] + [nat] * 3, out_specs=[nat] * 4,
        compiler_params=_params(("parallel", "parallel")))(*[landed] * n_slab, w, m, v)


def _adamw_rows(name, landed, base, w, m, v):
    depth, rs, width = w.shape

    def body(p_ref, w_ref, m_ref, v_ref, *o_refs):
        for o, val in zip(o_refs, _adamw_math(_sum_blocks(p_ref), w_ref[...], m_ref[...], v_ref[...]), strict=True):
            o[...] = val

    nat = pl.BlockSpec((None, rs, width), lambda l: (l, 0, 0))
    return pl.pallas_call(
        body, name=name, grid=(depth,), out_shape=[jax.ShapeDtypeStruct(w.shape, F32)] * 4,
        in_specs=[pl.BlockSpec((N_DEV, rs, width), lambda l: (0, base // rs + l, 0)), nat, nat, nat], out_specs=[nat] * 4,
        compiler_params=_params(("parallel",)))(landed, w, m, v)


def _adamw_conv(landed, base, w, m, v):
    depth, taps, cs = w.shape

    def body(p_ref, w_ref, m_ref, v_ref, *o_refs):
        g = _sum_blocks(p_ref)[:taps, :cs]
        for o, val in zip(o_refs, _adamw_math(g, w_ref[...], m_ref[...], v_ref[...]), strict=True):
            o[...] = val

    nat = pl.BlockSpec((None, taps, cs), lambda l: (l, 0, 0))
    return pl.pallas_call(
        body, name="adamw_conv", grid=(depth,), out_shape=[jax.ShapeDtypeStruct(w.shape, F32)] * 4,
        in_specs=[pl.BlockSpec((N_DEV, CONV_HALO, LANES), lambda l: (0, base // CONV_HALO + l * CONV_W_PIECES, 0)), nat, nat, nat],
        out_specs=[nat] * 4, compiler_params=_params(("parallel",)))(landed, w, m, v)


def _unshard_cols(name, gathered, base, layer, rows, cs, shift=0):
    n_slab = -(-cs // LANES)
    total = N_DEV * cs
    tr = _pick(rows, (SLAB_TILE,))
    start = base + layer * n_slab * rows

    def body(*refs):
        slabs, o_ref = refs[:n_slab], refs[n_slab]
        for j in range(N_DEV):
            for k, s in enumerate(slabs):
                for src, dst, width in _wrapped(j * cs + LANES * k - shift, min(LANES, cs - LANES * k), total):
                    o_ref[:, dst:dst + width] = s[j, :, src:src + width]

    slab = lambda k: pl.BlockSpec((N_DEV, tr, LANES), lambda i: (0, (start + k * rows) // tr + i, 0))
    return pl.pallas_call(
        body, name=name, grid=(rows // tr,), out_shape=jax.ShapeDtypeStruct((rows, total), gathered.dtype),
        in_specs=[slab(k) for k in range(n_slab)], out_specs=pl.BlockSpec((tr, total), lambda i: (i, 0)),
        compiler_params=_params(("parallel",)))(*[gathered] * n_slab)


def _shard_cols(name, full, cs, shift=0):
    rows, total = full.shape
    n_slab = -(-cs // LANES)
    tr = _pick(rows, (SLAB_TILE,))

    def body(f_ref, o_ref):
        for j in range(N_DEV):
            for k in range(n_slab):
                used = min(LANES, cs - LANES * k)
                for src, dst, width in _wrapped(j * cs + LANES * k - shift, used, total):
                    o_ref[j, k, :, src:src + width] = f_ref[:, dst:dst + width].astype(o_ref.dtype)
                if used < LANES:
                    o_ref[j, k, :, used:] = jnp.zeros((tr, LANES - used), o_ref.dtype)

    out = pl.pallas_call(
        body, name=name, grid=(rows // tr,), out_shape=jax.ShapeDtypeStruct((N_DEV, n_slab, rows, LANES), BF16),
        in_specs=[pl.BlockSpec((tr, total), lambda i: (i, 0))],
        out_specs=pl.BlockSpec((N_DEV, n_slab, tr, LANES), lambda i: (0, 0, i, 0)),
        compiler_params=_params(("parallel",)))(full)
    return out.reshape(N_DEV, n_slab * rows, LANES)


def _wrapped(pos, width, total):
    pos %= total
    if pos + width <= total:
        return [(0, pos, width)]
    head = total - pos
    return [(0, pos, head), (head, 0, width - head)]


CONV_W_PIECES = 3


def _pad_to(n, align):
    return -(-n // align) * align


def _layout(depth):
    dims = {name: (rows, cols) for name, rows, cols, _ in SHARDED}
    col, off = {}, 0
    for name in COL_SHARDED:
        rows, cols = dims[name]
        cs = cols // N_DEV
        col[name] = (off, rows, cs)
        off += depth * -(-cs // LANES) * rows
    conv_base = off
    col_rows = _pad_to(off + depth * CONV_W_PIECES * CONV_HALO, FLAT_ROW_ALIGN)
    row, off = {}, 0
    for name in ROW_SHARDED:
        rs = dims[name][0] // N_DEV
        row[name] = (off, rs)
        off += _pad_to(depth * rs, LANES)
    return col, conv_base, col_rows, row, off


def _slabs(shard):
    rows, cs = shard.shape
    parts = []
    for k in range(-(-cs // LANES)):
        part = shard[:, LANES * k:min(LANES * (k + 1), cs)]
        parts.append(jnp.pad(part, ((0, 0), (0, LANES - part.shape[1]))))
    return jnp.concatenate(parts, axis=0)


def _concat_padded(pieces, total, axis):
    used = sum(p.shape[axis] for p in pieces)
    if total > used:
        shape = list(pieces[0].shape)
        shape[axis] = total - used
        pieces = pieces + [jnp.zeros(shape, pieces[0].dtype)]
    return jnp.concatenate(pieces, axis=axis)


def _split3(a):
    hi = a.astype(BF16)
    r1 = a - hi.astype(F32)
    mid = r1.astype(BF16)
    return hi, mid, (r1 - mid.astype(F32)).astype(BF16)


def _pack_small(arrs, lead=()):
    flat = jnp.concatenate([a.reshape(lead + (-1,)) for a in arrs], axis=-1)
    total = _pad_to(flat.shape[-1], 512 * LANES)
    flat = jnp.pad(flat, [(0, 0)] * len(lead) + [(0, total - flat.shape[-1])])
    return flat.reshape(lead + (total // LANES, LANES))


def _unpack_small(flat, shapes):
    flat = flat.reshape(-1)
    res, off = [], 0
    for s in shapes:
        n = int(np.prod(s))
        res.append(flat[off:off + n].reshape(s))
        off += n
    return res


def _small_rows(a, depth):
    n16 = depth * SSM_GROUPS
    a_re, a_im, f_re, f_im = _ssm_coeffs(a["ssm_lambda_re"].reshape(n16, SSM_STATE), a["ssm_lambda_im"].reshape(n16, SSM_STATE),
                                         a["ssm_log_dt"].reshape(n16, 1))
    rows = []
    for l in range(depth):
        sp = {k: _row(a[k][l]) for k in ("mix_norm_g", "b_gate", "attn_sinks", "ssm_d", "b_ssm_glu", "conv_dw_b",
                                         "conv_norm_g", "conv_norm_b", "ffn_norm_g", "ple_norm_g")}
        g = slice(l * SSM_GROUPS, (l + 1) * SSM_GROUPS)
        sp["a_row"] = jnp.concatenate([a_re[g].reshape(1, -1), a_im[g].reshape(1, -1)], axis=1)
        sp["f_row"] = jnp.concatenate([f_re[g].reshape(1, -1), f_im[g].reshape(1, -1)], axis=1)
        sp["bmat"], sp["cmat"] = _ssm_mats(a["ssm_b_re"][l], a["ssm_b_im"][l], a["ssm_c_re"][l], a["ssm_c_im"][l])
        rows.append(sp)
    return rows


def _local_step(a, weights, depth):
    n_seq, seq, d = a["x"].shape
    t = n_seq * seq
    inv = ROPE_THETA ** (-jnp.arange(0, ROPE_DIM, 2, dtype=F32) / ROPE_DIM)
    lane = np.arange(LANES) % HEAD_DIM
    inv_lane = jnp.where(lane < ROPE_DIM, jnp.tile(inv, LANES // (ROPE_DIM // 2)), 0.0).reshape(1, LANES)
    ctab, stab = _rope_tables(a["positions"].reshape(t), inv_lane)
    small = _small_rows(a, depth)

    x = a["x"].reshape(t, d)
    saved = []
    for l in range(depth):
        x, sv = _layer_fwd(x, a["p"][l].reshape(t, -1), weights[l], small[l], ctab, stab, n_seq)
        saved.append(sv)
    loss, dx, d_final = _loss_and_grad(x, a["loss_target"].reshape(t, d), _row(a["final_norm_g"]))
    gws, gss = [None] * depth, [None] * depth
    for l in reversed(range(depth)):
        dx, gws[l], gss[l] = _layer_bwd(dx, saved[l], weights[l], small[l], ctab, stab, n_seq)

    n16 = depth * SSM_GROUPS
    halves = lambda k, h: jnp.concatenate([gss[l][k][:, h * SSM_LANES:(h + 1) * SSM_LANES].reshape(SSM_GROUPS, SSM_STATE)
                                           for l in range(depth)], axis=0)
    dlr, dli, ddt = _ssm_coeffs_bwd(a["ssm_lambda_re"].reshape(n16, SSM_STATE), a["ssm_lambda_im"].reshape(n16, SSM_STATE),
                                    a["ssm_log_dt"].reshape(n16, 1),
                                    (halves("a_row", 0), halves("a_row", 1), halves("f_row", 0), halves("f_row", 1)))
    bc = [_ssm_mats_t(gss[l]["bmat"], gss[l]["cmat"]) for l in range(depth)]
    gsmall = {k: jnp.stack([gss[l][k].reshape(a[k].shape[1:]) for l in range(depth)])
              for k in ("mix_norm_g", "b_gate", "attn_sinks", "ssm_d", "b_ssm_glu", "conv_dw_b", "conv_norm_g", "conv_norm_b",
                        "ffn_norm_g", "ple_norm_g")}
    gsmall["ssm_lambda_re"] = dlr.reshape(a["ssm_lambda_re"].shape)
    gsmall["ssm_lambda_im"] = dli.reshape(a["ssm_lambda_im"].shape)
    gsmall["ssm_log_dt"] = ddt.reshape(a["ssm_log_dt"].shape)
    for i, k in enumerate(("ssm_b_re", "ssm_b_im", "ssm_c_re", "ssm_c_im")):
        gsmall[k] = jnp.stack([bc[l][i] for l in range(depth)])
    gsmall["final_norm_g"] = d_final.reshape(a["final_norm_g"].shape)
    return loss, dx.reshape(n_seq, seq, d), gws, gsmall


def kernel(x, p, positions, mix_norm_g, w_in, b_gate, attn_sinks, w_attn_out, ssm_lambda_re, ssm_lambda_im, ssm_log_dt, ssm_b_re, ssm_b_im, ssm_c_re, ssm_c_im, ssm_d, w_ssm_glu, b_ssm_glu, conv_dw_w, conv_dw_b, conv_norm_g, conv_norm_b, w_conv_out, w_mix_out, ffn_norm_g, w_ffn_in, w_ffn_out, w_ple_in, ple_norm_g, w_ple_gate, final_norm_g, loss_target, m_mix_norm_g, m_w_in, m_b_gate, m_attn_sinks, m_w_attn_out, m_ssm_lambda_re, m_ssm_lambda_im, m_ssm_log_dt, m_ssm_b_re, m_ssm_b_im, m_ssm_c_re, m_ssm_c_im, m_ssm_d, m_w_ssm_glu, m_b_ssm_glu, m_conv_dw_w, m_conv_dw_b, m_conv_norm_g, m_conv_norm_b, m_w_conv_out, m_w_mix_out, m_ffn_norm_g, m_w_ffn_in, m_w_ffn_out, m_w_ple_in, m_ple_norm_g, m_w_ple_gate, m_final_norm_g, v_mix_norm_g, v_w_in, v_b_gate, v_attn_sinks, v_w_attn_out, v_ssm_lambda_re, v_ssm_lambda_im, v_ssm_log_dt, v_ssm_b_re, v_ssm_b_im, v_ssm_c_re, v_ssm_c_im, v_ssm_d, v_w_ssm_glu, v_b_ssm_glu, v_conv_dw_w, v_conv_dw_b, v_conv_norm_g, v_conv_norm_b, v_w_conv_out, v_w_mix_out, v_ffn_norm_g, v_w_ffn_in, v_w_ffn_out, v_w_ple_in, v_ple_norm_g, v_w_ple_gate, v_final_norm_g):
    a = dict(locals())
    depth = w_in.shape[0]
    col, conv_base, col_rows, row, row_rows = _layout(depth)
    shift = {"w_in": Z_SPLIT}
    conv_pad = ((0, 0), (0, CONV_HALO - CONV_K), (0, LANES - CONV_WIDTH // N_DEV))

    pieces = [_slabs(a[name][l].astype(BF16)) for name in COL_SHARDED for l in range(depth)]
    pieces += [jnp.pad(jnp.stack(_split3(a["conv_dw_w"][l])), conv_pad).reshape(-1, LANES) for l in range(depth)]
    regions = [_concat_padded([a[name][l].astype(BF16) for l in range(depth)], _pad_to(depth * rs, LANES), 0)
               for name, (_, rs) in row.items()]
    slab8, row8 = _all_gather("gather_weights", [_concat_padded(pieces, col_rows, 0), jnp.concatenate(regions, axis=0)])
    weights = []
    for l in range(depth):
        w = {name: _unshard_cols("unshard_" + name, slab8, base, l, rows, cs, shift.get(name, 0))
             for name, (base, rows, cs) in col.items()}
        for name, (base, rs) in row.items():
            w[name] = row8[:, base + l * rs:base + (l + 1) * rs].reshape(N_DEV * rs, -1)
        conv = slab8[:, conv_base + l * CONV_W_PIECES * CONV_HALO:conv_base + (l + 1) * CONV_W_PIECES * CONV_HALO]
        conv = conv.reshape(N_DEV, CONV_W_PIECES, CONV_HALO, LANES)[:, :, :CONV_K, :CONV_WIDTH // N_DEV].astype(F32)
        w["conv_dw_w"] = jnp.sum(conv, axis=1).transpose(1, 0, 2).reshape(CONV_K, CONV_WIDTH)
        weights.append(w)

    loss, grad_x, gws, gsmall = _local_step(a, weights, depth)
    loss = lax.psum(loss, ("x", "y", "c"))

    pieces = [_shard_cols("shard_" + name, gws[l][name], cs, shift.get(name, 0))
              for name, (_, _, cs) in col.items() for l in range(depth)]
    for l in range(depth):
        conv = gws[l]["conv_dw_w"].reshape(CONV_K, N_DEV, CONV_WIDTH // N_DEV).transpose(1, 0, 2).astype(BF16)
        pieces.append(jnp.pad(jnp.pad(conv, conv_pad), ((0, 0), (0, (CONV_W_PIECES - 1) * CONV_HALO), (0, 0))))
    regions = [_concat_padded([gws[l][name].astype(BF16).reshape(N_DEV, rs, -1) for l in range(depth)],
                              _pad_to(depth * rs, LANES), 1) for name, (_, rs) in row.items()]
    landed_slab, landed_row = _exchange("exchange_grads", [_concat_padded(pieces, col_rows, 1), jnp.concatenate(regions, axis=1)])
    state = lambda name: (a[name], a["m_" + name], a["v_" + name])
    big = {name: _adamw_cols("adamw_" + name, landed_slab, base, *state(name)) for name, (base, _, _) in col.items()}
    big.update({name: _adamw_rows("adamw_" + name, landed_row, base, *state(name)) for name, (base, _) in row.items()})
    big["conv_dw_w"] = _adamw_conv(landed_slab, conv_base, *state("conv_dw_w"))

    shapes = [a[k].shape for k in REPLICATED]
    parts, = _all_gather("gather_small_grads", [_pack_small([gsmall[k] for k in REPLICATED])])
    small_state = [_pack_small([a[pre + k] for k in REPLICATED]) for pre in ("", "m_", "v_")]
    small = [dict(zip(REPLICATED, _unpack_small(o, shapes), strict=True))
             for o in _adamw_flat("adamw_replicated", parts, *small_state)]

    def result(kind, name):
        if name in REPLICATED:
            return small[kind][name]
        return big[name][kind]

    return (loss, grad_x, *[result(kind, n) for kind in range(4) for n in WEIGHT_ORDER])
```

```python
import functools
import math

import numpy as np
import jax
import jax.numpy as jnp
from jax import lax
from jax.experimental import pallas as pl
from jax.experimental.pallas import tpu as pltpu

F32 = jnp.float32
BF16 = jnp.bfloat16
MXU_DTYPE = jnp.bfloat16
VMEM_LIMIT_BYTES = 56 * 2 ** 20
N_DEV = 8
LANES = 128

HEAD_DIM = 64
N_Q_HEADS = 8
N_KV_HEADS = 2
GQA_GROUP = 4
BLOCK = 128
ROPE_THETA = 500000.0
ROPE_DIM = 16
Q_WIDTH = 512
KV_WIDTH = 128
SSM_WIDTH = 256
SSM_GROUP = 16
SSM_GROUPS = 16
SSM_STATE = 64
SSM_LANES = SSM_GROUPS * SSM_STATE
CONV_WIDTH = 256
CONV_K = 31
CONV_HALO = 32
EPS = 1e-6
NEG_INF = -1e30
ADAM_LR, ADAM_B1, ADAM_B2, ADAM_EPS, ADAM_WD, ADAM_STEP = 0.001, 0.9, 0.999, 1e-08, 0.01, 10

ZG_W, ZQ_W, ZKV_W, ZS_W, ZC_W = 3072, 512, 256, 256, 512
ZQ_BLK, ZKV_BLK, ZS_BLK, ZC_BLK = 3072 // 512, 3584 // 256, 3840 // 256, 4096 // 512
Z_WIDTH = 4608
Z_SPLIT = 1536

SHARDED = (("w_in", 1024, 4608, 1), ("w_attn_out", 512, 1024, 1), ("w_ssm_glu", 256, 2048, 1),
           ("conv_dw_w", 31, 256, 1), ("w_conv_out", 256, 1024, 1), ("w_mix_out", 1024, 1024, 0),
           ("w_ffn_in", 1024, 5632, 1), ("w_ffn_out", 2816, 1024, 0), ("w_ple_in", 256, 1024, 1),
           ("w_ple_gate", 1024, 1024, 0))
COL_SHARDED = ("w_in", "w_ffn_in", "w_attn_out", "w_ssm_glu", "w_conv_out", "w_ple_in")
ROW_SHARDED = ("w_ffn_out", "w_mix_out", "w_ple_gate")
SLAB_TILE = 256
FLAT_ROW_ALIGN = 1024
REPLICATED = ("mix_norm_g", "b_gate", "attn_sinks", "ssm_lambda_re", "ssm_lambda_im", "ssm_log_dt", "ssm_b_re",
              "ssm_b_im", "ssm_c_re", "ssm_c_im", "ssm_d", "b_ssm_glu", "conv_dw_b", "conv_norm_g", "conv_norm_b",
              "ffn_norm_g", "ple_norm_g", "final_norm_g")
WEIGHT_ORDER = ("mix_norm_g", "w_in", "b_gate", "attn_sinks", "w_attn_out", "ssm_lambda_re", "ssm_lambda_im",
                "ssm_log_dt", "ssm_b_re", "ssm_b_im", "ssm_c_re", "ssm_c_im", "ssm_d", "w_ssm_glu", "b_ssm_glu",
                "conv_dw_w", "conv_dw_b", "conv_norm_g", "conv_norm_b", "w_conv_out", "w_mix_out", "ffn_norm_g",
                "w_ffn_in", "w_ffn_out", "w_ple_in", "ple_norm_g", "w_ple_gate", "final_norm_g")


def _params(sem=None):
    return pltpu.CompilerParams(dimension_semantics=sem, vmem_limit_bytes=VMEM_LIMIT_BYTES)


def _pick(n, cands):
    for c in cands:
        if n % c == 0:
            return c
    return n


def _dot(a, b, dims):
    return lax.dot_general(a.astype(MXU_DTYPE), b.astype(MXU_DTYPE), (dims, ((), ())), preferred_element_type=F32)


def _dot_nn(a, b):
    return _dot(a, b, ((1,), (0,)))


def _dot_nt(a, b):
    return _dot(a, b, ((1,), (1,)))


def _dot_tn(a, b):
    return _dot(a, b, ((0,), (0,)))


@jax.custom_vjp
def _mm(x, w):
    return _dot_nn(x, w)


def _mm_f(x, w):
    return _dot_nn(x, w), (x, w)


def _mm_b(res, dy):
    x, w = res
    return _dot_nt(dy, w).astype(x.dtype), _dot_tn(x, dy).astype(w.dtype)


_mm.defvjp(_mm_f, _mm_b)


def _rms(x, g):
    return x * lax.rsqrt(jnp.mean(x * x, axis=-1, keepdims=True) + EPS) * g


ROW_TILES = (1024, 512, 256, 128)
COL_TILES = (1536, 1408, 1024, 512, 256, 128)


def _matmul_nn(name, a, b, out_dtype):
    t, k = a.shape
    n = b.shape[1]
    tm, tn = _pick(t, ROW_TILES), _pick(n, COL_TILES)

    def body(a_ref, b_ref, o_ref):
        o_ref[...] = _dot_nn(a_ref[...], b_ref[...]).astype(o_ref.dtype)

    return pl.pallas_call(
        body, name=name, grid=(t // tm, n // tn), out_shape=jax.ShapeDtypeStruct((t, n), out_dtype),
        in_specs=[pl.BlockSpec((tm, k), lambda i, j: (i, 0)), pl.BlockSpec((k, tn), lambda i, j: (0, j))],
        out_specs=pl.BlockSpec((tm, tn), lambda i, j: (i, j)),
        compiler_params=_params(("parallel", "parallel")))(a, b)


def _matmul_nt(name, a, b, out_dtype):
    t, n = a.shape
    k = b.shape[0]
    tm, tk = _pick(t, ROW_TILES[1:]), _pick(k, COL_TILES)

    def body(a_ref, b_ref, o_ref):
        o_ref[...] = _dot_nt(a_ref[...], b_ref[...]).astype(o_ref.dtype)

    return pl.pallas_call(
        body, name=name, grid=(t // tm, k // tk), out_shape=jax.ShapeDtypeStruct((t, k), out_dtype),
        in_specs=[pl.BlockSpec((tm, n), lambda i, j: (i, 0)), pl.BlockSpec((tk, n), lambda i, j: (j, 0))],
        out_specs=pl.BlockSpec((tm, tk), lambda i, j: (i, j)),
        compiler_params=_params(("parallel", "parallel")))(a, b)


def _matmul_tn(name, a, b):
    t, m = a.shape
    n = b.shape[1]
    tm, tn, tt = _pick(m, COL_TILES[1:]), _pick(n, COL_TILES), _pick(t, ROW_TILES)

    def body(a_ref, b_ref, o_ref):
        @pl.when(pl.program_id(2) == 0)
        def _():
            o_ref[...] = jnp.zeros_like(o_ref)

        o_ref[...] += _dot_tn(a_ref[...], b_ref[...])

    return pl.pallas_call(
        body, name=name, grid=(m // tm, n // tn, t // tt), out_shape=jax.ShapeDtypeStruct((m, n), F32),
        in_specs=[pl.BlockSpec((tt, tm), lambda i, j, s: (s, i)), pl.BlockSpec((tt, tn), lambda i, j, s: (s, j))],
        out_specs=pl.BlockSpec((tm, tn), lambda i, j, s: (i, j)),
        compiler_params=_params(("parallel", "parallel", "arbitrary")))(a, b)


def _ffn_in_act(hf, w_fi):
    t, k = hf.shape
    f = w_fi.shape[1] // 2
    tm, tf = _pick(t, ROW_TILES[1:]), _pick(f, COL_TILES)
    nf = f // tf

    def body(a_ref, wg_ref, wu_ref, g_ref, u_ref, act_ref):
        a = a_ref[...]
        g, u = _dot_nn(a, wg_ref[...]), _dot_nn(a, wu_ref[...])
        g_ref[...] = g.astype(g_ref.dtype)
        u_ref[...] = u.astype(u_ref.dtype)
        act_ref[...] = (jax.nn.silu(g) * u).astype(act_ref.dtype)

    out = pl.BlockSpec((tm, tf), lambda i, j: (i, j))
    return pl.pallas_call(
        body, name="ffn_in_act", grid=(t // tm, nf), out_shape=[jax.ShapeDtypeStruct((t, f), MXU_DTYPE)] * 3,
        in_specs=[pl.BlockSpec((tm, k), lambda i, j: (i, 0)), pl.BlockSpec((k, tf), lambda i, j: (0, j)),
                  pl.BlockSpec((k, tf), lambda i, j: (0, j + nf))],
        out_specs=[out, out, out], compiler_params=_params(("parallel", "parallel")))(hf, w_fi, w_fi)


def _ffn_mid_bwd(dffn, w_fo, gate, up):
    t, d = dffn.shape
    f = w_fo.shape[0]
    tm, tf = _pick(t, ROW_TILES[1:]), _pick(f, COL_TILES)

    def body(a_ref, w_ref, g_ref, u_ref, dg_ref, du_ref):
        dact = _dot_nt(a_ref[...], w_ref[...])
        g, u = g_ref[...].astype(F32), u_ref[...].astype(F32)
        sg = jax.nn.sigmoid(g)
        dg_ref[...] = (dact * u * sg * (1.0 + g * (1.0 - sg))).astype(dg_ref.dtype)
        du_ref[...] = (dact * g * sg).astype(du_ref.dtype)

    blk = pl.BlockSpec((tm, tf), lambda i, j: (i, j))
    return pl.pallas_call(
        body, name="ffn_mid_bwd", grid=(t // tm, f // tf), out_shape=[jax.ShapeDtypeStruct((t, f), MXU_DTYPE)] * 2,
        in_specs=[pl.BlockSpec((tm, d), lambda i, j: (i, 0)), pl.BlockSpec((tf, d), lambda i, j: (j, 0)), blk, blk],
        out_specs=[blk, blk], compiler_params=_params(("parallel", "parallel")))(dffn, w_fo, gate, up)


def _ffn_in_dx(dgate, dup, w_fi):
    t, f = dgate.shape
    d = w_fi.shape[0]
    tm = _pick(t, ROW_TILES[1:])

    def body(g_ref, u_ref, w_ref, o_ref):
        o_ref[...] = _dot_nt(g_ref[...], w_ref[:, :f]) + _dot_nt(u_ref[...], w_ref[:, f:])

    blk = pl.BlockSpec((tm, f), lambda i: (i, 0))
    return pl.pallas_call(
        body, name="ffn_in_dx", grid=(t // tm,), out_shape=jax.ShapeDtypeStruct((t, d), F32),
        in_specs=[blk, blk, pl.BlockSpec(w_fi.shape, lambda i: (0, 0))], out_specs=pl.BlockSpec((tm, d), lambda i: (i, 0)),
        compiler_params=_params(("parallel",)))(dgate, dup, w_fi)


def _token_call(name, fn, tile, tok_ins, consts, tok_outs, acc_outs):
    n_rows = tok_ins[0][0].shape[0]
    tile = min(tile, n_rows)
    n_ti, n_c, n_to = len(tok_ins), len(consts), len(tok_outs)

    def body(*refs):
        ins = [r[...] for r in refs[:n_ti + n_c]]
        outs, accs = fn(*ins)
        for r, v in zip(refs[n_ti + n_c:n_ti + n_c + n_to], outs, strict=True):
            r[...] = v.astype(r.dtype)
        first = pl.program_id(0) == 0
        for r, v in zip(refs[n_ti + n_c + n_to:], accs, strict=True):
            @pl.when(first)
            def _(r=r):
                r[...] = jnp.zeros_like(r)

            r[...] += jnp.broadcast_to(v, r.shape).astype(F32)

    in_specs = [pl.BlockSpec((tile, w), functools.partial(lambda i, c: (i, c), c=cb)) for _, w, cb in tok_ins]
    in_specs += [pl.BlockSpec(c.shape, lambda i: (0, 0)) for c in consts]
    out_shape = [jax.ShapeDtypeStruct((n_rows, w), dt) for w, dt in tok_outs]
    out_shape += [jax.ShapeDtypeStruct(s, F32) for s in acc_outs]
    out_specs = [pl.BlockSpec((tile, w), lambda i: (i, 0)) for w, _ in tok_outs]
    out_specs += [pl.BlockSpec(s, lambda i: (0, 0)) for s in acc_outs]
    res = pl.pallas_call(
        body, name=name, grid=(n_rows // tile,), out_shape=out_shape, in_specs=in_specs, out_specs=out_specs,
        compiler_params=_params(("arbitrary",)))(*[a for a, _, _ in tok_ins], *consts)
    return res[:n_to], res[n_to:]


def _whole(a):
    return (a, a.shape[1], 0)


def _norm_in_tile(x, g):
    return _rms(x, g)


def _merge_tile(x, ya, ys, uc, gin, w_ao, w_sg, b_sg, w_co, b_gate, w_mo, g_ffn):
    d = x.shape[1]
    y_attn = _mm(ya, w_ao)
    pre = _mm(jax.nn.gelu(ys), w_sg) + b_sg
    y_ssm = pre[:, :d] * jax.nn.sigmoid(pre[:, d:])
    y_conv = _mm(uc, w_co)
    gates = jax.nn.sigmoid(gin + b_gate)
    merged = gates[:, :d] * y_attn + gates[:, d:2 * d] * y_ssm + gates[:, 2 * d:] * y_conv
    x1 = x + _mm(merged, w_mo)
    return x1, _rms(x1, g_ffn)


def _ple_tile(x2, p, w_pi, g_ple, w_pg):
    return x2 + jax.nn.sigmoid(_mm(_rms(x2, g_ple), w_pg)) * _mm(p, w_pi)


def _f32s(vals):
    return [v.astype(F32) for v in vals]


def _rope_tables(positions, inv_lane):
    def fn(pos, inv):
        ang = pos.astype(F32) * inv
        j = lax.broadcasted_iota(jnp.int32, ang.shape, 1) % HEAD_DIM
        c = jnp.where(j < ROPE_DIM, jnp.cos(ang), 1.0)
        s = jnp.sin(ang)
        s = jnp.where(j < ROPE_DIM // 2, -s, jnp.where(j < ROPE_DIM, s, 0.0))
        return [c, s], []

    (c, s), _ = _token_call("rope_tables", fn, 1024, [_whole(positions.reshape(-1, 1))], [inv_lane],
                            [(LANES, F32), (LANES, F32)], [])
    return c, s


def _swap_halves(t):
    n = t.shape[1]
    j = lax.broadcasted_iota(jnp.int32, t.shape, 1) % HEAD_DIM
    lower = pltpu.roll(t, n - ROPE_DIM // 2, 1)
    upper = jnp.where(j < ROPE_DIM, pltpu.roll(t, ROPE_DIM // 2, 1), 0.0)
    return jnp.where(j < ROPE_DIM // 2, lower, upper)


def _rope(t, c, s):
    return t * c + _swap_halves(t) * s


def _rope_t(dt, c, s):
    return dt * c + _swap_halves(dt * s)


def _tile4(a):
    return jnp.concatenate([a] * (Q_WIDTH // LANES), axis=1)


def _attn_mask(n):
    qi = lax.broadcasted_iota(jnp.int32, (GQA_GROUP * BLOCK, 2 * BLOCK), 0) % BLOCK
    kj = lax.broadcasted_iota(jnp.int32, (GQA_GROUP * BLOCK, 2 * BLOCK), 1)
    dist = qi + BLOCK - kj
    return (dist >= 0) & (dist < BLOCK) & ((n > 0) | (kj >= BLOCK))


def _attn_specs(nb):
    row = lambda b, n: b * nb + n
    prev = lambda b, n: b * nb + jnp.maximum(n - 1, 0)
    return [pl.BlockSpec((BLOCK, ZQ_W), lambda b, n: (row(b, n), ZQ_BLK)),
            pl.BlockSpec((BLOCK, ZKV_W), lambda b, n: (row(b, n), ZKV_BLK)),
            pl.BlockSpec((BLOCK, ZKV_W), lambda b, n: (prev(b, n), ZKV_BLK)),
            pl.BlockSpec((BLOCK, LANES), lambda b, n: (row(b, n), 0)),
            pl.BlockSpec((BLOCK, LANES), lambda b, n: (row(b, n), 0)),
            pl.BlockSpec((BLOCK, LANES), lambda b, n: (prev(b, n), 0)),
            pl.BlockSpec((BLOCK, LANES), lambda b, n: (prev(b, n), 0)),
            pl.BlockSpec((1, N_Q_HEADS), lambda b, n: (0, 0))]


ATTN_SCALE = HEAD_DIM ** -0.5


def _stack_heads(t, kh):
    return jnp.concatenate([t[:, (kh * GQA_GROUP + g) * HEAD_DIM:(kh * GQA_GROUP + g + 1) * HEAD_DIM]
                            for g in range(GQA_GROUP)], axis=0)


def _stack_sinks(sink, kh):
    return jnp.concatenate([jnp.broadcast_to(sink[:, kh * GQA_GROUP + g:kh * GQA_GROUP + g + 1], (BLOCK, 1))
                            for g in range(GQA_GROUP)], axis=0)


def _attn_band(q_ref, kv_ref, kvp_ref, c_ref, s_ref, cp_ref, sp_ref):
    c, s = c_ref[...], s_ref[...]
    q = _rope(q_ref[...], _tile4(c), _tile4(s)) * ATTN_SCALE
    kv, kvp = kv_ref[...], kvp_ref[...]
    k = _rope(kv[:, :KV_WIDTH], c, s)
    kp = _rope(kvp[:, :KV_WIDTH], cp_ref[...], sp_ref[...])
    kb = jnp.concatenate([kp, k], axis=0)
    vb = jnp.concatenate([kvp[:, KV_WIDTH:], kv[:, KV_WIDTH:]], axis=0)
    return q, kb, vb


def _attention_fwd(z, ctab, stab, sinks, n_seq):
    t = z.shape[0]
    nb = t // n_seq // BLOCK

    def body(q_ref, kv_ref, kvp_ref, c_ref, s_ref, cp_ref, sp_ref, sink_ref, o_ref, lse_ref):
        q, kb, vb = _attn_band(q_ref, kv_ref, kvp_ref, c_ref, s_ref, cp_ref, sp_ref)
        mask = _attn_mask(pl.program_id(1))
        sink = sink_ref[...]
        lane = lax.broadcasted_iota(jnp.int32, (BLOCK, N_Q_HEADS), 1)
        lse_all = jnp.zeros((BLOCK, N_Q_HEADS), F32)
        for kh in range(N_KV_HEADS):
            sc = jnp.where(mask, _dot_nt(_stack_heads(q, kh), kb[:, kh * HEAD_DIM:(kh + 1) * HEAD_DIM]), NEG_INF)
            sk = _stack_sinks(sink, kh)
            m = jnp.maximum(jnp.max(sc, axis=-1, keepdims=True), sk)
            pr = jnp.exp(sc - m)
            den = jnp.sum(pr, axis=-1, keepdims=True) + jnp.exp(sk - m)
            out = _dot_nn(pr * (1.0 / den), vb[:, kh * HEAD_DIM:(kh + 1) * HEAD_DIM])
            lse = m + jnp.log(den)
            for g in range(GQA_GROUP):
                h = kh * GQA_GROUP + g
                o_ref[:, h * HEAD_DIM:(h + 1) * HEAD_DIM] = out[g * BLOCK:(g + 1) * BLOCK].astype(o_ref.dtype)
                lse_all = jnp.where(lane == h, lse[g * BLOCK:(g + 1) * BLOCK], lse_all)
        lse_ref[...] = lse_all

    return pl.pallas_call(
        body, name="attn_fwd", grid=(n_seq, nb),
        out_shape=[jax.ShapeDtypeStruct((t, Q_WIDTH), MXU_DTYPE), jax.ShapeDtypeStruct((t, N_Q_HEADS), F32)],
        in_specs=_attn_specs(nb),
        out_specs=[pl.BlockSpec((BLOCK, Q_WIDTH), lambda b, n: (b * nb + n, 0)),
                   pl.BlockSpec((BLOCK, N_Q_HEADS), lambda b, n: (b * nb + n, 0))],
        compiler_params=_params(("parallel", "parallel")))(z, z, z, ctab, stab, ctab, stab, sinks)


def _attention_bwd(z, ctab, stab, sinks, ya, lse, dya, n_seq):
    t = z.shape[0]
    nb = t // n_seq // BLOCK

    def body(q_ref, kv_ref, kvp_ref, c_ref, s_ref, cp_ref, sp_ref, sink_ref, o_ref, lse_ref, do_ref,
             dq_ref, dkv_ref, dkvp_ref, dsink_ref):
        q, kb, vb = _attn_band(q_ref, kv_ref, kvp_ref, c_ref, s_ref, cp_ref, sp_ref)
        mask = _attn_mask(pl.program_id(1))
        sink = sink_ref[...]
        lse_all = lse_ref[...]
        o = o_ref[...].astype(F32)
        do = do_ref[...].astype(F32)
        lane = lax.broadcasted_iota(jnp.int32, (1, N_Q_HEADS), 1)
        dsink = jnp.zeros((1, N_Q_HEADS), F32)
        dq_parts = []
        dk_parts, dv_parts = [], []
        for kh in range(N_KV_HEADS):
            kbh = kb[:, kh * HEAD_DIM:(kh + 1) * HEAD_DIM]
            vbh = vb[:, kh * HEAD_DIM:(kh + 1) * HEAD_DIM]
            qs, dos = _stack_heads(q, kh), _stack_heads(do, kh)
            lse = jnp.concatenate([lse_all[:, kh * GQA_GROUP + g:kh * GQA_GROUP + g + 1] for g in range(GQA_GROUP)], axis=0)
            pr = jnp.exp(jnp.where(mask, _dot_nt(qs, kbh), NEG_INF) - lse)
            delta = jnp.sum(dos * _stack_heads(o, kh), axis=-1, keepdims=True)
            ds = pr * (_dot_nt(dos, vbh) - delta)
            dqs = _dot_nn(ds, kbh)
            dq_parts += [dqs[g * BLOCK:(g + 1) * BLOCK] for g in range(GQA_GROUP)]
            dk_parts.append(_dot_tn(ds, qs))
            dv_parts.append(_dot_tn(pr, dos))
            dsk = jnp.exp(_stack_sinks(sink, kh) - lse) * delta
            for g in range(GQA_GROUP):
                dsink = dsink + jnp.where(lane == kh * GQA_GROUP + g, -jnp.sum(dsk[g * BLOCK:(g + 1) * BLOCK]), 0.0)
        c, s = c_ref[...], s_ref[...]
        dq_ref[...] = _rope_t(jnp.concatenate(dq_parts, axis=1) * ATTN_SCALE, _tile4(c), _tile4(s)).astype(dq_ref.dtype)
        dk = jnp.concatenate(dk_parts, axis=1)
        dv = jnp.concatenate(dv_parts, axis=1)
        dkv_ref[:, :KV_WIDTH] = _rope_t(dk[BLOCK:], c, s)
        dkv_ref[:, KV_WIDTH:] = dv[BLOCK:]
        dkvp_ref[:, :KV_WIDTH] = _rope_t(dk[:BLOCK], cp_ref[...], sp_ref[...])
        dkvp_ref[:, KV_WIDTH:] = dv[:BLOCK]

        @pl.when((pl.program_id(0) == 0) & (pl.program_id(1) == 0))
        def _():
            dsink_ref[...] = jnp.zeros_like(dsink_ref)

        dsink_ref[...] += dsink

    row_spec = lambda w: pl.BlockSpec((BLOCK, w), lambda b, n: (b * nb + n, 0))
    return pl.pallas_call(
        body, name="attn_bwd", grid=(n_seq, nb),
        out_shape=[jax.ShapeDtypeStruct((t, Q_WIDTH), MXU_DTYPE), jax.ShapeDtypeStruct((t, ZKV_W), F32),
                   jax.ShapeDtypeStruct((t, ZKV_W), F32), jax.ShapeDtypeStruct((1, N_Q_HEADS), F32)],
        in_specs=_attn_specs(nb) + [row_spec(Q_WIDTH), row_spec(N_Q_HEADS), row_spec(Q_WIDTH)],
        out_specs=[row_spec(Q_WIDTH), row_spec(ZKV_W), row_spec(ZKV_W), pl.BlockSpec((1, N_Q_HEADS), lambda b, n: (0, 0))],
        compiler_params=_params(("arbitrary", "arbitrary")))(z, z, z, ctab, stab, ctab, stab, sinks, ya, lse, dya)


def _assemble_dz(dgin, dq, dkv, dkvp, ds, dc, n_seq):
    t = dq.shape[0]
    nb = t // n_seq // BLOCK

    def body(dg_ref, dq_ref, dkv_ref, dkvn_ref, ds_ref, dc_ref, o_ref):
        last = pl.program_id(1) == nb - 1
        o_ref[:, :ZG_W] = dg_ref[...]
        o_ref[:, ZG_W:ZG_W + ZQ_W] = dq_ref[...]
        dkv_sum = dkv_ref[...] + jnp.where(last, 0.0, dkvn_ref[...])
        o_ref[:, ZG_W + ZQ_W:ZG_W + ZQ_W + ZKV_W] = dkv_sum.astype(o_ref.dtype)
        o_ref[:, ZG_W + ZQ_W + ZKV_W:ZG_W + ZQ_W + ZKV_W + ZS_W] = ds_ref[...]
        o_ref[:, ZG_W + ZQ_W + ZKV_W + ZS_W:] = dc_ref[...]

    row_spec = lambda w: pl.BlockSpec((BLOCK, w), lambda b, n: (b * nb + n, 0))
    nxt = pl.BlockSpec((BLOCK, ZKV_W), lambda b, n: (b * nb + jnp.minimum(n + 1, nb - 1), 0))
    return pl.pallas_call(
        body, name="assemble_dz", grid=(n_seq, nb), out_shape=jax.ShapeDtypeStruct((t, Z_WIDTH), MXU_DTYPE),
        in_specs=[row_spec(ZG_W), row_spec(ZQ_W), row_spec(ZKV_W), nxt, row_spec(ZS_W), row_spec(ZC_W)],
        out_specs=row_spec(Z_WIDTH), compiler_params=_params(("parallel", "parallel")))(dgin, dq, dkv, dkvp, ds, dc)


def _ssm_coeff_tile(lam_re, lam_im, log_dt):
    lr = jnp.minimum(lam_re, -1e-4)
    dt = jnp.exp(log_dt)
    mag = jnp.exp(lr * dt)
    a_re = mag * jnp.cos(lam_im * dt)
    a_im = mag * jnp.sin(lam_im * dt)
    den = lr * lr + lam_im * lam_im
    x_re = a_re - 1.0
    f_re = (x_re * lr + a_im * lam_im) / den
    f_im = (a_im * lr - x_re * lam_im) / den
    return a_re, a_im, f_re, f_im


def _ssm_coeffs(lam_re, lam_im, log_dt):
    def body(lr_ref, li_ref, dt_ref, *o_refs):
        for r, v in zip(o_refs, _ssm_coeff_tile(lr_ref[...], li_ref[...], dt_ref[...]), strict=True):
            r[...] = v

    return pl.pallas_call(body, name="ssm_coeffs", out_shape=[jax.ShapeDtypeStruct(lam_re.shape, F32)] * 4)(
        lam_re, lam_im, log_dt)


def _ssm_coeffs_bwd(lam_re, lam_im, log_dt, cts):
    def body(lr_ref, li_ref, dt_ref, c0, c1, c2, c3, dlr_ref, dli_ref, ddt_ref):
        _, vjp = jax.vjp(_ssm_coeff_tile, lr_ref[...], li_ref[...], dt_ref[...])
        dlr, dli, ddt = vjp((c0[...], c1[...], c2[...], c3[...]))
        dlr_ref[...] = dlr
        dli_ref[...] = dli
        ddt_ref[...] = ddt

    return pl.pallas_call(
        body, name="ssm_coeffs_bwd",
        out_shape=[jax.ShapeDtypeStruct(lam_re.shape, F32)] * 2 + [jax.ShapeDtypeStruct(log_dt.shape, F32)])(
        lam_re, lam_im, log_dt, *cts)


def _ssm_chunk(t):
    return _pick(t, (256, 128))


def _ssm_fwd(z, bmat, a_row, f_row, cmat, d_row, n_seq):
    t = z.shape[0]
    seq = t // n_seq
    lc = _ssm_chunk(seq)
    nc = seq // lc
    n2 = 2 * SSM_LANES

    def body(u_ref, b_ref, a_ref, f_ref, c_ref, d_ref, y_ref, s_ref, bu_ref, st_ref):
        @pl.when(pl.program_id(1) == 0)
        def _():
            st_ref[...] = jnp.zeros_like(st_ref)

        u = u_ref[...]
        proj = _dot_nn(u, b_ref[...])
        fr, fi = f_ref[:, :SSM_LANES], f_ref[:, SSM_LANES:]
        pr, pi = proj[:, :SSM_LANES], proj[:, SSM_LANES:]
        bu_ref[:, :SSM_LANES] = fr * pr - fi * pi
        bu_ref[:, SSM_LANES:] = fr * pi + fi * pr
        ar, ai = a_ref[:, :SSM_LANES], a_ref[:, SSM_LANES:]

        def step(i, carry):
            sr, si = carry
            nr = ar * sr - ai * si + bu_ref[pl.ds(i, 1), pl.ds(0, SSM_LANES)]
            ni = ar * si + ai * sr + bu_ref[pl.ds(i, 1), pl.ds(SSM_LANES, SSM_LANES)]
            s_ref[pl.ds(i, 1), pl.ds(0, SSM_LANES)] = nr
            s_ref[pl.ds(i, 1), pl.ds(SSM_LANES, SSM_LANES)] = ni
            return nr, ni

        sr, si = lax.fori_loop(0, lc, step, (st_ref[0:1, :SSM_LANES], st_ref[0:1, SSM_LANES:]), unroll=8)
        st_ref[0:1, :SSM_LANES] = sr
        st_ref[0:1, SSM_LANES:] = si
        y_ref[...] = _dot_nn(s_ref[...], c_ref[...]) + d_ref[...] * u

    const = lambda shape: pl.BlockSpec(shape, lambda b, c: (0, 0))
    return pl.pallas_call(
        body, name="ssm_fwd", grid=(n_seq, nc),
        out_shape=[jax.ShapeDtypeStruct((t, SSM_WIDTH), F32), jax.ShapeDtypeStruct((t, n2), F32)],
        in_specs=[pl.BlockSpec((lc, ZS_W), lambda b, c: (b * nc + c, ZS_BLK)), const((SSM_WIDTH, n2)), const((1, n2)),
                  const((1, n2)), const((n2, SSM_WIDTH)), const((1, SSM_WIDTH))],
        out_specs=[pl.BlockSpec((lc, SSM_WIDTH), lambda b, c: (b * nc + c, 0)),
                   pl.BlockSpec((lc, n2), lambda b, c: (b * nc + c, 0))],
        scratch_shapes=[pltpu.VMEM((lc, n2), F32), pltpu.VMEM((8, n2), F32)],
        compiler_params=_params(("arbitrary", "arbitrary")))(z, bmat, a_row, f_row, cmat, d_row)


def _ssm_bwd(z, states, dy, bmat, a_row, f_row, cmat, d_row, n_seq):
    t = z.shape[0]
    seq = t // n_seq
    lc = _ssm_chunk(seq)
    nc = seq // lc
    n2 = 2 * SSM_LANES

    def body(dy_ref, u_ref, s_ref, b_ref, a_ref, f_ref, c_ref, d_ref,
             du_ref, db_ref, dc_ref, da_ref, df_ref, dd_ref, g_ref, carry_ref):
        @pl.when((pl.program_id(0) == 0) & (pl.program_id(1) == 0))
        def _():
            for r in (db_ref, dc_ref, da_ref, df_ref, dd_ref):
                r[...] = jnp.zeros_like(r)

        @pl.when(pl.program_id(1) == 0)
        def _():
            carry_ref[...] = jnp.zeros_like(carry_ref)

        dy, u, st = dy_ref[...], u_ref[...], s_ref[...]
        g_ref[0:lc, :] = _dot_nt(dy, c_ref[...])
        g_ref[lc:lc + 8, :] = carry_ref[...]
        dc_ref[...] += _dot_tn(st, dy)
        dd_ref[...] += jnp.sum(dy * u, axis=0, keepdims=True)
        ar, ai = a_ref[:, :SSM_LANES], a_ref[:, SSM_LANES:]

        def step(i, carry):
            gr, gi = carry
            r = lc - 1 - i
            nr = g_ref[pl.ds(r, 1), pl.ds(0, SSM_LANES)] + ar * gr + ai * gi
            ni = g_ref[pl.ds(r, 1), pl.ds(SSM_LANES, SSM_LANES)] - ai * gr + ar * gi
            g_ref[pl.ds(r, 1), pl.ds(0, SSM_LANES)] = nr
            g_ref[pl.ds(r, 1), pl.ds(SSM_LANES, SSM_LANES)] = ni
            return nr, ni

        gr, gi = lax.fori_loop(0, lc, step, (carry_ref[0:1, :SSM_LANES], carry_ref[0:1, SSM_LANES:]), unroll=8)
        carry_ref[0:1, :SSM_LANES] = gr
        carry_ref[0:1, SSM_LANES:] = gi
        sr, si = st[:, :SSM_LANES], st[:, SSM_LANES:]
        gnr, gni = g_ref[pl.ds(1, lc), pl.ds(0, SSM_LANES)], g_ref[pl.ds(1, lc), pl.ds(SSM_LANES, SSM_LANES)]
        da_ref[:, :SSM_LANES] += jnp.sum(gnr * sr + gni * si, axis=0, keepdims=True)
        da_ref[:, SSM_LANES:] += jnp.sum(gni * sr - gnr * si, axis=0, keepdims=True)
        gr_all, gi_all = g_ref[0:lc, :SSM_LANES], g_ref[0:lc, SSM_LANES:]
        proj = _dot_nn(u, b_ref[...])
        pr, pi = proj[:, :SSM_LANES], proj[:, SSM_LANES:]
        df_ref[:, :SSM_LANES] += jnp.sum(gr_all * pr + gi_all * pi, axis=0, keepdims=True)
        df_ref[:, SSM_LANES:] += jnp.sum(gi_all * pr - gr_all * pi, axis=0, keepdims=True)
        fr, fi = f_ref[:, :SSM_LANES], f_ref[:, SSM_LANES:]
        dproj = jnp.concatenate([fr * gr_all + fi * gi_all, fr * gi_all - fi * gr_all], axis=1).astype(MXU_DTYPE)
        du_ref[...] = (_dot_nt(dproj, b_ref[...]) + d_ref[...] * dy).astype(du_ref.dtype)
        db_ref[...] += _dot_tn(u, dproj)

    const = lambda shape: pl.BlockSpec(shape, lambda b, c: (0, 0))
    rows = lambda w, cb: pl.BlockSpec((lc, w), functools.partial(lambda b, c, cb: (b * nc + nc - 1 - c, cb), cb=cb))
    return pl.pallas_call(
        body, name="ssm_bwd", grid=(n_seq, nc),
        out_shape=[jax.ShapeDtypeStruct((t, SSM_WIDTH), MXU_DTYPE), jax.ShapeDtypeStruct((SSM_WIDTH, n2), F32),
                   jax.ShapeDtypeStruct((n2, SSM_WIDTH), F32), jax.ShapeDtypeStruct((1, n2), F32),
                   jax.ShapeDtypeStruct((1, n2), F32), jax.ShapeDtypeStruct((1, SSM_WIDTH), F32)],
        in_specs=[rows(SSM_WIDTH, 0), rows(ZS_W, ZS_BLK), rows(n2, 0), const((SSM_WIDTH, n2)), const((1, n2)),
                  const((1, n2)), const((n2, SSM_WIDTH)), const((1, SSM_WIDTH))],
        out_specs=[rows(SSM_WIDTH, 0), const((SSM_WIDTH, n2)), const((n2, SSM_WIDTH)), const((1, n2)), const((1, n2)),
                   const((1, SSM_WIDTH))],
        scratch_shapes=[pltpu.VMEM((lc + 8, n2), F32), pltpu.VMEM((8, n2), F32)],
        compiler_params=_params(("arbitrary", "arbitrary")))(dy, z, states, bmat, a_row, f_row, cmat, d_row)


def _conv_chunk(t):
    return _pick(t, (512, 256, 128))


def _glu(c):
    return c[:, :CONV_WIDTH] * jax.nn.sigmoid(c[:, CONV_WIDTH:])


def _conv_post_tile(v, g, b):
    mu = jnp.mean(v, axis=-1, keepdims=True)
    var = jnp.mean(jnp.square(v - mu), axis=-1, keepdims=True)
    return jax.nn.silu((v - mu) * lax.rsqrt(var + EPS) * g + b)


def _conv_specs(lc, nc):
    per = lc // CONV_HALO
    return [pl.BlockSpec((lc, ZC_W), lambda b, c: (b * nc + c, ZC_BLK)),
            pl.BlockSpec((CONV_HALO, ZC_W), lambda b, c: (jnp.maximum((b * nc + c) * per - 1, 0), ZC_BLK))]


def _conv_fill(c_ref, cp_ref, ue_ref, lc):
    ue_ref[0:CONV_HALO, :] = jnp.where(pl.program_id(1) > 0, _glu(cp_ref[...]), 0.0)
    ue_ref[CONV_HALO:CONV_HALO + lc, :] = _glu(c_ref[...])


def _conv_apply(ue_ref, w_ref, b_ref, lc):
    acc = jnp.zeros((lc, CONV_WIDTH), F32) + b_ref[...]
    for k in range(CONV_K):
        acc = acc + w_ref[k:k + 1, :] * ue_ref[pl.ds(k + CONV_HALO - CONV_K + 1, lc), :]
    return acc


def _conv_fwd(z, dw_w, dw_b, ln_g, ln_b, n_seq):
    t = z.shape[0]
    seq = t // n_seq
    lc = _conv_chunk(seq)
    nc = seq // lc

    def body(c_ref, cp_ref, w_ref, b_ref, g_ref, lb_ref, o_ref, ue_ref):
        _conv_fill(c_ref, cp_ref, ue_ref, lc)
        o_ref[...] = _conv_post_tile(_conv_apply(ue_ref, w_ref, b_ref, lc), g_ref[...], lb_ref[...]).astype(o_ref.dtype)

    const = lambda a: pl.BlockSpec(a.shape, lambda b, c: (0, 0))
    return pl.pallas_call(
        body, name="conv_fwd", grid=(n_seq, nc), out_shape=jax.ShapeDtypeStruct((t, CONV_WIDTH), MXU_DTYPE),
        in_specs=_conv_specs(lc, nc) + [const(dw_w), const(dw_b), const(ln_g), const(ln_b)],
        out_specs=pl.BlockSpec((lc, CONV_WIDTH), lambda b, c: (b * nc + c, 0)),
        scratch_shapes=[pltpu.VMEM((CONV_HALO + lc, CONV_WIDTH), F32)],
        compiler_params=_params(("parallel", "parallel")))(z, z, dw_w, dw_b, ln_g, ln_b)


def _conv_bwd_post(z, duc, dw_w, dw_b, ln_g, ln_b, n_seq):
    t = z.shape[0]
    seq = t // n_seq
    lc = _conv_chunk(seq)
    nc = seq // lc

    def body(c_ref, cp_ref, duc_ref, w_ref, b_ref, g_ref, lb_ref, dv_ref, dg_ref, dlb_ref, db_ref, ue_ref):
        @pl.when((pl.program_id(0) == 0) & (pl.program_id(1) == 0))
        def _():
            for r in (dg_ref, dlb_ref, db_ref):
                r[...] = jnp.zeros_like(r)

        _conv_fill(c_ref, cp_ref, ue_ref, lc)
        _, vjp = jax.vjp(_conv_post_tile, _conv_apply(ue_ref, w_ref, b_ref, lc), g_ref[...], lb_ref[...])
        dv, dg, dlb = vjp(duc_ref[...])
        dv_ref[...] = dv
        dg_ref[...] += dg
        dlb_ref[...] += dlb
        db_ref[...] += jnp.sum(dv, axis=0, keepdims=True)

    const = lambda a: pl.BlockSpec(a.shape, lambda b, c: (0, 0))
    vec = jax.ShapeDtypeStruct((1, CONV_WIDTH), F32)
    return pl.pallas_call(
        body, name="conv_bwd_post", grid=(n_seq, nc), out_shape=[jax.ShapeDtypeStruct((t, CONV_WIDTH), F32), vec, vec, vec],
        in_specs=_conv_specs(lc, nc) + [pl.BlockSpec((lc, CONV_WIDTH), lambda b, c: (b * nc + c, 0)),
                                       const(dw_w), const(dw_b), const(ln_g), const(ln_b)],
        out_specs=[pl.BlockSpec((lc, CONV_WIDTH), lambda b, c: (b * nc + c, 0))] + [const(dw_b)] * 3,
        scratch_shapes=[pltpu.VMEM((CONV_HALO + lc, CONV_WIDTH), F32)],
        compiler_params=_params(("arbitrary", "arbitrary")))(z, z, duc, dw_w, dw_b, ln_g, ln_b)


def _conv_bwd_taps(z, dv, dw_w, n_seq):
    t = z.shape[0]
    seq = t // n_seq
    lc = _conv_chunk(seq)
    nc = seq // lc
    per = lc // CONV_HALO
    n_halo = t // CONV_HALO

    def body(c_ref, cp_ref, dv_ref, dvn_ref, w_ref, dc_ref, dw_ref, ue_ref, dve_ref):
        @pl.when((pl.program_id(0) == 0) & (pl.program_id(1) == 0))
        def _():
            dw_ref[...] = jnp.zeros_like(dw_ref)

        _conv_fill(c_ref, cp_ref, ue_ref, lc)
        dv = dv_ref[...]
        dve_ref[0:lc, :] = dv
        dve_ref[lc:lc + CONV_HALO, :] = jnp.where(pl.program_id(1) < nc - 1, dvn_ref[...], 0.0)
        du = jnp.zeros((lc, CONV_WIDTH), F32)
        for k in range(CONV_K):
            du = du + w_ref[k:k + 1, :] * dve_ref[pl.ds(CONV_K - 1 - k, lc), :]
            dw_ref[k:k + 1, :] += jnp.sum(dv * ue_ref[pl.ds(k + CONV_HALO - CONV_K + 1, lc), :], axis=0, keepdims=True)
        c = c_ref[...]
        a, sg = c[:, :CONV_WIDTH], jax.nn.sigmoid(c[:, CONV_WIDTH:])
        dc_ref[:, :CONV_WIDTH] = (du * sg).astype(dc_ref.dtype)
        dc_ref[:, CONV_WIDTH:] = (du * a * sg * (1.0 - sg)).astype(dc_ref.dtype)

    return pl.pallas_call(
        body, name="conv_bwd_taps", grid=(n_seq, nc),
        out_shape=[jax.ShapeDtypeStruct((t, ZC_W), MXU_DTYPE), jax.ShapeDtypeStruct((CONV_HALO, CONV_WIDTH), F32)],
        in_specs=_conv_specs(lc, nc) + [
            pl.BlockSpec((lc, CONV_WIDTH), lambda b, c: (b * nc + c, 0)),
            pl.BlockSpec((CONV_HALO, CONV_WIDTH), lambda b, c: (jnp.minimum((b * nc + c + 1) * per, n_halo - 1), 0)),
            pl.BlockSpec(dw_w.shape, lambda b, c: (0, 0))],
        out_specs=[pl.BlockSpec((lc, ZC_W), lambda b, c: (b * nc + c, 0)),
                   pl.BlockSpec((CONV_HALO, CONV_WIDTH), lambda b, c: (0, 0))],
        scratch_shapes=[pltpu.VMEM((CONV_HALO + lc, CONV_WIDTH), F32), pltpu.VMEM((lc + CONV_HALO, CONV_WIDTH), F32)],
        compiler_params=_params(("arbitrary", "arbitrary")))(z, z, dv, dv, dw_w)


def _row(v):
    return v.reshape(1, -1)


def _ssm_mats(b_re, b_im, c_re, c_im):
    eye = jnp.eye(SSM_GROUPS, dtype=bool)
    bm = jnp.stack([b_re, b_im]).transpose(1, 3, 0, 2)[:, :, :, None, :]
    bmat = jnp.where(eye[:, None, None, :, None], bm, 0.0).reshape(SSM_WIDTH, 2 * SSM_LANES)
    cm = jnp.stack([c_re, -c_im]).transpose(0, 1, 3, 2)[:, :, :, None, :]
    cmat = jnp.where(eye[None, :, None, :, None], cm, 0.0).reshape(2 * SSM_LANES, SSM_WIDTH)
    return bmat.astype(MXU_DTYPE), cmat.astype(MXU_DTYPE)


def _ssm_mats_t(dbmat, dcmat):
    eye = jnp.eye(SSM_GROUPS, dtype=bool)
    db = dbmat.reshape(SSM_GROUPS, SSM_GROUP, 2, SSM_GROUPS, SSM_STATE)
    db = jnp.sum(jnp.where(eye[:, None, None, :, None], db, 0.0), axis=3).transpose(2, 0, 3, 1)
    dc = dcmat.reshape(2, SSM_GROUPS, SSM_STATE, SSM_GROUPS, SSM_GROUP)
    dc = jnp.sum(jnp.where(eye[None, :, None, :, None], dc, 0.0), axis=3).transpose(0, 1, 3, 2)
    return db[0], db[1], dc[0], -dc[1]


def _layer_fwd(x, p, w, sp, ctab, stab, n_seq):
    (h,), _ = _token_call("norm_in", lambda x, g: ([_norm_in_tile(x, g)], []), 512, [_whole(x)], [sp["mix_norm_g"]],
                          [(x.shape[1], MXU_DTYPE)], [])
    z = _matmul_nn("mm_in", h, w["w_in"], F32)
    ya, lse = _attention_fwd(z, ctab, stab, sp["attn_sinks"], n_seq)
    ys, states = _ssm_fwd(z, sp["bmat"], sp["a_row"], sp["f_row"], sp["cmat"], sp["ssm_d"], n_seq)
    uc = _conv_fwd(z, w["conv_dw_w"], sp["conv_dw_b"], sp["conv_norm_g"], sp["conv_norm_b"], n_seq)
    merge_consts = [w["w_attn_out"], w["w_ssm_glu"], sp["b_ssm_glu"], w["w_conv_out"], sp["b_gate"], w["w_mix_out"],
                    sp["ffn_norm_g"]]
    (x1, hf), _ = _token_call("merge", lambda *a: (list(_merge_tile(*_f32s(a))), []), 256,
                              [_whole(x), _whole(ya), _whole(ys), _whole(uc), (z, ZG_W, 0)], merge_consts,
                              [(x.shape[1], F32), (x.shape[1], MXU_DTYPE)], [])
    gate, up, act = _ffn_in_act(hf, w["w_ffn_in"])
    ffn = _matmul_nn("mm_ffn_out", act, w["w_ffn_out"], F32)

    def ple_fn(x1, ffn, p, w_pi, g_ple, w_pg):
        x2 = x1 + ffn
        return [x2, _ple_tile(x2, p, w_pi.astype(F32), g_ple, w_pg.astype(F32))], []

    (x2, x3), _ = _token_call("ple", ple_fn, 512, [_whole(x1), _whole(ffn), _whole(p)],
                              [w["w_ple_in"], sp["ple_norm_g"], w["w_ple_gate"]], [(x.shape[1], F32)] * 2, [])
    saved = dict(x=x, h=h, z=z, ya=ya, lse=lse, ys=ys, states=states, uc=uc, hf=hf, gate=gate, up=up, act=act, x2=x2, p=p)
    return x3, saved


def _layer_bwd(dx3, sv, w, sp, ctab, stab, n_seq):
    d = dx3.shape[1]
    gw, gs = {}, {}

    def ple_bwd(x2, p, dx3, w_pi, g_ple, w_pg):
        _, vjp = jax.vjp(lambda x2, w_pi, g_ple, w_pg: _ple_tile(x2, p, w_pi, g_ple, w_pg), x2, w_pi.astype(F32), g_ple,
                         w_pg.astype(F32))
        dx2, dw_pi, dg_ple, dw_pg = vjp(dx3)
        return [dx2, dx2], [dw_pi, dg_ple, dw_pg]

    (dx2, dffn), (gw["w_ple_in"], gs["ple_norm_g"], gw["w_ple_gate"]) = _token_call(
        "ple_bwd", ple_bwd, 256, [_whole(sv["x2"]), _whole(sv["p"]), _whole(dx3)],
        [w["w_ple_in"], sp["ple_norm_g"], w["w_ple_gate"]], [(d, F32), (d, MXU_DTYPE)],
        [w["w_ple_in"].shape, (1, d), w["w_ple_gate"].shape])

    dgate, dup = _ffn_mid_bwd(dffn, w["w_ffn_out"], sv["gate"], sv["up"])
    gw["w_ffn_out"] = _matmul_tn("mm_ffn_out_dw", sv["act"], dffn)
    dhf = _ffn_in_dx(dgate, dup, w["w_ffn_in"])
    gw["w_ffn_in"] = jnp.concatenate([_matmul_tn("mm_ffn_gate_dw", sv["hf"], dgate), _matmul_tn("mm_ffn_up_dw", sv["hf"], dup)],
                                     axis=1)

    def merge_bwd(x, ya, ys, uc, gin, dx1, dhf, *consts):
        consts = _f32s(consts)
        _, vjp = jax.vjp(_merge_tile, *_f32s((x, ya, ys, uc, gin)), *consts)
        g = vjp((dx1, dhf))
        return list(g[:5]), list(g[5:])

    merge_consts = [w["w_attn_out"], w["w_ssm_glu"], sp["b_ssm_glu"], w["w_conv_out"], sp["b_gate"], w["w_mix_out"],
                    sp["ffn_norm_g"]]
    (dx_res, dya, dys, duc, dgin), macc = _token_call(
        "merge_bwd", merge_bwd, 256,
        [_whole(sv["x"]), _whole(sv["ya"]), _whole(sv["ys"]), _whole(sv["uc"]), (sv["z"], ZG_W, 0), _whole(dx2), _whole(dhf)],
        merge_consts, [(d, F32), (Q_WIDTH, MXU_DTYPE), (SSM_WIDTH, F32), (CONV_WIDTH, F32), (ZG_W, MXU_DTYPE)],
        [c.shape for c in merge_consts])
    gw["w_attn_out"], gw["w_ssm_glu"], gs["b_ssm_glu"], gw["w_conv_out"], gs["b_gate"], gw["w_mix_out"], gs["ffn_norm_g"] = macc

    dv, gs["conv_norm_g"], gs["conv_norm_b"], gs["conv_dw_b"] = _conv_bwd_post(
        sv["z"], duc, w["conv_dw_w"], sp["conv_dw_b"], sp["conv_norm_g"], sp["conv_norm_b"], n_seq)
    dzc, dw_taps = _conv_bwd_taps(sv["z"], dv, w["conv_dw_w"], n_seq)
    gw["conv_dw_w"] = dw_taps[:CONV_K]

    dzs, gs["bmat"], gs["cmat"], gs["a_row"], gs["f_row"], gs["ssm_d"] = _ssm_bwd(
        sv["z"], sv["states"], dys, sp["bmat"], sp["a_row"], sp["f_row"], sp["cmat"], sp["ssm_d"], n_seq)

    dzq, dkv, dkvp, gs["attn_sinks"] = _attention_bwd(sv["z"], ctab, stab, sp["attn_sinks"], sv["ya"], sv["lse"], dya, n_seq)
    dz = _assemble_dz(dgin, dzq, dkv, dkvp, dzs, dzc, n_seq)
    dh = _matmul_nt("mm_in_dx", dz, w["w_in"], F32)
    gw["w_in"] = _matmul_tn("mm_in_dw", sv["h"], dz)

    def norm_bwd(x, dh, dx_res, g):
        _, vjp = jax.vjp(_norm_in_tile, x, g)
        dx, dg = vjp(dh)
        return [dx + dx_res], [dg]

    (dx,), (gs["mix_norm_g"],) = _token_call("norm_in_bwd", norm_bwd, 512, [_whole(sv["x"]), _whole(dh), _whole(dx_res)],
                                             [sp["mix_norm_g"]], [(d, F32)], [(1, d)])
    return dx, gw, gs


def _loss_and_grad(x, target, g):
    def fn(x, tgt, g):
        def f(x, g):
            err = _rms(x, g) - tgt
            return 0.5 * jnp.mean(err * err, axis=-1, keepdims=True)

        per_token, vjp = jax.vjp(f, x, g)
        dx, dg = vjp(jnp.ones_like(per_token))
        return [dx], [jnp.sum(per_token, axis=0, keepdims=True), dg]

    (dx,), (loss, dg) = _token_call("loss", fn, 512, [_whole(x), _whole(target)], [g], [(x.shape[1], F32)],
                                    [(8, LANES), (1, x.shape[1])])
    return loss[0, 0], dx, dg


def _mesh_place():
    return lax.axis_index("x"), lax.axis_index("y"), lax.axis_index("c")


def _flip(v, bit):
    return 1 - v if bit else v


_ANY = pl.BlockSpec(memory_space=pl.ANY)
_MESH = pl.DeviceIdType.MESH


def _all_gather(name, xs):
    n = len(xs)

    def body(*refs):
        x_refs, out_refs = refs[:n], refs[n:2 * n]
        send_sems, recv_sems, local_sems = refs[2 * n:]
        mx, my, mc = _mesh_place()
        me, sibling = (mx, my, mc), (mx, my, 1 - mc)
        chips = [(1 - mx, my), (mx, 1 - my), (1 - mx, 1 - my)]

        def slot(a, px, py, pc):
            return out_refs[a].at[4 * px + 2 * py + pc]

        def copy(a, k, block, to, src=None):
            return pltpu.make_async_remote_copy(
                src_ref=slot(a, *block) if src is None else src, dst_ref=slot(a, *block), send_sem=send_sems.at[7 * a + k],
                recv_sem=recv_sems.at[7 * a + k], device_id=to, device_id_type=_MESH)

        mine = [pltpu.make_async_copy(x_refs[a], slot(a, *me), local_sems.at[a]) for a in range(n)]
        for cp in mine:
            cp.start()
        first = [copy(a, 0, me, sibling, src=x_refs[a]) for a in range(n)]
        first += [copy(a, 1 + j, me, (*chip, mc), src=x_refs[a]) for j, chip in enumerate(chips) for a in range(n)]
        for cp in first:
            cp.start()
        passed = []
        for j, chip in enumerate(chips):
            for a in range(n):
                copy(a, 1 + j, (*chip, mc), me).wait_recv()
                passed.append(copy(a, 4 + j, (*chip, mc), sibling))
                passed[-1].start()
        for a in range(n):
            copy(a, 0, sibling, me).wait_recv()
            for j, chip in enumerate(chips):
                copy(a, 4 + j, (*chip, 1 - mc), me).wait_recv()
        for cp in first + passed:
            cp.wait_send()
        for cp in mine:
            cp.wait()

    return pl.pallas_call(
        body, name=name, out_shape=[jax.ShapeDtypeStruct((N_DEV,) + x.shape, x.dtype) for x in xs], in_specs=[_ANY] * n,
        out_specs=[_ANY] * n,
        scratch_shapes=[pltpu.SemaphoreType.DMA((7 * n,)), pltpu.SemaphoreType.DMA((7 * n,)), pltpu.SemaphoreType.DMA((n,))])(*xs)


def _direct_copies(kind, src_refs, land_refs, send_sems, recv_sems, local_sems):
    mx, my, mc = _mesh_place()
    me = 4 * mx + 2 * my + mc
    n = len(src_refs)
    own = [pltpu.make_async_copy(src_refs[a] if kind == "gather" else src_refs[a].at[me], land_refs[a].at[me], local_sems.at[a])
           for a in range(n)]
    copies = []
    for rel in range(1, N_DEV):
        px, py, pc = _flip(mx, rel & 4), _flip(my, rel & 2), _flip(mc, rel & 1)
        for a in range(n):
            src = src_refs[a] if kind == "gather" else src_refs[a].at[4 * px + 2 * py + pc]
            copies.append(pltpu.make_async_remote_copy(
                src_ref=src, dst_ref=land_refs[a].at[me], send_sem=send_sems.at[7 * a + rel - 1],
                recv_sem=recv_sems.at[7 * a + rel - 1], device_id=(px, py, pc), device_id_type=_MESH))
    return copies, own


_HBM = pl.BlockSpec(memory_space=pltpu.HBM)
_SEM = pl.BlockSpec(memory_space=pltpu.SEMAPHORE)
_DATAFLOW = pltpu.SideEffectType.DATAFLOW_SIDE_EFFECTING


def _exchange_start(name, kind, srcs):
    n = len(srcs)
    lands = [lax.empty(((N_DEV,) + s.shape) if kind == "gather" else s.shape, s.dtype) for s in srcs]

    def body(*refs):
        src_refs, land_refs = refs[:n], refs[n:2 * n]
        send_sems, recv_sems, local_sems = refs[2 * n:2 * n + 3]
        copies, own = _direct_copies(kind, src_refs, land_refs, send_sems, recv_sems, local_sems)
        for cp in own + copies:
            cp.start()
        refs[-1][...] = jnp.zeros_like(refs[-1])

    hbm = lambda a: pltpu.with_memory_space_constraint(a, pltpu.HBM)
    out = pl.pallas_call(
        body, name=name,
        out_shape=[pltpu.SemaphoreType.DMA((7 * n,)), pltpu.SemaphoreType.DMA((7 * n,)), pltpu.SemaphoreType.DMA((n,))]
        + [pltpu.HBM(a.shape, a.dtype) for a in srcs + lands] + [jax.ShapeDtypeStruct((8, LANES), F32)],
        in_specs=[_HBM] * (2 * n), out_specs=[_SEM] * 3 + [_HBM] * (2 * n) + [pl.BlockSpec(memory_space=pltpu.VMEM)],
        input_output_aliases={i: 3 + i for i in range(2 * n)},
        compiler_params=pltpu.CompilerParams(has_side_effects=_DATAFLOW))(*[hbm(a) for a in srcs + lands])
    return (kind, out[:-1]), out[-1]


def _exchange_wait(name, started, after):
    kind, (send_sems, recv_sems, local_sems, *arrays) = started
    n = len(arrays) // 2

    def body(*refs):
        src_refs, land_refs = refs[:n], refs[n:2 * n]
        copies, own = _direct_copies(kind, src_refs, land_refs, *refs[2 * n:2 * n + 3])
        for cp in copies + own:
            cp.wait()

    out = pl.pallas_call(
        body, name=name, out_shape=[pltpu.HBM(a.shape, a.dtype) for a in arrays],
        in_specs=[_HBM] * (2 * n) + [_SEM] * 3 + [_ANY], out_specs=[_HBM] * (2 * n),
        input_output_aliases={i: i for i in range(2 * n)},
        compiler_params=pltpu.CompilerParams(has_side_effects=_DATAFLOW))(*arrays, send_sems, recv_sems, local_sems, after)
    return out[n:]


def _adamw_math(g, w, m, v):
    m2 = ADAM_B1 * m + (1.0 - ADAM_B1) * g
    v2 = ADAM_B2 * v + (1.0 - ADAM_B2) * jnp.square(g)
    m_hat = m2 / (1.0 - ADAM_B1 ** ADAM_STEP)
    v_hat = v2 / (1.0 - ADAM_B2 ** ADAM_STEP)
    return g, -ADAM_LR * (m_hat / (jnp.sqrt(v_hat) + ADAM_EPS) + ADAM_WD * w), m2, v2


def _sum_blocks(ref):
    g = ref[0].astype(F32)
    for j in range(1, N_DEV):
        g = g + ref[j].astype(F32)
    return g


def _adamw_flat(name, parts, w, m, v):
    r = w.shape[0]
    tile = _pick(r, (1024, 512, 256, 128, 8))

    def body(p_ref, w_ref, m_ref, v_ref, *o_refs):
        for o, val in zip(o_refs, _adamw_math(_sum_blocks(p_ref), w_ref[...], m_ref[...], v_ref[...]), strict=True):
            o[...] = val

    flat = pl.BlockSpec((tile, LANES), lambda i: (i, 0))
    return pl.pallas_call(
        body, name=name, grid=(r // tile,), out_shape=[jax.ShapeDtypeStruct((r, LANES), F32)] * 4,
        in_specs=[pl.BlockSpec((N_DEV, tile, LANES), lambda i: (0, i, 0)), flat, flat, flat], out_specs=[flat] * 4,
        compiler_params=_params(("parallel",)))(parts, w, m, v)


def _adamw_cols(name, landed, base, stride, w, m, v):
    depth, rows, cs = w.shape
    n_slab = -(-cs // LANES)
    tr = _pick(rows, (SLAB_TILE,))

    def body(*refs):
        slabs, (w_ref, m_ref, v_ref), o_refs = refs[:n_slab], refs[n_slab:n_slab + 3], refs[n_slab + 3:]
        g = jnp.concatenate([_sum_blocks(s)[:, :min(LANES, cs - LANES * k)] for k, s in enumerate(slabs)], axis=1)
        for o, val in zip(o_refs, _adamw_math(g, w_ref[...], m_ref[...], v_ref[...]), strict=True):
            o[...] = val

    slab = lambda k: pl.BlockSpec((N_DEV, tr, LANES), lambda l, i: (0, (l * stride + base + k * rows) // tr + i, 0))
    nat = pl.BlockSpec((None, tr, cs), lambda l, i: (l, i, 0))
    return pl.pallas_call(
        body, name=name, grid=(depth, rows // tr), out_shape=[jax.ShapeDtypeStruct(w.shape, F32)] * 4,
        in_specs=[slab(k) for k in range(n_slab)] + [nat] * 3, out_specs=[nat] * 4,
        compiler_params=_params(("parallel", "parallel")))(*[landed] * n_slab, w, m, v)


def _adamw_rows(name, landed, base, stride, w, m, v):
    depth, rs, width = w.shape
    tr = math.gcd(rs, base, stride)

    def body(p_ref, w_ref, m_ref, v_ref, *o_refs):
        for o, val in zip(o_refs, _adamw_math(_sum_blocks(p_ref), w_ref[...], m_ref[...], v_ref[...]), strict=True):
            o[...] = val

    nat = pl.BlockSpec((None, tr, width), lambda l, i: (l, i, 0))
    return pl.pallas_call(
        body, name=name, grid=(depth, rs // tr), out_shape=[jax.ShapeDtypeStruct(w.shape, F32)] * 4,
        in_specs=[pl.BlockSpec((N_DEV, tr, width), lambda l, i: (0, (l * stride + base) // tr + i, 0)), nat, nat, nat],
        out_specs=[nat] * 4, compiler_params=_params(("parallel", "parallel")))(landed, w, m, v)


def _adamw_conv(landed, base, stride, w, m, v):
    depth, taps, cs = w.shape

    def body(p_ref, w_ref, m_ref, v_ref, *o_refs):
        g = _sum_blocks(p_ref)[:taps, :cs]
        for o, val in zip(o_refs, _adamw_math(g, w_ref[...], m_ref[...], v_ref[...]), strict=True):
            o[...] = val

    nat = pl.BlockSpec((None, taps, cs), lambda l: (l, 0, 0))
    return pl.pallas_call(
        body, name="adamw_conv", grid=(depth,), out_shape=[jax.ShapeDtypeStruct(w.shape, F32)] * 4,
        in_specs=[pl.BlockSpec((N_DEV, CONV_HALO, LANES), lambda l: (0, (l * stride + base) // CONV_HALO, 0)), nat, nat, nat],
        out_specs=[nat] * 4, compiler_params=_params(("parallel",)))(landed, w, m, v)


def _unshard_cols(name, gathered, start, rows, cs, shift=0):
    n_slab = -(-cs // LANES)
    total = N_DEV * cs
    tr = _pick(rows, (SLAB_TILE,))

    def body(*refs):
        slabs, o_ref = refs[:n_slab], refs[n_slab]
        for j in range(N_DEV):
            for k, s in enumerate(slabs):
                for src, dst, width in _wrapped(j * cs + LANES * k - shift, min(LANES, cs - LANES * k), total):
                    o_ref[:, dst:dst + width] = s[j, :, src:src + width]

    slab = lambda k: pl.BlockSpec((N_DEV, tr, LANES), lambda i: (0, (start + k * rows) // tr + i, 0))
    return pl.pallas_call(
        body, name=name, grid=(rows // tr,), out_shape=jax.ShapeDtypeStruct((rows, total), gathered.dtype),
        in_specs=[slab(k) for k in range(n_slab)], out_specs=pl.BlockSpec((tr, total), lambda i: (i, 0)),
        compiler_params=_params(("parallel",)))(*[gathered] * n_slab)


def _shard_cols(name, full, cs, shift=0):
    rows, total = full.shape
    n_slab = -(-cs // LANES)
    tr = _pick(rows, (SLAB_TILE,))

    def body(f_ref, o_ref):
        for j in range(N_DEV):
            for k in range(n_slab):
                used = min(LANES, cs - LANES * k)
                for src, dst, width in _wrapped(j * cs + LANES * k - shift, used, total):
                    o_ref[j, k, :, src:src + width] = f_ref[:, dst:dst + width].astype(o_ref.dtype)
                if used < LANES:
                    o_ref[j, k, :, used:] = jnp.zeros((tr, LANES - used), o_ref.dtype)

    out = pl.pallas_call(
        body, name=name, grid=(rows // tr,), out_shape=jax.ShapeDtypeStruct((N_DEV, n_slab, rows, LANES), BF16),
        in_specs=[pl.BlockSpec((tr, total), lambda i: (i, 0))],
        out_specs=pl.BlockSpec((N_DEV, n_slab, tr, LANES), lambda i: (0, 0, i, 0)),
        compiler_params=_params(("parallel",)))(full)
    return out.reshape(N_DEV, n_slab * rows, LANES)


def _wrapped(pos, width, total):
    pos %= total
    if pos + width <= total:
        return [(0, pos, width)]
    head = total - pos
    return [(0, pos, head), (head, 0, width - head)]


CONV_W_PIECES = 3


def _pad_to(n, align):
    return -(-n // align) * align


def _layout():
    dims = {name: (rows, cols) for name, rows, cols, _ in SHARDED}
    col, off = {}, 0
    for name in COL_SHARDED:
        rows, cols = dims[name]
        cs = cols // N_DEV
        col[name] = (off, rows, cs)
        off += -(-cs // LANES) * rows
    conv_base = off
    col_rows = _pad_to(off + CONV_W_PIECES * CONV_HALO, SLAB_TILE)
    row, off = {}, 0
    for name in ROW_SHARDED:
        rs = dims[name][0] // N_DEV
        row[name] = (off, rs)
        off += _pad_to(rs, LANES)
    return col, conv_base, col_rows, row, off


def _slabs(shard):
    rows, cs = shard.shape
    parts = []
    for k in range(-(-cs // LANES)):
        part = shard[:, LANES * k:min(LANES * (k + 1), cs)]
        parts.append(jnp.pad(part, ((0, 0), (0, LANES - part.shape[1]))))
    return jnp.concatenate(parts, axis=0)


def _concat_padded(pieces, total, axis):
    used = sum(p.shape[axis] for p in pieces)
    if total > used:
        shape = list(pieces[0].shape)
        shape[axis] = total - used
        pieces = pieces + [jnp.zeros(shape, pieces[0].dtype)]
    return jnp.concatenate(pieces, axis=axis)


def _split3(a):
    hi = a.astype(BF16)
    r1 = a - hi.astype(F32)
    mid = r1.astype(BF16)
    return hi, mid, (r1 - mid.astype(F32)).astype(BF16)


def _pack_small(arrs, lead=()):
    flat = jnp.concatenate([a.reshape(lead + (-1,)) for a in arrs], axis=-1)
    total = _pad_to(flat.shape[-1], 512 * LANES)
    flat = jnp.pad(flat, [(0, 0)] * len(lead) + [(0, total - flat.shape[-1])])
    return flat.reshape(lead + (total // LANES, LANES))


def _unpack_small(flat, shapes):
    flat = flat.reshape(-1)
    res, off = [], 0
    for s in shapes:
        n = int(np.prod(s))
        res.append(flat[off:off + n].reshape(s))
        off += n
    return res


def _small_rows(a, depth):
    n16 = depth * SSM_GROUPS
    a_re, a_im, f_re, f_im = _ssm_coeffs(a["ssm_lambda_re"].reshape(n16, SSM_STATE), a["ssm_lambda_im"].reshape(n16, SSM_STATE),
                                         a["ssm_log_dt"].reshape(n16, 1))
    rows = []
    for l in range(depth):
        sp = {k: _row(a[k][l]) for k in ("mix_norm_g", "b_gate", "attn_sinks", "ssm_d", "b_ssm_glu", "conv_dw_b",
                                         "conv_norm_g", "conv_norm_b", "ffn_norm_g", "ple_norm_g")}
        g = slice(l * SSM_GROUPS, (l + 1) * SSM_GROUPS)
        sp["a_row"] = jnp.concatenate([a_re[g].reshape(1, -1), a_im[g].reshape(1, -1)], axis=1)
        sp["f_row"] = jnp.concatenate([f_re[g].reshape(1, -1), f_im[g].reshape(1, -1)], axis=1)
        sp["bmat"], sp["cmat"] = _ssm_mats(a["ssm_b_re"][l], a["ssm_b_im"][l], a["ssm_c_re"][l], a["ssm_c_im"][l])
        rows.append(sp)
    return rows


def _local_step(a, get_weights, on_grads, depth):
    n_seq, seq, d = a["x"].shape
    t = n_seq * seq
    inv = ROPE_THETA ** (-jnp.arange(0, ROPE_DIM, 2, dtype=F32) / ROPE_DIM)
    lane = np.arange(LANES) % HEAD_DIM
    inv_lane = jnp.where(lane < ROPE_DIM, jnp.tile(inv, LANES // (ROPE_DIM // 2)), 0.0).reshape(1, LANES)
    ctab, stab = _rope_tables(a["positions"].reshape(t), inv_lane)
    small = _small_rows(a, depth)

    x = a["x"].reshape(t, d)
    saved, weights = [], []
    for l in range(depth):
        weights.append(get_weights(l, x))
        x, sv = _layer_fwd(x, a["p"][l].reshape(t, -1), weights[l], small[l], ctab, stab, n_seq)
        saved.append(sv)
    loss, dx, d_final = _loss_and_grad(x, a["loss_target"].reshape(t, d), _row(a["final_norm_g"]))
    gws, gss, token = [None] * depth, [None] * depth, None
    for l in reversed(range(depth)):
        sp = small[l] if token is None else dict(small[l], ple_norm_g=small[l]["ple_norm_g"] + token[0:1, 0:1])
        dx, gws[l], gss[l] = _layer_bwd(dx, saved[l], weights[l], sp, ctab, stab, n_seq)
        token = on_grads(l, gws[l])

    n16 = depth * SSM_GROUPS
    halves = lambda k, h: jnp.concatenate([gss[l][k][:, h * SSM_LANES:(h + 1) * SSM_LANES].reshape(SSM_GROUPS, SSM_STATE)
                                           for l in range(depth)], axis=0)
    dlr, dli, ddt = _ssm_coeffs_bwd(a["ssm_lambda_re"].reshape(n16, SSM_STATE), a["ssm_lambda_im"].reshape(n16, SSM_STATE),
                                    a["ssm_log_dt"].reshape(n16, 1),
                                    (halves("a_row", 0), halves("a_row", 1), halves("f_row", 0), halves("f_row", 1)))
    bc = [_ssm_mats_t(gss[l]["bmat"], gss[l]["cmat"]) for l in range(depth)]
    gsmall = {k: jnp.stack([gss[l][k].reshape(a[k].shape[1:]) for l in range(depth)])
              for k in ("mix_norm_g", "b_gate", "attn_sinks", "ssm_d", "b_ssm_glu", "conv_dw_b", "conv_norm_g", "conv_norm_b",
                        "ffn_norm_g", "ple_norm_g")}
    gsmall["ssm_lambda_re"] = dlr.reshape(a["ssm_lambda_re"].shape)
    gsmall["ssm_lambda_im"] = dli.reshape(a["ssm_lambda_im"].shape)
    gsmall["ssm_log_dt"] = ddt.reshape(a["ssm_log_dt"].shape)
    for i, k in enumerate(("ssm_b_re", "ssm_b_im", "ssm_c_re", "ssm_c_im")):
        gsmall[k] = jnp.stack([bc[l][i] for l in range(depth)])
    gsmall["final_norm_g"] = d_final.reshape(a["final_norm_g"].shape)
    return loss, dx.reshape(n_seq, seq, d), gws, gsmall


def kernel(x, p, positions, mix_norm_g, w_in, b_gate, attn_sinks, w_attn_out, ssm_lambda_re, ssm_lambda_im, ssm_log_dt, ssm_b_re, ssm_b_im, ssm_c_re, ssm_c_im, ssm_d, w_ssm_glu, b_ssm_glu, conv_dw_w, conv_dw_b, conv_norm_g, conv_norm_b, w_conv_out, w_mix_out, ffn_norm_g, w_ffn_in, w_ffn_out, w_ple_in, ple_norm_g, w_ple_gate, final_norm_g, loss_target, m_mix_norm_g, m_w_in, m_b_gate, m_attn_sinks, m_w_attn_out, m_ssm_lambda_re, m_ssm_lambda_im, m_ssm_log_dt, m_ssm_b_re, m_ssm_b_im, m_ssm_c_re, m_ssm_c_im, m_ssm_d, m_w_ssm_glu, m_b_ssm_glu, m_conv_dw_w, m_conv_dw_b, m_conv_norm_g, m_conv_norm_b, m_w_conv_out, m_w_mix_out, m_ffn_norm_g, m_w_ffn_in, m_w_ffn_out, m_w_ple_in, m_ple_norm_g, m_w_ple_gate, m_final_norm_g, v_mix_norm_g, v_w_in, v_b_gate, v_attn_sinks, v_w_attn_out, v_ssm_lambda_re, v_ssm_lambda_im, v_ssm_log_dt, v_ssm_b_re, v_ssm_b_im, v_ssm_c_re, v_ssm_c_im, v_ssm_d, v_w_ssm_glu, v_b_ssm_glu, v_conv_dw_w, v_conv_dw_b, v_conv_norm_g, v_conv_norm_b, v_w_conv_out, v_w_mix_out, v_ffn_norm_g, v_w_ffn_in, v_w_ffn_out, v_w_ple_in, v_ple_norm_g, v_w_ple_gate, v_final_norm_g):
    a = dict(locals())
    depth = w_in.shape[0]
    col, conv_base, col_rows, row, row_rows = _layout()
    shift = {"w_in": Z_SPLIT}
    conv_pad = ((0, 0), (0, CONV_HALO - CONV_K), (0, LANES - CONV_WIDTH // N_DEV))

    gathers, tokens = [], []
    for l in range(depth):
        pieces = [_slabs(a[name][l].astype(BF16)) for name in COL_SHARDED]
        pieces.append(jnp.pad(jnp.stack(_split3(a["conv_dw_w"][l])), conv_pad).reshape(-1, LANES))
        regions = [_concat_padded([a[name][l].astype(BF16)], _pad_to(rs, LANES), 0) for name, (_, rs) in row.items()]
        started, token = _exchange_start(f"gather_start_{l}", "gather",
                                         [_concat_padded(pieces, col_rows, 0), jnp.concatenate(regions, axis=0)])
        gathers.append(started)
        tokens.append(token[0:1, 0:1])

    def get_weights(l, x):
        slab8, row8 = _exchange_wait(f"gather_wait_{l}", gathers[l], x)
        w = {name: _unshard_cols("unshard_" + name, slab8, base, rows, cs, shift.get(name, 0))
             for name, (base, rows, cs) in col.items()}
        for name, (base, rs) in row.items():
            w[name] = row8[:, base:base + rs].reshape(N_DEV * rs, -1)
        conv = slab8[:, conv_base:conv_base + CONV_W_PIECES * CONV_HALO]
        conv = conv.reshape(N_DEV, CONV_W_PIECES, CONV_HALO, LANES)[:, :, :CONV_K, :CONV_WIDTH // N_DEV].astype(F32)
        w["conv_dw_w"] = jnp.sum(conv, axis=1).transpose(1, 0, 2).reshape(CONV_K, CONV_WIDTH)
        return w

    scatters = [None] * depth

    def on_grads(l, gw):
        pieces = [_shard_cols("shard_" + name, gw[name], cs, shift.get(name, 0)) for name, (_, _, cs) in col.items()]
        conv = gw["conv_dw_w"].reshape(CONV_K, N_DEV, CONV_WIDTH // N_DEV).transpose(1, 0, 2).astype(BF16)
        pieces.append(jnp.pad(jnp.pad(conv, conv_pad), ((0, 0), (0, (CONV_W_PIECES - 1) * CONV_HALO), (0, 0))))
        regions = [_concat_padded([gw[name].astype(BF16).reshape(N_DEV, rs, -1)], _pad_to(rs, LANES), 1)
                   for name, (_, rs) in row.items()]
        scatters[l], token = _exchange_start(f"grads_start_{l}", "scatter",
                                             [_concat_padded(pieces, col_rows, 1), jnp.concatenate(regions, axis=1)])
        return token

    local = dict(a, mix_norm_g=a["mix_norm_g"] + sum(tokens))
    loss, grad_x, _, gsmall = _local_step(local, get_weights, on_grads, depth)
    loss = lax.psum(loss, ("x", "y", "c"))

    landed = [_exchange_wait(f"grads_wait_{l}", scatters[l], grad_x) for l in range(depth)]
    landed_slab = jnp.concatenate([ls for ls, _ in landed], axis=1)
    landed_row = jnp.concatenate([lr for _, lr in landed], axis=1)
    state = lambda name: (a[name], a["m_" + name], a["v_" + name])
    big = {name: _adamw_cols("adamw_" + name, landed_slab, base, col_rows, *state(name)) for name, (base, _, _) in col.items()}
    big.update({name: _adamw_rows("adamw_" + name, landed_row, base, row_rows, *state(name)) for name, (base, _) in row.items()})
    big["conv_dw_w"] = _adamw_conv(landed_slab, conv_base, col_rows, *state("conv_dw_w"))

    shapes = [a[k].shape for k in REPLICATED]
    parts, = _all_gather("gather_small_grads", [_pack_small([gsmall[k] for k in REPLICATED])])
    small_state = [_pack_small([a[pre + k] for k in REPLICATED]) for pre in ("", "m_", "v_")]
    small = [dict(zip(REPLICATED, _unpack_small(o, shapes), strict=True))
             for o in _adamw_flat("adamw_replicated", parts, *small_state)]

    def result(kind, name):
        if name in REPLICATED:
            return small[kind][name]
        return big[name][kind]

    return (loss, grad_x, *[result(kind, n) for kind in range(4) for n in WEIGHT_ORDER])
```

```python
import functools
import math

import numpy as np
import jax
import jax.numpy as jnp
from jax import lax
from jax.experimental import pallas as pl
from jax.experimental.pallas import tpu as pltpu

F32 = jnp.float32
BF16 = jnp.bfloat16
MXU_DTYPE = jnp.bfloat16
VMEM_LIMIT_BYTES = 56 * 2 ** 20
N_DEV = 8
LANES = 128

HEAD_DIM = 64
N_Q_HEADS = 8
N_KV_HEADS = 2
GQA_GROUP = 4
BLOCK = 128
ROPE_THETA = 500000.0
ROPE_DIM = 16
Q_WIDTH = 512
KV_WIDTH = 128
SSM_WIDTH = 256
SSM_GROUP = 16
SSM_GROUPS = 16
SSM_STATE = 64
SSM_LANES = SSM_GROUPS * SSM_STATE
CONV_WIDTH = 256
CONV_K = 31
CONV_HALO = 32
EPS = 1e-6
NEG_INF = -1e30
ADAM_LR, ADAM_B1, ADAM_B2, ADAM_EPS, ADAM_WD, ADAM_STEP = 0.001, 0.9, 0.999, 1e-08, 0.01, 10

ZG_W, ZQ_W, ZKV_W, ZS_W, ZC_W = 3072, 512, 256, 256, 512
ZQ_BLK, ZKV_BLK, ZS_BLK, ZC_BLK = 3072 // 512, 3584 // 256, 3840 // 256, 4096 // 512
Z_WIDTH = 4608
Z_SPLIT = 1536

SHARDED = (("w_in", 1024, 4608, 1), ("w_attn_out", 512, 1024, 1), ("w_ssm_glu", 256, 2048, 1),
           ("conv_dw_w", 31, 256, 1), ("w_conv_out", 256, 1024, 1), ("w_mix_out", 1024, 1024, 0),
           ("w_ffn_in", 1024, 5632, 1), ("w_ffn_out", 2816, 1024, 0), ("w_ple_in", 256, 1024, 1),
           ("w_ple_gate", 1024, 1024, 0))
COL_SHARDED = ("w_in", "w_ffn_in", "w_attn_out", "w_ssm_glu", "w_conv_out", "w_ple_in")
ROW_SHARDED = ("w_ffn_out", "w_mix_out", "w_ple_gate")
SLAB_TILE = 256
FLAT_ROW_ALIGN = 1024
REPLICATED = ("mix_norm_g", "b_gate", "attn_sinks", "ssm_lambda_re", "ssm_lambda_im", "ssm_log_dt", "ssm_b_re",
              "ssm_b_im", "ssm_c_re", "ssm_c_im", "ssm_d", "b_ssm_glu", "conv_dw_b", "conv_norm_g", "conv_norm_b",
              "ffn_norm_g", "ple_norm_g", "final_norm_g")
WEIGHT_ORDER = ("mix_norm_g", "w_in", "b_gate", "attn_sinks", "w_attn_out", "ssm_lambda_re", "ssm_lambda_im",
                "ssm_log_dt", "ssm_b_re", "ssm_b_im", "ssm_c_re", "ssm_c_im", "ssm_d", "w_ssm_glu", "b_ssm_glu",
                "conv_dw_w", "conv_dw_b", "conv_norm_g", "conv_norm_b", "w_conv_out", "w_mix_out", "ffn_norm_g",
                "w_ffn_in", "w_ffn_out", "w_ple_in", "ple_norm_g", "w_ple_gate", "final_norm_g")


_ANY = pl.BlockSpec(memory_space=pl.ANY)


def _params(sem=None):
    return pltpu.CompilerParams(dimension_semantics=sem, vmem_limit_bytes=VMEM_LIMIT_BYTES)


def _pick(n, cands):
    for c in cands:
        if n % c == 0:
            return c
    return n


def _dot(a, b, dims):
    return lax.dot_general(a.astype(MXU_DTYPE), b.astype(MXU_DTYPE), (dims, ((), ())), preferred_element_type=F32)


def _dot_nn(a, b):
    return _dot(a, b, ((1,), (0,)))


def _dot_nt(a, b):
    return _dot(a, b, ((1,), (1,)))


def _dot_tn(a, b):
    return _dot(a, b, ((0,), (0,)))


@jax.custom_vjp
def _mm(x, w):
    return _dot_nn(x, w)


def _mm_f(x, w):
    return _dot_nn(x, w), (x, w)


def _mm_b(res, dy):
    x, w = res
    return _dot_nt(dy, w).astype(x.dtype), _dot_tn(x, dy).astype(w.dtype)


_mm.defvjp(_mm_f, _mm_b)


def _rms(x, g):
    return x * lax.rsqrt(jnp.mean(x * x, axis=-1, keepdims=True) + EPS) * g


ROW_TILES = (1024, 512, 256, 128)
COL_TILES = (1536, 1408, 1024, 512, 256, 128)


def _matmul_nn(name, a, b, out_dtype):
    t, k = a.shape
    n = b.shape[1]
    tm, tn = _pick(t, ROW_TILES), _pick(n, COL_TILES)

    def body(a_ref, b_ref, o_ref):
        o_ref[...] = _dot_nn(a_ref[...], b_ref[...]).astype(o_ref.dtype)

    return pl.pallas_call(
        body, name=name, grid=(t // tm, n // tn), out_shape=jax.ShapeDtypeStruct((t, n), out_dtype),
        in_specs=[pl.BlockSpec((tm, k), lambda i, j: (i, 0)), pl.BlockSpec((k, tn), lambda i, j: (0, j))],
        out_specs=pl.BlockSpec((tm, tn), lambda i, j: (i, j)),
        compiler_params=_params(("parallel", "parallel")))(a, b)


def _matmul_nt(name, a, b, out_dtype):
    t, n = a.shape
    k = b.shape[0]
    tm, tk = _pick(t, ROW_TILES[1:]), _pick(k, COL_TILES)

    def body(a_ref, b_ref, o_ref):
        o_ref[...] = _dot_nt(a_ref[...], b_ref[...]).astype(o_ref.dtype)

    return pl.pallas_call(
        body, name=name, grid=(t // tm, k // tk), out_shape=jax.ShapeDtypeStruct((t, k), out_dtype),
        in_specs=[pl.BlockSpec((tm, n), lambda i, j: (i, 0)), pl.BlockSpec((tk, n), lambda i, j: (j, 0))],
        out_specs=pl.BlockSpec((tm, tk), lambda i, j: (i, j)),
        compiler_params=_params(("parallel", "parallel")))(a, b)


def _matmul_tn(name, a, b):
    t, m = a.shape
    n = b.shape[1]
    tm, tn, tt = _pick(m, COL_TILES[1:]), _pick(n, COL_TILES), _pick(t, ROW_TILES)

    def body(a_ref, b_ref, o_ref):
        @pl.when(pl.program_id(2) == 0)
        def _():
            o_ref[...] = jnp.zeros_like(o_ref)

        o_ref[...] += _dot_tn(a_ref[...], b_ref[...])

    return pl.pallas_call(
        body, name=name, grid=(m // tm, n // tn, t // tt), out_shape=jax.ShapeDtypeStruct((m, n), F32),
        in_specs=[pl.BlockSpec((tt, tm), lambda i, j, s: (s, i)), pl.BlockSpec((tt, tn), lambda i, j, s: (s, j))],
        out_specs=pl.BlockSpec((tm, tn), lambda i, j, s: (i, j)),
        compiler_params=_params(("parallel", "parallel", "arbitrary")))(a, b)


def _ffn_in_act(hf, w_fi):
    t, k = hf.shape
    f = w_fi.shape[1] // 2
    tm, tf = _pick(t, ROW_TILES[1:]), _pick(f, COL_TILES)
    nf = f // tf

    def body(a_ref, wg_ref, wu_ref, g_ref, u_ref, act_ref):
        a = a_ref[...]
        g, u = _dot_nn(a, wg_ref[...]), _dot_nn(a, wu_ref[...])
        g_ref[...] = g.astype(g_ref.dtype)
        u_ref[...] = u.astype(u_ref.dtype)
        act_ref[...] = (jax.nn.silu(g) * u).astype(act_ref.dtype)

    out = pl.BlockSpec((tm, tf), lambda i, j: (i, j))
    return pl.pallas_call(
        body, name="ffn_in_act", grid=(t // tm, nf), out_shape=[jax.ShapeDtypeStruct((t, f), MXU_DTYPE)] * 3,
        in_specs=[pl.BlockSpec((tm, k), lambda i, j: (i, 0)), pl.BlockSpec((k, tf), lambda i, j: (0, j)),
                  pl.BlockSpec((k, tf), lambda i, j: (0, j + nf))],
        out_specs=[out, out, out], compiler_params=_params(("parallel", "parallel")))(hf, w_fi, w_fi)


def _ffn_mid_bwd(dffn, w_fo, gate, up):
    t, d = dffn.shape
    f = w_fo.shape[0]
    tm, tf = _pick(t, ROW_TILES[1:]), _pick(f, COL_TILES)

    def body(a_ref, w_ref, g_ref, u_ref, dg_ref, du_ref):
        dact = _dot_nt(a_ref[...], w_ref[...])
        g, u = g_ref[...].astype(F32), u_ref[...].astype(F32)
        sg = jax.nn.sigmoid(g)
        dg_ref[...] = (dact * u * sg * (1.0 + g * (1.0 - sg))).astype(dg_ref.dtype)
        du_ref[...] = (dact * g * sg).astype(du_ref.dtype)

    blk = pl.BlockSpec((tm, tf), lambda i, j: (i, j))
    return pl.pallas_call(
        body, name="ffn_mid_bwd", grid=(t // tm, f // tf), out_shape=[jax.ShapeDtypeStruct((t, f), MXU_DTYPE)] * 2,
        in_specs=[pl.BlockSpec((tm, d), lambda i, j: (i, 0)), pl.BlockSpec((tf, d), lambda i, j: (j, 0)), blk, blk],
        out_specs=[blk, blk], compiler_params=_params(("parallel", "parallel")))(dffn, w_fo, gate, up)


def _ffn_in_dx(dgate, dup, w_fi):
    t, f = dgate.shape
    d = w_fi.shape[0]
    tm = _pick(t, ROW_TILES[1:])

    def body(g_ref, u_ref, w_ref, o_ref):
        o_ref[...] = _dot_nt(g_ref[...], w_ref[:, :f]) + _dot_nt(u_ref[...], w_ref[:, f:])

    blk = pl.BlockSpec((tm, f), lambda i: (i, 0))
    return pl.pallas_call(
        body, name="ffn_in_dx", grid=(t // tm,), out_shape=jax.ShapeDtypeStruct((t, d), F32),
        in_specs=[blk, blk, pl.BlockSpec(w_fi.shape, lambda i: (0, 0))], out_specs=pl.BlockSpec((tm, d), lambda i: (i, 0)),
        compiler_params=_params(("parallel",)))(dgate, dup, w_fi)


def _token_call(name, fn, tile, tok_ins, consts, tok_outs, acc_outs, into=None):
    n_rows = tok_ins[0][0].shape[0]
    tile = min(tile, n_rows)
    n_ti, n_c = len(tok_ins), len(consts)
    n_in = n_ti + n_c + (into is not None)
    n_to = len(tok_outs) + (into is not None)

    def body(*refs):
        ins = [r[...] for r in refs[:n_ti + n_c]]
        outs, accs = fn(*ins)
        for r, v in zip(refs[n_in:n_in + n_to], outs, strict=True):
            r[...] = v.astype(r.dtype)
        first = pl.program_id(0) == 0
        for r, v in zip(refs[n_in + n_to:], accs, strict=True):
            @pl.when(first)
            def _(r=r):
                r[...] = jnp.zeros_like(r)

            r[...] += jnp.broadcast_to(v, r.shape).astype(F32)

    in_specs = [pl.BlockSpec((tile, w), functools.partial(lambda i, c: (i, c), c=cb)) for _, w, cb in tok_ins]
    in_specs += [pl.BlockSpec(c.shape, lambda i: (0, 0)) for c in consts]
    out_shape = [jax.ShapeDtypeStruct((n_rows, w), dt) for w, dt in tok_outs]
    out_specs = [pl.BlockSpec((tile, w), lambda i: (i, 0)) for w, _ in tok_outs]
    operands = [a for a, _, _ in tok_ins] + list(consts)
    aliases = {}
    if into is not None:
        target, width, col_block = into
        in_specs.append(_ANY)
        operands.append(target)
        out_shape.append(jax.ShapeDtypeStruct(target.shape, target.dtype))
        out_specs.append(pl.BlockSpec((tile, width), lambda i: (i, col_block)))
        aliases = {n_in - 1: n_to - 1}
    out_shape += [jax.ShapeDtypeStruct(s, F32) for s in acc_outs]
    out_specs += [pl.BlockSpec(s, lambda i: (0, 0)) for s in acc_outs]
    res = pl.pallas_call(
        body, name=name, grid=(n_rows // tile,), out_shape=out_shape, in_specs=in_specs, out_specs=out_specs,
        input_output_aliases=aliases, compiler_params=_params(("arbitrary",)))(*operands)
    return res[:n_to], res[n_to:]


def _whole(a):
    return (a, a.shape[1], 0)


def _norm_in_tile(x, g):
    return _rms(x, g)


def _merge_tile(x, ya, ys, uc, gin, w_ao, w_sg, b_sg, w_co, b_gate, w_mo, g_ffn):
    d = x.shape[1]
    y_attn = _mm(ya, w_ao)
    pre = _mm(jax.nn.gelu(ys), w_sg) + b_sg
    y_ssm = pre[:, :d] * jax.nn.sigmoid(pre[:, d:])
    y_conv = _mm(uc, w_co)
    gates = jax.nn.sigmoid(gin + b_gate)
    merged = gates[:, :d] * y_attn + gates[:, d:2 * d] * y_ssm + gates[:, 2 * d:] * y_conv
    x1 = x + _mm(merged, w_mo)
    return x1, _rms(x1, g_ffn)


def _ple_tile(x2, p, w_pi, g_ple, w_pg):
    return x2 + jax.nn.sigmoid(_mm(_rms(x2, g_ple), w_pg)) * _mm(p, w_pi)


def _f32s(vals):
    return [v.astype(F32) for v in vals]


def _rope_tables(positions, inv_lane):
    def fn(pos, inv):
        ang = pos.astype(F32) * inv
        j = lax.broadcasted_iota(jnp.int32, ang.shape, 1) % HEAD_DIM
        c = jnp.where(j < ROPE_DIM, jnp.cos(ang), 1.0)
        s = jnp.sin(ang)
        s = jnp.where(j < ROPE_DIM // 2, -s, jnp.where(j < ROPE_DIM, s, 0.0))
        return [c, s], []

    (c, s), _ = _token_call("rope_tables", fn, 1024, [_whole(positions.reshape(-1, 1))], [inv_lane],
                            [(LANES, F32), (LANES, F32)], [])
    return c, s


def _swap_halves(t):
    n = t.shape[1]
    j = lax.broadcasted_iota(jnp.int32, t.shape, 1) % HEAD_DIM
    lower = pltpu.roll(t, n - ROPE_DIM // 2, 1)
    upper = jnp.where(j < ROPE_DIM, pltpu.roll(t, ROPE_DIM // 2, 1), 0.0)
    return jnp.where(j < ROPE_DIM // 2, lower, upper)


def _rope(t, c, s):
    return t * c + _swap_halves(t) * s


def _rope_t(dt, c, s):
    return dt * c + _swap_halves(dt * s)


def _tile4(a):
    return jnp.concatenate([a] * (Q_WIDTH // LANES), axis=1)


def _attn_mask(n):
    qi = lax.broadcasted_iota(jnp.int32, (GQA_GROUP * BLOCK, 2 * BLOCK), 0) % BLOCK
    kj = lax.broadcasted_iota(jnp.int32, (GQA_GROUP * BLOCK, 2 * BLOCK), 1)
    dist = qi + BLOCK - kj
    return (dist >= 0) & (dist < BLOCK) & ((n > 0) | (kj >= BLOCK))


def _attn_specs(n_seq):
    own = lambda w, blk: pl.BlockSpec((n_seq, BLOCK, w), lambda n: (0, n, blk))
    prev = lambda w, blk: pl.BlockSpec((n_seq, BLOCK, w), lambda n: (0, jnp.maximum(n - 1, 0), blk))
    return [own(ZQ_W, ZQ_BLK), own(ZKV_W, ZKV_BLK), prev(ZKV_W, ZKV_BLK), own(LANES, 0), own(LANES, 0), prev(LANES, 0),
            prev(LANES, 0), pl.BlockSpec((1, N_Q_HEADS), lambda n: (0, 0))]


def _by_seq(a, n_seq):
    return a.reshape(n_seq, a.shape[0] // n_seq, a.shape[1])


ATTN_SCALE = HEAD_DIM ** -0.5


def _stack_heads(t, kh):
    return jnp.concatenate([t[:, (kh * GQA_GROUP + g) * HEAD_DIM:(kh * GQA_GROUP + g + 1) * HEAD_DIM]
                            for g in range(GQA_GROUP)], axis=0)


def _stack_sinks(sink, kh):
    return jnp.concatenate([jnp.broadcast_to(sink[:, kh * GQA_GROUP + g:kh * GQA_GROUP + g + 1], (BLOCK, 1))
                            for g in range(GQA_GROUP)], axis=0)


def _attn_band(b, q_ref, kv_ref, kvp_ref, c_ref, s_ref, cp_ref, sp_ref):
    c, s = c_ref[b], s_ref[b]
    q = _rope(q_ref[b], _tile4(c), _tile4(s)) * ATTN_SCALE
    kv, kvp = kv_ref[b], kvp_ref[b]
    k = _rope(kv[:, :KV_WIDTH], c, s)
    kp = _rope(kvp[:, :KV_WIDTH], cp_ref[b], sp_ref[b])
    kb = jnp.concatenate([kp, k], axis=0)
    vb = jnp.concatenate([kvp[:, KV_WIDTH:], kv[:, KV_WIDTH:]], axis=0)
    return q, kb, vb


def _attention_fwd(z, ctab, stab, sinks, n_seq):
    t = z.shape[0]
    seq = t // n_seq

    def body(q_ref, kv_ref, kvp_ref, c_ref, s_ref, cp_ref, sp_ref, sink_ref, o_ref, lse_ref):
        mask = _attn_mask(pl.program_id(0))
        sink = sink_ref[...]
        lane = lax.broadcasted_iota(jnp.int32, (BLOCK, N_Q_HEADS), 1)
        for b in range(n_seq):
            q, kb, vb = _attn_band(b, q_ref, kv_ref, kvp_ref, c_ref, s_ref, cp_ref, sp_ref)
            lse_all = jnp.zeros((BLOCK, N_Q_HEADS), F32)
            for kh in range(N_KV_HEADS):
                sc = jnp.where(mask, _dot_nt(_stack_heads(q, kh), kb[:, kh * HEAD_DIM:(kh + 1) * HEAD_DIM]), NEG_INF)
                sk = _stack_sinks(sink, kh)
                m = jnp.maximum(jnp.max(sc, axis=-1, keepdims=True), sk)
                pr = jnp.exp(sc - m)
                den = jnp.sum(pr, axis=-1, keepdims=True) + jnp.exp(sk - m)
                out = _dot_nn(pr * (1.0 / den), vb[:, kh * HEAD_DIM:(kh + 1) * HEAD_DIM])
                lse = m + jnp.log(den)
                for g in range(GQA_GROUP):
                    h = kh * GQA_GROUP + g
                    o_ref[b, :, h * HEAD_DIM:(h + 1) * HEAD_DIM] = out[g * BLOCK:(g + 1) * BLOCK].astype(o_ref.dtype)
                    lse_all = jnp.where(lane == h, lse[g * BLOCK:(g + 1) * BLOCK], lse_all)
            lse_ref[b] = lse_all

    rows = lambda w: pl.BlockSpec((n_seq, BLOCK, w), lambda n: (0, n, 0))
    z3, c3, s3 = _by_seq(z, n_seq), _by_seq(ctab, n_seq), _by_seq(stab, n_seq)
    ya, lse = pl.pallas_call(
        body, name="attn_fwd", grid=(seq // BLOCK,),
        out_shape=[jax.ShapeDtypeStruct((n_seq, seq, Q_WIDTH), MXU_DTYPE), jax.ShapeDtypeStruct((n_seq, seq, N_Q_HEADS), F32)],
        in_specs=_attn_specs(n_seq), out_specs=[rows(Q_WIDTH), rows(N_Q_HEADS)],
        compiler_params=_params(("parallel",)))(z3, z3, z3, c3, s3, c3, s3, sinks)
    return ya.reshape(t, Q_WIDTH), lse.reshape(t, N_Q_HEADS)


def _attention_bwd(z, ctab, stab, sinks, ya, lse, dya, dz, n_seq):
    t = z.shape[0]
    seq = t // n_seq

    def body(q_ref, kv_ref, kvp_ref, c_ref, s_ref, cp_ref, sp_ref, sink_ref, o_ref, lse_ref, do_ref, _,
             dq_ref, dkv_ref, dkvp_ref, dsink_ref):
        mask = _attn_mask(pl.program_id(0))
        sink = sink_ref[...]
        lane = lax.broadcasted_iota(jnp.int32, (1, N_Q_HEADS), 1)
        dsink = jnp.zeros((1, N_Q_HEADS), F32)
        for b in range(n_seq):
            q, kb, vb = _attn_band(b, q_ref, kv_ref, kvp_ref, c_ref, s_ref, cp_ref, sp_ref)
            lse_all = lse_ref[b]
            o = o_ref[b].astype(F32)
            do = do_ref[b].astype(F32)
            dq_parts = []
            dk_parts, dv_parts = [], []
            for kh in range(N_KV_HEADS):
                kbh = kb[:, kh * HEAD_DIM:(kh + 1) * HEAD_DIM]
                vbh = vb[:, kh * HEAD_DIM:(kh + 1) * HEAD_DIM]
                qs, dos = _stack_heads(q, kh), _stack_heads(do, kh)
                lse = jnp.concatenate([lse_all[:, kh * GQA_GROUP + g:kh * GQA_GROUP + g + 1] for g in range(GQA_GROUP)], axis=0)
                pr = jnp.exp(jnp.where(mask, _dot_nt(qs, kbh), NEG_INF) - lse)
                delta = jnp.sum(dos * _stack_heads(o, kh), axis=-1, keepdims=True)
                ds = pr * (_dot_nt(dos, vbh) - delta)
                dqs = _dot_nn(ds, kbh)
                dq_parts += [dqs[g * BLOCK:(g + 1) * BLOCK] for g in range(GQA_GROUP)]
                dk_parts.append(_dot_tn(ds, qs))
                dv_parts.append(_dot_tn(pr, dos))
                dsk = jnp.exp(_stack_sinks(sink, kh) - lse) * delta
                for g in range(GQA_GROUP):
                    dsink = dsink + jnp.where(lane == kh * GQA_GROUP + g, -jnp.sum(dsk[g * BLOCK:(g + 1) * BLOCK]), 0.0)
            c, s = c_ref[b], s_ref[b]
            dq_ref[b] = _rope_t(jnp.concatenate(dq_parts, axis=1) * ATTN_SCALE, _tile4(c), _tile4(s)).astype(dq_ref.dtype)
            dk = jnp.concatenate(dk_parts, axis=1)
            dv = jnp.concatenate(dv_parts, axis=1)
            dkv_ref[b, :, :KV_WIDTH] = _rope_t(dk[BLOCK:], c, s)
            dkv_ref[b, :, KV_WIDTH:] = dv[BLOCK:]
            dkvp_ref[b, :, :KV_WIDTH] = _rope_t(dk[:BLOCK], cp_ref[b], sp_ref[b])
            dkvp_ref[b, :, KV_WIDTH:] = dv[:BLOCK]

        @pl.when(pl.program_id(0) == 0)
        def _():
            dsink_ref[...] = jnp.zeros_like(dsink_ref)

        dsink_ref[...] += dsink

    rows = lambda w: pl.BlockSpec((n_seq, BLOCK, w), lambda n: (0, n, 0))
    by_seq = lambda a: _by_seq(a, n_seq)
    z3, c3, s3 = by_seq(z), by_seq(ctab), by_seq(stab)
    dz, dkv, dkvp, dsink = pl.pallas_call(
        body, name="attn_bwd", grid=(seq // BLOCK,),
        out_shape=[jax.ShapeDtypeStruct((n_seq, seq, Z_WIDTH), dz.dtype), jax.ShapeDtypeStruct((n_seq, seq, ZKV_W), F32),
                   jax.ShapeDtypeStruct((n_seq, seq, ZKV_W), F32), jax.ShapeDtypeStruct((1, N_Q_HEADS), F32)],
        in_specs=_attn_specs(n_seq) + [rows(Q_WIDTH), rows(N_Q_HEADS), rows(Q_WIDTH), _ANY],
        out_specs=[pl.BlockSpec((n_seq, BLOCK, ZQ_W), lambda n: (0, n, ZQ_BLK)), rows(ZKV_W), rows(ZKV_W),
                   pl.BlockSpec((1, N_Q_HEADS), lambda n: (0, 0))],
        input_output_aliases={11: 0},
        compiler_params=_params(("arbitrary",)))(z3, z3, z3, c3, s3, c3, s3, sinks, by_seq(ya), by_seq(lse), by_seq(dya), by_seq(dz))
    return dz.reshape(t, Z_WIDTH), dkv.reshape(t, ZKV_W), dkvp.reshape(t, ZKV_W), dsink


def _kv_combine(dkv, dkvp, dz, n_seq):
    t = dkv.shape[0]
    nb = t // n_seq // BLOCK

    def body(dkv_ref, dkvn_ref, _, o_ref):
        last = pl.program_id(1) == nb - 1
        o_ref[...] = (dkv_ref[...] + jnp.where(last, 0.0, dkvn_ref[...])).astype(o_ref.dtype)

    return pl.pallas_call(
        body, name="kv_combine", grid=(n_seq, nb), out_shape=jax.ShapeDtypeStruct(dz.shape, dz.dtype),
        in_specs=[pl.BlockSpec((BLOCK, ZKV_W), lambda b, n: (b * nb + n, 0)),
                  pl.BlockSpec((BLOCK, ZKV_W), lambda b, n: (b * nb + jnp.minimum(n + 1, nb - 1), 0)), _ANY],
        out_specs=pl.BlockSpec((BLOCK, ZKV_W), lambda b, n: (b * nb + n, ZKV_BLK)), input_output_aliases={2: 0},
        compiler_params=_params(("parallel", "parallel")))(dkv, dkvp, dz)


def _ssm_coeff_tile(lam_re, lam_im, log_dt):
    lr = jnp.minimum(lam_re, -1e-4)
    dt = jnp.exp(log_dt)
    mag = jnp.exp(lr * dt)
    a_re = mag * jnp.cos(lam_im * dt)
    a_im = mag * jnp.sin(lam_im * dt)
    den = lr * lr + lam_im * lam_im
    x_re = a_re - 1.0
    f_re = (x_re * lr + a_im * lam_im) / den
    f_im = (a_im * lr - x_re * lam_im) / den
    return a_re, a_im, f_re, f_im


def _ssm_coeffs(lam_re, lam_im, log_dt):
    def body(lr_ref, li_ref, dt_ref, *o_refs):
        for r, v in zip(o_refs, _ssm_coeff_tile(lr_ref[...], li_ref[...], dt_ref[...]), strict=True):
            r[...] = v

    return pl.pallas_call(body, name="ssm_coeffs", out_shape=[jax.ShapeDtypeStruct(lam_re.shape, F32)] * 4)(
        lam_re, lam_im, log_dt)


def _ssm_coeffs_bwd(lam_re, lam_im, log_dt, cts):
    def body(lr_ref, li_ref, dt_ref, c0, c1, c2, c3, dlr_ref, dli_ref, ddt_ref):
        _, vjp = jax.vjp(_ssm_coeff_tile, lr_ref[...], li_ref[...], dt_ref[...])
        dlr, dli, ddt = vjp((c0[...], c1[...], c2[...], c3[...]))
        dlr_ref[...] = dlr
        dli_ref[...] = dli
        ddt_ref[...] = ddt

    return pl.pallas_call(
        body, name="ssm_coeffs_bwd",
        out_shape=[jax.ShapeDtypeStruct(lam_re.shape, F32)] * 2 + [jax.ShapeDtypeStruct(log_dt.shape, F32)])(
        lam_re, lam_im, log_dt, *cts)


def _ssm_chunk(t):
    return _pick(t, (256, 128))


def _ssm_fwd(z, bmat, a_row, f_row, cmat, d_row, n_seq):
    t = z.shape[0]
    seq = t // n_seq
    lc = _ssm_chunk(seq)
    nc = seq // lc
    n2 = 2 * SSM_LANES

    def body(u_ref, b_ref, a_ref, f_ref, c_ref, d_ref, y_ref, s_ref, bu_ref, st_ref):
        @pl.when(pl.program_id(1) == 0)
        def _():
            st_ref[...] = jnp.zeros_like(st_ref)

        u = u_ref[...]
        proj = _dot_nn(u, b_ref[...])
        fr, fi = f_ref[:, :SSM_LANES], f_ref[:, SSM_LANES:]
        pr, pi = proj[:, :SSM_LANES], proj[:, SSM_LANES:]
        bu_ref[:, :SSM_LANES] = fr * pr - fi * pi
        bu_ref[:, SSM_LANES:] = fr * pi + fi * pr
        ar, ai = a_ref[:, :SSM_LANES], a_ref[:, SSM_LANES:]

        def step(i, carry):
            sr, si = carry
            nr = ar * sr - ai * si + bu_ref[pl.ds(i, 1), pl.ds(0, SSM_LANES)]
            ni = ar * si + ai * sr + bu_ref[pl.ds(i, 1), pl.ds(SSM_LANES, SSM_LANES)]
            s_ref[pl.ds(i, 1), pl.ds(0, SSM_LANES)] = nr
            s_ref[pl.ds(i, 1), pl.ds(SSM_LANES, SSM_LANES)] = ni
            return nr, ni

        sr, si = lax.fori_loop(0, lc, step, (st_ref[0:1, :SSM_LANES], st_ref[0:1, SSM_LANES:]), unroll=8)
        st_ref[0:1, :SSM_LANES] = sr
        st_ref[0:1, SSM_LANES:] = si
        y_ref[...] = _dot_nn(s_ref[...], c_ref[...]) + d_ref[...] * u

    const = lambda shape: pl.BlockSpec(shape, lambda b, c: (0, 0))
    return pl.pallas_call(
        body, name="ssm_fwd", grid=(n_seq, nc),
        out_shape=[jax.ShapeDtypeStruct((t, SSM_WIDTH), F32), jax.ShapeDtypeStruct((t, n2), F32)],
        in_specs=[pl.BlockSpec((lc, ZS_W), lambda b, c: (b * nc + c, ZS_BLK)), const((SSM_WIDTH, n2)), const((1, n2)),
                  const((1, n2)), const((n2, SSM_WIDTH)), const((1, SSM_WIDTH))],
        out_specs=[pl.BlockSpec((lc, SSM_WIDTH), lambda b, c: (b * nc + c, 0)),
                   pl.BlockSpec((lc, n2), lambda b, c: (b * nc + c, 0))],
        scratch_shapes=[pltpu.VMEM((lc, n2), F32), pltpu.VMEM((8, n2), F32)],
        compiler_params=_params(("arbitrary", "arbitrary")))(z, bmat, a_row, f_row, cmat, d_row)


def _ssm_bwd(z, states, dy, bmat, a_row, f_row, cmat, d_row, dz, n_seq):
    t = z.shape[0]
    seq = t // n_seq
    lc = _ssm_chunk(seq)
    nc = seq // lc
    n2 = 2 * SSM_LANES

    def body(dy_ref, u_ref, s_ref, b_ref, a_ref, f_ref, c_ref, d_ref, _,
             du_ref, db_ref, dc_ref, da_ref, df_ref, dd_ref, g_ref, carry_ref):
        @pl.when((pl.program_id(0) == 0) & (pl.program_id(1) == 0))
        def _():
            for r in (db_ref, dc_ref, da_ref, df_ref, dd_ref):
                r[...] = jnp.zeros_like(r)

        @pl.when(pl.program_id(1) == 0)
        def _():
            carry_ref[...] = jnp.zeros_like(carry_ref)

        dy, u, st = dy_ref[...], u_ref[...], s_ref[...]
        g_ref[0:lc, :] = _dot_nt(dy, c_ref[...])
        g_ref[lc:lc + 8, :] = carry_ref[...]
        dc_ref[...] += _dot_tn(st, dy)
        dd_ref[...] += jnp.sum(dy * u, axis=0, keepdims=True)
        ar, ai = a_ref[:, :SSM_LANES], a_ref[:, SSM_LANES:]

        def step(i, carry):
            gr, gi = carry
            r = lc - 1 - i
            nr = g_ref[pl.ds(r, 1), pl.ds(0, SSM_LANES)] + ar * gr + ai * gi
            ni = g_ref[pl.ds(r, 1), pl.ds(SSM_LANES, SSM_LANES)] - ai * gr + ar * gi
            g_ref[pl.ds(r, 1), pl.ds(0, SSM_LANES)] = nr
            g_ref[pl.ds(r, 1), pl.ds(SSM_LANES, SSM_LANES)] = ni
            return nr, ni

        gr, gi = lax.fori_loop(0, lc, step, (carry_ref[0:1, :SSM_LANES], carry_ref[0:1, SSM_LANES:]), unroll=8)
        carry_ref[0:1, :SSM_LANES] = gr
        carry_ref[0:1, SSM_LANES:] = gi
        sr, si = st[:, :SSM_LANES], st[:, SSM_LANES:]
        gnr, gni = g_ref[pl.ds(1, lc), pl.ds(0, SSM_LANES)], g_ref[pl.ds(1, lc), pl.ds(SSM_LANES, SSM_LANES)]
        da_ref[:, :SSM_LANES] += jnp.sum(gnr * sr + gni * si, axis=0, keepdims=True)
        da_ref[:, SSM_LANES:] += jnp.sum(gni * sr - gnr * si, axis=0, keepdims=True)
        gr_all, gi_all = g_ref[0:lc, :SSM_LANES], g_ref[0:lc, SSM_LANES:]
        proj = _dot_nn(u, b_ref[...])
        pr, pi = proj[:, :SSM_LANES], proj[:, SSM_LANES:]
        df_ref[:, :SSM_LANES] += jnp.sum(gr_all * pr + gi_all * pi, axis=0, keepdims=True)
        df_ref[:, SSM_LANES:] += jnp.sum(gi_all * pr - gr_all * pi, axis=0, keepdims=True)
        fr, fi = f_ref[:, :SSM_LANES], f_ref[:, SSM_LANES:]
        dproj = jnp.concatenate([fr * gr_all + fi * gi_all, fr * gi_all - fi * gr_all], axis=1).astype(MXU_DTYPE)
        du_ref[...] = (_dot_nt(dproj, b_ref[...]) + d_ref[...] * dy).astype(du_ref.dtype)
        db_ref[...] += _dot_tn(u, dproj)

    const = lambda shape: pl.BlockSpec(shape, lambda b, c: (0, 0))
    rows = lambda w, cb: pl.BlockSpec((lc, w), functools.partial(lambda b, c, cb: (b * nc + nc - 1 - c, cb), cb=cb))
    return pl.pallas_call(
        body, name="ssm_bwd", grid=(n_seq, nc),
        out_shape=[jax.ShapeDtypeStruct(dz.shape, dz.dtype), jax.ShapeDtypeStruct((SSM_WIDTH, n2), F32),
                   jax.ShapeDtypeStruct((n2, SSM_WIDTH), F32), jax.ShapeDtypeStruct((1, n2), F32),
                   jax.ShapeDtypeStruct((1, n2), F32), jax.ShapeDtypeStruct((1, SSM_WIDTH), F32)],
        in_specs=[rows(SSM_WIDTH, 0), rows(ZS_W, ZS_BLK), rows(n2, 0), const((SSM_WIDTH, n2)), const((1, n2)),
                  const((1, n2)), const((n2, SSM_WIDTH)), const((1, SSM_WIDTH)), _ANY],
        out_specs=[rows(ZS_W, ZS_BLK), const((SSM_WIDTH, n2)), const((n2, SSM_WIDTH)), const((1, n2)), const((1, n2)),
                   const((1, SSM_WIDTH))],
        input_output_aliases={8: 0},
        scratch_shapes=[pltpu.VMEM((lc + 8, n2), F32), pltpu.VMEM((8, n2), F32)],
        compiler_params=_params(("arbitrary", "arbitrary")))(dy, z, states, bmat, a_row, f_row, cmat, d_row, dz)


def _conv_chunk(t):
    return _pick(t, (512, 256, 128))


def _glu(c):
    return c[:, :CONV_WIDTH] * jax.nn.sigmoid(c[:, CONV_WIDTH:])


def _conv_post_tile(v, g, b):
    mu = jnp.mean(v, axis=-1, keepdims=True)
    var = jnp.mean(jnp.square(v - mu), axis=-1, keepdims=True)
    return jax.nn.silu((v - mu) * lax.rsqrt(var + EPS) * g + b)


def _conv_specs(lc, nc):
    per = lc // CONV_HALO
    return [pl.BlockSpec((lc, ZC_W), lambda b, c: (b * nc + c, ZC_BLK)),
            pl.BlockSpec((CONV_HALO, ZC_W), lambda b, c: (jnp.maximum((b * nc + c) * per - 1, 0), ZC_BLK))]


def _conv_fill(c_ref, cp_ref, ue_ref, lc):
    ue_ref[0:CONV_HALO, :] = jnp.where(pl.program_id(1) > 0, _glu(cp_ref[...]), 0.0)
    ue_ref[CONV_HALO:CONV_HALO + lc, :] = _glu(c_ref[...])


def _conv_apply(ue_ref, w_ref, b_ref, lc):
    acc = jnp.zeros((lc, CONV_WIDTH), F32) + b_ref[...]
    for k in range(CONV_K):
        acc = acc + w_ref[k:k + 1, :] * ue_ref[pl.ds(k + CONV_HALO - CONV_K + 1, lc), :]
    return acc


def _conv_fwd(z, dw_w, dw_b, ln_g, ln_b, n_seq):
    t = z.shape[0]
    seq = t // n_seq
    lc = _conv_chunk(seq)
    nc = seq // lc

    def body(c_ref, cp_ref, w_ref, b_ref, g_ref, lb_ref, o_ref, ue_ref):
        _conv_fill(c_ref, cp_ref, ue_ref, lc)
        o_ref[...] = _conv_post_tile(_conv_apply(ue_ref, w_ref, b_ref, lc), g_ref[...], lb_ref[...]).astype(o_ref.dtype)

    const = lambda a: pl.BlockSpec(a.shape, lambda b, c: (0, 0))
    return pl.pallas_call(
        body, name="conv_fwd", grid=(n_seq, nc), out_shape=jax.ShapeDtypeStruct((t, CONV_WIDTH), MXU_DTYPE),
        in_specs=_conv_specs(lc, nc) + [const(dw_w), const(dw_b), const(ln_g), const(ln_b)],
        out_specs=pl.BlockSpec((lc, CONV_WIDTH), lambda b, c: (b * nc + c, 0)),
        scratch_shapes=[pltpu.VMEM((CONV_HALO + lc, CONV_WIDTH), F32)],
        compiler_params=_params(("parallel", "parallel")))(z, z, dw_w, dw_b, ln_g, ln_b)


def _conv_bwd_post(z, duc, dw_w, dw_b, ln_g, ln_b, n_seq):
    t = z.shape[0]
    seq = t // n_seq
    lc = _conv_chunk(seq)
    nc = seq // lc

    def body(c_ref, cp_ref, duc_ref, w_ref, b_ref, g_ref, lb_ref, dv_ref, dg_ref, dlb_ref, db_ref, ue_ref):
        @pl.when((pl.program_id(0) == 0) & (pl.program_id(1) == 0))
        def _():
            for r in (dg_ref, dlb_ref, db_ref):
                r[...] = jnp.zeros_like(r)

        _conv_fill(c_ref, cp_ref, ue_ref, lc)
        _, vjp = jax.vjp(_conv_post_tile, _conv_apply(ue_ref, w_ref, b_ref, lc), g_ref[...], lb_ref[...])
        dv, dg, dlb = vjp(duc_ref[...])
        dv_ref[...] = dv
        dg_ref[...] += dg
        dlb_ref[...] += dlb
        db_ref[...] += jnp.sum(dv, axis=0, keepdims=True)

    const = lambda a: pl.BlockSpec(a.shape, lambda b, c: (0, 0))
    vec = jax.ShapeDtypeStruct((1, CONV_WIDTH), F32)
    return pl.pallas_call(
        body, name="conv_bwd_post", grid=(n_seq, nc), out_shape=[jax.ShapeDtypeStruct((t, CONV_WIDTH), F32), vec, vec, vec],
        in_specs=_conv_specs(lc, nc) + [pl.BlockSpec((lc, CONV_WIDTH), lambda b, c: (b * nc + c, 0)),
                                       const(dw_w), const(dw_b), const(ln_g), const(ln_b)],
        out_specs=[pl.BlockSpec((lc, CONV_WIDTH), lambda b, c: (b * nc + c, 0))] + [const(dw_b)] * 3,
        scratch_shapes=[pltpu.VMEM((CONV_HALO + lc, CONV_WIDTH), F32)],
        compiler_params=_params(("arbitrary", "arbitrary")))(z, z, duc, dw_w, dw_b, ln_g, ln_b)


def _conv_bwd_taps(z, dv, dw_w, dz, n_seq):
    t = z.shape[0]
    seq = t // n_seq
    lc = _conv_chunk(seq)
    nc = seq // lc
    per = lc // CONV_HALO
    n_halo = t // CONV_HALO

    def body(c_ref, cp_ref, dv_ref, dvn_ref, w_ref, _, dc_ref, dw_ref, ue_ref, dve_ref):
        @pl.when((pl.program_id(0) == 0) & (pl.program_id(1) == 0))
        def _():
            dw_ref[...] = jnp.zeros_like(dw_ref)

        _conv_fill(c_ref, cp_ref, ue_ref, lc)
        dv = dv_ref[...]
        dve_ref[0:lc, :] = dv
        dve_ref[lc:lc + CONV_HALO, :] = jnp.where(pl.program_id(1) < nc - 1, dvn_ref[...], 0.0)
        du = jnp.zeros((lc, CONV_WIDTH), F32)
        for k in range(CONV_K):
            du = du + w_ref[k:k + 1, :] * dve_ref[pl.ds(CONV_K - 1 - k, lc), :]
            dw_ref[k:k + 1, :] += jnp.sum(dv * ue_ref[pl.ds(k + CONV_HALO - CONV_K + 1, lc), :], axis=0, keepdims=True)
        c = c_ref[...]
        a, sg = c[:, :CONV_WIDTH], jax.nn.sigmoid(c[:, CONV_WIDTH:])
        dc_ref[:, :CONV_WIDTH] = (du * sg).astype(dc_ref.dtype)
        dc_ref[:, CONV_WIDTH:] = (du * a * sg * (1.0 - sg)).astype(dc_ref.dtype)

    return pl.pallas_call(
        body, name="conv_bwd_taps", grid=(n_seq, nc),
        out_shape=[jax.ShapeDtypeStruct(dz.shape, dz.dtype), jax.ShapeDtypeStruct((CONV_HALO, CONV_WIDTH), F32)],
        in_specs=_conv_specs(lc, nc) + [
            pl.BlockSpec((lc, CONV_WIDTH), lambda b, c: (b * nc + c, 0)),
            pl.BlockSpec((CONV_HALO, CONV_WIDTH), lambda b, c: (jnp.minimum((b * nc + c + 1) * per, n_halo - 1), 0)),
            pl.BlockSpec(dw_w.shape, lambda b, c: (0, 0)), _ANY],
        out_specs=[pl.BlockSpec((lc, ZC_W), lambda b, c: (b * nc + c, ZC_BLK)),
                   pl.BlockSpec((CONV_HALO, CONV_WIDTH), lambda b, c: (0, 0))],
        input_output_aliases={5: 0},
        scratch_shapes=[pltpu.VMEM((CONV_HALO + lc, CONV_WIDTH), F32), pltpu.VMEM((lc + CONV_HALO, CONV_WIDTH), F32)],
        compiler_params=_params(("arbitrary", "arbitrary")))(z, z, dv, dv, dw_w, dz)


def _row(v):
    return v.reshape(1, -1)


def _ssm_mats(b_re, b_im, c_re, c_im):
    eye = jnp.eye(SSM_GROUPS, dtype=bool)
    bm = jnp.stack([b_re, b_im]).transpose(1, 3, 0, 2)[:, :, :, None, :]
    bmat = jnp.where(eye[:, None, None, :, None], bm, 0.0).reshape(SSM_WIDTH, 2 * SSM_LANES)
    cm = jnp.stack([c_re, -c_im]).transpose(0, 1, 3, 2)[:, :, :, None, :]
    cmat = jnp.where(eye[None, :, None, :, None], cm, 0.0).reshape(2 * SSM_LANES, SSM_WIDTH)
    return bmat.astype(MXU_DTYPE), cmat.astype(MXU_DTYPE)


def _ssm_mats_t(dbmat, dcmat):
    eye = jnp.eye(SSM_GROUPS, dtype=bool)
    db = dbmat.reshape(SSM_GROUPS, SSM_GROUP, 2, SSM_GROUPS, SSM_STATE)
    db = jnp.sum(jnp.where(eye[:, None, None, :, None], db, 0.0), axis=3).transpose(2, 0, 3, 1)
    dc = dcmat.reshape(2, SSM_GROUPS, SSM_STATE, SSM_GROUPS, SSM_GROUP)
    dc = jnp.sum(jnp.where(eye[None, :, None, :, None], dc, 0.0), axis=3).transpose(0, 1, 3, 2)
    return db[0], db[1], dc[0], -dc[1]


def _layer_fwd(x, p, w, sp, ctab, stab, n_seq):
    (h,), _ = _token_call("norm_in", lambda x, g: ([_norm_in_tile(x, g)], []), 512, [_whole(x)], [sp["mix_norm_g"]],
                          [(x.shape[1], MXU_DTYPE)], [])
    z = _matmul_nn("mm_in", h, w["w_in"], F32)
    ya, lse = _attention_fwd(z, ctab, stab, sp["attn_sinks"], n_seq)
    ys, states = _ssm_fwd(z, sp["bmat"], sp["a_row"], sp["f_row"], sp["cmat"], sp["ssm_d"], n_seq)
    uc = _conv_fwd(z, w["conv_dw_w"], sp["conv_dw_b"], sp["conv_norm_g"], sp["conv_norm_b"], n_seq)
    merge_consts = [w["w_attn_out"], w["w_ssm_glu"], sp["b_ssm_glu"], w["w_conv_out"], sp["b_gate"], w["w_mix_out"],
                    sp["ffn_norm_g"]]
    (x1, hf), _ = _token_call("merge", lambda *a: (list(_merge_tile(*_f32s(a))), []), 512,
                              [_whole(x), _whole(ya), _whole(ys), _whole(uc), (z, ZG_W, 0)], merge_consts,
                              [(x.shape[1], F32), (x.shape[1], MXU_DTYPE)], [])
    gate, up, act = _ffn_in_act(hf, w["w_ffn_in"])
    ffn = _matmul_nn("mm_ffn_out", act, w["w_ffn_out"], F32)

    def ple_fn(x1, ffn, p, w_pi, g_ple, w_pg):
        x2 = x1 + ffn
        return [x2, _ple_tile(x2, p, w_pi.astype(F32), g_ple, w_pg.astype(F32))], []

    (x2, x3), _ = _token_call("ple", ple_fn, 512, [_whole(x1), _whole(ffn), _whole(p)],
                              [w["w_ple_in"], sp["ple_norm_g"], w["w_ple_gate"]], [(x.shape[1], F32)] * 2, [])
    saved = dict(x=x, h=h, z=z, ya=ya, lse=lse, ys=ys, states=states, uc=uc, hf=hf, gate=gate, up=up, act=act, x2=x2, p=p)
    return x3, saved


def _layer_bwd(dx3, sv, w, sp, ctab, stab, n_seq):
    d = dx3.shape[1]
    gw, gs = {}, {}

    def ple_bwd(x2, p, dx3, w_pi, g_ple, w_pg):
        _, vjp = jax.vjp(lambda x2, w_pi, g_ple, w_pg: _ple_tile(x2, p, w_pi, g_ple, w_pg), x2, w_pi.astype(F32), g_ple,
                         w_pg.astype(F32))
        dx2, dw_pi, dg_ple, dw_pg = vjp(dx3)
        return [dx2, dx2], [dw_pi, dg_ple, dw_pg]

    (dx2, dffn), (gw["w_ple_in"], gs["ple_norm_g"], gw["w_ple_gate"]) = _token_call(
        "ple_bwd", ple_bwd, 512, [_whole(sv["x2"]), _whole(sv["p"]), _whole(dx3)],
        [w["w_ple_in"], sp["ple_norm_g"], w["w_ple_gate"]], [(d, F32), (d, MXU_DTYPE)],
        [w["w_ple_in"].shape, (1, d), w["w_ple_gate"].shape])

    dgate, dup = _ffn_mid_bwd(dffn, w["w_ffn_out"], sv["gate"], sv["up"])
    gw["w_ffn_out"] = _matmul_tn("mm_ffn_out_dw", sv["act"], dffn)
    dhf = _ffn_in_dx(dgate, dup, w["w_ffn_in"])
    gw["w_ffn_in"] = jnp.concatenate([_matmul_tn("mm_ffn_gate_dw", sv["hf"], dgate), _matmul_tn("mm_ffn_up_dw", sv["hf"], dup)],
                                     axis=1)

    def merge_bwd(x, ya, ys, uc, gin, dx1, dhf, *consts):
        consts = _f32s(consts)
        _, vjp = jax.vjp(_merge_tile, *_f32s((x, ya, ys, uc, gin)), *consts)
        g = vjp((dx1, dhf))
        return list(g[:5]), list(g[5:])

    merge_consts = [w["w_attn_out"], w["w_ssm_glu"], sp["b_ssm_glu"], w["w_conv_out"], sp["b_gate"], w["w_mix_out"],
                    sp["ffn_norm_g"]]
    dz = lax.empty(sv["z"].shape, MXU_DTYPE)
    (dx_res, dya, dys, duc, dz), macc = _token_call(
        "merge_bwd", merge_bwd, 256,
        [_whole(sv["x"]), _whole(sv["ya"]), _whole(sv["ys"]), _whole(sv["uc"]), (sv["z"], ZG_W, 0), _whole(dx2), _whole(dhf)],
        merge_consts, [(d, F32), (Q_WIDTH, MXU_DTYPE), (SSM_WIDTH, F32), (CONV_WIDTH, F32)],
        [c.shape for c in merge_consts], into=(dz, ZG_W, 0))
    gw["w_attn_out"], gw["w_ssm_glu"], gs["b_ssm_glu"], gw["w_conv_out"], gs["b_gate"], gw["w_mix_out"], gs["ffn_norm_g"] = macc

    dv, gs["conv_norm_g"], gs["conv_norm_b"], gs["conv_dw_b"] = _conv_bwd_post(
        sv["z"], duc, w["conv_dw_w"], sp["conv_dw_b"], sp["conv_norm_g"], sp["conv_norm_b"], n_seq)
    dz, dw_taps = _conv_bwd_taps(sv["z"], dv, w["conv_dw_w"], dz, n_seq)
    gw["conv_dw_w"] = dw_taps[:CONV_K]

    dz, gs["bmat"], gs["cmat"], gs["a_row"], gs["f_row"], gs["ssm_d"] = _ssm_bwd(
        sv["z"], sv["states"], dys, sp["bmat"], sp["a_row"], sp["f_row"], sp["cmat"], sp["ssm_d"], dz, n_seq)

    dz, dkv, dkvp, gs["attn_sinks"] = _attention_bwd(sv["z"], ctab, stab, sp["attn_sinks"], sv["ya"], sv["lse"], dya, dz, n_seq)
    dz = _kv_combine(dkv, dkvp, dz, n_seq)
    dh = _matmul_nt("mm_in_dx", dz, w["w_in"], F32)
    gw["w_in"] = _matmul_tn("mm_in_dw", sv["h"], dz)

    def norm_bwd(x, dh, dx_res, g):
        _, vjp = jax.vjp(_norm_in_tile, x, g)
        dx, dg = vjp(dh)
        return [dx + dx_res], [dg]

    (dx,), (gs["mix_norm_g"],) = _token_call("norm_in_bwd", norm_bwd, 512, [_whole(sv["x"]), _whole(dh), _whole(dx_res)],
                                             [sp["mix_norm_g"]], [(d, F32)], [(1, d)])
    return dx, gw, gs


def _loss_and_grad(x, target, g):
    def fn(x, tgt, g):
        def f(x, g):
            err = _rms(x, g) - tgt
            return 0.5 * jnp.mean(err * err, axis=-1, keepdims=True)

        per_token, vjp = jax.vjp(f, x, g)
        dx, dg = vjp(jnp.ones_like(per_token))
        return [dx], [jnp.sum(per_token, axis=0, keepdims=True), dg]

    (dx,), (loss, dg) = _token_call("loss", fn, 512, [_whole(x), _whole(target)], [g], [(x.shape[1], F32)],
                                    [(8, LANES), (1, x.shape[1])])
    return loss[0, 0], dx, dg


def _mesh_place():
    return lax.axis_index("x"), lax.axis_index("y"), lax.axis_index("c")


def _flip(v, bit):
    return 1 - v if bit else v


_MESH = pl.DeviceIdType.MESH


def _all_gather(name, xs):
    n = len(xs)

    def body(*refs):
        x_refs, out_refs = refs[:n], refs[n:2 * n]
        send_sems, recv_sems, local_sems = refs[2 * n:]
        mx, my, mc = _mesh_place()
        me, sibling = (mx, my, mc), (mx, my, 1 - mc)
        chips = [(1 - mx, my), (mx, 1 - my), (1 - mx, 1 - my)]

        def slot(a, px, py, pc):
            return out_refs[a].at[4 * px + 2 * py + pc]

        def copy(a, k, block, to, src=None):
            return pltpu.make_async_remote_copy(
                src_ref=slot(a, *block) if src is None else src, dst_ref=slot(a, *block), send_sem=send_sems.at[7 * a + k],
                recv_sem=recv_sems.at[7 * a + k], device_id=to, device_id_type=_MESH)

        mine = [pltpu.make_async_copy(x_refs[a], slot(a, *me), local_sems.at[a]) for a in range(n)]
        for cp in mine:
            cp.start()
        first = [copy(a, 0, me, sibling, src=x_refs[a]) for a in range(n)]
        first += [copy(a, 1 + j, me, (*chip, mc), src=x_refs[a]) for j, chip in enumerate(chips) for a in range(n)]
        for cp in first:
            cp.start()
        passed = []
        for j, chip in enumerate(chips):
            for a in range(n):
                copy(a, 1 + j, (*chip, mc), me).wait_recv()
                passed.append(copy(a, 4 + j, (*chip, mc), sibling))
                passed[-1].start()
        for a in range(n):
            copy(a, 0, sibling, me).wait_recv()
            for j, chip in enumerate(chips):
                copy(a, 4 + j, (*chip, 1 - mc), me).wait_recv()
        for cp in first + passed:
            cp.wait_send()
        for cp in mine:
            cp.wait()

    return pl.pallas_call(
        body, name=name, out_shape=[jax.ShapeDtypeStruct((N_DEV,) + x.shape, x.dtype) for x in xs], in_specs=[_ANY] * n,
        out_specs=[_ANY] * n,
        scratch_shapes=[pltpu.SemaphoreType.DMA((7 * n,)), pltpu.SemaphoreType.DMA((7 * n,)), pltpu.SemaphoreType.DMA((n,))])(*xs)


def _direct_copies(kind, src_refs, land_refs, send_sems, recv_sems, local_sems):
    mx, my, mc = _mesh_place()
    me = 4 * mx + 2 * my + mc
    n = len(src_refs)
    own = [pltpu.make_async_copy(src_refs[a] if kind == "gather" else src_refs[a].at[me], land_refs[a].at[me], local_sems.at[a])
           for a in range(n)]
    copies = []
    for rel in range(1, N_DEV):
        px, py, pc = _flip(mx, rel & 4), _flip(my, rel & 2), _flip(mc, rel & 1)
        for a in range(n):
            src = src_refs[a] if kind == "gather" else src_refs[a].at[4 * px + 2 * py + pc]
            copies.append(pltpu.make_async_remote_copy(
                src_ref=src, dst_ref=land_refs[a].at[me], send_sem=send_sems.at[7 * a + rel - 1],
                recv_sem=recv_sems.at[7 * a + rel - 1], device_id=(px, py, pc), device_id_type=_MESH))
    return copies, own


_HBM = pl.BlockSpec(memory_space=pltpu.HBM)
_SEM = pl.BlockSpec(memory_space=pltpu.SEMAPHORE)
_DATAFLOW = pltpu.SideEffectType.DATAFLOW_SIDE_EFFECTING


def _exchange_start(name, kind, srcs):
    n = len(srcs)
    lands = [lax.empty(((N_DEV,) + s.shape) if kind == "gather" else s.shape, s.dtype) for s in srcs]

    def body(*refs):
        src_refs, land_refs = refs[:n], refs[n:2 * n]
        send_sems, recv_sems, local_sems = refs[2 * n:2 * n + 3]
        copies, own = _direct_copies(kind, src_refs, land_refs, send_sems, recv_sems, local_sems)
        for cp in own + copies:
            cp.start()
        refs[-1][...] = jnp.zeros_like(refs[-1])

    hbm = lambda a: pltpu.with_memory_space_constraint(a, pltpu.HBM)
    out = pl.pallas_call(
        body, name=name,
        out_shape=[pltpu.SemaphoreType.DMA((7 * n,)), pltpu.SemaphoreType.DMA((7 * n,)), pltpu.SemaphoreType.DMA((n,))]
        + [pltpu.HBM(a.shape, a.dtype) for a in srcs + lands] + [jax.ShapeDtypeStruct((8, LANES), F32)],
        in_specs=[_HBM] * (2 * n), out_specs=[_SEM] * 3 + [_HBM] * (2 * n) + [pl.BlockSpec(memory_space=pltpu.VMEM)],
        input_output_aliases={i: 3 + i for i in range(2 * n)},
        compiler_params=pltpu.CompilerParams(has_side_effects=_DATAFLOW))(*[hbm(a) for a in srcs + lands])
    return (kind, out[:-1]), out[-1]


def _exchange_wait(name, started, after):
    kind, (send_sems, recv_sems, local_sems, *arrays) = started
    n = len(arrays) // 2

    def body(*refs):
        src_refs, land_refs = refs[:n], refs[n:2 * n]
        copies, own = _direct_copies(kind, src_refs, land_refs, *refs[2 * n:2 * n + 3])
        for cp in copies + own:
            cp.wait()

    out = pl.pallas_call(
        body, name=name, out_shape=[pltpu.HBM(a.shape, a.dtype) for a in arrays],
        in_specs=[_HBM] * (2 * n) + [_SEM] * 3 + [_ANY], out_specs=[_HBM] * (2 * n),
        input_output_aliases={i: i for i in range(2 * n)},
        compiler_params=pltpu.CompilerParams(has_side_effects=_DATAFLOW))(*arrays, send_sems, recv_sems, local_sems, after)
    return out[n:]


def _adamw_math(g, w, m, v):
    m2 = ADAM_B1 * m + (1.0 - ADAM_B1) * g
    v2 = ADAM_B2 * v + (1.0 - ADAM_B2) * jnp.square(g)
    m_hat = m2 / (1.0 - ADAM_B1 ** ADAM_STEP)
    v_hat = v2 / (1.0 - ADAM_B2 ** ADAM_STEP)
    return g, -ADAM_LR * (m_hat / (jnp.sqrt(v_hat) + ADAM_EPS) + ADAM_WD * w), m2, v2


def _sum_blocks(ref):
    g = ref[0].astype(F32)
    for j in range(1, N_DEV):
        g = g + ref[j].astype(F32)
    return g


def _adamw_flat(name, parts, w, m, v):
    r = w.shape[0]
    tile = _pick(r, (1024, 512, 256, 128, 8))

    def body(p_ref, w_ref, m_ref, v_ref, *o_refs):
        for o, val in zip(o_refs, _adamw_math(_sum_blocks(p_ref), w_ref[...], m_ref[...], v_ref[...]), strict=True):
            o[...] = val

    flat = pl.BlockSpec((tile, LANES), lambda i: (i, 0))
    return pl.pallas_call(
        body, name=name, grid=(r // tile,), out_shape=[jax.ShapeDtypeStruct((r, LANES), F32)] * 4,
        in_specs=[pl.BlockSpec((N_DEV, tile, LANES), lambda i: (0, i, 0)), flat, flat, flat], out_specs=[flat] * 4,
        compiler_params=_params(("parallel",)))(parts, w, m, v)


def _adamw_cols(name, landed, base, stride, w, m, v):
    depth, rows, cs = w.shape
    n_slab = -(-cs // LANES)
    tr = _pick(rows, (SLAB_TILE,))

    def body(*refs):
        slabs, (w_ref, m_ref, v_ref), o_refs = refs[:n_slab], refs[n_slab:n_slab + 3], refs[n_slab + 3:]
        g = jnp.concatenate([_sum_blocks(s)[:, :min(LANES, cs - LANES * k)] for k, s in enumerate(slabs)], axis=1)
        for o, val in zip(o_refs, _adamw_math(g, w_ref[...], m_ref[...], v_ref[...]), strict=True):
            o[...] = val

    slab = lambda k: pl.BlockSpec((N_DEV, tr, LANES), lambda l, i: (0, (l * stride + base + k * rows) // tr + i, 0))
    nat = pl.BlockSpec((None, tr, cs), lambda l, i: (l, i, 0))
    return pl.pallas_call(
        body, name=name, grid=(depth, rows // tr), out_shape=[jax.ShapeDtypeStruct(w.shape, F32)] * 4,
        in_specs=[slab(k) for k in range(n_slab)] + [nat] * 3, out_specs=[nat] * 4,
        compiler_params=_params(("parallel", "parallel")))(*[landed] * n_slab, w, m, v)


def _adamw_rows(name, landed, base, stride, w, m, v):
    depth, rs, width = w.shape
    tr = math.gcd(rs, base, stride)

    def body(p_ref, w_ref, m_ref, v_ref, *o_refs):
        for o, val in zip(o_refs, _adamw_math(_sum_blocks(p_ref), w_ref[...], m_ref[...], v_ref[...]), strict=True):
            o[...] = val

    nat = pl.BlockSpec((None, tr, width), lambda l, i: (l, i, 0))
    return pl.pallas_call(
        body, name=name, grid=(depth, rs // tr), out_shape=[jax.ShapeDtypeStruct(w.shape, F32)] * 4,
        in_specs=[pl.BlockSpec((N_DEV, tr, width), lambda l, i: (0, (l * stride + base) // tr + i, 0)), nat, nat, nat],
        out_specs=[nat] * 4, compiler_params=_params(("parallel", "parallel")))(landed, w, m, v)


def _adamw_conv(landed, base, stride, w, m, v):
    depth, taps, cs = w.shape

    def body(p_ref, w_ref, m_ref, v_ref, *o_refs):
        g = _sum_blocks(p_ref)[:taps, :cs]
        for o, val in zip(o_refs, _adamw_math(g, w_ref[...], m_ref[...], v_ref[...]), strict=True):
            o[...] = val

    nat = pl.BlockSpec((None, taps, cs), lambda l: (l, 0, 0))
    return pl.pallas_call(
        body, name="adamw_conv", grid=(depth,), out_shape=[jax.ShapeDtypeStruct(w.shape, F32)] * 4,
        in_specs=[pl.BlockSpec((N_DEV, CONV_HALO, LANES), lambda l: (0, (l * stride + base) // CONV_HALO, 0)), nat, nat, nat],
        out_specs=[nat] * 4, compiler_params=_params(("parallel",)))(landed, w, m, v)


def _unshard_cols(name, gathered, start, rows, cs, shift=0):
    n_slab = -(-cs // LANES)
    total = N_DEV * cs
    tr = _pick(rows, (SLAB_TILE,))

    def body(*refs):
        slabs, o_ref = refs[:n_slab], refs[n_slab]
        for j in range(N_DEV):
            for k, s in enumerate(slabs):
                for src, dst, width in _wrapped(j * cs + LANES * k - shift, min(LANES, cs - LANES * k), total):
                    o_ref[:, dst:dst + width] = s[j, :, src:src + width]

    slab = lambda k: pl.BlockSpec((N_DEV, tr, LANES), lambda i: (0, (start + k * rows) // tr + i, 0))
    return pl.pallas_call(
        body, name=name, grid=(rows // tr,), out_shape=jax.ShapeDtypeStruct((rows, total), gathered.dtype),
        in_specs=[slab(k) for k in range(n_slab)], out_specs=pl.BlockSpec((tr, total), lambda i: (i, 0)),
        compiler_params=_params(("parallel",)))(*[gathered] * n_slab)


def _shard_cols(name, full, cs, shift=0):
    rows, total = full.shape
    n_slab = -(-cs // LANES)
    tr = _pick(rows, (SLAB_TILE,))

    def body(f_ref, o_ref):
        for j in range(N_DEV):
            for k in range(n_slab):
                used = min(LANES, cs - LANES * k)
                for src, dst, width in _wrapped(j * cs + LANES * k - shift, used, total):
                    o_ref[j, k, :, src:src + width] = f_ref[:, dst:dst + width].astype(o_ref.dtype)
                if used < LANES:
                    o_ref[j, k, :, used:] = jnp.zeros((tr, LANES - used), o_ref.dtype)

    out = pl.pallas_call(
        body, name=name, grid=(rows // tr,), out_shape=jax.ShapeDtypeStruct((N_DEV, n_slab, rows, LANES), BF16),
        in_specs=[pl.BlockSpec((tr, total), lambda i: (i, 0))],
        out_specs=pl.BlockSpec((N_DEV, n_slab, tr, LANES), lambda i: (0, 0, i, 0)),
        compiler_params=_params(("parallel",)))(full)
    return out.reshape(N_DEV, n_slab * rows, LANES)


def _wrapped(pos, width, total):
    pos %= total
    if pos + width <= total:
        return [(0, pos, width)]
    head = total - pos
    return [(0, pos, head), (head, 0, width - head)]


CONV_W_PIECES = 3


def _pad_to(n, align):
    return -(-n // align) * align


def _layout():
    dims = {name: (rows, cols) for name, rows, cols, _ in SHARDED}
    col, off = {}, 0
    for name in COL_SHARDED:
        rows, cols = dims[name]
        cs = cols // N_DEV
        col[name] = (off, rows, cs)
        off += -(-cs // LANES) * rows
    conv_base = off
    col_rows = _pad_to(off + CONV_W_PIECES * CONV_HALO, SLAB_TILE)
    row, off = {}, 0
    for name in ROW_SHARDED:
        rs = dims[name][0] // N_DEV
        row[name] = (off, rs)
        off += _pad_to(rs, LANES)
    return col, conv_base, col_rows, row, off


def _slabs(shard):
    rows, cs = shard.shape
    parts = []
    for k in range(-(-cs // LANES)):
        part = shard[:, LANES * k:min(LANES * (k + 1), cs)]
        parts.append(jnp.pad(part, ((0, 0), (0, LANES - part.shape[1]))))
    return jnp.concatenate(parts, axis=0)


def _concat_padded(pieces, total, axis):
    used = sum(p.shape[axis] for p in pieces)
    if total > used:
        shape = list(pieces[0].shape)
        shape[axis] = total - used
        pieces = pieces + [jnp.zeros(shape, pieces[0].dtype)]
    return jnp.concatenate(pieces, axis=axis)


def _split3(a):
    hi = a.astype(BF16)
    r1 = a - hi.astype(F32)
    mid = r1.astype(BF16)
    return hi, mid, (r1 - mid.astype(F32)).astype(BF16)


def _pack_small(arrs, lead=()):
    flat = jnp.concatenate([a.reshape(lead + (-1,)) for a in arrs], axis=-1)
    total = _pad_to(flat.shape[-1], 512 * LANES)
    flat = jnp.pad(flat, [(0, 0)] * len(lead) + [(0, total - flat.shape[-1])])
    return flat.reshape(lead + (total // LANES, LANES))


def _unpack_small(flat, shapes):
    flat = flat.reshape(-1)
    res, off = [], 0
    for s in shapes:
        n = int(np.prod(s))
        res.append(flat[off:off + n].reshape(s))
        off += n
    return res


def _small_rows(a, depth):
    n16 = depth * SSM_GROUPS
    a_re, a_im, f_re, f_im = _ssm_coeffs(a["ssm_lambda_re"].reshape(n16, SSM_STATE), a["ssm_lambda_im"].reshape(n16, SSM_STATE),
                                         a["ssm_log_dt"].reshape(n16, 1))
    rows = []
    for l in range(depth):
        sp = {k: _row(a[k][l]) for k in ("mix_norm_g", "b_gate", "attn_sinks", "ssm_d", "b_ssm_glu", "conv_dw_b",
                                         "conv_norm_g", "conv_norm_b", "ffn_norm_g", "ple_norm_g")}
        g = slice(l * SSM_GROUPS, (l + 1) * SSM_GROUPS)
        sp["a_row"] = jnp.concatenate([a_re[g].reshape(1, -1), a_im[g].reshape(1, -1)], axis=1)
        sp["f_row"] = jnp.concatenate([f_re[g].reshape(1, -1), f_im[g].reshape(1, -1)], axis=1)
        sp["bmat"], sp["cmat"] = _ssm_mats(a["ssm_b_re"][l], a["ssm_b_im"][l], a["ssm_c_re"][l], a["ssm_c_im"][l])
        rows.append(sp)
    return rows


def _local_step(a, get_weights, on_grads, depth):
    n_seq, seq, d = a["x"].shape
    t = n_seq * seq
    inv = ROPE_THETA ** (-jnp.arange(0, ROPE_DIM, 2, dtype=F32) / ROPE_DIM)
    lane = np.arange(LANES) % HEAD_DIM
    inv_lane = jnp.where(lane < ROPE_DIM, jnp.tile(inv, LANES // (ROPE_DIM // 2)), 0.0).reshape(1, LANES)
    ctab, stab = _rope_tables(a["positions"].reshape(t), inv_lane)
    small = _small_rows(a, depth)

    x = a["x"].reshape(t, d)
    saved, weights = [], []
    for l in range(depth):
        weights.append(get_weights(l, x))
        x, sv = _layer_fwd(x, a["p"][l].reshape(t, -1), weights[l], small[l], ctab, stab, n_seq)
        saved.append(sv)
    loss, dx, d_final = _loss_and_grad(x, a["loss_target"].reshape(t, d), _row(a["final_norm_g"]))
    gws, gss, token = [None] * depth, [None] * depth, None
    for l in reversed(range(depth)):
        sp = small[l] if token is None else dict(small[l], ple_norm_g=small[l]["ple_norm_g"] + token[0:1, 0:1])
        dx, gws[l], gss[l] = _layer_bwd(dx, saved[l], weights[l], sp, ctab, stab, n_seq)
        token = on_grads(l, gws[l])

    n16 = depth * SSM_GROUPS
    halves = lambda k, h: jnp.concatenate([gss[l][k][:, h * SSM_LANES:(h + 1) * SSM_LANES].reshape(SSM_GROUPS, SSM_STATE)
                                           for l in range(depth)], axis=0)
    dlr, dli, ddt = _ssm_coeffs_bwd(a["ssm_lambda_re"].reshape(n16, SSM_STATE), a["ssm_lambda_im"].reshape(n16, SSM_STATE),
                                    a["ssm_log_dt"].reshape(n16, 1),
                                    (halves("a_row", 0), halves("a_row", 1), halves("f_row", 0), halves("f_row", 1)))
    bc = [_ssm_mats_t(gss[l]["bmat"], gss[l]["cmat"]) for l in range(depth)]
    gsmall = {k: jnp.stack([gss[l][k].reshape(a[k].shape[1:]) for l in range(depth)])
              for k in ("mix_norm_g", "b_gate", "attn_sinks", "ssm_d", "b_ssm_glu", "conv_dw_b", "conv_norm_g", "conv_norm_b",
                        "ffn_norm_g", "ple_norm_g")}
    gsmall["ssm_lambda_re"] = dlr.reshape(a["ssm_lambda_re"].shape)
    gsmall["ssm_lambda_im"] = dli.reshape(a["ssm_lambda_im"].shape)
    gsmall["ssm_log_dt"] = ddt.reshape(a["ssm_log_dt"].shape)
    for i, k in enumerate(("ssm_b_re", "ssm_b_im", "ssm_c_re", "ssm_c_im")):
        gsmall[k] = jnp.stack([bc[l][i] for l in range(depth)])
    gsmall["final_norm_g"] = d_final.reshape(a["final_norm_g"].shape)
    return loss, dx.reshape(n_seq, seq, d), gws, gsmall


def kernel(x, p, positions, mix_norm_g, w_in, b_gate, attn_sinks, w_attn_out, ssm_lambda_re, ssm_lambda_im, ssm_log_dt, ssm_b_re, ssm_b_im, ssm_c_re, ssm_c_im, ssm_d, w_ssm_glu, b_ssm_glu, conv_dw_w, conv_dw_b, conv_norm_g, conv_norm_b, w_conv_out, w_mix_out, ffn_norm_g, w_ffn_in, w_ffn_out, w_ple_in, ple_norm_g, w_ple_gate, final_norm_g, loss_target, m_mix_norm_g, m_w_in, m_b_gate, m_attn_sinks, m_w_attn_out, m_ssm_lambda_re, m_ssm_lambda_im, m_ssm_log_dt, m_ssm_b_re, m_ssm_b_im, m_ssm_c_re, m_ssm_c_im, m_ssm_d, m_w_ssm_glu, m_b_ssm_glu, m_conv_dw_w, m_conv_dw_b, m_conv_norm_g, m_conv_norm_b, m_w_conv_out, m_w_mix_out, m_ffn_norm_g, m_w_ffn_in, m_w_ffn_out, m_w_ple_in, m_ple_norm_g, m_w_ple_gate, m_final_norm_g, v_mix_norm_g, v_w_in, v_b_gate, v_attn_sinks, v_w_attn_out, v_ssm_lambda_re, v_ssm_lambda_im, v_ssm_log_dt, v_ssm_b_re, v_ssm_b_im, v_ssm_c_re, v_ssm_c_im, v_ssm_d, v_w_ssm_glu, v_b_ssm_glu, v_conv_dw_w, v_conv_dw_b, v_conv_norm_g, v_conv_norm_b, v_w_conv_out, v_w_mix_out, v_ffn_norm_g, v_w_ffn_in, v_w_ffn_out, v_w_ple_in, v_ple_norm_g, v_w_ple_gate, v_final_norm_g):
    a = dict(locals())
    depth = w_in.shape[0]
    col, conv_base, col_rows, row, row_rows = _layout()
    shift = {"w_in": Z_SPLIT}
    conv_pad = ((0, 0), (0, CONV_HALO - CONV_K), (0, LANES - CONV_WIDTH // N_DEV))

    gathers, tokens = [], []
    for l in range(depth):
        pieces = [_slabs(a[name][l].astype(BF16)) for name in COL_SHARDED]
        pieces.append(jnp.pad(jnp.stack(_split3(a["conv_dw_w"][l])), conv_pad).reshape(-1, LANES))
        regions = [_concat_padded([a[name][l].astype(BF16)], _pad_to(rs, LANES), 0) for name, (_, rs) in row.items()]
        started, token = _exchange_start(f"gather_start_{l}", "gather",
                                         [_concat_padded(pieces, col_rows, 0), jnp.concatenate(regions, axis=0)])
        gathers.append(started)
        tokens.append(token[0:1, 0:1])

    def get_weights(l, x):
        slab8, row8 = _exchange_wait(f"gather_wait_{l}", gathers[l], x)
        w = {name: _unshard_cols("unshard_" + name, slab8, base, rows, cs, shift.get(name, 0))
             for name, (base, rows, cs) in col.items()}
        for name, (base, rs) in row.items():
            w[name] = row8[:, base:base + rs].reshape(N_DEV * rs, -1)
        conv = slab8[:, conv_base:conv_base + CONV_W_PIECES * CONV_HALO]
        conv = conv.reshape(N_DEV, CONV_W_PIECES, CONV_HALO, LANES)[:, :, :CONV_K, :CONV_WIDTH // N_DEV].astype(F32)
        w["conv_dw_w"] = jnp.sum(conv, axis=1).transpose(1, 0, 2).reshape(CONV_K, CONV_WIDTH)
        return w

    scatters = [None] * depth

    def on_grads(l, gw):
        pieces = [_shard_cols("shard_" + name, gw[name], cs, shift.get(name, 0)) for name, (_, _, cs) in col.items()]
        conv = gw["conv_dw_w"].reshape(CONV_K, N_DEV, CONV_WIDTH // N_DEV).transpose(1, 0, 2).astype(BF16)
        pieces.append(jnp.pad(jnp.pad(conv, conv_pad), ((0, 0), (0, (CONV_W_PIECES - 1) * CONV_HALO), (0, 0))))
        regions = [_concat_padded([gw[name].astype(BF16).reshape(N_DEV, rs, -1)], _pad_to(rs, LANES), 1)
                   for name, (_, rs) in row.items()]
        scatters[l], token = _exchange_start(f"grads_start_{l}", "scatter",
                                             [_concat_padded(pieces, col_rows, 1), jnp.concatenate(regions, axis=1)])
        return token

    local = dict(a, mix_norm_g=a["mix_norm_g"] + sum(tokens))
    loss, grad_x, _, gsmall = _local_step(local, get_weights, on_grads, depth)
    loss = lax.psum(loss, ("x", "y", "c"))

    landed = [_exchange_wait(f"grads_wait_{l}", scatters[l], grad_x) for l in range(depth)]
    landed_slab = jnp.concatenate([ls for ls, _ in landed], axis=1)
    landed_row = jnp.concatenate([lr for _, lr in landed], axis=1)
    state = lambda name: (a[name], a["m_" + name], a["v_" + name])
    big = {name: _adamw_cols("adamw_" + name, landed_slab, base, col_rows, *state(name)) for name, (base, _, _) in col.items()}
    big.update({name: _adamw_rows("adamw_" + name, landed_row, base, row_rows, *state(name)) for name, (base, _) in row.items()})
    big["conv_dw_w"] = _adamw_conv(landed_slab, conv_base, col_rows, *state("conv_dw_w"))

    shapes = [a[k].shape for k in REPLICATED]
    parts, = _all_gather("gather_small_grads", [_pack_small([gsmall[k] for k in REPLICATED])])
    small_state = [_pack_small([a[pre + k] for k in REPLICATED]) for pre in ("", "m_", "v_")]
    small = [dict(zip(REPLICATED, _unpack_small(o, shapes), strict=True))
             for o in _adamw_flat("adamw_replicated", parts, *small_state)]

    def result(kind, name):
        if name in REPLICATED:
            return small[kind][name]
        return big[name][kind]

    return (loss, grad_x, *[result(kind, n) for kind in range(4) for n in WEIGHT_ORDER])
```

```python
import functools
import math

import numpy as np
import jax
import jax.numpy as jnp
from jax import lax
from jax.experimental import pallas as pl
from jax.experimental.pallas import tpu as pltpu

F32 = jnp.float32
BF16 = jnp.bfloat16
MXU_DTYPE = jnp.bfloat16
VMEM_LIMIT_BYTES = 56 * 2 ** 20
N_DEV = 8
LANES = 128

HEAD_DIM = 64
N_Q_HEADS = 8
N_KV_HEADS = 2
GQA_GROUP = 4
BLOCK = 128
ROPE_THETA = 500000.0
ROPE_DIM = 16
Q_WIDTH = 512
KV_WIDTH = 128
SSM_WIDTH = 256
SSM_GROUP = 16
SSM_GROUPS = 16
SSM_STATE = 64
SSM_LANES = SSM_GROUPS * SSM_STATE
CONV_WIDTH = 256
CONV_K = 31
CONV_HALO = 32
EPS = 1e-6
NEG_INF = -1e30
ADAM_LR, ADAM_B1, ADAM_B2, ADAM_EPS, ADAM_WD, ADAM_STEP = 0.001, 0.9, 0.999, 1e-08, 0.01, 10

ZG_W, ZQ_W, ZKV_W, ZS_W, ZC_W = 3072, 512, 256, 256, 512
ZQ_BLK, ZKV_BLK, ZS_BLK, ZC_BLK = 3072 // 512, 3584 // 256, 3840 // 256, 4096 // 512
Z_WIDTH = 4608
Z_SPLIT = 1536

SHARDED = (("w_in", 1024, 4608, 1), ("w_attn_out", 512, 1024, 1), ("w_ssm_glu", 256, 2048, 1),
           ("conv_dw_w", 31, 256, 1), ("w_conv_out", 256, 1024, 1), ("w_mix_out", 1024, 1024, 0),
           ("w_ffn_in", 1024, 5632, 1), ("w_ffn_out", 2816, 1024, 0), ("w_ple_in", 256, 1024, 1),
           ("w_ple_gate", 1024, 1024, 0))
GROUPS = {"mix": (("w_in", "w_attn_out", "w_ssm_glu", "w_conv_out"), ("w_mix_out",), True),
          "ffn": (("w_ffn_in", "w_ple_in"), ("w_ffn_out", "w_ple_gate"), False)}
SLAB_TILE = 256
FLAT_ROW_ALIGN = 1024
REPLICATED = ("mix_norm_g", "b_gate", "attn_sinks", "ssm_lambda_re", "ssm_lambda_im", "ssm_log_dt", "ssm_b_re",
              "ssm_b_im", "ssm_c_re", "ssm_c_im", "ssm_d", "b_ssm_glu", "conv_dw_b", "conv_norm_g", "conv_norm_b",
              "ffn_norm_g", "ple_norm_g", "final_norm_g")
WEIGHT_ORDER = ("mix_norm_g", "w_in", "b_gate", "attn_sinks", "w_attn_out", "ssm_lambda_re", "ssm_lambda_im",
                "ssm_log_dt", "ssm_b_re", "ssm_b_im", "ssm_c_re", "ssm_c_im", "ssm_d", "w_ssm_glu", "b_ssm_glu",
                "conv_dw_w", "conv_dw_b", "conv_norm_g", "conv_norm_b", "w_conv_out", "w_mix_out", "ffn_norm_g",
                "w_ffn_in", "w_ffn_out", "w_ple_in", "ple_norm_g", "w_ple_gate", "final_norm_g")


_ANY = pl.BlockSpec(memory_space=pl.ANY)


def _params(sem=None):
    return pltpu.CompilerParams(dimension_semantics=sem, vmem_limit_bytes=VMEM_LIMIT_BYTES)


def _pick(n, cands):
    for c in cands:
        if n % c == 0:
            return c
    return n


def _dot(a, b, dims):
    return lax.dot_general(a.astype(MXU_DTYPE), b.astype(MXU_DTYPE), (dims, ((), ())), preferred_element_type=F32)


def _dot_nn(a, b):
    return _dot(a, b, ((1,), (0,)))


def _dot_nt(a, b):
    return _dot(a, b, ((1,), (1,)))


def _dot_tn(a, b):
    return _dot(a, b, ((0,), (0,)))


@jax.custom_vjp
def _mm(x, w):
    return _dot_nn(x, w)


def _mm_f(x, w):
    return _dot_nn(x, w), (x, w)


def _mm_b(res, dy):
    x, w = res
    return _dot_nt(dy, w).astype(x.dtype), _dot_tn(x, dy).astype(w.dtype)


_mm.defvjp(_mm_f, _mm_b)


def _rms(x, g):
    return x * lax.rsqrt(jnp.mean(x * x, axis=-1, keepdims=True) + EPS) * g


ROW_TILES = (1024, 512, 256, 128)
COL_TILES = (1536, 1408, 1024, 512, 256, 128)


def _matmul_nn(name, a, b, out_dtype):
    t, k = a.shape
    n = b.shape[1]
    tm, tn = _pick(t, ROW_TILES), _pick(n, COL_TILES)

    def body(a_ref, b_ref, o_ref):
        o_ref[...] = _dot_nn(a_ref[...], b_ref[...]).astype(o_ref.dtype)

    return pl.pallas_call(
        body, name=name, grid=(t // tm, n // tn), out_shape=jax.ShapeDtypeStruct((t, n), out_dtype),
        in_specs=[pl.BlockSpec((tm, k), lambda i, j: (i, 0)), pl.BlockSpec((k, tn), lambda i, j: (0, j))],
        out_specs=pl.BlockSpec((tm, tn), lambda i, j: (i, j)),
        compiler_params=_params(("parallel", "parallel")))(a, b)


def _matmul_tn(name, a, b):
    t, m = a.shape
    n = b.shape[1]
    tm, tn, tt = _pick(m, COL_TILES[1:]), _pick(n, COL_TILES), _pick(t, ROW_TILES)

    def body(a_ref, b_ref, o_ref):
        @pl.when(pl.program_id(2) == 0)
        def _():
            o_ref[...] = jnp.zeros_like(o_ref)

        o_ref[...] += _dot_tn(a_ref[...], b_ref[...])

    return pl.pallas_call(
        body, name=name, grid=(m // tm, n // tn, t // tt), out_shape=jax.ShapeDtypeStruct((m, n), F32),
        in_specs=[pl.BlockSpec((tt, tm), lambda i, j, s: (s, i)), pl.BlockSpec((tt, tn), lambda i, j, s: (s, j))],
        out_specs=pl.BlockSpec((tm, tn), lambda i, j, s: (i, j)),
        compiler_params=_params(("parallel", "parallel", "arbitrary")))(a, b)


def _in_proj(x, g, w):
    t, d = x.shape
    n = w.shape[1]
    tm, tn = _pick(t, ROW_TILES), _pick(n, COL_TILES)

    def body(x_ref, g_ref, w_ref, z_ref, h_ref):
        h = _rms(x_ref[...], g_ref[...]).astype(h_ref.dtype)

        @pl.when(pl.program_id(1) == 0)
        def _():
            h_ref[...] = h

        z_ref[...] = _dot_nn(h, w_ref[...])

    return pl.pallas_call(
        body, name="in_proj", grid=(t // tm, n // tn),
        out_shape=[jax.ShapeDtypeStruct((t, n), F32), jax.ShapeDtypeStruct((t, d), MXU_DTYPE)],
        in_specs=[pl.BlockSpec((tm, d), lambda i, j: (i, 0)), pl.BlockSpec((1, d), lambda i, j: (0, 0)),
                  pl.BlockSpec((d, tn), lambda i, j: (0, j))],
        out_specs=[pl.BlockSpec((tm, tn), lambda i, j: (i, j)), pl.BlockSpec((tm, d), lambda i, j: (i, 0))],
        compiler_params=_params(("parallel", "arbitrary")))(x, g, w)


def _in_proj_bwd(dz, w, x, dx_res, g):
    t, d = x.shape
    tm = _pick(t, ROW_TILES[1:])

    def body(dz_ref, w_ref, x_ref, r_ref, g_ref, dx_ref, dg_ref):
        _, vjp = jax.vjp(_norm_in_tile, x_ref[...], g_ref[...])
        dx, dg = vjp(_dot_nt(dz_ref[...], w_ref[...]))
        dx_ref[...] = dx + r_ref[...]

        @pl.when(pl.program_id(0) == 0)
        def _():
            dg_ref[...] = jnp.zeros_like(dg_ref)

        dg_ref[...] += dg

    rows = lambda width: pl.BlockSpec((tm, width), lambda i: (i, 0))
    whole = lambda a: pl.BlockSpec(a.shape, lambda i: (0, 0))
    return pl.pallas_call(
        body, name="in_proj_bwd", grid=(t // tm,), out_shape=[jax.ShapeDtypeStruct((t, d), F32), jax.ShapeDtypeStruct((1, d), F32)],
        in_specs=[rows(dz.shape[1]), whole(w), rows(d), rows(d), whole(g)], out_specs=[rows(d), whole(g)],
        compiler_params=_params(("arbitrary",)))(dz, w, x, dx_res, g)


def _ffn_in_act(hf, w_fi):
    t, k = hf.shape
    f = w_fi.shape[1] // 2
    tm, tf = _pick(t, ROW_TILES[1:]), _pick(f, COL_TILES)
    nf = f // tf

    def body(a_ref, wg_ref, wu_ref, g_ref, u_ref, act_ref):
        a = a_ref[...]
        g, u = _dot_nn(a, wg_ref[...]), _dot_nn(a, wu_ref[...])
        g_ref[...] = g.astype(g_ref.dtype)
        u_ref[...] = u.astype(u_ref.dtype)
        act_ref[...] = (jax.nn.silu(g) * u).astype(act_ref.dtype)

    out = pl.BlockSpec((tm, tf), lambda i, j: (i, j))
    return pl.pallas_call(
        body, name="ffn_in_act", grid=(t // tm, nf), out_shape=[jax.ShapeDtypeStruct((t, f), MXU_DTYPE)] * 3,
        in_specs=[pl.BlockSpec((tm, k), lambda i, j: (i, 0)), pl.BlockSpec((k, tf), lambda i, j: (0, j)),
                  pl.BlockSpec((k, tf), lambda i, j: (0, j + nf))],
        out_specs=[out, out, out], compiler_params=_params(("parallel", "parallel")))(hf, w_fi, w_fi)


def _ffn_mid_bwd(dffn, w_fo, gate, up):
    t, d = dffn.shape
    f = w_fo.shape[0]
    tm, tf = _pick(t, ROW_TILES[1:]), _pick(f, COL_TILES)

    def body(a_ref, w_ref, g_ref, u_ref, dg_ref, du_ref):
        dact = _dot_nt(a_ref[...], w_ref[...])
        g, u = g_ref[...].astype(F32), u_ref[...].astype(F32)
        sg = jax.nn.sigmoid(g)
        dg_ref[...] = (dact * u * sg * (1.0 + g * (1.0 - sg))).astype(dg_ref.dtype)
        du_ref[...] = (dact * g * sg).astype(du_ref.dtype)

    blk = pl.BlockSpec((tm, tf), lambda i, j: (i, j))
    return pl.pallas_call(
        body, name="ffn_mid_bwd", grid=(t // tm, f // tf), out_shape=[jax.ShapeDtypeStruct((t, f), MXU_DTYPE)] * 2,
        in_specs=[pl.BlockSpec((tm, d), lambda i, j: (i, 0)), pl.BlockSpec((tf, d), lambda i, j: (j, 0)), blk, blk],
        out_specs=[blk, blk], compiler_params=_params(("parallel", "parallel")))(dffn, w_fo, gate, up)


def _ffn_in_dx(dgate, dup, w_fi):
    t, f = dgate.shape
    d = w_fi.shape[0]
    tm = _pick(t, ROW_TILES[1:])

    def body(g_ref, u_ref, w_ref, o_ref):
        o_ref[...] = _dot_nt(g_ref[...], w_ref[:, :f]) + _dot_nt(u_ref[...], w_ref[:, f:])

    blk = pl.BlockSpec((tm, f), lambda i: (i, 0))
    return pl.pallas_call(
        body, name="ffn_in_dx", grid=(t // tm,), out_shape=jax.ShapeDtypeStruct((t, d), F32),
        in_specs=[blk, blk, pl.BlockSpec(w_fi.shape, lambda i: (0, 0))], out_specs=pl.BlockSpec((tm, d), lambda i: (i, 0)),
        compiler_params=_params(("parallel",)))(dgate, dup, w_fi)


def _token_call(name, fn, tile, tok_ins, consts, tok_outs, acc_outs, into=None):
    n_rows = tok_ins[0][0].shape[0]
    tile = min(tile, n_rows)
    n_ti, n_c = len(tok_ins), len(consts)
    n_in = n_ti + n_c + (into is not None)
    n_to = len(tok_outs) + (into is not None)

    def body(*refs):
        ins = [r[...] for r in refs[:n_ti + n_c]]
        outs, accs = fn(*ins)
        for r, v in zip(refs[n_in:n_in + n_to], outs, strict=True):
            r[...] = v.astype(r.dtype)
        first = pl.program_id(0) == 0
        for r, v in zip(refs[n_in + n_to:], accs, strict=True):
            @pl.when(first)
            def _(r=r):
                r[...] = jnp.zeros_like(r)

            r[...] += jnp.broadcast_to(v, r.shape).astype(F32)

    in_specs = [pl.BlockSpec((tile, w), functools.partial(lambda i, c: (i, c), c=cb)) for _, w, cb in tok_ins]
    in_specs += [pl.BlockSpec(c.shape, lambda i: (0, 0)) for c in consts]
    out_shape = [jax.ShapeDtypeStruct((n_rows, w), dt) for w, dt in tok_outs]
    out_specs = [pl.BlockSpec((tile, w), lambda i: (i, 0)) for w, _ in tok_outs]
    operands = [a for a, _, _ in tok_ins] + list(consts)
    aliases = {}
    if into is not None:
        target, width, col_block = into
        in_specs.append(_ANY)
        operands.append(target)
        out_shape.append(jax.ShapeDtypeStruct(target.shape, target.dtype))
        out_specs.append(pl.BlockSpec((tile, width), lambda i: (i, col_block)))
        aliases = {n_in - 1: n_to - 1}
    out_shape += [jax.ShapeDtypeStruct(s, F32) for s in acc_outs]
    out_specs += [pl.BlockSpec(s, lambda i: (0, 0)) for s in acc_outs]
    res = pl.pallas_call(
        body, name=name, grid=(n_rows // tile,), out_shape=out_shape, in_specs=in_specs, out_specs=out_specs,
        input_output_aliases=aliases, compiler_params=_params(("arbitrary",)))(*operands)
    return res[:n_to], res[n_to:]


def _whole(a):
    return (a, a.shape[1], 0)


def _norm_in_tile(x, g):
    return _rms(x, g)


def _merge_tile(x, ya, ys, uc, gin, w_ao, w_sg, b_sg, w_co, b_gate, w_mo, g_ffn):
    d = x.shape[1]
    y_attn = _mm(ya, w_ao)
    pre = _mm(jax.nn.gelu(ys), w_sg) + b_sg
    y_ssm = pre[:, :d] * jax.nn.sigmoid(pre[:, d:])
    y_conv = _mm(uc, w_co)
    gates = jax.nn.sigmoid(gin + b_gate)
    merged = gates[:, :d] * y_attn + gates[:, d:2 * d] * y_ssm + gates[:, 2 * d:] * y_conv
    x1 = x + _mm(merged, w_mo)
    return x1, _rms(x1, g_ffn)


def _ple_tile(x2, p, w_pi, g_ple, w_pg):
    return x2 + jax.nn.sigmoid(_mm(_rms(x2, g_ple), w_pg)) * _mm(p, w_pi)


def _f32s(vals):
    return [v.astype(F32) for v in vals]


def _rope_tables(positions, inv_lane):
    def fn(pos, inv):
        ang = pos.astype(F32) * inv
        j = lax.broadcasted_iota(jnp.int32, ang.shape, 1) % HEAD_DIM
        c = jnp.where(j < ROPE_DIM, jnp.cos(ang), 1.0)
        s = jnp.sin(ang)
        s = jnp.where(j < ROPE_DIM // 2, -s, jnp.where(j < ROPE_DIM, s, 0.0))
        return [c, s], []

    (c, s), _ = _token_call("rope_tables", fn, 1024, [_whole(positions.reshape(-1, 1))], [inv_lane],
                            [(LANES, F32), (LANES, F32)], [])
    return c, s


def _swap_halves(t):
    n = t.shape[1]
    j = lax.broadcasted_iota(jnp.int32, t.shape, 1) % HEAD_DIM
    lower = pltpu.roll(t, n - ROPE_DIM // 2, 1)
    upper = jnp.where(j < ROPE_DIM, pltpu.roll(t, ROPE_DIM // 2, 1), 0.0)
    return jnp.where(j < ROPE_DIM // 2, lower, upper)


def _rope(t, c, s):
    return t * c + _swap_halves(t) * s


def _rope_t(dt, c, s):
    return dt * c + _swap_halves(dt * s)


def _tile4(a):
    return jnp.concatenate([a] * (Q_WIDTH // LANES), axis=1)


def _attn_mask(n):
    qi = lax.broadcasted_iota(jnp.int32, (GQA_GROUP * BLOCK, 2 * BLOCK), 0) % BLOCK
    kj = lax.broadcasted_iota(jnp.int32, (GQA_GROUP * BLOCK, 2 * BLOCK), 1)
    dist = qi + BLOCK - kj
    return (dist >= 0) & (dist < BLOCK) & ((n > 0) | (kj >= BLOCK))


def _attn_specs(n_seq):
    own = lambda w, blk: pl.BlockSpec((n_seq, BLOCK, w), lambda n: (0, n, blk))
    prev = lambda w, blk: pl.BlockSpec((n_seq, BLOCK, w), lambda n: (0, jnp.maximum(n - 1, 0), blk))
    return [own(ZQ_W, ZQ_BLK), own(ZKV_W, ZKV_BLK), prev(ZKV_W, ZKV_BLK), own(LANES, 0), own(LANES, 0), prev(LANES, 0),
            prev(LANES, 0), pl.BlockSpec((1, N_Q_HEADS), lambda n: (0, 0))]


def _by_seq(a, n_seq):
    return a.reshape(n_seq, a.shape[0] // n_seq, a.shape[1])


ATTN_SCALE = HEAD_DIM ** -0.5


def _stack_heads(t, kh):
    return jnp.concatenate([t[:, (kh * GQA_GROUP + g) * HEAD_DIM:(kh * GQA_GROUP + g + 1) * HEAD_DIM]
                            for g in range(GQA_GROUP)], axis=0)


def _stack_sinks(sink, kh):
    return jnp.concatenate([jnp.broadcast_to(sink[:, kh * GQA_GROUP + g:kh * GQA_GROUP + g + 1], (BLOCK, 1))
                            for g in range(GQA_GROUP)], axis=0)


def _attn_band(b, q_ref, kv_ref, kvp_ref, c_ref, s_ref, cp_ref, sp_ref):
    c, s = c_ref[b], s_ref[b]
    q = _rope(q_ref[b], _tile4(c), _tile4(s)) * ATTN_SCALE
    kv, kvp = kv_ref[b], kvp_ref[b]
    k = _rope(kv[:, :KV_WIDTH], c, s)
    kp = _rope(kvp[:, :KV_WIDTH], cp_ref[b], sp_ref[b])
    kb = jnp.concatenate([kp, k], axis=0)
    vb = jnp.concatenate([kvp[:, KV_WIDTH:], kv[:, KV_WIDTH:]], axis=0)
    return q, kb, vb


def _attention_fwd(z, ctab, stab, sinks, n_seq):
    t = z.shape[0]
    seq = t // n_seq

    def body(q_ref, kv_ref, kvp_ref, c_ref, s_ref, cp_ref, sp_ref, sink_ref, o_ref, lse_ref):
        mask = _attn_mask(pl.program_id(0))
        sink = sink_ref[...]
        lane = lax.broadcasted_iota(jnp.int32, (BLOCK, N_Q_HEADS), 1)
        for b in range(n_seq):
            q, kb, vb = _attn_band(b, q_ref, kv_ref, kvp_ref, c_ref, s_ref, cp_ref, sp_ref)
            lse_all = jnp.zeros((BLOCK, N_Q_HEADS), F32)
            for kh in range(N_KV_HEADS):
                sc = jnp.where(mask, _dot_nt(_stack_heads(q, kh), kb[:, kh * HEAD_DIM:(kh + 1) * HEAD_DIM]), NEG_INF)
                sk = _stack_sinks(sink, kh)
                m = jnp.maximum(jnp.max(sc, axis=-1, keepdims=True), sk)
                pr = jnp.exp(sc - m)
                den = jnp.sum(pr, axis=-1, keepdims=True) + jnp.exp(sk - m)
                out = _dot_nn(pr * (1.0 / den), vb[:, kh * HEAD_DIM:(kh + 1) * HEAD_DIM])
                lse = m + jnp.log(den)
                for g in range(GQA_GROUP):
                    h = kh * GQA_GROUP + g
                    o_ref[b, :, h * HEAD_DIM:(h + 1) * HEAD_DIM] = out[g * BLOCK:(g + 1) * BLOCK].astype(o_ref.dtype)
                    lse_all = jnp.where(lane == h, lse[g * BLOCK:(g + 1) * BLOCK], lse_all)
            lse_ref[b] = lse_all

    rows = lambda w: pl.BlockSpec((n_seq, BLOCK, w), lambda n: (0, n, 0))
    z3, c3, s3 = _by_seq(z, n_seq), _by_seq(ctab, n_seq), _by_seq(stab, n_seq)
    ya, lse = pl.pallas_call(
        body, name="attn_fwd", grid=(seq // BLOCK,),
        out_shape=[jax.ShapeDtypeStruct((n_seq, seq, Q_WIDTH), MXU_DTYPE), jax.ShapeDtypeStruct((n_seq, seq, N_Q_HEADS), F32)],
        in_specs=_attn_specs(n_seq), out_specs=[rows(Q_WIDTH), rows(N_Q_HEADS)],
        compiler_params=_params(("parallel",)))(z3, z3, z3, c3, s3, c3, s3, sinks)
    return ya.reshape(t, Q_WIDTH), lse.reshape(t, N_Q_HEADS)


def _attention_bwd(z, ctab, stab, sinks, ya, lse, dya, dz, n_seq):
    t = z.shape[0]
    seq = t // n_seq

    def body(q_ref, kv_ref, kvp_ref, c_ref, s_ref, cp_ref, sp_ref, sink_ref, o_ref, lse_ref, do_ref, _,
             dq_ref, dkv_ref, dkvp_ref, dsink_ref):
        mask = _attn_mask(pl.program_id(0))
        sink = sink_ref[...]
        lane = lax.broadcasted_iota(jnp.int32, (1, N_Q_HEADS), 1)
        dsink = jnp.zeros((1, N_Q_HEADS), F32)
        for b in range(n_seq):
            q, kb, vb = _attn_band(b, q_ref, kv_ref, kvp_ref, c_ref, s_ref, cp_ref, sp_ref)
            lse_all = lse_ref[b]
            o = o_ref[b].astype(F32)
            do = do_ref[b].astype(F32)
            dq_parts = []
            dk_parts, dv_parts = [], []
            for kh in range(N_KV_HEADS):
                kbh = kb[:, kh * HEAD_DIM:(kh + 1) * HEAD_DIM]
                vbh = vb[:, kh * HEAD_DIM:(kh + 1) * HEAD_DIM]
                qs, dos = _stack_heads(q, kh), _stack_heads(do, kh)
                lse = jnp.concatenate([lse_all[:, kh * GQA_GROUP + g:kh * GQA_GROUP + g + 1] for g in range(GQA_GROUP)], axis=0)
                pr = jnp.exp(jnp.where(mask, _dot_nt(qs, kbh), NEG_INF) - lse)
                delta = jnp.sum(dos * _stack_heads(o, kh), axis=-1, keepdims=True)
                ds = pr * (_dot_nt(dos, vbh) - delta)
                dqs = _dot_nn(ds, kbh)
                dq_parts += [dqs[g * BLOCK:(g + 1) * BLOCK] for g in range(GQA_GROUP)]
                dk_parts.append(_dot_tn(ds, qs))
                dv_parts.append(_dot_tn(pr, dos))
                dsk = jnp.exp(_stack_sinks(sink, kh) - lse) * delta
                for g in range(GQA_GROUP):
                    dsink = dsink + jnp.where(lane == kh * GQA_GROUP + g, -jnp.sum(dsk[g * BLOCK:(g + 1) * BLOCK]), 0.0)
            c, s = c_ref[b], s_ref[b]
            dq_ref[b] = _rope_t(jnp.concatenate(dq_parts, axis=1) * ATTN_SCALE, _tile4(c), _tile4(s)).astype(dq_ref.dtype)
            dk = jnp.concatenate(dk_parts, axis=1)
            dv = jnp.concatenate(dv_parts, axis=1)
            dkv_ref[b, :, :KV_WIDTH] = _rope_t(dk[BLOCK:], c, s)
            dkv_ref[b, :, KV_WIDTH:] = dv[BLOCK:]
            dkvp_ref[b, :, :KV_WIDTH] = _rope_t(dk[:BLOCK], cp_ref[b], sp_ref[b])
            dkvp_ref[b, :, KV_WIDTH:] = dv[:BLOCK]

        @pl.when(pl.program_id(0) == 0)
        def _():
            dsink_ref[...] = jnp.zeros_like(dsink_ref)

        dsink_ref[...] += dsink

    rows = lambda w: pl.BlockSpec((n_seq, BLOCK, w), lambda n: (0, n, 0))
    by_seq = lambda a: _by_seq(a, n_seq)
    z3, c3, s3 = by_seq(z), by_seq(ctab), by_seq(stab)
    dz, dkv, dkvp, dsink = pl.pallas_call(
        body, name="attn_bwd", grid=(seq // BLOCK,),
        out_shape=[jax.ShapeDtypeStruct((n_seq, seq, Z_WIDTH), dz.dtype), jax.ShapeDtypeStruct((n_seq, seq, ZKV_W), F32),
                   jax.ShapeDtypeStruct((n_seq, seq, ZKV_W), F32), jax.ShapeDtypeStruct((1, N_Q_HEADS), F32)],
        in_specs=_attn_specs(n_seq) + [rows(Q_WIDTH), rows(N_Q_HEADS), rows(Q_WIDTH), _ANY],
        out_specs=[pl.BlockSpec((n_seq, BLOCK, ZQ_W), lambda n: (0, n, ZQ_BLK)), rows(ZKV_W), rows(ZKV_W),
                   pl.BlockSpec((1, N_Q_HEADS), lambda n: (0, 0))],
        input_output_aliases={11: 0},
        compiler_params=_params(("arbitrary",)))(z3, z3, z3, c3, s3, c3, s3, sinks, by_seq(ya), by_seq(lse), by_seq(dya), by_seq(dz))
    return dz.reshape(t, Z_WIDTH), dkv.reshape(t, ZKV_W), dkvp.reshape(t, ZKV_W), dsink


def _kv_combine(dkv, dkvp, dz, n_seq):
    t = dkv.shape[0]
    seq = t // n_seq
    rows = _pick(seq, (512, 256, 128))
    nt, per = seq // rows, rows // BLOCK
    n_blocks = t // BLOCK

    def body(dkv_ref, dkvp_ref, dkvn_ref, _, o_ref):
        nxt = jnp.where(pl.program_id(1) == nt - 1, 0.0, dkvn_ref[...])
        shifted = nxt if per == 1 else jnp.concatenate([dkvp_ref[BLOCK:, :], nxt], axis=0)
        o_ref[...] = (dkv_ref[...] + shifted).astype(o_ref.dtype)

    tile = pl.BlockSpec((rows, ZKV_W), lambda b, i: (b * nt + i, 0))
    return pl.pallas_call(
        body, name="kv_combine", grid=(n_seq, nt), out_shape=jax.ShapeDtypeStruct(dz.shape, dz.dtype),
        in_specs=[tile, tile,
                  pl.BlockSpec((BLOCK, ZKV_W), lambda b, i: (jnp.minimum((b * nt + i + 1) * per, n_blocks - 1), 0)), _ANY],
        out_specs=pl.BlockSpec((rows, ZKV_W), lambda b, i: (b * nt + i, ZKV_BLK)), input_output_aliases={3: 0},
        compiler_params=_params(("parallel", "parallel")))(dkv, dkvp, dkvp, dz)


def _ssm_coeff_tile(lam_re, lam_im, log_dt):
    lr = jnp.minimum(lam_re, -1e-4)
    dt = jnp.exp(log_dt)
    mag = jnp.exp(lr * dt)
    a_re = mag * jnp.cos(lam_im * dt)
    a_im = mag * jnp.sin(lam_im * dt)
    den = lr * lr + lam_im * lam_im
    x_re = a_re - 1.0
    f_re = (x_re * lr + a_im * lam_im) / den
    f_im = (a_im * lr - x_re * lam_im) / den
    return a_re, a_im, f_re, f_im


def _ssm_coeffs(lam_re, lam_im, log_dt):
    def body(lr_ref, li_ref, dt_ref, *o_refs):
        for r, v in zip(o_refs, _ssm_coeff_tile(lr_ref[...], li_ref[...], dt_ref[...]), strict=True):
            r[...] = v

    return pl.pallas_call(body, name="ssm_coeffs", out_shape=[jax.ShapeDtypeStruct(lam_re.shape, F32)] * 4)(
        lam_re, lam_im, log_dt)


def _ssm_coeffs_bwd(lam_re, lam_im, log_dt, cts):
    def body(lr_ref, li_ref, dt_ref, c0, c1, c2, c3, dlr_ref, dli_ref, ddt_ref):
        _, vjp = jax.vjp(_ssm_coeff_tile, lr_ref[...], li_ref[...], dt_ref[...])
        dlr, dli, ddt = vjp((c0[...], c1[...], c2[...], c3[...]))
        dlr_ref[...] = dlr
        dli_ref[...] = dli
        ddt_ref[...] = ddt

    return pl.pallas_call(
        body, name="ssm_coeffs_bwd",
        out_shape=[jax.ShapeDtypeStruct(lam_re.shape, F32)] * 2 + [jax.ShapeDtypeStruct(log_dt.shape, F32)])(
        lam_re, lam_im, log_dt, *cts)


def _ssm_chunk(t):
    return _pick(t, (256, 128))


def _ssm_fwd(z, bmat, a_row, f_row, cmat, d_row, n_seq):
    t = z.shape[0]
    seq = t // n_seq
    lc = _ssm_chunk(seq)
    nc = seq // lc
    n2 = 2 * SSM_LANES

    def body(u_ref, b_ref, a_ref, f_ref, c_ref, d_ref, y_ref, s_ref, bu_ref, st_ref):
        @pl.when(pl.program_id(1) == 0)
        def _():
            st_ref[...] = jnp.zeros_like(st_ref)

        u = u_ref[...]
        proj = _dot_nn(u, b_ref[...])
        fr, fi = f_ref[:, :SSM_LANES], f_ref[:, SSM_LANES:]
        pr, pi = proj[:, :SSM_LANES], proj[:, SSM_LANES:]
        bu_ref[:, :SSM_LANES] = fr * pr - fi * pi
        bu_ref[:, SSM_LANES:] = fr * pi + fi * pr
        ar, ai = a_ref[:, :SSM_LANES], a_ref[:, SSM_LANES:]

        def step(i, carry):
            sr, si = carry
            nr = ar * sr - ai * si + bu_ref[pl.ds(i, 1), pl.ds(0, SSM_LANES)]
            ni = ar * si + ai * sr + bu_ref[pl.ds(i, 1), pl.ds(SSM_LANES, SSM_LANES)]
            s_ref[pl.ds(i, 1), pl.ds(0, SSM_LANES)] = nr
            s_ref[pl.ds(i, 1), pl.ds(SSM_LANES, SSM_LANES)] = ni
            return nr, ni

        sr, si = lax.fori_loop(0, lc, step, (st_ref[0:1, :SSM_LANES], st_ref[0:1, SSM_LANES:]), unroll=8)
        st_ref[0:1, :SSM_LANES] = sr
        st_ref[0:1, SSM_LANES:] = si
        y_ref[...] = _dot_nn(s_ref[...], c_ref[...]) + d_ref[...] * u

    const = lambda shape: pl.BlockSpec(shape, lambda b, c: (0, 0))
    return pl.pallas_call(
        body, name="ssm_fwd", grid=(n_seq, nc),
        out_shape=[jax.ShapeDtypeStruct((t, SSM_WIDTH), F32), jax.ShapeDtypeStruct((t, n2), F32)],
        in_specs=[pl.BlockSpec((lc, ZS_W), lambda b, c: (b * nc + c, ZS_BLK)), const((SSM_WIDTH, n2)), const((1, n2)),
                  const((1, n2)), const((n2, SSM_WIDTH)), const((1, SSM_WIDTH))],
        out_specs=[pl.BlockSpec((lc, SSM_WIDTH), lambda b, c: (b * nc + c, 0)),
                   pl.BlockSpec((lc, n2), lambda b, c: (b * nc + c, 0))],
        scratch_shapes=[pltpu.VMEM((lc, n2), F32), pltpu.VMEM((8, n2), F32)],
        compiler_params=_params(("arbitrary", "arbitrary")))(z, bmat, a_row, f_row, cmat, d_row)


def _ssm_bwd(z, states, dy, bmat, a_row, f_row, cmat, d_row, dz, n_seq):
    t = z.shape[0]
    seq = t // n_seq
    lc = _ssm_chunk(seq)
    nc = seq // lc
    n2 = 2 * SSM_LANES

    def body(dy_ref, u_ref, s_ref, b_ref, a_ref, f_ref, c_ref, d_ref, _,
             du_ref, db_ref, dc_ref, da_ref, df_ref, dd_ref, g_ref, carry_ref):
        @pl.when((pl.program_id(0) == 0) & (pl.program_id(1) == 0))
        def _():
            for r in (db_ref, dc_ref, da_ref, df_ref, dd_ref):
                r[...] = jnp.zeros_like(r)

        @pl.when(pl.program_id(1) == 0)
        def _():
            carry_ref[...] = jnp.zeros_like(carry_ref)

        dy, u, st = dy_ref[...], u_ref[...], s_ref[...]
        g_ref[0:lc, :] = _dot_nt(dy, c_ref[...])
        g_ref[lc:lc + 8, :] = carry_ref[...]
        dc_ref[...] += _dot_tn(st, dy)
        dd_ref[...] += jnp.sum(dy * u, axis=0, keepdims=True)
        ar, ai = a_ref[:, :SSM_LANES], a_ref[:, SSM_LANES:]

        def step(i, carry):
            gr, gi = carry
            r = lc - 1 - i
            nr = g_ref[pl.ds(r, 1), pl.ds(0, SSM_LANES)] + ar * gr + ai * gi
            ni = g_ref[pl.ds(r, 1), pl.ds(SSM_LANES, SSM_LANES)] - ai * gr + ar * gi
            g_ref[pl.ds(r, 1), pl.ds(0, SSM_LANES)] = nr
            g_ref[pl.ds(r, 1), pl.ds(SSM_LANES, SSM_LANES)] = ni
            return nr, ni

        gr, gi = lax.fori_loop(0, lc, step, (carry_ref[0:1, :SSM_LANES], carry_ref[0:1, SSM_LANES:]), unroll=8)
        carry_ref[0:1, :SSM_LANES] = gr
        carry_ref[0:1, SSM_LANES:] = gi
        sr, si = st[:, :SSM_LANES], st[:, SSM_LANES:]
        gnr, gni = g_ref[pl.ds(1, lc), pl.ds(0, SSM_LANES)], g_ref[pl.ds(1, lc), pl.ds(SSM_LANES, SSM_LANES)]
        da_ref[:, :SSM_LANES] += jnp.sum(gnr * sr + gni * si, axis=0, keepdims=True)
        da_ref[:, SSM_LANES:] += jnp.sum(gni * sr - gnr * si, axis=0, keepdims=True)
        gr_all, gi_all = g_ref[0:lc, :SSM_LANES], g_ref[0:lc, SSM_LANES:]
        proj = _dot_nn(u, b_ref[...])
        pr, pi = proj[:, :SSM_LANES], proj[:, SSM_LANES:]
        df_ref[:, :SSM_LANES] += jnp.sum(gr_all * pr + gi_all * pi, axis=0, keepdims=True)
        df_ref[:, SSM_LANES:] += jnp.sum(gi_all * pr - gr_all * pi, axis=0, keepdims=True)
        fr, fi = f_ref[:, :SSM_LANES], f_ref[:, SSM_LANES:]
        dproj = jnp.concatenate([fr * gr_all + fi * gi_all, fr * gi_all - fi * gr_all], axis=1).astype(MXU_DTYPE)
        du_ref[...] = (_dot_nt(dproj, b_ref[...]) + d_ref[...] * dy).astype(du_ref.dtype)
        db_ref[...] += _dot_tn(u, dproj)

    const = lambda shape: pl.BlockSpec(shape, lambda b, c: (0, 0))
    rows = lambda w, cb: pl.BlockSpec((lc, w), functools.partial(lambda b, c, cb: (b * nc + nc - 1 - c, cb), cb=cb))
    return pl.pallas_call(
        body, name="ssm_bwd", grid=(n_seq, nc),
        out_shape=[jax.ShapeDtypeStruct(dz.shape, dz.dtype), jax.ShapeDtypeStruct((SSM_WIDTH, n2), F32),
                   jax.ShapeDtypeStruct((n2, SSM_WIDTH), F32), jax.ShapeDtypeStruct((1, n2), F32),
                   jax.ShapeDtypeStruct((1, n2), F32), jax.ShapeDtypeStruct((1, SSM_WIDTH), F32)],
        in_specs=[rows(SSM_WIDTH, 0), rows(ZS_W, ZS_BLK), rows(n2, 0), const((SSM_WIDTH, n2)), const((1, n2)),
                  const((1, n2)), const((n2, SSM_WIDTH)), const((1, SSM_WIDTH)), _ANY],
        out_specs=[rows(ZS_W, ZS_BLK), const((SSM_WIDTH, n2)), const((n2, SSM_WIDTH)), const((1, n2)), const((1, n2)),
                   const((1, SSM_WIDTH))],
        input_output_aliases={8: 0},
        scratch_shapes=[pltpu.VMEM((lc + 8, n2), F32), pltpu.VMEM((8, n2), F32)],
        compiler_params=_params(("arbitrary", "arbitrary")))(dy, z, states, bmat, a_row, f_row, cmat, d_row, dz)


def _conv_chunk(t):
    return _pick(t, (512, 256, 128))


def _glu(c):
    return c[:, :CONV_WIDTH] * jax.nn.sigmoid(c[:, CONV_WIDTH:])


def _conv_post_tile(v, g, b):
    mu = jnp.mean(v, axis=-1, keepdims=True)
    var = jnp.mean(jnp.square(v - mu), axis=-1, keepdims=True)
    return jax.nn.silu((v - mu) * lax.rsqrt(var + EPS) * g + b)


def _conv_specs(lc, nc):
    per = lc // CONV_HALO
    return [pl.BlockSpec((lc, ZC_W), lambda b, c: (b * nc + c, ZC_BLK)),
            pl.BlockSpec((CONV_HALO, ZC_W), lambda b, c: (jnp.maximum((b * nc + c) * per - 1, 0), ZC_BLK))]


def _conv_fill(c_ref, cp_ref, ue_ref, lc):
    ue_ref[0:CONV_HALO, :] = jnp.where(pl.program_id(1) > 0, _glu(cp_ref[...]), 0.0)
    ue_ref[CONV_HALO:CONV_HALO + lc, :] = _glu(c_ref[...])


def _conv_apply(ue_ref, w_ref, b_ref, lc):
    acc = jnp.zeros((lc, CONV_WIDTH), F32) + b_ref[...]
    for k in range(CONV_K):
        acc = acc + w_ref[k:k + 1, :] * ue_ref[pl.ds(k + CONV_HALO - CONV_K + 1, lc), :]
    return acc


def _conv_fwd(z, dw_w, dw_b, ln_g, ln_b, n_seq):
    t = z.shape[0]
    seq = t // n_seq
    lc = _conv_chunk(seq)
    nc = seq // lc

    def body(c_ref, cp_ref, w_ref, b_ref, g_ref, lb_ref, o_ref, ue_ref):
        _conv_fill(c_ref, cp_ref, ue_ref, lc)
        o_ref[...] = _conv_post_tile(_conv_apply(ue_ref, w_ref, b_ref, lc), g_ref[...], lb_ref[...]).astype(o_ref.dtype)

    const = lambda a: pl.BlockSpec(a.shape, lambda b, c: (0, 0))
    return pl.pallas_call(
        body, name="conv_fwd", grid=(n_seq, nc), out_shape=jax.ShapeDtypeStruct((t, CONV_WIDTH), MXU_DTYPE),
        in_specs=_conv_specs(lc, nc) + [const(dw_w), const(dw_b), const(ln_g), const(ln_b)],
        out_specs=pl.BlockSpec((lc, CONV_WIDTH), lambda b, c: (b * nc + c, 0)),
        scratch_shapes=[pltpu.VMEM((CONV_HALO + lc, CONV_WIDTH), F32)],
        compiler_params=_params(("parallel", "parallel")))(z, z, dw_w, dw_b, ln_g, ln_b)


def _conv_bwd_post(z, duc, dw_w, dw_b, ln_g, ln_b, n_seq):
    t = z.shape[0]
    seq = t // n_seq
    lc = _conv_chunk(seq)
    nc = seq // lc

    def body(c_ref, cp_ref, duc_ref, w_ref, b_ref, g_ref, lb_ref, dv_ref, dg_ref, dlb_ref, db_ref, ue_ref):
        @pl.when((pl.program_id(0) == 0) & (pl.program_id(1) == 0))
        def _():
            for r in (dg_ref, dlb_ref, db_ref):
                r[...] = jnp.zeros_like(r)

        _conv_fill(c_ref, cp_ref, ue_ref, lc)
        _, vjp = jax.vjp(_conv_post_tile, _conv_apply(ue_ref, w_ref, b_ref, lc), g_ref[...], lb_ref[...])
        dv, dg, dlb = vjp(duc_ref[...])
        dv_ref[...] = dv
        dg_ref[...] += dg
        dlb_ref[...] += dlb
        db_ref[...] += jnp.sum(dv, axis=0, keepdims=True)

    const = lambda a: pl.BlockSpec(a.shape, lambda b, c: (0, 0))
    vec = jax.ShapeDtypeStruct((1, CONV_WIDTH), F32)
    return pl.pallas_call(
        body, name="conv_bwd_post", grid=(n_seq, nc), out_shape=[jax.ShapeDtypeStruct((t, CONV_WIDTH), F32), vec, vec, vec],
        in_specs=_conv_specs(lc, nc) + [pl.BlockSpec((lc, CONV_WIDTH), lambda b, c: (b * nc + c, 0)),
                                       const(dw_w), const(dw_b), const(ln_g), const(ln_b)],
        out_specs=[pl.BlockSpec((lc, CONV_WIDTH), lambda b, c: (b * nc + c, 0))] + [const(dw_b)] * 3,
        scratch_shapes=[pltpu.VMEM((CONV_HALO + lc, CONV_WIDTH), F32)],
        compiler_params=_params(("arbitrary", "arbitrary")))(z, z, duc, dw_w, dw_b, ln_g, ln_b)


def _conv_bwd_taps(z, dv, dw_w, dz, n_seq):
    t = z.shape[0]
    seq = t // n_seq
    lc = _conv_chunk(seq)
    nc = seq // lc
    per = lc // CONV_HALO
    n_halo = t // CONV_HALO

    def body(c_ref, cp_ref, dv_ref, dvn_ref, w_ref, _, dc_ref, dw_ref, ue_ref, dve_ref):
        @pl.when((pl.program_id(0) == 0) & (pl.program_id(1) == 0))
        def _():
            dw_ref[...] = jnp.zeros_like(dw_ref)

        _conv_fill(c_ref, cp_ref, ue_ref, lc)
        dv = dv_ref[...]
        dve_ref[0:lc, :] = dv
        dve_ref[lc:lc + CONV_HALO, :] = jnp.where(pl.program_id(1) < nc - 1, dvn_ref[...], 0.0)
        du = jnp.zeros((lc, CONV_WIDTH), F32)
        for k in range(CONV_K):
            du = du + w_ref[k:k + 1, :] * dve_ref[pl.ds(CONV_K - 1 - k, lc), :]
            dw_ref[k:k + 1, :] += jnp.sum(dv * ue_ref[pl.ds(k + CONV_HALO - CONV_K + 1, lc), :], axis=0, keepdims=True)
        c = c_ref[...]
        a, sg = c[:, :CONV_WIDTH], jax.nn.sigmoid(c[:, CONV_WIDTH:])
        dc_ref[:, :CONV_WIDTH] = (du * sg).astype(dc_ref.dtype)
        dc_ref[:, CONV_WIDTH:] = (du * a * sg * (1.0 - sg)).astype(dc_ref.dtype)

    return pl.pallas_call(
        body, name="conv_bwd_taps", grid=(n_seq, nc),
        out_shape=[jax.ShapeDtypeStruct(dz.shape, dz.dtype), jax.ShapeDtypeStruct((CONV_HALO, CONV_WIDTH), F32)],
        in_specs=_conv_specs(lc, nc) + [
            pl.BlockSpec((lc, CONV_WIDTH), lambda b, c: (b * nc + c, 0)),
            pl.BlockSpec((CONV_HALO, CONV_WIDTH), lambda b, c: (jnp.minimum((b * nc + c + 1) * per, n_halo - 1), 0)),
            pl.BlockSpec(dw_w.shape, lambda b, c: (0, 0)), _ANY],
        out_specs=[pl.BlockSpec((lc, ZC_W), lambda b, c: (b * nc + c, ZC_BLK)),
                   pl.BlockSpec((CONV_HALO, CONV_WIDTH), lambda b, c: (0, 0))],
        input_output_aliases={5: 0},
        scratch_shapes=[pltpu.VMEM((CONV_HALO + lc, CONV_WIDTH), F32), pltpu.VMEM((lc + CONV_HALO, CONV_WIDTH), F32)],
        compiler_params=_params(("arbitrary", "arbitrary")))(z, z, dv, dv, dw_w, dz)


def _row(v):
    return v.reshape(1, -1)


def _ssm_mats(b_re, b_im, c_re, c_im):
    eye = jnp.eye(SSM_GROUPS, dtype=bool)
    bm = jnp.stack([b_re, b_im]).transpose(1, 3, 0, 2)[:, :, :, None, :]
    bmat = jnp.where(eye[:, None, None, :, None], bm, 0.0).reshape(SSM_WIDTH, 2 * SSM_LANES)
    cm = jnp.stack([c_re, -c_im]).transpose(0, 1, 3, 2)[:, :, :, None, :]
    cmat = jnp.where(eye[None, :, None, :, None], cm, 0.0).reshape(2 * SSM_LANES, SSM_WIDTH)
    return bmat.astype(MXU_DTYPE), cmat.astype(MXU_DTYPE)


def _ssm_mats_t(dbmat, dcmat):
    eye = jnp.eye(SSM_GROUPS, dtype=bool)
    db = dbmat.reshape(SSM_GROUPS, SSM_GROUP, 2, SSM_GROUPS, SSM_STATE)
    db = jnp.sum(jnp.where(eye[:, None, None, :, None], db, 0.0), axis=3).transpose(2, 0, 3, 1)
    dc = dcmat.reshape(2, SSM_GROUPS, SSM_STATE, SSM_GROUPS, SSM_GROUP)
    dc = jnp.sum(jnp.where(eye[None, :, None, :, None], dc, 0.0), axis=3).transpose(0, 1, 3, 2)
    return db[0], db[1], dc[0], -dc[1]


def _layer_fwd(x, p, w, get_ffn_weights, sp, ctab, stab, n_seq):
    z, h = _in_proj(x, sp["mix_norm_g"], w["w_in"])
    ya, lse = _attention_fwd(z, ctab, stab, sp["attn_sinks"], n_seq)
    ys, states = _ssm_fwd(z, sp["bmat"], sp["a_row"], sp["f_row"], sp["cmat"], sp["ssm_d"], n_seq)
    uc = _conv_fwd(z, w["conv_dw_w"], sp["conv_dw_b"], sp["conv_norm_g"], sp["conv_norm_b"], n_seq)
    merge_consts = [w["w_attn_out"], w["w_ssm_glu"], sp["b_ssm_glu"], w["w_conv_out"], sp["b_gate"], w["w_mix_out"],
                    sp["ffn_norm_g"]]
    (x1, hf), _ = _token_call("merge", lambda *a: (list(_merge_tile(*_f32s(a))), []), 512,
                              [_whole(x), _whole(ya), _whole(ys), _whole(uc), (z, ZG_W, 0)], merge_consts,
                              [(x.shape[1], F32), (x.shape[1], MXU_DTYPE)], [])
    w = dict(w, **get_ffn_weights(x1))
    gate, up, act = _ffn_in_act(hf, w["w_ffn_in"])
    ffn = _matmul_nn("mm_ffn_out", act, w["w_ffn_out"], F32)

    def ple_fn(x1, ffn, p, w_pi, g_ple, w_pg):
        x2 = x1 + ffn
        return [x2, _ple_tile(x2, p, w_pi.astype(F32), g_ple, w_pg.astype(F32))], []

    (x2, x3), _ = _token_call("ple", ple_fn, 512, [_whole(x1), _whole(ffn), _whole(p)],
                              [w["w_ple_in"], sp["ple_norm_g"], w["w_ple_gate"]], [(x.shape[1], F32)] * 2, [])
    saved = dict(x=x, h=h, z=z, ya=ya, lse=lse, ys=ys, states=states, uc=uc, hf=hf, gate=gate, up=up, act=act, x2=x2, p=p)
    return x3, saved, w


def _layer_bwd(dx3, sv, w, sp, on_ffn_grads, ctab, stab, n_seq):
    d = dx3.shape[1]
    gw, gs = {}, {}

    def ple_bwd(x2, p, dx3, w_pi, g_ple, w_pg):
        _, vjp = jax.vjp(lambda x2, w_pi, g_ple, w_pg: _ple_tile(x2, p, w_pi, g_ple, w_pg), x2, w_pi.astype(F32), g_ple,
                         w_pg.astype(F32))
        dx2, dw_pi, dg_ple, dw_pg = vjp(dx3)
        return [dx2, dx2], [dw_pi, dg_ple, dw_pg]

    (dx2, dffn), (gw["w_ple_in"], gs["ple_norm_g"], gw["w_ple_gate"]) = _token_call(
        "ple_bwd", ple_bwd, 512, [_whole(sv["x2"]), _whole(sv["p"]), _whole(dx3)],
        [w["w_ple_in"], sp["ple_norm_g"], w["w_ple_gate"]], [(d, F32), (d, MXU_DTYPE)],
        [w["w_ple_in"].shape, (1, d), w["w_ple_gate"].shape])

    dgate, dup = _ffn_mid_bwd(dffn, w["w_ffn_out"], sv["gate"], sv["up"])
    gw["w_ffn_out"] = _matmul_tn("mm_ffn_out_dw", sv["act"], dffn)
    dhf = _ffn_in_dx(dgate, dup, w["w_ffn_in"])
    gw["w_ffn_in"] = jnp.concatenate([_matmul_tn("mm_ffn_gate_dw", sv["hf"], dgate), _matmul_tn("mm_ffn_up_dw", sv["hf"], dup)],
                                     axis=1)

    token = on_ffn_grads(gw)

    def merge_bwd(x, ya, ys, uc, gin, dx1, dhf, *consts):
        consts = _f32s(consts)
        _, vjp = jax.vjp(_merge_tile, *_f32s((x, ya, ys, uc, gin)), *consts)
        g = vjp((dx1, dhf))
        return list(g[:5]), list(g[5:])

    b_gate = sp["b_gate"] if token is None else sp["b_gate"] + token[0:1, 0:1]
    merge_consts = [w["w_attn_out"], w["w_ssm_glu"], sp["b_ssm_glu"], w["w_conv_out"], b_gate, w["w_mix_out"], sp["ffn_norm_g"]]
    dz = lax.empty(sv["z"].shape, MXU_DTYPE)
    (dx_res, dya, dys, duc, dz), macc = _token_call(
        "merge_bwd", merge_bwd, 256,
        [_whole(sv["x"]), _whole(sv["ya"]), _whole(sv["ys"]), _whole(sv["uc"]), (sv["z"], ZG_W, 0), _whole(dx2), _whole(dhf)],
        merge_consts, [(d, F32), (Q_WIDTH, MXU_DTYPE), (SSM_WIDTH, F32), (CONV_WIDTH, F32)],
        [c.shape for c in merge_consts], into=(dz, ZG_W, 0))
    gw["w_attn_out"], gw["w_ssm_glu"], gs["b_ssm_glu"], gw["w_conv_out"], gs["b_gate"], gw["w_mix_out"], gs["ffn_norm_g"] = macc

    dv, gs["conv_norm_g"], gs["conv_norm_b"], gs["conv_dw_b"] = _conv_bwd_post(
        sv["z"], duc, w["conv_dw_w"], sp["conv_dw_b"], sp["conv_norm_g"], sp["conv_norm_b"], n_seq)
    dz, dw_taps = _conv_bwd_taps(sv["z"], dv, w["conv_dw_w"], dz, n_seq)
    gw["conv_dw_w"] = dw_taps[:CONV_K]

    dz, gs["bmat"], gs["cmat"], gs["a_row"], gs["f_row"], gs["ssm_d"] = _ssm_bwd(
        sv["z"], sv["states"], dys, sp["bmat"], sp["a_row"], sp["f_row"], sp["cmat"], sp["ssm_d"], dz, n_seq)

    dz, dkv, dkvp, gs["attn_sinks"] = _attention_bwd(sv["z"], ctab, stab, sp["attn_sinks"], sv["ya"], sv["lse"], dya, dz, n_seq)
    dz = _kv_combine(dkv, dkvp, dz, n_seq)
    gw["w_in"] = _matmul_tn("mm_in_dw", sv["h"], dz)
    dx, gs["mix_norm_g"] = _in_proj_bwd(dz, w["w_in"], sv["x"], dx_res, sp["mix_norm_g"])
    return dx, gw, gs


def _loss_and_grad(x, target, g):
    def fn(x, tgt, g):
        def f(x, g):
            err = _rms(x, g) - tgt
            return 0.5 * jnp.mean(err * err, axis=-1, keepdims=True)

        per_token, vjp = jax.vjp(f, x, g)
        dx, dg = vjp(jnp.ones_like(per_token))
        return [dx], [jnp.sum(per_token, axis=0, keepdims=True), dg]

    (dx,), (loss, dg) = _token_call("loss", fn, 512, [_whole(x), _whole(target)], [g], [(x.shape[1], F32)],
                                    [(8, LANES), (1, x.shape[1])])
    return loss[0, 0], dx, dg


def _mesh_place():
    return lax.axis_index("x"), lax.axis_index("y"), lax.axis_index("c")


def _flip(v, bit):
    return 1 - v if bit else v


_MESH = pl.DeviceIdType.MESH


def _all_gather(name, xs):
    n = len(xs)

    def body(*refs):
        x_refs, out_refs = refs[:n], refs[n:2 * n]
        send_sems, recv_sems, local_sems = refs[2 * n:]
        mx, my, mc = _mesh_place()
        me, sibling = (mx, my, mc), (mx, my, 1 - mc)
        chips = [(1 - mx, my), (mx, 1 - my), (1 - mx, 1 - my)]

        def slot(a, px, py, pc):
            return out_refs[a].at[4 * px + 2 * py + pc]

        def copy(a, k, block, to, src=None):
            return pltpu.make_async_remote_copy(
                src_ref=slot(a, *block) if src is None else src, dst_ref=slot(a, *block), send_sem=send_sems.at[7 * a + k],
                recv_sem=recv_sems.at[7 * a + k], device_id=to, device_id_type=_MESH)

        mine = [pltpu.make_async_copy(x_refs[a], slot(a, *me), local_sems.at[a]) for a in range(n)]
        for cp in mine:
            cp.start()
        first = [copy(a, 0, me, sibling, src=x_refs[a]) for a in range(n)]
        first += [copy(a, 1 + j, me, (*chip, mc), src=x_refs[a]) for j, chip in enumerate(chips) for a in range(n)]
        for cp in first:
            cp.start()
        passed = []
        for j, chip in enumerate(chips):
            for a in range(n):
                copy(a, 1 + j, (*chip, mc), me).wait_recv()
                passed.append(copy(a, 4 + j, (*chip, mc), sibling))
                passed[-1].start()
        for a in range(n):
            copy(a, 0, sibling, me).wait_recv()
            for j, chip in enumerate(chips):
                copy(a, 4 + j, (*chip, 1 - mc), me).wait_recv()
        for cp in first + passed:
            cp.wait_send()
        for cp in mine:
            cp.wait()

    return pl.pallas_call(
        body, name=name, out_shape=[jax.ShapeDtypeStruct((N_DEV,) + x.shape, x.dtype) for x in xs], in_specs=[_ANY] * n,
        out_specs=[_ANY] * n,
        scratch_shapes=[pltpu.SemaphoreType.DMA((7 * n,)), pltpu.SemaphoreType.DMA((7 * n,)), pltpu.SemaphoreType.DMA((n,))])(*xs)


def _direct_copies(kind, src_refs, land_refs, send_sems, recv_sems, local_sems):
    mx, my, mc = _mesh_place()
    me = 4 * mx + 2 * my + mc
    n = len(src_refs)
    own = [pltpu.make_async_copy(src_refs[a] if kind == "gather" else src_refs[a].at[me], land_refs[a].at[me], local_sems.at[a])
           for a in range(n)]
    copies = []
    for rel in range(1, N_DEV):
        px, py, pc = _flip(mx, rel & 4), _flip(my, rel & 2), _flip(mc, rel & 1)
        for a in range(n):
            src = src_refs[a] if kind == "gather" else src_refs[a].at[4 * px + 2 * py + pc]
            copies.append(pltpu.make_async_remote_copy(
                src_ref=src, dst_ref=land_refs[a].at[me], send_sem=send_sems.at[7 * a + rel - 1],
                recv_sem=recv_sems.at[7 * a + rel - 1], device_id=(px, py, pc), device_id_type=_MESH))
    return copies, own


_HBM = pl.BlockSpec(memory_space=pltpu.HBM)
_SEM = pl.BlockSpec(memory_space=pltpu.SEMAPHORE)
_DATAFLOW = pltpu.SideEffectType.DATAFLOW_SIDE_EFFECTING


def _exchange_start(name, kind, groups):
    sizes = [len(g) for g in groups]
    srcs = [s for g in groups for s in g]
    lands = [lax.empty(((N_DEV,) + s.shape) if kind == "gather" else s.shape, s.dtype) for s in srcs]
    n, n_g = len(srcs), len(groups)
    first = [sum(sizes[:g]) for g in range(n_g)]

    def body(*refs):
        src_refs, land_refs, sems = refs[:n], refs[n:2 * n], refs[2 * n:2 * n + 3 * n_g]
        for g in range(n_g):
            span = slice(first[g], first[g] + sizes[g])
            copies, own = _direct_copies(kind, src_refs[span], land_refs[span], *sems[3 * g:3 * g + 3])
            for cp in own + copies:
                cp.start()
        refs[-1][...] = jnp.zeros_like(refs[-1])

    hbm = lambda a: pltpu.with_memory_space_constraint(a, pltpu.HBM)
    sem_shapes = [pltpu.SemaphoreType.DMA((k * m,)) for m in sizes for k in (7, 7, 1)]
    out = pl.pallas_call(
        body, name=name,
        out_shape=sem_shapes + [pltpu.HBM(a.shape, a.dtype) for a in srcs + lands] + [jax.ShapeDtypeStruct((8, LANES), F32)],
        in_specs=[_HBM] * (2 * n), out_specs=[_SEM] * (3 * n_g) + [_HBM] * (2 * n) + [pl.BlockSpec(memory_space=pltpu.VMEM)],
        input_output_aliases={i: 3 * n_g + i for i in range(2 * n)},
        compiler_params=pltpu.CompilerParams(has_side_effects=_DATAFLOW))(*[hbm(a) for a in srcs + lands])
    sems, arrays = out[:3 * n_g], out[3 * n_g:-1]
    started = [(kind, (*sems[3 * g:3 * g + 3], *arrays[first[g]:first[g] + sizes[g]],
                       *arrays[n + first[g]:n + first[g] + sizes[g]])) for g in range(n_g)]
    return started, out[-1]


def _exchange_wait(name, started, after):
    kind, (send_sems, recv_sems, local_sems, *arrays) = started
    n = len(arrays) // 2

    def body(*refs):
        src_refs, land_refs = refs[:n], refs[n:2 * n]
        copies, own = _direct_copies(kind, src_refs, land_refs, *refs[2 * n:2 * n + 3])
        for cp in copies + own:
            cp.wait()

    out = pl.pallas_call(
        body, name=name, out_shape=[pltpu.HBM(a.shape, a.dtype) for a in arrays],
        in_specs=[_HBM] * (2 * n) + [_SEM] * 3 + [_ANY], out_specs=[_HBM] * (2 * n),
        input_output_aliases={i: i for i in range(2 * n)},
        compiler_params=pltpu.CompilerParams(has_side_effects=_DATAFLOW))(*arrays, send_sems, recv_sems, local_sems, after)
    return out[n:]


def _adamw_math(g, w, m, v):
    m2 = ADAM_B1 * m + (1.0 - ADAM_B1) * g
    v2 = ADAM_B2 * v + (1.0 - ADAM_B2) * jnp.square(g)
    m_hat = m2 / (1.0 - ADAM_B1 ** ADAM_STEP)
    v_hat = v2 / (1.0 - ADAM_B2 ** ADAM_STEP)
    return g, -ADAM_LR * (m_hat / (jnp.sqrt(v_hat) + ADAM_EPS) + ADAM_WD * w), m2, v2


def _sum_blocks(ref):
    g = ref[0].astype(F32)
    for j in range(1, N_DEV):
        g = g + ref[j].astype(F32)
    return g


def _adamw_flat(name, parts, w, m, v):
    r = w.shape[0]
    tile = _pick(r, (1024, 512, 256, 128, 8))

    def body(p_ref, w_ref, m_ref, v_ref, *o_refs):
        for o, val in zip(o_refs, _adamw_math(_sum_blocks(p_ref), w_ref[...], m_ref[...], v_ref[...]), strict=True):
            o[...] = val

    flat = pl.BlockSpec((tile, LANES), lambda i: (i, 0))
    return pl.pallas_call(
        body, name=name, grid=(r // tile,), out_shape=[jax.ShapeDtypeStruct((r, LANES), F32)] * 4,
        in_specs=[pl.BlockSpec((N_DEV, tile, LANES), lambda i: (0, i, 0)), flat, flat, flat], out_specs=[flat] * 4,
        compiler_params=_params(("parallel",)))(parts, w, m, v)


def _adamw_cols(name, landed, base, stride, w, m, v):
    depth, rows, cs = w.shape
    n_slab = -(-cs // LANES)
    tr = _pick(rows, (SLAB_TILE,))

    def body(*refs):
        slabs, (w_ref, m_ref, v_ref), o_refs = refs[:n_slab], refs[n_slab:n_slab + 3], refs[n_slab + 3:]
        g = jnp.concatenate([_sum_blocks(s)[:, :min(LANES, cs - LANES * k)] for k, s in enumerate(slabs)], axis=1)
        for o, val in zip(o_refs, _adamw_math(g, w_ref[...], m_ref[...], v_ref[...]), strict=True):
            o[...] = val

    slab = lambda k: pl.BlockSpec((N_DEV, tr, LANES), lambda l, i: (0, (l * stride + base + k * rows) // tr + i, 0))
    nat = pl.BlockSpec((None, tr, cs), lambda l, i: (l, i, 0))
    return pl.pallas_call(
        body, name=name, grid=(depth, rows // tr), out_shape=[jax.ShapeDtypeStruct(w.shape, F32)] * 4,
        in_specs=[slab(k) for k in range(n_slab)] + [nat] * 3, out_specs=[nat] * 4,
        compiler_params=_params(("parallel", "parallel")))(*[landed] * n_slab, w, m, v)


def _adamw_rows(name, landed, base, stride, w, m, v):
    depth, rs, width = w.shape
    tr = math.gcd(rs, base, stride)

    def body(p_ref, w_ref, m_ref, v_ref, *o_refs):
        for o, val in zip(o_refs, _adamw_math(_sum_blocks(p_ref), w_ref[...], m_ref[...], v_ref[...]), strict=True):
            o[...] = val

    nat = pl.BlockSpec((None, tr, width), lambda l, i: (l, i, 0))
    return pl.pallas_call(
        body, name=name, grid=(depth, rs // tr), out_shape=[jax.ShapeDtypeStruct(w.shape, F32)] * 4,
        in_specs=[pl.BlockSpec((N_DEV, tr, width), lambda l, i: (0, (l * stride + base) // tr + i, 0)), nat, nat, nat],
        out_specs=[nat] * 4, compiler_params=_params(("parallel", "parallel")))(landed, w, m, v)


def _adamw_conv(landed, base, stride, w, m, v):
    depth, taps, cs = w.shape

    def body(p_ref, w_ref, m_ref, v_ref, *o_refs):
        g = _sum_blocks(p_ref)[:taps, :cs]
        for o, val in zip(o_refs, _adamw_math(g, w_ref[...], m_ref[...], v_ref[...]), strict=True):
            o[...] = val

    nat = pl.BlockSpec((None, taps, cs), lambda l: (l, 0, 0))
    return pl.pallas_call(
        body, name="adamw_conv", grid=(depth,), out_shape=[jax.ShapeDtypeStruct(w.shape, F32)] * 4,
        in_specs=[pl.BlockSpec((N_DEV, CONV_HALO, LANES), lambda l: (0, (l * stride + base) // CONV_HALO, 0)), nat, nat, nat],
        out_specs=[nat] * 4, compiler_params=_params(("parallel",)))(landed, w, m, v)


def _unshard_cols(name, gathered, start, rows, cs, shift=0):
    n_slab = -(-cs // LANES)
    total = N_DEV * cs
    tr = _pick(rows, (SLAB_TILE,))

    def body(*refs):
        slabs, o_ref = refs[:n_slab], refs[n_slab]
        for j in range(N_DEV):
            for k, s in enumerate(slabs):
                for src, dst, width in _wrapped(j * cs + LANES * k - shift, min(LANES, cs - LANES * k), total):
                    o_ref[:, dst:dst + width] = s[j, :, src:src + width]

    slab = lambda k: pl.BlockSpec((N_DEV, tr, LANES), lambda i: (0, (start + k * rows) // tr + i, 0))
    return pl.pallas_call(
        body, name=name, grid=(rows // tr,), out_shape=jax.ShapeDtypeStruct((rows, total), gathered.dtype),
        in_specs=[slab(k) for k in range(n_slab)], out_specs=pl.BlockSpec((tr, total), lambda i: (i, 0)),
        compiler_params=_params(("parallel",)))(*[gathered] * n_slab)


def _shard_cols(name, full, cs, shift=0):
    rows, total = full.shape
    n_slab = -(-cs // LANES)
    tr = _pick(rows, (SLAB_TILE,))

    def body(f_ref, o_ref):
        for j in range(N_DEV):
            for k in range(n_slab):
                used = min(LANES, cs - LANES * k)
                for src, dst, width in _wrapped(j * cs + LANES * k - shift, used, total):
                    o_ref[j, k, :, src:src + width] = f_ref[:, dst:dst + width].astype(o_ref.dtype)
                if used < LANES:
                    o_ref[j, k, :, used:] = jnp.zeros((tr, LANES - used), o_ref.dtype)

    out = pl.pallas_call(
        body, name=name, grid=(rows // tr,), out_shape=jax.ShapeDtypeStruct((N_DEV, n_slab, rows, LANES), BF16),
        in_specs=[pl.BlockSpec((tr, total), lambda i: (i, 0))],
        out_specs=pl.BlockSpec((N_DEV, n_slab, tr, LANES), lambda i: (0, 0, i, 0)),
        compiler_params=_params(("parallel",)))(full)
    return out.reshape(N_DEV, n_slab * rows, LANES)


def _wrapped(pos, width, total):
    pos %= total
    if pos + width <= total:
        return [(0, pos, width)]
    head = total - pos
    return [(0, pos, head), (head, 0, width - head)]


CONV_W_PIECES = 3


def _pad_to(n, align):
    return -(-n // align) * align


def _layout(group):
    col_names, row_names, with_conv = GROUPS[group]
    dims = {name: (rows, cols) for name, rows, cols, _ in SHARDED}
    col, off = {}, 0
    for name in col_names:
        rows, cols = dims[name]
        cs = cols // N_DEV
        col[name] = (off, rows, cs)
        off += -(-cs // LANES) * rows
    conv_base = off
    col_rows = _pad_to(off + with_conv * CONV_W_PIECES * CONV_HALO, SLAB_TILE)
    row, off = {}, 0
    for name in row_names:
        rs = dims[name][0] // N_DEV
        row[name] = (off, rs)
        off += _pad_to(rs, LANES)
    return col, conv_base, col_rows, row, off


def _slabs(shard):
    rows, cs = shard.shape
    parts = []
    for k in range(-(-cs // LANES)):
        part = shard[:, LANES * k:min(LANES * (k + 1), cs)]
        parts.append(jnp.pad(part, ((0, 0), (0, LANES - part.shape[1]))))
    return jnp.concatenate(parts, axis=0)


def _concat_padded(pieces, total, axis):
    used = sum(p.shape[axis] for p in pieces)
    if total > used:
        shape = list(pieces[0].shape)
        shape[axis] = total - used
        pieces = pieces + [jnp.zeros(shape, pieces[0].dtype)]
    return jnp.concatenate(pieces, axis=axis)


def _split3(a):
    hi = a.astype(BF16)
    r1 = a - hi.astype(F32)
    mid = r1.astype(BF16)
    return hi, mid, (r1 - mid.astype(F32)).astype(BF16)


def _pack_small(arrs, lead=()):
    flat = jnp.concatenate([a.reshape(lead + (-1,)) for a in arrs], axis=-1)
    total = _pad_to(flat.shape[-1], 512 * LANES)
    flat = jnp.pad(flat, [(0, 0)] * len(lead) + [(0, total - flat.shape[-1])])
    return flat.reshape(lead + (total // LANES, LANES))


def _unpack_small(flat, shapes):
    flat = flat.reshape(-1)
    res, off = [], 0
    for s in shapes:
        n = int(np.prod(s))
        res.append(flat[off:off + n].reshape(s))
        off += n
    return res


def _small_rows(a, depth):
    n16 = depth * SSM_GROUPS
    a_re, a_im, f_re, f_im = _ssm_coeffs(a["ssm_lambda_re"].reshape(n16, SSM_STATE), a["ssm_lambda_im"].reshape(n16, SSM_STATE),
                                         a["ssm_log_dt"].reshape(n16, 1))
    rows = []
    for l in range(depth):
        sp = {k: _row(a[k][l]) for k in ("mix_norm_g", "b_gate", "attn_sinks", "ssm_d", "b_ssm_glu", "conv_dw_b",
                                         "conv_norm_g", "conv_norm_b", "ffn_norm_g", "ple_norm_g")}
        g = slice(l * SSM_GROUPS, (l + 1) * SSM_GROUPS)
        sp["a_row"] = jnp.concatenate([a_re[g].reshape(1, -1), a_im[g].reshape(1, -1)], axis=1)
        sp["f_row"] = jnp.concatenate([f_re[g].reshape(1, -1), f_im[g].reshape(1, -1)], axis=1)
        sp["bmat"], sp["cmat"] = _ssm_mats(a["ssm_b_re"][l], a["ssm_b_im"][l], a["ssm_c_re"][l], a["ssm_c_im"][l])
        rows.append(sp)
    return rows


def _local_step(a, get_weights, on_grads, depth):
    n_seq, seq, d = a["x"].shape
    t = n_seq * seq
    inv = ROPE_THETA ** (-jnp.arange(0, ROPE_DIM, 2, dtype=F32) / ROPE_DIM)
    lane = np.arange(LANES) % HEAD_DIM
    inv_lane = jnp.where(lane < ROPE_DIM, jnp.tile(inv, LANES // (ROPE_DIM // 2)), 0.0).reshape(1, LANES)
    ctab, stab = _rope_tables(a["positions"].reshape(t), inv_lane)
    small = _small_rows(a, depth)

    x = a["x"].reshape(t, d)
    saved, weights = [], []
    for l in range(depth):
        x, sv, w = _layer_fwd(x, a["p"][l].reshape(t, -1), get_weights(l, "mix", x),
                              functools.partial(get_weights, l, "ffn"), small[l], ctab, stab, n_seq)
        saved.append(sv)
        weights.append(w)
    loss, dx, d_final = _loss_and_grad(x, a["loss_target"].reshape(t, d), _row(a["final_norm_g"]))
    gws, gss, token = [None] * depth, [None] * depth, None
    for l in reversed(range(depth)):
        sp = small[l] if token is None else dict(small[l], ple_norm_g=small[l]["ple_norm_g"] + token[0:1, 0:1])
        dx, gws[l], gss[l] = _layer_bwd(dx, saved[l], weights[l], sp, functools.partial(on_grads, l, "ffn"), ctab, stab, n_seq)
        token = on_grads(l, "mix", gws[l])

    n16 = depth * SSM_GROUPS
    halves = lambda k, h: jnp.concatenate([gss[l][k][:, h * SSM_LANES:(h + 1) * SSM_LANES].reshape(SSM_GROUPS, SSM_STATE)
                                           for l in range(depth)], axis=0)
    dlr, dli, ddt = _ssm_coeffs_bwd(a["ssm_lambda_re"].reshape(n16, SSM_STATE), a["ssm_lambda_im"].reshape(n16, SSM_STATE),
                                    a["ssm_log_dt"].reshape(n16, 1),
                                    (halves("a_row", 0), halves("a_row", 1), halves("f_row", 0), halves("f_row", 1)))
    bc = [_ssm_mats_t(gss[l]["bmat"], gss[l]["cmat"]) for l in range(depth)]
    gsmall = {k: jnp.stack([gss[l][k].reshape(a[k].shape[1:]) for l in range(depth)])
              for k in ("mix_norm_g", "b_gate", "attn_sinks", "ssm_d", "b_ssm_glu", "conv_dw_b", "conv_norm_g", "conv_norm_b",
                        "ffn_norm_g", "ple_norm_g")}
    gsmall["ssm_lambda_re"] = dlr.reshape(a["ssm_lambda_re"].shape)
    gsmall["ssm_lambda_im"] = dli.reshape(a["ssm_lambda_im"].shape)
    gsmall["ssm_log_dt"] = ddt.reshape(a["ssm_log_dt"].shape)
    for i, k in enumerate(("ssm_b_re", "ssm_b_im", "ssm_c_re", "ssm_c_im")):
        gsmall[k] = jnp.stack([bc[l][i] for l in range(depth)])
    gsmall["final_norm_g"] = d_final.reshape(a["final_norm_g"].shape)
    return loss, dx.reshape(n_seq, seq, d), gws, gsmall


def kernel(x, p, positions, mix_norm_g, w_in, b_gate, attn_sinks, w_attn_out, ssm_lambda_re, ssm_lambda_im, ssm_log_dt, ssm_b_re, ssm_b_im, ssm_c_re, ssm_c_im, ssm_d, w_ssm_glu, b_ssm_glu, conv_dw_w, conv_dw_b, conv_norm_g, conv_norm_b, w_conv_out, w_mix_out, ffn_norm_g, w_ffn_in, w_ffn_out, w_ple_in, ple_norm_g, w_ple_gate, final_norm_g, loss_target, m_mix_norm_g, m_w_in, m_b_gate, m_attn_sinks, m_w_attn_out, m_ssm_lambda_re, m_ssm_lambda_im, m_ssm_log_dt, m_ssm_b_re, m_ssm_b_im, m_ssm_c_re, m_ssm_c_im, m_ssm_d, m_w_ssm_glu, m_b_ssm_glu, m_conv_dw_w, m_conv_dw_b, m_conv_norm_g, m_conv_norm_b, m_w_conv_out, m_w_mix_out, m_ffn_norm_g, m_w_ffn_in, m_w_ffn_out, m_w_ple_in, m_ple_norm_g, m_w_ple_gate, m_final_norm_g, v_mix_norm_g, v_w_in, v_b_gate, v_attn_sinks, v_w_attn_out, v_ssm_lambda_re, v_ssm_lambda_im, v_ssm_log_dt, v_ssm_b_re, v_ssm_b_im, v_ssm_c_re, v_ssm_c_im, v_ssm_d, v_w_ssm_glu, v_b_ssm_glu, v_conv_dw_w, v_conv_dw_b, v_conv_norm_g, v_conv_norm_b, v_w_conv_out, v_w_mix_out, v_ffn_norm_g, v_w_ffn_in, v_w_ffn_out, v_w_ple_in, v_ple_norm_g, v_w_ple_gate, v_final_norm_g):
    a = dict(locals())
    depth = w_in.shape[0]
    layouts = {group: _layout(group) for group in GROUPS}
    shift = {"w_in": Z_SPLIT}
    conv_pad = ((0, 0), (0, CONV_HALO - CONV_K), (0, LANES - CONV_WIDTH // N_DEV))

    def packed_weights(l, group):
        col, _, col_rows, row, _ = layouts[group]
        pieces = [_slabs(a[name][l].astype(BF16)) for name in col]
        if GROUPS[group][2]:
            pieces.append(jnp.pad(jnp.stack(_split3(a["conv_dw_w"][l])), conv_pad).reshape(-1, LANES))
        regions = [_concat_padded([a[name][l].astype(BF16)], _pad_to(rs, LANES), 0) for name, (_, rs) in row.items()]
        return [_concat_padded(pieces, col_rows, 0), jnp.concatenate(regions, axis=0)]

    gathers, tokens = [], []
    for l in range(depth):
        started, token = _exchange_start(f"gather_start_{l}", "gather", [packed_weights(l, group) for group in GROUPS])
        gathers.append(dict(zip(GROUPS, started, strict=True)))
        tokens.append(token[0:1, 0:1])

    def get_weights(l, group, after):
        col, conv_base, _, row, _ = layouts[group]
        slab8, row8 = _exchange_wait(f"gather_wait_{group}_{l}", gathers[l][group], after)
        w = {name: _unshard_cols("unshard_" + name, slab8, base, rows, cs, shift.get(name, 0))
             for name, (base, rows, cs) in col.items()}
        for name, (base, rs) in row.items():
            w[name] = row8[:, base:base + rs].reshape(N_DEV * rs, -1)
        if GROUPS[group][2]:
            conv = slab8[:, conv_base:conv_base + CONV_W_PIECES * CONV_HALO]
            conv = conv.reshape(N_DEV, CONV_W_PIECES, CONV_HALO, LANES)[:, :, :CONV_K, :CONV_WIDTH // N_DEV].astype(F32)
            w["conv_dw_w"] = jnp.sum(conv, axis=1).transpose(1, 0, 2).reshape(CONV_K, CONV_WIDTH)
        return w

    scatters = {}

    def on_grads(l, group, gw):
        col, _, col_rows, row, _ = layouts[group]
        pieces = [_shard_cols("shard_" + name, gw[name], cs, shift.get(name, 0)) for name, (_, _, cs) in col.items()]
        if GROUPS[group][2]:
            conv = gw["conv_dw_w"].reshape(CONV_K, N_DEV, CONV_WIDTH // N_DEV).transpose(1, 0, 2).astype(BF16)
            pieces.append(jnp.pad(jnp.pad(conv, conv_pad), ((0, 0), (0, (CONV_W_PIECES - 1) * CONV_HALO), (0, 0))))
        regions = [_concat_padded([gw[name].astype(BF16).reshape(N_DEV, rs, -1)], _pad_to(rs, LANES), 1)
                   for name, (_, rs) in row.items()]
        (scatters[l, group],), token = _exchange_start(
            f"grads_start_{group}_{l}", "scatter", [[_concat_padded(pieces, col_rows, 1), jnp.concatenate(regions, axis=1)]])
        return token

    local = dict(a, mix_norm_g=a["mix_norm_g"] + sum(tokens))
    loss, grad_x, _, gsmall = _local_step(local, get_weights, on_grads, depth)
    loss = lax.psum(loss, ("x", "y", "c"))

    state = lambda name: (a[name], a["m_" + name], a["v_" + name])
    big = {}
    for group, (col, conv_base, col_rows, row, row_rows) in layouts.items():
        landed = [_exchange_wait(f"grads_wait_{group}_{l}", scatters[l, group], grad_x) for l in range(depth)]
        landed_slab = jnp.concatenate([ls for ls, _ in landed], axis=1)
        landed_row = jnp.concatenate([lr for _, lr in landed], axis=1)
        big.update({name: _adamw_cols("adamw_" + name, landed_slab, base, col_rows, *state(name)) for name, (base, _, _) in col.items()})
        big.update({name: _adamw_rows("adamw_" + name, landed_row, base, row_rows, *state(name)) for name, (base, _) in row.items()})
        if GROUPS[group][2]:
            big["conv_dw_w"] = _adamw_conv(landed_slab, conv_base, col_rows, *state("conv_dw_w"))

    shapes = [a[k].shape for k in REPLICATED]
    parts, = _all_gather("gather_small_grads", [_pack_small([gsmall[k] for k in REPLICATED])])
    small_state = [_pack_small([a[pre + k] for k in REPLICATED]) for pre in ("", "m_", "v_")]
    small = [dict(zip(REPLICATED, _unpack_small(o, shapes), strict=True))
             for o in _adamw_flat("adamw_replicated", parts, *small_state)]

    def result(kind, name):
        if name in REPLICATED:
            return small[kind][name]
        return big[name][kind]

    return (loss, grad_x, *[result(kind, n) for kind in range(4) for n in WEIGHT_ORDER])
```

```python
import functools
import math

import numpy as np
import jax
import jax.numpy as jnp
from jax import lax
from jax.experimental import pallas as pl
from jax.experimental.pallas import tpu as pltpu

F32 = jnp.float32
BF16 = jnp.bfloat16
MXU_DTYPE = jnp.bfloat16
VMEM_LIMIT_BYTES = 56 * 2 ** 20
N_DEV = 8
LANES = 128

HEAD_DIM = 64
N_Q_HEADS = 8
N_KV_HEADS = 2
GQA_GROUP = 4
BLOCK = 128
ROPE_THETA = 500000.0
ROPE_DIM = 16
Q_WIDTH = 512
KV_WIDTH = 128
SSM_WIDTH = 256
SSM_GROUP = 16
SSM_GROUPS = 16
SSM_STATE = 64
SSM_LANES = SSM_GROUPS * SSM_STATE
CONV_WIDTH = 256
CONV_K = 31
CONV_HALO = 32
EPS = 1e-6
NEG_INF = -1e30
ADAM_LR, ADAM_B1, ADAM_B2, ADAM_EPS, ADAM_WD, ADAM_STEP = 0.001, 0.9, 0.999, 1e-08, 0.01, 10

ZG_W, ZQ_W, ZKV_W, ZS_W, ZC_W = 3072, 512, 256, 256, 512
ZQ_BLK, ZKV_BLK, ZS_BLK, ZC_BLK = 3072 // 512, 3584 // 256, 3840 // 256, 4096 // 512
Z_WIDTH = 4608
Z_SPLIT = 1536

SHARDED = (("w_in", 1024, 4608, 1), ("w_attn_out", 512, 1024, 1), ("w_ssm_glu", 256, 2048, 1),
           ("conv_dw_w", 31, 256, 1), ("w_conv_out", 256, 1024, 1), ("w_mix_out", 1024, 1024, 0),
           ("w_ffn_in", 1024, 5632, 1), ("w_ffn_out", 2816, 1024, 0), ("w_ple_in", 256, 1024, 1),
           ("w_ple_gate", 1024, 1024, 0))
GROUPS = {"mix": (("w_in", "w_attn_out", "w_ssm_glu", "w_conv_out"), ("w_mix_out",), True),
          "ffn": (("w_ffn_in", "w_ple_in"), ("w_ffn_out", "w_ple_gate"), False)}
SLAB_TILE = 256
FLAT_ROW_ALIGN = 1024
REPLICATED = ("mix_norm_g", "b_gate", "attn_sinks", "ssm_lambda_re", "ssm_lambda_im", "ssm_log_dt", "ssm_b_re",
              "ssm_b_im", "ssm_c_re", "ssm_c_im", "ssm_d", "b_ssm_glu", "conv_dw_b", "conv_norm_g", "conv_norm_b",
              "ffn_norm_g", "ple_norm_g", "final_norm_g")
WEIGHT_ORDER = ("mix_norm_g", "w_in", "b_gate", "attn_sinks", "w_attn_out", "ssm_lambda_re", "ssm_lambda_im",
                "ssm_log_dt", "ssm_b_re", "ssm_b_im", "ssm_c_re", "ssm_c_im", "ssm_d", "w_ssm_glu", "b_ssm_glu",
                "conv_dw_w", "conv_dw_b", "conv_norm_g", "conv_norm_b", "w_conv_out", "w_mix_out", "ffn_norm_g",
                "w_ffn_in", "w_ffn_out", "w_ple_in", "ple_norm_g", "w_ple_gate", "final_norm_g")


_ANY = pl.BlockSpec(memory_space=pl.ANY)


def _params(sem=None):
    return pltpu.CompilerParams(dimension_semantics=sem, vmem_limit_bytes=VMEM_LIMIT_BYTES)


def _pick(n, cands):
    for c in cands:
        if n % c == 0:
            return c
    return n


def _dot(a, b, dims):
    return lax.dot_general(a.astype(MXU_DTYPE), b.astype(MXU_DTYPE), (dims, ((), ())), preferred_element_type=F32)


def _dot_nn(a, b):
    return _dot(a, b, ((1,), (0,)))


def _dot_nt(a, b):
    return _dot(a, b, ((1,), (1,)))


def _dot_tn(a, b):
    return _dot(a, b, ((0,), (0,)))


@jax.custom_vjp
def _mm(x, w):
    return _dot_nn(x, w)


def _mm_f(x, w):
    return _dot_nn(x, w), (x, w)


def _mm_b(res, dy):
    x, w = res
    return _dot_nt(dy, w).astype(x.dtype), _dot_tn(x, dy).astype(w.dtype)


_mm.defvjp(_mm_f, _mm_b)


def _rms(x, g):
    return x * lax.rsqrt(jnp.mean(x * x, axis=-1, keepdims=True) + EPS) * g


ROW_TILES = (1024, 512, 256, 128)
COL_TILES = (1536, 1408, 1024, 512, 256, 128)


def _matmul_nn(name, a, b, out_dtype):
    t, k = a.shape
    n = b.shape[1]
    tm, tn = _pick(t, ROW_TILES), _pick(n, COL_TILES)

    def body(a_ref, b_ref, o_ref):
        o_ref[...] = _dot_nn(a_ref[...], b_ref[...]).astype(o_ref.dtype)

    return pl.pallas_call(
        body, name=name, grid=(t // tm, n // tn), out_shape=jax.ShapeDtypeStruct((t, n), out_dtype),
        in_specs=[pl.BlockSpec((tm, k), lambda i, j: (i, 0)), pl.BlockSpec((k, tn), lambda i, j: (0, j))],
        out_specs=pl.BlockSpec((tm, tn), lambda i, j: (i, j)),
        compiler_params=_params(("parallel", "parallel")))(a, b)


def _matmul_tn(name, a, b):
    t, m = a.shape
    n = b.shape[1]
    tm, tn, tt = _pick(m, COL_TILES[1:]), _pick(n, COL_TILES), _pick(t, ROW_TILES)

    def body(a_ref, b_ref, o_ref):
        @pl.when(pl.program_id(2) == 0)
        def _():
            o_ref[...] = jnp.zeros_like(o_ref)

        o_ref[...] += _dot_tn(a_ref[...], b_ref[...])

    return pl.pallas_call(
        body, name=name, grid=(m // tm, n // tn, t // tt), out_shape=jax.ShapeDtypeStruct((m, n), F32),
        in_specs=[pl.BlockSpec((tt, tm), lambda i, j, s: (s, i)), pl.BlockSpec((tt, tn), lambda i, j, s: (s, j))],
        out_specs=pl.BlockSpec((tm, tn), lambda i, j, s: (i, j)),
        compiler_params=_params(("parallel", "parallel", "arbitrary")))(a, b)


def _two_parts(n):
    cut = n // (2 * LANES) * LANES
    return [slice(0, n)] if cut == 0 else [slice(0, cut), slice(cut, n)]


def _in_proj(x, g, w):
    t, d = x.shape
    n = w.shape[1]
    tm, tn = _pick(t, ROW_TILES), _pick(n, COL_TILES)

    def body(x_ref, g_ref, w_ref, z_ref, h_ref):
        h = _rms(x_ref[...], g_ref[...]).astype(h_ref.dtype)

        @pl.when(pl.program_id(1) == 0)
        def _():
            h_ref[...] = h

        z_ref[...] = _dot_nn(h, w_ref[...])

    return pl.pallas_call(
        body, name="in_proj", grid=(t // tm, n // tn),
        out_shape=[jax.ShapeDtypeStruct((t, n), F32), jax.ShapeDtypeStruct((t, d), MXU_DTYPE)],
        in_specs=[pl.BlockSpec((tm, d), lambda i, j: (i, 0)), pl.BlockSpec((1, d), lambda i, j: (0, 0)),
                  pl.BlockSpec((d, tn), lambda i, j: (0, j))],
        out_specs=[pl.BlockSpec((tm, tn), lambda i, j: (i, j)), pl.BlockSpec((tm, d), lambda i, j: (i, 0))],
        compiler_params=_params(("parallel", "arbitrary")))(x, g, w)


def _in_proj_bwd(dz, w, x, dx_res, g):
    t, d = x.shape
    tm = _pick(t, ROW_TILES[1:])

    def body(dz_ref, w_ref, x_ref, r_ref, g_ref, dx_ref, dg_ref):
        _, vjp = jax.vjp(_norm_in_tile, x_ref[...], g_ref[...])
        dx, dg = vjp(_dot_nt(dz_ref[...], w_ref[...]))
        dx_ref[...] = dx + r_ref[...]

        @pl.when(pl.program_id(0) == 0)
        def _():
            dg_ref[...] = jnp.zeros_like(dg_ref)

        dg_ref[...] += dg

    rows = lambda width: pl.BlockSpec((tm, width), lambda i: (i, 0))
    whole = lambda a: pl.BlockSpec(a.shape, lambda i: (0, 0))
    return pl.pallas_call(
        body, name="in_proj_bwd", grid=(t // tm,), out_shape=[jax.ShapeDtypeStruct((t, d), F32), jax.ShapeDtypeStruct((1, d), F32)],
        in_specs=[rows(dz.shape[1]), whole(w), rows(d), rows(d), whole(g)], out_specs=[rows(d), whole(g)],
        compiler_params=_params(("arbitrary",)))(dz, w, x, dx_res, g)


def _ffn_in_act(hf, w_fi):
    t, k = hf.shape
    f = w_fi.shape[1] // 2
    tm, tf = _pick(t, ROW_TILES[1:]), _pick(f, COL_TILES)
    nf = f // tf

    def body(a_ref, wg_ref, wu_ref, g_ref, u_ref, act_ref):
        a = a_ref[...]
        for cols in _two_parts(tf):
            g, u = _dot_nn(a, wg_ref[:, cols]), _dot_nn(a, wu_ref[:, cols])
            g_ref[:, cols] = g.astype(g_ref.dtype)
            u_ref[:, cols] = u.astype(u_ref.dtype)
            act_ref[:, cols] = (jax.nn.silu(g) * u).astype(act_ref.dtype)

    out = pl.BlockSpec((tm, tf), lambda i, j: (i, j))
    return pl.pallas_call(
        body, name="ffn_in_act", grid=(t // tm, nf), out_shape=[jax.ShapeDtypeStruct((t, f), MXU_DTYPE)] * 3,
        in_specs=[pl.BlockSpec((tm, k), lambda i, j: (i, 0)), pl.BlockSpec((k, tf), lambda i, j: (0, j)),
                  pl.BlockSpec((k, tf), lambda i, j: (0, j + nf))],
        out_specs=[out, out, out], compiler_params=_params(("parallel", "parallel")))(hf, w_fi, w_fi)


def _ffn_mid_bwd(dffn, w_fo, gate, up):
    t, d = dffn.shape
    f = w_fo.shape[0]
    tm, tf = _pick(t, ROW_TILES[1:]), _pick(f, COL_TILES)

    def body(a_ref, w_ref, g_ref, u_ref, dg_ref, du_ref):
        a = a_ref[...]
        for cols in _two_parts(tf):
            dact = _dot_nt(a, w_ref[cols, :])
            g, u = g_ref[:, cols].astype(F32), u_ref[:, cols].astype(F32)
            sg = jax.nn.sigmoid(g)
            dg_ref[:, cols] = (dact * u * sg * (1.0 + g * (1.0 - sg))).astype(dg_ref.dtype)
            du_ref[:, cols] = (dact * g * sg).astype(du_ref.dtype)

    blk = pl.BlockSpec((tm, tf), lambda i, j: (i, j))
    return pl.pallas_call(
        body, name="ffn_mid_bwd", grid=(t // tm, f // tf), out_shape=[jax.ShapeDtypeStruct((t, f), MXU_DTYPE)] * 2,
        in_specs=[pl.BlockSpec((tm, d), lambda i, j: (i, 0)), pl.BlockSpec((tf, d), lambda i, j: (j, 0)), blk, blk],
        out_specs=[blk, blk], compiler_params=_params(("parallel", "parallel")))(dffn, w_fo, gate, up)


def _ffn_in_dx(dgate, dup, w_fi):
    t, f = dgate.shape
    d = w_fi.shape[0]
    tm = _pick(t, ROW_TILES[1:])

    def body(g_ref, u_ref, w_ref, o_ref):
        o_ref[...] = _dot_nt(g_ref[...], w_ref[:, :f]) + _dot_nt(u_ref[...], w_ref[:, f:])

    blk = pl.BlockSpec((tm, f), lambda i: (i, 0))
    return pl.pallas_call(
        body, name="ffn_in_dx", grid=(t // tm,), out_shape=jax.ShapeDtypeStruct((t, d), F32),
        in_specs=[blk, blk, pl.BlockSpec(w_fi.shape, lambda i: (0, 0))], out_specs=pl.BlockSpec((tm, d), lambda i: (i, 0)),
        compiler_params=_params(("parallel",)))(dgate, dup, w_fi)


def _token_call(name, fn, tile, tok_ins, consts, tok_outs, acc_outs, into=None):
    n_rows = tok_ins[0][0].shape[0]
    tile = min(tile, n_rows)
    n_ti, n_c = len(tok_ins), len(consts)
    n_in = n_ti + n_c + (into is not None)
    n_to = len(tok_outs) + (into is not None)

    def body(*refs):
        ins = [r[...] for r in refs[:n_ti + n_c]]
        outs, accs = fn(*ins)
        for r, v in zip(refs[n_in:n_in + n_to], outs, strict=True):
            r[...] = v.astype(r.dtype)
        first = pl.program_id(0) == 0
        for r, v in zip(refs[n_in + n_to:], accs, strict=True):
            @pl.when(first)
            def _(r=r):
                r[...] = jnp.zeros_like(r)

            r[...] += jnp.broadcast_to(v, r.shape).astype(F32)

    in_specs = [pl.BlockSpec((tile, w), functools.partial(lambda i, c: (i, c), c=cb)) for _, w, cb in tok_ins]
    in_specs += [pl.BlockSpec(c.shape, lambda i: (0, 0)) for c in consts]
    out_shape = [jax.ShapeDtypeStruct((n_rows, w), dt) for w, dt in tok_outs]
    out_specs = [pl.BlockSpec((tile, w), lambda i: (i, 0)) for w, _ in tok_outs]
    operands = [a for a, _, _ in tok_ins] + list(consts)
    aliases = {}
    if into is not None:
        target, width, col_block = into
        in_specs.append(_ANY)
        operands.append(target)
        out_shape.append(jax.ShapeDtypeStruct(target.shape, target.dtype))
        out_specs.append(pl.BlockSpec((tile, width), lambda i: (i, col_block)))
        aliases = {n_in - 1: n_to - 1}
    out_shape += [jax.ShapeDtypeStruct(s, F32) for s in acc_outs]
    out_specs += [pl.BlockSpec(s, lambda i: (0, 0)) for s in acc_outs]
    res = pl.pallas_call(
        body, name=name, grid=(n_rows // tile,), out_shape=out_shape, in_specs=in_specs, out_specs=out_specs,
        input_output_aliases=aliases, compiler_params=_params(("arbitrary",)))(*operands)
    return res[:n_to], res[n_to:]


def _whole(a):
    return (a, a.shape[1], 0)


def _norm_in_tile(x, g):
    return _rms(x, g)


def _conv_post_tile(v, g, b):
    mu = jnp.mean(v, axis=-1, keepdims=True)
    var = jnp.mean(jnp.square(v - mu), axis=-1, keepdims=True)
    return jax.nn.silu((v - mu) * lax.rsqrt(var + EPS) * g + b)


def _merge_tile(x, ya, ys, v, gin, w_ao, w_sg, b_sg, ln_g, ln_b, w_co, b_gate, w_mo, g_ffn):
    d = x.shape[1]
    y_attn = _mm(ya, w_ao)
    pre = _mm(jax.nn.gelu(ys), w_sg) + b_sg
    y_ssm = pre[:, :d] * jax.nn.sigmoid(pre[:, d:])
    y_conv = _mm(_conv_post_tile(v, ln_g, ln_b), w_co)
    gates = jax.nn.sigmoid(gin + b_gate)
    merged = gates[:, :d] * y_attn + gates[:, d:2 * d] * y_ssm + gates[:, 2 * d:] * y_conv
    x1 = x + _mm(merged, w_mo)
    return x1, _rms(x1, g_ffn)


def _ple_tile(x2, p, w_pi, g_ple, w_pg):
    return x2 + jax.nn.sigmoid(_mm(_rms(x2, g_ple), w_pg)) * _mm(p, w_pi)


def _f32s(vals):
    return [v.astype(F32) for v in vals]


def _rope_tables(positions, inv_lane):
    def fn(pos, inv):
        ang = pos.astype(F32) * inv
        j = lax.broadcasted_iota(jnp.int32, ang.shape, 1) % HEAD_DIM
        c = jnp.where(j < ROPE_DIM, jnp.cos(ang), 1.0)
        s = jnp.sin(ang)
        s = jnp.where(j < ROPE_DIM // 2, -s, jnp.where(j < ROPE_DIM, s, 0.0))
        return [c, s], []

    (c, s), _ = _token_call("rope_tables", fn, 1024, [_whole(positions.reshape(-1, 1))], [inv_lane],
                            [(LANES, F32), (LANES, F32)], [])
    return c, s


def _swap_halves(t):
    n = t.shape[1]
    j = lax.broadcasted_iota(jnp.int32, t.shape, 1) % HEAD_DIM
    lower = pltpu.roll(t, n - ROPE_DIM // 2, 1)
    upper = jnp.where(j < ROPE_DIM, pltpu.roll(t, ROPE_DIM // 2, 1), 0.0)
    return jnp.where(j < ROPE_DIM // 2, lower, upper)


def _rope(t, c, s):
    return t * c + _swap_halves(t) * s


def _rope_t(dt, c, s):
    return dt * c + _swap_halves(dt * s)


def _tile4(a):
    return jnp.concatenate([a] * (Q_WIDTH // LANES), axis=1)


def _attn_mask(n):
    qi = lax.broadcasted_iota(jnp.int32, (GQA_GROUP * BLOCK, 2 * BLOCK), 0) % BLOCK
    kj = lax.broadcasted_iota(jnp.int32, (GQA_GROUP * BLOCK, 2 * BLOCK), 1)
    dist = qi + BLOCK - kj
    return (dist >= 0) & (dist < BLOCK) & ((n > 0) | (kj >= BLOCK))


def _attn_specs(n_seq):
    own = lambda w, blk: pl.BlockSpec((n_seq, BLOCK, w), lambda n: (0, n, blk))
    prev = lambda w, blk: pl.BlockSpec((n_seq, BLOCK, w), lambda n: (0, jnp.maximum(n - 1, 0), blk))
    return [own(ZQ_W, ZQ_BLK), own(ZKV_W, ZKV_BLK), prev(ZKV_W, ZKV_BLK), own(LANES, 0), own(LANES, 0), prev(LANES, 0),
            prev(LANES, 0), pl.BlockSpec((1, N_Q_HEADS), lambda n: (0, 0))]


def _by_seq(a, n_seq):
    return a.reshape(n_seq, a.shape[0] // n_seq, a.shape[1])


ATTN_SCALE = HEAD_DIM ** -0.5


def _stack_heads(t, kh):
    return jnp.concatenate([t[:, (kh * GQA_GROUP + g) * HEAD_DIM:(kh * GQA_GROUP + g + 1) * HEAD_DIM]
                            for g in range(GQA_GROUP)], axis=0)


def _stack_sinks(sink, kh):
    return jnp.concatenate([jnp.broadcast_to(sink[:, kh * GQA_GROUP + g:kh * GQA_GROUP + g + 1], (BLOCK, 1))
                            for g in range(GQA_GROUP)], axis=0)


def _attn_band(b, q_ref, kv_ref, kvp_ref, c_ref, s_ref, cp_ref, sp_ref):
    c, s = c_ref[b], s_ref[b]
    q = _rope(q_ref[b], _tile4(c), _tile4(s)) * ATTN_SCALE
    kv, kvp = kv_ref[b], kvp_ref[b]
    k = _rope(kv[:, :KV_WIDTH], c, s)
    kp = _rope(kvp[:, :KV_WIDTH], cp_ref[b], sp_ref[b])
    kb = jnp.concatenate([kp, k], axis=0)
    vb = jnp.concatenate([kvp[:, KV_WIDTH:], kv[:, KV_WIDTH:]], axis=0)
    return q, kb, vb


def _attention_fwd(z, ctab, stab, sinks, n_seq):
    t = z.shape[0]
    seq = t // n_seq

    def body(q_ref, kv_ref, kvp_ref, c_ref, s_ref, cp_ref, sp_ref, sink_ref, o_ref, lse_ref):
        mask = _attn_mask(pl.program_id(0))
        sink = sink_ref[...]
        lane = lax.broadcasted_iota(jnp.int32, (BLOCK, N_Q_HEADS), 1)
        for b in range(n_seq):
            q, kb, vb = _attn_band(b, q_ref, kv_ref, kvp_ref, c_ref, s_ref, cp_ref, sp_ref)
            lse_all = jnp.zeros((BLOCK, N_Q_HEADS), F32)
            for kh in range(N_KV_HEADS):
                sc = jnp.where(mask, _dot_nt(_stack_heads(q, kh), kb[:, kh * HEAD_DIM:(kh + 1) * HEAD_DIM]), NEG_INF)
                sk = _stack_sinks(sink, kh)
                m = jnp.maximum(jnp.max(sc, axis=-1, keepdims=True), sk)
                pr = jnp.exp(sc - m)
                den = jnp.sum(pr, axis=-1, keepdims=True) + jnp.exp(sk - m)
                out = _dot_nn(pr * (1.0 / den), vb[:, kh * HEAD_DIM:(kh + 1) * HEAD_DIM])
                lse = m + jnp.log(den)
                for g in range(GQA_GROUP):
                    h = kh * GQA_GROUP + g
                    o_ref[b, :, h * HEAD_DIM:(h + 1) * HEAD_DIM] = out[g * BLOCK:(g + 1) * BLOCK].astype(o_ref.dtype)
                    lse_all = jnp.where(lane == h, lse[g * BLOCK:(g + 1) * BLOCK], lse_all)
            lse_ref[b] = lse_all

    rows = lambda w: pl.BlockSpec((n_seq, BLOCK, w), lambda n: (0, n, 0))
    z3, c3, s3 = _by_seq(z, n_seq), _by_seq(ctab, n_seq), _by_seq(stab, n_seq)
    ya, lse = pl.pallas_call(
        body, name="attn_fwd", grid=(seq // BLOCK,),
        out_shape=[jax.ShapeDtypeStruct((n_seq, seq, Q_WIDTH), MXU_DTYPE), jax.ShapeDtypeStruct((n_seq, seq, N_Q_HEADS), F32)],
        in_specs=_attn_specs(n_seq), out_specs=[rows(Q_WIDTH), rows(N_Q_HEADS)],
        compiler_params=_params(("parallel",)))(z3, z3, z3, c3, s3, c3, s3, sinks)
    return ya.reshape(t, Q_WIDTH), lse.reshape(t, N_Q_HEADS)


def _attention_bwd(z, ctab, stab, sinks, ya, lse, dya, dz, n_seq):
    t = z.shape[0]
    seq = t // n_seq

    def body(q_ref, kv_ref, kvp_ref, c_ref, s_ref, cp_ref, sp_ref, sink_ref, o_ref, lse_ref, do_ref, _,
             dq_ref, dkv_ref, dkvp_ref, dsink_ref):
        mask = _attn_mask(pl.program_id(0))
        sink = sink_ref[...]
        lane = lax.broadcasted_iota(jnp.int32, (1, N_Q_HEADS), 1)
        dsink = jnp.zeros((1, N_Q_HEADS), F32)
        for b in range(n_seq):
            q, kb, vb = _attn_band(b, q_ref, kv_ref, kvp_ref, c_ref, s_ref, cp_ref, sp_ref)
            lse_all = lse_ref[b]
            o = o_ref[b].astype(F32)
            do = do_ref[b].astype(F32)
            dq_parts = []
            dk_parts, dv_parts = [], []
            for kh in range(N_KV_HEADS):
                kbh = kb[:, kh * HEAD_DIM:(kh + 1) * HEAD_DIM]
                vbh = vb[:, kh * HEAD_DIM:(kh + 1) * HEAD_DIM]
                qs, dos = _stack_heads(q, kh), _stack_heads(do, kh)
                lse = jnp.concatenate([lse_all[:, kh * GQA_GROUP + g:kh * GQA_GROUP + g + 1] for g in range(GQA_GROUP)], axis=0)
                pr = jnp.exp(jnp.where(mask, _dot_nt(qs, kbh), NEG_INF) - lse)
                delta = jnp.sum(dos * _stack_heads(o, kh), axis=-1, keepdims=True)
                ds = pr * (_dot_nt(dos, vbh) - delta)
                dqs = _dot_nn(ds, kbh)
                dq_parts += [dqs[g * BLOCK:(g + 1) * BLOCK] for g in range(GQA_GROUP)]
                dk_parts.append(_dot_tn(ds, qs))
                dv_parts.append(_dot_tn(pr, dos))
                dsk = jnp.exp(_stack_sinks(sink, kh) - lse) * delta
                for g in range(GQA_GROUP):
                    dsink = dsink + jnp.where(lane == kh * GQA_GROUP + g, -jnp.sum(dsk[g * BLOCK:(g + 1) * BLOCK]), 0.0)
            c, s = c_ref[b], s_ref[b]
            dq_ref[b] = _rope_t(jnp.concatenate(dq_parts, axis=1) * ATTN_SCALE, _tile4(c), _tile4(s)).astype(dq_ref.dtype)
            dk = jnp.concatenate(dk_parts, axis=1)
            dv = jnp.concatenate(dv_parts, axis=1)
            dkv_ref[b, :, :KV_WIDTH] = _rope_t(dk[BLOCK:], c, s)
            dkv_ref[b, :, KV_WIDTH:] = dv[BLOCK:]
            dkvp_ref[b, :, :KV_WIDTH] = _rope_t(dk[:BLOCK], cp_ref[b], sp_ref[b])
            dkvp_ref[b, :, KV_WIDTH:] = dv[:BLOCK]

        @pl.when(pl.program_id(0) == 0)
        def _():
            dsink_ref[...] = jnp.zeros_like(dsink_ref)

        dsink_ref[...] += dsink

    rows = lambda w: pl.BlockSpec((n_seq, BLOCK, w), lambda n: (0, n, 0))
    by_seq = lambda a: _by_seq(a, n_seq)
    z3, c3, s3 = by_seq(z), by_seq(ctab), by_seq(stab)
    dz, dkv, dkvp, dsink = pl.pallas_call(
        body, name="attn_bwd", grid=(seq // BLOCK,),
        out_shape=[jax.ShapeDtypeStruct((n_seq, seq, Z_WIDTH), dz.dtype), jax.ShapeDtypeStruct((n_seq, seq, ZKV_W), F32),
                   jax.ShapeDtypeStruct((n_seq, seq, ZKV_W), F32), jax.ShapeDtypeStruct((1, N_Q_HEADS), F32)],
        in_specs=_attn_specs(n_seq) + [rows(Q_WIDTH), rows(N_Q_HEADS), rows(Q_WIDTH), _ANY],
        out_specs=[pl.BlockSpec((n_seq, BLOCK, ZQ_W), lambda n: (0, n, ZQ_BLK)), rows(ZKV_W), rows(ZKV_W),
                   pl.BlockSpec((1, N_Q_HEADS), lambda n: (0, 0))],
        input_output_aliases={11: 0},
        compiler_params=_params(("arbitrary",)))(z3, z3, z3, c3, s3, c3, s3, sinks, by_seq(ya), by_seq(lse), by_seq(dya), by_seq(dz))
    return dz.reshape(t, Z_WIDTH), dkv.reshape(t, ZKV_W), dkvp.reshape(t, ZKV_W), dsink


def _kv_combine(dkv, dkvp, dz, n_seq):
    t = dkv.shape[0]
    seq = t // n_seq
    rows = _pick(seq, (512, 256, 128))
    nt, per = seq // rows, rows // BLOCK
    n_blocks = t // BLOCK

    def body(dkv_ref, dkvp_ref, dkvn_ref, _, o_ref):
        nxt = jnp.where(pl.program_id(1) == nt - 1, 0.0, dkvn_ref[...])
        shifted = nxt if per == 1 else jnp.concatenate([dkvp_ref[BLOCK:, :], nxt], axis=0)
        o_ref[...] = (dkv_ref[...] + shifted).astype(o_ref.dtype)

    tile = pl.BlockSpec((rows, ZKV_W), lambda b, i: (b * nt + i, 0))
    return pl.pallas_call(
        body, name="kv_combine", grid=(n_seq, nt), out_shape=jax.ShapeDtypeStruct(dz.shape, dz.dtype),
        in_specs=[tile, tile,
                  pl.BlockSpec((BLOCK, ZKV_W), lambda b, i: (jnp.minimum((b * nt + i + 1) * per, n_blocks - 1), 0)), _ANY],
        out_specs=pl.BlockSpec((rows, ZKV_W), lambda b, i: (b * nt + i, ZKV_BLK)), input_output_aliases={3: 0},
        compiler_params=_params(("parallel", "parallel")))(dkv, dkvp, dkvp, dz)


def _ssm_coeff_tile(lam_re, lam_im, log_dt):
    lr = jnp.minimum(lam_re, -1e-4)
    dt = jnp.exp(log_dt)
    mag = jnp.exp(lr * dt)
    a_re = mag * jnp.cos(lam_im * dt)
    a_im = mag * jnp.sin(lam_im * dt)
    den = lr * lr + lam_im * lam_im
    x_re = a_re - 1.0
    f_re = (x_re * lr + a_im * lam_im) / den
    f_im = (a_im * lr - x_re * lam_im) / den
    return a_re, a_im, f_re, f_im


def _ssm_coeffs(lam_re, lam_im, log_dt):
    def body(lr_ref, li_ref, dt_ref, *o_refs):
        for r, v in zip(o_refs, _ssm_coeff_tile(lr_ref[...], li_ref[...], dt_ref[...]), strict=True):
            r[...] = v

    return pl.pallas_call(body, name="ssm_coeffs", out_shape=[jax.ShapeDtypeStruct(lam_re.shape, F32)] * 4)(
        lam_re, lam_im, log_dt)


def _ssm_coeffs_bwd(lam_re, lam_im, log_dt, cts):
    def body(lr_ref, li_ref, dt_ref, c0, c1, c2, c3, dlr_ref, dli_ref, ddt_ref):
        _, vjp = jax.vjp(_ssm_coeff_tile, lr_ref[...], li_ref[...], dt_ref[...])
        dlr, dli, ddt = vjp((c0[...], c1[...], c2[...], c3[...]))
        dlr_ref[...] = dlr
        dli_ref[...] = dli
        ddt_ref[...] = ddt

    return pl.pallas_call(
        body, name="ssm_coeffs_bwd",
        out_shape=[jax.ShapeDtypeStruct(lam_re.shape, F32)] * 2 + [jax.ShapeDtypeStruct(log_dt.shape, F32)])(
        lam_re, lam_im, log_dt, *cts)


def _ssm_chunk(t):
    return _pick(t, (256, 128))


def _ssm_fwd(z, bmat, a_row, f_row, cmat, d_row, n_seq):
    t = z.shape[0]
    seq = t // n_seq
    lc = _ssm_chunk(seq)
    nc = seq // lc
    n2 = 2 * SSM_LANES

    def body(u_ref, b_ref, a_ref, f_ref, c_ref, d_ref, y_ref, s_ref, bu_ref, st_ref):
        @pl.when(pl.program_id(1) == 0)
        def _():
            st_ref[...] = jnp.zeros_like(st_ref)

        u = u_ref[...]
        proj = _dot_nn(u, b_ref[...])
        fr, fi = f_ref[:, :SSM_LANES], f_ref[:, SSM_LANES:]
        pr, pi = proj[:, :SSM_LANES], proj[:, SSM_LANES:]
        bu_ref[:, :SSM_LANES] = fr * pr - fi * pi
        bu_ref[:, SSM_LANES:] = fr * pi + fi * pr
        ar, ai = a_ref[:, :SSM_LANES], a_ref[:, SSM_LANES:]

        def step(i, carry):
            sr, si = carry
            nr = ar * sr - ai * si + bu_ref[pl.ds(i, 1), pl.ds(0, SSM_LANES)]
            ni = ar * si + ai * sr + bu_ref[pl.ds(i, 1), pl.ds(SSM_LANES, SSM_LANES)]
            s_ref[pl.ds(i, 1), pl.ds(0, SSM_LANES)] = nr
            s_ref[pl.ds(i, 1), pl.ds(SSM_LANES, SSM_LANES)] = ni
            return nr, ni

        sr, si = lax.fori_loop(0, lc, step, (st_ref[0:1, :SSM_LANES], st_ref[0:1, SSM_LANES:]), unroll=8)
        st_ref[0:1, :SSM_LANES] = sr
        st_ref[0:1, SSM_LANES:] = si
        y_ref[...] = _dot_nn(s_ref[...], c_ref[...]) + d_ref[...] * u

    const = lambda shape: pl.BlockSpec(shape, lambda b, c: (0, 0))
    return pl.pallas_call(
        body, name="ssm_fwd", grid=(n_seq, nc),
        out_shape=[jax.ShapeDtypeStruct((t, SSM_WIDTH), F32), jax.ShapeDtypeStruct((t, n2), F32)],
        in_specs=[pl.BlockSpec((lc, ZS_W), lambda b, c: (b * nc + c, ZS_BLK)), const((SSM_WIDTH, n2)), const((1, n2)),
                  const((1, n2)), const((n2, SSM_WIDTH)), const((1, SSM_WIDTH))],
        out_specs=[pl.BlockSpec((lc, SSM_WIDTH), lambda b, c: (b * nc + c, 0)),
                   pl.BlockSpec((lc, n2), lambda b, c: (b * nc + c, 0))],
        scratch_shapes=[pltpu.VMEM((lc, n2), F32), pltpu.VMEM((8, n2), F32)],
        compiler_params=_params(("arbitrary", "arbitrary")))(z, bmat, a_row, f_row, cmat, d_row)


def _ssm_bwd(z, states, dy, bmat, a_row, f_row, cmat, d_row, dz, n_seq):
    t = z.shape[0]
    seq = t // n_seq
    lc = _ssm_chunk(seq)
    nc = seq // lc
    n2 = 2 * SSM_LANES

    def body(dy_ref, u_ref, s_ref, b_ref, a_ref, f_ref, c_ref, d_ref, _,
             du_ref, db_ref, dc_ref, da_ref, df_ref, dd_ref, g_ref, carry_ref):
        @pl.when((pl.program_id(0) == 0) & (pl.program_id(1) == 0))
        def _():
            for r in (db_ref, dc_ref, da_ref, df_ref, dd_ref):
                r[...] = jnp.zeros_like(r)

        @pl.when(pl.program_id(1) == 0)
        def _():
            carry_ref[...] = jnp.zeros_like(carry_ref)

        dy, u, st = dy_ref[...], u_ref[...], s_ref[...]
        g_ref[0:lc, :] = _dot_nt(dy, c_ref[...])
        g_ref[lc:lc + 8, :] = carry_ref[...]
        dc_ref[...] += _dot_tn(st, dy)
        dd_ref[...] += jnp.sum(dy * u, axis=0, keepdims=True)
        ar, ai = a_ref[:, :SSM_LANES], a_ref[:, SSM_LANES:]

        def step(i, carry):
            gr, gi = carry
            r = lc - 1 - i
            nr = g_ref[pl.ds(r, 1), pl.ds(0, SSM_LANES)] + ar * gr + ai * gi
            ni = g_ref[pl.ds(r, 1), pl.ds(SSM_LANES, SSM_LANES)] - ai * gr + ar * gi
            g_ref[pl.ds(r, 1), pl.ds(0, SSM_LANES)] = nr
            g_ref[pl.ds(r, 1), pl.ds(SSM_LANES, SSM_LANES)] = ni
            return nr, ni

        gr, gi = lax.fori_loop(0, lc, step, (carry_ref[0:1, :SSM_LANES], carry_ref[0:1, SSM_LANES:]), unroll=8)
        carry_ref[0:1, :SSM_LANES] = gr
        carry_ref[0:1, SSM_LANES:] = gi
        sr, si = st[:, :SSM_LANES], st[:, SSM_LANES:]
        gnr, gni = g_ref[pl.ds(1, lc), pl.ds(0, SSM_LANES)], g_ref[pl.ds(1, lc), pl.ds(SSM_LANES, SSM_LANES)]
        da_ref[:, :SSM_LANES] += jnp.sum(gnr * sr + gni * si, axis=0, keepdims=True)
        da_ref[:, SSM_LANES:] += jnp.sum(gni * sr - gnr * si, axis=0, keepdims=True)
        gr_all, gi_all = g_ref[0:lc, :SSM_LANES], g_ref[0:lc, SSM_LANES:]
        proj = _dot_nn(u, b_ref[...])
        pr, pi = proj[:, :SSM_LANES], proj[:, SSM_LANES:]
        df_ref[:, :SSM_LANES] += jnp.sum(gr_all * pr + gi_all * pi, axis=0, keepdims=True)
        df_ref[:, SSM_LANES:] += jnp.sum(gi_all * pr - gr_all * pi, axis=0, keepdims=True)
        fr, fi = f_ref[:, :SSM_LANES], f_ref[:, SSM_LANES:]
        dproj = jnp.concatenate([fr * gr_all + fi * gi_all, fr * gi_all - fi * gr_all], axis=1).astype(MXU_DTYPE)
        du_ref[...] = (_dot_nt(dproj, b_ref[...]) + d_ref[...] * dy).astype(du_ref.dtype)
        db_ref[...] += _dot_tn(u, dproj)

    const = lambda shape: pl.BlockSpec(shape, lambda b, c: (0, 0))
    rows = lambda w, cb: pl.BlockSpec((lc, w), functools.partial(lambda b, c, cb: (b * nc + nc - 1 - c, cb), cb=cb))
    return pl.pallas_call(
        body, name="ssm_bwd", grid=(n_seq, nc),
        out_shape=[jax.ShapeDtypeStruct(dz.shape, dz.dtype), jax.ShapeDtypeStruct((SSM_WIDTH, n2), F32),
                   jax.ShapeDtypeStruct((n2, SSM_WIDTH), F32), jax.ShapeDtypeStruct((1, n2), F32),
                   jax.ShapeDtypeStruct((1, n2), F32), jax.ShapeDtypeStruct((1, SSM_WIDTH), F32)],
        in_specs=[rows(SSM_WIDTH, 0), rows(ZS_W, ZS_BLK), rows(n2, 0), const((SSM_WIDTH, n2)), const((1, n2)),
                  const((1, n2)), const((n2, SSM_WIDTH)), const((1, SSM_WIDTH)), _ANY],
        out_specs=[rows(ZS_W, ZS_BLK), const((SSM_WIDTH, n2)), const((n2, SSM_WIDTH)), const((1, n2)), const((1, n2)),
                   const((1, SSM_WIDTH))],
        input_output_aliases={8: 0},
        scratch_shapes=[pltpu.VMEM((lc + 8, n2), F32), pltpu.VMEM((8, n2), F32)],
        compiler_params=_params(("arbitrary", "arbitrary")))(dy, z, states, bmat, a_row, f_row, cmat, d_row, dz)


def _conv_chunk(t):
    return _pick(t, (512, 256, 128))


def _glu(c):
    return c[:, :CONV_WIDTH] * jax.nn.sigmoid(c[:, CONV_WIDTH:])


def _conv_specs(lc, nc):
    per = lc // CONV_HALO
    return [pl.BlockSpec((lc, ZC_W), lambda b, c: (b * nc + c, ZC_BLK)),
            pl.BlockSpec((CONV_HALO, ZC_W), lambda b, c: (jnp.maximum((b * nc + c) * per - 1, 0), ZC_BLK))]


def _conv_fill(c_ref, cp_ref, ue_ref, lc):
    ue_ref[0:CONV_HALO, :] = jnp.where(pl.program_id(1) > 0, _glu(cp_ref[...]), 0.0)
    ue_ref[CONV_HALO:CONV_HALO + lc, :] = _glu(c_ref[...])


def _conv_apply(ue_ref, w_ref, b_ref, lc):
    acc = jnp.zeros((lc, CONV_WIDTH), F32) + b_ref[...]
    for k in range(CONV_K):
        acc = acc + w_ref[k:k + 1, :] * ue_ref[pl.ds(k + CONV_HALO - CONV_K + 1, lc), :]
    return acc


def _conv_fwd(z, dw_w, dw_b, n_seq):
    t = z.shape[0]
    seq = t // n_seq
    lc = _conv_chunk(seq)
    nc = seq // lc

    def body(c_ref, cp_ref, w_ref, b_ref, o_ref, ue_ref):
        _conv_fill(c_ref, cp_ref, ue_ref, lc)
        o_ref[...] = _conv_apply(ue_ref, w_ref, b_ref, lc)

    const = lambda a: pl.BlockSpec(a.shape, lambda b, c: (0, 0))
    return pl.pallas_call(
        body, name="conv_fwd", grid=(n_seq, nc), out_shape=jax.ShapeDtypeStruct((t, CONV_WIDTH), F32),
        in_specs=_conv_specs(lc, nc) + [const(dw_w), const(dw_b)],
        out_specs=pl.BlockSpec((lc, CONV_WIDTH), lambda b, c: (b * nc + c, 0)),
        scratch_shapes=[pltpu.VMEM((CONV_HALO + lc, CONV_WIDTH), F32)],
        compiler_params=_params(("parallel", "parallel")))(z, z, dw_w, dw_b)


def _conv_bwd_taps(z, dv, dw_w, dz, n_seq):
    t = z.shape[0]
    seq = t // n_seq
    lc = _conv_chunk(seq)
    nc = seq // lc
    per = lc // CONV_HALO
    n_halo = t // CONV_HALO

    def body(c_ref, cp_ref, dv_ref, dvn_ref, w_ref, _, dc_ref, dw_ref, ue_ref, dve_ref):
        @pl.when((pl.program_id(0) == 0) & (pl.program_id(1) == 0))
        def _():
            dw_ref[...] = jnp.zeros_like(dw_ref)

        _conv_fill(c_ref, cp_ref, ue_ref, lc)
        dv = dv_ref[...]
        dve_ref[0:lc, :] = dv
        dve_ref[lc:lc + CONV_HALO, :] = jnp.where(pl.program_id(1) < nc - 1, dvn_ref[...], 0.0)
        dw_ref[CONV_K:CONV_K + 1, :] += jnp.sum(dv, axis=0, keepdims=True)
        du = jnp.zeros((lc, CONV_WIDTH), F32)
        for k in range(CONV_K):
            du = du + w_ref[k:k + 1, :] * dve_ref[pl.ds(CONV_K - 1 - k, lc), :]
            dw_ref[k:k + 1, :] += jnp.sum(dv * ue_ref[pl.ds(k + CONV_HALO - CONV_K + 1, lc), :], axis=0, keepdims=True)
        c = c_ref[...]
        a, sg = c[:, :CONV_WIDTH], jax.nn.sigmoid(c[:, CONV_WIDTH:])
        dc_ref[:, :CONV_WIDTH] = (du * sg).astype(dc_ref.dtype)
        dc_ref[:, CONV_WIDTH:] = (du * a * sg * (1.0 - sg)).astype(dc_ref.dtype)

    return pl.pallas_call(
        body, name="conv_bwd_taps", grid=(n_seq, nc),
        out_shape=[jax.ShapeDtypeStruct(dz.shape, dz.dtype), jax.ShapeDtypeStruct((CONV_HALO, CONV_WIDTH), F32)],
        in_specs=_conv_specs(lc, nc) + [
            pl.BlockSpec((lc, CONV_WIDTH), lambda b, c: (b * nc + c, 0)),
            pl.BlockSpec((CONV_HALO, CONV_WIDTH), lambda b, c: (jnp.minimum((b * nc + c + 1) * per, n_halo - 1), 0)),
            pl.BlockSpec(dw_w.shape, lambda b, c: (0, 0)), _ANY],
        out_specs=[pl.BlockSpec((lc, ZC_W), lambda b, c: (b * nc + c, ZC_BLK)),
                   pl.BlockSpec((CONV_HALO, CONV_WIDTH), lambda b, c: (0, 0))],
        input_output_aliases={5: 0},
        scratch_shapes=[pltpu.VMEM((CONV_HALO + lc, CONV_WIDTH), F32), pltpu.VMEM((lc + CONV_HALO, CONV_WIDTH), F32)],
        compiler_params=_params(("arbitrary", "arbitrary")))(z, z, dv, dv, dw_w, dz)


def _row(v):
    return v.reshape(1, -1)


def _ssm_mats(b_re, b_im, c_re, c_im):
    eye = jnp.eye(SSM_GROUPS, dtype=bool)
    bm = jnp.stack([b_re, b_im]).transpose(1, 3, 0, 2)[:, :, :, None, :]
    bmat = jnp.where(eye[:, None, None, :, None], bm, 0.0).reshape(SSM_WIDTH, 2 * SSM_LANES)
    cm = jnp.stack([c_re, -c_im]).transpose(0, 1, 3, 2)[:, :, :, None, :]
    cmat = jnp.where(eye[None, :, None, :, None], cm, 0.0).reshape(2 * SSM_LANES, SSM_WIDTH)
    return bmat.astype(MXU_DTYPE), cmat.astype(MXU_DTYPE)


def _ssm_mats_t(dbmat, dcmat):
    eye = jnp.eye(SSM_GROUPS, dtype=bool)
    db = dbmat.reshape(SSM_GROUPS, SSM_GROUP, 2, SSM_GROUPS, SSM_STATE)
    db = jnp.sum(jnp.where(eye[:, None, None, :, None], db, 0.0), axis=3).transpose(2, 0, 3, 1)
    dc = dcmat.reshape(2, SSM_GROUPS, SSM_STATE, SSM_GROUPS, SSM_GROUP)
    dc = jnp.sum(jnp.where(eye[None, :, None, :, None], dc, 0.0), axis=3).transpose(0, 1, 3, 2)
    return db[0], db[1], dc[0], -dc[1]


def _layer_fwd(x, p, w, get_ffn_weights, sp, ctab, stab, n_seq):
    z, h = _in_proj(x, sp["mix_norm_g"], w["w_in"])
    ya, lse = _attention_fwd(z, ctab, stab, sp["attn_sinks"], n_seq)
    ys, states = _ssm_fwd(z, sp["bmat"], sp["a_row"], sp["f_row"], sp["cmat"], sp["ssm_d"], n_seq)
    v = _conv_fwd(z, w["conv_dw_w"], sp["conv_dw_b"], n_seq)
    merge_consts = [w["w_attn_out"], w["w_ssm_glu"], sp["b_ssm_glu"], sp["conv_norm_g"], sp["conv_norm_b"], w["w_conv_out"],
                    sp["b_gate"], w["w_mix_out"], sp["ffn_norm_g"]]
    (x1, hf), _ = _token_call("merge", lambda *a: (list(_merge_tile(*_f32s(a))), []), 512,
                              [_whole(x), _whole(ya), _whole(ys), _whole(v), (z, ZG_W, 0)], merge_consts,
                              [(x.shape[1], F32), (x.shape[1], MXU_DTYPE)], [])
    w = dict(w, **get_ffn_weights(x1))
    gate, up, act = _ffn_in_act(hf, w["w_ffn_in"])
    ffn = _matmul_nn("mm_ffn_out", act, w["w_ffn_out"], F32)

    def ple_fn(x1, ffn, p, w_pi, g_ple, w_pg):
        x2 = x1 + ffn
        return [x2, _ple_tile(x2, p, w_pi.astype(F32), g_ple, w_pg.astype(F32))], []

    (x2, x3), _ = _token_call("ple", ple_fn, 512, [_whole(x1), _whole(ffn), _whole(p)],
                              [w["w_ple_in"], sp["ple_norm_g"], w["w_ple_gate"]], [(x.shape[1], F32)] * 2, [])
    saved = dict(x=x, h=h, z=z, ya=ya, lse=lse, ys=ys, states=states, v=v, hf=hf, gate=gate, up=up, act=act, x2=x2, p=p)
    return x3, saved, w


def _layer_bwd(dx3, sv, w, sp, on_grads, ctab, stab, n_seq):
    d = dx3.shape[1]
    gw, gs = {}, {}

    def ple_bwd(x2, p, dx3, w_pi, g_ple, w_pg):
        _, vjp = jax.vjp(lambda x2, w_pi, g_ple, w_pg: _ple_tile(x2, p, w_pi, g_ple, w_pg), x2, w_pi.astype(F32), g_ple,
                         w_pg.astype(F32))
        dx2, dw_pi, dg_ple, dw_pg = vjp(dx3)
        return [dx2, dx2], [dw_pi, dg_ple, dw_pg]

    (dx2, dffn), (gw["w_ple_in"], gs["ple_norm_g"], gw["w_ple_gate"]) = _token_call(
        "ple_bwd", ple_bwd, 512, [_whole(sv["x2"]), _whole(sv["p"]), _whole(dx3)],
        [w["w_ple_in"], sp["ple_norm_g"], w["w_ple_gate"]], [(d, F32), (d, MXU_DTYPE)],
        [w["w_ple_in"].shape, (1, d), w["w_ple_gate"].shape])

    dgate, dup = _ffn_mid_bwd(dffn, w["w_ffn_out"], sv["gate"], sv["up"])
    gw["w_ffn_out"] = _matmul_tn("mm_ffn_out_dw", sv["act"], dffn)
    dhf = _ffn_in_dx(dgate, dup, w["w_ffn_in"])
    gw["w_ffn_in"] = jnp.concatenate([_matmul_tn("mm_ffn_gate_dw", sv["hf"], dgate), _matmul_tn("mm_ffn_up_dw", sv["hf"], dup)],
                                     axis=1)

    token = on_grads("ffn", gw)

    def merge_bwd(x, ya, ys, v, gin, dx1, dhf, *consts):
        consts = _f32s(consts)
        _, vjp = jax.vjp(_merge_tile, *_f32s((x, ya, ys, v, gin)), *consts)
        g = vjp((dx1, dhf))
        return list(g[:5]), list(g[5:])

    b_gate = sp["b_gate"] if token is None else sp["b_gate"] + token[0:1, 0:1]
    merge_consts = [w["w_attn_out"], w["w_ssm_glu"], sp["b_ssm_glu"], sp["conv_norm_g"], sp["conv_norm_b"], w["w_conv_out"],
                    b_gate, w["w_mix_out"], sp["ffn_norm_g"]]
    dz = lax.empty(sv["z"].shape, MXU_DTYPE)
    (dx_res, dya, dys, dv, dz), macc = _token_call(
        "merge_bwd", merge_bwd, 256,
        [_whole(sv["x"]), _whole(sv["ya"]), _whole(sv["ys"]), _whole(sv["v"]), (sv["z"], ZG_W, 0), _whole(dx2), _whole(dhf)],
        merge_consts, [(d, F32), (Q_WIDTH, MXU_DTYPE), (SSM_WIDTH, F32), (CONV_WIDTH, F32)],
        [c.shape for c in merge_consts], into=(dz, ZG_W, 0))
    (gw["w_attn_out"], gw["w_ssm_glu"], gs["b_ssm_glu"], gs["conv_norm_g"], gs["conv_norm_b"], gw["w_conv_out"], gs["b_gate"],
     gw["w_mix_out"], gs["ffn_norm_g"]) = macc

    dz, dw_taps = _conv_bwd_taps(sv["z"], dv, w["conv_dw_w"], dz, n_seq)
    gw["conv_dw_w"], gs["conv_dw_b"] = dw_taps[:CONV_K], dw_taps[CONV_K:]

    dz, gs["bmat"], gs["cmat"], gs["a_row"], gs["f_row"], gs["ssm_d"] = _ssm_bwd(
        sv["z"], sv["states"], dys, sp["bmat"], sp["a_row"], sp["f_row"], sp["cmat"], sp["ssm_d"], dz, n_seq)

    dz, dkv, dkvp, gs["attn_sinks"] = _attention_bwd(sv["z"], ctab, stab, sp["attn_sinks"], sv["ya"], sv["lse"], dya, dz, n_seq)
    dz = _kv_combine(dkv, dkvp, dz, n_seq)
    gw["w_in"] = _matmul_tn("mm_in_dw", sv["h"], dz)
    token = on_grads("mix", gw)
    g_in = sp["mix_norm_g"] if token is None else sp["mix_norm_g"] + token[0:1, 0:1]
    dx, gs["mix_norm_g"] = _in_proj_bwd(dz, w["w_in"], sv["x"], dx_res, g_in)
    return dx, gw, gs


def _loss_and_grad(x, target, g):
    def fn(x, tgt, g):
        def f(x, g):
            err = _rms(x, g) - tgt
            return 0.5 * jnp.mean(err * err, axis=-1, keepdims=True)

        per_token, vjp = jax.vjp(f, x, g)
        dx, dg = vjp(jnp.ones_like(per_token))
        return [dx], [jnp.sum(per_token, axis=0, keepdims=True), dg]

    (dx,), (loss, dg) = _token_call("loss", fn, 512, [_whole(x), _whole(target)], [g], [(x.shape[1], F32)],
                                    [(8, LANES), (1, x.shape[1])])
    return loss[0, 0], dx, dg


def _mesh_place():
    return lax.axis_index("x"), lax.axis_index("y"), lax.axis_index("c")


def _flip(v, bit):
    return 1 - v if bit else v


_MESH = pl.DeviceIdType.MESH


def _all_gather(name, xs):
    n = len(xs)

    def body(*refs):
        x_refs, out_refs = refs[:n], refs[n:2 * n]
        send_sems, recv_sems, local_sems = refs[2 * n:]
        mx, my, mc = _mesh_place()
        me, sibling = (mx, my, mc), (mx, my, 1 - mc)
        chips = [(1 - mx, my), (mx, 1 - my), (1 - mx, 1 - my)]

        def slot(a, px, py, pc):
            return out_refs[a].at[4 * px + 2 * py + pc]

        def copy(a, k, block, to, src=None):
            return pltpu.make_async_remote_copy(
                src_ref=slot(a, *block) if src is None else src, dst_ref=slot(a, *block), send_sem=send_sems.at[7 * a + k],
                recv_sem=recv_sems.at[7 * a + k], device_id=to, device_id_type=_MESH)

        mine = [pltpu.make_async_copy(x_refs[a], slot(a, *me), local_sems.at[a]) for a in range(n)]
        for cp in mine:
            cp.start()
        first = [copy(a, 0, me, sibling, src=x_refs[a]) for a in range(n)]
        first += [copy(a, 1 + j, me, (*chip, mc), src=x_refs[a]) for j, chip in enumerate(chips) for a in range(n)]
        for cp in first:
            cp.start()
        passed = []
        for j, chip in enumerate(chips):
            for a in range(n):
                copy(a, 1 + j, (*chip, mc), me).wait_recv()
                passed.append(copy(a, 4 + j, (*chip, mc), sibling))
                passed[-1].start()
        for a in range(n):
            copy(a, 0, sibling, me).wait_recv()
            for j, chip in enumerate(chips):
                copy(a, 4 + j, (*chip, 1 - mc), me).wait_recv()
        for cp in first + passed:
            cp.wait_send()
        for cp in mine:
            cp.wait()

    return pl.pallas_call(
        body, name=name, out_shape=[jax.ShapeDtypeStruct((N_DEV,) + x.shape, x.dtype) for x in xs], in_specs=[_ANY] * n,
        out_specs=[_ANY] * n,
        scratch_shapes=[pltpu.SemaphoreType.DMA((7 * n,)), pltpu.SemaphoreType.DMA((7 * n,)), pltpu.SemaphoreType.DMA((n,))])(*xs)


def _direct_copies(kind, src_refs, land_refs, send_sems, recv_sems, local_sems):
    mx, my, mc = _mesh_place()
    me = 4 * mx + 2 * my + mc
    n = len(src_refs)
    own = [pltpu.make_async_copy(src_refs[a] if kind == "gather" else src_refs[a].at[me], land_refs[a].at[me], local_sems.at[a])
           for a in range(n)]
    copies = []
    for rel in range(1, N_DEV):
        px, py, pc = _flip(mx, rel & 4), _flip(my, rel & 2), _flip(mc, rel & 1)
        for a in range(n):
            src = src_refs[a] if kind == "gather" else src_refs[a].at[4 * px + 2 * py + pc]
            copies.append(pltpu.make_async_remote_copy(
                src_ref=src, dst_ref=land_refs[a].at[me], send_sem=send_sems.at[7 * a + rel - 1],
                recv_sem=recv_sems.at[7 * a + rel - 1], device_id=(px, py, pc), device_id_type=_MESH))
    return copies, own


_HBM = pl.BlockSpec(memory_space=pltpu.HBM)
_SEM = pl.BlockSpec(memory_space=pltpu.SEMAPHORE)
_DATAFLOW = pltpu.SideEffectType.DATAFLOW_SIDE_EFFECTING


def _exchange_start(name, kind, groups):
    sizes = [len(g) for g in groups]
    srcs = [s for g in groups for s in g]
    lands = [lax.empty(((N_DEV,) + s.shape) if kind == "gather" else s.shape, s.dtype) for s in srcs]
    n, n_g = len(srcs), len(groups)
    first = [sum(sizes[:g]) for g in range(n_g)]

    def body(*refs):
        src_refs, land_refs, sems = refs[:n], refs[n:2 * n], refs[2 * n:2 * n + 3 * n_g]
        for g in range(n_g):
            span = slice(first[g], first[g] + sizes[g])
            copies, own = _direct_copies(kind, src_refs[span], land_refs[span], *sems[3 * g:3 * g + 3])
            for cp in own + copies:
                cp.start()
        refs[-1][...] = jnp.zeros_like(refs[-1])

    hbm = lambda a: pltpu.with_memory_space_constraint(a, pltpu.HBM)
    sem_shapes = [pltpu.SemaphoreType.DMA((k * m,)) for m in sizes for k in (7, 7, 1)]
    out = pl.pallas_call(
        body, name=name,
        out_shape=sem_shapes + [pltpu.HBM(a.shape, a.dtype) for a in srcs + lands] + [jax.ShapeDtypeStruct((8, LANES), F32)],
        in_specs=[_HBM] * (2 * n), out_specs=[_SEM] * (3 * n_g) + [_HBM] * (2 * n) + [pl.BlockSpec(memory_space=pltpu.VMEM)],
        input_output_aliases={i: 3 * n_g + i for i in range(2 * n)},
        compiler_params=pltpu.CompilerParams(has_side_effects=_DATAFLOW))(*[hbm(a) for a in srcs + lands])
    sems, arrays = out[:3 * n_g], out[3 * n_g:-1]
    started = [(kind, (*sems[3 * g:3 * g + 3], *arrays[first[g]:first[g] + sizes[g]],
                       *arrays[n + first[g]:n + first[g] + sizes[g]])) for g in range(n_g)]
    return started, out[-1]


def _exchange_wait(name, started, after):
    kind, (send_sems, recv_sems, local_sems, *arrays) = started
    n = len(arrays) // 2

    def body(*refs):
        src_refs, land_refs = refs[:n], refs[n:2 * n]
        copies, own = _direct_copies(kind, src_refs, land_refs, *refs[2 * n:2 * n + 3])
        for cp in copies + own:
            cp.wait()

    out = pl.pallas_call(
        body, name=name, out_shape=[pltpu.HBM(a.shape, a.dtype) for a in arrays],
        in_specs=[_HBM] * (2 * n) + [_SEM] * 3 + [_ANY], out_specs=[_HBM] * (2 * n),
        input_output_aliases={i: i for i in range(2 * n)},
        compiler_params=pltpu.CompilerParams(has_side_effects=_DATAFLOW))(*arrays, send_sems, recv_sems, local_sems, after)
    return out[n:]


def _adamw_math(g, w, m, v):
    m2 = ADAM_B1 * m + (1.0 - ADAM_B1) * g
    v2 = ADAM_B2 * v + (1.0 - ADAM_B2) * jnp.square(g)
    m_hat = m2 / (1.0 - ADAM_B1 ** ADAM_STEP)
    v_hat = v2 / (1.0 - ADAM_B2 ** ADAM_STEP)
    return g, -ADAM_LR * (m_hat / (jnp.sqrt(v_hat) + ADAM_EPS) + ADAM_WD * w), m2, v2


def _sum_blocks(ref):
    g = ref[0].astype(F32)
    for j in range(1, N_DEV):
        g = g + ref[j].astype(F32)
    return g


def _adamw_flat(name, parts, w, m, v):
    r = w.shape[0]
    tile = _pick(r, (1024, 512, 256, 128, 8))

    def body(p_ref, w_ref, m_ref, v_ref, *o_refs):
        for o, val in zip(o_refs, _adamw_math(_sum_blocks(p_ref), w_ref[...], m_ref[...], v_ref[...]), strict=True):
            o[...] = val

    flat = pl.BlockSpec((tile, LANES), lambda i: (i, 0))
    return pl.pallas_call(
        body, name=name, grid=(r // tile,), out_shape=[jax.ShapeDtypeStruct((r, LANES), F32)] * 4,
        in_specs=[pl.BlockSpec((N_DEV, tile, LANES), lambda i: (0, i, 0)), flat, flat, flat], out_specs=[flat] * 4,
        compiler_params=_params(("parallel",)))(parts, w, m, v)


def _adamw_cols(name, landed, base, stride, w, m, v):
    depth, rows, cs = w.shape
    n_slab = -(-cs // LANES)
    tr = _pick(rows, (SLAB_TILE,))

    def body(*refs):
        slabs, (w_ref, m_ref, v_ref), o_refs = refs[:n_slab], refs[n_slab:n_slab + 3], refs[n_slab + 3:]
        g = jnp.concatenate([_sum_blocks(s)[:, :min(LANES, cs - LANES * k)] for k, s in enumerate(slabs)], axis=1)
        for o, val in zip(o_refs, _adamw_math(g, w_ref[...], m_ref[...], v_ref[...]), strict=True):
            o[...] = val

    slab = lambda k: pl.BlockSpec((N_DEV, tr, LANES), lambda l, i: (0, (l * stride + base + k * rows) // tr + i, 0))
    nat = pl.BlockSpec((None, tr, cs), lambda l, i: (l, i, 0))
    return pl.pallas_call(
        body, name=name, grid=(depth, rows // tr), out_shape=[jax.ShapeDtypeStruct(w.shape, F32)] * 4,
        in_specs=[slab(k) for k in range(n_slab)] + [nat] * 3, out_specs=[nat] * 4,
        compiler_params=_params(("parallel", "parallel")))(*[landed] * n_slab, w, m, v)


def _adamw_rows(name, landed, base, stride, w, m, v):
    depth, rs, width = w.shape
    tr = math.gcd(rs, base, stride)

    def body(p_ref, w_ref, m_ref, v_ref, *o_refs):
        for o, val in zip(o_refs, _adamw_math(_sum_blocks(p_ref), w_ref[...], m_ref[...], v_ref[...]), strict=True):
            o[...] = val

    nat = pl.BlockSpec((None, tr, width), lambda l, i: (l, i, 0))
    return pl.pallas_call(
        body, name=name, grid=(depth, rs // tr), out_shape=[jax.ShapeDtypeStruct(w.shape, F32)] * 4,
        in_specs=[pl.BlockSpec((N_DEV, tr, width), lambda l, i: (0, (l * stride + base) // tr + i, 0)), nat, nat, nat],
        out_specs=[nat] * 4, compiler_params=_params(("parallel", "parallel")))(landed, w, m, v)


def _adamw_conv(landed, base, stride, w, m, v):
    depth, taps, cs = w.shape

    def body(p_ref, w_ref, m_ref, v_ref, *o_refs):
        g = _sum_blocks(p_ref)[:taps, :cs]
        for o, val in zip(o_refs, _adamw_math(g, w_ref[...], m_ref[...], v_ref[...]), strict=True):
            o[...] = val

    nat = pl.BlockSpec((None, taps, cs), lambda l: (l, 0, 0))
    return pl.pallas_call(
        body, name="adamw_conv", grid=(depth,), out_shape=[jax.ShapeDtypeStruct(w.shape, F32)] * 4,
        in_specs=[pl.BlockSpec((N_DEV, CONV_HALO, LANES), lambda l: (0, (l * stride + base) // CONV_HALO, 0)), nat, nat, nat],
        out_specs=[nat] * 4, compiler_params=_params(("parallel",)))(landed, w, m, v)


def _unshard_cols(name, gathered, start, rows, cs, shift=0):
    n_slab = -(-cs // LANES)
    total = N_DEV * cs
    tr = _pick(rows, (SLAB_TILE,))

    def body(*refs):
        slabs, o_ref = refs[:n_slab], refs[n_slab]
        for j in range(N_DEV):
            for k, s in enumerate(slabs):
                for src, dst, width in _wrapped(j * cs + LANES * k - shift, min(LANES, cs - LANES * k), total):
                    o_ref[:, dst:dst + width] = s[j, :, src:src + width]

    slab = lambda k: pl.BlockSpec((N_DEV, tr, LANES), lambda i: (0, (start + k * rows) // tr + i, 0))
    return pl.pallas_call(
        body, name=name, grid=(rows // tr,), out_shape=jax.ShapeDtypeStruct((rows, total), gathered.dtype),
        in_specs=[slab(k) for k in range(n_slab)], out_specs=pl.BlockSpec((tr, total), lambda i: (i, 0)),
        compiler_params=_params(("parallel",)))(*[gathered] * n_slab)


def _shard_cols(name, full, cs, shift=0):
    rows, total = full.shape
    n_slab = -(-cs // LANES)
    tr = _pick(rows, (SLAB_TILE,))

    def body(f_ref, o_ref):
        for j in range(N_DEV):
            for k in range(n_slab):
                used = min(LANES, cs - LANES * k)
                for src, dst, width in _wrapped(j * cs + LANES * k - shift, used, total):
                    o_ref[j, k, :, src:src + width] = f_ref[:, dst:dst + width].astype(o_ref.dtype)
                if used < LANES:
                    o_ref[j, k, :, used:] = jnp.zeros((tr, LANES - used), o_ref.dtype)

    out = pl.pallas_call(
        body, name=name, grid=(rows // tr,), out_shape=jax.ShapeDtypeStruct((N_DEV, n_slab, rows, LANES), BF16),
        in_specs=[pl.BlockSpec((tr, total), lambda i: (i, 0))],
        out_specs=pl.BlockSpec((N_DEV, n_slab, tr, LANES), lambda i: (0, 0, i, 0)),
        compiler_params=_params(("parallel",)))(full)
    return out.reshape(N_DEV, n_slab * rows, LANES)


def _wrapped(pos, width, total):
    pos %= total
    if pos + width <= total:
        return [(0, pos, width)]
    head = total - pos
    return [(0, pos, head), (head, 0, width - head)]


CONV_W_PIECES = 3


def _pad_to(n, align):
    return -(-n // align) * align


def _layout(group):
    col_names, row_names, with_conv = GROUPS[group]
    dims = {name: (rows, cols) for name, rows, cols, _ in SHARDED}
    col, off = {}, 0
    for name in col_names:
        rows, cols = dims[name]
        cs = cols // N_DEV
        col[name] = (off, rows, cs)
        off += -(-cs // LANES) * rows
    conv_base = off
    col_rows = _pad_to(off + with_conv * CONV_W_PIECES * CONV_HALO, SLAB_TILE)
    row, off = {}, 0
    for name in row_names:
        rs = dims[name][0] // N_DEV
        row[name] = (off, rs)
        off += _pad_to(rs, LANES)
    return col, conv_base, col_rows, row, off


def _slabs(shard, fill):
    rows, cs = shard.shape
    parts = []
    for k in range(-(-cs // LANES)):
        part = shard[:, LANES * k:min(LANES * (k + 1), cs)]
        parts.append(jnp.pad(part, ((0, 0), (0, LANES - part.shape[1])), constant_values=fill))
    return jnp.concatenate(parts, axis=0)


def _concat_padded(pieces, total, axis):
    used = sum(p.shape[axis] for p in pieces)
    if total > used:
        shape = list(pieces[0].shape)
        shape[axis] = total - used
        pieces = pieces + [jnp.zeros(shape, pieces[0].dtype)]
    return jnp.concatenate(pieces, axis=axis)


def _split3(a):
    hi = a.astype(BF16)
    r1 = a - hi.astype(F32)
    mid = r1.astype(BF16)
    return hi, mid, (r1 - mid.astype(F32)).astype(BF16)


def _pack_small(arrs, lead=()):
    flat = jnp.concatenate([a.reshape(lead + (-1,)) for a in arrs], axis=-1)
    total = _pad_to(flat.shape[-1], 512 * LANES)
    flat = jnp.pad(flat, [(0, 0)] * len(lead) + [(0, total - flat.shape[-1])])
    return flat.reshape(lead + (total // LANES, LANES))


def _unpack_small(flat, shapes):
    flat = flat.reshape(-1)
    res, off = [], 0
    for s in shapes:
        n = int(np.prod(s))
        res.append(flat[off:off + n].reshape(s))
        off += n
    return res


def _small_rows(a, depth):
    n16 = depth * SSM_GROUPS
    a_re, a_im, f_re, f_im = _ssm_coeffs(a["ssm_lambda_re"].reshape(n16, SSM_STATE), a["ssm_lambda_im"].reshape(n16, SSM_STATE),
                                         a["ssm_log_dt"].reshape(n16, 1))
    rows = []
    for l in range(depth):
        sp = {k: _row(a[k][l]) for k in ("mix_norm_g", "b_gate", "attn_sinks", "ssm_d", "b_ssm_glu", "conv_dw_b",
                                         "conv_norm_g", "conv_norm_b", "ffn_norm_g", "ple_norm_g")}
        g = slice(l * SSM_GROUPS, (l + 1) * SSM_GROUPS)
        sp["a_row"] = jnp.concatenate([a_re[g].reshape(1, -1), a_im[g].reshape(1, -1)], axis=1)
        sp["f_row"] = jnp.concatenate([f_re[g].reshape(1, -1), f_im[g].reshape(1, -1)], axis=1)
        sp["bmat"], sp["cmat"] = _ssm_mats(a["ssm_b_re"][l], a["ssm_b_im"][l], a["ssm_c_re"][l], a["ssm_c_im"][l])
        rows.append(sp)
    return rows


def _local_step(a, get_weights, on_grads, depth):
    n_seq, seq, d = a["x"].shape
    t = n_seq * seq
    inv = ROPE_THETA ** (-jnp.arange(0, ROPE_DIM, 2, dtype=F32) / ROPE_DIM)
    lane = np.arange(LANES) % HEAD_DIM
    inv_lane = jnp.where(lane < ROPE_DIM, jnp.tile(inv, LANES // (ROPE_DIM // 2)), 0.0).reshape(1, LANES)
    ctab, stab = _rope_tables(a["positions"].reshape(t), inv_lane)
    small = _small_rows(a, depth)

    x = a["x"].reshape(t, d)
    saved, weights = [], []
    for l in range(depth):
        x, sv, w = _layer_fwd(x, a["p"][l].reshape(t, -1), get_weights(l, "mix", x),
                              functools.partial(get_weights, l, "ffn"), small[l], ctab, stab, n_seq)
        saved.append(sv)
        weights.append(w)
    loss, dx, d_final = _loss_and_grad(x, a["loss_target"].reshape(t, d), _row(a["final_norm_g"]))
    gws, gss = [None] * depth, [None] * depth
    for l in reversed(range(depth)):
        dx, gws[l], gss[l] = _layer_bwd(dx, saved[l], weights[l], small[l], functools.partial(on_grads, l), ctab, stab, n_seq)

    n16 = depth * SSM_GROUPS
    halves = lambda k, h: jnp.concatenate([gss[l][k][:, h * SSM_LANES:(h + 1) * SSM_LANES].reshape(SSM_GROUPS, SSM_STATE)
                                           for l in range(depth)], axis=0)
    dlr, dli, ddt = _ssm_coeffs_bwd(a["ssm_lambda_re"].reshape(n16, SSM_STATE), a["ssm_lambda_im"].reshape(n16, SSM_STATE),
                                    a["ssm_log_dt"].reshape(n16, 1),
                                    (halves("a_row", 0), halves("a_row", 1), halves("f_row", 0), halves("f_row", 1)))
    bc = [_ssm_mats_t(gss[l]["bmat"], gss[l]["cmat"]) for l in range(depth)]
    gsmall = {k: jnp.stack([gss[l][k].reshape(a[k].shape[1:]) for l in range(depth)])
              for k in ("mix_norm_g", "b_gate", "attn_sinks", "ssm_d", "b_ssm_glu", "conv_dw_b", "conv_norm_g", "conv_norm_b",
                        "ffn_norm_g", "ple_norm_g")}
    gsmall["ssm_lambda_re"] = dlr.reshape(a["ssm_lambda_re"].shape)
    gsmall["ssm_lambda_im"] = dli.reshape(a["ssm_lambda_im"].shape)
    gsmall["ssm_log_dt"] = ddt.reshape(a["ssm_log_dt"].shape)
    for i, k in enumerate(("ssm_b_re", "ssm_b_im", "ssm_c_re", "ssm_c_im")):
        gsmall[k] = jnp.stack([bc[l][i] for l in range(depth)])
    gsmall["final_norm_g"] = d_final.reshape(a["final_norm_g"].shape)
    return loss, dx.reshape(n_seq, seq, d), gws, gsmall


def kernel(x, p, positions, mix_norm_g, w_in, b_gate, attn_sinks, w_attn_out, ssm_lambda_re, ssm_lambda_im, ssm_log_dt, ssm_b_re, ssm_b_im, ssm_c_re, ssm_c_im, ssm_d, w_ssm_glu, b_ssm_glu, conv_dw_w, conv_dw_b, conv_norm_g, conv_norm_b, w_conv_out, w_mix_out, ffn_norm_g, w_ffn_in, w_ffn_out, w_ple_in, ple_norm_g, w_ple_gate, final_norm_g, loss_target, m_mix_norm_g, m_w_in, m_b_gate, m_attn_sinks, m_w_attn_out, m_ssm_lambda_re, m_ssm_lambda_im, m_ssm_log_dt, m_ssm_b_re, m_ssm_b_im, m_ssm_c_re, m_ssm_c_im, m_ssm_d, m_w_ssm_glu, m_b_ssm_glu, m_conv_dw_w, m_conv_dw_b, m_conv_norm_g, m_conv_norm_b, m_w_conv_out, m_w_mix_out, m_ffn_norm_g, m_w_ffn_in, m_w_ffn_out, m_w_ple_in, m_ple_norm_g, m_w_ple_gate, m_final_norm_g, v_mix_norm_g, v_w_in, v_b_gate, v_attn_sinks, v_w_attn_out, v_ssm_lambda_re, v_ssm_lambda_im, v_ssm_log_dt, v_ssm_b_re, v_ssm_b_im, v_ssm_c_re, v_ssm_c_im, v_ssm_d, v_w_ssm_glu, v_b_ssm_glu, v_conv_dw_w, v_conv_dw_b, v_conv_norm_g, v_conv_norm_b, v_w_conv_out, v_w_mix_out, v_ffn_norm_g, v_w_ffn_in, v_w_ffn_out, v_w_ple_in, v_ple_norm_g, v_w_ple_gate, v_final_norm_g):
    a = dict(locals())
    depth = w_in.shape[0]
    layouts = {group: _layout(group) for group in GROUPS}
    shift = {"w_in": Z_SPLIT}
    conv_pad = ((0, 0), (0, CONV_HALO - CONV_K), (0, LANES - CONV_WIDTH // N_DEV))

    def packed_weights(l, group, fill):
        col, _, col_rows, row, _ = layouts[group]
        pieces = [_slabs(a[name][l].astype(BF16), fill) for name in col]
        if GROUPS[group][2]:
            pieces.append(jnp.pad(jnp.stack(_split3(a["conv_dw_w"][l])), conv_pad).reshape(-1, LANES))
        regions = [_concat_padded([a[name][l].astype(BF16)], _pad_to(rs, LANES), 0) for name, (_, rs) in row.items()]
        return [_concat_padded(pieces, col_rows, 0), jnp.concatenate(regions, axis=0)]

    gathers, tokens, fill = [], [], jnp.zeros((), BF16)
    for l in range(depth):
        started, token = _exchange_start(f"gather_start_{l}", "gather", [packed_weights(l, group, fill) for group in GROUPS])
        gathers.append(dict(zip(GROUPS, started, strict=True)))
        tokens.append(token[0:1, 0:1])
        fill = token[0, 0].astype(BF16)

    def get_weights(l, group, after):
        col, conv_base, _, row, _ = layouts[group]
        slab8, row8 = _exchange_wait(f"gather_wait_{group}_{l}", gathers[l][group], after)
        w = {name: _unshard_cols("unshard_" + name, slab8, base, rows, cs, shift.get(name, 0))
             for name, (base, rows, cs) in col.items()}
        for name, (base, rs) in row.items():
            w[name] = row8[:, base:base + rs].reshape(N_DEV * rs, -1)
        if GROUPS[group][2]:
            conv = slab8[:, conv_base:conv_base + CONV_W_PIECES * CONV_HALO]
            conv = conv.reshape(N_DEV, CONV_W_PIECES, CONV_HALO, LANES)[:, :, :CONV_K, :CONV_WIDTH // N_DEV].astype(F32)
            w["conv_dw_w"] = jnp.sum(conv, axis=1).transpose(1, 0, 2).reshape(CONV_K, CONV_WIDTH)
        return w

    scatters = {}

    def on_grads(l, group, gw):
        col, _, col_rows, row, _ = layouts[group]
        pieces = [_shard_cols("shard_" + name, gw[name], cs, shift.get(name, 0)) for name, (_, _, cs) in col.items()]
        if GROUPS[group][2]:
            conv = gw["conv_dw_w"].reshape(CONV_K, N_DEV, CONV_WIDTH // N_DEV).transpose(1, 0, 2).astype(BF16)
            pieces.append(jnp.pad(jnp.pad(conv, conv_pad), ((0, 0), (0, (CONV_W_PIECES - 1) * CONV_HALO), (0, 0))))
        regions = [_concat_padded([gw[name].astype(BF16).reshape(N_DEV, rs, -1)], _pad_to(rs, LANES), 1)
                   for name, (_, rs) in row.items()]
        (scatters[l, group],), token = _exchange_start(
            f"grads_start_{group}_{l}", "scatter", [[_concat_padded(pieces, col_rows, 1), jnp.concatenate(regions, axis=1)]])
        return token

    local = dict(a, mix_norm_g=a["mix_norm_g"] + sum(tokens))
    loss, grad_x, _, gsmall = _local_step(local, get_weights, on_grads, depth)
    loss = lax.psum(loss, ("x", "y", "c"))

    shapes = [a[k].shape for k in REPLICATED]
    parts, = _all_gather("gather_small_grads", [_pack_small([gsmall[k] for k in REPLICATED])])
    small_state = [_pack_small([a[pre + k] for k in REPLICATED]) for pre in ("", "m_", "v_")]
    small_flat = _adamw_flat("adamw_replicated", parts, *small_state)
    small = [dict(zip(REPLICATED, _unpack_small(o, shapes), strict=True)) for o in small_flat]

    state = lambda name: (a[name], a["m_" + name], a["v_" + name])
    big, after = {}, small_flat[1]
    for group in ("ffn", "mix"):
        col, conv_base, col_rows, row, row_rows = layouts[group]
        landed = [_exchange_wait(f"grads_wait_{group}_{l}", scatters[l, group], after) for l in range(depth)]
        landed_slab = jnp.concatenate([ls for ls, _ in landed], axis=1)
        landed_row = jnp.concatenate([lr for _, lr in landed], axis=1)
        big.update({name: _adamw_cols("adamw_" + name, landed_slab, base, col_rows, *state(name)) for name, (base, _, _) in col.items()})
        big.update({name: _adamw_rows("adamw_" + name, landed_row, base, row_rows, *state(name)) for name, (base, _) in row.items()})
        if GROUPS[group][2]:
            big["conv_dw_w"] = _adamw_conv(landed_slab, conv_base, col_rows, *state("conv_dw_w"))
        after = big[next(iter(col))][1]

    def result(kind, name):
        if name in REPLICATED:
            return small[kind][name]
        return big[name][kind]

    return (loss, grad_x, *[result(kind, n) for kind in range(4) for n in WEIGHT_ORDER])
```

```python
import functools
import math

import numpy as np
import jax
import jax.numpy as jnp
from jax import lax
from jax.experimental import pallas as pl
from jax.experimental.pallas import tpu as pltpu

F32 = jnp.float32
BF16 = jnp.bfloat16
MXU_DTYPE = jnp.bfloat16
VMEM_LIMIT_BYTES = 56 * 2 ** 20
N_DEV = 8
LANES = 128

HEAD_DIM = 64
N_Q_HEADS = 8
N_KV_HEADS = 2
GQA_GROUP = 4
BLOCK = 128
ROPE_THETA = 500000.0
ROPE_DIM = 16
Q_WIDTH = 512
KV_WIDTH = 128
SSM_WIDTH = 256
SSM_GROUP = 16
SSM_GROUPS = 16
SSM_STATE = 64
SSM_LANES = SSM_GROUPS * SSM_STATE
CONV_WIDTH = 256
CONV_K = 31
CONV_HALO = 32
EPS = 1e-6
NEG_INF = -1e30
ADAM_LR, ADAM_B1, ADAM_B2, ADAM_EPS, ADAM_WD, ADAM_STEP = 0.001, 0.9, 0.999, 1e-08, 0.01, 10

ZG_W, ZQ_W, ZKV_W, ZS_W, ZC_W = 3072, 512, 256, 256, 512
ZQ_BLK, ZKV_BLK, ZS_BLK, ZC_BLK = 3072 // 512, 3584 // 256, 3840 // 256, 4096 // 512
Z_WIDTH = 4608
Z_SPLIT = 1536

SHARDED = (("w_in", 1024, 4608, 1), ("w_attn_out", 512, 1024, 1), ("w_ssm_glu", 256, 2048, 1),
           ("conv_dw_w", 31, 256, 1), ("w_conv_out", 256, 1024, 1), ("w_mix_out", 1024, 1024, 0),
           ("w_ffn_in", 1024, 5632, 1), ("w_ffn_out", 2816, 1024, 0), ("w_ple_in", 256, 1024, 1),
           ("w_ple_gate", 1024, 1024, 0))
GROUPS = {"mix": (("w_in", "w_attn_out", "w_ssm_glu", "w_conv_out"), ("w_mix_out",), True),
          "ffn": (("w_ffn_in", "w_ple_in"), ("w_ffn_out", "w_ple_gate"), False)}
SLAB_TILE = 256
FLAT_ROW_ALIGN = 1024
REPLICATED = ("mix_norm_g", "b_gate", "attn_sinks", "ssm_lambda_re", "ssm_lambda_im", "ssm_log_dt", "ssm_b_re",
              "ssm_b_im", "ssm_c_re", "ssm_c_im", "ssm_d", "b_ssm_glu", "conv_dw_b", "conv_norm_g", "conv_norm_b",
              "ffn_norm_g", "ple_norm_g", "final_norm_g")
WEIGHT_ORDER = ("mix_norm_g", "w_in", "b_gate", "attn_sinks", "w_attn_out", "ssm_lambda_re", "ssm_lambda_im",
                "ssm_log_dt", "ssm_b_re", "ssm_b_im", "ssm_c_re", "ssm_c_im", "ssm_d", "w_ssm_glu", "b_ssm_glu",
                "conv_dw_w", "conv_dw_b", "conv_norm_g", "conv_norm_b", "w_conv_out", "w_mix_out", "ffn_norm_g",
                "w_ffn_in", "w_ffn_out", "w_ple_in", "ple_norm_g", "w_ple_gate", "final_norm_g")


_ANY = pl.BlockSpec(memory_space=pl.ANY)


def _params(sem=None):
    return pltpu.CompilerParams(dimension_semantics=sem, vmem_limit_bytes=VMEM_LIMIT_BYTES)


def _pick(n, cands):
    for c in cands:
        if n % c == 0:
            return c
    return n


def _dot(a, b, dims):
    return lax.dot_general(a.astype(MXU_DTYPE), b.astype(MXU_DTYPE), (dims, ((), ())), preferred_element_type=F32)


def _dot_nn(a, b):
    return _dot(a, b, ((1,), (0,)))


def _dot_nt(a, b):
    return _dot(a, b, ((1,), (1,)))


def _dot_tn(a, b):
    return _dot(a, b, ((0,), (0,)))


@jax.custom_vjp
def _mm(x, w):
    return _dot_nn(x, w)


def _mm_f(x, w):
    return _dot_nn(x, w), (x, w)


def _mm_b(res, dy):
    x, w = res
    return _dot_nt(dy, w).astype(x.dtype), _dot_tn(x, dy).astype(w.dtype)


_mm.defvjp(_mm_f, _mm_b)


def _rms(x, g):
    return x * lax.rsqrt(jnp.mean(x * x, axis=-1, keepdims=True) + EPS) * g


ROW_TILES = (1024, 512, 256, 128)
COL_TILES = (1536, 1408, 1024, 512, 256, 128)


def _matmul_nn(name, a, b, out_dtype):
    t, k = a.shape
    n = b.shape[1]
    tm, tn = _pick(t, ROW_TILES), _pick(n, COL_TILES)

    def body(a_ref, b_ref, o_ref):
        o_ref[...] = _dot_nn(a_ref[...], b_ref[...]).astype(o_ref.dtype)

    return pl.pallas_call(
        body, name=name, grid=(t // tm, n // tn), out_shape=jax.ShapeDtypeStruct((t, n), out_dtype),
        in_specs=[pl.BlockSpec((tm, k), lambda i, j: (i, 0)), pl.BlockSpec((k, tn), lambda i, j: (0, j))],
        out_specs=pl.BlockSpec((tm, tn), lambda i, j: (i, j)),
        compiler_params=_params(("parallel", "parallel")))(a, b)


def _matmul_tn(name, a, b):
    t, m = a.shape
    n = b.shape[1]
    tm, tn, tt = _pick(m, COL_TILES[1:]), _pick(n, COL_TILES), _pick(t, ROW_TILES)

    def body(a_ref, b_ref, o_ref):
        @pl.when(pl.program_id(2) == 0)
        def _():
            o_ref[...] = jnp.zeros_like(o_ref)

        o_ref[...] += _dot_tn(a_ref[...], b_ref[...])

    return pl.pallas_call(
        body, name=name, grid=(m // tm, n // tn, t // tt), out_shape=jax.ShapeDtypeStruct((m, n), F32),
        in_specs=[pl.BlockSpec((tt, tm), lambda i, j, s: (s, i)), pl.BlockSpec((tt, tn), lambda i, j, s: (s, j))],
        out_specs=pl.BlockSpec((tm, tn), lambda i, j, s: (i, j)),
        compiler_params=_params(("parallel", "parallel", "arbitrary")))(a, b)


def _two_parts(n):
    cut = n // (2 * LANES) * LANES
    return [slice(0, n)] if cut == 0 else [slice(0, cut), slice(cut, n)]


def _in_proj(x, g, w):
    t, d = x.shape
    n = w.shape[1]
    tm, tn = _pick(t, ROW_TILES), _pick(n, COL_TILES)

    def body(x_ref, g_ref, w_ref, z_ref, h_ref):
        h = _rms(x_ref[...], g_ref[...]).astype(h_ref.dtype)

        @pl.when(pl.program_id(1) == 0)
        def _():
            h_ref[...] = h

        z_ref[...] = _dot_nn(h, w_ref[...])

    return pl.pallas_call(
        body, name="in_proj", grid=(t // tm, n // tn),
        out_shape=[jax.ShapeDtypeStruct((t, n), F32), jax.ShapeDtypeStruct((t, d), MXU_DTYPE)],
        in_specs=[pl.BlockSpec((tm, d), lambda i, j: (i, 0)), pl.BlockSpec((1, d), lambda i, j: (0, 0)),
                  pl.BlockSpec((d, tn), lambda i, j: (0, j))],
        out_specs=[pl.BlockSpec((tm, tn), lambda i, j: (i, j)), pl.BlockSpec((tm, d), lambda i, j: (i, 0))],
        compiler_params=_params(("parallel", "arbitrary")))(x, g, w)


def _in_proj_bwd(dz, w, x, dx_res, g):
    t, d = x.shape
    tm = _pick(t, ROW_TILES[1:])

    def body(dz_ref, w_ref, x_ref, r_ref, g_ref, dx_ref, dg_ref):
        _, vjp = jax.vjp(_norm_in_tile, x_ref[...], g_ref[...])
        dx, dg = vjp(_dot_nt(dz_ref[...], w_ref[...]))
        dx_ref[...] = dx + r_ref[...]

        @pl.when(pl.program_id(0) == 0)
        def _():
            dg_ref[...] = jnp.zeros_like(dg_ref)

        dg_ref[...] += dg

    rows = lambda width: pl.BlockSpec((tm, width), lambda i: (i, 0))
    whole = lambda a: pl.BlockSpec(a.shape, lambda i: (0, 0))
    return pl.pallas_call(
        body, name="in_proj_bwd", grid=(t // tm,), out_shape=[jax.ShapeDtypeStruct((t, d), F32), jax.ShapeDtypeStruct((1, d), F32)],
        in_specs=[rows(dz.shape[1]), whole(w), rows(d), rows(d), whole(g)], out_specs=[rows(d), whole(g)],
        compiler_params=_params(("arbitrary",)))(dz, w, x, dx_res, g)


def _ffn_in_act(hf, w_fi):
    t, k = hf.shape
    f = w_fi.shape[1] // 2
    tm, tf = _pick(t, ROW_TILES[1:]), _pick(f, COL_TILES)
    nf = f // tf

    def body(a_ref, wg_ref, wu_ref, g_ref, u_ref, act_ref):
        a = a_ref[...]
        for cols in _two_parts(tf):
            g, u = _dot_nn(a, wg_ref[:, cols]), _dot_nn(a, wu_ref[:, cols])
            g_ref[:, cols] = g.astype(g_ref.dtype)
            u_ref[:, cols] = u.astype(u_ref.dtype)
            act_ref[:, cols] = (jax.nn.silu(g) * u).astype(act_ref.dtype)

    out = pl.BlockSpec((tm, tf), lambda i, j: (i, j))
    return pl.pallas_call(
        body, name="ffn_in_act", grid=(t // tm, nf), out_shape=[jax.ShapeDtypeStruct((t, f), MXU_DTYPE)] * 3,
        in_specs=[pl.BlockSpec((tm, k), lambda i, j: (i, 0)), pl.BlockSpec((k, tf), lambda i, j: (0, j)),
                  pl.BlockSpec((k, tf), lambda i, j: (0, j + nf))],
        out_specs=[out, out, out], compiler_params=_params(("parallel", "parallel")))(hf, w_fi, w_fi)


def _ffn_mid_bwd(dffn, w_fo, gate, up):
    t, d = dffn.shape
    f = w_fo.shape[0]
    tm, tf = _pick(t, ROW_TILES[1:]), _pick(f, COL_TILES)

    def body(a_ref, w_ref, g_ref, u_ref, dg_ref, du_ref):
        a = a_ref[...]
        for cols in _two_parts(tf):
            dact = _dot_nt(a, w_ref[cols, :])
            g, u = g_ref[:, cols].astype(F32), u_ref[:, cols].astype(F32)
            sg = jax.nn.sigmoid(g)
            dg_ref[:, cols] = (dact * u * sg * (1.0 + g * (1.0 - sg))).astype(dg_ref.dtype)
            du_ref[:, cols] = (dact * g * sg).astype(du_ref.dtype)

    blk = pl.BlockSpec((tm, tf), lambda i, j: (i, j))
    return pl.pallas_call(
        body, name="ffn_mid_bwd", grid=(t // tm, f // tf), out_shape=[jax.ShapeDtypeStruct((t, f), MXU_DTYPE)] * 2,
        in_specs=[pl.BlockSpec((tm, d), lambda i, j: (i, 0)), pl.BlockSpec((tf, d), lambda i, j: (j, 0)), blk, blk],
        out_specs=[blk, blk], compiler_params=_params(("parallel", "parallel")))(dffn, w_fo, gate, up)


def _ffn_in_dx(dgate, dup, w_fi):
    t, f = dgate.shape
    d = w_fi.shape[0]
    tm = _pick(t, ROW_TILES[1:])

    def body(g_ref, u_ref, w_ref, o_ref):
        o_ref[...] = _dot_nt(g_ref[...], w_ref[:, :f]) + _dot_nt(u_ref[...], w_ref[:, f:])

    blk = pl.BlockSpec((tm, f), lambda i: (i, 0))
    return pl.pallas_call(
        body, name="ffn_in_dx", grid=(t // tm,), out_shape=jax.ShapeDtypeStruct((t, d), F32),
        in_specs=[blk, blk, pl.BlockSpec(w_fi.shape, lambda i: (0, 0))], out_specs=pl.BlockSpec((tm, d), lambda i: (i, 0)),
        compiler_params=_params(("parallel",)))(dgate, dup, w_fi)


def _token_call(name, fn, tile, tok_ins, consts, tok_outs, acc_outs, into=None):
    n_rows = tok_ins[0][0].shape[0]
    tile = min(tile, n_rows)
    n_ti, n_c = len(tok_ins), len(consts)
    n_in = n_ti + n_c + (into is not None)
    n_to = len(tok_outs) + (into is not None)

    def body(*refs):
        ins = [r[...] for r in refs[:n_ti + n_c]]
        outs, accs = fn(*ins)
        for r, v in zip(refs[n_in:n_in + n_to], outs, strict=True):
            r[...] = v.astype(r.dtype)
        first = pl.program_id(0) == 0
        for r, v in zip(refs[n_in + n_to:], accs, strict=True):
            @pl.when(first)
            def _(r=r):
                r[...] = jnp.zeros_like(r)

            r[...] += jnp.broadcast_to(v, r.shape).astype(F32)

    in_specs = [pl.BlockSpec((tile, w), functools.partial(lambda i, c: (i, c), c=cb)) for _, w, cb in tok_ins]
    in_specs += [pl.BlockSpec(c.shape, lambda i: (0, 0)) for c in consts]
    out_shape = [jax.ShapeDtypeStruct((n_rows, w), dt) for w, dt in tok_outs]
    out_specs = [pl.BlockSpec((tile, w), lambda i: (i, 0)) for w, _ in tok_outs]
    operands = [a for a, _, _ in tok_ins] + list(consts)
    aliases = {}
    if into is not None:
        target, width, col_block = into
        in_specs.append(_ANY)
        operands.append(target)
        out_shape.append(jax.ShapeDtypeStruct(target.shape, target.dtype))
        out_specs.append(pl.BlockSpec((tile, width), lambda i: (i, col_block)))
        aliases = {n_in - 1: n_to - 1}
    out_shape += [jax.ShapeDtypeStruct(s, F32) for s in acc_outs]
    out_specs += [pl.BlockSpec(s, lambda i: (0, 0)) for s in acc_outs]
    res = pl.pallas_call(
        body, name=name, grid=(n_rows // tile,), out_shape=out_shape, in_specs=in_specs, out_specs=out_specs,
        input_output_aliases=aliases, compiler_params=_params(("arbitrary",)))(*operands)
    return res[:n_to], res[n_to:]


def _whole(a):
    return (a, a.shape[1], 0)


def _norm_in_tile(x, g):
    return _rms(x, g)


def _conv_post_tile(v, g, b):
    mu = jnp.mean(v, axis=-1, keepdims=True)
    var = jnp.mean(jnp.square(v - mu), axis=-1, keepdims=True)
    return jax.nn.silu((v - mu) * lax.rsqrt(var + EPS) * g + b)


def _merge_tile(x, ya, ys, v, gin, w_ao, w_sg, b_sg, ln_g, ln_b, w_co, b_gate, w_mo, g_ffn):
    d = x.shape[1]
    y_attn = _mm(ya, w_ao)
    pre = _mm(jax.nn.gelu(ys), w_sg) + b_sg
    y_ssm = pre[:, :d] * jax.nn.sigmoid(pre[:, d:])
    y_conv = _mm(_conv_post_tile(v, ln_g, ln_b), w_co)
    gates = jax.nn.sigmoid(gin + b_gate)
    merged = gates[:, :d] * y_attn + gates[:, d:2 * d] * y_ssm + gates[:, 2 * d:] * y_conv
    x1 = x + _mm(merged, w_mo)
    return x1, _rms(x1, g_ffn)


def _ple_tile(x2, p, w_pi, g_ple, w_pg):
    return x2 + jax.nn.sigmoid(_mm(_rms(x2, g_ple), w_pg)) * _mm(p, w_pi)


def _f32s(vals):
    return [v.astype(F32) for v in vals]


def _rope_tables(positions, inv_lane):
    def fn(pos, inv):
        ang = pos.astype(F32) * inv
        j = lax.broadcasted_iota(jnp.int32, ang.shape, 1) % HEAD_DIM
        c = jnp.where(j < ROPE_DIM, jnp.cos(ang), 1.0)
        s = jnp.sin(ang)
        s = jnp.where(j < ROPE_DIM // 2, -s, jnp.where(j < ROPE_DIM, s, 0.0))
        return [c, s], []

    (c, s), _ = _token_call("rope_tables", fn, 1024, [_whole(positions.reshape(-1, 1))], [inv_lane],
                            [(LANES, F32), (LANES, F32)], [])
    return c, s


def _swap_halves(t):
    n = t.shape[1]
    j = lax.broadcasted_iota(jnp.int32, t.shape, 1) % HEAD_DIM
    lower = pltpu.roll(t, n - ROPE_DIM // 2, 1)
    upper = jnp.where(j < ROPE_DIM, pltpu.roll(t, ROPE_DIM // 2, 1), 0.0)
    return jnp.where(j < ROPE_DIM // 2, lower, upper)


def _rope(t, c, s):
    return t * c + _swap_halves(t) * s


def _rope_t(dt, c, s):
    return dt * c + _swap_halves(dt * s)


def _tile4(a):
    return jnp.concatenate([a] * (Q_WIDTH // LANES), axis=1)


def _attn_mask(n):
    qi = lax.broadcasted_iota(jnp.int32, (GQA_GROUP * BLOCK, 2 * BLOCK), 0) % BLOCK
    kj = lax.broadcasted_iota(jnp.int32, (GQA_GROUP * BLOCK, 2 * BLOCK), 1)
    dist = qi + BLOCK - kj
    return (dist >= 0) & (dist < BLOCK) & ((n > 0) | (kj >= BLOCK))


def _attn_specs(n_seq):
    own = lambda w, blk: pl.BlockSpec((n_seq, BLOCK, w), lambda n: (0, n, blk))
    prev = lambda w, blk: pl.BlockSpec((n_seq, BLOCK, w), lambda n: (0, jnp.maximum(n - 1, 0), blk))
    return [own(ZQ_W, ZQ_BLK), own(ZKV_W, ZKV_BLK), prev(ZKV_W, ZKV_BLK), own(LANES, 0), own(LANES, 0), prev(LANES, 0),
            prev(LANES, 0), pl.BlockSpec((1, N_Q_HEADS), lambda n: (0, 0))]


def _by_seq(a, n_seq):
    return a.reshape(n_seq, a.shape[0] // n_seq, a.shape[1])


ATTN_SCALE = HEAD_DIM ** -0.5


def _stack_heads(t, kh):
    return jnp.concatenate([t[:, (kh * GQA_GROUP + g) * HEAD_DIM:(kh * GQA_GROUP + g + 1) * HEAD_DIM]
                            for g in range(GQA_GROUP)], axis=0)


def _stack_sinks(sink, kh):
    return jnp.concatenate([jnp.broadcast_to(sink[:, kh * GQA_GROUP + g:kh * GQA_GROUP + g + 1], (BLOCK, 1))
                            for g in range(GQA_GROUP)], axis=0)


def _attn_band(b, q_ref, kv_ref, kvp_ref, c_ref, s_ref, cp_ref, sp_ref):
    c, s = c_ref[b], s_ref[b]
    q = _rope(q_ref[b], _tile4(c), _tile4(s)) * ATTN_SCALE
    kv, kvp = kv_ref[b], kvp_ref[b]
    k = _rope(kv[:, :KV_WIDTH], c, s)
    kp = _rope(kvp[:, :KV_WIDTH], cp_ref[b], sp_ref[b])
    kb = jnp.concatenate([kp, k], axis=0)
    vb = jnp.concatenate([kvp[:, KV_WIDTH:], kv[:, KV_WIDTH:]], axis=0)
    return q, kb, vb


def _attention_fwd(z, ctab, stab, sinks, n_seq):
    t = z.shape[0]
    seq = t // n_seq

    def body(q_ref, kv_ref, kvp_ref, c_ref, s_ref, cp_ref, sp_ref, sink_ref, o_ref, lse_ref):
        mask = _attn_mask(pl.program_id(0))
        sink = sink_ref[...]
        lane = lax.broadcasted_iota(jnp.int32, (BLOCK, N_Q_HEADS), 1)
        for b in range(n_seq):
            q, kb, vb = _attn_band(b, q_ref, kv_ref, kvp_ref, c_ref, s_ref, cp_ref, sp_ref)
            lse_all = jnp.zeros((BLOCK, N_Q_HEADS), F32)
            for kh in range(N_KV_HEADS):
                sc = jnp.where(mask, _dot_nt(_stack_heads(q, kh), kb[:, kh * HEAD_DIM:(kh + 1) * HEAD_DIM]), NEG_INF)
                sk = _stack_sinks(sink, kh)
                m = jnp.maximum(jnp.max(sc, axis=-1, keepdims=True), sk)
                pr = jnp.exp(sc - m)
                den = jnp.sum(pr, axis=-1, keepdims=True) + jnp.exp(sk - m)
                out = _dot_nn(pr * (1.0 / den), vb[:, kh * HEAD_DIM:(kh + 1) * HEAD_DIM])
                lse = m + jnp.log(den)
                for g in range(GQA_GROUP):
                    h = kh * GQA_GROUP + g
                    o_ref[b, :, h * HEAD_DIM:(h + 1) * HEAD_DIM] = out[g * BLOCK:(g + 1) * BLOCK].astype(o_ref.dtype)
                    lse_all = jnp.where(lane == h, lse[g * BLOCK:(g + 1) * BLOCK], lse_all)
            lse_ref[b] = lse_all

    rows = lambda w: pl.BlockSpec((n_seq, BLOCK, w), lambda n: (0, n, 0))
    z3, c3, s3 = _by_seq(z, n_seq), _by_seq(ctab, n_seq), _by_seq(stab, n_seq)
    ya, lse = pl.pallas_call(
        body, name="attn_fwd", grid=(seq // BLOCK,),
        out_shape=[jax.ShapeDtypeStruct((n_seq, seq, Q_WIDTH), MXU_DTYPE), jax.ShapeDtypeStruct((n_seq, seq, N_Q_HEADS), F32)],
        in_specs=_attn_specs(n_seq), out_specs=[rows(Q_WIDTH), rows(N_Q_HEADS)],
        compiler_params=_params(("parallel",)))(z3, z3, z3, c3, s3, c3, s3, sinks)
    return ya.reshape(t, Q_WIDTH), lse.reshape(t, N_Q_HEADS)


def _attention_bwd(z, ctab, stab, sinks, ya, lse, dya, dz, n_seq):
    t = z.shape[0]
    seq = t // n_seq

    def body(q_ref, kv_ref, kvp_ref, c_ref, s_ref, cp_ref, sp_ref, sink_ref, o_ref, lse_ref, do_ref, _,
             dq_ref, dkv_ref, dkvp_ref, dsink_ref):
        mask = _attn_mask(pl.program_id(0))
        sink = sink_ref[...]
        lane = lax.broadcasted_iota(jnp.int32, (1, N_Q_HEADS), 1)
        dsink = jnp.zeros((1, N_Q_HEADS), F32)
        for b in range(n_seq):
            q, kb, vb = _attn_band(b, q_ref, kv_ref, kvp_ref, c_ref, s_ref, cp_ref, sp_ref)
            lse_all = lse_ref[b]
            o = o_ref[b].astype(F32)
            do = do_ref[b].astype(F32)
            dq_parts = []
            dk_parts, dv_parts = [], []
            for kh in range(N_KV_HEADS):
                kbh = kb[:, kh * HEAD_DIM:(kh + 1) * HEAD_DIM]
                vbh = vb[:, kh * HEAD_DIM:(kh + 1) * HEAD_DIM]
                qs, dos = _stack_heads(q, kh), _stack_heads(do, kh)
                lse = jnp.concatenate([lse_all[:, kh * GQA_GROUP + g:kh * GQA_GROUP + g + 1] for g in range(GQA_GROUP)], axis=0)
                pr = jnp.exp(jnp.where(mask, _dot_nt(qs, kbh), NEG_INF) - lse)
                delta = jnp.sum(dos * _stack_heads(o, kh), axis=-1, keepdims=True)
                ds = pr * (_dot_nt(dos, vbh) - delta)
                dqs = _dot_nn(ds, kbh)
                dq_parts += [dqs[g * BLOCK:(g + 1) * BLOCK] for g in range(GQA_GROUP)]
                dk_parts.append(_dot_tn(ds, qs))
                dv_parts.append(_dot_tn(pr, dos))
                dsk = jnp.exp(_stack_sinks(sink, kh) - lse) * delta
                for g in range(GQA_GROUP):
                    dsink = dsink + jnp.where(lane == kh * GQA_GROUP + g, -jnp.sum(dsk[g * BLOCK:(g + 1) * BLOCK]), 0.0)
            c, s = c_ref[b], s_ref[b]
            dq_ref[b] = _rope_t(jnp.concatenate(dq_parts, axis=1) * ATTN_SCALE, _tile4(c), _tile4(s)).astype(dq_ref.dtype)
            dk = jnp.concatenate(dk_parts, axis=1)
            dv = jnp.concatenate(dv_parts, axis=1)
            dkv_ref[b, :, :KV_WIDTH] = _rope_t(dk[BLOCK:], c, s)
            dkv_ref[b, :, KV_WIDTH:] = dv[BLOCK:]
            dkvp_ref[b, :, :KV_WIDTH] = _rope_t(dk[:BLOCK], cp_ref[b], sp_ref[b])
            dkvp_ref[b, :, KV_WIDTH:] = dv[:BLOCK]

        @pl.when(pl.program_id(0) == 0)
        def _():
            dsink_ref[...] = jnp.zeros_like(dsink_ref)

        dsink_ref[...] += dsink

    rows = lambda w: pl.BlockSpec((n_seq, BLOCK, w), lambda n: (0, n, 0))
    by_seq = lambda a: _by_seq(a, n_seq)
    z3, c3, s3 = by_seq(z), by_seq(ctab), by_seq(stab)
    dz, dkv, dkvp, dsink = pl.pallas_call(
        body, name="attn_bwd", grid=(seq // BLOCK,),
        out_shape=[jax.ShapeDtypeStruct((n_seq, seq, Z_WIDTH), dz.dtype), jax.ShapeDtypeStruct((n_seq, seq, ZKV_W), F32),
                   jax.ShapeDtypeStruct((n_seq, seq, ZKV_W), F32), jax.ShapeDtypeStruct((1, N_Q_HEADS), F32)],
        in_specs=_attn_specs(n_seq) + [rows(Q_WIDTH), rows(N_Q_HEADS), rows(Q_WIDTH), _ANY],
        out_specs=[pl.BlockSpec((n_seq, BLOCK, ZQ_W), lambda n: (0, n, ZQ_BLK)), rows(ZKV_W), rows(ZKV_W),
                   pl.BlockSpec((1, N_Q_HEADS), lambda n: (0, 0))],
        input_output_aliases={11: 0},
        compiler_params=_params(("arbitrary",)))(z3, z3, z3, c3, s3, c3, s3, sinks, by_seq(ya), by_seq(lse), by_seq(dya), by_seq(dz))
    return dz.reshape(t, Z_WIDTH), dkv.reshape(t, ZKV_W), dkvp.reshape(t, ZKV_W), dsink


def _kv_combine(dkv, dkvp, dz, n_seq):
    t = dkv.shape[0]
    seq = t // n_seq
    rows = _pick(seq, (512, 256, 128))
    nt, per = seq // rows, rows // BLOCK
    n_blocks = t // BLOCK

    def body(dkv_ref, dkvp_ref, dkvn_ref, _, o_ref):
        nxt = jnp.where(pl.program_id(1) == nt - 1, 0.0, dkvn_ref[...])
        shifted = nxt if per == 1 else jnp.concatenate([dkvp_ref[BLOCK:, :], nxt], axis=0)
        o_ref[...] = (dkv_ref[...] + shifted).astype(o_ref.dtype)

    tile = pl.BlockSpec((rows, ZKV_W), lambda b, i: (b * nt + i, 0))
    return pl.pallas_call(
        body, name="kv_combine", grid=(n_seq, nt), out_shape=jax.ShapeDtypeStruct(dz.shape, dz.dtype),
        in_specs=[tile, tile,
                  pl.BlockSpec((BLOCK, ZKV_W), lambda b, i: (jnp.minimum((b * nt + i + 1) * per, n_blocks - 1), 0)), _ANY],
        out_specs=pl.BlockSpec((rows, ZKV_W), lambda b, i: (b * nt + i, ZKV_BLK)), input_output_aliases={3: 0},
        compiler_params=_params(("parallel", "parallel")))(dkv, dkvp, dkvp, dz)


def _ssm_coeff_tile(lam_re, lam_im, log_dt):
    lr = jnp.minimum(lam_re, -1e-4)
    dt = jnp.exp(log_dt)
    mag = jnp.exp(lr * dt)
    a_re = mag * jnp.cos(lam_im * dt)
    a_im = mag * jnp.sin(lam_im * dt)
    den = lr * lr + lam_im * lam_im
    x_re = a_re - 1.0
    f_re = (x_re * lr + a_im * lam_im) / den
    f_im = (a_im * lr - x_re * lam_im) / den
    return a_re, a_im, f_re, f_im


def _ssm_coeffs(lam_re, lam_im, log_dt):
    def body(lr_ref, li_ref, dt_ref, *o_refs):
        for r, v in zip(o_refs, _ssm_coeff_tile(lr_ref[...], li_ref[...], dt_ref[...]), strict=True):
            r[...] = v

    return pl.pallas_call(body, name="ssm_coeffs", out_shape=[jax.ShapeDtypeStruct(lam_re.shape, F32)] * 4)(
        lam_re, lam_im, log_dt)


def _ssm_coeffs_bwd(lam_re, lam_im, log_dt, cts):
    def body(lr_ref, li_ref, dt_ref, c0, c1, c2, c3, dlr_ref, dli_ref, ddt_ref):
        _, vjp = jax.vjp(_ssm_coeff_tile, lr_ref[...], li_ref[...], dt_ref[...])
        dlr, dli, ddt = vjp((c0[...], c1[...], c2[...], c3[...]))
        dlr_ref[...] = dlr
        dli_ref[...] = dli
        ddt_ref[...] = ddt

    return pl.pallas_call(
        body, name="ssm_coeffs_bwd",
        out_shape=[jax.ShapeDtypeStruct(lam_re.shape, F32)] * 2 + [jax.ShapeDtypeStruct(log_dt.shape, F32)])(
        lam_re, lam_im, log_dt, *cts)


def _ssm_chunk(t):
    return _pick(t, (256, 128))


def _ssm_fwd(z, bmat, a_row, f_row, cmat, d_row, n_seq):
    t = z.shape[0]
    seq = t // n_seq
    lc = _ssm_chunk(seq)
    nc = seq // lc
    n2 = 2 * SSM_LANES

    def body(u_ref, b_ref, a_ref, f_ref, c_ref, d_ref, y_ref, s_ref, bu_ref, st_ref):
        @pl.when(pl.program_id(1) == 0)
        def _():
            st_ref[...] = jnp.zeros_like(st_ref)

        u = u_ref[...]
        proj = _dot_nn(u, b_ref[...])
        fr, fi = f_ref[:, :SSM_LANES], f_ref[:, SSM_LANES:]
        pr, pi = proj[:, :SSM_LANES], proj[:, SSM_LANES:]
        bu_ref[:, :SSM_LANES] = fr * pr - fi * pi
        bu_ref[:, SSM_LANES:] = fr * pi + fi * pr
        ar, ai = a_ref[:, :SSM_LANES], a_ref[:, SSM_LANES:]

        def step(i, carry):
            sr, si = carry
            nr = ar * sr - ai * si + bu_ref[pl.ds(i, 1), pl.ds(0, SSM_LANES)]
            ni = ar * si + ai * sr + bu_ref[pl.ds(i, 1), pl.ds(SSM_LANES, SSM_LANES)]
            s_ref[pl.ds(i, 1), pl.ds(0, SSM_LANES)] = nr
            s_ref[pl.ds(i, 1), pl.ds(SSM_LANES, SSM_LANES)] = ni
            return nr, ni

        sr, si = lax.fori_loop(0, lc, step, (st_ref[0:1, :SSM_LANES], st_ref[0:1, SSM_LANES:]), unroll=8)
        st_ref[0:1, :SSM_LANES] = sr
        st_ref[0:1, SSM_LANES:] = si
        y_ref[...] = _dot_nn(s_ref[...], c_ref[...]) + d_ref[...] * u

    const = lambda shape: pl.BlockSpec(shape, lambda b, c: (0, 0))
    return pl.pallas_call(
        body, name="ssm_fwd", grid=(n_seq, nc),
        out_shape=[jax.ShapeDtypeStruct((t, SSM_WIDTH), F32), jax.ShapeDtypeStruct((t, n2), F32)],
        in_specs=[pl.BlockSpec((lc, ZS_W), lambda b, c: (b * nc + c, ZS_BLK)), const((SSM_WIDTH, n2)), const((1, n2)),
                  const((1, n2)), const((n2, SSM_WIDTH)), const((1, SSM_WIDTH))],
        out_specs=[pl.BlockSpec((lc, SSM_WIDTH), lambda b, c: (b * nc + c, 0)),
                   pl.BlockSpec((lc, n2), lambda b, c: (b * nc + c, 0))],
        scratch_shapes=[pltpu.VMEM((lc, n2), F32), pltpu.VMEM((8, n2), F32)],
        compiler_params=_params(("arbitrary", "arbitrary")))(z, bmat, a_row, f_row, cmat, d_row)


def _ssm_bwd(z, states, dy, bmat, a_row, f_row, cmat, d_row, dz, n_seq):
    t = z.shape[0]
    seq = t // n_seq
    lc = _ssm_chunk(seq)
    nc = seq // lc
    n2 = 2 * SSM_LANES

    def body(dy_ref, u_ref, s_ref, b_ref, a_ref, f_ref, c_ref, d_ref, _,
             du_ref, db_ref, dc_ref, da_ref, df_ref, dd_ref, g_ref, carry_ref):
        @pl.when((pl.program_id(0) == 0) & (pl.program_id(1) == 0))
        def _():
            for r in (db_ref, dc_ref, da_ref, df_ref, dd_ref):
                r[...] = jnp.zeros_like(r)

        @pl.when(pl.program_id(1) == 0)
        def _():
            carry_ref[...] = jnp.zeros_like(carry_ref)

        dy, u, st = dy_ref[...], u_ref[...], s_ref[...]
        g_ref[0:lc, :] = _dot_nt(dy, c_ref[...])
        g_ref[lc:lc + 8, :] = carry_ref[...]
        dc_ref[...] += _dot_tn(st, dy)
        dd_ref[...] += jnp.sum(dy * u, axis=0, keepdims=True)
        ar, ai = a_ref[:, :SSM_LANES], a_ref[:, SSM_LANES:]

        def step(i, carry):
            gr, gi = carry
            r = lc - 1 - i
            nr = g_ref[pl.ds(r, 1), pl.ds(0, SSM_LANES)] + ar * gr + ai * gi
            ni = g_ref[pl.ds(r, 1), pl.ds(SSM_LANES, SSM_LANES)] - ai * gr + ar * gi
            g_ref[pl.ds(r, 1), pl.ds(0, SSM_LANES)] = nr
            g_ref[pl.ds(r, 1), pl.ds(SSM_LANES, SSM_LANES)] = ni
            return nr, ni

        gr, gi = lax.fori_loop(0, lc, step, (carry_ref[0:1, :SSM_LANES], carry_ref[0:1, SSM_LANES:]), unroll=8)
        carry_ref[0:1, :SSM_LANES] = gr
        carry_ref[0:1, SSM_LANES:] = gi
        sr, si = st[:, :SSM_LANES], st[:, SSM_LANES:]
        gnr, gni = g_ref[pl.ds(1, lc), pl.ds(0, SSM_LANES)], g_ref[pl.ds(1, lc), pl.ds(SSM_LANES, SSM_LANES)]
        da_ref[:, :SSM_LANES] += jnp.sum(gnr * sr + gni * si, axis=0, keepdims=True)
        da_ref[:, SSM_LANES:] += jnp.sum(gni * sr - gnr * si, axis=0, keepdims=True)
        gr_all, gi_all = g_ref[0:lc, :SSM_LANES], g_ref[0:lc, SSM_LANES:]
        proj = _dot_nn(u, b_ref[...])
        pr, pi = proj[:, :SSM_LANES], proj[:, SSM_LANES:]
        df_ref[:, :SSM_LANES] += jnp.sum(gr_all * pr + gi_all * pi, axis=0, keepdims=True)
        df_ref[:, SSM_LANES:] += jnp.sum(gi_all * pr - gr_all * pi, axis=0, keepdims=True)
        fr, fi = f_ref[:, :SSM_LANES], f_ref[:, SSM_LANES:]
        dproj = jnp.concatenate([fr * gr_all + fi * gi_all, fr * gi_all - fi * gr_all], axis=1).astype(MXU_DTYPE)
        du_ref[...] = (_dot_nt(dproj, b_ref[...]) + d_ref[...] * dy).astype(du_ref.dtype)
        db_ref[...] += _dot_tn(u, dproj)

    const = lambda shape: pl.BlockSpec(shape, lambda b, c: (0, 0))
    rows = lambda w, cb: pl.BlockSpec((lc, w), functools.partial(lambda b, c, cb: (b * nc + nc - 1 - c, cb), cb=cb))
    return pl.pallas_call(
        body, name="ssm_bwd", grid=(n_seq, nc),
        out_shape=[jax.ShapeDtypeStruct(dz.shape, dz.dtype), jax.ShapeDtypeStruct((SSM_WIDTH, n2), F32),
                   jax.ShapeDtypeStruct((n2, SSM_WIDTH), F32), jax.ShapeDtypeStruct((1, n2), F32),
                   jax.ShapeDtypeStruct((1, n2), F32), jax.ShapeDtypeStruct((1, SSM_WIDTH), F32)],
        in_specs=[rows(SSM_WIDTH, 0), rows(ZS_W, ZS_BLK), rows(n2, 0), const((SSM_WIDTH, n2)), const((1, n2)),
                  const((1, n2)), const((n2, SSM_WIDTH)), const((1, SSM_WIDTH)), _ANY],
        out_specs=[rows(ZS_W, ZS_BLK), const((SSM_WIDTH, n2)), const((n2, SSM_WIDTH)), const((1, n2)), const((1, n2)),
                   const((1, SSM_WIDTH))],
        input_output_aliases={8: 0},
        scratch_shapes=[pltpu.VMEM((lc + 8, n2), F32), pltpu.VMEM((8, n2), F32)],
        compiler_params=_params(("arbitrary", "arbitrary")))(dy, z, states, bmat, a_row, f_row, cmat, d_row, dz)


def _conv_chunk(t):
    return _pick(t, (512, 256, 128))


def _glu(c):
    return c[:, :CONV_WIDTH] * jax.nn.sigmoid(c[:, CONV_WIDTH:])


def _conv_specs(lc, nc):
    per = lc // CONV_HALO
    return [pl.BlockSpec((lc, ZC_W), lambda b, c: (b * nc + c, ZC_BLK)),
            pl.BlockSpec((CONV_HALO, ZC_W), lambda b, c: (jnp.maximum((b * nc + c) * per - 1, 0), ZC_BLK))]


def _conv_fill(c_ref, cp_ref, ue_ref, lc):
    ue_ref[0:CONV_HALO, :] = jnp.where(pl.program_id(1) > 0, _glu(cp_ref[...]), 0.0)
    ue_ref[CONV_HALO:CONV_HALO + lc, :] = _glu(c_ref[...])


def _conv_apply(ue_ref, w_ref, b_ref, lc):
    acc = jnp.zeros((lc, CONV_WIDTH), F32) + b_ref[...]
    for k in range(CONV_K):
        acc = acc + w_ref[k:k + 1, :] * ue_ref[pl.ds(k + CONV_HALO - CONV_K + 1, lc), :]
    return acc


def _conv_fwd(z, dw_w, dw_b, n_seq):
    t = z.shape[0]
    seq = t // n_seq
    lc = _conv_chunk(seq)
    nc = seq // lc

    def body(c_ref, cp_ref, w_ref, b_ref, o_ref, ue_ref):
        _conv_fill(c_ref, cp_ref, ue_ref, lc)
        o_ref[...] = _conv_apply(ue_ref, w_ref, b_ref, lc)

    const = lambda a: pl.BlockSpec(a.shape, lambda b, c: (0, 0))
    return pl.pallas_call(
        body, name="conv_fwd", grid=(n_seq, nc), out_shape=jax.ShapeDtypeStruct((t, CONV_WIDTH), F32),
        in_specs=_conv_specs(lc, nc) + [const(dw_w), const(dw_b)],
        out_specs=pl.BlockSpec((lc, CONV_WIDTH), lambda b, c: (b * nc + c, 0)),
        scratch_shapes=[pltpu.VMEM((CONV_HALO + lc, CONV_WIDTH), F32)],
        compiler_params=_params(("parallel", "parallel")))(z, z, dw_w, dw_b)


def _conv_bwd_taps(z, dv, dw_w, dz, n_seq):
    t = z.shape[0]
    seq = t // n_seq
    lc = _conv_chunk(seq)
    nc = seq // lc
    per = lc // CONV_HALO
    n_halo = t // CONV_HALO

    def body(c_ref, cp_ref, dv_ref, dvn_ref, w_ref, _, dc_ref, dw_ref, ue_ref, dve_ref):
        @pl.when((pl.program_id(0) == 0) & (pl.program_id(1) == 0))
        def _():
            dw_ref[...] = jnp.zeros_like(dw_ref)

        _conv_fill(c_ref, cp_ref, ue_ref, lc)
        dv = dv_ref[...]
        dve_ref[0:lc, :] = dv
        dve_ref[lc:lc + CONV_HALO, :] = jnp.where(pl.program_id(1) < nc - 1, dvn_ref[...], 0.0)
        dw_ref[CONV_K:CONV_K + 1, :] += jnp.sum(dv, axis=0, keepdims=True)
        du = jnp.zeros((lc, CONV_WIDTH), F32)
        for k in range(CONV_K):
            du = du + w_ref[k:k + 1, :] * dve_ref[pl.ds(CONV_K - 1 - k, lc), :]
            dw_ref[k:k + 1, :] += jnp.sum(dv * ue_ref[pl.ds(k + CONV_HALO - CONV_K + 1, lc), :], axis=0, keepdims=True)
        c = c_ref[...]
        a, sg = c[:, :CONV_WIDTH], jax.nn.sigmoid(c[:, CONV_WIDTH:])
        dc_ref[:, :CONV_WIDTH] = (du * sg).astype(dc_ref.dtype)
        dc_ref[:, CONV_WIDTH:] = (du * a * sg * (1.0 - sg)).astype(dc_ref.dtype)

    return pl.pallas_call(
        body, name="conv_bwd_taps", grid=(n_seq, nc),
        out_shape=[jax.ShapeDtypeStruct(dz.shape, dz.dtype), jax.ShapeDtypeStruct((CONV_HALO, CONV_WIDTH), F32)],
        in_specs=_conv_specs(lc, nc) + [
            pl.BlockSpec((lc, CONV_WIDTH), lambda b, c: (b * nc + c, 0)),
            pl.BlockSpec((CONV_HALO, CONV_WIDTH), lambda b, c: (jnp.minimum((b * nc + c + 1) * per, n_halo - 1), 0)),
            pl.BlockSpec(dw_w.shape, lambda b, c: (0, 0)), _ANY],
        out_specs=[pl.BlockSpec((lc, ZC_W), lambda b, c: (b * nc + c, ZC_BLK)),
                   pl.BlockSpec((CONV_HALO, CONV_WIDTH), lambda b, c: (0, 0))],
        input_output_aliases={5: 0},
        scratch_shapes=[pltpu.VMEM((CONV_HALO + lc, CONV_WIDTH), F32), pltpu.VMEM((lc + CONV_HALO, CONV_WIDTH), F32)],
        compiler_params=_params(("arbitrary", "arbitrary")))(z, z, dv, dv, dw_w, dz)


def _row(v):
    return v.reshape(1, -1)


def _ssm_mats(b_re, b_im, c_re, c_im):
    eye = jnp.eye(SSM_GROUPS, dtype=bool)
    bm = jnp.stack([b_re, b_im]).transpose(1, 3, 0, 2)[:, :, :, None, :]
    bmat = jnp.where(eye[:, None, None, :, None], bm, 0.0).reshape(SSM_WIDTH, 2 * SSM_LANES)
    cm = jnp.stack([c_re, -c_im]).transpose(0, 1, 3, 2)[:, :, :, None, :]
    cmat = jnp.where(eye[None, :, None, :, None], cm, 0.0).reshape(2 * SSM_LANES, SSM_WIDTH)
    return bmat.astype(MXU_DTYPE), cmat.astype(MXU_DTYPE)


def _ssm_mats_t(dbmat, dcmat):
    eye = jnp.eye(SSM_GROUPS, dtype=bool)
    db = dbmat.reshape(SSM_GROUPS, SSM_GROUP, 2, SSM_GROUPS, SSM_STATE)
    db = jnp.sum(jnp.where(eye[:, None, None, :, None], db, 0.0), axis=3).transpose(2, 0, 3, 1)
    dc = dcmat.reshape(2, SSM_GROUPS, SSM_STATE, SSM_GROUPS, SSM_GROUP)
    dc = jnp.sum(jnp.where(eye[None, :, None, :, None], dc, 0.0), axis=3).transpose(0, 1, 3, 2)
    return db[0], db[1], dc[0], -dc[1]


def _layer_fwd(x, p, w, get_ffn_weights, sp, ctab, stab, n_seq):
    z, h = _in_proj(x, sp["mix_norm_g"], w["w_in"])
    ya, lse = _attention_fwd(z, ctab, stab, sp["attn_sinks"], n_seq)
    ys, states = _ssm_fwd(z, sp["bmat"], sp["a_row"], sp["f_row"], sp["cmat"], sp["ssm_d"], n_seq)
    v = _conv_fwd(z, w["conv_dw_w"], sp["conv_dw_b"], n_seq)
    merge_consts = [w["w_attn_out"], w["w_ssm_glu"], sp["b_ssm_glu"], sp["conv_norm_g"], sp["conv_norm_b"], w["w_conv_out"],
                    sp["b_gate"], w["w_mix_out"], sp["ffn_norm_g"]]
    (x1, hf), _ = _token_call("merge", lambda *a: (list(_merge_tile(*_f32s(a))), []), 512,
                              [_whole(x), _whole(ya), _whole(ys), _whole(v), (z, ZG_W, 0)], merge_consts,
                              [(x.shape[1], F32), (x.shape[1], MXU_DTYPE)], [])
    w = dict(w, **get_ffn_weights(x1))
    gate, up, act = _ffn_in_act(hf, w["w_ffn_in"])
    ffn = _matmul_nn("mm_ffn_out", act, w["w_ffn_out"], F32)

    def ple_fn(x1, ffn, p, w_pi, g_ple, w_pg):
        x2 = x1 + ffn
        return [x2, _ple_tile(x2, p, w_pi.astype(F32), g_ple, w_pg.astype(F32))], []

    (x2, x3), _ = _token_call("ple", ple_fn, 512, [_whole(x1), _whole(ffn), _whole(p)],
                              [w["w_ple_in"], sp["ple_norm_g"], w["w_ple_gate"]], [(x.shape[1], F32)] * 2, [])
    saved = dict(x=x, h=h, z=z, ya=ya, lse=lse, ys=ys, states=states, v=v, hf=hf, gate=gate, up=up, act=act, x2=x2, p=p)
    return x3, saved, w


def _layer_bwd(dx3, sv, w, sp, on_grads, ctab, stab, n_seq):
    d = dx3.shape[1]
    gw, gs = {}, {}

    def ple_bwd(x2, p, dx3, w_pi, g_ple, w_pg):
        _, vjp = jax.vjp(lambda x2, w_pi, g_ple, w_pg: _ple_tile(x2, p, w_pi, g_ple, w_pg), x2, w_pi.astype(F32), g_ple,
                         w_pg.astype(F32))
        dx2, dw_pi, dg_ple, dw_pg = vjp(dx3)
        return [dx2, dx2], [dw_pi, dg_ple, dw_pg]

    (dx2, dffn), (gw["w_ple_in"], gs["ple_norm_g"], gw["w_ple_gate"]) = _token_call(
        "ple_bwd", ple_bwd, 512, [_whole(sv["x2"]), _whole(sv["p"]), _whole(dx3)],
        [w["w_ple_in"], sp["ple_norm_g"], w["w_ple_gate"]], [(d, F32), (d, MXU_DTYPE)],
        [w["w_ple_in"].shape, (1, d), w["w_ple_gate"].shape])

    dgate, dup = _ffn_mid_bwd(dffn, w["w_ffn_out"], sv["gate"], sv["up"])
    gw["w_ffn_out"] = _matmul_tn("mm_ffn_out_dw", sv["act"], dffn)
    dhf = _ffn_in_dx(dgate, dup, w["w_ffn_in"])
    gw["w_ffn_in"] = jnp.concatenate([_matmul_tn("mm_ffn_gate_dw", sv["hf"], dgate), _matmul_tn("mm_ffn_up_dw", sv["hf"], dup)],
                                     axis=1)

    token = on_grads("ffn", gw)

    def merge_bwd(x, ya, ys, v, gin, dx1, dhf, *consts):
        consts = _f32s(consts)
        _, vjp = jax.vjp(_merge_tile, *_f32s((x, ya, ys, v, gin)), *consts)
        g = vjp((dx1, dhf))
        return list(g[:5]), list(g[5:])

    b_gate = sp["b_gate"] if token is None else sp["b_gate"] + token[0:1, 0:1]
    merge_consts = [w["w_attn_out"], w["w_ssm_glu"], sp["b_ssm_glu"], sp["conv_norm_g"], sp["conv_norm_b"], w["w_conv_out"],
                    b_gate, w["w_mix_out"], sp["ffn_norm_g"]]
    dz = lax.empty(sv["z"].shape, MXU_DTYPE)
    (dx_res, dya, dys, dv, dz), macc = _token_call(
        "merge_bwd", merge_bwd, 256,
        [_whole(sv["x"]), _whole(sv["ya"]), _whole(sv["ys"]), _whole(sv["v"]), (sv["z"], ZG_W, 0), _whole(dx2), _whole(dhf)],
        merge_consts, [(d, F32), (Q_WIDTH, MXU_DTYPE), (SSM_WIDTH, F32), (CONV_WIDTH, F32)],
        [c.shape for c in merge_consts], into=(dz, ZG_W, 0))
    (gw["w_attn_out"], gw["w_ssm_glu"], gs["b_ssm_glu"], gs["conv_norm_g"], gs["conv_norm_b"], gw["w_conv_out"], gs["b_gate"],
     gw["w_mix_out"], gs["ffn_norm_g"]) = macc

    dz, dw_taps = _conv_bwd_taps(sv["z"], dv, w["conv_dw_w"], dz, n_seq)
    gw["conv_dw_w"], gs["conv_dw_b"] = dw_taps[:CONV_K], dw_taps[CONV_K:]

    dz, gs["bmat"], gs["cmat"], gs["a_row"], gs["f_row"], gs["ssm_d"] = _ssm_bwd(
        sv["z"], sv["states"], dys, sp["bmat"], sp["a_row"], sp["f_row"], sp["cmat"], sp["ssm_d"], dz, n_seq)

    dz, dkv, dkvp, gs["attn_sinks"] = _attention_bwd(sv["z"], ctab, stab, sp["attn_sinks"], sv["ya"], sv["lse"], dya, dz, n_seq)
    dz = _kv_combine(dkv, dkvp, dz, n_seq)
    gw["w_in"] = _matmul_tn("mm_in_dw", sv["h"], dz)
    token = on_grads("mix", gw)
    g_in = sp["mix_norm_g"] if token is None else sp["mix_norm_g"] + token[0:1, 0:1]
    dx, gs["mix_norm_g"] = _in_proj_bwd(dz, w["w_in"], sv["x"], dx_res, g_in)
    return dx, gw, gs


def _loss_and_grad(x, target, g):
    def fn(x, tgt, g):
        def f(x, g):
            err = _rms(x, g) - tgt
            return 0.5 * jnp.mean(err * err, axis=-1, keepdims=True)

        per_token, vjp = jax.vjp(f, x, g)
        dx, dg = vjp(jnp.ones_like(per_token))
        return [dx], [jnp.sum(per_token, axis=0, keepdims=True), dg]

    (dx,), (loss, dg) = _token_call("loss", fn, 512, [_whole(x), _whole(target)], [g], [(x.shape[1], F32)],
                                    [(8, LANES), (1, x.shape[1])])
    return loss[0, 0], dx, dg


def _mesh_place():
    return lax.axis_index("x"), lax.axis_index("y"), lax.axis_index("c")


def _flip(v, bit):
    return 1 - v if bit else v


_MESH = pl.DeviceIdType.MESH


def _all_gather(name, xs):
    n = len(xs)

    def body(*refs):
        x_refs, out_refs = refs[:n], refs[n:2 * n]
        send_sems, recv_sems, local_sems = refs[2 * n:]
        mx, my, mc = _mesh_place()
        me, sibling = (mx, my, mc), (mx, my, 1 - mc)
        chips = [(1 - mx, my), (mx, 1 - my), (1 - mx, 1 - my)]

        def slot(a, px, py, pc):
            return out_refs[a].at[4 * px + 2 * py + pc]

        def copy(a, k, block, to, src=None):
            return pltpu.make_async_remote_copy(
                src_ref=slot(a, *block) if src is None else src, dst_ref=slot(a, *block), send_sem=send_sems.at[7 * a + k],
                recv_sem=recv_sems.at[7 * a + k], device_id=to, device_id_type=_MESH)

        mine = [pltpu.make_async_copy(x_refs[a], slot(a, *me), local_sems.at[a]) for a in range(n)]
        for cp in mine:
            cp.start()
        first = [copy(a, 0, me, sibling, src=x_refs[a]) for a in range(n)]
        first += [copy(a, 1 + j, me, (*chip, mc), src=x_refs[a]) for j, chip in enumerate(chips) for a in range(n)]
        for cp in first:
            cp.start()
        passed = []
        for j, chip in enumerate(chips):
            for a in range(n):
                copy(a, 1 + j, (*chip, mc), me).wait_recv()
                passed.append(copy(a, 4 + j, (*chip, mc), sibling))
                passed[-1].start()
        for a in range(n):
            copy(a, 0, sibling, me).wait_recv()
            for j, chip in enumerate(chips):
                copy(a, 4 + j, (*chip, 1 - mc), me).wait_recv()
        for cp in first + passed:
            cp.wait_send()
        for cp in mine:
            cp.wait()

    return pl.pallas_call(
        body, name=name, out_shape=[jax.ShapeDtypeStruct((N_DEV,) + x.shape, x.dtype) for x in xs], in_specs=[_ANY] * n,
        out_specs=[_ANY] * n,
        scratch_shapes=[pltpu.SemaphoreType.DMA((7 * n,)), pltpu.SemaphoreType.DMA((7 * n,)), pltpu.SemaphoreType.DMA((n,))])(*xs)


def _direct_copies(kind, src_refs, land_refs, send_sems, recv_sems, local_sems):
    mx, my, mc = _mesh_place()
    me = 4 * mx + 2 * my + mc
    n = len(src_refs)
    own = [pltpu.make_async_copy(src_refs[a] if kind == "gather" else src_refs[a].at[me], land_refs[a].at[me], local_sems.at[a])
           for a in range(n)]
    copies = []
    for rel in range(1, N_DEV):
        px, py, pc = _flip(mx, rel & 4), _flip(my, rel & 2), _flip(mc, rel & 1)
        for a in range(n):
            src = src_refs[a] if kind == "gather" else src_refs[a].at[4 * px + 2 * py + pc]
            copies.append(pltpu.make_async_remote_copy(
                src_ref=src, dst_ref=land_refs[a].at[me], send_sem=send_sems.at[7 * a + rel - 1],
                recv_sem=recv_sems.at[7 * a + rel - 1], device_id=(px, py, pc), device_id_type=_MESH))
    return copies, own


_HBM = pl.BlockSpec(memory_space=pltpu.HBM)
_SEM = pl.BlockSpec(memory_space=pltpu.SEMAPHORE)
_DATAFLOW = pltpu.SideEffectType.DATAFLOW_SIDE_EFFECTING


def _exchange_start(name, kind, groups):
    sizes = [len(g) for g in groups]
    srcs = [s for g in groups for s in g]
    lands = [lax.empty(((N_DEV,) + s.shape) if kind == "gather" else s.shape, s.dtype) for s in srcs]
    n, n_g = len(srcs), len(groups)
    first = [sum(sizes[:g]) for g in range(n_g)]

    def body(*refs):
        src_refs, land_refs, sems = refs[:n], refs[n:2 * n], refs[2 * n:2 * n + 3 * n_g]
        for g in range(n_g):
            span = slice(first[g], first[g] + sizes[g])
            copies, own = _direct_copies(kind, src_refs[span], land_refs[span], *sems[3 * g:3 * g + 3])
            for cp in own + copies:
                cp.start()
        refs[-1][...] = jnp.zeros_like(refs[-1])

    hbm = lambda a: pltpu.with_memory_space_constraint(a, pltpu.HBM)
    sem_shapes = [pltpu.SemaphoreType.DMA((k * m,)) for m in sizes for k in (7, 7, 1)]
    out = pl.pallas_call(
        body, name=name,
        out_shape=sem_shapes + [pltpu.HBM(a.shape, a.dtype) for a in srcs + lands] + [jax.ShapeDtypeStruct((8, LANES), F32)],
        in_specs=[_HBM] * (2 * n), out_specs=[_SEM] * (3 * n_g) + [_HBM] * (2 * n) + [pl.BlockSpec(memory_space=pltpu.VMEM)],
        input_output_aliases={i: 3 * n_g + i for i in range(2 * n)},
        compiler_params=pltpu.CompilerParams(has_side_effects=_DATAFLOW))(*[hbm(a) for a in srcs + lands])
    sems, arrays = out[:3 * n_g], out[3 * n_g:-1]
    started = [(kind, (*sems[3 * g:3 * g + 3], *arrays[first[g]:first[g] + sizes[g]],
                       *arrays[n + first[g]:n + first[g] + sizes[g]])) for g in range(n_g)]
    return started, out[-1]


def _exchange_wait(name, started, after):
    kind, (send_sems, recv_sems, local_sems, *arrays) = started
    n = len(arrays) // 2

    def body(*refs):
        src_refs, land_refs = refs[:n], refs[n:2 * n]
        copies, own = _direct_copies(kind, src_refs, land_refs, *refs[2 * n:2 * n + 3])
        for cp in copies + own:
            cp.wait()

    out = pl.pallas_call(
        body, name=name, out_shape=[pltpu.HBM(a.shape, a.dtype) for a in arrays],
        in_specs=[_HBM] * (2 * n) + [_SEM] * 3 + [_ANY], out_specs=[_HBM] * (2 * n),
        input_output_aliases={i: i for i in range(2 * n)},
        compiler_params=pltpu.CompilerParams(has_side_effects=_DATAFLOW))(*arrays, send_sems, recv_sems, local_sems, after)
    return out[n:]


def _adamw_math(g, w, m, v):
    m2 = ADAM_B1 * m + (1.0 - ADAM_B1) * g
    v2 = ADAM_B2 * v + (1.0 - ADAM_B2) * jnp.square(g)
    m_hat = m2 / (1.0 - ADAM_B1 ** ADAM_STEP)
    v_hat = v2 / (1.0 - ADAM_B2 ** ADAM_STEP)
    return g, -ADAM_LR * (m_hat / (jnp.sqrt(v_hat) + ADAM_EPS) + ADAM_WD * w), m2, v2


def _sum_blocks(ref):
    g = ref[0].astype(F32)
    for j in range(1, N_DEV):
        g = g + ref[j].astype(F32)
    return g


def _adamw_flat(name, parts, w, m, v):
    r = w.shape[0]
    tile = _pick(r, (1024, 512, 256, 128, 8))

    def body(p_ref, w_ref, m_ref, v_ref, *o_refs):
        for o, val in zip(o_refs, _adamw_math(_sum_blocks(p_ref), w_ref[...], m_ref[...], v_ref[...]), strict=True):
            o[...] = val

    flat = pl.BlockSpec((tile, LANES), lambda i: (i, 0))
    return pl.pallas_call(
        body, name=name, grid=(r // tile,), out_shape=[jax.ShapeDtypeStruct((r, LANES), F32)] * 4,
        in_specs=[pl.BlockSpec((N_DEV, tile, LANES), lambda i: (0, i, 0)), flat, flat, flat], out_specs=[flat] * 4,
        compiler_params=_params(("parallel",)))(parts, w, m, v)


def _adamw_cols(name, landed, base, stride, w, m, v):
    depth, rows, cs = w.shape
    n_slab = -(-cs // LANES)
    tr = _pick(rows, (SLAB_TILE,))

    def body(*refs):
        slabs, (w_ref, m_ref, v_ref), o_refs = refs[:n_slab], refs[n_slab:n_slab + 3], refs[n_slab + 3:]
        g = jnp.concatenate([_sum_blocks(s)[:, :min(LANES, cs - LANES * k)] for k, s in enumerate(slabs)], axis=1)
        for o, val in zip(o_refs, _adamw_math(g, w_ref[...], m_ref[...], v_ref[...]), strict=True):
            o[...] = val

    slab = lambda k: pl.BlockSpec((N_DEV, tr, LANES), lambda l, i: (0, (l * stride + base + k * rows) // tr + i, 0))
    nat = pl.BlockSpec((None, tr, cs), lambda l, i: (l, i, 0))
    return pl.pallas_call(
        body, name=name, grid=(depth, rows // tr), out_shape=[jax.ShapeDtypeStruct(w.shape, F32)] * 4,
        in_specs=[slab(k) for k in range(n_slab)] + [nat] * 3, out_specs=[nat] * 4,
        compiler_params=_params(("parallel", "parallel")))(*[landed] * n_slab, w, m, v)


def _adamw_rows(name, landed, base, stride, w, m, v):
    depth, rs, width = w.shape
    tr = math.gcd(rs, base, stride)

    def body(p_ref, w_ref, m_ref, v_ref, *o_refs):
        for o, val in zip(o_refs, _adamw_math(_sum_blocks(p_ref), w_ref[...], m_ref[...], v_ref[...]), strict=True):
            o[...] = val

    nat = pl.BlockSpec((None, tr, width), lambda l, i: (l, i, 0))
    return pl.pallas_call(
        body, name=name, grid=(depth, rs // tr), out_shape=[jax.ShapeDtypeStruct(w.shape, F32)] * 4,
        in_specs=[pl.BlockSpec((N_DEV, tr, width), lambda l, i: (0, (l * stride + base) // tr + i, 0)), nat, nat, nat],
        out_specs=[nat] * 4, compiler_params=_params(("parallel", "parallel")))(landed, w, m, v)


def _adamw_conv(landed, base, stride, w, m, v):
    depth, taps, cs = w.shape

    def body(p_ref, w_ref, m_ref, v_ref, *o_refs):
        g = _sum_blocks(p_ref)[:taps, :cs]
        for o, val in zip(o_refs, _adamw_math(g, w_ref[...], m_ref[...], v_ref[...]), strict=True):
            o[...] = val

    nat = pl.BlockSpec((None, taps, cs), lambda l: (l, 0, 0))
    return pl.pallas_call(
        body, name="adamw_conv", grid=(depth,), out_shape=[jax.ShapeDtypeStruct(w.shape, F32)] * 4,
        in_specs=[pl.BlockSpec((N_DEV, CONV_HALO, LANES), lambda l: (0, (l * stride + base) // CONV_HALO, 0)), nat, nat, nat],
        out_specs=[nat] * 4, compiler_params=_params(("parallel",)))(landed, w, m, v)


def _unshard_cols(name, gathered, start, rows, cs, shift=0):
    n_slab = -(-cs // LANES)
    total = N_DEV * cs
    tr = _pick(rows, (SLAB_TILE,))

    def body(*refs):
        slabs, o_ref = refs[:n_slab], refs[n_slab]
        for j in range(N_DEV):
            for k, s in enumerate(slabs):
                for src, dst, width in _wrapped(j * cs + LANES * k - shift, min(LANES, cs - LANES * k), total):
                    o_ref[:, dst:dst + width] = s[j, :, src:src + width]

    slab = lambda k: pl.BlockSpec((N_DEV, tr, LANES), lambda i: (0, (start + k * rows) // tr + i, 0))
    return pl.pallas_call(
        body, name=name, grid=(rows // tr,), out_shape=jax.ShapeDtypeStruct((rows, total), gathered.dtype),
        in_specs=[slab(k) for k in range(n_slab)], out_specs=pl.BlockSpec((tr, total), lambda i: (i, 0)),
        compiler_params=_params(("parallel",)))(*[gathered] * n_slab)


def _shard_cols(name, full, cs, shift, buf, start):
    rows, total = full.shape
    n_slab = -(-cs // LANES)
    tr = _pick(rows, (SLAB_TILE,))

    def body(f_ref, _, o_ref):
        for k in range(n_slab):
            @pl.when(pl.program_id(1) == k)
            def _(k=k):
                used = min(LANES, cs - LANES * k)
                for j in range(N_DEV):
                    for src, dst, width in _wrapped(j * cs + LANES * k - shift, used, total):
                        o_ref[j, :, src:src + width] = f_ref[:, dst:dst + width].astype(o_ref.dtype)
                    if used < LANES:
                        o_ref[j, :, used:] = jnp.zeros((tr, LANES - used), o_ref.dtype)

    return pl.pallas_call(
        body, name=name, grid=(rows // tr, n_slab), out_shape=jax.ShapeDtypeStruct(buf.shape, buf.dtype),
        in_specs=[pl.BlockSpec((tr, total), lambda i, k: (i, 0)), _ANY],
        out_specs=pl.BlockSpec((N_DEV, tr, LANES), lambda i, k: (0, (start + k * rows) // tr + i, 0)),
        input_output_aliases={1: 0}, compiler_params=_params(("parallel", "arbitrary")))(full, buf)


def _pack_cols(name, shards, buf, start):
    depth, rows, cs = shards.shape
    n_slab = -(-cs // LANES)
    tr = _pick(rows, (SLAB_TILE,))

    def body(x_ref, _, o_ref):
        for k in range(n_slab):
            @pl.when(pl.program_id(2) == k)
            def _(k=k):
                used = min(LANES, cs - LANES * k)
                o_ref[:, :used] = x_ref[:, LANES * k:LANES * k + used].astype(o_ref.dtype)
                if used < LANES:
                    o_ref[:, used:] = jnp.zeros((tr, LANES - used), o_ref.dtype)

    return pl.pallas_call(
        body, name=name, grid=(depth, rows // tr, n_slab), out_shape=jax.ShapeDtypeStruct(buf.shape, buf.dtype),
        in_specs=[pl.BlockSpec((None, tr, cs), lambda l, i, k: (l, i, 0)), _ANY],
        out_specs=pl.BlockSpec((None, tr, LANES), lambda l, i, k: (l, (start + k * rows) // tr + i, 0)),
        input_output_aliases={1: 0}, compiler_params=_params(("parallel", "parallel", "arbitrary")))(shards, buf)


def _wrapped(pos, width, total):
    pos %= total
    if pos + width <= total:
        return [(0, pos, width)]
    head = total - pos
    return [(0, pos, head), (head, 0, width - head)]


CONV_W_PIECES = 3


def _pad_to(n, align):
    return -(-n // align) * align


def _layout(group):
    col_names, row_names, with_conv = GROUPS[group]
    dims = {name: (rows, cols) for name, rows, cols, _ in SHARDED}
    col, off = {}, 0
    for name in col_names:
        rows, cols = dims[name]
        cs = cols // N_DEV
        col[name] = (off, rows, cs)
        off += -(-cs // LANES) * rows
    conv_base = off
    col_rows = _pad_to(off + with_conv * CONV_W_PIECES * CONV_HALO, SLAB_TILE)
    row, off = {}, 0
    for name in row_names:
        rs = dims[name][0] // N_DEV
        row[name] = (off, rs)
        off += _pad_to(rs, LANES)
    return col, conv_base, col_rows, row, off


def _concat_padded(pieces, total, axis):
    used = sum(p.shape[axis] for p in pieces)
    if total > used:
        shape = list(pieces[0].shape)
        shape[axis] = total - used
        pieces = pieces + [jnp.zeros(shape, pieces[0].dtype)]
    return jnp.concatenate(pieces, axis=axis)


def _split3(a):
    hi = a.astype(BF16)
    r1 = a - hi.astype(F32)
    mid = r1.astype(BF16)
    return hi, mid, (r1 - mid.astype(F32)).astype(BF16)


def _pack_small(arrs, lead=()):
    flat = jnp.concatenate([a.reshape(lead + (-1,)) for a in arrs], axis=-1)
    total = _pad_to(flat.shape[-1], 512 * LANES)
    flat = jnp.pad(flat, [(0, 0)] * len(lead) + [(0, total - flat.shape[-1])])
    return flat.reshape(lead + (total // LANES, LANES))


def _unpack_small(flat, shapes):
    flat = flat.reshape(-1)
    res, off = [], 0
    for s in shapes:
        n = int(np.prod(s))
        res.append(flat[off:off + n].reshape(s))
        off += n
    return res


def _small_rows(a, depth):
    n16 = depth * SSM_GROUPS
    a_re, a_im, f_re, f_im = _ssm_coeffs(a["ssm_lambda_re"].reshape(n16, SSM_STATE), a["ssm_lambda_im"].reshape(n16, SSM_STATE),
                                         a["ssm_log_dt"].reshape(n16, 1))
    rows = []
    for l in range(depth):
        sp = {k: _row(a[k][l]) for k in ("mix_norm_g", "b_gate", "attn_sinks", "ssm_d", "b_ssm_glu", "conv_dw_b",
                                         "conv_norm_g", "conv_norm_b", "ffn_norm_g", "ple_norm_g")}
        g = slice(l * SSM_GROUPS, (l + 1) * SSM_GROUPS)
        sp["a_row"] = jnp.concatenate([a_re[g].reshape(1, -1), a_im[g].reshape(1, -1)], axis=1)
        sp["f_row"] = jnp.concatenate([f_re[g].reshape(1, -1), f_im[g].reshape(1, -1)], axis=1)
        sp["bmat"], sp["cmat"] = _ssm_mats(a["ssm_b_re"][l], a["ssm_b_im"][l], a["ssm_c_re"][l], a["ssm_c_im"][l])
        rows.append(sp)
    return rows


def _local_step(a, get_weights, on_grads, depth):
    n_seq, seq, d = a["x"].shape
    t = n_seq * seq
    inv = ROPE_THETA ** (-jnp.arange(0, ROPE_DIM, 2, dtype=F32) / ROPE_DIM)
    lane = np.arange(LANES) % HEAD_DIM
    inv_lane = jnp.where(lane < ROPE_DIM, jnp.tile(inv, LANES // (ROPE_DIM // 2)), 0.0).reshape(1, LANES)
    ctab, stab = _rope_tables(a["positions"].reshape(t), inv_lane)
    small = _small_rows(a, depth)

    x = a["x"].reshape(t, d)
    saved, weights = [], []
    for l in range(depth):
        x, sv, w = _layer_fwd(x, a["p"][l].reshape(t, -1), get_weights(l, "mix", x),
                              functools.partial(get_weights, l, "ffn"), small[l], ctab, stab, n_seq)
        saved.append(sv)
        weights.append(w)
    loss, dx, d_final = _loss_and_grad(x, a["loss_target"].reshape(t, d), _row(a["final_norm_g"]))
    gws, gss = [None] * depth, [None] * depth
    for l in reversed(range(depth)):
        dx, gws[l], gss[l] = _layer_bwd(dx, saved[l], weights[l], small[l], functools.partial(on_grads, l), ctab, stab, n_seq)

    n16 = depth * SSM_GROUPS
    halves = lambda k, h: jnp.concatenate([gss[l][k][:, h * SSM_LANES:(h + 1) * SSM_LANES].reshape(SSM_GROUPS, SSM_STATE)
                                           for l in range(depth)], axis=0)
    dlr, dli, ddt = _ssm_coeffs_bwd(a["ssm_lambda_re"].reshape(n16, SSM_STATE), a["ssm_lambda_im"].reshape(n16, SSM_STATE),
                                    a["ssm_log_dt"].reshape(n16, 1),
                                    (halves("a_row", 0), halves("a_row", 1), halves("f_row", 0), halves("f_row", 1)))
    bc = [_ssm_mats_t(gss[l]["bmat"], gss[l]["cmat"]) for l in range(depth)]
    gsmall = {k: jnp.stack([gss[l][k].reshape(a[k].shape[1:]) for l in range(depth)])
              for k in ("mix_norm_g", "b_gate", "attn_sinks", "ssm_d", "b_ssm_glu", "conv_dw_b", "conv_norm_g", "conv_norm_b",
                        "ffn_norm_g", "ple_norm_g")}
    gsmall["ssm_lambda_re"] = dlr.reshape(a["ssm_lambda_re"].shape)
    gsmall["ssm_lambda_im"] = dli.reshape(a["ssm_lambda_im"].shape)
    gsmall["ssm_log_dt"] = ddt.reshape(a["ssm_log_dt"].shape)
    for i, k in enumerate(("ssm_b_re", "ssm_b_im", "ssm_c_re", "ssm_c_im")):
        gsmall[k] = jnp.stack([bc[l][i] for l in range(depth)])
    gsmall["final_norm_g"] = d_final.reshape(a["final_norm_g"].shape)
    return loss, dx.reshape(n_seq, seq, d), gws, gsmall


def kernel(x, p, positions, mix_norm_g, w_in, b_gate, attn_sinks, w_attn_out, ssm_lambda_re, ssm_lambda_im, ssm_log_dt, ssm_b_re, ssm_b_im, ssm_c_re, ssm_c_im, ssm_d, w_ssm_glu, b_ssm_glu, conv_dw_w, conv_dw_b, conv_norm_g, conv_norm_b, w_conv_out, w_mix_out, ffn_norm_g, w_ffn_in, w_ffn_out, w_ple_in, ple_norm_g, w_ple_gate, final_norm_g, loss_target, m_mix_norm_g, m_w_in, m_b_gate, m_attn_sinks, m_w_attn_out, m_ssm_lambda_re, m_ssm_lambda_im, m_ssm_log_dt, m_ssm_b_re, m_ssm_b_im, m_ssm_c_re, m_ssm_c_im, m_ssm_d, m_w_ssm_glu, m_b_ssm_glu, m_conv_dw_w, m_conv_dw_b, m_conv_norm_g, m_conv_norm_b, m_w_conv_out, m_w_mix_out, m_ffn_norm_g, m_w_ffn_in, m_w_ffn_out, m_w_ple_in, m_ple_norm_g, m_w_ple_gate, m_final_norm_g, v_mix_norm_g, v_w_in, v_b_gate, v_attn_sinks, v_w_attn_out, v_ssm_lambda_re, v_ssm_lambda_im, v_ssm_log_dt, v_ssm_b_re, v_ssm_b_im, v_ssm_c_re, v_ssm_c_im, v_ssm_d, v_w_ssm_glu, v_b_ssm_glu, v_conv_dw_w, v_conv_dw_b, v_conv_norm_g, v_conv_norm_b, v_w_conv_out, v_w_mix_out, v_ffn_norm_g, v_w_ffn_in, v_w_ffn_out, v_w_ple_in, v_ple_norm_g, v_w_ple_gate, v_final_norm_g):
    a = dict(locals())
    depth = w_in.shape[0]
    layouts = {group: _layout(group) for group in GROUPS}
    shift = {"w_in": Z_SPLIT}
    conv_pad = ((0, 0), (0, CONV_HALO - CONV_K), (0, LANES - CONV_WIDTH // N_DEV))

    def packed_weights(group):
        col, conv_base, col_rows, row, _ = layouts[group]
        slabs = lax.empty((depth, col_rows, LANES), BF16)
        for name, (base, _, _) in col.items():
            slabs = _pack_cols("pack_" + name, a[name], slabs, base)
        if GROUPS[group][2]:
            conv = jnp.stack(_split3(a["conv_dw_w"]), axis=1)
            conv = jnp.pad(conv, ((0, 0),) + conv_pad).reshape(depth, CONV_W_PIECES * CONV_HALO, LANES)
            slabs = slabs.at[:, conv_base:conv_base + CONV_W_PIECES * CONV_HALO].set(conv)
        regions = [_concat_padded([a[name].astype(BF16)], _pad_to(rs, LANES), 1) for name, (_, rs) in row.items()]
        return slabs, jnp.concatenate(regions, axis=1)

    packed = {group: packed_weights(group) for group in GROUPS}
    gathers, tokens = [], []
    for l in range(depth):
        started, token = _exchange_start(f"gather_start_{l}", "gather", [[buf[l] for buf in packed[group]] for group in GROUPS])
        gathers.append(dict(zip(GROUPS, started, strict=True)))
        tokens.append(token[0:1, 0:1])

    def get_weights(l, group, after):
        col, conv_base, _, row, _ = layouts[group]
        slab8, row8 = _exchange_wait(f"gather_wait_{group}_{l}", gathers[l][group], after)
        w = {name: _unshard_cols("unshard_" + name, slab8, base, rows, cs, shift.get(name, 0))
             for name, (base, rows, cs) in col.items()}
        for name, (base, rs) in row.items():
            w[name] = row8[:, base:base + rs].reshape(N_DEV * rs, -1)
        if GROUPS[group][2]:
            conv = slab8[:, conv_base:conv_base + CONV_W_PIECES * CONV_HALO]
            conv = conv.reshape(N_DEV, CONV_W_PIECES, CONV_HALO, LANES)[:, :, :CONV_K, :CONV_WIDTH // N_DEV].astype(F32)
            w["conv_dw_w"] = jnp.sum(conv, axis=1).transpose(1, 0, 2).reshape(CONV_K, CONV_WIDTH)
        return w

    scatters = {}

    def on_grads(l, group, gw):
        col, conv_base, col_rows, row, _ = layouts[group]
        slabs = lax.empty((N_DEV, col_rows, LANES), BF16)
        for name, (base, _, cs) in col.items():
            slabs = _shard_cols("shard_" + name, gw[name], cs, shift.get(name, 0), slabs, base)
        if GROUPS[group][2]:
            conv = gw["conv_dw_w"].reshape(CONV_K, N_DEV, CONV_WIDTH // N_DEV).transpose(1, 0, 2).astype(BF16)
            slabs = slabs.at[:, conv_base:conv_base + CONV_HALO].set(jnp.pad(conv, conv_pad))
        regions = [_concat_padded([gw[name].astype(BF16).reshape(N_DEV, rs, -1)], _pad_to(rs, LANES), 1)
                   for name, (_, rs) in row.items()]
        (scatters[l, group],), token = _exchange_start(
            f"grads_start_{group}_{l}", "scatter", [[slabs, jnp.concatenate(regions, axis=1)]])
        return token

    local = dict(a, mix_norm_g=a["mix_norm_g"] + sum(tokens))
    loss, grad_x, _, gsmall = _local_step(local, get_weights, on_grads, depth)
    loss = lax.psum(loss, ("x", "y", "c"))

    shapes = [a[k].shape for k in REPLICATED]
    parts, = _all_gather("gather_small_grads", [_pack_small([gsmall[k] for k in REPLICATED])])
    small_state = [_pack_small([a[pre + k] for k in REPLICATED]) for pre in ("", "m_", "v_")]
    small_flat = _adamw_flat("adamw_replicated", parts, *small_state)
    small = [dict(zip(REPLICATED, _unpack_small(o, shapes), strict=True)) for o in small_flat]

    state = lambda name: (a[name], a["m_" + name], a["v_" + name])
    big, after = {}, small_flat[1]
    for group in ("ffn", "mix"):
        col, conv_base, col_rows, row, row_rows = layouts[group]
        landed = [_exchange_wait(f"grads_wait_{group}_{l}", scatters[l, group], after) for l in range(depth)]
        landed_slab = jnp.concatenate([ls for ls, _ in landed], axis=1)
        landed_row = jnp.concatenate([lr for _, lr in landed], axis=1)
        big.update({name: _adamw_cols("adamw_" + name, landed_slab, base, col_rows, *state(name)) for name, (base, _, _) in col.items()})
        big.update({name: _adamw_rows("adamw_" + name, landed_row, base, row_rows, *state(name)) for name, (base, _) in row.items()})
        if GROUPS[group][2]:
            big["conv_dw_w"] = _adamw_conv(landed_slab, conv_base, col_rows, *state("conv_dw_w"))
        after = big[next(iter(col))][1]

    def result(kind, name):
        if name in REPLICATED:
            return small[kind][name]
        return big[name][kind]

    return (loss, grad_x, *[result(kind, n) for kind in range(4) for n in WEIGHT_ORDER])
```

```python
import functools
import math

import numpy as np
import jax
import jax.numpy as jnp
from jax import lax
from jax.experimental import pallas as pl
from jax.experimental.pallas import tpu as pltpu

F32 = jnp.float32
BF16 = jnp.bfloat16
MXU_DTYPE = jnp.bfloat16
VMEM_LIMIT_BYTES = 56 * 2 ** 20
N_DEV = 8
LANES = 128

HEAD_DIM = 64
N_Q_HEADS = 8
N_KV_HEADS = 2
GQA_GROUP = 4
BLOCK = 128
ROPE_THETA = 500000.0
ROPE_DIM = 16
Q_WIDTH = 512
KV_WIDTH = 128
SSM_WIDTH = 256
SSM_GROUP = 16
SSM_GROUPS = 16
SSM_STATE = 64
SSM_LANES = SSM_GROUPS * SSM_STATE
CONV_WIDTH = 256
CONV_K = 31
CONV_HALO = 32
EPS = 1e-6
NEG_INF = -1e30
ADAM_LR, ADAM_B1, ADAM_B2, ADAM_EPS, ADAM_WD, ADAM_STEP = 0.001, 0.9, 0.999, 1e-08, 0.01, 10

ZG_W, ZQ_W, ZKV_W, ZS_W, ZC_W = 3072, 512, 256, 256, 512
ZQ_BLK, ZKV_BLK, ZS_BLK, ZC_BLK = 3072 // 512, 3584 // 256, 3840 // 256, 4096 // 512
Z_WIDTH = 4608
Z_SPLIT = 1536

SHARDED = (("w_in", 1024, 4608, 1), ("w_attn_out", 512, 1024, 1), ("w_ssm_glu", 256, 2048, 1),
           ("conv_dw_w", 31, 256, 1), ("w_conv_out", 256, 1024, 1), ("w_mix_out", 1024, 1024, 0),
           ("w_ffn_in", 1024, 5632, 1), ("w_ffn_out", 2816, 1024, 0), ("w_ple_in", 256, 1024, 1),
           ("w_ple_gate", 1024, 1024, 0))
GROUPS = {"mix": (("w_in", "w_attn_out", "w_ssm_glu", "w_conv_out"), ("w_mix_out",), True),
          "ffn": (("w_ffn_in", "w_ple_in"), ("w_ffn_out", "w_ple_gate"), False)}
SLAB_TILE = 256
FLAT_ROW_ALIGN = 1024
REPLICATED = ("mix_norm_g", "b_gate", "attn_sinks", "ssm_lambda_re", "ssm_lambda_im", "ssm_log_dt", "ssm_b_re",
              "ssm_b_im", "ssm_c_re", "ssm_c_im", "ssm_d", "b_ssm_glu", "conv_dw_b", "conv_norm_g", "conv_norm_b",
              "ffn_norm_g", "ple_norm_g", "final_norm_g")
WEIGHT_ORDER = ("mix_norm_g", "w_in", "b_gate", "attn_sinks", "w_attn_out", "ssm_lambda_re", "ssm_lambda_im",
                "ssm_log_dt", "ssm_b_re", "ssm_b_im", "ssm_c_re", "ssm_c_im", "ssm_d", "w_ssm_glu", "b_ssm_glu",
                "conv_dw_w", "conv_dw_b", "conv_norm_g", "conv_norm_b", "w_conv_out", "w_mix_out", "ffn_norm_g",
                "w_ffn_in", "w_ffn_out", "w_ple_in", "ple_norm_g", "w_ple_gate", "final_norm_g")


_ANY = pl.BlockSpec(memory_space=pl.ANY)


def _params(sem=None):
    return pltpu.CompilerParams(dimension_semantics=sem, vmem_limit_bytes=VMEM_LIMIT_BYTES)


def _pick(n, cands):
    for c in cands:
        if n % c == 0:
            return c
    return n


def _dot(a, b, dims):
    return lax.dot_general(a.astype(MXU_DTYPE), b.astype(MXU_DTYPE), (dims, ((), ())), preferred_element_type=F32)


def _dot_nn(a, b):
    return _dot(a, b, ((1,), (0,)))


def _dot_nt(a, b):
    return _dot(a, b, ((1,), (1,)))


def _dot_tn(a, b):
    return _dot(a, b, ((0,), (0,)))


@jax.custom_vjp
def _mm(x, w):
    return _dot_nn(x, w)


def _mm_f(x, w):
    return _dot_nn(x, w), (x, w)


def _mm_b(res, dy):
    x, w = res
    return _dot_nt(dy, w).astype(x.dtype), _dot_tn(x, dy).astype(w.dtype)


_mm.defvjp(_mm_f, _mm_b)


def _rms(x, g):
    return x * lax.rsqrt(jnp.mean(x * x, axis=-1, keepdims=True) + EPS) * g


ROW_TILES = (1024, 512, 256, 128)
COL_TILES = (1536, 1408, 1024, 512, 256, 128)


def _matmul_nn(name, a, b, out_dtype):
    t, k = a.shape
    n = b.shape[1]
    tm, tn = _pick(t, ROW_TILES), _pick(n, COL_TILES)

    def body(a_ref, b_ref, o_ref):
        o_ref[...] = _dot_nn(a_ref[...], b_ref[...]).astype(o_ref.dtype)

    return pl.pallas_call(
        body, name=name, grid=(t // tm, n // tn), out_shape=jax.ShapeDtypeStruct((t, n), out_dtype),
        in_specs=[pl.BlockSpec((tm, k), lambda i, j: (i, 0)), pl.BlockSpec((k, tn), lambda i, j: (0, j))],
        out_specs=pl.BlockSpec((tm, tn), lambda i, j: (i, j)),
        compiler_params=_params(("parallel", "parallel")))(a, b)


def _matmul_tn(name, a, b):
    t, m = a.shape
    n = b.shape[1]
    tm, tn, tt = _pick(m, COL_TILES[1:]), _pick(n, COL_TILES), _pick(t, ROW_TILES)

    def body(a_ref, b_ref, o_ref):
        @pl.when(pl.program_id(2) == 0)
        def _():
            o_ref[...] = jnp.zeros_like(o_ref)

        o_ref[...] += _dot_tn(a_ref[...], b_ref[...])

    return pl.pallas_call(
        body, name=name, grid=(m // tm, n // tn, t // tt), out_shape=jax.ShapeDtypeStruct((m, n), F32),
        in_specs=[pl.BlockSpec((tt, tm), lambda i, j, s: (s, i)), pl.BlockSpec((tt, tn), lambda i, j, s: (s, j))],
        out_specs=pl.BlockSpec((tm, tn), lambda i, j, s: (i, j)),
        compiler_params=_params(("parallel", "parallel", "arbitrary")))(a, b)


def _two_parts(n):
    cut = n // (2 * LANES) * LANES
    return [slice(0, n)] if cut == 0 else [slice(0, cut), slice(cut, n)]


def _in_proj(x, g, w):
    t, d = x.shape
    n = w.shape[1]
    tm, tn = _pick(t, ROW_TILES), _pick(n, COL_TILES)

    def body(x_ref, g_ref, w_ref, z_ref, h_ref):
        h = _rms(x_ref[...], g_ref[...]).astype(h_ref.dtype)

        @pl.when(pl.program_id(1) == 0)
        def _():
            h_ref[...] = h

        z_ref[...] = _dot_nn(h, w_ref[...])

    return pl.pallas_call(
        body, name="in_proj", grid=(t // tm, n // tn),
        out_shape=[jax.ShapeDtypeStruct((t, n), F32), jax.ShapeDtypeStruct((t, d), MXU_DTYPE)],
        in_specs=[pl.BlockSpec((tm, d), lambda i, j: (i, 0)), pl.BlockSpec((1, d), lambda i, j: (0, 0)),
                  pl.BlockSpec((d, tn), lambda i, j: (0, j))],
        out_specs=[pl.BlockSpec((tm, tn), lambda i, j: (i, j)), pl.BlockSpec((tm, d), lambda i, j: (i, 0))],
        compiler_params=_params(("parallel", "arbitrary")))(x, g, w)


def _in_proj_bwd(dz, w, x, dx_res, g):
    t, d = x.shape
    tm = _pick(t, ROW_TILES[1:])

    def body(dz_ref, w_ref, x_ref, r_ref, g_ref, dx_ref, dg_ref):
        _, vjp = jax.vjp(_norm_in_tile, x_ref[...], g_ref[...])
        dx, dg = vjp(_dot_nt(dz_ref[...], w_ref[...]))
        dx_ref[...] = dx + r_ref[...]

        @pl.when(pl.program_id(0) == 0)
        def _():
            dg_ref[...] = jnp.zeros_like(dg_ref)

        dg_ref[...] += dg

    rows = lambda width: pl.BlockSpec((tm, width), lambda i: (i, 0))
    whole = lambda a: pl.BlockSpec(a.shape, lambda i: (0, 0))
    return pl.pallas_call(
        body, name="in_proj_bwd", grid=(t // tm,), out_shape=[jax.ShapeDtypeStruct((t, d), F32), jax.ShapeDtypeStruct((1, d), F32)],
        in_specs=[rows(dz.shape[1]), whole(w), rows(d), rows(d), whole(g)], out_specs=[rows(d), whole(g)],
        compiler_params=_params(("arbitrary",)))(dz, w, x, dx_res, g)


def _ffn_in_act(hf, w_fi):
    t, k = hf.shape
    f = w_fi.shape[1] // 2
    tm, tf = _pick(t, ROW_TILES[1:]), _pick(f, COL_TILES)
    nf = f // tf

    def body(a_ref, wg_ref, wu_ref, g_ref, u_ref, act_ref):
        a = a_ref[...]
        for cols in _two_parts(tf):
            g, u = _dot_nn(a, wg_ref[:, cols]), _dot_nn(a, wu_ref[:, cols])
            g_ref[:, cols] = g.astype(g_ref.dtype)
            u_ref[:, cols] = u.astype(u_ref.dtype)
            act_ref[:, cols] = (jax.nn.silu(g) * u).astype(act_ref.dtype)

    out = pl.BlockSpec((tm, tf), lambda i, j: (i, j))
    return pl.pallas_call(
        body, name="ffn_in_act", grid=(t // tm, nf), out_shape=[jax.ShapeDtypeStruct((t, f), MXU_DTYPE)] * 3,
        in_specs=[pl.BlockSpec((tm, k), lambda i, j: (i, 0)), pl.BlockSpec((k, tf), lambda i, j: (0, j)),
                  pl.BlockSpec((k, tf), lambda i, j: (0, j + nf))],
        out_specs=[out, out, out], compiler_params=_params(("parallel", "parallel")))(hf, w_fi, w_fi)


def _ffn_mid_bwd(dffn, w_fo, gate, up):
    t, d = dffn.shape
    f = w_fo.shape[0]
    tm, tf = _pick(t, ROW_TILES[1:]), _pick(f, COL_TILES)

    def body(a_ref, w_ref, g_ref, u_ref, dg_ref, du_ref):
        a = a_ref[...]
        for cols in _two_parts(tf):
            dact = _dot_nt(a, w_ref[cols, :])
            g, u = g_ref[:, cols].astype(F32), u_ref[:, cols].astype(F32)
            sg = jax.nn.sigmoid(g)
            dg_ref[:, cols] = (dact * u * sg * (1.0 + g * (1.0 - sg))).astype(dg_ref.dtype)
            du_ref[:, cols] = (dact * g * sg).astype(du_ref.dtype)

    blk = pl.BlockSpec((tm, tf), lambda i, j: (i, j))
    return pl.pallas_call(
        body, name="ffn_mid_bwd", grid=(t // tm, f // tf), out_shape=[jax.ShapeDtypeStruct((t, f), MXU_DTYPE)] * 2,
        in_specs=[pl.BlockSpec((tm, d), lambda i, j: (i, 0)), pl.BlockSpec((tf, d), lambda i, j: (j, 0)), blk, blk],
        out_specs=[blk, blk], compiler_params=_params(("parallel", "parallel")))(dffn, w_fo, gate, up)


def _ffn_in_dx(dgate, dup, w_fi):
    t, f = dgate.shape
    d = w_fi.shape[0]
    tm = _pick(t, ROW_TILES[1:])

    def body(g_ref, u_ref, w_ref, o_ref):
        o_ref[...] = _dot_nt(g_ref[...], w_ref[:, :f]) + _dot_nt(u_ref[...], w_ref[:, f:])

    blk = pl.BlockSpec((tm, f), lambda i: (i, 0))
    return pl.pallas_call(
        body, name="ffn_in_dx", grid=(t // tm,), out_shape=jax.ShapeDtypeStruct((t, d), F32),
        in_specs=[blk, blk, pl.BlockSpec(w_fi.shape, lambda i: (0, 0))], out_specs=pl.BlockSpec((tm, d), lambda i: (i, 0)),
        compiler_params=_params(("parallel",)))(dgate, dup, w_fi)


def _token_call(name, fn, tile, tok_ins, consts, tok_outs, acc_outs, into=None):
    n_rows = tok_ins[0][0].shape[0]
    tile = min(tile, n_rows)
    n_ti, n_c = len(tok_ins), len(consts)
    n_in = n_ti + n_c + (into is not None)
    n_to = len(tok_outs) + (into is not None)

    def body(*refs):
        ins = [r[...] for r in refs[:n_ti + n_c]]
        outs, accs = fn(*ins)
        for r, v in zip(refs[n_in:n_in + n_to], outs, strict=True):
            r[...] = v.astype(r.dtype)
        first = pl.program_id(0) == 0
        for r, v in zip(refs[n_in + n_to:], accs, strict=True):
            @pl.when(first)
            def _(r=r):
                r[...] = jnp.zeros_like(r)

            r[...] += jnp.broadcast_to(v, r.shape).astype(F32)

    in_specs = [pl.BlockSpec((tile, w), functools.partial(lambda i, c: (i, c), c=cb)) for _, w, cb in tok_ins]
    in_specs += [pl.BlockSpec(c.shape, lambda i: (0, 0)) for c in consts]
    out_shape = [jax.ShapeDtypeStruct((n_rows, w), dt) for w, dt in tok_outs]
    out_specs = [pl.BlockSpec((tile, w), lambda i: (i, 0)) for w, _ in tok_outs]
    operands = [a for a, _, _ in tok_ins] + list(consts)
    aliases = {}
    if into is not None:
        target, width, col_block = into
        in_specs.append(_ANY)
        operands.append(target)
        out_shape.append(jax.ShapeDtypeStruct(target.shape, target.dtype))
        out_specs.append(pl.BlockSpec((tile, width), lambda i: (i, col_block)))
        aliases = {n_in - 1: n_to - 1}
    out_shape += [jax.ShapeDtypeStruct(s, F32) for s in acc_outs]
    out_specs += [pl.BlockSpec(s, lambda i: (0, 0)) for s in acc_outs]
    res = pl.pallas_call(
        body, name=name, grid=(n_rows // tile,), out_shape=out_shape, in_specs=in_specs, out_specs=out_specs,
        input_output_aliases=aliases, compiler_params=_params(("arbitrary",)))(*operands)
    return res[:n_to], res[n_to:]


def _whole(a):
    return (a, a.shape[1], 0)


def _norm_in_tile(x, g):
    return _rms(x, g)


def _conv_post_tile(v, g, b):
    mu = jnp.mean(v, axis=-1, keepdims=True)
    var = jnp.mean(jnp.square(v - mu), axis=-1, keepdims=True)
    return jax.nn.silu((v - mu) * lax.rsqrt(var + EPS) * g + b)


def _branches_tile(ya, ys, v, gin, w_ao, w_sg, b_sg, ln_g, ln_b, w_co, b_gate):
    d = w_ao.shape[1]
    y_attn = _mm(ya, w_ao)
    pre = _mm(jax.nn.gelu(ys), w_sg) + b_sg
    y_ssm = pre[:, :d] * jax.nn.sigmoid(pre[:, d:])
    y_conv = _mm(_conv_post_tile(v, ln_g, ln_b), w_co)
    gates = jax.nn.sigmoid(gin + b_gate)
    return gates[:, :d] * y_attn + gates[:, d:2 * d] * y_ssm + gates[:, 2 * d:] * y_conv


def _merge_tile(x, ya, ys, v, gin, w_ao, w_sg, b_sg, ln_g, ln_b, w_co, b_gate, w_mo, g_ffn):
    x1 = x + _mm(_branches_tile(ya, ys, v, gin, w_ao, w_sg, b_sg, ln_g, ln_b, w_co, b_gate), w_mo)
    return x1, _rms(x1, g_ffn)


def _merge_bwd_tile(x1, ya, ys, v, gin, dx1, dhf, w_ao, w_sg, b_sg, ln_g, ln_b, w_co, b_gate, w_mo, g_ffn):
    _, norm_vjp = jax.vjp(_rms, x1, g_ffn)
    dx1_norm, dg_ffn = norm_vjp(dhf)
    dx1 = dx1 + dx1_norm
    merged, branch_vjp = jax.vjp(_branches_tile, ya, ys, v, gin, w_ao, w_sg, b_sg, ln_g, ln_b, w_co, b_gate)
    grads = branch_vjp(_dot_nt(dx1, w_mo))
    return [dx1, *grads[:4]], [*grads[4:], _dot_tn(merged, dx1), dg_ffn]


def _ple_tile(x2, p, w_pi, g_ple, w_pg):
    pre, e = _mm(_rms(x2, g_ple), w_pg), _mm(p, w_pi)
    return x2 + jax.nn.sigmoid(pre) * e, pre, e


def _ple_bwd_tile(x2, p, pre, e, dx3, w_pi, g_ple, w_pg):
    sig = jax.nn.sigmoid(pre)
    dpre = dx3 * e * sig * (1.0 - sig)
    hn, norm_vjp = jax.vjp(_rms, x2, g_ple)
    dx2_norm, dg_ple = norm_vjp(_dot_nt(dpre, w_pg))
    return dx3 + dx2_norm, [_dot_tn(p, dx3 * sig), dg_ple, _dot_tn(hn, dpre)]


def _f32s(vals):
    return [v.astype(F32) for v in vals]


def _rope_tables(positions, inv_lane):
    def fn(pos, inv):
        ang = pos.astype(F32) * inv
        j = lax.broadcasted_iota(jnp.int32, ang.shape, 1) % HEAD_DIM
        c = jnp.where(j < ROPE_DIM, jnp.cos(ang), 1.0)
        s = jnp.sin(ang)
        s = jnp.where(j < ROPE_DIM // 2, -s, jnp.where(j < ROPE_DIM, s, 0.0))
        return [c, s], []

    (c, s), _ = _token_call("rope_tables", fn, 1024, [_whole(positions.reshape(-1, 1))], [inv_lane],
                            [(LANES, F32), (LANES, F32)], [])
    return c, s


def _swap_halves(t):
    n = t.shape[1]
    j = lax.broadcasted_iota(jnp.int32, t.shape, 1) % HEAD_DIM
    lower = pltpu.roll(t, n - ROPE_DIM // 2, 1)
    upper = jnp.where(j < ROPE_DIM, pltpu.roll(t, ROPE_DIM // 2, 1), 0.0)
    return jnp.where(j < ROPE_DIM // 2, lower, upper)


def _rope(t, c, s):
    return t * c + _swap_halves(t) * s


def _rope_t(dt, c, s):
    return dt * c + _swap_halves(dt * s)


def _tile4(a):
    return jnp.concatenate([a] * (Q_WIDTH // LANES), axis=1)


def _attn_mask(n):
    qi = lax.broadcasted_iota(jnp.int32, (GQA_GROUP * BLOCK, 2 * BLOCK), 0) % BLOCK
    kj = lax.broadcasted_iota(jnp.int32, (GQA_GROUP * BLOCK, 2 * BLOCK), 1)
    dist = qi + BLOCK - kj
    return (dist >= 0) & (dist < BLOCK) & ((n > 0) | (kj >= BLOCK))


def _attn_specs(n_seq):
    own = lambda w, blk: pl.BlockSpec((n_seq, BLOCK, w), lambda n: (0, n, blk))
    prev = lambda w, blk: pl.BlockSpec((n_seq, BLOCK, w), lambda n: (0, jnp.maximum(n - 1, 0), blk))
    return [own(ZQ_W, ZQ_BLK), own(ZKV_W, ZKV_BLK), prev(ZKV_W, ZKV_BLK), own(LANES, 0), own(LANES, 0), prev(LANES, 0),
            prev(LANES, 0), pl.BlockSpec((1, N_Q_HEADS), lambda n: (0, 0))]


def _by_seq(a, n_seq):
    return a.reshape(n_seq, a.shape[0] // n_seq, a.shape[1])


ATTN_SCALE = HEAD_DIM ** -0.5


def _stack_heads(t, kh):
    return jnp.concatenate([t[:, (kh * GQA_GROUP + g) * HEAD_DIM:(kh * GQA_GROUP + g + 1) * HEAD_DIM]
                            for g in range(GQA_GROUP)], axis=0)


def _stack_sinks(sink, kh):
    return jnp.concatenate([jnp.broadcast_to(sink[:, kh * GQA_GROUP + g:kh * GQA_GROUP + g + 1], (BLOCK, 1))
                            for g in range(GQA_GROUP)], axis=0)


def _attn_band(b, q_ref, kv_ref, kvp_ref, c_ref, s_ref, cp_ref, sp_ref):
    c, s = c_ref[b], s_ref[b]
    q = _rope(q_ref[b], _tile4(c), _tile4(s)) * ATTN_SCALE
    kv, kvp = kv_ref[b], kvp_ref[b]
    k = _rope(kv[:, :KV_WIDTH], c, s)
    kp = _rope(kvp[:, :KV_WIDTH], cp_ref[b], sp_ref[b])
    kb = jnp.concatenate([kp, k], axis=0)
    vb = jnp.concatenate([kvp[:, KV_WIDTH:], kv[:, KV_WIDTH:]], axis=0)
    return q, kb, vb


def _attention_fwd(z, ctab, stab, sinks, n_seq):
    t = z.shape[0]
    seq = t // n_seq

    def body(q_ref, kv_ref, kvp_ref, c_ref, s_ref, cp_ref, sp_ref, sink_ref, o_ref, lse_ref):
        mask = _attn_mask(pl.program_id(0))
        sink = sink_ref[...]
        lane = lax.broadcasted_iota(jnp.int32, (BLOCK, N_Q_HEADS), 1)
        for b in range(n_seq):
            q, kb, vb = _attn_band(b, q_ref, kv_ref, kvp_ref, c_ref, s_ref, cp_ref, sp_ref)
            lse_all = jnp.zeros((BLOCK, N_Q_HEADS), F32)
            for kh in range(N_KV_HEADS):
                sc = jnp.where(mask, _dot_nt(_stack_heads(q, kh), kb[:, kh * HEAD_DIM:(kh + 1) * HEAD_DIM]), NEG_INF)
                sk = _stack_sinks(sink, kh)
                m = jnp.maximum(jnp.max(sc, axis=-1, keepdims=True), sk)
                pr = jnp.exp(sc - m)
                den = jnp.sum(pr, axis=-1, keepdims=True) + jnp.exp(sk - m)
                out = _dot_nn(pr * (1.0 / den), vb[:, kh * HEAD_DIM:(kh + 1) * HEAD_DIM])
                lse = m + jnp.log(den)
                for g in range(GQA_GROUP):
                    h = kh * GQA_GROUP + g
                    o_ref[b, :, h * HEAD_DIM:(h + 1) * HEAD_DIM] = out[g * BLOCK:(g + 1) * BLOCK].astype(o_ref.dtype)
                    lse_all = jnp.where(lane == h, lse[g * BLOCK:(g + 1) * BLOCK], lse_all)
            lse_ref[b] = lse_all

    rows = lambda w: pl.BlockSpec((n_seq, BLOCK, w), lambda n: (0, n, 0))
    z3, c3, s3 = _by_seq(z, n_seq), _by_seq(ctab, n_seq), _by_seq(stab, n_seq)
    ya, lse = pl.pallas_call(
        body, name="attn_fwd", grid=(seq // BLOCK,),
        out_shape=[jax.ShapeDtypeStruct((n_seq, seq, Q_WIDTH), MXU_DTYPE), jax.ShapeDtypeStruct((n_seq, seq, N_Q_HEADS), F32)],
        in_specs=_attn_specs(n_seq), out_specs=[rows(Q_WIDTH), rows(N_Q_HEADS)],
        compiler_params=_params(("parallel",)))(z3, z3, z3, c3, s3, c3, s3, sinks)
    return ya.reshape(t, Q_WIDTH), lse.reshape(t, N_Q_HEADS)


def _attention_bwd(z, ctab, stab, sinks, ya, lse, dya, dz, n_seq):
    t = z.shape[0]
    seq = t // n_seq

    def body(q_ref, kv_ref, kvp_ref, c_ref, s_ref, cp_ref, sp_ref, sink_ref, o_ref, lse_ref, do_ref, _,
             dq_ref, dkv_ref, dkvp_ref, dsink_ref):
        mask = _attn_mask(pl.program_id(0))
        sink = sink_ref[...]
        lane = lax.broadcasted_iota(jnp.int32, (1, N_Q_HEADS), 1)
        dsink = jnp.zeros((1, N_Q_HEADS), F32)
        for b in range(n_seq):
            q, kb, vb = _attn_band(b, q_ref, kv_ref, kvp_ref, c_ref, s_ref, cp_ref, sp_ref)
            lse_all = lse_ref[b]
            o = o_ref[b].astype(F32)
            do = do_ref[b].astype(F32)
            dq_parts = []
            dk_parts, dv_parts = [], []
            for kh in range(N_KV_HEADS):
                kbh = kb[:, kh * HEAD_DIM:(kh + 1) * HEAD_DIM]
                vbh = vb[:, kh * HEAD_DIM:(kh + 1) * HEAD_DIM]
                qs, dos = _stack_heads(q, kh), _stack_heads(do, kh)
                lse = jnp.concatenate([lse_all[:, kh * GQA_GROUP + g:kh * GQA_GROUP + g + 1] for g in range(GQA_GROUP)], axis=0)
                pr = jnp.exp(jnp.where(mask, _dot_nt(qs, kbh), NEG_INF) - lse)
                delta = jnp.sum(dos * _stack_heads(o, kh), axis=-1, keepdims=True)
                ds = pr * (_dot_nt(dos, vbh) - delta)
                dqs = _dot_nn(ds, kbh)
                dq_parts += [dqs[g * BLOCK:(g + 1) * BLOCK] for g in range(GQA_GROUP)]
                dk_parts.append(_dot_tn(ds, qs))
                dv_parts.append(_dot_tn(pr, dos))
                dsk = jnp.exp(_stack_sinks(sink, kh) - lse) * delta
                for g in range(GQA_GROUP):
                    dsink = dsink + jnp.where(lane == kh * GQA_GROUP + g, -jnp.sum(dsk[g * BLOCK:(g + 1) * BLOCK]), 0.0)
            c, s = c_ref[b], s_ref[b]
            dq_ref[b] = _rope_t(jnp.concatenate(dq_parts, axis=1) * ATTN_SCALE, _tile4(c), _tile4(s)).astype(dq_ref.dtype)
            dk = jnp.concatenate(dk_parts, axis=1)
            dv = jnp.concatenate(dv_parts, axis=1)
            dkv_ref[b, :, :KV_WIDTH] = _rope_t(dk[BLOCK:], c, s)
            dkv_ref[b, :, KV_WIDTH:] = dv[BLOCK:]
            dkvp_ref[b, :, :KV_WIDTH] = _rope_t(dk[:BLOCK], cp_ref[b], sp_ref[b])
            dkvp_ref[b, :, KV_WIDTH:] = dv[:BLOCK]

        @pl.when(pl.program_id(0) == 0)
        def _():
            dsink_ref[...] = jnp.zeros_like(dsink_ref)

        dsink_ref[...] += dsink

    rows = lambda w: pl.BlockSpec((n_seq, BLOCK, w), lambda n: (0, n, 0))
    by_seq = lambda a: _by_seq(a, n_seq)
    z3, c3, s3 = by_seq(z), by_seq(ctab), by_seq(stab)
    dz, dkv, dkvp, dsink = pl.pallas_call(
        body, name="attn_bwd", grid=(seq // BLOCK,),
        out_shape=[jax.ShapeDtypeStruct((n_seq, seq, Z_WIDTH), dz.dtype), jax.ShapeDtypeStruct((n_seq, seq, ZKV_W), F32),
                   jax.ShapeDtypeStruct((n_seq, seq, ZKV_W), F32), jax.ShapeDtypeStruct((1, N_Q_HEADS), F32)],
        in_specs=_attn_specs(n_seq) + [rows(Q_WIDTH), rows(N_Q_HEADS), rows(Q_WIDTH), _ANY],
        out_specs=[pl.BlockSpec((n_seq, BLOCK, ZQ_W), lambda n: (0, n, ZQ_BLK)), rows(ZKV_W), rows(ZKV_W),
                   pl.BlockSpec((1, N_Q_HEADS), lambda n: (0, 0))],
        input_output_aliases={11: 0},
        compiler_params=_params(("arbitrary",)))(z3, z3, z3, c3, s3, c3, s3, sinks, by_seq(ya), by_seq(lse), by_seq(dya), by_seq(dz))
    return dz.reshape(t, Z_WIDTH), dkv.reshape(t, ZKV_W), dkvp.reshape(t, ZKV_W), dsink


def _kv_combine(dkv, dkvp, dz, n_seq):
    t = dkv.shape[0]
    seq = t // n_seq
    rows = _pick(seq, (512, 256, 128))
    nt, per = seq // rows, rows // BLOCK
    n_blocks = t // BLOCK

    def body(dkv_ref, dkvp_ref, dkvn_ref, _, o_ref):
        nxt = jnp.where(pl.program_id(1) == nt - 1, 0.0, dkvn_ref[...])
        shifted = nxt if per == 1 else jnp.concatenate([dkvp_ref[BLOCK:, :], nxt], axis=0)
        o_ref[...] = (dkv_ref[...] + shifted).astype(o_ref.dtype)

    tile = pl.BlockSpec((rows, ZKV_W), lambda b, i: (b * nt + i, 0))
    return pl.pallas_call(
        body, name="kv_combine", grid=(n_seq, nt), out_shape=jax.ShapeDtypeStruct(dz.shape, dz.dtype),
        in_specs=[tile, tile,
                  pl.BlockSpec((BLOCK, ZKV_W), lambda b, i: (jnp.minimum((b * nt + i + 1) * per, n_blocks - 1), 0)), _ANY],
        out_specs=pl.BlockSpec((rows, ZKV_W), lambda b, i: (b * nt + i, ZKV_BLK)), input_output_aliases={3: 0},
        compiler_params=_params(("parallel", "parallel")))(dkv, dkvp, dkvp, dz)


def _ssm_coeff_tile(lam_re, lam_im, log_dt):
    lr = jnp.minimum(lam_re, -1e-4)
    dt = jnp.exp(log_dt)
    mag = jnp.exp(lr * dt)
    a_re = mag * jnp.cos(lam_im * dt)
    a_im = mag * jnp.sin(lam_im * dt)
    den = lr * lr + lam_im * lam_im
    x_re = a_re - 1.0
    f_re = (x_re * lr + a_im * lam_im) / den
    f_im = (a_im * lr - x_re * lam_im) / den
    return a_re, a_im, f_re, f_im


def _ssm_coeffs(lam_re, lam_im, log_dt):
    def body(lr_ref, li_ref, dt_ref, *o_refs):
        for r, v in zip(o_refs, _ssm_coeff_tile(lr_ref[...], li_ref[...], dt_ref[...]), strict=True):
            r[...] = v

    return pl.pallas_call(body, name="ssm_coeffs", out_shape=[jax.ShapeDtypeStruct(lam_re.shape, F32)] * 4)(
        lam_re, lam_im, log_dt)


def _ssm_coeffs_bwd(lam_re, lam_im, log_dt, cts):
    def body(lr_ref, li_ref, dt_ref, c0, c1, c2, c3, dlr_ref, dli_ref, ddt_ref):
        _, vjp = jax.vjp(_ssm_coeff_tile, lr_ref[...], li_ref[...], dt_ref[...])
        dlr, dli, ddt = vjp((c0[...], c1[...], c2[...], c3[...]))
        dlr_ref[...] = dlr
        dli_ref[...] = dli
        ddt_ref[...] = ddt

    return pl.pallas_call(
        body, name="ssm_coeffs_bwd",
        out_shape=[jax.ShapeDtypeStruct(lam_re.shape, F32)] * 2 + [jax.ShapeDtypeStruct(log_dt.shape, F32)])(
        lam_re, lam_im, log_dt, *cts)


def _ssm_chunk(t):
    return _pick(t, (256, 128))


def _ssm_fwd(z, bmat, a_row, f_row, cmat, d_row, n_seq):
    t = z.shape[0]
    seq = t // n_seq
    lc = _ssm_chunk(seq)
    nc = seq // lc
    n2 = 2 * SSM_LANES

    re, im = pl.ds(0, SSM_LANES), pl.ds(SSM_LANES, SSM_LANES)

    def body(u_ref, b_ref, a_ref, f_ref, c_ref, d_ref, y_ref, s_ref, bu_ref, st_ref):
        @pl.when(pl.program_id(0) == 0)
        def _():
            st_ref[...] = jnp.zeros_like(st_ref)

        fr, fi = f_ref[:, :SSM_LANES], f_ref[:, SSM_LANES:]
        for b in range(n_seq):
            proj = _dot_nn(u_ref[b], b_ref[...])
            pr, pi = proj[:, :SSM_LANES], proj[:, SSM_LANES:]
            bu_ref[b, :, :SSM_LANES] = fr * pr - fi * pi
            bu_ref[b, :, SSM_LANES:] = fr * pi + fi * pr
        ar, ai = a_ref[:, :SSM_LANES], a_ref[:, SSM_LANES:]

        def step(i, carry):
            out = []
            for b in range(n_seq):
                sr, si = carry[2 * b], carry[2 * b + 1]
                nr = ar * sr - ai * si + bu_ref[b, pl.ds(i, 1), re]
                ni = ar * si + ai * sr + bu_ref[b, pl.ds(i, 1), im]
                s_ref[b, pl.ds(i, 1), re] = nr
                s_ref[b, pl.ds(i, 1), im] = ni
                out += [nr, ni]
            return tuple(out)

        carry = lax.fori_loop(0, lc, step, tuple(st_ref[b, 0:1, part] for b in range(n_seq) for part in (re, im)), unroll=8)
        for b in range(n_seq):
            st_ref[b, 0:1, re], st_ref[b, 0:1, im] = carry[2 * b], carry[2 * b + 1]
            y_ref[b] = _dot_nn(s_ref[b], c_ref[...]) + d_ref[...] * u_ref[b]

    const = lambda shape: pl.BlockSpec(shape, lambda c: (0, 0))
    rows = lambda w, cb: pl.BlockSpec((n_seq, lc, w), lambda c: (0, c, cb))
    ys, states = pl.pallas_call(
        body, name="ssm_fwd", grid=(nc,),
        out_shape=[jax.ShapeDtypeStruct((n_seq, seq, SSM_WIDTH), F32), jax.ShapeDtypeStruct((n_seq, seq, n2), F32)],
        in_specs=[rows(ZS_W, ZS_BLK), const((SSM_WIDTH, n2)), const((1, n2)), const((1, n2)), const((n2, SSM_WIDTH)),
                  const((1, SSM_WIDTH))],
        out_specs=[rows(SSM_WIDTH, 0), rows(n2, 0)],
        scratch_shapes=[pltpu.VMEM((n_seq, lc, n2), F32), pltpu.VMEM((n_seq, 8, n2), F32)],
        compiler_params=_params(("arbitrary",)))(_by_seq(z, n_seq), bmat, a_row, f_row, cmat, d_row)
    return ys.reshape(t, SSM_WIDTH), states.reshape(t, n2)


def _ssm_bwd(z, states, dy, bmat, a_row, f_row, cmat, d_row, dz, n_seq):
    t = z.shape[0]
    seq = t // n_seq
    lc = _ssm_chunk(seq)
    nc = seq // lc
    n2 = 2 * SSM_LANES

    def body(dy_ref, u_ref, s_ref, b_ref, a_ref, f_ref, c_ref, d_ref, _,
             du_ref, db_ref, dc_ref, da_ref, df_ref, dd_ref, g_ref, carry_ref):
        @pl.when(pl.program_id(0) == 0)
        def _():
            for r in (db_ref, dc_ref, da_ref, df_ref, dd_ref, carry_ref):
                r[...] = jnp.zeros_like(r)

        re, im = pl.ds(0, SSM_LANES), pl.ds(SSM_LANES, SSM_LANES)
        for b in range(n_seq):
            dy = dy_ref[b]
            g_ref[b, 0:lc, :] = _dot_nt(dy, c_ref[...])
            g_ref[b, lc:lc + 8, :] = carry_ref[b]
            dc_ref[...] += _dot_tn(s_ref[b], dy)
            dd_ref[...] += jnp.sum(dy * u_ref[b], axis=0, keepdims=True)
        ar, ai = a_ref[:, :SSM_LANES], a_ref[:, SSM_LANES:]

        def step(i, carry):
            r = lc - 1 - i
            out = []
            for b in range(n_seq):
                gr, gi = carry[2 * b], carry[2 * b + 1]
                nr = g_ref[b, pl.ds(r, 1), re] + ar * gr + ai * gi
                ni = g_ref[b, pl.ds(r, 1), im] - ai * gr + ar * gi
                g_ref[b, pl.ds(r, 1), re] = nr
                g_ref[b, pl.ds(r, 1), im] = ni
                out += [nr, ni]
            return tuple(out)

        carry = lax.fori_loop(0, lc, step, tuple(carry_ref[b, 0:1, part] for b in range(n_seq) for part in (re, im)), unroll=8)
        fr, fi = f_ref[:, :SSM_LANES], f_ref[:, SSM_LANES:]
        for b in range(n_seq):
            carry_ref[b, 0:1, re], carry_ref[b, 0:1, im] = carry[2 * b], carry[2 * b + 1]
            dy, u, st = dy_ref[b], u_ref[b], s_ref[b]
            sr, si = st[:, :SSM_LANES], st[:, SSM_LANES:]
            gnr, gni = g_ref[b, pl.ds(1, lc), re], g_ref[b, pl.ds(1, lc), im]
            da_ref[:, :SSM_LANES] += jnp.sum(gnr * sr + gni * si, axis=0, keepdims=True)
            da_ref[:, SSM_LANES:] += jnp.sum(gni * sr - gnr * si, axis=0, keepdims=True)
            gr_all, gi_all = g_ref[b, 0:lc, :SSM_LANES], g_ref[b, 0:lc, SSM_LANES:]
            proj = _dot_nn(u, b_ref[...])
            pr, pi = proj[:, :SSM_LANES], proj[:, SSM_LANES:]
            df_ref[:, :SSM_LANES] += jnp.sum(gr_all * pr + gi_all * pi, axis=0, keepdims=True)
            df_ref[:, SSM_LANES:] += jnp.sum(gi_all * pr - gr_all * pi, axis=0, keepdims=True)
            dproj = jnp.concatenate([fr * gr_all + fi * gi_all, fr * gi_all - fi * gr_all], axis=1).astype(MXU_DTYPE)
            du_ref[b] = (_dot_nt(dproj, b_ref[...]) + d_ref[...] * dy).astype(du_ref.dtype)
            db_ref[...] += _dot_tn(u, dproj)

    const = lambda shape: pl.BlockSpec(shape, lambda c: (0, 0))
    rows = lambda w, cb: pl.BlockSpec((n_seq, lc, w), lambda c: (0, nc - 1 - c, cb))
    by_seq = lambda a: _by_seq(a, n_seq)
    dz, *sums = pl.pallas_call(
        body, name="ssm_bwd", grid=(nc,),
        out_shape=[jax.ShapeDtypeStruct((n_seq, seq, Z_WIDTH), dz.dtype), jax.ShapeDtypeStruct((SSM_WIDTH, n2), F32),
                   jax.ShapeDtypeStruct((n2, SSM_WIDTH), F32), jax.ShapeDtypeStruct((1, n2), F32),
                   jax.ShapeDtypeStruct((1, n2), F32), jax.ShapeDtypeStruct((1, SSM_WIDTH), F32)],
        in_specs=[rows(SSM_WIDTH, 0), rows(ZS_W, ZS_BLK), rows(n2, 0), const((SSM_WIDTH, n2)), const((1, n2)),
                  const((1, n2)), const((n2, SSM_WIDTH)), const((1, SSM_WIDTH)), _ANY],
        out_specs=[rows(ZS_W, ZS_BLK), const((SSM_WIDTH, n2)), const((n2, SSM_WIDTH)), const((1, n2)), const((1, n2)),
                   const((1, SSM_WIDTH))],
        input_output_aliases={8: 0},
        scratch_shapes=[pltpu.VMEM((n_seq, lc + 8, n2), F32), pltpu.VMEM((n_seq, 8, n2), F32)],
        compiler_params=_params(("arbitrary",)))(by_seq(dy), by_seq(z), by_seq(states), bmat, a_row, f_row, cmat, d_row, by_seq(dz))
    return (dz.reshape(t, Z_WIDTH), *sums)


def _conv_chunk(t):
    return _pick(t, (512, 256, 128))


def _glu(c):
    return c[:, :CONV_WIDTH] * jax.nn.sigmoid(c[:, CONV_WIDTH:])


def _conv_specs(lc, nc):
    per = lc // CONV_HALO
    return [pl.BlockSpec((lc, ZC_W), lambda b, c: (b * nc + c, ZC_BLK)),
            pl.BlockSpec((CONV_HALO, ZC_W), lambda b, c: (jnp.maximum((b * nc + c) * per - 1, 0), ZC_BLK))]


def _conv_fill(c_ref, cp_ref, ue_ref, lc):
    ue_ref[0:CONV_HALO, :] = jnp.where(pl.program_id(1) > 0, _glu(cp_ref[...]), 0.0)
    ue_ref[CONV_HALO:CONV_HALO + lc, :] = _glu(c_ref[...])


def _conv_apply(ue_ref, w_ref, b_ref, lc):
    acc = jnp.zeros((lc, CONV_WIDTH), F32) + b_ref[...]
    for k in range(CONV_K):
        acc = acc + w_ref[k:k + 1, :] * ue_ref[pl.ds(k + CONV_HALO - CONV_K + 1, lc), :]
    return acc


def _conv_fwd(z, dw_w, dw_b, n_seq):
    t = z.shape[0]
    seq = t // n_seq
    lc = _conv_chunk(seq)
    nc = seq // lc

    def body(c_ref, cp_ref, w_ref, b_ref, o_ref, ue_ref):
        _conv_fill(c_ref, cp_ref, ue_ref, lc)
        o_ref[...] = _conv_apply(ue_ref, w_ref, b_ref, lc)

    const = lambda a: pl.BlockSpec(a.shape, lambda b, c: (0, 0))
    return pl.pallas_call(
        body, name="conv_fwd", grid=(n_seq, nc), out_shape=jax.ShapeDtypeStruct((t, CONV_WIDTH), F32),
        in_specs=_conv_specs(lc, nc) + [const(dw_w), const(dw_b)],
        out_specs=pl.BlockSpec((lc, CONV_WIDTH), lambda b, c: (b * nc + c, 0)),
        scratch_shapes=[pltpu.VMEM((CONV_HALO + lc, CONV_WIDTH), F32)],
        compiler_params=_params(("parallel", "parallel")))(z, z, dw_w, dw_b)


def _conv_bwd_taps(z, dv, dw_w, dz, n_seq):
    t = z.shape[0]
    seq = t // n_seq
    lc = _conv_chunk(seq)
    nc = seq // lc
    per = lc // CONV_HALO
    n_halo = t // CONV_HALO

    def body(c_ref, cp_ref, dv_ref, dvn_ref, w_ref, _, dc_ref, dw_ref, ue_ref, dve_ref):
        @pl.when((pl.program_id(0) == 0) & (pl.program_id(1) == 0))
        def _():
            dw_ref[...] = jnp.zeros_like(dw_ref)

        _conv_fill(c_ref, cp_ref, ue_ref, lc)
        dv = dv_ref[...]
        dve_ref[0:lc, :] = dv
        dve_ref[lc:lc + CONV_HALO, :] = jnp.where(pl.program_id(1) < nc - 1, dvn_ref[...], 0.0)
        dw_ref[CONV_K:CONV_K + 1, :] += jnp.sum(dv, axis=0, keepdims=True)
        du = jnp.zeros((lc, CONV_WIDTH), F32)
        for k in range(CONV_K):
            du = du + w_ref[k:k + 1, :] * dve_ref[pl.ds(CONV_K - 1 - k, lc), :]
            dw_ref[k:k + 1, :] += jnp.sum(dv * ue_ref[pl.ds(k + CONV_HALO - CONV_K + 1, lc), :], axis=0, keepdims=True)
        c = c_ref[...]
        a, sg = c[:, :CONV_WIDTH], jax.nn.sigmoid(c[:, CONV_WIDTH:])
        dc_ref[:, :CONV_WIDTH] = (du * sg).astype(dc_ref.dtype)
        dc_ref[:, CONV_WIDTH:] = (du * a * sg * (1.0 - sg)).astype(dc_ref.dtype)

    return pl.pallas_call(
        body, name="conv_bwd_taps", grid=(n_seq, nc),
        out_shape=[jax.ShapeDtypeStruct(dz.shape, dz.dtype), jax.ShapeDtypeStruct((CONV_HALO, CONV_WIDTH), F32)],
        in_specs=_conv_specs(lc, nc) + [
            pl.BlockSpec((lc, CONV_WIDTH), lambda b, c: (b * nc + c, 0)),
            pl.BlockSpec((CONV_HALO, CONV_WIDTH), lambda b, c: (jnp.minimum((b * nc + c + 1) * per, n_halo - 1), 0)),
            pl.BlockSpec(dw_w.shape, lambda b, c: (0, 0)), _ANY],
        out_specs=[pl.BlockSpec((lc, ZC_W), lambda b, c: (b * nc + c, ZC_BLK)),
                   pl.BlockSpec((CONV_HALO, CONV_WIDTH), lambda b, c: (0, 0))],
        input_output_aliases={5: 0},
        scratch_shapes=[pltpu.VMEM((CONV_HALO + lc, CONV_WIDTH), F32), pltpu.VMEM((lc + CONV_HALO, CONV_WIDTH), F32)],
        compiler_params=_params(("arbitrary", "arbitrary")))(z, z, dv, dv, dw_w, dz)


def _row(v):
    return v.reshape(1, -1)


def _ssm_mats(b_re, b_im, c_re, c_im):
    eye = jnp.eye(SSM_GROUPS, dtype=bool)
    bm = jnp.stack([b_re, b_im]).transpose(1, 3, 0, 2)[:, :, :, None, :]
    bmat = jnp.where(eye[:, None, None, :, None], bm, 0.0).reshape(SSM_WIDTH, 2 * SSM_LANES)
    cm = jnp.stack([c_re, -c_im]).transpose(0, 1, 3, 2)[:, :, :, None, :]
    cmat = jnp.where(eye[None, :, None, :, None], cm, 0.0).reshape(2 * SSM_LANES, SSM_WIDTH)
    return bmat.astype(MXU_DTYPE), cmat.astype(MXU_DTYPE)


def _ssm_mats_t(dbmat, dcmat):
    eye = jnp.eye(SSM_GROUPS, dtype=bool)
    db = dbmat.reshape(SSM_GROUPS, SSM_GROUP, 2, SSM_GROUPS, SSM_STATE)
    db = jnp.sum(jnp.where(eye[:, None, None, :, None], db, 0.0), axis=3).transpose(2, 0, 3, 1)
    dc = dcmat.reshape(2, SSM_GROUPS, SSM_STATE, SSM_GROUPS, SSM_GROUP)
    dc = jnp.sum(jnp.where(eye[None, :, None, :, None], dc, 0.0), axis=3).transpose(0, 1, 3, 2)
    return db[0], db[1], dc[0], -dc[1]


def _layer_fwd(x, p, w, get_ffn_weights, sp, ctab, stab, n_seq):
    z, h = _in_proj(x, sp["mix_norm_g"], w["w_in"])
    ya, lse = _attention_fwd(z, ctab, stab, sp["attn_sinks"], n_seq)
    ys, states = _ssm_fwd(z, sp["bmat"], sp["a_row"], sp["f_row"], sp["cmat"], sp["ssm_d"], n_seq)
    v = _conv_fwd(z, w["conv_dw_w"], sp["conv_dw_b"], n_seq)
    merge_consts = [w["w_attn_out"], w["w_ssm_glu"], sp["b_ssm_glu"], sp["conv_norm_g"], sp["conv_norm_b"], w["w_conv_out"],
                    sp["b_gate"], w["w_mix_out"], sp["ffn_norm_g"]]
    (x1, hf), _ = _token_call("merge", lambda *a: (list(_merge_tile(*_f32s(a))), []), 512,
                              [_whole(x), _whole(ya), _whole(ys), _whole(v), (z, ZG_W, 0)], merge_consts,
                              [(x.shape[1], F32), (x.shape[1], MXU_DTYPE)], [])
    w = dict(w, **get_ffn_weights(x1))
    gate, up, act = _ffn_in_act(hf, w["w_ffn_in"])
    ffn = _matmul_nn("mm_ffn_out", act, w["w_ffn_out"], F32)

    def ple_fn(x1, ffn, p, w_pi, g_ple, w_pg):
        x2 = x1 + ffn
        return [x2, *_ple_tile(x2, p, w_pi, g_ple, w_pg)], []

    d = x.shape[1]
    (x2, x3, ple_pre, ple_e), _ = _token_call(
        "ple", ple_fn, 512, [_whole(x1), _whole(ffn), _whole(p)], [w["w_ple_in"], sp["ple_norm_g"], w["w_ple_gate"]],
        [(d, F32), (d, F32), (d, MXU_DTYPE), (d, MXU_DTYPE)], [])
    saved = dict(h=h, z=z, ya=ya, lse=lse, ys=ys, states=states, v=v, x1=x1, hf=hf, gate=gate, up=up, act=act, x2=x2, p=p,
                 ple_pre=ple_pre, ple_e=ple_e, x=x)
    return x3, saved, w


def _layer_bwd(dx3, sv, w, sp, on_grads, ctab, stab, n_seq):
    d = dx3.shape[1]
    gw, gs = {}, {}

    def ple_bwd(*a):
        dx2, accs = _ple_bwd_tile(*_f32s(a))
        return [dx2, dx2], accs

    (dx2, dffn), (gw["w_ple_in"], gs["ple_norm_g"], gw["w_ple_gate"]) = _token_call(
        "ple_bwd", ple_bwd, 512, [_whole(sv["x2"]), _whole(sv["p"]), _whole(sv["ple_pre"]), _whole(sv["ple_e"]), _whole(dx3)],
        [w["w_ple_in"], sp["ple_norm_g"], w["w_ple_gate"]], [(d, F32), (d, MXU_DTYPE)],
        [w["w_ple_in"].shape, (1, d), w["w_ple_gate"].shape])

    dgate, dup = _ffn_mid_bwd(dffn, w["w_ffn_out"], sv["gate"], sv["up"])
    gw["w_ffn_out"] = _matmul_tn("mm_ffn_out_dw", sv["act"], dffn)
    dhf = _ffn_in_dx(dgate, dup, w["w_ffn_in"])
    gw["w_ffn_in"] = jnp.concatenate([_matmul_tn("mm_ffn_gate_dw", sv["hf"], dgate), _matmul_tn("mm_ffn_up_dw", sv["hf"], dup)],
                                     axis=1)

    token = on_grads("ffn", gw)

    def merge_bwd(*a):
        return _merge_bwd_tile(*_f32s(a))

    b_gate = sp["b_gate"] if token is None else sp["b_gate"] + token[0:1, 0:1]
    merge_consts = [w["w_attn_out"], w["w_ssm_glu"], sp["b_ssm_glu"], sp["conv_norm_g"], sp["conv_norm_b"], w["w_conv_out"],
                    b_gate, w["w_mix_out"], sp["ffn_norm_g"]]
    dz = lax.empty(sv["z"].shape, MXU_DTYPE)
    (dx_res, dya, dys, dv, dz), macc = _token_call(
        "merge_bwd", merge_bwd, 256,
        [_whole(sv["x1"]), _whole(sv["ya"]), _whole(sv["ys"]), _whole(sv["v"]), (sv["z"], ZG_W, 0), _whole(dx2), _whole(dhf)],
        merge_consts, [(d, F32), (Q_WIDTH, MXU_DTYPE), (SSM_WIDTH, F32), (CONV_WIDTH, F32)],
        [c.shape for c in merge_consts], into=(dz, ZG_W, 0))
    (gw["w_attn_out"], gw["w_ssm_glu"], gs["b_ssm_glu"], gs["conv_norm_g"], gs["conv_norm_b"], gw["w_conv_out"], gs["b_gate"],
     gw["w_mix_out"], gs["ffn_norm_g"]) = macc

    dz, dw_taps = _conv_bwd_taps(sv["z"], dv, w["conv_dw_w"], dz, n_seq)
    gw["conv_dw_w"], gs["conv_dw_b"] = dw_taps[:CONV_K], dw_taps[CONV_K:]

    dz, gs["bmat"], gs["cmat"], gs["a_row"], gs["f_row"], gs["ssm_d"] = _ssm_bwd(
        sv["z"], sv["states"], dys, sp["bmat"], sp["a_row"], sp["f_row"], sp["cmat"], sp["ssm_d"], dz, n_seq)

    dz, dkv, dkvp, gs["attn_sinks"] = _attention_bwd(sv["z"], ctab, stab, sp["attn_sinks"], sv["ya"], sv["lse"], dya, dz, n_seq)
    dz = _kv_combine(dkv, dkvp, dz, n_seq)
    gw["w_in"] = _matmul_tn("mm_in_dw", sv["h"], dz)
    token = on_grads("mix", gw)
    g_in = sp["mix_norm_g"] if token is None else sp["mix_norm_g"] + token[0:1, 0:1]
    dx, gs["mix_norm_g"] = _in_proj_bwd(dz, w["w_in"], sv["x"], dx_res, g_in)
    return dx, gw, gs


def _loss_and_grad(x, target, g):
    def fn(x, tgt, g):
        def f(x, g):
            err = _rms(x, g) - tgt
            return 0.5 * jnp.mean(err * err, axis=-1, keepdims=True)

        per_token, vjp = jax.vjp(f, x, g)
        dx, dg = vjp(jnp.ones_like(per_token))
        return [dx], [jnp.sum(per_token, axis=0, keepdims=True), dg]

    (dx,), (loss, dg) = _token_call("loss", fn, 512, [_whole(x), _whole(target)], [g], [(x.shape[1], F32)],
                                    [(8, LANES), (1, x.shape[1])])
    return loss[0, 0], dx, dg


def _mesh_place():
    return lax.axis_index("x"), lax.axis_index("y"), lax.axis_index("c")


def _flip(v, bit):
    return 1 - v if bit else v


_MESH = pl.DeviceIdType.MESH


def _all_gather(name, xs):
    n = len(xs)

    def body(*refs):
        x_refs, out_refs = refs[:n], refs[n:2 * n]
        send_sems, recv_sems, local_sems = refs[2 * n:]
        mx, my, mc = _mesh_place()
        me, sibling = (mx, my, mc), (mx, my, 1 - mc)
        chips = [(1 - mx, my), (mx, 1 - my), (1 - mx, 1 - my)]

        def slot(a, px, py, pc):
            return out_refs[a].at[4 * px + 2 * py + pc]

        def copy(a, k, block, to, src=None):
            return pltpu.make_async_remote_copy(
                src_ref=slot(a, *block) if src is None else src, dst_ref=slot(a, *block), send_sem=send_sems.at[7 * a + k],
                recv_sem=recv_sems.at[7 * a + k], device_id=to, device_id_type=_MESH)

        mine = [pltpu.make_async_copy(x_refs[a], slot(a, *me), local_sems.at[a]) for a in range(n)]
        for cp in mine:
            cp.start()
        first = [copy(a, 0, me, sibling, src=x_refs[a]) for a in range(n)]
        first += [copy(a, 1 + j, me, (*chip, mc), src=x_refs[a]) for j, chip in enumerate(chips) for a in range(n)]
        for cp in first:
            cp.start()
        passed = []
        for j, chip in enumerate(chips):
            for a in range(n):
                copy(a, 1 + j, (*chip, mc), me).wait_recv()
                passed.append(copy(a, 4 + j, (*chip, mc), sibling))
                passed[-1].start()
        for a in range(n):
            copy(a, 0, sibling, me).wait_recv()
            for j, chip in enumerate(chips):
                copy(a, 4 + j, (*chip, 1 - mc), me).wait_recv()
        for cp in first + passed:
            cp.wait_send()
        for cp in mine:
            cp.wait()

    return pl.pallas_call(
        body, name=name, out_shape=[jax.ShapeDtypeStruct((N_DEV,) + x.shape, x.dtype) for x in xs], in_specs=[_ANY] * n,
        out_specs=[_ANY] * n,
        scratch_shapes=[pltpu.SemaphoreType.DMA((7 * n,)), pltpu.SemaphoreType.DMA((7 * n,)), pltpu.SemaphoreType.DMA((n,))])(*xs)


def _direct_copies(kind, src_refs, land_refs, send_sems, recv_sems, local_sems):
    mx, my, mc = _mesh_place()
    me = 4 * mx + 2 * my + mc
    n = len(src_refs)
    own = [pltpu.make_async_copy(src_refs[a] if kind == "gather" else src_refs[a].at[me], land_refs[a].at[me], local_sems.at[a])
           for a in range(n)]
    copies = []
    for rel in range(1, N_DEV):
        px, py, pc = _flip(mx, rel & 4), _flip(my, rel & 2), _flip(mc, rel & 1)
        for a in range(n):
            src = src_refs[a] if kind == "gather" else src_refs[a].at[4 * px + 2 * py + pc]
            copies.append(pltpu.make_async_remote_copy(
                src_ref=src, dst_ref=land_refs[a].at[me], send_sem=send_sems.at[7 * a + rel - 1],
                recv_sem=recv_sems.at[7 * a + rel - 1], device_id=(px, py, pc), device_id_type=_MESH))
    return copies, own


_HBM = pl.BlockSpec(memory_space=pltpu.HBM)
_SEM = pl.BlockSpec(memory_space=pltpu.SEMAPHORE)
_DATAFLOW = pltpu.SideEffectType.DATAFLOW_SIDE_EFFECTING


def _exchange_start(name, kind, groups):
    sizes = [len(g) for g in groups]
    srcs = [s for g in groups for s in g]
    lands = [lax.empty(((N_DEV,) + s.shape) if kind == "gather" else s.shape, s.dtype) for s in srcs]
    n, n_g = len(srcs), len(groups)
    first = [sum(sizes[:g]) for g in range(n_g)]

    def body(*refs):
        src_refs, land_refs, sems = refs[:n], refs[n:2 * n], refs[2 * n:2 * n + 3 * n_g]
        for g in range(n_g):
            span = slice(first[g], first[g] + sizes[g])
            copies, own = _direct_copies(kind, src_refs[span], land_refs[span], *sems[3 * g:3 * g + 3])
            for cp in own + copies:
                cp.start()
        refs[-1][...] = jnp.zeros_like(refs[-1])

    hbm = lambda a: pltpu.with_memory_space_constraint(a, pltpu.HBM)
    sem_shapes = [pltpu.SemaphoreType.DMA((k * m,)) for m in sizes for k in (7, 7, 1)]
    out = pl.pallas_call(
        body, name=name,
        out_shape=sem_shapes + [pltpu.HBM(a.shape, a.dtype) for a in srcs + lands] + [jax.ShapeDtypeStruct((8, LANES), F32)],
        in_specs=[_HBM] * (2 * n), out_specs=[_SEM] * (3 * n_g) + [_HBM] * (2 * n) + [pl.BlockSpec(memory_space=pltpu.VMEM)],
        input_output_aliases={i: 3 * n_g + i for i in range(2 * n)},
        compiler_params=pltpu.CompilerParams(has_side_effects=_DATAFLOW))(*[hbm(a) for a in srcs + lands])
    sems, arrays = out[:3 * n_g], out[3 * n_g:-1]
    started = [(kind, (*sems[3 * g:3 * g + 3], *arrays[first[g]:first[g] + sizes[g]],
                       *arrays[n + first[g]:n + first[g] + sizes[g]])) for g in range(n_g)]
    return started, out[-1]


def _exchange_wait(name, started, after):
    kind, (send_sems, recv_sems, local_sems, *arrays) = started
    n = len(arrays) // 2

    def body(*refs):
        src_refs, land_refs = refs[:n], refs[n:2 * n]
        copies, own = _direct_copies(kind, src_refs, land_refs, *refs[2 * n:2 * n + 3])
        for cp in copies + own:
            cp.wait()

    out = pl.pallas_call(
        body, name=name, out_shape=[pltpu.HBM(a.shape, a.dtype) for a in arrays],
        in_specs=[_HBM] * (2 * n) + [_SEM] * 3 + [_ANY], out_specs=[_HBM] * (2 * n),
        input_output_aliases={i: i for i in range(2 * n)},
        compiler_params=pltpu.CompilerParams(has_side_effects=_DATAFLOW))(*arrays, send_sems, recv_sems, local_sems, after)
    return out[n:]


def _adamw_math(g, w, m, v):
    m2 = ADAM_B1 * m + (1.0 - ADAM_B1) * g
    v2 = ADAM_B2 * v + (1.0 - ADAM_B2) * jnp.square(g)
    m_hat = m2 / (1.0 - ADAM_B1 ** ADAM_STEP)
    v_hat = v2 / (1.0 - ADAM_B2 ** ADAM_STEP)
    return g, -ADAM_LR * (m_hat / (jnp.sqrt(v_hat) + ADAM_EPS) + ADAM_WD * w), m2, v2


def _sum_blocks(ref):
    g = ref[0].astype(F32)
    for j in range(1, N_DEV):
        g = g + ref[j].astype(F32)
    return g


def _adamw_flat(name, parts, w, m, v):
    r = w.shape[0]
    tile = _pick(r, (1024, 512, 256, 128, 8))

    def body(p_ref, w_ref, m_ref, v_ref, *o_refs):
        for o, val in zip(o_refs, _adamw_math(_sum_blocks(p_ref), w_ref[...], m_ref[...], v_ref[...]), strict=True):
            o[...] = val

    flat = pl.BlockSpec((tile, LANES), lambda i: (i, 0))
    return pl.pallas_call(
        body, name=name, grid=(r // tile,), out_shape=[jax.ShapeDtypeStruct((r, LANES), F32)] * 4,
        in_specs=[pl.BlockSpec((N_DEV, tile, LANES), lambda i: (0, i, 0)), flat, flat, flat], out_specs=[flat] * 4,
        compiler_params=_params(("parallel",)))(parts, w, m, v)


def _adamw_cols(name, landed, base, stride, w, m, v):
    depth, rows, cs = w.shape
    n_slab = -(-cs // LANES)
    tr = _pick(rows, (SLAB_TILE,))

    def body(*refs):
        slabs, (w_ref, m_ref, v_ref), o_refs = refs[:n_slab], refs[n_slab:n_slab + 3], refs[n_slab + 3:]
        g = jnp.concatenate([_sum_blocks(s)[:, :min(LANES, cs - LANES * k)] for k, s in enumerate(slabs)], axis=1)
        for o, val in zip(o_refs, _adamw_math(g, w_ref[...], m_ref[...], v_ref[...]), strict=True):
            o[...] = val

    slab = lambda k: pl.BlockSpec((N_DEV, tr, LANES), lambda l, i: (0, (l * stride + base + k * rows) // tr + i, 0))
    nat = pl.BlockSpec((None, tr, cs), lambda l, i: (l, i, 0))
    return pl.pallas_call(
        body, name=name, grid=(depth, rows // tr), out_shape=[jax.ShapeDtypeStruct(w.shape, F32)] * 4,
        in_specs=[slab(k) for k in range(n_slab)] + [nat] * 3, out_specs=[nat] * 4,
        compiler_params=_params(("parallel", "parallel")))(*[landed] * n_slab, w, m, v)


def _adamw_rows(name, landed, base, stride, w, m, v):
    depth, rs, width = w.shape
    tr = math.gcd(rs, base, stride)

    def body(p_ref, w_ref, m_ref, v_ref, *o_refs):
        for o, val in zip(o_refs, _adamw_math(_sum_blocks(p_ref), w_ref[...], m_ref[...], v_ref[...]), strict=True):
            o[...] = val

    nat = pl.BlockSpec((None, tr, width), lambda l, i: (l, i, 0))
    return pl.pallas_call(
        body, name=name, grid=(depth, rs // tr), out_shape=[jax.ShapeDtypeStruct(w.shape, F32)] * 4,
        in_specs=[pl.BlockSpec((N_DEV, tr, width), lambda l, i: (0, (l * stride + base) // tr + i, 0)), nat, nat, nat],
        out_specs=[nat] * 4, compiler_params=_params(("parallel", "parallel")))(landed, w, m, v)


def _adamw_conv(landed, base, stride, w, m, v):
    depth, taps, cs = w.shape

    def body(p_ref, w_ref, m_ref, v_ref, *o_refs):
        g = _sum_blocks(p_ref)[:taps, :cs]
        for o, val in zip(o_refs, _adamw_math(g, w_ref[...], m_ref[...], v_ref[...]), strict=True):
            o[...] = val

    nat = pl.BlockSpec((None, taps, cs), lambda l: (l, 0, 0))
    return pl.pallas_call(
        body, name="adamw_conv", grid=(depth,), out_shape=[jax.ShapeDtypeStruct(w.shape, F32)] * 4,
        in_specs=[pl.BlockSpec((N_DEV, CONV_HALO, LANES), lambda l: (0, (l * stride + base) // CONV_HALO, 0)), nat, nat, nat],
        out_specs=[nat] * 4, compiler_params=_params(("parallel",)))(landed, w, m, v)


def _unshard_cols(name, gathered, start, rows, cs, shift=0):
    n_slab = -(-cs // LANES)
    total = N_DEV * cs
    tr = _pick(rows, (SLAB_TILE,))

    def body(*refs):
        slabs, o_ref = refs[:n_slab], refs[n_slab]
        for j in range(N_DEV):
            for k, s in enumerate(slabs):
                for src, dst, width in _wrapped(j * cs + LANES * k - shift, min(LANES, cs - LANES * k), total):
                    o_ref[:, dst:dst + width] = s[j, :, src:src + width]

    slab = lambda k: pl.BlockSpec((N_DEV, tr, LANES), lambda i: (0, (start + k * rows) // tr + i, 0))
    return pl.pallas_call(
        body, name=name, grid=(rows // tr,), out_shape=jax.ShapeDtypeStruct((rows, total), gathered.dtype),
        in_specs=[slab(k) for k in range(n_slab)], out_specs=pl.BlockSpec((tr, total), lambda i: (i, 0)),
        compiler_params=_params(("parallel",)))(*[gathered] * n_slab)


def _shard_cols(name, full, cs, shift=0):
    rows, total = full.shape
    n_slab = -(-cs // LANES)
    tr = _pick(rows, (SLAB_TILE,))

    def body(f_ref, o_ref):
        for j in range(N_DEV):
            for k in range(n_slab):
                used = min(LANES, cs - LANES * k)
                for src, dst, width in _wrapped(j * cs + LANES * k - shift, used, total):
                    o_ref[j, k, :, src:src + width] = f_ref[:, dst:dst + width].astype(o_ref.dtype)
                if used < LANES:
                    o_ref[j, k, :, used:] = jnp.zeros((tr, LANES - used), o_ref.dtype)

    out = pl.pallas_call(
        body, name=name, grid=(rows // tr,), out_shape=jax.ShapeDtypeStruct((N_DEV, n_slab, rows, LANES), BF16),
        in_specs=[pl.BlockSpec((tr, total), lambda i: (i, 0))],
        out_specs=pl.BlockSpec((N_DEV, n_slab, tr, LANES), lambda i: (0, 0, i, 0)),
        compiler_params=_params(("parallel",)))(full)
    return out.reshape(N_DEV, n_slab * rows, LANES)


def _wrapped(pos, width, total):
    pos %= total
    if pos + width <= total:
        return [(0, pos, width)]
    head = total - pos
    return [(0, pos, head), (head, 0, width - head)]


CONV_W_PIECES = 3


def _pad_to(n, align):
    return -(-n // align) * align


def _layout(group):
    col_names, row_names, with_conv = GROUPS[group]
    dims = {name: (rows, cols) for name, rows, cols, _ in SHARDED}
    col, off = {}, 0
    for name in col_names:
        rows, cols = dims[name]
        cs = cols // N_DEV
        col[name] = (off, rows, cs)
        off += -(-cs // LANES) * rows
    conv_base = off
    col_rows = _pad_to(off + with_conv * CONV_W_PIECES * CONV_HALO, SLAB_TILE)
    row, off = {}, 0
    for name in row_names:
        rs = dims[name][0] // N_DEV
        row[name] = (off, rs)
        off += _pad_to(rs, LANES)
    return col, conv_base, col_rows, row, off


def _slabs(shard, fill):
    rows, cs = shard.shape
    parts = []
    for k in range(-(-cs // LANES)):
        part = shard[:, LANES * k:min(LANES * (k + 1), cs)]
        parts.append(jnp.pad(part, ((0, 0), (0, LANES - part.shape[1])), constant_values=fill))
    return jnp.concatenate(parts, axis=0)


def _concat_padded(pieces, total, axis):
    used = sum(p.shape[axis] for p in pieces)
    if total > used:
        shape = list(pieces[0].shape)
        shape[axis] = total - used
        pieces = pieces + [jnp.zeros(shape, pieces[0].dtype)]
    return jnp.concatenate(pieces, axis=axis)


def _split3(a):
    hi = a.astype(BF16)
    r1 = a - hi.astype(F32)
    mid = r1.astype(BF16)
    return hi, mid, (r1 - mid.astype(F32)).astype(BF16)


def _pack_small(arrs, lead=()):
    flat = jnp.concatenate([a.reshape(lead + (-1,)) for a in arrs], axis=-1)
    total = _pad_to(flat.shape[-1], 512 * LANES)
    flat = jnp.pad(flat, [(0, 0)] * len(lead) + [(0, total - flat.shape[-1])])
    return flat.reshape(lead + (total // LANES, LANES))


def _unpack_small(flat, shapes):
    flat = flat.reshape(-1)
    res, off = [], 0
    for s in shapes:
        n = int(np.prod(s))
        res.append(flat[off:off + n].reshape(s))
        off += n
    return res


def _small_rows(a, depth):
    n16 = depth * SSM_GROUPS
    a_re, a_im, f_re, f_im = _ssm_coeffs(a["ssm_lambda_re"].reshape(n16, SSM_STATE), a["ssm_lambda_im"].reshape(n16, SSM_STATE),
                                         a["ssm_log_dt"].reshape(n16, 1))
    rows = []
    for l in range(depth):
        sp = {k: _row(a[k][l]) for k in ("mix_norm_g", "b_gate", "attn_sinks", "ssm_d", "b_ssm_glu", "conv_dw_b",
                                         "conv_norm_g", "conv_norm_b", "ffn_norm_g", "ple_norm_g")}
        g = slice(l * SSM_GROUPS, (l + 1) * SSM_GROUPS)
        sp["a_row"] = jnp.concatenate([a_re[g].reshape(1, -1), a_im[g].reshape(1, -1)], axis=1)
        sp["f_row"] = jnp.concatenate([f_re[g].reshape(1, -1), f_im[g].reshape(1, -1)], axis=1)
        sp["bmat"], sp["cmat"] = _ssm_mats(a["ssm_b_re"][l], a["ssm_b_im"][l], a["ssm_c_re"][l], a["ssm_c_im"][l])
        rows.append(sp)
    return rows


def _local_step(a, get_weights, on_grads, depth):
    n_seq, seq, d = a["x"].shape
    t = n_seq * seq
    inv = ROPE_THETA ** (-jnp.arange(0, ROPE_DIM, 2, dtype=F32) / ROPE_DIM)
    lane = np.arange(LANES) % HEAD_DIM
    inv_lane = jnp.where(lane < ROPE_DIM, jnp.tile(inv, LANES // (ROPE_DIM // 2)), 0.0).reshape(1, LANES)
    ctab, stab = _rope_tables(a["positions"].reshape(t), inv_lane)
    small = _small_rows(a, depth)

    x = a["x"].reshape(t, d)
    saved, weights = [], []
    for l in range(depth):
        x, sv, w = _layer_fwd(x, a["p"][l].reshape(t, -1), get_weights(l, "mix", x),
                              functools.partial(get_weights, l, "ffn"), small[l], ctab, stab, n_seq)
        saved.append(sv)
        weights.append(w)
    loss, dx, d_final = _loss_and_grad(x, a["loss_target"].reshape(t, d), _row(a["final_norm_g"]))
    gws, gss = [None] * depth, [None] * depth
    for l in reversed(range(depth)):
        dx, gws[l], gss[l] = _layer_bwd(dx, saved[l], weights[l], small[l], functools.partial(on_grads, l), ctab, stab, n_seq)

    n16 = depth * SSM_GROUPS
    halves = lambda k, h: jnp.concatenate([gss[l][k][:, h * SSM_LANES:(h + 1) * SSM_LANES].reshape(SSM_GROUPS, SSM_STATE)
                                           for l in range(depth)], axis=0)
    dlr, dli, ddt = _ssm_coeffs_bwd(a["ssm_lambda_re"].reshape(n16, SSM_STATE), a["ssm_lambda_im"].reshape(n16, SSM_STATE),
                                    a["ssm_log_dt"].reshape(n16, 1),
                                    (halves("a_row", 0), halves("a_row", 1), halves("f_row", 0), halves("f_row", 1)))
    bc = [_ssm_mats_t(gss[l]["bmat"], gss[l]["cmat"]) for l in range(depth)]
    gsmall = {k: jnp.stack([gss[l][k].reshape(a[k].shape[1:]) for l in range(depth)])
              for k in ("mix_norm_g", "b_gate", "attn_sinks", "ssm_d", "b_ssm_glu", "conv_dw_b", "conv_norm_g", "conv_norm_b",
                        "ffn_norm_g", "ple_norm_g")}
    gsmall["ssm_lambda_re"] = dlr.reshape(a["ssm_lambda_re"].shape)
    gsmall["ssm_lambda_im"] = dli.reshape(a["ssm_lambda_im"].shape)
    gsmall["ssm_log_dt"] = ddt.reshape(a["ssm_log_dt"].shape)
    for i, k in enumerate(("ssm_b_re", "ssm_b_im", "ssm_c_re", "ssm_c_im")):
        gsmall[k] = jnp.stack([bc[l][i] for l in range(depth)])
    gsmall["final_norm_g"] = d_final.reshape(a["final_norm_g"].shape)
    return loss, dx.reshape(n_seq, seq, d), gws, gsmall


def kernel(x, p, positions, mix_norm_g, w_in, b_gate, attn_sinks, w_attn_out, ssm_lambda_re, ssm_lambda_im, ssm_log_dt, ssm_b_re, ssm_b_im, ssm_c_re, ssm_c_im, ssm_d, w_ssm_glu, b_ssm_glu, conv_dw_w, conv_dw_b, conv_norm_g, conv_norm_b, w_conv_out, w_mix_out, ffn_norm_g, w_ffn_in, w_ffn_out, w_ple_in, ple_norm_g, w_ple_gate, final_norm_g, loss_target, m_mix_norm_g, m_w_in, m_b_gate, m_attn_sinks, m_w_attn_out, m_ssm_lambda_re, m_ssm_lambda_im, m_ssm_log_dt, m_ssm_b_re, m_ssm_b_im, m_ssm_c_re, m_ssm_c_im, m_ssm_d, m_w_ssm_glu, m_b_ssm_glu, m_conv_dw_w, m_conv_dw_b, m_conv_norm_g, m_conv_norm_b, m_w_conv_out, m_w_mix_out, m_ffn_norm_g, m_w_ffn_in, m_w_ffn_out, m_w_ple_in, m_ple_norm_g, m_w_ple_gate, m_final_norm_g, v_mix_norm_g, v_w_in, v_b_gate, v_attn_sinks, v_w_attn_out, v_ssm_lambda_re, v_ssm_lambda_im, v_ssm_log_dt, v_ssm_b_re, v_ssm_b_im, v_ssm_c_re, v_ssm_c_im, v_ssm_d, v_w_ssm_glu, v_b_ssm_glu, v_conv_dw_w, v_conv_dw_b, v_conv_norm_g, v_conv_norm_b, v_w_conv_out, v_w_mix_out, v_ffn_norm_g, v_w_ffn_in, v_w_ffn_out, v_w_ple_in, v_ple_norm_g, v_w_ple_gate, v_final_norm_g):
    a = dict(locals())
    depth = w_in.shape[0]
    layouts = {group: _layout(group) for group in GROUPS}
    shift = {"w_in": Z_SPLIT}
    conv_pad = ((0, 0), (0, CONV_HALO - CONV_K), (0, LANES - CONV_WIDTH // N_DEV))

    def packed_weights(l, group, fill):
        col, _, col_rows, row, _ = layouts[group]
        pieces = [_slabs(a[name][l].astype(BF16), fill) for name in col]
        if GROUPS[group][2]:
            pieces.append(jnp.pad(jnp.stack(_split3(a["conv_dw_w"][l])), conv_pad).reshape(-1, LANES))
        regions = [_concat_padded([a[name][l].astype(BF16)], _pad_to(rs, LANES), 0) for name, (_, rs) in row.items()]
        return [_concat_padded(pieces, col_rows, 0), jnp.concatenate(regions, axis=0)]

    gathers, tokens, fill = [], [], jnp.zeros((), BF16)
    for l in range(depth):
        started, token = _exchange_start(f"gather_start_{l}", "gather", [packed_weights(l, group, fill) for group in GROUPS])
        gathers.append(dict(zip(GROUPS, started, strict=True)))
        tokens.append(token[0:1, 0:1])
        fill = token[0, 0].astype(BF16)

    def get_weights(l, group, after):
        col, conv_base, _, row, _ = layouts[group]
        slab8, row8 = _exchange_wait(f"gather_wait_{group}_{l}", gathers[l][group], after)
        w = {name: _unshard_cols("unshard_" + name, slab8, base, rows, cs, shift.get(name, 0))
             for name, (base, rows, cs) in col.items()}
        for name, (base, rs) in row.items():
            w[name] = row8[:, base:base + rs].reshape(N_DEV * rs, -1)
        if GROUPS[group][2]:
            conv = slab8[:, conv_base:conv_base + CONV_W_PIECES * CONV_HALO]
            conv = conv.reshape(N_DEV, CONV_W_PIECES, CONV_HALO, LANES)[:, :, :CONV_K, :CONV_WIDTH // N_DEV].astype(F32)
            w["conv_dw_w"] = jnp.sum(conv, axis=1).transpose(1, 0, 2).reshape(CONV_K, CONV_WIDTH)
        return w

    scatters = {}

    def on_grads(l, group, gw):
        col, _, col_rows, row, _ = layouts[group]
        pieces = [_shard_cols("shard_" + name, gw[name], cs, shift.get(name, 0)) for name, (_, _, cs) in col.items()]
        if GROUPS[group][2]:
            conv = gw["conv_dw_w"].reshape(CONV_K, N_DEV, CONV_WIDTH // N_DEV).transpose(1, 0, 2).astype(BF16)
            pieces.append(jnp.pad(jnp.pad(conv, conv_pad), ((0, 0), (0, (CONV_W_PIECES - 1) * CONV_HALO), (0, 0))))
        regions = [_concat_padded([gw[name].astype(BF16).reshape(N_DEV, rs, -1)], _pad_to(rs, LANES), 1)
                   for name, (_, rs) in row.items()]
        (scatters[l, group],), token = _exchange_start(
            f"grads_start_{group}_{l}", "scatter", [[_concat_padded(pieces, col_rows, 1), jnp.concatenate(regions, axis=1)]])
        return token

    local = dict(a, mix_norm_g=a["mix_norm_g"] + sum(tokens))
    loss, grad_x, _, gsmall = _local_step(local, get_weights, on_grads, depth)
    loss = lax.psum(loss, ("x", "y", "c"))

    shapes = [a[k].shape for k in REPLICATED]
    parts, = _all_gather("gather_small_grads", [_pack_small([gsmall[k] for k in REPLICATED])])
    small_state = [_pack_small([a[pre + k] for k in REPLICATED]) for pre in ("", "m_", "v_")]
    small_flat = _adamw_flat("adamw_replicated", parts, *small_state)
    small = [dict(zip(REPLICATED, _unpack_small(o, shapes), strict=True)) for o in small_flat]

    state = lambda name: (a[name], a["m_" + name], a["v_" + name])
    big, after = {}, small_flat[1]
    for group in ("ffn", "mix"):
        col, conv_base, col_rows, row, row_rows = layouts[group]
        landed = [_exchange_wait(f"grads_wait_{group}_{l}", scatters[l, group], after) for l in range(depth)]
        landed_slab = jnp.concatenate([ls for ls, _ in landed], axis=1)
        landed_row = jnp.concatenate([lr for _, lr in landed], axis=1)
        big.update({name: _adamw_cols("adamw_" + name, landed_slab, base, col_rows, *state(name)) for name, (base, _, _) in col.items()})
        big.update({name: _adamw_rows("adamw_" + name, landed_row, base, row_rows, *state(name)) for name, (base, _) in row.items()})
        if GROUPS[group][2]:
            big["conv_dw_w"] = _adamw_conv(landed_slab, conv_base, col_rows, *state("conv_dw_w"))
        after = big[next(iter(col))][1]

    def result(kind, name):
        if name in REPLICATED:
            return small[kind][name]
        return big[name][kind]

    return (loss, grad_x, *[result(kind, n) for kind in range(4) for n in WEIGHT_ORDER])
```

```python
import functools
import math

import numpy as np
import jax
import jax.numpy as jnp
from jax import lax
from jax.experimental import pallas as pl
from jax.experimental.pallas import tpu as pltpu

F32 = jnp.float32
BF16 = jnp.bfloat16
MXU_DTYPE = jnp.bfloat16
VMEM_LIMIT_BYTES = 56 * 2 ** 20
N_DEV = 8
LANES = 128

HEAD_DIM = 64
N_Q_HEADS = 8
N_KV_HEADS = 2
GQA_GROUP = 4
BLOCK = 128
ROPE_THETA = 500000.0
ROPE_DIM = 16
Q_WIDTH = 512
KV_WIDTH = 128
SSM_WIDTH = 256
SSM_GROUP = 16
SSM_GROUPS = 16
SSM_STATE = 64
SSM_LANES = SSM_GROUPS * SSM_STATE
CONV_WIDTH = 256
CONV_K = 31
CONV_HALO = 32
EPS = 1e-6
NEG_INF = -1e30
ADAM_LR, ADAM_B1, ADAM_B2, ADAM_EPS, ADAM_WD, ADAM_STEP = 0.001, 0.9, 0.999, 1e-08, 0.01, 10

ZG_W, ZQ_W, ZKV_W, ZS_W, ZC_W = 3072, 512, 256, 256, 512
ZQ_BLK, ZKV_BLK, ZS_BLK, ZC_BLK = 3072 // 512, 3584 // 256, 3840 // 256, 4096 // 512
Z_WIDTH = 4608
Z_SPLIT = 1536

SHARDED = (("w_in", 1024, 4608, 1), ("w_attn_out", 512, 1024, 1), ("w_ssm_glu", 256, 2048, 1),
           ("conv_dw_w", 31, 256, 1), ("w_conv_out", 256, 1024, 1), ("w_mix_out", 1024, 1024, 0),
           ("w_ffn_in", 1024, 5632, 1), ("w_ffn_out", 2816, 1024, 0), ("w_ple_in", 256, 1024, 1),
           ("w_ple_gate", 1024, 1024, 0))
GROUPS = {"mix": (("w_in", "w_attn_out", "w_ssm_glu", "w_conv_out"), ("w_mix_out",), True),
          "ffn": (("w_ffn_in", "w_ple_in"), ("w_ffn_out", "w_ple_gate"), False)}
SLAB_TILE = 256
FLAT_ROW_ALIGN = 1024
REPLICATED = ("mix_norm_g", "b_gate", "attn_sinks", "ssm_lambda_re", "ssm_lambda_im", "ssm_log_dt", "ssm_b_re",
              "ssm_b_im", "ssm_c_re", "ssm_c_im", "ssm_d", "b_ssm_glu", "conv_dw_b", "conv_norm_g", "conv_norm_b",
              "ffn_norm_g", "ple_norm_g", "final_norm_g")
WEIGHT_ORDER = ("mix_norm_g", "w_in", "b_gate", "attn_sinks", "w_attn_out", "ssm_lambda_re", "ssm_lambda_im",
                "ssm_log_dt", "ssm_b_re", "ssm_b_im", "ssm_c_re", "ssm_c_im", "ssm_d", "w_ssm_glu", "b_ssm_glu",
                "conv_dw_w", "conv_dw_b", "conv_norm_g", "conv_norm_b", "w_conv_out", "w_mix_out", "ffn_norm_g",
                "w_ffn_in", "w_ffn_out", "w_ple_in", "ple_norm_g", "w_ple_gate", "final_norm_g")


_ANY = pl.BlockSpec(memory_space=pl.ANY)


def _params(sem=None):
    return pltpu.CompilerParams(dimension_semantics=sem, vmem_limit_bytes=VMEM_LIMIT_BYTES)


def _pick(n, cands):
    for c in cands:
        if n % c == 0:
            return c
    return n


def _dot(a, b, dims):
    return lax.dot_general(a.astype(MXU_DTYPE), b.astype(MXU_DTYPE), (dims, ((), ())), preferred_element_type=F32)


def _dot_nn(a, b):
    return _dot(a, b, ((1,), (0,)))


def _dot_nt(a, b):
    return _dot(a, b, ((1,), (1,)))


def _dot_tn(a, b):
    return _dot(a, b, ((0,), (0,)))


@jax.custom_vjp
def _mm(x, w):
    return _dot_nn(x, w)


def _mm_f(x, w):
    return _dot_nn(x, w), (x, w)


def _mm_b(res, dy):
    x, w = res
    return _dot_nt(dy, w).astype(x.dtype), _dot_tn(x, dy).astype(w.dtype)


_mm.defvjp(_mm_f, _mm_b)


def _rms(x, g):
    return x * lax.rsqrt(jnp.mean(x * x, axis=-1, keepdims=True) + EPS) * g


ROW_TILES = (1024, 512, 256, 128)
COL_TILES = (1536, 1408, 1024, 512, 256, 128)


def _matmul_nn(name, a, b, out_dtype):
    t, k = a.shape
    n = b.shape[1]
    tm, tn = _pick(t, ROW_TILES), _pick(n, COL_TILES)

    def body(a_ref, b_ref, o_ref):
        o_ref[...] = _dot_nn(a_ref[...], b_ref[...]).astype(o_ref.dtype)

    return pl.pallas_call(
        body, name=name, grid=(t // tm, n // tn), out_shape=jax.ShapeDtypeStruct((t, n), out_dtype),
        in_specs=[pl.BlockSpec((tm, k), lambda i, j: (i, 0)), pl.BlockSpec((k, tn), lambda i, j: (0, j))],
        out_specs=pl.BlockSpec((tm, tn), lambda i, j: (i, j)),
        compiler_params=_params(("parallel", "parallel")))(a, b)


def _matmul_tn(name, a, b):
    t, m = a.shape
    n = b.shape[1]
    tm, tn, tt = _pick(m, COL_TILES[1:]), _pick(n, COL_TILES), _pick(t, ROW_TILES)

    def body(a_ref, b_ref, o_ref):
        @pl.when(pl.program_id(2) == 0)
        def _():
            o_ref[...] = jnp.zeros_like(o_ref)

        o_ref[...] += _dot_tn(a_ref[...], b_ref[...])

    return pl.pallas_call(
        body, name=name, grid=(m // tm, n // tn, t // tt), out_shape=jax.ShapeDtypeStruct((m, n), F32),
        in_specs=[pl.BlockSpec((tt, tm), lambda i, j, s: (s, i)), pl.BlockSpec((tt, tn), lambda i, j, s: (s, j))],
        out_specs=pl.BlockSpec((tm, tn), lambda i, j, s: (i, j)),
        compiler_params=_params(("parallel", "parallel", "arbitrary")))(a, b)


def _two_parts(n):
    cut = n // (2 * LANES) * LANES
    return [slice(0, n)] if cut == 0 else [slice(0, cut), slice(cut, n)]


def _in_proj(x, g, w):
    t, d = x.shape
    n = w.shape[1]
    tm, tn = _pick(t, ROW_TILES), _pick(n, COL_TILES)

    def body(x_ref, g_ref, w_ref, z_ref, h_ref):
        h = _rms(x_ref[...], g_ref[...]).astype(h_ref.dtype)

        @pl.when(pl.program_id(1) == 0)
        def _():
            h_ref[...] = h

        z_ref[...] = _dot_nn(h, w_ref[...])

    return pl.pallas_call(
        body, name="in_proj", grid=(t // tm, n // tn),
        out_shape=[jax.ShapeDtypeStruct((t, n), F32), jax.ShapeDtypeStruct((t, d), MXU_DTYPE)],
        in_specs=[pl.BlockSpec((tm, d), lambda i, j: (i, 0)), pl.BlockSpec((1, d), lambda i, j: (0, 0)),
                  pl.BlockSpec((d, tn), lambda i, j: (0, j))],
        out_specs=[pl.BlockSpec((tm, tn), lambda i, j: (i, j)), pl.BlockSpec((tm, d), lambda i, j: (i, 0))],
        compiler_params=_params(("parallel", "arbitrary")))(x, g, w)


def _in_proj_bwd(dz, w, x, dx_res, g):
    t, d = x.shape
    tm = _pick(t, ROW_TILES[1:])

    def body(dz_ref, w_ref, x_ref, r_ref, g_ref, dx_ref, dg_ref):
        _, vjp = jax.vjp(_norm_in_tile, x_ref[...], g_ref[...])
        dx, dg = vjp(_dot_nt(dz_ref[...], w_ref[...]))
        dx_ref[...] = dx + r_ref[...]

        @pl.when(pl.program_id(0) == 0)
        def _():
            dg_ref[...] = jnp.zeros_like(dg_ref)

        dg_ref[...] += dg

    rows = lambda width: pl.BlockSpec((tm, width), lambda i: (i, 0))
    whole = lambda a: pl.BlockSpec(a.shape, lambda i: (0, 0))
    return pl.pallas_call(
        body, name="in_proj_bwd", grid=(t // tm,), out_shape=[jax.ShapeDtypeStruct((t, d), F32), jax.ShapeDtypeStruct((1, d), F32)],
        in_specs=[rows(dz.shape[1]), whole(w), rows(d), rows(d), whole(g)], out_specs=[rows(d), whole(g)],
        compiler_params=_params(("arbitrary",)))(dz, w, x, dx_res, g)


def _ffn_in_act(hf, w_fi):
    t, k = hf.shape
    f = w_fi.shape[1] // 2
    tm, tf = _pick(t, ROW_TILES[1:]), _pick(f, COL_TILES)
    nf = f // tf

    def body(a_ref, wg_ref, wu_ref, g_ref, u_ref, act_ref):
        a = a_ref[...]
        for cols in _two_parts(tf):
            g, u = _dot_nn(a, wg_ref[:, cols]), _dot_nn(a, wu_ref[:, cols])
            g_ref[:, cols] = g.astype(g_ref.dtype)
            u_ref[:, cols] = u.astype(u_ref.dtype)
            act_ref[:, cols] = (jax.nn.silu(g) * u).astype(act_ref.dtype)

    out = pl.BlockSpec((tm, tf), lambda i, j: (i, j))
    return pl.pallas_call(
        body, name="ffn_in_act", grid=(t // tm, nf), out_shape=[jax.ShapeDtypeStruct((t, f), MXU_DTYPE)] * 3,
        in_specs=[pl.BlockSpec((tm, k), lambda i, j: (i, 0)), pl.BlockSpec((k, tf), lambda i, j: (0, j)),
                  pl.BlockSpec((k, tf), lambda i, j: (0, j + nf))],
        out_specs=[out, out, out], compiler_params=_params(("parallel", "parallel")))(hf, w_fi, w_fi)


def _ffn_mid_bwd(dffn, w_fo, gate, up):
    t, d = dffn.shape
    f = w_fo.shape[0]
    tm, tf = _pick(t, ROW_TILES[1:]), _pick(f, COL_TILES)

    def body(a_ref, w_ref, g_ref, u_ref, dg_ref, du_ref):
        a = a_ref[...]
        for cols in _two_parts(tf):
            dact = _dot_nt(a, w_ref[cols, :])
            g, u = g_ref[:, cols].astype(F32), u_ref[:, cols].astype(F32)
            sg = jax.nn.sigmoid(g)
            dg_ref[:, cols] = (dact * u * sg * (1.0 + g * (1.0 - sg))).astype(dg_ref.dtype)
            du_ref[:, cols] = (dact * g * sg).astype(du_ref.dtype)

    blk = pl.BlockSpec((tm, tf), lambda i, j: (i, j))
    return pl.pallas_call(
        body, name="ffn_mid_bwd", grid=(t // tm, f // tf), out_shape=[jax.ShapeDtypeStruct((t, f), MXU_DTYPE)] * 2,
        in_specs=[pl.BlockSpec((tm, d), lambda i, j: (i, 0)), pl.BlockSpec((tf, d), lambda i, j: (j, 0)), blk, blk],
        out_specs=[blk, blk], compiler_params=_params(("parallel", "parallel")))(dffn, w_fo, gate, up)


def _ffn_in_dx(dgate, dup, w_fi):
    t, f = dgate.shape
    d = w_fi.shape[0]
    tm = _pick(t, ROW_TILES[1:])

    def body(g_ref, u_ref, w_ref, o_ref):
        o_ref[...] = _dot_nt(g_ref[...], w_ref[:, :f]) + _dot_nt(u_ref[...], w_ref[:, f:])

    blk = pl.BlockSpec((tm, f), lambda i: (i, 0))
    return pl.pallas_call(
        body, name="ffn_in_dx", grid=(t // tm,), out_shape=jax.ShapeDtypeStruct((t, d), F32),
        in_specs=[blk, blk, pl.BlockSpec(w_fi.shape, lambda i: (0, 0))], out_specs=pl.BlockSpec((tm, d), lambda i: (i, 0)),
        compiler_params=_params(("parallel",)))(dgate, dup, w_fi)


def _token_call(name, fn, tile, tok_ins, consts, tok_outs, acc_outs, into=None):
    n_rows = tok_ins[0][0].shape[0]
    tile = min(tile, n_rows)
    n_ti, n_c = len(tok_ins), len(consts)
    n_in = n_ti + n_c + (into is not None)
    n_to = len(tok_outs) + (into is not None)

    def body(*refs):
        ins = [r[...] for r in refs[:n_ti + n_c]]
        outs, accs = fn(*ins)
        for r, v in zip(refs[n_in:n_in + n_to], outs, strict=True):
            r[...] = v.astype(r.dtype)
        first = pl.program_id(0) == 0
        for r, v in zip(refs[n_in + n_to:], accs, strict=True):
            @pl.when(first)
            def _(r=r):
                r[...] = jnp.zeros_like(r)

            r[...] += jnp.broadcast_to(v, r.shape).astype(F32)

    in_specs = [pl.BlockSpec((tile, w), functools.partial(lambda i, c: (i, c), c=cb)) for _, w, cb in tok_ins]
    in_specs += [pl.BlockSpec(c.shape, lambda i: (0, 0)) for c in consts]
    out_shape = [jax.ShapeDtypeStruct((n_rows, w), dt) for w, dt in tok_outs]
    out_specs = [pl.BlockSpec((tile, w), lambda i: (i, 0)) for w, _ in tok_outs]
    operands = [a for a, _, _ in tok_ins] + list(consts)
    aliases = {}
    if into is not None:
        target, width, col_block = into
        in_specs.append(_ANY)
        operands.append(target)
        out_shape.append(jax.ShapeDtypeStruct(target.shape, target.dtype))
        out_specs.append(pl.BlockSpec((tile, width), lambda i: (i, col_block)))
        aliases = {n_in - 1: n_to - 1}
    out_shape += [jax.ShapeDtypeStruct(s, F32) for s in acc_outs]
    out_specs += [pl.BlockSpec(s, lambda i: (0, 0)) for s in acc_outs]
    res = pl.pallas_call(
        body, name=name, grid=(n_rows // tile,), out_shape=out_shape, in_specs=in_specs, out_specs=out_specs,
        input_output_aliases=aliases, compiler_params=_params(("arbitrary",)))(*operands)
    return res[:n_to], res[n_to:]


def _whole(a):
    return (a, a.shape[1], 0)


def _norm_in_tile(x, g):
    return _rms(x, g)


def _conv_post_tile(v, g, b):
    mu = jnp.mean(v, axis=-1, keepdims=True)
    var = jnp.mean(jnp.square(v - mu), axis=-1, keepdims=True)
    return jax.nn.silu((v - mu) * lax.rsqrt(var + EPS) * g + b)


def _branches_tile(ya, ys, v, gin, w_ao, w_sg, b_sg, ln_g, ln_b, w_co, b_gate):
    d = w_ao.shape[1]
    y_attn = _mm(ya, w_ao)
    pre = _mm(jax.nn.gelu(ys), w_sg) + b_sg
    y_ssm = pre[:, :d] * jax.nn.sigmoid(pre[:, d:])
    y_conv = _mm(_conv_post_tile(v, ln_g, ln_b), w_co)
    gates = jax.nn.sigmoid(gin + b_gate)
    return gates[:, :d] * y_attn + gates[:, d:2 * d] * y_ssm + gates[:, 2 * d:] * y_conv


def _merge_tile(x, ya, ys, v, gin, w_ao, w_sg, b_sg, ln_g, ln_b, w_co, b_gate, w_mo, g_ffn):
    x1 = x + _mm(_branches_tile(ya, ys, v, gin, w_ao, w_sg, b_sg, ln_g, ln_b, w_co, b_gate), w_mo)
    return x1, _rms(x1, g_ffn)


def _merge_bwd_tile(x1, ya, ys, v, gin, dx1, dhf, w_ao, w_sg, b_sg, ln_g, ln_b, w_co, b_gate, w_mo, g_ffn):
    _, norm_vjp = jax.vjp(_rms, x1, g_ffn)
    dx1_norm, dg_ffn = norm_vjp(dhf)
    dx1 = dx1 + dx1_norm
    merged, branch_vjp = jax.vjp(_branches_tile, ya, ys, v, gin, w_ao, w_sg, b_sg, ln_g, ln_b, w_co, b_gate)
    grads = branch_vjp(_dot_nt(dx1, w_mo))
    return [dx1, *grads[:4]], [*grads[4:], _dot_tn(merged, dx1), dg_ffn]


def _ple_tile(x2, p, w_pi, g_ple, w_pg):
    pre, e = _mm(_rms(x2, g_ple), w_pg), _mm(p, w_pi)
    return x2 + jax.nn.sigmoid(pre) * e, pre, e


def _ple_bwd_tile(x2, p, pre, e, dx3, w_pi, g_ple, w_pg):
    sig = jax.nn.sigmoid(pre)
    dpre = dx3 * e * sig * (1.0 - sig)
    hn, norm_vjp = jax.vjp(_rms, x2, g_ple)
    dx2_norm, dg_ple = norm_vjp(_dot_nt(dpre, w_pg))
    return dx3 + dx2_norm, [_dot_tn(p, dx3 * sig), dg_ple, _dot_tn(hn, dpre)]


def _f32s(vals):
    return [v.astype(F32) for v in vals]


def _rope_tables(positions, inv_lane):
    def fn(pos, inv):
        ang = pos.astype(F32) * inv
        j = lax.broadcasted_iota(jnp.int32, ang.shape, 1) % HEAD_DIM
        c = jnp.where(j < ROPE_DIM, jnp.cos(ang), 1.0)
        s = jnp.sin(ang)
        s = jnp.where(j < ROPE_DIM // 2, -s, jnp.where(j < ROPE_DIM, s, 0.0))
        return [c, s], []

    (c, s), _ = _token_call("rope_tables", fn, 1024, [_whole(positions.reshape(-1, 1))], [inv_lane],
                            [(LANES, F32), (LANES, F32)], [])
    return c, s


def _swap_halves(t):
    n = t.shape[1]
    j = lax.broadcasted_iota(jnp.int32, t.shape, 1) % HEAD_DIM
    lower = pltpu.roll(t, n - ROPE_DIM // 2, 1)
    upper = jnp.where(j < ROPE_DIM, pltpu.roll(t, ROPE_DIM // 2, 1), 0.0)
    return jnp.where(j < ROPE_DIM // 2, lower, upper)


def _rope(t, c, s):
    return t * c + _swap_halves(t) * s


def _rope_t(dt, c, s):
    return dt * c + _swap_halves(dt * s)


def _tile4(a):
    return jnp.concatenate([a] * (Q_WIDTH // LANES), axis=1)


def _attn_mask(n):
    qi = lax.broadcasted_iota(jnp.int32, (GQA_GROUP * BLOCK, 2 * BLOCK), 0) % BLOCK
    kj = lax.broadcasted_iota(jnp.int32, (GQA_GROUP * BLOCK, 2 * BLOCK), 1)
    dist = qi + BLOCK - kj
    return (dist >= 0) & (dist < BLOCK) & ((n > 0) | (kj >= BLOCK))


def _attn_specs(n_seq):
    own = lambda w, blk: pl.BlockSpec((n_seq, BLOCK, w), lambda n: (0, n, blk))
    prev = lambda w, blk: pl.BlockSpec((n_seq, BLOCK, w), lambda n: (0, jnp.maximum(n - 1, 0), blk))
    return [own(ZQ_W, ZQ_BLK), own(ZKV_W, ZKV_BLK), prev(ZKV_W, ZKV_BLK), own(LANES, 0), own(LANES, 0), prev(LANES, 0),
            prev(LANES, 0), pl.BlockSpec((1, N_Q_HEADS), lambda n: (0, 0))]


def _by_seq(a, n_seq):
    return a.reshape(n_seq, a.shape[0] // n_seq, a.shape[1])


ATTN_SCALE = HEAD_DIM ** -0.5


def _stack_heads(t, kh):
    return jnp.concatenate([t[:, (kh * GQA_GROUP + g) * HEAD_DIM:(kh * GQA_GROUP + g + 1) * HEAD_DIM]
                            for g in range(GQA_GROUP)], axis=0)


def _stack_sinks(sink, kh):
    return jnp.concatenate([jnp.broadcast_to(sink[:, kh * GQA_GROUP + g:kh * GQA_GROUP + g + 1], (BLOCK, 1))
                            for g in range(GQA_GROUP)], axis=0)


def _attn_band(b, q_ref, kv_ref, kvp_ref, c_ref, s_ref, cp_ref, sp_ref):
    c, s = c_ref[b], s_ref[b]
    q = _rope(q_ref[b], _tile4(c), _tile4(s)) * ATTN_SCALE
    kv, kvp = kv_ref[b], kvp_ref[b]
    k = _rope(kv[:, :KV_WIDTH], c, s)
    kp = _rope(kvp[:, :KV_WIDTH], cp_ref[b], sp_ref[b])
    kb = jnp.concatenate([kp, k], axis=0)
    vb = jnp.concatenate([kvp[:, KV_WIDTH:], kv[:, KV_WIDTH:]], axis=0)
    return q, kb, vb


def _attention_fwd(z, ctab, stab, sinks, n_seq):
    t = z.shape[0]
    seq = t // n_seq

    def body(q_ref, kv_ref, kvp_ref, c_ref, s_ref, cp_ref, sp_ref, sink_ref, o_ref, lse_ref):
        mask = _attn_mask(pl.program_id(0))
        sink = sink_ref[...]
        lane = lax.broadcasted_iota(jnp.int32, (BLOCK, N_Q_HEADS), 1)
        for b in range(n_seq):
            q, kb, vb = _attn_band(b, q_ref, kv_ref, kvp_ref, c_ref, s_ref, cp_ref, sp_ref)
            lse_all = jnp.zeros((BLOCK, N_Q_HEADS), F32)
            for kh in range(N_KV_HEADS):
                sc = jnp.where(mask, _dot_nt(_stack_heads(q, kh), kb[:, kh * HEAD_DIM:(kh + 1) * HEAD_DIM]), NEG_INF)
                sk = _stack_sinks(sink, kh)
                m = jnp.maximum(jnp.max(sc, axis=-1, keepdims=True), sk)
                pr = jnp.exp(sc - m)
                den = jnp.sum(pr, axis=-1, keepdims=True) + jnp.exp(sk - m)
                out = _dot_nn(pr * (1.0 / den), vb[:, kh * HEAD_DIM:(kh + 1) * HEAD_DIM])
                lse = m + jnp.log(den)
                for g in range(GQA_GROUP):
                    h = kh * GQA_GROUP + g
                    o_ref[b, :, h * HEAD_DIM:(h + 1) * HEAD_DIM] = out[g * BLOCK:(g + 1) * BLOCK].astype(o_ref.dtype)
                    lse_all = jnp.where(lane == h, lse[g * BLOCK:(g + 1) * BLOCK], lse_all)
            lse_ref[b] = lse_all

    rows = lambda w: pl.BlockSpec((n_seq, BLOCK, w), lambda n: (0, n, 0))
    z3, c3, s3 = _by_seq(z, n_seq), _by_seq(ctab, n_seq), _by_seq(stab, n_seq)
    ya, lse = pl.pallas_call(
        body, name="attn_fwd", grid=(seq // BLOCK,),
        out_shape=[jax.ShapeDtypeStruct((n_seq, seq, Q_WIDTH), MXU_DTYPE), jax.ShapeDtypeStruct((n_seq, seq, N_Q_HEADS), F32)],
        in_specs=_attn_specs(n_seq), out_specs=[rows(Q_WIDTH), rows(N_Q_HEADS)],
        compiler_params=_params(("parallel",)))(z3, z3, z3, c3, s3, c3, s3, sinks)
    return ya.reshape(t, Q_WIDTH), lse.reshape(t, N_Q_HEADS)


def _attention_bwd(z, ctab, stab, sinks, ya, lse, dya, dz, n_seq):
    t = z.shape[0]
    seq = t // n_seq

    def body(q_ref, kv_ref, kvp_ref, c_ref, s_ref, cp_ref, sp_ref, sink_ref, o_ref, lse_ref, do_ref, _,
             dq_ref, dkv_ref, dkvp_ref, dsink_ref):
        mask = _attn_mask(pl.program_id(0))
        sink = sink_ref[...]
        lane = lax.broadcasted_iota(jnp.int32, (1, N_Q_HEADS), 1)
        dsink = jnp.zeros((1, N_Q_HEADS), F32)
        for b in range(n_seq):
            q, kb, vb = _attn_band(b, q_ref, kv_ref, kvp_ref, c_ref, s_ref, cp_ref, sp_ref)
            lse_all = lse_ref[b]
            o = o_ref[b].astype(F32)
            do = do_ref[b].astype(F32)
            dq_parts = []
            dk_parts, dv_parts = [], []
            for kh in range(N_KV_HEADS):
                kbh = kb[:, kh * HEAD_DIM:(kh + 1) * HEAD_DIM]
                vbh = vb[:, kh * HEAD_DIM:(kh + 1) * HEAD_DIM]
                qs, dos = _stack_heads(q, kh), _stack_heads(do, kh)
                lse = jnp.concatenate([lse_all[:, kh * GQA_GROUP + g:kh * GQA_GROUP + g + 1] for g in range(GQA_GROUP)], axis=0)
                pr = jnp.exp(jnp.where(mask, _dot_nt(qs, kbh), NEG_INF) - lse)
                delta = jnp.sum(dos * _stack_heads(o, kh), axis=-1, keepdims=True)
                ds = pr * (_dot_nt(dos, vbh) - delta)
                dqs = _dot_nn(ds, kbh)
                dq_parts += [dqs[g * BLOCK:(g + 1) * BLOCK] for g in range(GQA_GROUP)]
                dk_parts.append(_dot_tn(ds, qs))
                dv_parts.append(_dot_tn(pr, dos))
                dsk = jnp.exp(_stack_sinks(sink, kh) - lse) * delta
                for g in range(GQA_GROUP):
                    dsink = dsink + jnp.where(lane == kh * GQA_GROUP + g, -jnp.sum(dsk[g * BLOCK:(g + 1) * BLOCK]), 0.0)
            c, s = c_ref[b], s_ref[b]
            dq_ref[b] = _rope_t(jnp.concatenate(dq_parts, axis=1) * ATTN_SCALE, _tile4(c), _tile4(s)).astype(dq_ref.dtype)
            dk = jnp.concatenate(dk_parts, axis=1)
            dv = jnp.concatenate(dv_parts, axis=1)
            dkv_ref[b, :, :KV_WIDTH] = _rope_t(dk[BLOCK:], c, s)
            dkv_ref[b, :, KV_WIDTH:] = dv[BLOCK:]
            dkvp_ref[b, :, :KV_WIDTH] = _rope_t(dk[:BLOCK], cp_ref[b], sp_ref[b])
            dkvp_ref[b, :, KV_WIDTH:] = dv[:BLOCK]

        @pl.when(pl.program_id(0) == 0)
        def _():
            dsink_ref[...] = jnp.zeros_like(dsink_ref)

        dsink_ref[...] += dsink

    rows = lambda w: pl.BlockSpec((n_seq, BLOCK, w), lambda n: (0, n, 0))
    by_seq = lambda a: _by_seq(a, n_seq)
    z3, c3, s3 = by_seq(z), by_seq(ctab), by_seq(stab)
    dz, dkv, dkvp, dsink = pl.pallas_call(
        body, name="attn_bwd", grid=(seq // BLOCK,),
        out_shape=[jax.ShapeDtypeStruct((n_seq, seq, Z_WIDTH), dz.dtype), jax.ShapeDtypeStruct((n_seq, seq, ZKV_W), F32),
                   jax.ShapeDtypeStruct((n_seq, seq, ZKV_W), F32), jax.ShapeDtypeStruct((1, N_Q_HEADS), F32)],
        in_specs=_attn_specs(n_seq) + [rows(Q_WIDTH), rows(N_Q_HEADS), rows(Q_WIDTH), _ANY],
        out_specs=[pl.BlockSpec((n_seq, BLOCK, ZQ_W), lambda n: (0, n, ZQ_BLK)), rows(ZKV_W), rows(ZKV_W),
                   pl.BlockSpec((1, N_Q_HEADS), lambda n: (0, 0))],
        input_output_aliases={11: 0},
        compiler_params=_params(("arbitrary",)))(z3, z3, z3, c3, s3, c3, s3, sinks, by_seq(ya), by_seq(lse), by_seq(dya), by_seq(dz))
    return dz.reshape(t, Z_WIDTH), dkv.reshape(t, ZKV_W), dkvp.reshape(t, ZKV_W), dsink


def _kv_combine(dkv, dkvp, dz, n_seq):
    t = dkv.shape[0]
    seq = t // n_seq
    rows = _pick(seq, (512, 256, 128))
    nt, per = seq // rows, rows // BLOCK
    n_blocks = t // BLOCK

    def body(dkv_ref, dkvp_ref, dkvn_ref, _, o_ref):
        nxt = jnp.where(pl.program_id(1) == nt - 1, 0.0, dkvn_ref[...])
        shifted = nxt if per == 1 else jnp.concatenate([dkvp_ref[BLOCK:, :], nxt], axis=0)
        o_ref[...] = (dkv_ref[...] + shifted).astype(o_ref.dtype)

    tile = pl.BlockSpec((rows, ZKV_W), lambda b, i: (b * nt + i, 0))
    return pl.pallas_call(
        body, name="kv_combine", grid=(n_seq, nt), out_shape=jax.ShapeDtypeStruct(dz.shape, dz.dtype),
        in_specs=[tile, tile,
                  pl.BlockSpec((BLOCK, ZKV_W), lambda b, i: (jnp.minimum((b * nt + i + 1) * per, n_blocks - 1), 0)), _ANY],
        out_specs=pl.BlockSpec((rows, ZKV_W), lambda b, i: (b * nt + i, ZKV_BLK)), input_output_aliases={3: 0},
        compiler_params=_params(("parallel", "parallel")))(dkv, dkvp, dkvp, dz)


def _ssm_coeff_tile(lam_re, lam_im, log_dt):
    lr = jnp.minimum(lam_re, -1e-4)
    dt = jnp.exp(log_dt)
    mag = jnp.exp(lr * dt)
    a_re = mag * jnp.cos(lam_im * dt)
    a_im = mag * jnp.sin(lam_im * dt)
    den = lr * lr + lam_im * lam_im
    x_re = a_re - 1.0
    f_re = (x_re * lr + a_im * lam_im) / den
    f_im = (a_im * lr - x_re * lam_im) / den
    return a_re, a_im, f_re, f_im


def _ssm_coeffs(lam_re, lam_im, log_dt):
    def body(lr_ref, li_ref, dt_ref, *o_refs):
        for r, v in zip(o_refs, _ssm_coeff_tile(lr_ref[...], li_ref[...], dt_ref[...]), strict=True):
            r[...] = v

    return pl.pallas_call(body, name="ssm_coeffs", out_shape=[jax.ShapeDtypeStruct(lam_re.shape, F32)] * 4)(
        lam_re, lam_im, log_dt)


def _ssm_coeffs_bwd(lam_re, lam_im, log_dt, cts):
    def body(lr_ref, li_ref, dt_ref, c0, c1, c2, c3, dlr_ref, dli_ref, ddt_ref):
        _, vjp = jax.vjp(_ssm_coeff_tile, lr_ref[...], li_ref[...], dt_ref[...])
        dlr, dli, ddt = vjp((c0[...], c1[...], c2[...], c3[...]))
        dlr_ref[...] = dlr
        dli_ref[...] = dli
        ddt_ref[...] = ddt

    return pl.pallas_call(
        body, name="ssm_coeffs_bwd",
        out_shape=[jax.ShapeDtypeStruct(lam_re.shape, F32)] * 2 + [jax.ShapeDtypeStruct(log_dt.shape, F32)])(
        lam_re, lam_im, log_dt, *cts)


def _ssm_chunk(t):
    return _pick(t, (256, 128))


def _ssm_fwd(z, bmat, a_row, f_row, cmat, d_row, n_seq):
    t = z.shape[0]
    seq = t // n_seq
    lc = _ssm_chunk(seq)
    nc = seq // lc
    n2 = 2 * SSM_LANES

    re, im = pl.ds(0, SSM_LANES), pl.ds(SSM_LANES, SSM_LANES)

    def body(u_ref, b_ref, a_ref, f_ref, c_ref, d_ref, y_ref, s_ref, bu_ref, st_ref):
        @pl.when(pl.program_id(0) == 0)
        def _():
            st_ref[...] = jnp.zeros_like(st_ref)

        fr, fi = f_ref[:, :SSM_LANES], f_ref[:, SSM_LANES:]
        for b in range(n_seq):
            proj = _dot_nn(u_ref[b], b_ref[...])
            pr, pi = proj[:, :SSM_LANES], proj[:, SSM_LANES:]
            bu_ref[b, :, :SSM_LANES] = fr * pr - fi * pi
            bu_ref[b, :, SSM_LANES:] = fr * pi + fi * pr
        ar, ai = a_ref[:, :SSM_LANES], a_ref[:, SSM_LANES:]

        def step(i, carry):
            out = []
            for b in range(n_seq):
                sr, si = carry[2 * b], carry[2 * b + 1]
                nr = ar * sr - ai * si + bu_ref[b, pl.ds(i, 1), re]
                ni = ar * si + ai * sr + bu_ref[b, pl.ds(i, 1), im]
                s_ref[b, pl.ds(i, 1), re] = nr
                s_ref[b, pl.ds(i, 1), im] = ni
                out += [nr, ni]
            return tuple(out)

        carry = lax.fori_loop(0, lc, step, tuple(st_ref[b, 0:1, part] for b in range(n_seq) for part in (re, im)), unroll=8)
        for b in range(n_seq):
            st_ref[b, 0:1, re], st_ref[b, 0:1, im] = carry[2 * b], carry[2 * b + 1]
            y_ref[b] = _dot_nn(s_ref[b], c_ref[...]) + d_ref[...] * u_ref[b]

    const = lambda shape: pl.BlockSpec(shape, lambda c: (0, 0))
    rows = lambda w, cb: pl.BlockSpec((n_seq, lc, w), lambda c: (0, c, cb))
    ys, states = pl.pallas_call(
        body, name="ssm_fwd", grid=(nc,),
        out_shape=[jax.ShapeDtypeStruct((n_seq, seq, SSM_WIDTH), F32), jax.ShapeDtypeStruct((n_seq, seq, n2), F32)],
        in_specs=[rows(ZS_W, ZS_BLK), const((SSM_WIDTH, n2)), const((1, n2)), const((1, n2)), const((n2, SSM_WIDTH)),
                  const((1, SSM_WIDTH))],
        out_specs=[rows(SSM_WIDTH, 0), rows(n2, 0)],
        scratch_shapes=[pltpu.VMEM((n_seq, lc, n2), F32), pltpu.VMEM((n_seq, 8, n2), F32)],
        compiler_params=_params(("arbitrary",)))(_by_seq(z, n_seq), bmat, a_row, f_row, cmat, d_row)
    return ys.reshape(t, SSM_WIDTH), states.reshape(t, n2)


def _ssm_bwd(z, states, dy, bmat, a_row, f_row, cmat, d_row, dz, n_seq):
    t = z.shape[0]
    seq = t // n_seq
    lc = _ssm_chunk(seq)
    nc = seq // lc
    n2 = 2 * SSM_LANES

    def body(dy_ref, u_ref, s_ref, b_ref, a_ref, f_ref, c_ref, d_ref, _,
             du_ref, db_ref, dc_ref, da_ref, df_ref, dd_ref, g_ref, carry_ref):
        @pl.when(pl.program_id(0) == 0)
        def _():
            for r in (db_ref, dc_ref, da_ref, df_ref, dd_ref, carry_ref):
                r[...] = jnp.zeros_like(r)

        re, im = pl.ds(0, SSM_LANES), pl.ds(SSM_LANES, SSM_LANES)
        for b in range(n_seq):
            dy = dy_ref[b]
            g_ref[b, 0:lc, :] = _dot_nt(dy, c_ref[...])
            g_ref[b, lc:lc + 8, :] = carry_ref[b]
            dc_ref[...] += _dot_tn(s_ref[b], dy)
            dd_ref[...] += jnp.sum(dy * u_ref[b], axis=0, keepdims=True)
        ar, ai = a_ref[:, :SSM_LANES], a_ref[:, SSM_LANES:]

        def step(i, carry):
            r = lc - 1 - i
            out = []
            for b in range(n_seq):
                gr, gi = carry[2 * b], carry[2 * b + 1]
                nr = g_ref[b, pl.ds(r, 1), re] + ar * gr + ai * gi
                ni = g_ref[b, pl.ds(r, 1), im] - ai * gr + ar * gi
                g_ref[b, pl.ds(r, 1), re] = nr
                g_ref[b, pl.ds(r, 1), im] = ni
                out += [nr, ni]
            return tuple(out)

        carry = lax.fori_loop(0, lc, step, tuple(carry_ref[b, 0:1, part] for b in range(n_seq) for part in (re, im)), unroll=8)
        fr, fi = f_ref[:, :SSM_LANES], f_ref[:, SSM_LANES:]
        for b in range(n_seq):
            carry_ref[b, 0:1, re], carry_ref[b, 0:1, im] = carry[2 * b], carry[2 * b + 1]
            dy, u, st = dy_ref[b], u_ref[b], s_ref[b]
            sr, si = st[:, :SSM_LANES], st[:, SSM_LANES:]
            gnr, gni = g_ref[b, pl.ds(1, lc), re], g_ref[b, pl.ds(1, lc), im]
            da_ref[:, :SSM_LANES] += jnp.sum(gnr * sr + gni * si, axis=0, keepdims=True)
            da_ref[:, SSM_LANES:] += jnp.sum(gni * sr - gnr * si, axis=0, keepdims=True)
            gr_all, gi_all = g_ref[b, 0:lc, :SSM_LANES], g_ref[b, 0:lc, SSM_LANES:]
            proj = _dot_nn(u, b_ref[...])
            pr, pi = proj[:, :SSM_LANES], proj[:, SSM_LANES:]
            df_ref[:, :SSM_LANES] += jnp.sum(gr_all * pr + gi_all * pi, axis=0, keepdims=True)
            df_ref[:, SSM_LANES:] += jnp.sum(gi_all * pr - gr_all * pi, axis=0, keepdims=True)
            dproj = jnp.concatenate([fr * gr_all + fi * gi_all, fr * gi_all - fi * gr_all], axis=1).astype(MXU_DTYPE)
            du_ref[b] = (_dot_nt(dproj, b_ref[...]) + d_ref[...] * dy).astype(du_ref.dtype)
            db_ref[...] += _dot_tn(u, dproj)

    const = lambda shape: pl.BlockSpec(shape, lambda c: (0, 0))
    rows = lambda w, cb: pl.BlockSpec((n_seq, lc, w), lambda c: (0, nc - 1 - c, cb))
    by_seq = lambda a: _by_seq(a, n_seq)
    dz, *sums = pl.pallas_call(
        body, name="ssm_bwd", grid=(nc,),
        out_shape=[jax.ShapeDtypeStruct((n_seq, seq, Z_WIDTH), dz.dtype), jax.ShapeDtypeStruct((SSM_WIDTH, n2), F32),
                   jax.ShapeDtypeStruct((n2, SSM_WIDTH), F32), jax.ShapeDtypeStruct((1, n2), F32),
                   jax.ShapeDtypeStruct((1, n2), F32), jax.ShapeDtypeStruct((1, SSM_WIDTH), F32)],
        in_specs=[rows(SSM_WIDTH, 0), rows(ZS_W, ZS_BLK), rows(n2, 0), const((SSM_WIDTH, n2)), const((1, n2)),
                  const((1, n2)), const((n2, SSM_WIDTH)), const((1, SSM_WIDTH)), _ANY],
        out_specs=[rows(ZS_W, ZS_BLK), const((SSM_WIDTH, n2)), const((n2, SSM_WIDTH)), const((1, n2)), const((1, n2)),
                   const((1, SSM_WIDTH))],
        input_output_aliases={8: 0},
        scratch_shapes=[pltpu.VMEM((n_seq, lc + 8, n2), F32), pltpu.VMEM((n_seq, 8, n2), F32)],
        compiler_params=_params(("arbitrary",)))(by_seq(dy), by_seq(z), by_seq(states), bmat, a_row, f_row, cmat, d_row, by_seq(dz))
    return (dz.reshape(t, Z_WIDTH), *sums)


def _conv_chunk(t):
    return _pick(t, (512, 256, 128))


def _glu(c):
    return c[:, :CONV_WIDTH] * jax.nn.sigmoid(c[:, CONV_WIDTH:])


def _conv_specs(lc, nc):
    per = lc // CONV_HALO
    return [pl.BlockSpec((lc, ZC_W), lambda b, c: (b * nc + c, ZC_BLK)),
            pl.BlockSpec((CONV_HALO, ZC_W), lambda b, c: (jnp.maximum((b * nc + c) * per - 1, 0), ZC_BLK))]


def _conv_fill(c_ref, cp_ref, ue_ref, lc):
    ue_ref[0:CONV_HALO, :] = jnp.where(pl.program_id(1) > 0, _glu(cp_ref[...]), 0.0)
    ue_ref[CONV_HALO:CONV_HALO + lc, :] = _glu(c_ref[...])


CONV_ROWS = 64


def _conv_apply(ue_ref, w_ref, b_ref, o_ref, lc):
    for r0 in range(0, lc, CONV_ROWS):
        acc = jnp.zeros((CONV_ROWS, CONV_WIDTH), F32) + b_ref[...]
        for k in range(CONV_K):
            acc = acc + w_ref[k:k + 1, :] * ue_ref[pl.ds(r0 + k + CONV_HALO - CONV_K + 1, CONV_ROWS), :]
        o_ref[r0:r0 + CONV_ROWS, :] = acc


def _conv_fwd(z, dw_w, dw_b, n_seq):
    t = z.shape[0]
    seq = t // n_seq
    lc = _conv_chunk(seq)
    nc = seq // lc

    def body(c_ref, cp_ref, w_ref, b_ref, o_ref, ue_ref):
        _conv_fill(c_ref, cp_ref, ue_ref, lc)
        _conv_apply(ue_ref, w_ref, b_ref, o_ref, lc)

    const = lambda a: pl.BlockSpec(a.shape, lambda b, c: (0, 0))
    return pl.pallas_call(
        body, name="conv_fwd", grid=(n_seq, nc), out_shape=jax.ShapeDtypeStruct((t, CONV_WIDTH), F32),
        in_specs=_conv_specs(lc, nc) + [const(dw_w), const(dw_b)],
        out_specs=pl.BlockSpec((lc, CONV_WIDTH), lambda b, c: (b * nc + c, 0)),
        scratch_shapes=[pltpu.VMEM((CONV_HALO + lc, CONV_WIDTH), F32)],
        compiler_params=_params(("parallel", "parallel")))(z, z, dw_w, dw_b)


def _conv_bwd_taps(z, dv, dw_w, dz, n_seq):
    t = z.shape[0]
    seq = t // n_seq
    lc = _conv_chunk(seq)
    nc = seq // lc
    per = lc // CONV_HALO
    n_halo = t // CONV_HALO

    def body(c_ref, cp_ref, dv_ref, dvn_ref, w_ref, _, dc_ref, dw_ref, ue_ref, dve_ref):
        @pl.when((pl.program_id(0) == 0) & (pl.program_id(1) == 0))
        def _():
            dw_ref[...] = jnp.zeros_like(dw_ref)

        _conv_fill(c_ref, cp_ref, ue_ref, lc)
        dv = dv_ref[...]
        dve_ref[0:lc, :] = dv
        dve_ref[lc:lc + CONV_HALO, :] = jnp.where(pl.program_id(1) < nc - 1, dvn_ref[...], 0.0)
        dw_ref[CONV_K:CONV_K + 1, :] += jnp.sum(dv, axis=0, keepdims=True)
        for r0 in range(0, lc, CONV_ROWS):
            rows = pl.ds(r0, CONV_ROWS)
            dv_rows = dv_ref[rows, :]
            du = jnp.zeros((CONV_ROWS, CONV_WIDTH), F32)
            for k in range(CONV_K):
                du = du + w_ref[k:k + 1, :] * dve_ref[pl.ds(r0 + CONV_K - 1 - k, CONV_ROWS), :]
                taps = ue_ref[pl.ds(r0 + k + CONV_HALO - CONV_K + 1, CONV_ROWS), :]
                dw_ref[k:k + 1, :] += jnp.sum(dv_rows * taps, axis=0, keepdims=True)
            c = c_ref[rows, :]
            a, sg = c[:, :CONV_WIDTH], jax.nn.sigmoid(c[:, CONV_WIDTH:])
            dc_ref[rows, :CONV_WIDTH] = (du * sg).astype(dc_ref.dtype)
            dc_ref[rows, CONV_WIDTH:] = (du * a * sg * (1.0 - sg)).astype(dc_ref.dtype)

    return pl.pallas_call(
        body, name="conv_bwd_taps", grid=(n_seq, nc),
        out_shape=[jax.ShapeDtypeStruct(dz.shape, dz.dtype), jax.ShapeDtypeStruct((CONV_HALO, CONV_WIDTH), F32)],
        in_specs=_conv_specs(lc, nc) + [
            pl.BlockSpec((lc, CONV_WIDTH), lambda b, c: (b * nc + c, 0)),
            pl.BlockSpec((CONV_HALO, CONV_WIDTH), lambda b, c: (jnp.minimum((b * nc + c + 1) * per, n_halo - 1), 0)),
            pl.BlockSpec(dw_w.shape, lambda b, c: (0, 0)), _ANY],
        out_specs=[pl.BlockSpec((lc, ZC_W), lambda b, c: (b * nc + c, ZC_BLK)),
                   pl.BlockSpec((CONV_HALO, CONV_WIDTH), lambda b, c: (0, 0))],
        input_output_aliases={5: 0},
        scratch_shapes=[pltpu.VMEM((CONV_HALO + lc, CONV_WIDTH), F32), pltpu.VMEM((lc + CONV_HALO, CONV_WIDTH), F32)],
        compiler_params=_params(("arbitrary", "arbitrary")))(z, z, dv, dv, dw_w, dz)


def _row(v):
    return v.reshape(1, -1)


def _ssm_mats(b_re, b_im, c_re, c_im):
    eye = jnp.eye(SSM_GROUPS, dtype=bool)
    bm = jnp.stack([b_re, b_im]).transpose(1, 3, 0, 2)[:, :, :, None, :]
    bmat = jnp.where(eye[:, None, None, :, None], bm, 0.0).reshape(SSM_WIDTH, 2 * SSM_LANES)
    cm = jnp.stack([c_re, -c_im]).transpose(0, 1, 3, 2)[:, :, :, None, :]
    cmat = jnp.where(eye[None, :, None, :, None], cm, 0.0).reshape(2 * SSM_LANES, SSM_WIDTH)
    return bmat.astype(MXU_DTYPE), cmat.astype(MXU_DTYPE)


def _ssm_mats_t(dbmat, dcmat):
    eye = jnp.eye(SSM_GROUPS, dtype=bool)
    db = dbmat.reshape(SSM_GROUPS, SSM_GROUP, 2, SSM_GROUPS, SSM_STATE)
    db = jnp.sum(jnp.where(eye[:, None, None, :, None], db, 0.0), axis=3).transpose(2, 0, 3, 1)
    dc = dcmat.reshape(2, SSM_GROUPS, SSM_STATE, SSM_GROUPS, SSM_GROUP)
    dc = jnp.sum(jnp.where(eye[None, :, None, :, None], dc, 0.0), axis=3).transpose(0, 1, 3, 2)
    return db[0], db[1], dc[0], -dc[1]


def _layer_fwd(x, p, w, get_ffn_weights, sp, ctab, stab, n_seq):
    z, h = _in_proj(x, sp["mix_norm_g"], w["w_in"])
    ya, lse = _attention_fwd(z, ctab, stab, sp["attn_sinks"], n_seq)
    ys, states = _ssm_fwd(z, sp["bmat"], sp["a_row"], sp["f_row"], sp["cmat"], sp["ssm_d"], n_seq)
    v = _conv_fwd(z, w["conv_dw_w"], sp["conv_dw_b"], n_seq)
    merge_consts = [w["w_attn_out"], w["w_ssm_glu"], sp["b_ssm_glu"], sp["conv_norm_g"], sp["conv_norm_b"], w["w_conv_out"],
                    sp["b_gate"], w["w_mix_out"], sp["ffn_norm_g"]]
    (x1, hf), _ = _token_call("merge", lambda *a: (list(_merge_tile(*_f32s(a))), []), 512,
                              [_whole(x), _whole(ya), _whole(ys), _whole(v), (z, ZG_W, 0)], merge_consts,
                              [(x.shape[1], F32), (x.shape[1], MXU_DTYPE)], [])
    w = dict(w, **get_ffn_weights(x1))
    gate, up, act = _ffn_in_act(hf, w["w_ffn_in"])
    ffn = _matmul_nn("mm_ffn_out", act, w["w_ffn_out"], F32)

    def ple_fn(x1, ffn, p, w_pi, g_ple, w_pg):
        x2 = x1 + ffn
        return [x2, *_ple_tile(x2, p, w_pi, g_ple, w_pg)], []

    d = x.shape[1]
    (x2, x3, ple_pre, ple_e), _ = _token_call(
        "ple", ple_fn, 512, [_whole(x1), _whole(ffn), _whole(p)], [w["w_ple_in"], sp["ple_norm_g"], w["w_ple_gate"]],
        [(d, F32), (d, F32), (d, MXU_DTYPE), (d, MXU_DTYPE)], [])
    saved = dict(h=h, z=z, ya=ya, lse=lse, ys=ys, states=states, v=v, x1=x1, hf=hf, gate=gate, up=up, act=act, x2=x2, p=p,
                 ple_pre=ple_pre, ple_e=ple_e, x=x)
    return x3, saved, w


def _layer_bwd(dx3, sv, w, sp, on_grads, ctab, stab, n_seq):
    d = dx3.shape[1]
    gw, gs = {}, {}

    def ple_bwd(*a):
        dx2, accs = _ple_bwd_tile(*_f32s(a))
        return [dx2, dx2], accs

    (dx2, dffn), (gw["w_ple_in"], gs["ple_norm_g"], gw["w_ple_gate"]) = _token_call(
        "ple_bwd", ple_bwd, 512, [_whole(sv["x2"]), _whole(sv["p"]), _whole(sv["ple_pre"]), _whole(sv["ple_e"]), _whole(dx3)],
        [w["w_ple_in"], sp["ple_norm_g"], w["w_ple_gate"]], [(d, F32), (d, MXU_DTYPE)],
        [w["w_ple_in"].shape, (1, d), w["w_ple_gate"].shape])

    dgate, dup = _ffn_mid_bwd(dffn, w["w_ffn_out"], sv["gate"], sv["up"])
    gw["w_ffn_out"] = _matmul_tn("mm_ffn_out_dw", sv["act"], dffn)
    dhf = _ffn_in_dx(dgate, dup, w["w_ffn_in"])
    gw["w_ffn_in"] = jnp.concatenate([_matmul_tn("mm_ffn_gate_dw", sv["hf"], dgate), _matmul_tn("mm_ffn_up_dw", sv["hf"], dup)],
                                     axis=1)

    token = on_grads("ffn", gw)

    def merge_bwd(*a):
        return _merge_bwd_tile(*_f32s(a))

    b_gate = sp["b_gate"] if token is None else sp["b_gate"] + token[0:1, 0:1]
    merge_consts = [w["w_attn_out"], w["w_ssm_glu"], sp["b_ssm_glu"], sp["conv_norm_g"], sp["conv_norm_b"], w["w_conv_out"],
                    b_gate, w["w_mix_out"], sp["ffn_norm_g"]]
    dz = lax.empty(sv["z"].shape, MXU_DTYPE)
    (dx_res, dya, dys, dv, dz), macc = _token_call(
        "merge_bwd", merge_bwd, 256,
        [_whole(sv["x1"]), _whole(sv["ya"]), _whole(sv["ys"]), _whole(sv["v"]), (sv["z"], ZG_W, 0), _whole(dx2), _whole(dhf)],
        merge_consts, [(d, F32), (Q_WIDTH, MXU_DTYPE), (SSM_WIDTH, F32), (CONV_WIDTH, F32)],
        [c.shape for c in merge_consts], into=(dz, ZG_W, 0))
    (gw["w_attn_out"], gw["w_ssm_glu"], gs["b_ssm_glu"], gs["conv_norm_g"], gs["conv_norm_b"], gw["w_conv_out"], gs["b_gate"],
     gw["w_mix_out"], gs["ffn_norm_g"]) = macc

    dz, dw_taps = _conv_bwd_taps(sv["z"], dv, w["conv_dw_w"], dz, n_seq)
    gw["conv_dw_w"], gs["conv_dw_b"] = dw_taps[:CONV_K], dw_taps[CONV_K:]

    dz, gs["bmat"], gs["cmat"], gs["a_row"], gs["f_row"], gs["ssm_d"] = _ssm_bwd(
        sv["z"], sv["states"], dys, sp["bmat"], sp["a_row"], sp["f_row"], sp["cmat"], sp["ssm_d"], dz, n_seq)

    dz, dkv, dkvp, gs["attn_sinks"] = _attention_bwd(sv["z"], ctab, stab, sp["attn_sinks"], sv["ya"], sv["lse"], dya, dz, n_seq)
    dz = _kv_combine(dkv, dkvp, dz, n_seq)
    gw["w_in"] = _matmul_tn("mm_in_dw", sv["h"], dz)
    token = on_grads("mix", gw)
    g_in = sp["mix_norm_g"] if token is None else sp["mix_norm_g"] + token[0:1, 0:1]
    dx, gs["mix_norm_g"] = _in_proj_bwd(dz, w["w_in"], sv["x"], dx_res, g_in)
    return dx, gw, gs


def _loss_and_grad(x, target, g):
    def fn(x, tgt, g):
        def f(x, g):
            err = _rms(x, g) - tgt
            return 0.5 * jnp.mean(err * err, axis=-1, keepdims=True)

        per_token, vjp = jax.vjp(f, x, g)
        dx, dg = vjp(jnp.ones_like(per_token))
        return [dx], [jnp.sum(per_token, axis=0, keepdims=True), dg]

    (dx,), (loss, dg) = _token_call("loss", fn, 512, [_whole(x), _whole(target)], [g], [(x.shape[1], F32)],
                                    [(8, LANES), (1, x.shape[1])])
    return loss[0, 0], dx, dg


def _mesh_place():
    return lax.axis_index("x"), lax.axis_index("y"), lax.axis_index("c")


def _flip(v, bit):
    return 1 - v if bit else v


_MESH = pl.DeviceIdType.MESH


def _all_gather(name, xs):
    n = len(xs)

    def body(*refs):
        x_refs, out_refs = refs[:n], refs[n:2 * n]
        send_sems, recv_sems, local_sems = refs[2 * n:]
        mx, my, mc = _mesh_place()
        me, sibling = (mx, my, mc), (mx, my, 1 - mc)
        chips = [(1 - mx, my), (mx, 1 - my), (1 - mx, 1 - my)]

        def slot(a, px, py, pc):
            return out_refs[a].at[4 * px + 2 * py + pc]

        def copy(a, k, block, to, src=None):
            return pltpu.make_async_remote_copy(
                src_ref=slot(a, *block) if src is None else src, dst_ref=slot(a, *block), send_sem=send_sems.at[7 * a + k],
                recv_sem=recv_sems.at[7 * a + k], device_id=to, device_id_type=_MESH)

        mine = [pltpu.make_async_copy(x_refs[a], slot(a, *me), local_sems.at[a]) for a in range(n)]
        for cp in mine:
            cp.start()
        first = [copy(a, 0, me, sibling, src=x_refs[a]) for a in range(n)]
        first += [copy(a, 1 + j, me, (*chip, mc), src=x_refs[a]) for j, chip in enumerate(chips) for a in range(n)]
        for cp in first:
            cp.start()
        passed = []
        for j, chip in enumerate(chips):
            for a in range(n):
                copy(a, 1 + j, (*chip, mc), me).wait_recv()
                passed.append(copy(a, 4 + j, (*chip, mc), sibling))
                passed[-1].start()
        for a in range(n):
            copy(a, 0, sibling, me).wait_recv()
            for j, chip in enumerate(chips):
                copy(a, 4 + j, (*chip, 1 - mc), me).wait_recv()
        for cp in first + passed:
            cp.wait_send()
        for cp in mine:
            cp.wait()

    return pl.pallas_call(
        body, name=name, out_shape=[jax.ShapeDtypeStruct((N_DEV,) + x.shape, x.dtype) for x in xs], in_specs=[_ANY] * n,
        out_specs=[_ANY] * n,
        scratch_shapes=[pltpu.SemaphoreType.DMA((7 * n,)), pltpu.SemaphoreType.DMA((7 * n,)), pltpu.SemaphoreType.DMA((n,))])(*xs)


def _direct_copies(kind, src_refs, land_refs, send_sems, recv_sems, local_sems):
    mx, my, mc = _mesh_place()
    me = 4 * mx + 2 * my + mc
    n = len(src_refs)
    own = [pltpu.make_async_copy(src_refs[a] if kind == "gather" else src_refs[a].at[me], land_refs[a].at[me], local_sems.at[a])
           for a in range(n)]
    copies = []
    for rel in range(1, N_DEV):
        px, py, pc = _flip(mx, rel & 4), _flip(my, rel & 2), _flip(mc, rel & 1)
        for a in range(n):
            src = src_refs[a] if kind == "gather" else src_refs[a].at[4 * px + 2 * py + pc]
            copies.append(pltpu.make_async_remote_copy(
                src_ref=src, dst_ref=land_refs[a].at[me], send_sem=send_sems.at[7 * a + rel - 1],
                recv_sem=recv_sems.at[7 * a + rel - 1], device_id=(px, py, pc), device_id_type=_MESH))
    return copies, own


_HBM = pl.BlockSpec(memory_space=pltpu.HBM)
_SEM = pl.BlockSpec(memory_space=pltpu.SEMAPHORE)
_DATAFLOW = pltpu.SideEffectType.DATAFLOW_SIDE_EFFECTING


def _exchange_start(name, kind, groups):
    sizes = [len(g) for g in groups]
    srcs = [s for g in groups for s in g]
    lands = [lax.empty(((N_DEV,) + s.shape) if kind == "gather" else s.shape, s.dtype) for s in srcs]
    n, n_g = len(srcs), len(groups)
    first = [sum(sizes[:g]) for g in range(n_g)]

    def body(*refs):
        src_refs, land_refs, sems = refs[:n], refs[n:2 * n], refs[2 * n:2 * n + 3 * n_g]
        for g in range(n_g):
            span = slice(first[g], first[g] + sizes[g])
            copies, own = _direct_copies(kind, src_refs[span], land_refs[span], *sems[3 * g:3 * g + 3])
            for cp in own + copies:
                cp.start()
        refs[-1][...] = jnp.zeros_like(refs[-1])

    hbm = lambda a: pltpu.with_memory_space_constraint(a, pltpu.HBM)
    sem_shapes = [pltpu.SemaphoreType.DMA((k * m,)) for m in sizes for k in (7, 7, 1)]
    out = pl.pallas_call(
        body, name=name,
        out_shape=sem_shapes + [pltpu.HBM(a.shape, a.dtype) for a in srcs + lands] + [jax.ShapeDtypeStruct((8, LANES), F32)],
        in_specs=[_HBM] * (2 * n), out_specs=[_SEM] * (3 * n_g) + [_HBM] * (2 * n) + [pl.BlockSpec(memory_space=pltpu.VMEM)],
        input_output_aliases={i: 3 * n_g + i for i in range(2 * n)},
        compiler_params=pltpu.CompilerParams(has_side_effects=_DATAFLOW))(*[hbm(a) for a in srcs + lands])
    sems, arrays = out[:3 * n_g], out[3 * n_g:-1]
    started = [(kind, (*sems[3 * g:3 * g + 3], *arrays[first[g]:first[g] + sizes[g]],
                       *arrays[n + first[g]:n + first[g] + sizes[g]])) for g in range(n_g)]
    return started, out[-1]


def _exchange_wait(name, started, after):
    kind, (send_sems, recv_sems, local_sems, *arrays) = started
    n = len(arrays) // 2

    def body(*refs):
        src_refs, land_refs = refs[:n], refs[n:2 * n]
        copies, own = _direct_copies(kind, src_refs, land_refs, *refs[2 * n:2 * n + 3])
        for cp in copies + own:
            cp.wait()

    out = pl.pallas_call(
        body, name=name, out_shape=[pltpu.HBM(a.shape, a.dtype) for a in arrays],
        in_specs=[_HBM] * (2 * n) + [_SEM] * 3 + [_ANY], out_specs=[_HBM] * (2 * n),
        input_output_aliases={i: i for i in range(2 * n)},
        compiler_params=pltpu.CompilerParams(has_side_effects=_DATAFLOW))(*arrays, send_sems, recv_sems, local_sems, after)
    return out[n:]


def _adamw_math(g, w, m, v):
    m2 = ADAM_B1 * m + (1.0 - ADAM_B1) * g
    v2 = ADAM_B2 * v + (1.0 - ADAM_B2) * jnp.square(g)
    m_hat = m2 / (1.0 - ADAM_B1 ** ADAM_STEP)
    v_hat = v2 / (1.0 - ADAM_B2 ** ADAM_STEP)
    return g, -ADAM_LR * (m_hat / (jnp.sqrt(v_hat) + ADAM_EPS) + ADAM_WD * w), m2, v2


def _sum_blocks(ref):
    g = ref[0].astype(F32)
    for j in range(1, N_DEV):
        g = g + ref[j].astype(F32)
    return g


def _adamw_flat(name, parts, w, m, v):
    r = w.shape[0]
    tile = _pick(r, (1024, 512, 256, 128, 8))

    def body(p_ref, w_ref, m_ref, v_ref, *o_refs):
        for o, val in zip(o_refs, _adamw_math(_sum_blocks(p_ref), w_ref[...], m_ref[...], v_ref[...]), strict=True):
            o[...] = val

    flat = pl.BlockSpec((tile, LANES), lambda i: (i, 0))
    return pl.pallas_call(
        body, name=name, grid=(r // tile,), out_shape=[jax.ShapeDtypeStruct((r, LANES), F32)] * 4,
        in_specs=[pl.BlockSpec((N_DEV, tile, LANES), lambda i: (0, i, 0)), flat, flat, flat], out_specs=[flat] * 4,
        compiler_params=_params(("parallel",)))(parts, w, m, v)


def _adamw_cols(name, landed, base, stride, w, m, v):
    depth, rows, cs = w.shape
    n_slab = -(-cs // LANES)
    tr = _pick(rows, (SLAB_TILE,))

    def body(*refs):
        slabs, (w_ref, m_ref, v_ref), o_refs = refs[:n_slab], refs[n_slab:n_slab + 3], refs[n_slab + 3:]
        g = jnp.concatenate([_sum_blocks(s)[:, :min(LANES, cs - LANES * k)] for k, s in enumerate(slabs)], axis=1)
        for o, val in zip(o_refs, _adamw_math(g, w_ref[...], m_ref[...], v_ref[...]), strict=True):
            o[...] = val

    slab = lambda k: pl.BlockSpec((N_DEV, tr, LANES), lambda l, i: (0, (l * stride + base + k * rows) // tr + i, 0))
    nat = pl.BlockSpec((None, tr, cs), lambda l, i: (l, i, 0))
    return pl.pallas_call(
        body, name=name, grid=(depth, rows // tr), out_shape=[jax.ShapeDtypeStruct(w.shape, F32)] * 4,
        in_specs=[slab(k) for k in range(n_slab)] + [nat] * 3, out_specs=[nat] * 4,
        compiler_params=_params(("parallel", "parallel")))(*[landed] * n_slab, w, m, v)


def _adamw_rows(name, landed, base, stride, w, m, v):
    depth, rs, width = w.shape
    tr = math.gcd(rs, base, stride)

    def body(p_ref, w_ref, m_ref, v_ref, *o_refs):
        for o, val in zip(o_refs, _adamw_math(_sum_blocks(p_ref), w_ref[...], m_ref[...], v_ref[...]), strict=True):
            o[...] = val

    nat = pl.BlockSpec((None, tr, width), lambda l, i: (l, i, 0))
    return pl.pallas_call(
        body, name=name, grid=(depth, rs // tr), out_shape=[jax.ShapeDtypeStruct(w.shape, F32)] * 4,
        in_specs=[pl.BlockSpec((N_DEV, tr, width), lambda l, i: (0, (l * stride + base) // tr + i, 0)), nat, nat, nat],
        out_specs=[nat] * 4, compiler_params=_params(("parallel", "parallel")))(landed, w, m, v)


def _adamw_conv(landed, base, stride, w, m, v):
    depth, taps, cs = w.shape

    def body(p_ref, w_ref, m_ref, v_ref, *o_refs):
        g = _sum_blocks(p_ref)[:taps, :cs]
        for o, val in zip(o_refs, _adamw_math(g, w_ref[...], m_ref[...], v_ref[...]), strict=True):
            o[...] = val

    nat = pl.BlockSpec((None, taps, cs), lambda l: (l, 0, 0))
    return pl.pallas_call(
        body, name="adamw_conv", grid=(depth,), out_shape=[jax.ShapeDtypeStruct(w.shape, F32)] * 4,
        in_specs=[pl.BlockSpec((N_DEV, CONV_HALO, LANES), lambda l: (0, (l * stride + base) // CONV_HALO, 0)), nat, nat, nat],
        out_specs=[nat] * 4, compiler_params=_params(("parallel",)))(landed, w, m, v)


def _unshard_cols(name, gathered, start, rows, cs, shift=0):
    n_slab = -(-cs // LANES)
    total = N_DEV * cs
    tr = _pick(rows, (SLAB_TILE,))

    def body(*refs):
        slabs, o_ref = refs[:n_slab], refs[n_slab]
        for j in range(N_DEV):
            for k, s in enumerate(slabs):
                for src, dst, width in _wrapped(j * cs + LANES * k - shift, min(LANES, cs - LANES * k), total):
                    o_ref[:, dst:dst + width] = s[j, :, src:src + width]

    slab = lambda k: pl.BlockSpec((N_DEV, tr, LANES), lambda i: (0, (start + k * rows) // tr + i, 0))
    return pl.pallas_call(
        body, name=name, grid=(rows // tr,), out_shape=jax.ShapeDtypeStruct((rows, total), gathered.dtype),
        in_specs=[slab(k) for k in range(n_slab)], out_specs=pl.BlockSpec((tr, total), lambda i: (i, 0)),
        compiler_params=_params(("parallel",)))(*[gathered] * n_slab)


def _shard_cols(name, full, cs, shift=0):
    rows, total = full.shape
    n_slab = -(-cs // LANES)
    tr = _pick(rows, (SLAB_TILE,))

    def body(f_ref, o_ref):
        for j in range(N_DEV):
            for k in range(n_slab):
                used = min(LANES, cs - LANES * k)
                for src, dst, width in _wrapped(j * cs + LANES * k - shift, used, total):
                    o_ref[j, k, :, src:src + width] = f_ref[:, dst:dst + width].astype(o_ref.dtype)
                if used < LANES:
                    o_ref[j, k, :, used:] = jnp.zeros((tr, LANES - used), o_ref.dtype)

    out = pl.pallas_call(
        body, name=name, grid=(rows // tr,), out_shape=jax.ShapeDtypeStruct((N_DEV, n_slab, rows, LANES), BF16),
        in_specs=[pl.BlockSpec((tr, total), lambda i: (i, 0))],
        out_specs=pl.BlockSpec((N_DEV, n_slab, tr, LANES), lambda i: (0, 0, i, 0)),
        compiler_params=_params(("parallel",)))(full)
    return out.reshape(N_DEV, n_slab * rows, LANES)


def _wrapped(pos, width, total):
    pos %= total
    if pos + width <= total:
        return [(0, pos, width)]
    head = total - pos
    return [(0, pos, head), (head, 0, width - head)]


CONV_W_PIECES = 3


def _pad_to(n, align):
    return -(-n // align) * align


def _layout(group):
    col_names, row_names, with_conv = GROUPS[group]
    dims = {name: (rows, cols) for name, rows, cols, _ in SHARDED}
    col, off = {}, 0
    for name in col_names:
        rows, cols = dims[name]
        cs = cols // N_DEV
        col[name] = (off, rows, cs)
        off += -(-cs // LANES) * rows
    conv_base = off
    col_rows = _pad_to(off + with_conv * CONV_W_PIECES * CONV_HALO, SLAB_TILE)
    return col, conv_base, col_rows, {name: dims[name][0] // N_DEV for name in row_names}


def _slabs(shard, fill):
    rows, cs = shard.shape
    parts = []
    for k in range(-(-cs // LANES)):
        part = shard[:, LANES * k:min(LANES * (k + 1), cs)]
        parts.append(jnp.pad(part, ((0, 0), (0, LANES - part.shape[1])), constant_values=fill))
    return jnp.concatenate(parts, axis=0)


def _concat_padded(pieces, total, axis):
    used = sum(p.shape[axis] for p in pieces)
    if total > used:
        shape = list(pieces[0].shape)
        shape[axis] = total - used
        pieces = pieces + [jnp.zeros(shape, pieces[0].dtype)]
    return jnp.concatenate(pieces, axis=axis)


def _split3(a):
    hi = a.astype(BF16)
    r1 = a - hi.astype(F32)
    mid = r1.astype(BF16)
    return hi, mid, (r1 - mid.astype(F32)).astype(BF16)


def _pack_small(arrs, lead=()):
    flat = jnp.concatenate([a.reshape(lead + (-1,)) for a in arrs], axis=-1)
    total = _pad_to(flat.shape[-1], 512 * LANES)
    flat = jnp.pad(flat, [(0, 0)] * len(lead) + [(0, total - flat.shape[-1])])
    return flat.reshape(lead + (total // LANES, LANES))


def _unpack_small(flat, shapes):
    flat = flat.reshape(-1)
    res, off = [], 0
    for s in shapes:
        n = int(np.prod(s))
        res.append(flat[off:off + n].reshape(s))
        off += n
    return res


def _small_rows(a, depth):
    n16 = depth * SSM_GROUPS
    a_re, a_im, f_re, f_im = _ssm_coeffs(a["ssm_lambda_re"].reshape(n16, SSM_STATE), a["ssm_lambda_im"].reshape(n16, SSM_STATE),
                                         a["ssm_log_dt"].reshape(n16, 1))
    rows = []
    for l in range(depth):
        sp = {k: _row(a[k][l]) for k in ("mix_norm_g", "b_gate", "attn_sinks", "ssm_d", "b_ssm_glu", "conv_dw_b",
                                         "conv_norm_g", "conv_norm_b", "ffn_norm_g", "ple_norm_g")}
        g = slice(l * SSM_GROUPS, (l + 1) * SSM_GROUPS)
        sp["a_row"] = jnp.concatenate([a_re[g].reshape(1, -1), a_im[g].reshape(1, -1)], axis=1)
        sp["f_row"] = jnp.concatenate([f_re[g].reshape(1, -1), f_im[g].reshape(1, -1)], axis=1)
        sp["bmat"], sp["cmat"] = _ssm_mats(a["ssm_b_re"][l], a["ssm_b_im"][l], a["ssm_c_re"][l], a["ssm_c_im"][l])
        rows.append(sp)
    return rows


def _local_step(a, get_weights, on_grads, depth):
    n_seq, seq, d = a["x"].shape
    t = n_seq * seq
    inv = ROPE_THETA ** (-jnp.arange(0, ROPE_DIM, 2, dtype=F32) / ROPE_DIM)
    lane = np.arange(LANES) % HEAD_DIM
    inv_lane = jnp.where(lane < ROPE_DIM, jnp.tile(inv, LANES // (ROPE_DIM // 2)), 0.0).reshape(1, LANES)
    ctab, stab = _rope_tables(a["positions"].reshape(t), inv_lane)
    small = _small_rows(a, depth)

    x = a["x"].reshape(t, d)
    saved, weights = [], []
    for l in range(depth):
        x, sv, w = _layer_fwd(x, a["p"][l].reshape(t, -1), get_weights(l, "mix", x),
                              functools.partial(get_weights, l, "ffn"), small[l], ctab, stab, n_seq)
        saved.append(sv)
        weights.append(w)
    loss, dx, d_final = _loss_and_grad(x, a["loss_target"].reshape(t, d), _row(a["final_norm_g"]))
    gws, gss = [None] * depth, [None] * depth
    for l in reversed(range(depth)):
        dx, gws[l], gss[l] = _layer_bwd(dx, saved[l], weights[l], small[l], functools.partial(on_grads, l), ctab, stab, n_seq)

    n16 = depth * SSM_GROUPS
    halves = lambda k, h: jnp.concatenate([gss[l][k][:, h * SSM_LANES:(h + 1) * SSM_LANES].reshape(SSM_GROUPS, SSM_STATE)
                                           for l in range(depth)], axis=0)
    dlr, dli, ddt = _ssm_coeffs_bwd(a["ssm_lambda_re"].reshape(n16, SSM_STATE), a["ssm_lambda_im"].reshape(n16, SSM_STATE),
                                    a["ssm_log_dt"].reshape(n16, 1),
                                    (halves("a_row", 0), halves("a_row", 1), halves("f_row", 0), halves("f_row", 1)))
    bc = [_ssm_mats_t(gss[l]["bmat"], gss[l]["cmat"]) for l in range(depth)]
    gsmall = {k: jnp.stack([gss[l][k].reshape(a[k].shape[1:]) for l in range(depth)])
              for k in ("mix_norm_g", "b_gate", "attn_sinks", "ssm_d", "b_ssm_glu", "conv_dw_b", "conv_norm_g", "conv_norm_b",
                        "ffn_norm_g", "ple_norm_g")}
    gsmall["ssm_lambda_re"] = dlr.reshape(a["ssm_lambda_re"].shape)
    gsmall["ssm_lambda_im"] = dli.reshape(a["ssm_lambda_im"].shape)
    gsmall["ssm_log_dt"] = ddt.reshape(a["ssm_log_dt"].shape)
    for i, k in enumerate(("ssm_b_re", "ssm_b_im", "ssm_c_re", "ssm_c_im")):
        gsmall[k] = jnp.stack([bc[l][i] for l in range(depth)])
    gsmall["final_norm_g"] = d_final.reshape(a["final_norm_g"].shape)
    return loss, dx.reshape(n_seq, seq, d), gws, gsmall


def kernel(x, p, positions, mix_norm_g, w_in, b_gate, attn_sinks, w_attn_out, ssm_lambda_re, ssm_lambda_im, ssm_log_dt, ssm_b_re, ssm_b_im, ssm_c_re, ssm_c_im, ssm_d, w_ssm_glu, b_ssm_glu, conv_dw_w, conv_dw_b, conv_norm_g, conv_norm_b, w_conv_out, w_mix_out, ffn_norm_g, w_ffn_in, w_ffn_out, w_ple_in, ple_norm_g, w_ple_gate, final_norm_g, loss_target, m_mix_norm_g, m_w_in, m_b_gate, m_attn_sinks, m_w_attn_out, m_ssm_lambda_re, m_ssm_lambda_im, m_ssm_log_dt, m_ssm_b_re, m_ssm_b_im, m_ssm_c_re, m_ssm_c_im, m_ssm_d, m_w_ssm_glu, m_b_ssm_glu, m_conv_dw_w, m_conv_dw_b, m_conv_norm_g, m_conv_norm_b, m_w_conv_out, m_w_mix_out, m_ffn_norm_g, m_w_ffn_in, m_w_ffn_out, m_w_ple_in, m_ple_norm_g, m_w_ple_gate, m_final_norm_g, v_mix_norm_g, v_w_in, v_b_gate, v_attn_sinks, v_w_attn_out, v_ssm_lambda_re, v_ssm_lambda_im, v_ssm_log_dt, v_ssm_b_re, v_ssm_b_im, v_ssm_c_re, v_ssm_c_im, v_ssm_d, v_w_ssm_glu, v_b_ssm_glu, v_conv_dw_w, v_conv_dw_b, v_conv_norm_g, v_conv_norm_b, v_w_conv_out, v_w_mix_out, v_ffn_norm_g, v_w_ffn_in, v_w_ffn_out, v_w_ple_in, v_ple_norm_g, v_w_ple_gate, v_final_norm_g):
    a = dict(locals())
    depth = w_in.shape[0]
    layouts = {group: _layout(group) for group in GROUPS}
    shift = {"w_in": Z_SPLIT}
    conv_pad = ((0, 0), (0, CONV_HALO - CONV_K), (0, LANES - CONV_WIDTH // N_DEV))

    def packed_weights(l, group, fill):
        col, _, col_rows, row = layouts[group]
        pieces = [_slabs(a[name][l].astype(BF16), fill) for name in col]
        if GROUPS[group][2]:
            pieces.append(jnp.pad(jnp.stack(_split3(a["conv_dw_w"][l])), conv_pad).reshape(-1, LANES))
        return [_concat_padded(pieces, col_rows, 0)] + [a[name][l].astype(BF16) for name in row]

    gathers, tokens, fill = [], [], jnp.zeros((), BF16)
    for l in range(depth):
        started, token = _exchange_start(f"gather_start_{l}", "gather", [packed_weights(l, group, fill) for group in GROUPS])
        gathers.append(dict(zip(GROUPS, started, strict=True)))
        tokens.append(token[0:1, 0:1])
        fill = token[0, 0].astype(BF16)

    def get_weights(l, group, after):
        col, conv_base, _, row = layouts[group]
        slab8, *rows8 = _exchange_wait(f"gather_wait_{group}_{l}", gathers[l][group], after)
        w = {name: _unshard_cols("unshard_" + name, slab8, base, rows, cs, shift.get(name, 0))
             for name, (base, rows, cs) in col.items()}
        for (name, rs), gathered in zip(row.items(), rows8, strict=True):
            w[name] = gathered.reshape(N_DEV * rs, -1)
        if GROUPS[group][2]:
            conv = slab8[:, conv_base:conv_base + CONV_W_PIECES * CONV_HALO]
            conv = conv.reshape(N_DEV, CONV_W_PIECES, CONV_HALO, LANES)[:, :, :CONV_K, :CONV_WIDTH // N_DEV].astype(F32)
            w["conv_dw_w"] = jnp.sum(conv, axis=1).transpose(1, 0, 2).reshape(CONV_K, CONV_WIDTH)
        return w

    scatters = {}

    def on_grads(l, group, gw):
        col, _, col_rows, row = layouts[group]
        pieces = [_shard_cols("shard_" + name, gw[name], cs, shift.get(name, 0)) for name, (_, _, cs) in col.items()]
        if GROUPS[group][2]:
            conv = gw["conv_dw_w"].reshape(CONV_K, N_DEV, CONV_WIDTH // N_DEV).transpose(1, 0, 2).astype(BF16)
            pieces.append(jnp.pad(jnp.pad(conv, conv_pad), ((0, 0), (0, (CONV_W_PIECES - 1) * CONV_HALO), (0, 0))))
        by_shard = [gw[name].astype(BF16).reshape(N_DEV, rs, -1) for name, rs in row.items()]
        (scatters[l, group],), token = _exchange_start(
            f"grads_start_{group}_{l}", "scatter", [[_concat_padded(pieces, col_rows, 1)] + by_shard])
        return token

    local = dict(a, mix_norm_g=a["mix_norm_g"] + sum(tokens))
    loss, grad_x, _, gsmall = _local_step(local, get_weights, on_grads, depth)
    loss = lax.psum(loss, ("x", "y", "c"))

    shapes = [a[k].shape for k in REPLICATED]
    parts, = _all_gather("gather_small_grads", [_pack_small([gsmall[k] for k in REPLICATED])])
    small_state = [_pack_small([a[pre + k] for k in REPLICATED]) for pre in ("", "m_", "v_")]
    small_flat = _adamw_flat("adamw_replicated", parts, *small_state)
    small = [dict(zip(REPLICATED, _unpack_small(o, shapes), strict=True)) for o in small_flat]

    state = lambda name: (a[name], a["m_" + name], a["v_" + name])
    big, after = {}, small_flat[1]
    for group in ("ffn", "mix"):
        col, conv_base, col_rows, row = layouts[group]
        landed = [_exchange_wait(f"grads_wait_{group}_{l}", scatters[l, group], after) for l in range(depth)]
        landed_slab = jnp.concatenate([arrays[0] for arrays in landed], axis=1)
        big.update({name: _adamw_cols("adamw_" + name, landed_slab, base, col_rows, *state(name)) for name, (base, _, _) in col.items()})
        for i, (name, rs) in enumerate(row.items()):
            landed_rows = jnp.concatenate([arrays[1 + i] for arrays in landed], axis=1)
            big[name] = _adamw_rows("adamw_" + name, landed_rows, 0, rs, *state(name))
        if GROUPS[group][2]:
            big["conv_dw_w"] = _adamw_conv(landed_slab, conv_base, col_rows, *state("conv_dw_w"))
        after = big[next(iter(col))][1]

    def result(kind, name):
        if name in REPLICATED:
            return small[kind][name]
        return big[name][kind]

    return (loss, grad_x, *[result(kind, n) for kind in range(4) for n in WEIGHT_ORDER])
```

```python
import functools
import math

import numpy as np
import jax
import jax.numpy as jnp
from jax import lax
from jax.experimental import pallas as pl
from jax.experimental.pallas import tpu as pltpu

F32 = jnp.float32
BF16 = jnp.bfloat16
MXU_DTYPE = jnp.bfloat16
VMEM_LIMIT_BYTES = 56 * 2 ** 20
N_DEV = 8
LANES = 128

HEAD_DIM = 64
N_Q_HEADS = 8
N_KV_HEADS = 2
GQA_GROUP = 4
BLOCK = 128
ROPE_THETA = 500000.0
ROPE_DIM = 16
Q_WIDTH = 512
KV_WIDTH = 128
SSM_WIDTH = 256
SSM_GROUP = 16
SSM_GROUPS = 16
SSM_STATE = 64
SSM_LANES = SSM_GROUPS * SSM_STATE
CONV_WIDTH = 256
CONV_K = 31
CONV_HALO = 32
EPS = 1e-6
NEG_INF = -1e30
ADAM_LR, ADAM_B1, ADAM_B2, ADAM_EPS, ADAM_WD, ADAM_STEP = 0.001, 0.9, 0.999, 1e-08, 0.01, 10

ZG_W, ZQ_W, ZKV_W, ZS_W, ZC_W = 3072, 512, 256, 256, 512
ZQ_BLK, ZKV_BLK, ZS_BLK, ZC_BLK = 3072 // 512, 3584 // 256, 3840 // 256, 4096 // 512
Z_WIDTH = 4608
Z_SPLIT = 1536

SHARDED = (("w_in", 1024, 4608, 1), ("w_attn_out", 512, 1024, 1), ("w_ssm_glu", 256, 2048, 1),
           ("conv_dw_w", 31, 256, 1), ("w_conv_out", 256, 1024, 1), ("w_mix_out", 1024, 1024, 0),
           ("w_ffn_in", 1024, 5632, 1), ("w_ffn_out", 2816, 1024, 0), ("w_ple_in", 256, 1024, 1),
           ("w_ple_gate", 1024, 1024, 0))
GROUPS = {"mix": (("w_in", "w_attn_out", "w_ssm_glu", "w_conv_out"), ("w_mix_out",), True),
          "ffn": (("w_ffn_in", "w_ple_in"), ("w_ffn_out", "w_ple_gate"), False)}
SLAB_TILE = 256
FLAT_ROW_ALIGN = 1024
REPLICATED = ("mix_norm_g", "b_gate", "attn_sinks", "ssm_lambda_re", "ssm_lambda_im", "ssm_log_dt", "ssm_b_re",
              "ssm_b_im", "ssm_c_re", "ssm_c_im", "ssm_d", "b_ssm_glu", "conv_dw_b", "conv_norm_g", "conv_norm_b",
              "ffn_norm_g", "ple_norm_g", "final_norm_g")
WEIGHT_ORDER = ("mix_norm_g", "w_in", "b_gate", "attn_sinks", "w_attn_out", "ssm_lambda_re", "ssm_lambda_im",
                "ssm_log_dt", "ssm_b_re", "ssm_b_im", "ssm_c_re", "ssm_c_im", "ssm_d", "w_ssm_glu", "b_ssm_glu",
                "conv_dw_w", "conv_dw_b", "conv_norm_g", "conv_norm_b", "w_conv_out", "w_mix_out", "ffn_norm_g",
                "w_ffn_in", "w_ffn_out", "w_ple_in", "ple_norm_g", "w_ple_gate", "final_norm_g")


_ANY = pl.BlockSpec(memory_space=pl.ANY)


def _params(sem=None):
    return pltpu.CompilerParams(dimension_semantics=sem, vmem_limit_bytes=VMEM_LIMIT_BYTES)


def _pick(n, cands):
    for c in cands:
        if n % c == 0:
            return c
    return n


def _dot(a, b, dims):
    return lax.dot_general(a.astype(MXU_DTYPE), b.astype(MXU_DTYPE), (dims, ((), ())), preferred_element_type=F32)


def _dot_nn(a, b):
    return _dot(a, b, ((1,), (0,)))


def _dot_nt(a, b):
    return _dot(a, b, ((1,), (1,)))


def _dot_tn(a, b):
    return _dot(a, b, ((0,), (0,)))


@jax.custom_vjp
def _mm(x, w):
    return _dot_nn(x, w)


def _mm_f(x, w):
    return _dot_nn(x, w), (x, w)


def _mm_b(res, dy):
    x, w = res
    return _dot_nt(dy, w).astype(x.dtype), _dot_tn(x, dy).astype(w.dtype)


_mm.defvjp(_mm_f, _mm_b)


def _rms(x, g):
    return x * lax.rsqrt(jnp.mean(x * x, axis=-1, keepdims=True) + EPS) * g


ROW_TILES = (1024, 512, 256, 128)
COL_TILES = (1536, 1408, 1024, 512, 256, 128)


def _matmul_add(name, a, b, residual):
    t, k = a.shape
    n = b.shape[1]
    tm, tn = _pick(t, ROW_TILES), _pick(n, COL_TILES)

    def body(a_ref, b_ref, r_ref, o_ref):
        o_ref[...] = r_ref[...] + _dot_nn(a_ref[...], b_ref[...])

    tile = pl.BlockSpec((tm, tn), lambda i, j: (i, j))
    return pl.pallas_call(
        body, name=name, grid=(t // tm, n // tn), out_shape=jax.ShapeDtypeStruct((t, n), F32),
        in_specs=[pl.BlockSpec((tm, k), lambda i, j: (i, 0)), pl.BlockSpec((k, tn), lambda i, j: (0, j)), tile],
        out_specs=tile, compiler_params=_params(("parallel", "parallel")))(a, b, residual)


def _matmul_tn(name, a, b):
    t, m = a.shape
    n = b.shape[1]
    tm, tn, tt = _pick(m, COL_TILES[1:]), _pick(n, COL_TILES), _pick(t, ROW_TILES)

    def body(a_ref, b_ref, o_ref):
        @pl.when(pl.program_id(2) == 0)
        def _():
            o_ref[...] = jnp.zeros_like(o_ref)

        o_ref[...] += _dot_tn(a_ref[...], b_ref[...])

    return pl.pallas_call(
        body, name=name, grid=(m // tm, n // tn, t // tt), out_shape=jax.ShapeDtypeStruct((m, n), F32),
        in_specs=[pl.BlockSpec((tt, tm), lambda i, j, s: (s, i)), pl.BlockSpec((tt, tn), lambda i, j, s: (s, j))],
        out_specs=pl.BlockSpec((tm, tn), lambda i, j, s: (i, j)),
        compiler_params=_params(("parallel", "parallel", "arbitrary")))(a, b)


def _two_parts(n):
    cut = n // (2 * LANES) * LANES
    return [slice(0, n)] if cut == 0 else [slice(0, cut), slice(cut, n)]


def _in_proj(x, g, w):
    t, d = x.shape
    n = w.shape[1]
    tm, tn = _pick(t, ROW_TILES), _pick(n, COL_TILES)

    def body(x_ref, g_ref, w_ref, z_ref, h_ref):
        @pl.when(pl.program_id(1) == 0)
        def _():
            h_ref[...] = _rms(x_ref[...], g_ref[...]).astype(h_ref.dtype)

        z_ref[...] = _dot_nn(h_ref[...], w_ref[...])

    return pl.pallas_call(
        body, name="in_proj", grid=(t // tm, n // tn),
        out_shape=[jax.ShapeDtypeStruct((t, n), F32), jax.ShapeDtypeStruct((t, d), MXU_DTYPE)],
        in_specs=[pl.BlockSpec((tm, d), lambda i, j: (i, 0)), pl.BlockSpec((1, d), lambda i, j: (0, 0)),
                  pl.BlockSpec((d, tn), lambda i, j: (0, j))],
        out_specs=[pl.BlockSpec((tm, tn), lambda i, j: (i, j)), pl.BlockSpec((tm, d), lambda i, j: (i, 0))],
        compiler_params=_params(("parallel", "arbitrary")))(x, g, w)


def _in_proj_bwd(dz, w, x, dx_res, g):
    t, d = x.shape
    tm = _pick(t, ROW_TILES[1:])

    def body(dz_ref, w_ref, x_ref, r_ref, g_ref, dx_ref, dg_ref):
        _, vjp = jax.vjp(_norm_in_tile, x_ref[...], g_ref[...])
        dx, dg = vjp(_dot_nt(dz_ref[...], w_ref[...]))
        dx_ref[...] = dx + r_ref[...]

        @pl.when(pl.program_id(0) == 0)
        def _():
            dg_ref[...] = jnp.zeros_like(dg_ref)

        dg_ref[...] += dg

    rows = lambda width: pl.BlockSpec((tm, width), lambda i: (i, 0))
    whole = lambda a: pl.BlockSpec(a.shape, lambda i: (0, 0))
    return pl.pallas_call(
        body, name="in_proj_bwd", grid=(t // tm,), out_shape=[jax.ShapeDtypeStruct((t, d), F32), jax.ShapeDtypeStruct((1, d), F32)],
        in_specs=[rows(dz.shape[1]), whole(w), rows(d), rows(d), whole(g)], out_specs=[rows(d), whole(g)],
        compiler_params=_params(("arbitrary",)))(dz, w, x, dx_res, g)


def _ffn_in_act(hf, w_fi):
    t, k = hf.shape
    f = w_fi.shape[1] // 2
    tm, tf = _pick(t, ROW_TILES[1:]), _pick(f, COL_TILES)
    nf = f // tf

    def body(a_ref, wg_ref, wu_ref, g_ref, u_ref, act_ref):
        a = a_ref[...]
        for cols in _two_parts(tf):
            g, u = _dot_nn(a, wg_ref[:, cols]), _dot_nn(a, wu_ref[:, cols])
            g_ref[:, cols] = g.astype(g_ref.dtype)
            u_ref[:, cols] = u.astype(u_ref.dtype)
            act_ref[:, cols] = (jax.nn.silu(g) * u).astype(act_ref.dtype)

    out = pl.BlockSpec((tm, tf), lambda i, j: (i, j))
    return pl.pallas_call(
        body, name="ffn_in_act", grid=(t // tm, nf), out_shape=[jax.ShapeDtypeStruct((t, f), MXU_DTYPE)] * 3,
        in_specs=[pl.BlockSpec((tm, k), lambda i, j: (i, 0)), pl.BlockSpec((k, tf), lambda i, j: (0, j)),
                  pl.BlockSpec((k, tf), lambda i, j: (0, j + nf))],
        out_specs=[out, out, out], compiler_params=_params(("parallel", "parallel")))(hf, w_fi, w_fi)


def _ffn_mid_bwd(dffn, w_fo, gate, up):
    t, d = dffn.shape
    f = w_fo.shape[0]
    tm, tf = _pick(t, ROW_TILES[1:]), _pick(f, COL_TILES)

    def body(a_ref, w_ref, g_ref, u_ref, dg_ref, du_ref):
        a = a_ref[...]
        for cols in _two_parts(tf):
            dact = _dot_nt(a, w_ref[cols, :])
            g, u = g_ref[:, cols].astype(F32), u_ref[:, cols].astype(F32)
            sg = jax.nn.sigmoid(g)
            dg_ref[:, cols] = (dact * u * sg * (1.0 + g * (1.0 - sg))).astype(dg_ref.dtype)
            du_ref[:, cols] = (dact * g * sg).astype(du_ref.dtype)

    blk = pl.BlockSpec((tm, tf), lambda i, j: (i, j))
    return pl.pallas_call(
        body, name="ffn_mid_bwd", grid=(t // tm, f // tf), out_shape=[jax.ShapeDtypeStruct((t, f), MXU_DTYPE)] * 2,
        in_specs=[pl.BlockSpec((tm, d), lambda i, j: (i, 0)), pl.BlockSpec((tf, d), lambda i, j: (j, 0)), blk, blk],
        out_specs=[blk, blk], compiler_params=_params(("parallel", "parallel")))(dffn, w_fo, gate, up)


def _ffn_in_dx(dgate, dup, w_fi):
    t, f = dgate.shape
    d = w_fi.shape[0]
    tm = _pick(t, ROW_TILES[1:])

    def body(g_ref, u_ref, w_ref, o_ref):
        o_ref[...] = _dot_nt(g_ref[...], w_ref[:, :f]) + _dot_nt(u_ref[...], w_ref[:, f:])

    blk = pl.BlockSpec((tm, f), lambda i: (i, 0))
    return pl.pallas_call(
        body, name="ffn_in_dx", grid=(t // tm,), out_shape=jax.ShapeDtypeStruct((t, d), F32),
        in_specs=[blk, blk, pl.BlockSpec(w_fi.shape, lambda i: (0, 0))], out_specs=pl.BlockSpec((tm, d), lambda i: (i, 0)),
        compiler_params=_params(("parallel",)))(dgate, dup, w_fi)


def _token_call(name, fn, tile, tok_ins, consts, tok_outs, acc_outs, into=None):
    n_rows = tok_ins[0][0].shape[0]
    tile = min(tile, n_rows)
    n_ti, n_c = len(tok_ins), len(consts)
    n_in = n_ti + n_c + (into is not None)
    n_to = len(tok_outs) + (into is not None)

    def body(*refs):
        ins = [r[...] for r in refs[:n_ti + n_c]]
        outs, accs = fn(*ins)
        for r, v in zip(refs[n_in:n_in + n_to], outs, strict=True):
            r[...] = v.astype(r.dtype)
        first = pl.program_id(0) == 0
        for r, v in zip(refs[n_in + n_to:], accs, strict=True):
            @pl.when(first)
            def _(r=r):
                r[...] = jnp.zeros_like(r)

            r[...] += jnp.broadcast_to(v, r.shape).astype(F32)

    in_specs = [pl.BlockSpec((tile, w), functools.partial(lambda i, c: (i, c), c=cb)) for _, w, cb in tok_ins]
    in_specs += [pl.BlockSpec(c.shape, lambda i: (0, 0)) for c in consts]
    out_shape = [jax.ShapeDtypeStruct((n_rows, w), dt) for w, dt in tok_outs]
    out_specs = [pl.BlockSpec((tile, w), lambda i: (i, 0)) for w, _ in tok_outs]
    operands = [a for a, _, _ in tok_ins] + list(consts)
    aliases = {}
    if into is not None:
        target, width, col_block = into
        in_specs.append(_ANY)
        operands.append(target)
        out_shape.append(jax.ShapeDtypeStruct(target.shape, target.dtype))
        out_specs.append(pl.BlockSpec((tile, width), lambda i: (i, col_block)))
        aliases = {n_in - 1: n_to - 1}
    out_shape += [jax.ShapeDtypeStruct(s, F32) for s in acc_outs]
    out_specs += [pl.BlockSpec(s, lambda i: (0, 0)) for s in acc_outs]
    res = pl.pallas_call(
        body, name=name, grid=(n_rows // tile,), out_shape=out_shape, in_specs=in_specs, out_specs=out_specs,
        input_output_aliases=aliases, compiler_params=_params(("arbitrary",)))(*operands)
    return res[:n_to], res[n_to:]


def _whole(a):
    return (a, a.shape[1], 0)


def _norm_in_tile(x, g):
    return _rms(x, g)


def _conv_post_tile(v, g, b):
    mu = jnp.mean(v, axis=-1, keepdims=True)
    var = jnp.mean(jnp.square(v - mu), axis=-1, keepdims=True)
    return jax.nn.silu((v - mu) * lax.rsqrt(var + EPS) * g + b)


def _branches_tile(ya, ys, v, gin, w_ao, w_sg, b_sg, ln_g, ln_b, w_co, b_gate):
    d = w_ao.shape[1]
    y_attn = _mm(ya, w_ao)
    pre = _mm(jax.nn.gelu(ys), w_sg) + b_sg
    y_ssm = pre[:, :d] * jax.nn.sigmoid(pre[:, d:])
    y_conv = _mm(_conv_post_tile(v, ln_g, ln_b), w_co)
    gates = jax.nn.sigmoid(gin + b_gate)
    return gates[:, :d] * y_attn + gates[:, d:2 * d] * y_ssm + gates[:, 2 * d:] * y_conv


def _merge_tile(x, ya, ys, v, gin, w_ao, w_sg, b_sg, ln_g, ln_b, w_co, b_gate, w_mo, g_ffn):
    x1 = x + _mm(_branches_tile(ya, ys, v, gin, w_ao, w_sg, b_sg, ln_g, ln_b, w_co, b_gate), w_mo)
    return x1, _rms(x1, g_ffn)


def _merge_bwd_tile(x1, ya, ys, v, gin, dx1, dhf, w_ao, w_sg, b_sg, ln_g, ln_b, w_co, b_gate, w_mo, g_ffn):
    _, norm_vjp = jax.vjp(_rms, x1, g_ffn)
    dx1_norm, dg_ffn = norm_vjp(dhf)
    dx1 = dx1 + dx1_norm
    merged, branch_vjp = jax.vjp(_branches_tile, ya, ys, v, gin, w_ao, w_sg, b_sg, ln_g, ln_b, w_co, b_gate)
    grads = branch_vjp(_dot_nt(dx1, w_mo))
    return [dx1, *grads[:4]], [*grads[4:], _dot_tn(merged, dx1), dg_ffn]


def _ple_tile(x2, p, w_pi, g_ple, w_pg):
    pre, e = _mm(_rms(x2, g_ple), w_pg), _mm(p, w_pi)
    return x2 + jax.nn.sigmoid(pre) * e, pre, e


def _ple_bwd_tile(x2, p, pre, e, dx3, w_pi, g_ple, w_pg):
    sig = jax.nn.sigmoid(pre)
    dpre = dx3 * e * sig * (1.0 - sig)
    hn, norm_vjp = jax.vjp(_rms, x2, g_ple)
    dx2_norm, dg_ple = norm_vjp(_dot_nt(dpre, w_pg))
    return dx3 + dx2_norm, [_dot_tn(p, dx3 * sig), dg_ple, _dot_tn(hn, dpre)]


def _f32s(vals):
    return [v.astype(F32) for v in vals]


def _rope_tables(positions, inv_lane):
    def fn(pos, inv):
        ang = pos.astype(F32) * inv
        j = lax.broadcasted_iota(jnp.int32, ang.shape, 1) % HEAD_DIM
        c = jnp.where(j < ROPE_DIM, jnp.cos(ang), 1.0)
        s = jnp.sin(ang)
        s = jnp.where(j < ROPE_DIM // 2, -s, jnp.where(j < ROPE_DIM, s, 0.0))
        return [c, s], []

    (c, s), _ = _token_call("rope_tables", fn, 1024, [_whole(positions.reshape(-1, 1))], [inv_lane],
                            [(LANES, F32), (LANES, F32)], [])
    return c, s


def _swap_halves(t):
    n = t.shape[1]
    j = lax.broadcasted_iota(jnp.int32, t.shape, 1) % HEAD_DIM
    lower = pltpu.roll(t, n - ROPE_DIM // 2, 1)
    upper = jnp.where(j < ROPE_DIM, pltpu.roll(t, ROPE_DIM // 2, 1), 0.0)
    return jnp.where(j < ROPE_DIM // 2, lower, upper)


def _rope(t, c, s):
    return t * c + _swap_halves(t) * s


def _rope_t(dt, c, s):
    return dt * c + _swap_halves(dt * s)


def _tile4(a):
    return jnp.concatenate([a] * (Q_WIDTH // LANES), axis=1)


def _attn_mask(n):
    qi = lax.broadcasted_iota(jnp.int32, (GQA_GROUP * BLOCK, 2 * BLOCK), 0) % BLOCK
    kj = lax.broadcasted_iota(jnp.int32, (GQA_GROUP * BLOCK, 2 * BLOCK), 1)
    dist = qi + BLOCK - kj
    return (dist >= 0) & (dist < BLOCK) & ((n > 0) | (kj >= BLOCK))


def _attn_specs(n_seq):
    own = lambda w, blk: pl.BlockSpec((n_seq, BLOCK, w), lambda n: (0, n, blk))
    prev = lambda w, blk: pl.BlockSpec((n_seq, BLOCK, w), lambda n: (0, jnp.maximum(n - 1, 0), blk))
    return [own(ZQ_W, ZQ_BLK), own(ZKV_W, ZKV_BLK), prev(ZKV_W, ZKV_BLK), own(LANES, 0), own(LANES, 0), prev(LANES, 0),
            prev(LANES, 0), pl.BlockSpec((1, N_Q_HEADS), lambda n: (0, 0))]


def _by_seq(a, n_seq):
    return a.reshape(n_seq, a.shape[0] // n_seq, a.shape[1])


ATTN_SCALE = HEAD_DIM ** -0.5


def _stack_heads(t, kh):
    return jnp.concatenate([t[:, (kh * GQA_GROUP + g) * HEAD_DIM:(kh * GQA_GROUP + g + 1) * HEAD_DIM]
                            for g in range(GQA_GROUP)], axis=0)


def _stack_sinks(sink, kh):
    return jnp.concatenate([jnp.broadcast_to(sink[:, kh * GQA_GROUP + g:kh * GQA_GROUP + g + 1], (BLOCK, 1))
                            for g in range(GQA_GROUP)], axis=0)


def _attn_band(b, q_ref, kv_ref, kvp_ref, c_ref, s_ref, cp_ref, sp_ref):
    c, s = c_ref[b], s_ref[b]
    q = _rope(q_ref[b], _tile4(c), _tile4(s)) * ATTN_SCALE
    kv, kvp = kv_ref[b], kvp_ref[b]
    k = _rope(kv[:, :KV_WIDTH], c, s)
    kp = _rope(kvp[:, :KV_WIDTH], cp_ref[b], sp_ref[b])
    kb = jnp.concatenate([kp, k], axis=0)
    vb = jnp.concatenate([kvp[:, KV_WIDTH:], kv[:, KV_WIDTH:]], axis=0)
    return q, kb, vb


def _attention_fwd(z, ctab, stab, sinks, n_seq):
    t = z.shape[0]
    seq = t // n_seq

    def body(q_ref, kv_ref, kvp_ref, c_ref, s_ref, cp_ref, sp_ref, sink_ref, o_ref, lse_ref):
        mask = _attn_mask(pl.program_id(0))
        sink = sink_ref[...]
        lane = lax.broadcasted_iota(jnp.int32, (BLOCK, N_Q_HEADS), 1)
        for b in range(n_seq):
            q, kb, vb = _attn_band(b, q_ref, kv_ref, kvp_ref, c_ref, s_ref, cp_ref, sp_ref)
            lse_all = jnp.zeros((BLOCK, N_Q_HEADS), F32)
            for kh in range(N_KV_HEADS):
                sc = jnp.where(mask, _dot_nt(_stack_heads(q, kh), kb[:, kh * HEAD_DIM:(kh + 1) * HEAD_DIM]), NEG_INF)
                sk = _stack_sinks(sink, kh)
                m = jnp.maximum(jnp.max(sc, axis=-1, keepdims=True), sk)
                pr = jnp.exp(sc - m)
                den = jnp.sum(pr, axis=-1, keepdims=True) + jnp.exp(sk - m)
                out = _dot_nn(pr * (1.0 / den), vb[:, kh * HEAD_DIM:(kh + 1) * HEAD_DIM])
                lse = m + jnp.log(den)
                for g in range(GQA_GROUP):
                    h = kh * GQA_GROUP + g
                    o_ref[b, :, h * HEAD_DIM:(h + 1) * HEAD_DIM] = out[g * BLOCK:(g + 1) * BLOCK].astype(o_ref.dtype)
                    lse_all = jnp.where(lane == h, lse[g * BLOCK:(g + 1) * BLOCK], lse_all)
            lse_ref[b] = lse_all

    rows = lambda w: pl.BlockSpec((n_seq, BLOCK, w), lambda n: (0, n, 0))
    z3, c3, s3 = _by_seq(z, n_seq), _by_seq(ctab, n_seq), _by_seq(stab, n_seq)
    ya, lse = pl.pallas_call(
        body, name="attn_fwd", grid=(seq // BLOCK,),
        out_shape=[jax.ShapeDtypeStruct((n_seq, seq, Q_WIDTH), MXU_DTYPE), jax.ShapeDtypeStruct((n_seq, seq, N_Q_HEADS), F32)],
        in_specs=_attn_specs(n_seq), out_specs=[rows(Q_WIDTH), rows(N_Q_HEADS)],
        compiler_params=_params(("parallel",)))(z3, z3, z3, c3, s3, c3, s3, sinks)
    return ya.reshape(t, Q_WIDTH), lse.reshape(t, N_Q_HEADS)


def _attention_bwd(z, ctab, stab, sinks, ya, lse, dya, dz, n_seq):
    t = z.shape[0]
    seq = t // n_seq

    def body(q_ref, kv_ref, kvp_ref, c_ref, s_ref, cp_ref, sp_ref, sink_ref, o_ref, lse_ref, do_ref, _,
             dq_ref, dkv_ref, dkvp_ref, dsink_ref):
        mask = _attn_mask(pl.program_id(0))
        sink = sink_ref[...]
        lane = lax.broadcasted_iota(jnp.int32, (1, N_Q_HEADS), 1)
        dsink = jnp.zeros((1, N_Q_HEADS), F32)
        for b in range(n_seq):
            q, kb, vb = _attn_band(b, q_ref, kv_ref, kvp_ref, c_ref, s_ref, cp_ref, sp_ref)
            lse_all = lse_ref[b]
            o = o_ref[b].astype(F32)
            do = do_ref[b].astype(F32)
            dq_parts = []
            dk_parts, dv_parts = [], []
            for kh in range(N_KV_HEADS):
                kbh = kb[:, kh * HEAD_DIM:(kh + 1) * HEAD_DIM]
                vbh = vb[:, kh * HEAD_DIM:(kh + 1) * HEAD_DIM]
                qs, dos = _stack_heads(q, kh), _stack_heads(do, kh)
                lse = jnp.concatenate([lse_all[:, kh * GQA_GROUP + g:kh * GQA_GROUP + g + 1] for g in range(GQA_GROUP)], axis=0)
                pr = jnp.exp(jnp.where(mask, _dot_nt(qs, kbh), NEG_INF) - lse)
                delta = jnp.sum(dos * _stack_heads(o, kh), axis=-1, keepdims=True)
                ds = pr * (_dot_nt(dos, vbh) - delta)
                dqs = _dot_nn(ds, kbh)
                dq_parts += [dqs[g * BLOCK:(g + 1) * BLOCK] for g in range(GQA_GROUP)]
                dk_parts.append(_dot_tn(ds, qs))
                dv_parts.append(_dot_tn(pr, dos))
                dsk = jnp.exp(_stack_sinks(sink, kh) - lse) * delta
                for g in range(GQA_GROUP):
                    dsink = dsink + jnp.where(lane == kh * GQA_GROUP + g, -jnp.sum(dsk[g * BLOCK:(g + 1) * BLOCK]), 0.0)
            c, s = c_ref[b], s_ref[b]
            dq_ref[b] = _rope_t(jnp.concatenate(dq_parts, axis=1) * ATTN_SCALE, _tile4(c), _tile4(s)).astype(dq_ref.dtype)
            dk = jnp.concatenate(dk_parts, axis=1)
            dv = jnp.concatenate(dv_parts, axis=1)
            dkv_ref[b, :, :KV_WIDTH] = _rope_t(dk[BLOCK:], c, s)
            dkv_ref[b, :, KV_WIDTH:] = dv[BLOCK:]
            dkvp_ref[b, :, :KV_WIDTH] = _rope_t(dk[:BLOCK], cp_ref[b], sp_ref[b])
            dkvp_ref[b, :, KV_WIDTH:] = dv[:BLOCK]

        @pl.when(pl.program_id(0) == 0)
        def _():
            dsink_ref[...] = jnp.zeros_like(dsink_ref)

        dsink_ref[...] += dsink

    rows = lambda w: pl.BlockSpec((n_seq, BLOCK, w), lambda n: (0, n, 0))
    by_seq = lambda a: _by_seq(a, n_seq)
    z3, c3, s3 = by_seq(z), by_seq(ctab), by_seq(stab)
    dz, dkv, dkvp, dsink = pl.pallas_call(
        body, name="attn_bwd", grid=(seq // BLOCK,),
        out_shape=[jax.ShapeDtypeStruct((n_seq, seq, Z_WIDTH), dz.dtype), jax.ShapeDtypeStruct((n_seq, seq, ZKV_W), F32),
                   jax.ShapeDtypeStruct((n_seq, seq, ZKV_W), F32), jax.ShapeDtypeStruct((1, N_Q_HEADS), F32)],
        in_specs=_attn_specs(n_seq) + [rows(Q_WIDTH), rows(N_Q_HEADS), rows(Q_WIDTH), _ANY],
        out_specs=[pl.BlockSpec((n_seq, BLOCK, ZQ_W), lambda n: (0, n, ZQ_BLK)), rows(ZKV_W), rows(ZKV_W),
                   pl.BlockSpec((1, N_Q_HEADS), lambda n: (0, 0))],
        input_output_aliases={11: 0},
        compiler_params=_params(("arbitrary",)))(z3, z3, z3, c3, s3, c3, s3, sinks, by_seq(ya), by_seq(lse), by_seq(dya), by_seq(dz))
    return dz.reshape(t, Z_WIDTH), dkv.reshape(t, ZKV_W), dkvp.reshape(t, ZKV_W), dsink


def _kv_combine(dkv, dkvp, dz, n_seq):
    t = dkv.shape[0]
    seq = t // n_seq
    rows = _pick(seq, (512, 256, 128))
    nt, per = seq // rows, rows // BLOCK
    n_blocks = t // BLOCK

    def body(dkv_ref, dkvp_ref, dkvn_ref, _, o_ref):
        nxt = jnp.where(pl.program_id(1) == nt - 1, 0.0, dkvn_ref[...])
        shifted = nxt if per == 1 else jnp.concatenate([dkvp_ref[BLOCK:, :], nxt], axis=0)
        o_ref[...] = (dkv_ref[...] + shifted).astype(o_ref.dtype)

    tile = pl.BlockSpec((rows, ZKV_W), lambda b, i: (b * nt + i, 0))
    return pl.pallas_call(
        body, name="kv_combine", grid=(n_seq, nt), out_shape=jax.ShapeDtypeStruct(dz.shape, dz.dtype),
        in_specs=[tile, tile,
                  pl.BlockSpec((BLOCK, ZKV_W), lambda b, i: (jnp.minimum((b * nt + i + 1) * per, n_blocks - 1), 0)), _ANY],
        out_specs=pl.BlockSpec((rows, ZKV_W), lambda b, i: (b * nt + i, ZKV_BLK)), input_output_aliases={3: 0},
        compiler_params=_params(("parallel", "parallel")))(dkv, dkvp, dkvp, dz)


def _ssm_coeff_tile(lam_re, lam_im, log_dt):
    lr = jnp.minimum(lam_re, -1e-4)
    dt = jnp.exp(log_dt)
    mag = jnp.exp(lr * dt)
    a_re = mag * jnp.cos(lam_im * dt)
    a_im = mag * jnp.sin(lam_im * dt)
    den = lr * lr + lam_im * lam_im
    x_re = a_re - 1.0
    f_re = (x_re * lr + a_im * lam_im) / den
    f_im = (a_im * lr - x_re * lam_im) / den
    return a_re, a_im, f_re, f_im


def _ssm_coeffs(lam_re, lam_im, log_dt):
    def body(lr_ref, li_ref, dt_ref, *o_refs):
        for r, v in zip(o_refs, _ssm_coeff_tile(lr_ref[...], li_ref[...], dt_ref[...]), strict=True):
            r[...] = v

    return pl.pallas_call(body, name="ssm_coeffs", out_shape=[jax.ShapeDtypeStruct(lam_re.shape, F32)] * 4)(
        lam_re, lam_im, log_dt)


def _ssm_coeffs_bwd(lam_re, lam_im, log_dt, cts):
    def body(lr_ref, li_ref, dt_ref, c0, c1, c2, c3, dlr_ref, dli_ref, ddt_ref):
        _, vjp = jax.vjp(_ssm_coeff_tile, lr_ref[...], li_ref[...], dt_ref[...])
        dlr, dli, ddt = vjp((c0[...], c1[...], c2[...], c3[...]))
        dlr_ref[...] = dlr
        dli_ref[...] = dli
        ddt_ref[...] = ddt

    return pl.pallas_call(
        body, name="ssm_coeffs_bwd",
        out_shape=[jax.ShapeDtypeStruct(lam_re.shape, F32)] * 2 + [jax.ShapeDtypeStruct(log_dt.shape, F32)])(
        lam_re, lam_im, log_dt, *cts)


def _ssm_chunk(t):
    return _pick(t, (256, 128))


def _ssm_fwd(z, bmat, a_row, f_row, cmat, d_row, n_seq):
    t = z.shape[0]
    seq = t // n_seq
    lc = _ssm_chunk(seq)
    nc = seq // lc
    n2 = 2 * SSM_LANES

    re, im = pl.ds(0, SSM_LANES), pl.ds(SSM_LANES, SSM_LANES)

    def body(u_ref, b_ref, a_ref, f_ref, c_ref, d_ref, y_ref, s_ref, bu_ref, st_ref):
        @pl.when(pl.program_id(0) == 0)
        def _():
            st_ref[...] = jnp.zeros_like(st_ref)

        fr, fi = f_ref[:, :SSM_LANES], f_ref[:, SSM_LANES:]
        for b in range(n_seq):
            proj = _dot_nn(u_ref[b], b_ref[...])
            pr, pi = proj[:, :SSM_LANES], proj[:, SSM_LANES:]
            bu_ref[b, :, :SSM_LANES] = fr * pr - fi * pi
            bu_ref[b, :, SSM_LANES:] = fr * pi + fi * pr
        ar, ai = a_ref[:, :SSM_LANES], a_ref[:, SSM_LANES:]

        def step(i, carry):
            out = []
            for b in range(n_seq):
                sr, si = carry[2 * b], carry[2 * b + 1]
                nr = ar * sr - ai * si + bu_ref[b, pl.ds(i, 1), re]
                ni = ar * si + ai * sr + bu_ref[b, pl.ds(i, 1), im]
                s_ref[b, pl.ds(i, 1), re] = nr
                s_ref[b, pl.ds(i, 1), im] = ni
                out += [nr, ni]
            return tuple(out)

        carry = lax.fori_loop(0, lc, step, tuple(st_ref[b, 0:1, part] for b in range(n_seq) for part in (re, im)), unroll=8)
        for b in range(n_seq):
            st_ref[b, 0:1, re], st_ref[b, 0:1, im] = carry[2 * b], carry[2 * b + 1]
            y_ref[b] = _dot_nn(s_ref[b], c_ref[...]) + d_ref[...] * u_ref[b]

    const = lambda shape: pl.BlockSpec(shape, lambda c: (0, 0))
    rows = lambda w, cb: pl.BlockSpec((n_seq, lc, w), lambda c: (0, c, cb))
    ys, states = pl.pallas_call(
        body, name="ssm_fwd", grid=(nc,),
        out_shape=[jax.ShapeDtypeStruct((n_seq, seq, SSM_WIDTH), F32), jax.ShapeDtypeStruct((n_seq, seq, n2), F32)],
        in_specs=[rows(ZS_W, ZS_BLK), const((SSM_WIDTH, n2)), const((1, n2)), const((1, n2)), const((n2, SSM_WIDTH)),
                  const((1, SSM_WIDTH))],
        out_specs=[rows(SSM_WIDTH, 0), rows(n2, 0)],
        scratch_shapes=[pltpu.VMEM((n_seq, lc, n2), F32), pltpu.VMEM((n_seq, 8, n2), F32)],
        compiler_params=_params(("arbitrary",)))(_by_seq(z, n_seq), bmat, a_row, f_row, cmat, d_row)
    return ys.reshape(t, SSM_WIDTH), states.reshape(t, n2)


def _ssm_bwd(z, states, dy, bmat, a_row, f_row, cmat, d_row, dz, n_seq):
    t = z.shape[0]
    seq = t // n_seq
    lc = _ssm_chunk(seq)
    nc = seq // lc
    n2 = 2 * SSM_LANES

    def body(dy_ref, u_ref, s_ref, b_ref, a_ref, f_ref, c_ref, d_ref, _,
             du_ref, db_ref, dc_ref, da_ref, df_ref, dd_ref, g_ref, carry_ref):
        @pl.when(pl.program_id(0) == 0)
        def _():
            for r in (db_ref, dc_ref, da_ref, df_ref, dd_ref, carry_ref):
                r[...] = jnp.zeros_like(r)

        re, im = pl.ds(0, SSM_LANES), pl.ds(SSM_LANES, SSM_LANES)
        for b in range(n_seq):
            dy = dy_ref[b]
            g_ref[b, 0:lc, :] = _dot_nt(dy, c_ref[...])
            g_ref[b, lc:lc + 8, :] = carry_ref[b]
            dc_ref[...] += _dot_tn(s_ref[b], dy)
            dd_ref[...] += jnp.sum(dy * u_ref[b], axis=0, keepdims=True)
        ar, ai = a_ref[:, :SSM_LANES], a_ref[:, SSM_LANES:]

        def step(i, carry):
            r = lc - 1 - i
            out = []
            for b in range(n_seq):
                gr, gi = carry[2 * b], carry[2 * b + 1]
                nr = g_ref[b, pl.ds(r, 1), re] + ar * gr + ai * gi
                ni = g_ref[b, pl.ds(r, 1), im] - ai * gr + ar * gi
                g_ref[b, pl.ds(r, 1), re] = nr
                g_ref[b, pl.ds(r, 1), im] = ni
                out += [nr, ni]
            return tuple(out)

        carry = lax.fori_loop(0, lc, step, tuple(carry_ref[b, 0:1, part] for b in range(n_seq) for part in (re, im)), unroll=8)
        fr, fi = f_ref[:, :SSM_LANES], f_ref[:, SSM_LANES:]
        for b in range(n_seq):
            carry_ref[b, 0:1, re], carry_ref[b, 0:1, im] = carry[2 * b], carry[2 * b + 1]
            dy, u, st = dy_ref[b], u_ref[b], s_ref[b]
            sr, si = st[:, :SSM_LANES], st[:, SSM_LANES:]
            gnr, gni = g_ref[b, pl.ds(1, lc), re], g_ref[b, pl.ds(1, lc), im]
            da_ref[:, :SSM_LANES] += jnp.sum(gnr * sr + gni * si, axis=0, keepdims=True)
            da_ref[:, SSM_LANES:] += jnp.sum(gni * sr - gnr * si, axis=0, keepdims=True)
            gr_all, gi_all = g_ref[b, 0:lc, :SSM_LANES], g_ref[b, 0:lc, SSM_LANES:]
            proj = _dot_nn(u, b_ref[...])
            pr, pi = proj[:, :SSM_LANES], proj[:, SSM_LANES:]
            df_ref[:, :SSM_LANES] += jnp.sum(gr_all * pr + gi_all * pi, axis=0, keepdims=True)
            df_ref[:, SSM_LANES:] += jnp.sum(gi_all * pr - gr_all * pi, axis=0, keepdims=True)
            dproj = jnp.concatenate([fr * gr_all + fi * gi_all, fr * gi_all - fi * gr_all], axis=1).astype(MXU_DTYPE)
            du_ref[b] = (_dot_nt(dproj, b_ref[...]) + d_ref[...] * dy).astype(du_ref.dtype)
            db_ref[...] += _dot_tn(u, dproj)

    const = lambda shape: pl.BlockSpec(shape, lambda c: (0, 0))
    rows = lambda w, cb: pl.BlockSpec((n_seq, lc, w), lambda c: (0, nc - 1 - c, cb))
    by_seq = lambda a: _by_seq(a, n_seq)
    dz, *sums = pl.pallas_call(
        body, name="ssm_bwd", grid=(nc,),
        out_shape=[jax.ShapeDtypeStruct((n_seq, seq, Z_WIDTH), dz.dtype), jax.ShapeDtypeStruct((SSM_WIDTH, n2), F32),
                   jax.ShapeDtypeStruct((n2, SSM_WIDTH), F32), jax.ShapeDtypeStruct((1, n2), F32),
                   jax.ShapeDtypeStruct((1, n2), F32), jax.ShapeDtypeStruct((1, SSM_WIDTH), F32)],
        in_specs=[rows(SSM_WIDTH, 0), rows(ZS_W, ZS_BLK), rows(n2, 0), const((SSM_WIDTH, n2)), const((1, n2)),
                  const((1, n2)), const((n2, SSM_WIDTH)), const((1, SSM_WIDTH)), _ANY],
        out_specs=[rows(ZS_W, ZS_BLK), const((SSM_WIDTH, n2)), const((n2, SSM_WIDTH)), const((1, n2)), const((1, n2)),
                   const((1, SSM_WIDTH))],
        input_output_aliases={8: 0},
        scratch_shapes=[pltpu.VMEM((n_seq, lc + 8, n2), F32), pltpu.VMEM((n_seq, 8, n2), F32)],
        compiler_params=_params(("arbitrary",)))(by_seq(dy), by_seq(z), by_seq(states), bmat, a_row, f_row, cmat, d_row, by_seq(dz))
    return (dz.reshape(t, Z_WIDTH), *sums)


def _conv_chunk(t):
    return _pick(t, (512, 256, 128))


def _glu(c):
    return c[:, :CONV_WIDTH] * jax.nn.sigmoid(c[:, CONV_WIDTH:])


def _conv_specs(lc, nc):
    per = lc // CONV_HALO
    return [pl.BlockSpec((lc, ZC_W), lambda b, c: (b * nc + c, ZC_BLK)),
            pl.BlockSpec((CONV_HALO, ZC_W), lambda b, c: (jnp.maximum((b * nc + c) * per - 1, 0), ZC_BLK))]


def _conv_fill(c_ref, cp_ref, ue_ref, lc):
    ue_ref[0:CONV_HALO, :] = jnp.where(pl.program_id(1) > 0, _glu(cp_ref[...]), 0.0)
    ue_ref[CONV_HALO:CONV_HALO + lc, :] = _glu(c_ref[...])


CONV_ROWS = 64


def _conv_apply(ue_ref, w_ref, b_ref, o_ref, lc):
    for r0 in range(0, lc, CONV_ROWS):
        acc = jnp.zeros((CONV_ROWS, CONV_WIDTH), F32) + b_ref[...]
        for k in range(CONV_K):
            acc = acc + w_ref[k:k + 1, :] * ue_ref[pl.ds(r0 + k + CONV_HALO - CONV_K + 1, CONV_ROWS), :]
        o_ref[r0:r0 + CONV_ROWS, :] = acc


def _conv_fwd(z, dw_w, dw_b, n_seq):
    t = z.shape[0]
    seq = t // n_seq
    lc = _conv_chunk(seq)
    nc = seq // lc

    def body(c_ref, cp_ref, w_ref, b_ref, o_ref, ue_ref):
        _conv_fill(c_ref, cp_ref, ue_ref, lc)
        _conv_apply(ue_ref, w_ref, b_ref, o_ref, lc)

    const = lambda a: pl.BlockSpec(a.shape, lambda b, c: (0, 0))
    return pl.pallas_call(
        body, name="conv_fwd", grid=(n_seq, nc), out_shape=jax.ShapeDtypeStruct((t, CONV_WIDTH), F32),
        in_specs=_conv_specs(lc, nc) + [const(dw_w), const(dw_b)],
        out_specs=pl.BlockSpec((lc, CONV_WIDTH), lambda b, c: (b * nc + c, 0)),
        scratch_shapes=[pltpu.VMEM((CONV_HALO + lc, CONV_WIDTH), F32)],
        compiler_params=_params(("parallel", "parallel")))(z, z, dw_w, dw_b)


def _conv_bwd_taps(z, dv, dw_w, dz, n_seq):
    t = z.shape[0]
    seq = t // n_seq
    lc = _conv_chunk(seq)
    nc = seq // lc
    per = lc // CONV_HALO
    n_halo = t // CONV_HALO

    def body(c_ref, cp_ref, dv_ref, dvn_ref, w_ref, _, dc_ref, dw_ref, ue_ref, dve_ref):
        @pl.when((pl.program_id(0) == 0) & (pl.program_id(1) == 0))
        def _():
            dw_ref[...] = jnp.zeros_like(dw_ref)

        _conv_fill(c_ref, cp_ref, ue_ref, lc)
        dv = dv_ref[...]
        dve_ref[0:lc, :] = dv
        dve_ref[lc:lc + CONV_HALO, :] = jnp.where(pl.program_id(1) < nc - 1, dvn_ref[...], 0.0)
        dw_ref[CONV_K:CONV_K + 1, :] += jnp.sum(dv, axis=0, keepdims=True)
        for r0 in range(0, lc, CONV_ROWS):
            rows = pl.ds(r0, CONV_ROWS)
            dv_rows = dv_ref[rows, :]
            du = jnp.zeros((CONV_ROWS, CONV_WIDTH), F32)
            for k in range(CONV_K):
                du = du + w_ref[k:k + 1, :] * dve_ref[pl.ds(r0 + CONV_K - 1 - k, CONV_ROWS), :]
                taps = ue_ref[pl.ds(r0 + k + CONV_HALO - CONV_K + 1, CONV_ROWS), :]
                dw_ref[k:k + 1, :] += jnp.sum(dv_rows * taps, axis=0, keepdims=True)
            c = c_ref[rows, :]
            a, sg = c[:, :CONV_WIDTH], jax.nn.sigmoid(c[:, CONV_WIDTH:])
            dc_ref[rows, :CONV_WIDTH] = (du * sg).astype(dc_ref.dtype)
            dc_ref[rows, CONV_WIDTH:] = (du * a * sg * (1.0 - sg)).astype(dc_ref.dtype)

    return pl.pallas_call(
        body, name="conv_bwd_taps", grid=(n_seq, nc),
        out_shape=[jax.ShapeDtypeStruct(dz.shape, dz.dtype), jax.ShapeDtypeStruct((CONV_HALO, CONV_WIDTH), F32)],
        in_specs=_conv_specs(lc, nc) + [
            pl.BlockSpec((lc, CONV_WIDTH), lambda b, c: (b * nc + c, 0)),
            pl.BlockSpec((CONV_HALO, CONV_WIDTH), lambda b, c: (jnp.minimum((b * nc + c + 1) * per, n_halo - 1), 0)),
            pl.BlockSpec(dw_w.shape, lambda b, c: (0, 0)), _ANY],
        out_specs=[pl.BlockSpec((lc, ZC_W), lambda b, c: (b * nc + c, ZC_BLK)),
                   pl.BlockSpec((CONV_HALO, CONV_WIDTH), lambda b, c: (0, 0))],
        input_output_aliases={5: 0},
        scratch_shapes=[pltpu.VMEM((CONV_HALO + lc, CONV_WIDTH), F32), pltpu.VMEM((lc + CONV_HALO, CONV_WIDTH), F32)],
        compiler_params=_params(("arbitrary", "arbitrary")))(z, z, dv, dv, dw_w, dz)


def _row(v):
    return v.reshape(1, -1)


def _ssm_mats(b_re, b_im, c_re, c_im):
    eye = jnp.eye(SSM_GROUPS, dtype=bool)
    bm = jnp.stack([b_re, b_im]).transpose(1, 3, 0, 2)[:, :, :, None, :]
    bmat = jnp.where(eye[:, None, None, :, None], bm, 0.0).reshape(SSM_WIDTH, 2 * SSM_LANES)
    cm = jnp.stack([c_re, -c_im]).transpose(0, 1, 3, 2)[:, :, :, None, :]
    cmat = jnp.where(eye[None, :, None, :, None], cm, 0.0).reshape(2 * SSM_LANES, SSM_WIDTH)
    return bmat.astype(MXU_DTYPE), cmat.astype(MXU_DTYPE)


def _ssm_mats_t(dbmat, dcmat):
    eye = jnp.eye(SSM_GROUPS, dtype=bool)
    db = dbmat.reshape(SSM_GROUPS, SSM_GROUP, 2, SSM_GROUPS, SSM_STATE)
    db = jnp.sum(jnp.where(eye[:, None, None, :, None], db, 0.0), axis=3).transpose(2, 0, 3, 1)
    dc = dcmat.reshape(2, SSM_GROUPS, SSM_STATE, SSM_GROUPS, SSM_GROUP)
    dc = jnp.sum(jnp.where(eye[None, :, None, :, None], dc, 0.0), axis=3).transpose(0, 1, 3, 2)
    return db[0], db[1], dc[0], -dc[1]


def _layer_fwd(x, p, w, get_ffn_weights, sp, ctab, stab, n_seq):
    z, h = _in_proj(x, sp["mix_norm_g"], w["w_in"])
    ya, lse = _attention_fwd(z, ctab, stab, sp["attn_sinks"], n_seq)
    ys, states = _ssm_fwd(z, sp["bmat"], sp["a_row"], sp["f_row"], sp["cmat"], sp["ssm_d"], n_seq)
    v = _conv_fwd(z, w["conv_dw_w"], sp["conv_dw_b"], n_seq)
    merge_consts = [w["w_attn_out"], w["w_ssm_glu"], sp["b_ssm_glu"], sp["conv_norm_g"], sp["conv_norm_b"], w["w_conv_out"],
                    sp["b_gate"], w["w_mix_out"], sp["ffn_norm_g"]]
    (x1, hf), _ = _token_call("merge", lambda *a: (list(_merge_tile(*_f32s(a))), []), 512,
                              [_whole(x), _whole(ya), _whole(ys), _whole(v), (z, ZG_W, 0)], merge_consts,
                              [(x.shape[1], F32), (x.shape[1], MXU_DTYPE)], [])
    w = dict(w, **get_ffn_weights(x1))
    gate, up, act = _ffn_in_act(hf, w["w_ffn_in"])
    x2 = _matmul_add("mm_ffn_out", act, w["w_ffn_out"], x1)
    d = x.shape[1]
    (x3, ple_pre, ple_e), _ = _token_call(
        "ple", lambda *a: (list(_ple_tile(*a)), []), 512, [_whole(x2), _whole(p)],
        [w["w_ple_in"], sp["ple_norm_g"], w["w_ple_gate"]], [(d, F32), (d, MXU_DTYPE), (d, MXU_DTYPE)], [])
    saved = dict(h=h, z=z, ya=ya, lse=lse, ys=ys, states=states, v=v, x1=x1, hf=hf, gate=gate, up=up, act=act, x2=x2, p=p,
                 ple_pre=ple_pre, ple_e=ple_e, x=x)
    return x3, saved, w


def _layer_bwd(dx3, sv, w, sp, on_grads, ctab, stab, n_seq):
    d = dx3.shape[1]
    gw, gs = {}, {}

    def ple_bwd(*a):
        dx2, accs = _ple_bwd_tile(*_f32s(a))
        return [dx2, dx2], accs

    (dx2, dffn), (gw["w_ple_in"], gs["ple_norm_g"], gw["w_ple_gate"]) = _token_call(
        "ple_bwd", ple_bwd, 512, [_whole(sv["x2"]), _whole(sv["p"]), _whole(sv["ple_pre"]), _whole(sv["ple_e"]), _whole(dx3)],
        [w["w_ple_in"], sp["ple_norm_g"], w["w_ple_gate"]], [(d, F32), (d, MXU_DTYPE)],
        [w["w_ple_in"].shape, (1, d), w["w_ple_gate"].shape])

    dgate, dup = _ffn_mid_bwd(dffn, w["w_ffn_out"], sv["gate"], sv["up"])
    gw["w_ffn_out"] = _matmul_tn("mm_ffn_out_dw", sv["act"], dffn)
    dhf = _ffn_in_dx(dgate, dup, w["w_ffn_in"])
    gw["w_ffn_in"] = jnp.concatenate([_matmul_tn("mm_ffn_gate_dw", sv["hf"], dgate), _matmul_tn("mm_ffn_up_dw", sv["hf"], dup)],
                                     axis=1)

    token = on_grads("ffn", gw)

    def merge_bwd(*a):
        return _merge_bwd_tile(*_f32s(a))

    b_gate = sp["b_gate"] if token is None else sp["b_gate"] + token[0:1, 0:1]
    merge_consts = [w["w_attn_out"], w["w_ssm_glu"], sp["b_ssm_glu"], sp["conv_norm_g"], sp["conv_norm_b"], w["w_conv_out"],
                    b_gate, w["w_mix_out"], sp["ffn_norm_g"]]
    dz = lax.empty(sv["z"].shape, MXU_DTYPE)
    (dx_res, dya, dys, dv, dz), macc = _token_call(
        "merge_bwd", merge_bwd, 256,
        [_whole(sv["x1"]), _whole(sv["ya"]), _whole(sv["ys"]), _whole(sv["v"]), (sv["z"], ZG_W, 0), _whole(dx2), _whole(dhf)],
        merge_consts, [(d, F32), (Q_WIDTH, MXU_DTYPE), (SSM_WIDTH, F32), (CONV_WIDTH, F32)],
        [c.shape for c in merge_consts], into=(dz, ZG_W, 0))
    (gw["w_attn_out"], gw["w_ssm_glu"], gs["b_ssm_glu"], gs["conv_norm_g"], gs["conv_norm_b"], gw["w_conv_out"], gs["b_gate"],
     gw["w_mix_out"], gs["ffn_norm_g"]) = macc

    dz, dw_taps = _conv_bwd_taps(sv["z"], dv, w["conv_dw_w"], dz, n_seq)
    gw["conv_dw_w"], gs["conv_dw_b"] = dw_taps[:CONV_K], dw_taps[CONV_K:]

    dz, gs["bmat"], gs["cmat"], gs["a_row"], gs["f_row"], gs["ssm_d"] = _ssm_bwd(
        sv["z"], sv["states"], dys, sp["bmat"], sp["a_row"], sp["f_row"], sp["cmat"], sp["ssm_d"], dz, n_seq)

    dz, dkv, dkvp, gs["attn_sinks"] = _attention_bwd(sv["z"], ctab, stab, sp["attn_sinks"], sv["ya"], sv["lse"], dya, dz, n_seq)
    dz = _kv_combine(dkv, dkvp, dz, n_seq)
    gw["w_in"] = _matmul_tn("mm_in_dw", sv["h"], dz)
    token = on_grads("mix", gw)
    g_in = sp["mix_norm_g"] if token is None else sp["mix_norm_g"] + token[0:1, 0:1]
    dx, gs["mix_norm_g"] = _in_proj_bwd(dz, w["w_in"], sv["x"], dx_res, g_in)
    return dx, gw, gs


def _loss_and_grad(x, target, g):
    def fn(x, tgt, g):
        def f(x, g):
            err = _rms(x, g) - tgt
            return 0.5 * jnp.mean(err * err, axis=-1, keepdims=True)

        per_token, vjp = jax.vjp(f, x, g)
        dx, dg = vjp(jnp.ones_like(per_token))
        return [dx], [jnp.sum(per_token, axis=0, keepdims=True), dg]

    (dx,), (loss, dg) = _token_call("loss", fn, 512, [_whole(x), _whole(target)], [g], [(x.shape[1], F32)],
                                    [(8, LANES), (1, x.shape[1])])
    return loss[0, 0], dx, dg


def _mesh_place():
    return lax.axis_index("x"), lax.axis_index("y"), lax.axis_index("c")


def _flip(v, bit):
    return 1 - v if bit else v


_MESH = pl.DeviceIdType.MESH


def _all_gather(name, xs):
    n = len(xs)

    def body(*refs):
        x_refs, out_refs = refs[:n], refs[n:2 * n]
        send_sems, recv_sems, local_sems = refs[2 * n:]
        mx, my, mc = _mesh_place()
        me, sibling = (mx, my, mc), (mx, my, 1 - mc)
        chips = [(1 - mx, my), (mx, 1 - my), (1 - mx, 1 - my)]

        def slot(a, px, py, pc):
            return out_refs[a].at[4 * px + 2 * py + pc]

        def copy(a, k, block, to, src=None):
            return pltpu.make_async_remote_copy(
                src_ref=slot(a, *block) if src is None else src, dst_ref=slot(a, *block), send_sem=send_sems.at[7 * a + k],
                recv_sem=recv_sems.at[7 * a + k], device_id=to, device_id_type=_MESH)

        mine = [pltpu.make_async_copy(x_refs[a], slot(a, *me), local_sems.at[a]) for a in range(n)]
        for cp in mine:
            cp.start()
        first = [copy(a, 0, me, sibling, src=x_refs[a]) for a in range(n)]
        first += [copy(a, 1 + j, me, (*chip, mc), src=x_refs[a]) for j, chip in enumerate(chips) for a in range(n)]
        for cp in first:
            cp.start()
        passed = []
        for j, chip in enumerate(chips):
            for a in range(n):
                copy(a, 1 + j, (*chip, mc), me).wait_recv()
                passed.append(copy(a, 4 + j, (*chip, mc), sibling))
                passed[-1].start()
        for a in range(n):
            copy(a, 0, sibling, me).wait_recv()
            for j, chip in enumerate(chips):
                copy(a, 4 + j, (*chip, 1 - mc), me).wait_recv()
        for cp in first + passed:
            cp.wait_send()
        for cp in mine:
            cp.wait()

    return pl.pallas_call(
        body, name=name, out_shape=[jax.ShapeDtypeStruct((N_DEV,) + x.shape, x.dtype) for x in xs], in_specs=[_ANY] * n,
        out_specs=[_ANY] * n,
        scratch_shapes=[pltpu.SemaphoreType.DMA((7 * n,)), pltpu.SemaphoreType.DMA((7 * n,)), pltpu.SemaphoreType.DMA((n,))])(*xs)


def _direct_copies(kind, src_refs, land_refs, send_sems, recv_sems, local_sems):
    mx, my, mc = _mesh_place()
    me = 4 * mx + 2 * my + mc
    n = len(src_refs)
    own = [pltpu.make_async_copy(src_refs[a] if kind == "gather" else src_refs[a].at[me], land_refs[a].at[me], local_sems.at[a])
           for a in range(n)]
    copies = []
    for rel in range(1, N_DEV):
        px, py, pc = _flip(mx, rel & 4), _flip(my, rel & 2), _flip(mc, rel & 1)
        for a in range(n):
            src = src_refs[a] if kind == "gather" else src_refs[a].at[4 * px + 2 * py + pc]
            copies.append(pltpu.make_async_remote_copy(
                src_ref=src, dst_ref=land_refs[a].at[me], send_sem=send_sems.at[7 * a + rel - 1],
                recv_sem=recv_sems.at[7 * a + rel - 1], device_id=(px, py, pc), device_id_type=_MESH))
    return copies, own


_HBM = pl.BlockSpec(memory_space=pltpu.HBM)
_SEM = pl.BlockSpec(memory_space=pltpu.SEMAPHORE)
_DATAFLOW = pltpu.SideEffectType.DATAFLOW_SIDE_EFFECTING


def _exchange_start(name, kind, groups):
    sizes = [len(g) for g in groups]
    srcs = [s for g in groups for s in g]
    lands = [lax.empty(((N_DEV,) + s.shape) if kind == "gather" else s.shape, s.dtype) for s in srcs]
    n, n_g = len(srcs), len(groups)
    first = [sum(sizes[:g]) for g in range(n_g)]

    def body(*refs):
        src_refs, land_refs, sems = refs[:n], refs[n:2 * n], refs[2 * n:2 * n + 3 * n_g]
        for g in range(n_g):
            span = slice(first[g], first[g] + sizes[g])
            copies, own = _direct_copies(kind, src_refs[span], land_refs[span], *sems[3 * g:3 * g + 3])
            for cp in own + copies:
                cp.start()
        refs[-1][...] = jnp.zeros_like(refs[-1])

    hbm = lambda a: pltpu.with_memory_space_constraint(a, pltpu.HBM)
    sem_shapes = [pltpu.SemaphoreType.DMA((k * m,)) for m in sizes for k in (7, 7, 1)]
    out = pl.pallas_call(
        body, name=name,
        out_shape=sem_shapes + [pltpu.HBM(a.shape, a.dtype) for a in srcs + lands] + [jax.ShapeDtypeStruct((8, LANES), F32)],
        in_specs=[_HBM] * (2 * n), out_specs=[_SEM] * (3 * n_g) + [_HBM] * (2 * n) + [pl.BlockSpec(memory_space=pltpu.VMEM)],
        input_output_aliases={i: 3 * n_g + i for i in range(2 * n)},
        compiler_params=pltpu.CompilerParams(has_side_effects=_DATAFLOW))(*[hbm(a) for a in srcs + lands])
    sems, arrays = out[:3 * n_g], out[3 * n_g:-1]
    started = [(kind, (*sems[3 * g:3 * g + 3], *arrays[first[g]:first[g] + sizes[g]],
                       *arrays[n + first[g]:n + first[g] + sizes[g]])) for g in range(n_g)]
    return started, out[-1]


def _exchange_wait(name, started, after):
    kind, (send_sems, recv_sems, local_sems, *arrays) = started
    n = len(arrays) // 2

    def body(*refs):
        src_refs, land_refs = refs[:n], refs[n:2 * n]
        copies, own = _direct_copies(kind, src_refs, land_refs, *refs[2 * n:2 * n + 3])
        for cp in copies + own:
            cp.wait()

    out = pl.pallas_call(
        body, name=name, out_shape=[pltpu.HBM(a.shape, a.dtype) for a in arrays],
        in_specs=[_HBM] * (2 * n) + [_SEM] * 3 + [_ANY], out_specs=[_HBM] * (2 * n),
        input_output_aliases={i: i for i in range(2 * n)},
        compiler_params=pltpu.CompilerParams(has_side_effects=_DATAFLOW))(*arrays, send_sems, recv_sems, local_sems, after)
    return out[n:]


def _adamw_math(g, w, m, v):
    m2 = ADAM_B1 * m + (1.0 - ADAM_B1) * g
    v2 = ADAM_B2 * v + (1.0 - ADAM_B2) * jnp.square(g)
    m_hat = m2 / (1.0 - ADAM_B1 ** ADAM_STEP)
    v_hat = v2 / (1.0 - ADAM_B2 ** ADAM_STEP)
    return g, -ADAM_LR * (m_hat / (jnp.sqrt(v_hat) + ADAM_EPS) + ADAM_WD * w), m2, v2


def _sum_blocks(ref):
    g = ref[0].astype(F32)
    for j in range(1, N_DEV):
        g = g + ref[j].astype(F32)
    return g


def _adamw_flat(name, parts, w, m, v):
    r = w.shape[0]
    tile = _pick(r, (1024, 512, 256, 128, 8))

    def body(p_ref, w_ref, m_ref, v_ref, *o_refs):
        for o, val in zip(o_refs, _adamw_math(_sum_blocks(p_ref), w_ref[...], m_ref[...], v_ref[...]), strict=True):
            o[...] = val

    flat = pl.BlockSpec((tile, LANES), lambda i: (i, 0))
    return pl.pallas_call(
        body, name=name, grid=(r // tile,), out_shape=[jax.ShapeDtypeStruct((r, LANES), F32)] * 4,
        in_specs=[pl.BlockSpec((N_DEV, tile, LANES), lambda i: (0, i, 0)), flat, flat, flat], out_specs=[flat] * 4,
        compiler_params=_params(("parallel",)))(parts, w, m, v)


def _adamw_cols(name, landed, base, stride, w, m, v):
    depth, rows, cs = w.shape
    n_slab = -(-cs // LANES)
    tr = _pick(rows, (SLAB_TILE,))

    def body(*refs):
        slabs, (w_ref, m_ref, v_ref), o_refs = refs[:n_slab], refs[n_slab:n_slab + 3], refs[n_slab + 3:]
        g = jnp.concatenate([_sum_blocks(s)[:, :min(LANES, cs - LANES * k)] for k, s in enumerate(slabs)], axis=1)
        for o, val in zip(o_refs, _adamw_math(g, w_ref[...], m_ref[...], v_ref[...]), strict=True):
            o[...] = val

    slab = lambda k: pl.BlockSpec((N_DEV, tr, LANES), lambda l, i: (0, (l * stride + base + k * rows) // tr + i, 0))
    nat = pl.BlockSpec((None, tr, cs), lambda l, i: (l, i, 0))
    return pl.pallas_call(
        body, name=name, grid=(depth, rows // tr), out_shape=[jax.ShapeDtypeStruct(w.shape, F32)] * 4,
        in_specs=[slab(k) for k in range(n_slab)] + [nat] * 3, out_specs=[nat] * 4,
        compiler_params=_params(("parallel", "parallel")))(*[landed] * n_slab, w, m, v)


def _adamw_rows(name, landed, base, stride, w, m, v):
    depth, rs, width = w.shape
    tr = math.gcd(rs, base, stride)

    def body(p_ref, w_ref, m_ref, v_ref, *o_refs):
        for o, val in zip(o_refs, _adamw_math(_sum_blocks(p_ref), w_ref[...], m_ref[...], v_ref[...]), strict=True):
            o[...] = val

    nat = pl.BlockSpec((None, tr, width), lambda l, i: (l, i, 0))
    return pl.pallas_call(
        body, name=name, grid=(depth, rs // tr), out_shape=[jax.ShapeDtypeStruct(w.shape, F32)] * 4,
        in_specs=[pl.BlockSpec((N_DEV, tr, width), lambda l, i: (0, (l * stride + base) // tr + i, 0)), nat, nat, nat],
        out_specs=[nat] * 4, compiler_params=_params(("parallel", "parallel")))(landed, w, m, v)


def _adamw_conv(landed, base, stride, w, m, v):
    depth, taps, cs = w.shape

    def body(p_ref, w_ref, m_ref, v_ref, *o_refs):
        g = _sum_blocks(p_ref)[:taps, :cs]
        for o, val in zip(o_refs, _adamw_math(g, w_ref[...], m_ref[...], v_ref[...]), strict=True):
            o[...] = val

    nat = pl.BlockSpec((None, taps, cs), lambda l: (l, 0, 0))
    return pl.pallas_call(
        body, name="adamw_conv", grid=(depth,), out_shape=[jax.ShapeDtypeStruct(w.shape, F32)] * 4,
        in_specs=[pl.BlockSpec((N_DEV, CONV_HALO, LANES), lambda l: (0, (l * stride + base) // CONV_HALO, 0)), nat, nat, nat],
        out_specs=[nat] * 4, compiler_params=_params(("parallel",)))(landed, w, m, v)


def _unshard_cols(name, gathered, start, rows, cs, shift=0):
    n_slab = -(-cs // LANES)
    total = N_DEV * cs
    tr = _pick(rows, (SLAB_TILE,))

    def body(*refs):
        slabs, o_ref = refs[:n_slab], refs[n_slab]
        for j in range(N_DEV):
            for k, s in enumerate(slabs):
                for src, dst, width in _wrapped(j * cs + LANES * k - shift, min(LANES, cs - LANES * k), total):
                    o_ref[:, dst:dst + width] = s[j, :, src:src + width]

    slab = lambda k: pl.BlockSpec((N_DEV, tr, LANES), lambda i: (0, (start + k * rows) // tr + i, 0))
    return pl.pallas_call(
        body, name=name, grid=(rows // tr,), out_shape=jax.ShapeDtypeStruct((rows, total), gathered.dtype),
        in_specs=[slab(k) for k in range(n_slab)], out_specs=pl.BlockSpec((tr, total), lambda i: (i, 0)),
        compiler_params=_params(("parallel",)))(*[gathered] * n_slab)


def _shard_cols(name, full, cs, shift=0):
    rows, total = full.shape
    n_slab = -(-cs // LANES)
    tr = _pick(rows, (SLAB_TILE,))

    def body(f_ref, o_ref):
        for j in range(N_DEV):
            for k in range(n_slab):
                used = min(LANES, cs - LANES * k)
                for src, dst, width in _wrapped(j * cs + LANES * k - shift, used, total):
                    o_ref[j, k, :, src:src + width] = f_ref[:, dst:dst + width].astype(o_ref.dtype)
                if used < LANES:
                    o_ref[j, k, :, used:] = jnp.zeros((tr, LANES - used), o_ref.dtype)

    out = pl.pallas_call(
        body, name=name, grid=(rows // tr,), out_shape=jax.ShapeDtypeStruct((N_DEV, n_slab, rows, LANES), BF16),
        in_specs=[pl.BlockSpec((tr, total), lambda i: (i, 0))],
        out_specs=pl.BlockSpec((N_DEV, n_slab, tr, LANES), lambda i: (0, 0, i, 0)),
        compiler_params=_params(("parallel",)))(full)
    return out.reshape(N_DEV, n_slab * rows, LANES)


def _wrapped(pos, width, total):
    pos %= total
    if pos + width <= total:
        return [(0, pos, width)]
    head = total - pos
    return [(0, pos, head), (head, 0, width - head)]


CONV_W_PIECES = 3


def _pad_to(n, align):
    return -(-n // align) * align


def _layout(group):
    col_names, row_names, with_conv = GROUPS[group]
    dims = {name: (rows, cols) for name, rows, cols, _ in SHARDED}
    col, off = {}, 0
    for name in col_names:
        rows, cols = dims[name]
        cs = cols // N_DEV
        col[name] = (off, rows, cs)
        off += -(-cs // LANES) * rows
    conv_base = off
    col_rows = _pad_to(off + with_conv * CONV_W_PIECES * CONV_HALO, SLAB_TILE)
    return col, conv_base, col_rows, {name: dims[name][0] // N_DEV for name in row_names}


def _slabs(shard, fill):
    rows, cs = shard.shape
    parts = []
    for k in range(-(-cs // LANES)):
        part = shard[:, LANES * k:min(LANES * (k + 1), cs)]
        parts.append(jnp.pad(part, ((0, 0), (0, LANES - part.shape[1])), constant_values=fill))
    return jnp.concatenate(parts, axis=0)


def _concat_padded(pieces, total, axis):
    used = sum(p.shape[axis] for p in pieces)
    if total > used:
        shape = list(pieces[0].shape)
        shape[axis] = total - used
        pieces = pieces + [jnp.zeros(shape, pieces[0].dtype)]
    return jnp.concatenate(pieces, axis=axis)


def _split3(a):
    hi = a.astype(BF16)
    r1 = a - hi.astype(F32)
    mid = r1.astype(BF16)
    return hi, mid, (r1 - mid.astype(F32)).astype(BF16)


def _pack_small(arrs, lead=()):
    flat = jnp.concatenate([a.reshape(lead + (-1,)) for a in arrs], axis=-1)
    total = _pad_to(flat.shape[-1], 512 * LANES)
    flat = jnp.pad(flat, [(0, 0)] * len(lead) + [(0, total - flat.shape[-1])])
    return flat.reshape(lead + (total // LANES, LANES))


def _unpack_small(flat, shapes):
    flat = flat.reshape(-1)
    res, off = [], 0
    for s in shapes:
        n = int(np.prod(s))
        res.append(flat[off:off + n].reshape(s))
        off += n
    return res


def _small_rows(a, depth):
    n16 = depth * SSM_GROUPS
    a_re, a_im, f_re, f_im = _ssm_coeffs(a["ssm_lambda_re"].reshape(n16, SSM_STATE), a["ssm_lambda_im"].reshape(n16, SSM_STATE),
                                         a["ssm_log_dt"].reshape(n16, 1))
    rows = []
    for l in range(depth):
        sp = {k: _row(a[k][l]) for k in ("mix_norm_g", "b_gate", "attn_sinks", "ssm_d", "b_ssm_glu", "conv_dw_b",
                                         "conv_norm_g", "conv_norm_b", "ffn_norm_g", "ple_norm_g")}
        g = slice(l * SSM_GROUPS, (l + 1) * SSM_GROUPS)
        sp["a_row"] = jnp.concatenate([a_re[g].reshape(1, -1), a_im[g].reshape(1, -1)], axis=1)
        sp["f_row"] = jnp.concatenate([f_re[g].reshape(1, -1), f_im[g].reshape(1, -1)], axis=1)
        sp["bmat"], sp["cmat"] = _ssm_mats(a["ssm_b_re"][l], a["ssm_b_im"][l], a["ssm_c_re"][l], a["ssm_c_im"][l])
        rows.append(sp)
    return rows


def _local_step(a, get_weights, on_grads, depth):
    n_seq, seq, d = a["x"].shape
    t = n_seq * seq
    inv = ROPE_THETA ** (-jnp.arange(0, ROPE_DIM, 2, dtype=F32) / ROPE_DIM)
    lane = np.arange(LANES) % HEAD_DIM
    inv_lane = jnp.where(lane < ROPE_DIM, jnp.tile(inv, LANES // (ROPE_DIM // 2)), 0.0).reshape(1, LANES)
    ctab, stab = _rope_tables(a["positions"].reshape(t), inv_lane)
    small = _small_rows(a, depth)

    x = a["x"].reshape(t, d)
    saved, weights = [], []
    for l in range(depth):
        x, sv, w = _layer_fwd(x, a["p"][l].reshape(t, -1), get_weights(l, "mix", x),
                              functools.partial(get_weights, l, "ffn"), small[l], ctab, stab, n_seq)
        saved.append(sv)
        weights.append(w)
    loss, dx, d_final = _loss_and_grad(x, a["loss_target"].reshape(t, d), _row(a["final_norm_g"]))
    gws, gss = [None] * depth, [None] * depth
    for l in reversed(range(depth)):
        dx, gws[l], gss[l] = _layer_bwd(dx, saved[l], weights[l], small[l], functools.partial(on_grads, l), ctab, stab, n_seq)

    n16 = depth * SSM_GROUPS
    halves = lambda k, h: jnp.concatenate([gss[l][k][:, h * SSM_LANES:(h + 1) * SSM_LANES].reshape(SSM_GROUPS, SSM_STATE)
                                           for l in range(depth)], axis=0)
    dlr, dli, ddt = _ssm_coeffs_bwd(a["ssm_lambda_re"].reshape(n16, SSM_STATE), a["ssm_lambda_im"].reshape(n16, SSM_STATE),
                                    a["ssm_log_dt"].reshape(n16, 1),
                                    (halves("a_row", 0), halves("a_row", 1), halves("f_row", 0), halves("f_row", 1)))
    bc = [_ssm_mats_t(gss[l]["bmat"], gss[l]["cmat"]) for l in range(depth)]
    gsmall = {k: jnp.stack([gss[l][k].reshape(a[k].shape[1:]) for l in range(depth)])
              for k in ("mix_norm_g", "b_gate", "attn_sinks", "ssm_d", "b_ssm_glu", "conv_dw_b", "conv_norm_g", "conv_norm_b",
                        "ffn_norm_g", "ple_norm_g")}
    gsmall["ssm_lambda_re"] = dlr.reshape(a["ssm_lambda_re"].shape)
    gsmall["ssm_lambda_im"] = dli.reshape(a["ssm_lambda_im"].shape)
    gsmall["ssm_log_dt"] = ddt.reshape(a["ssm_log_dt"].shape)
    for i, k in enumerate(("ssm_b_re", "ssm_b_im", "ssm_c_re", "ssm_c_im")):
        gsmall[k] = jnp.stack([bc[l][i] for l in range(depth)])
    gsmall["final_norm_g"] = d_final.reshape(a["final_norm_g"].shape)
    return loss, dx.reshape(n_seq, seq, d), gws, gsmall


def kernel(x, p, positions, mix_norm_g, w_in, b_gate, attn_sinks, w_attn_out, ssm_lambda_re, ssm_lambda_im, ssm_log_dt, ssm_b_re, ssm_b_im, ssm_c_re, ssm_c_im, ssm_d, w_ssm_glu, b_ssm_glu, conv_dw_w, conv_dw_b, conv_norm_g, conv_norm_b, w_conv_out, w_mix_out, ffn_norm_g, w_ffn_in, w_ffn_out, w_ple_in, ple_norm_g, w_ple_gate, final_norm_g, loss_target, m_mix_norm_g, m_w_in, m_b_gate, m_attn_sinks, m_w_attn_out, m_ssm_lambda_re, m_ssm_lambda_im, m_ssm_log_dt, m_ssm_b_re, m_ssm_b_im, m_ssm_c_re, m_ssm_c_im, m_ssm_d, m_w_ssm_glu, m_b_ssm_glu, m_conv_dw_w, m_conv_dw_b, m_conv_norm_g, m_conv_norm_b, m_w_conv_out, m_w_mix_out, m_ffn_norm_g, m_w_ffn_in, m_w_ffn_out, m_w_ple_in, m_ple_norm_g, m_w_ple_gate, m_final_norm_g, v_mix_norm_g, v_w_in, v_b_gate, v_attn_sinks, v_w_attn_out, v_ssm_lambda_re, v_ssm_lambda_im, v_ssm_log_dt, v_ssm_b_re, v_ssm_b_im, v_ssm_c_re, v_ssm_c_im, v_ssm_d, v_w_ssm_glu, v_b_ssm_glu, v_conv_dw_w, v_conv_dw_b, v_conv_norm_g, v_conv_norm_b, v_w_conv_out, v_w_mix_out, v_ffn_norm_g, v_w_ffn_in, v_w_ffn_out, v_w_ple_in, v_ple_norm_g, v_w_ple_gate, v_final_norm_g):
    a = dict(locals())
    depth = w_in.shape[0]
    layouts = {group: _layout(group) for group in GROUPS}
    shift = {"w_in": Z_SPLIT}
    conv_pad = ((0, 0), (0, CONV_HALO - CONV_K), (0, LANES - CONV_WIDTH // N_DEV))

    def packed_weights(l, group, fill):
        col, _, col_rows, row = layouts[group]
        pieces = [_slabs(a[name][l].astype(BF16), fill) for name in col]
        if GROUPS[group][2]:
            pieces.append(jnp.pad(jnp.stack(_split3(a["conv_dw_w"][l])), conv_pad).reshape(-1, LANES))
        return [_concat_padded(pieces, col_rows, 0)] + [a[name][l].astype(BF16) for name in row]

    gathers, tokens, fill = [], [], jnp.zeros((), BF16)
    for l in range(depth):
        started, token = _exchange_start(f"gather_start_{l}", "gather", [packed_weights(l, group, fill) for group in GROUPS])
        gathers.append(dict(zip(GROUPS, started, strict=True)))
        tokens.append(token[0:1, 0:1])
        fill = token[0, 0].astype(BF16)

    def get_weights(l, group, after):
        col, conv_base, _, row = layouts[group]
        slab8, *rows8 = _exchange_wait(f"gather_wait_{group}_{l}", gathers[l][group], after)
        w = {name: _unshard_cols("unshard_" + name, slab8, base, rows, cs, shift.get(name, 0))
             for name, (base, rows, cs) in col.items()}
        for (name, rs), gathered in zip(row.items(), rows8, strict=True):
            w[name] = gathered.reshape(N_DEV * rs, -1)
        if GROUPS[group][2]:
            conv = slab8[:, conv_base:conv_base + CONV_W_PIECES * CONV_HALO]
            conv = conv.reshape(N_DEV, CONV_W_PIECES, CONV_HALO, LANES)[:, :, :CONV_K, :CONV_WIDTH // N_DEV].astype(F32)
            w["conv_dw_w"] = jnp.sum(conv, axis=1).transpose(1, 0, 2).reshape(CONV_K, CONV_WIDTH)
        return w

    scatters = {}

    def on_grads(l, group, gw):
        col, _, col_rows, row = layouts[group]
        pieces = [_shard_cols("shard_" + name, gw[name], cs, shift.get(name, 0)) for name, (_, _, cs) in col.items()]
        if GROUPS[group][2]:
            conv = gw["conv_dw_w"].reshape(CONV_K, N_DEV, CONV_WIDTH // N_DEV).transpose(1, 0, 2).astype(BF16)
            pieces.append(jnp.pad(jnp.pad(conv, conv_pad), ((0, 0), (0, (CONV_W_PIECES - 1) * CONV_HALO), (0, 0))))
        by_shard = [gw[name].astype(BF16).reshape(N_DEV, rs, -1) for name, rs in row.items()]
        (scatters[l, group],), token = _exchange_start(
            f"grads_start_{group}_{l}", "scatter", [[_concat_padded(pieces, col_rows, 1)] + by_shard])
        return token

    local = dict(a, mix_norm_g=a["mix_norm_g"] + sum(tokens))
    loss, grad_x, _, gsmall = _local_step(local, get_weights, on_grads, depth)
    loss = lax.psum(loss, ("x", "y", "c"))

    shapes = [a[k].shape for k in REPLICATED]
    parts, = _all_gather("gather_small_grads", [_pack_small([gsmall[k] for k in REPLICATED])])
    small_state = [_pack_small([a[pre + k] for k in REPLICATED]) for pre in ("", "m_", "v_")]
    small_flat = _adamw_flat("adamw_replicated", parts, *small_state)
    small = [dict(zip(REPLICATED, _unpack_small(o, shapes), strict=True)) for o in small_flat]

    state = lambda name: (a[name], a["m_" + name], a["v_" + name])
    big, after = {}, small_flat[1]
    for group in ("ffn", "mix"):
        col, conv_base, col_rows, row = layouts[group]
        landed = [_exchange_wait(f"grads_wait_{group}_{l}", scatters[l, group], after) for l in range(depth)]
        landed_slab = jnp.concatenate([arrays[0] for arrays in landed], axis=1)
        big.update({name: _adamw_cols("adamw_" + name, landed_slab, base, col_rows, *state(name)) for name, (base, _, _) in col.items()})
        for i, (name, rs) in enumerate(row.items()):
            landed_rows = jnp.concatenate([arrays[1 + i] for arrays in landed], axis=1)
            big[name] = _adamw_rows("adamw_" + name, landed_rows, 0, rs, *state(name))
        if GROUPS[group][2]:
            big["conv_dw_w"] = _adamw_conv(landed_slab, conv_base, col_rows, *state("conv_dw_w"))
        after = big[next(iter(col))][1]

    def result(kind, name):
        if name in REPLICATED:
            return small[kind][name]
        return big[name][kind]

    return (loss, grad_x, *[result(kind, n) for kind in range(4) for n in WEIGHT_ORDER])
```

```python
import functools
import math

import numpy as np
import jax
import jax.numpy as jnp
from jax import lax
from jax.experimental import pallas as pl
from jax.experimental.pallas import tpu as pltpu

F32 = jnp.float32
BF16 = jnp.bfloat16
MXU_DTYPE = jnp.bfloat16
VMEM_LIMIT_BYTES = 56 * 2 ** 20
N_DEV = 8
LANES = 128

HEAD_DIM = 64
N_Q_HEADS = 8
N_KV_HEADS = 2
GQA_GROUP = 4
BLOCK = 128
ROPE_THETA = 500000.0
ROPE_DIM = 16
Q_WIDTH = 512
KV_WIDTH = 128
SSM_WIDTH = 256
SSM_GROUP = 16
SSM_GROUPS = 16
SSM_STATE = 64
SSM_LANES = SSM_GROUPS * SSM_STATE
CONV_WIDTH = 256
CONV_K = 31
CONV_HALO = 32
EPS = 1e-6
NEG_INF = -1e30
ADAM_LR, ADAM_B1, ADAM_B2, ADAM_EPS, ADAM_WD, ADAM_STEP = 0.001, 0.9, 0.999, 1e-08, 0.01, 10

ZG_W, ZQ_W, ZKV_W, ZS_W, ZC_W = 3072, 512, 256, 256, 512
ZQ_BLK, ZKV_BLK, ZS_BLK, ZC_BLK = 3072 // 512, 3584 // 256, 3840 // 256, 4096 // 512
Z_WIDTH = 4608
Z_SPLIT = 1536

SHARDED = (("w_in", 1024, 4608, 1), ("w_attn_out", 512, 1024, 1), ("w_ssm_glu", 256, 2048, 1),
           ("conv_dw_w", 31, 256, 1), ("w_conv_out", 256, 1024, 1), ("w_mix_out", 1024, 1024, 0),
           ("w_ffn_in", 1024, 5632, 1), ("w_ffn_out", 2816, 1024, 0), ("w_ple_in", 256, 1024, 1),
           ("w_ple_gate", 1024, 1024, 0))
GROUPS = {"mix": (("w_in", "w_attn_out", "w_ssm_glu", "w_conv_out"), ("w_mix_out",), True),
          "ffn": (("w_ffn_in", "w_ple_in"), ("w_ffn_out", "w_ple_gate"), False)}
SLAB_TILE = 256
FLAT_ROW_ALIGN = 1024
REPLICATED = ("mix_norm_g", "b_gate", "attn_sinks", "ssm_lambda_re", "ssm_lambda_im", "ssm_log_dt", "ssm_b_re",
              "ssm_b_im", "ssm_c_re", "ssm_c_im", "ssm_d", "b_ssm_glu", "conv_dw_b", "conv_norm_g", "conv_norm_b",
              "ffn_norm_g", "ple_norm_g", "final_norm_g")
WEIGHT_ORDER = ("mix_norm_g", "w_in", "b_gate", "attn_sinks", "w_attn_out", "ssm_lambda_re", "ssm_lambda_im",
                "ssm_log_dt", "ssm_b_re", "ssm_b_im", "ssm_c_re", "ssm_c_im", "ssm_d", "w_ssm_glu", "b_ssm_glu",
                "conv_dw_w", "conv_dw_b", "conv_norm_g", "conv_norm_b", "w_conv_out", "w_mix_out", "ffn_norm_g",
                "w_ffn_in", "w_ffn_out", "w_ple_in", "ple_norm_g", "w_ple_gate", "final_norm_g")


_ANY = pl.BlockSpec(memory_space=pl.ANY)


def _params(sem=None):
    return pltpu.CompilerParams(dimension_semantics=sem, vmem_limit_bytes=VMEM_LIMIT_BYTES)


def _pick(n, cands):
    for c in cands:
        if n % c == 0:
            return c
    return n


def _dot(a, b, dims):
    return lax.dot_general(a.astype(MXU_DTYPE), b.astype(MXU_DTYPE), (dims, ((), ())), preferred_element_type=F32)


def _dot_nn(a, b):
    return _dot(a, b, ((1,), (0,)))


def _dot_nt(a, b):
    return _dot(a, b, ((1,), (1,)))


def _dot_tn(a, b):
    return _dot(a, b, ((0,), (0,)))


@jax.custom_vjp
def _mm(x, w):
    return _dot_nn(x, w)


def _mm_f(x, w):
    return _dot_nn(x, w), (x, w)


def _mm_b(res, dy):
    x, w = res
    return _dot_nt(dy, w).astype(x.dtype), _dot_tn(x, dy).astype(w.dtype)


_mm.defvjp(_mm_f, _mm_b)


def _rms(x, g):
    return x * lax.rsqrt(jnp.mean(x * x, axis=-1, keepdims=True) + EPS) * g


ROW_TILES = (1024, 512, 256, 128)
COL_TILES = (1536, 1408, 1024, 512, 256, 128)


def _matmul_add(name, a, b, residual):
    t, k = a.shape
    n = b.shape[1]
    tm, tn = _pick(t, ROW_TILES), _pick(n, COL_TILES)

    def body(a_ref, b_ref, r_ref, o_ref):
        o_ref[...] = r_ref[...] + _dot_nn(a_ref[...], b_ref[...])

    tile = pl.BlockSpec((tm, tn), lambda i, j: (i, j))
    return pl.pallas_call(
        body, name=name, grid=(t // tm, n // tn), out_shape=jax.ShapeDtypeStruct((t, n), F32),
        in_specs=[pl.BlockSpec((tm, k), lambda i, j: (i, 0)), pl.BlockSpec((k, tn), lambda i, j: (0, j)), tile],
        out_specs=tile, compiler_params=_params(("parallel", "parallel")))(a, b, residual)


def _matmul_tn(name, a, b):
    t, m = a.shape
    n = b.shape[1]
    tm, tn, tt = _pick(m, COL_TILES[1:]), _pick(n, COL_TILES), _pick(t, ROW_TILES)

    def body(a_ref, b_ref, o_ref):
        @pl.when(pl.program_id(2) == 0)
        def _():
            o_ref[...] = jnp.zeros_like(o_ref)

        o_ref[...] += _dot_tn(a_ref[...], b_ref[...])

    return pl.pallas_call(
        body, name=name, grid=(m // tm, n // tn, t // tt), out_shape=jax.ShapeDtypeStruct((m, n), F32),
        in_specs=[pl.BlockSpec((tt, tm), lambda i, j, s: (s, i)), pl.BlockSpec((tt, tn), lambda i, j, s: (s, j))],
        out_specs=pl.BlockSpec((tm, tn), lambda i, j, s: (i, j)),
        compiler_params=_params(("parallel", "parallel", "arbitrary")))(a, b)


def _two_parts(n):
    cut = n // (2 * LANES) * LANES
    return [slice(0, n)] if cut == 0 else [slice(0, cut), slice(cut, n)]


def _in_proj(x, g, w):
    t, d = x.shape
    n = w.shape[1]
    tm, tn = _pick(t, ROW_TILES), _pick(n, COL_TILES)

    def body(x_ref, g_ref, w_ref, z_ref, h_ref):
        @pl.when(pl.program_id(1) == 0)
        def _():
            h_ref[...] = _rms(x_ref[...], g_ref[...]).astype(h_ref.dtype)

        z_ref[...] = _dot_nn(h_ref[...], w_ref[...])

    return pl.pallas_call(
        body, name="in_proj", grid=(t // tm, n // tn),
        out_shape=[jax.ShapeDtypeStruct((t, n), F32), jax.ShapeDtypeStruct((t, d), MXU_DTYPE)],
        in_specs=[pl.BlockSpec((tm, d), lambda i, j: (i, 0)), pl.BlockSpec((1, d), lambda i, j: (0, 0)),
                  pl.BlockSpec((d, tn), lambda i, j: (0, j))],
        out_specs=[pl.BlockSpec((tm, tn), lambda i, j: (i, j)), pl.BlockSpec((tm, d), lambda i, j: (i, 0))],
        compiler_params=_params(("parallel", "arbitrary")))(x, g, w)


def _in_proj_bwd(dz, w, x, dx_res, g):
    t, d = x.shape
    tm = _pick(t, ROW_TILES[1:])

    def body(dz_ref, w_ref, x_ref, r_ref, g_ref, dx_ref, dg_ref):
        _, vjp = jax.vjp(_norm_in_tile, x_ref[...], g_ref[...])
        dx, dg = vjp(_dot_nt(dz_ref[...], w_ref[...]))
        dx_ref[...] = dx + r_ref[...]

        @pl.when(pl.program_id(0) == 0)
        def _():
            dg_ref[...] = jnp.zeros_like(dg_ref)

        dg_ref[...] += dg

    rows = lambda width: pl.BlockSpec((tm, width), lambda i: (i, 0))
    whole = lambda a: pl.BlockSpec(a.shape, lambda i: (0, 0))
    return pl.pallas_call(
        body, name="in_proj_bwd", grid=(t // tm,), out_shape=[jax.ShapeDtypeStruct((t, d), F32), jax.ShapeDtypeStruct((1, d), F32)],
        in_specs=[rows(dz.shape[1]), whole(w), rows(d), rows(d), whole(g)], out_specs=[rows(d), whole(g)],
        compiler_params=_params(("arbitrary",)))(dz, w, x, dx_res, g)


def _ffn_in_act(hf, w_fi):
    t, k = hf.shape
    f = w_fi.shape[1] // 2
    tm, tf = _pick(t, ROW_TILES[1:]), _pick(f, COL_TILES)
    nf = f // tf

    def body(a_ref, wg_ref, wu_ref, g_ref, u_ref, act_ref):
        a = a_ref[...]
        for cols in _two_parts(tf):
            g, u = _dot_nn(a, wg_ref[:, cols]), _dot_nn(a, wu_ref[:, cols])
            g_ref[:, cols] = g.astype(g_ref.dtype)
            u_ref[:, cols] = u.astype(u_ref.dtype)
            act_ref[:, cols] = (jax.nn.silu(g) * u).astype(act_ref.dtype)

    out = pl.BlockSpec((tm, tf), lambda i, j: (i, j))
    return pl.pallas_call(
        body, name="ffn_in_act", grid=(t // tm, nf), out_shape=[jax.ShapeDtypeStruct((t, f), MXU_DTYPE)] * 3,
        in_specs=[pl.BlockSpec((tm, k), lambda i, j: (i, 0)), pl.BlockSpec((k, tf), lambda i, j: (0, j)),
                  pl.BlockSpec((k, tf), lambda i, j: (0, j + nf))],
        out_specs=[out, out, out], compiler_params=_params(("parallel", "parallel")))(hf, w_fi, w_fi)


def _ffn_mid_bwd(dffn, w_fo, gate, up):
    t, d = dffn.shape
    f = w_fo.shape[0]
    tm, tf = _pick(t, ROW_TILES[1:]), _pick(f, COL_TILES)

    def body(a_ref, w_ref, g_ref, u_ref, dg_ref, du_ref):
        a = a_ref[...]
        for cols in _two_parts(tf):
            dact = _dot_nt(a, w_ref[cols, :])
            g, u = g_ref[:, cols].astype(F32), u_ref[:, cols].astype(F32)
            sg = jax.nn.sigmoid(g)
            dg_ref[:, cols] = (dact * u * sg * (1.0 + g * (1.0 - sg))).astype(dg_ref.dtype)
            du_ref[:, cols] = (dact * g * sg).astype(du_ref.dtype)

    blk = pl.BlockSpec((tm, tf), lambda i, j: (i, j))
    return pl.pallas_call(
        body, name="ffn_mid_bwd", grid=(t // tm, f // tf), out_shape=[jax.ShapeDtypeStruct((t, f), MXU_DTYPE)] * 2,
        in_specs=[pl.BlockSpec((tm, d), lambda i, j: (i, 0)), pl.BlockSpec((tf, d), lambda i, j: (j, 0)), blk, blk],
        out_specs=[blk, blk], compiler_params=_params(("parallel", "parallel")))(dffn, w_fo, gate, up)


def _ffn_in_dx(dgate, dup, w_fi):
    t, f = dgate.shape
    d = w_fi.shape[0]
    tm = _pick(t, ROW_TILES[1:])

    def body(g_ref, u_ref, w_ref, o_ref):
        o_ref[...] = _dot_nt(g_ref[...], w_ref[:, :f]) + _dot_nt(u_ref[...], w_ref[:, f:])

    blk = pl.BlockSpec((tm, f), lambda i: (i, 0))
    return pl.pallas_call(
        body, name="ffn_in_dx", grid=(t // tm,), out_shape=jax.ShapeDtypeStruct((t, d), F32),
        in_specs=[blk, blk, pl.BlockSpec(w_fi.shape, lambda i: (0, 0))], out_specs=pl.BlockSpec((tm, d), lambda i: (i, 0)),
        compiler_params=_params(("parallel",)))(dgate, dup, w_fi)


def _token_call(name, fn, tile, tok_ins, consts, tok_outs, acc_outs, into=None):
    n_rows = tok_ins[0][0].shape[0]
    tile = min(tile, n_rows)
    n_ti, n_c = len(tok_ins), len(consts)
    n_in = n_ti + n_c + (into is not None)
    n_to = len(tok_outs) + (into is not None)

    def body(*refs):
        ins = [r[...] for r in refs[:n_ti + n_c]]
        outs, accs = fn(*ins)
        for r, v in zip(refs[n_in:n_in + n_to], outs, strict=True):
            r[...] = v.astype(r.dtype)
        first = pl.program_id(0) == 0
        for r, v in zip(refs[n_in + n_to:], accs, strict=True):
            @pl.when(first)
            def _(r=r):
                r[...] = jnp.zeros_like(r)

            r[...] += jnp.broadcast_to(v, r.shape).astype(F32)

    in_specs = [pl.BlockSpec((tile, w), functools.partial(lambda i, c: (i, c), c=cb)) for _, w, cb in tok_ins]
    in_specs += [pl.BlockSpec(c.shape, lambda i: (0, 0)) for c in consts]
    out_shape = [jax.ShapeDtypeStruct((n_rows, w), dt) for w, dt in tok_outs]
    out_specs = [pl.BlockSpec((tile, w), lambda i: (i, 0)) for w, _ in tok_outs]
    operands = [a for a, _, _ in tok_ins] + list(consts)
    aliases = {}
    if into is not None:
        target, width, col_block = into
        in_specs.append(_ANY)
        operands.append(target)
        out_shape.append(jax.ShapeDtypeStruct(target.shape, target.dtype))
        out_specs.append(pl.BlockSpec((tile, width), lambda i: (i, col_block)))
        aliases = {n_in - 1: n_to - 1}
    out_shape += [jax.ShapeDtypeStruct(s, F32) for s in acc_outs]
    out_specs += [pl.BlockSpec(s, lambda i: (0, 0)) for s in acc_outs]
    res = pl.pallas_call(
        body, name=name, grid=(n_rows // tile,), out_shape=out_shape, in_specs=in_specs, out_specs=out_specs,
        input_output_aliases=aliases, compiler_params=_params(("arbitrary",)))(*operands)
    return res[:n_to], res[n_to:]


def _whole(a):
    return (a, a.shape[1], 0)


def _norm_in_tile(x, g):
    return _rms(x, g)


def _conv_post_tile(v, g, b):
    mu = jnp.mean(v, axis=-1, keepdims=True)
    var = jnp.mean(jnp.square(v - mu), axis=-1, keepdims=True)
    return jax.nn.silu((v - mu) * lax.rsqrt(var + EPS) * g + b)


def _branches_tile(ya, ys, v, gin, w_ao, w_sg, b_sg, ln_g, ln_b, w_co, b_gate):
    d = w_ao.shape[1]
    y_attn = _mm(ya, w_ao)
    pre = _mm(jax.nn.gelu(ys), w_sg) + b_sg
    y_ssm = pre[:, :d] * jax.nn.sigmoid(pre[:, d:])
    y_conv = _mm(_conv_post_tile(v, ln_g, ln_b), w_co)
    gates = jax.nn.sigmoid(gin + b_gate)
    return gates[:, :d] * y_attn + gates[:, d:2 * d] * y_ssm + gates[:, 2 * d:] * y_conv


def _merge_tile(x, ya, ys, v, gin, w_ao, w_sg, b_sg, ln_g, ln_b, w_co, b_gate, w_mo, g_ffn):
    x1 = x + _mm(_branches_tile(ya, ys, v, gin, w_ao, w_sg, b_sg, ln_g, ln_b, w_co, b_gate), w_mo)
    return x1, _rms(x1, g_ffn)


def _merge_bwd_tile(x1, ya, ys, v, gin, dx1, dhf, w_ao, w_sg, b_sg, ln_g, ln_b, w_co, b_gate, w_mo, g_ffn):
    _, norm_vjp = jax.vjp(_rms, x1, g_ffn)
    dx1_norm, dg_ffn = norm_vjp(dhf)
    dx1 = dx1 + dx1_norm
    merged, branch_vjp = jax.vjp(_branches_tile, ya, ys, v, gin, w_ao, w_sg, b_sg, ln_g, ln_b, w_co, b_gate)
    grads = branch_vjp(_dot_nt(dx1, w_mo))
    return [dx1, *grads[:4]], [*grads[4:], _dot_tn(merged, dx1), dg_ffn]


def _ple_tile(x2, p, w_pi, g_ple, w_pg):
    pre, e = _mm(_rms(x2, g_ple), w_pg), _mm(p, w_pi)
    return x2 + jax.nn.sigmoid(pre) * e, pre, e


def _ple_bwd_tile(x2, p, pre, e, dx3, w_pi, g_ple, w_pg):
    sig = jax.nn.sigmoid(pre)
    dpre = dx3 * e * sig * (1.0 - sig)
    hn, norm_vjp = jax.vjp(_rms, x2, g_ple)
    dx2_norm, dg_ple = norm_vjp(_dot_nt(dpre, w_pg))
    return dx3 + dx2_norm, [_dot_tn(p, dx3 * sig), dg_ple, _dot_tn(hn, dpre)]


def _f32s(vals):
    return [v.astype(F32) for v in vals]


def _rope_tables(positions, inv_lane):
    def fn(pos, inv):
        ang = pos.astype(F32) * inv
        j = lax.broadcasted_iota(jnp.int32, ang.shape, 1) % HEAD_DIM
        c = jnp.where(j < ROPE_DIM, jnp.cos(ang), 1.0)
        s = jnp.sin(ang)
        s = jnp.where(j < ROPE_DIM // 2, -s, jnp.where(j < ROPE_DIM, s, 0.0))
        return [c, s], []

    (c, s), _ = _token_call("rope_tables", fn, 1024, [_whole(positions.reshape(-1, 1))], [inv_lane],
                            [(LANES, F32), (LANES, F32)], [])
    return c, s


def _swap_halves(t):
    n = t.shape[1]
    j = lax.broadcasted_iota(jnp.int32, t.shape, 1) % HEAD_DIM
    lower = pltpu.roll(t, n - ROPE_DIM // 2, 1)
    upper = jnp.where(j < ROPE_DIM, pltpu.roll(t, ROPE_DIM // 2, 1), 0.0)
    return jnp.where(j < ROPE_DIM // 2, lower, upper)


def _rope(t, c, s):
    return t * c + _swap_halves(t) * s


def _rope_t(dt, c, s):
    return dt * c + _swap_halves(dt * s)


def _tile4(a):
    return jnp.concatenate([a] * (Q_WIDTH // LANES), axis=1)


def _attn_mask(n):
    qi = lax.broadcasted_iota(jnp.int32, (GQA_GROUP * BLOCK, 2 * BLOCK), 0) % BLOCK
    kj = lax.broadcasted_iota(jnp.int32, (GQA_GROUP * BLOCK, 2 * BLOCK), 1)
    dist = qi + BLOCK - kj
    return (dist >= 0) & (dist < BLOCK) & ((n > 0) | (kj >= BLOCK))


def _attn_specs(n_seq):
    own = lambda w, blk: pl.BlockSpec((n_seq, BLOCK, w), lambda n: (0, n, blk))
    prev = lambda w, blk: pl.BlockSpec((n_seq, BLOCK, w), lambda n: (0, jnp.maximum(n - 1, 0), blk))
    return [own(ZQ_W, ZQ_BLK), own(ZKV_W, ZKV_BLK), prev(ZKV_W, ZKV_BLK), own(LANES, 0), own(LANES, 0), prev(LANES, 0),
            prev(LANES, 0), pl.BlockSpec((1, N_Q_HEADS), lambda n: (0, 0))]


def _by_seq(a, n_seq):
    return a.reshape(n_seq, a.shape[0] // n_seq, a.shape[1])


ATTN_SCALE = HEAD_DIM ** -0.5


def _stack_heads(t, kh):
    return jnp.concatenate([t[:, (kh * GQA_GROUP + g) * HEAD_DIM:(kh * GQA_GROUP + g + 1) * HEAD_DIM]
                            for g in range(GQA_GROUP)], axis=0)


def _stack_sinks(sink, kh):
    return jnp.concatenate([jnp.broadcast_to(sink[:, kh * GQA_GROUP + g:kh * GQA_GROUP + g + 1], (BLOCK, 1))
                            for g in range(GQA_GROUP)], axis=0)


def _attn_band(b, q_ref, kv_ref, kvp_ref, c_ref, s_ref, cp_ref, sp_ref):
    c, s = c_ref[b], s_ref[b]
    q = _rope(q_ref[b], _tile4(c), _tile4(s)) * ATTN_SCALE
    kv, kvp = kv_ref[b], kvp_ref[b]
    k = _rope(kv[:, :KV_WIDTH], c, s)
    kp = _rope(kvp[:, :KV_WIDTH], cp_ref[b], sp_ref[b])
    kb = jnp.concatenate([kp, k], axis=0)
    vb = jnp.concatenate([kvp[:, KV_WIDTH:], kv[:, KV_WIDTH:]], axis=0)
    return q, kb, vb


def _attention_fwd(z, ctab, stab, sinks, n_seq):
    t = z.shape[0]
    seq = t // n_seq

    def body(q_ref, kv_ref, kvp_ref, c_ref, s_ref, cp_ref, sp_ref, sink_ref, o_ref, lse_ref):
        mask = _attn_mask(pl.program_id(0))
        sink = sink_ref[...]
        lane = lax.broadcasted_iota(jnp.int32, (BLOCK, N_Q_HEADS), 1)
        for b in range(n_seq):
            q, kb, vb = _attn_band(b, q_ref, kv_ref, kvp_ref, c_ref, s_ref, cp_ref, sp_ref)
            lse_all = jnp.zeros((BLOCK, N_Q_HEADS), F32)
            for kh in range(N_KV_HEADS):
                sc = jnp.where(mask, _dot_nt(_stack_heads(q, kh), kb[:, kh * HEAD_DIM:(kh + 1) * HEAD_DIM]), NEG_INF)
                sk = _stack_sinks(sink, kh)
                m = jnp.maximum(jnp.max(sc, axis=-1, keepdims=True), sk)
                pr = jnp.exp(sc - m)
                den = jnp.sum(pr, axis=-1, keepdims=True) + jnp.exp(sk - m)
                out = _dot_nn(pr * (1.0 / den), vb[:, kh * HEAD_DIM:(kh + 1) * HEAD_DIM])
                lse = m + jnp.log(den)
                for g in range(GQA_GROUP):
                    h = kh * GQA_GROUP + g
                    o_ref[b, :, h * HEAD_DIM:(h + 1) * HEAD_DIM] = out[g * BLOCK:(g + 1) * BLOCK].astype(o_ref.dtype)
                    lse_all = jnp.where(lane == h, lse[g * BLOCK:(g + 1) * BLOCK], lse_all)
            lse_ref[b] = lse_all

    rows = lambda w: pl.BlockSpec((n_seq, BLOCK, w), lambda n: (0, n, 0))
    z3, c3, s3 = _by_seq(z, n_seq), _by_seq(ctab, n_seq), _by_seq(stab, n_seq)
    ya, lse = pl.pallas_call(
        body, name="attn_fwd", grid=(seq // BLOCK,),
        out_shape=[jax.ShapeDtypeStruct((n_seq, seq, Q_WIDTH), MXU_DTYPE), jax.ShapeDtypeStruct((n_seq, seq, N_Q_HEADS), F32)],
        in_specs=_attn_specs(n_seq), out_specs=[rows(Q_WIDTH), rows(N_Q_HEADS)],
        compiler_params=_params(("parallel",)))(z3, z3, z3, c3, s3, c3, s3, sinks)
    return ya.reshape(t, Q_WIDTH), lse.reshape(t, N_Q_HEADS)


def _attention_bwd(z, ctab, stab, sinks, ya, lse, dya, dz, n_seq):
    t = z.shape[0]
    seq = t // n_seq

    def body(q_ref, kv_ref, kvp_ref, c_ref, s_ref, cp_ref, sp_ref, sink_ref, o_ref, lse_ref, do_ref, _,
             dq_ref, dkv_ref, dkvp_ref, dsink_ref):
        mask = _attn_mask(pl.program_id(0))
        sink = sink_ref[...]
        lane = lax.broadcasted_iota(jnp.int32, (1, N_Q_HEADS), 1)
        dsink = jnp.zeros((1, N_Q_HEADS), F32)
        for b in range(n_seq):
            q, kb, vb = _attn_band(b, q_ref, kv_ref, kvp_ref, c_ref, s_ref, cp_ref, sp_ref)
            lse_all = lse_ref[b]
            o = o_ref[b].astype(F32)
            do = do_ref[b].astype(F32)
            dq_parts = []
            dk_parts, dv_parts = [], []
            for kh in range(N_KV_HEADS):
                kbh = kb[:, kh * HEAD_DIM:(kh + 1) * HEAD_DIM]
                vbh = vb[:, kh * HEAD_DIM:(kh + 1) * HEAD_DIM]
                qs, dos = _stack_heads(q, kh), _stack_heads(do, kh)
                lse = jnp.concatenate([lse_all[:, kh * GQA_GROUP + g:kh * GQA_GROUP + g + 1] for g in range(GQA_GROUP)], axis=0)
                pr = jnp.exp(jnp.where(mask, _dot_nt(qs, kbh), NEG_INF) - lse)
                delta = jnp.sum(dos * _stack_heads(o, kh), axis=-1, keepdims=True)
                ds = pr * (_dot_nt(dos, vbh) - delta)
                dqs = _dot_nn(ds, kbh)
                dq_parts += [dqs[g * BLOCK:(g + 1) * BLOCK] for g in range(GQA_GROUP)]
                dk_parts.append(_dot_tn(ds, qs))
                dv_parts.append(_dot_tn(pr, dos))
                dsk = jnp.exp(_stack_sinks(sink, kh) - lse) * delta
                for g in range(GQA_GROUP):
                    dsink = dsink + jnp.where(lane == kh * GQA_GROUP + g, -jnp.sum(dsk[g * BLOCK:(g + 1) * BLOCK]), 0.0)
            c, s = c_ref[b], s_ref[b]
            dq_ref[b] = _rope_t(jnp.concatenate(dq_parts, axis=1) * ATTN_SCALE, _tile4(c), _tile4(s)).astype(dq_ref.dtype)
            dk = jnp.concatenate(dk_parts, axis=1)
            dv = jnp.concatenate(dv_parts, axis=1)
            dkv_ref[b, :, :KV_WIDTH] = _rope_t(dk[BLOCK:], c, s)
            dkv_ref[b, :, KV_WIDTH:] = dv[BLOCK:]
            dkvp_ref[b, :, :KV_WIDTH] = _rope_t(dk[:BLOCK], cp_ref[b], sp_ref[b])
            dkvp_ref[b, :, KV_WIDTH:] = dv[:BLOCK]

        @pl.when(pl.program_id(0) == 0)
        def _():
            dsink_ref[...] = jnp.zeros_like(dsink_ref)

        dsink_ref[...] += dsink

    rows = lambda w: pl.BlockSpec((n_seq, BLOCK, w), lambda n: (0, n, 0))
    by_seq = lambda a: _by_seq(a, n_seq)
    z3, c3, s3 = by_seq(z), by_seq(ctab), by_seq(stab)
    dz, dkv, dkvp, dsink = pl.pallas_call(
        body, name="attn_bwd", grid=(seq // BLOCK,),
        out_shape=[jax.ShapeDtypeStruct((n_seq, seq, Z_WIDTH), dz.dtype), jax.ShapeDtypeStruct((n_seq, seq, ZKV_W), F32),
                   jax.ShapeDtypeStruct((n_seq, seq, ZKV_W), F32), jax.ShapeDtypeStruct((1, N_Q_HEADS), F32)],
        in_specs=_attn_specs(n_seq) + [rows(Q_WIDTH), rows(N_Q_HEADS), rows(Q_WIDTH), _ANY],
        out_specs=[pl.BlockSpec((n_seq, BLOCK, ZQ_W), lambda n: (0, n, ZQ_BLK)), rows(ZKV_W), rows(ZKV_W),
                   pl.BlockSpec((1, N_Q_HEADS), lambda n: (0, 0))],
        input_output_aliases={11: 0},
        compiler_params=_params(("arbitrary",)))(z3, z3, z3, c3, s3, c3, s3, sinks, by_seq(ya), by_seq(lse), by_seq(dya), by_seq(dz))
    return dz.reshape(t, Z_WIDTH), dkv.reshape(t, ZKV_W), dkvp.reshape(t, ZKV_W), dsink


def _kv_combine(dkv, dkvp, dz, n_seq):
    t = dkv.shape[0]
    seq = t // n_seq
    rows = _pick(seq, (512, 256, 128))
    nt, per = seq // rows, rows // BLOCK
    n_blocks = t // BLOCK

    def body(dkv_ref, dkvp_ref, dkvn_ref, _, o_ref):
        nxt = jnp.where(pl.program_id(1) == nt - 1, 0.0, dkvn_ref[...])
        shifted = nxt if per == 1 else jnp.concatenate([dkvp_ref[BLOCK:, :], nxt], axis=0)
        o_ref[...] = (dkv_ref[...] + shifted).astype(o_ref.dtype)

    tile = pl.BlockSpec((rows, ZKV_W), lambda b, i: (b * nt + i, 0))
    return pl.pallas_call(
        body, name="kv_combine", grid=(n_seq, nt), out_shape=jax.ShapeDtypeStruct(dz.shape, dz.dtype),
        in_specs=[tile, tile,
                  pl.BlockSpec((BLOCK, ZKV_W), lambda b, i: (jnp.minimum((b * nt + i + 1) * per, n_blocks - 1), 0)), _ANY],
        out_specs=pl.BlockSpec((rows, ZKV_W), lambda b, i: (b * nt + i, ZKV_BLK)), input_output_aliases={3: 0},
        compiler_params=_params(("parallel", "parallel")))(dkv, dkvp, dkvp, dz)


def _ssm_coeff_tile(lam_re, lam_im, log_dt):
    lr = jnp.minimum(lam_re, -1e-4)
    dt = jnp.exp(log_dt)
    mag = jnp.exp(lr * dt)
    a_re = mag * jnp.cos(lam_im * dt)
    a_im = mag * jnp.sin(lam_im * dt)
    den = lr * lr + lam_im * lam_im
    x_re = a_re - 1.0
    f_re = (x_re * lr + a_im * lam_im) / den
    f_im = (a_im * lr - x_re * lam_im) / den
    return a_re, a_im, f_re, f_im


def _ssm_coeffs(lam_re, lam_im, log_dt):
    def body(lr_ref, li_ref, dt_ref, *o_refs):
        for r, v in zip(o_refs, _ssm_coeff_tile(lr_ref[...], li_ref[...], dt_ref[...]), strict=True):
            r[...] = v

    return pl.pallas_call(body, name="ssm_coeffs", out_shape=[jax.ShapeDtypeStruct(lam_re.shape, F32)] * 4)(
        lam_re, lam_im, log_dt)


def _ssm_coeffs_bwd(lam_re, lam_im, log_dt, cts):
    def body(lr_ref, li_ref, dt_ref, c0, c1, c2, c3, dlr_ref, dli_ref, ddt_ref):
        _, vjp = jax.vjp(_ssm_coeff_tile, lr_ref[...], li_ref[...], dt_ref[...])
        dlr, dli, ddt = vjp((c0[...], c1[...], c2[...], c3[...]))
        dlr_ref[...] = dlr
        dli_ref[...] = dli
        ddt_ref[...] = ddt

    return pl.pallas_call(
        body, name="ssm_coeffs_bwd",
        out_shape=[jax.ShapeDtypeStruct(lam_re.shape, F32)] * 2 + [jax.ShapeDtypeStruct(log_dt.shape, F32)])(
        lam_re, lam_im, log_dt, *cts)


def _ssm_chunk(t):
    return _pick(t, (256, 128))


def _ssm_fwd(z, bmat, a_row, f_row, cmat, d_row, n_seq):
    t = z.shape[0]
    seq = t // n_seq
    lc = _ssm_chunk(seq)
    nc = seq // lc
    n2 = 2 * SSM_LANES

    re, im = pl.ds(0, SSM_LANES), pl.ds(SSM_LANES, SSM_LANES)

    def body(u_ref, b_ref, a_ref, f_ref, c_ref, d_ref, y_ref, s_ref, bu_ref, st_ref):
        @pl.when(pl.program_id(0) == 0)
        def _():
            st_ref[...] = jnp.zeros_like(st_ref)

        fr, fi = f_ref[:, :SSM_LANES], f_ref[:, SSM_LANES:]
        for b in range(n_seq):
            proj = _dot_nn(u_ref[b], b_ref[...])
            pr, pi = proj[:, :SSM_LANES], proj[:, SSM_LANES:]
            bu_ref[b, :, :SSM_LANES] = fr * pr - fi * pi
            bu_ref[b, :, SSM_LANES:] = fr * pi + fi * pr
        ar, ai = a_ref[:, :SSM_LANES], a_ref[:, SSM_LANES:]

        def step(i, carry):
            out = []
            for b in range(n_seq):
                sr, si = carry[2 * b], carry[2 * b + 1]
                nr = ar * sr - ai * si + bu_ref[b, pl.ds(i, 1), re]
                ni = ar * si + ai * sr + bu_ref[b, pl.ds(i, 1), im]
                s_ref[b, pl.ds(i, 1), re] = nr
                s_ref[b, pl.ds(i, 1), im] = ni
                out += [nr, ni]
            return tuple(out)

        carry = lax.fori_loop(0, lc, step, tuple(st_ref[b, 0:1, part] for b in range(n_seq) for part in (re, im)), unroll=8)
        for b in range(n_seq):
            st_ref[b, 0:1, re], st_ref[b, 0:1, im] = carry[2 * b], carry[2 * b + 1]
            y_ref[b] = _dot_nn(s_ref[b], c_ref[...]) + d_ref[...] * u_ref[b]

    const = lambda shape: pl.BlockSpec(shape, lambda c: (0, 0))
    rows = lambda w, cb: pl.BlockSpec((n_seq, lc, w), lambda c: (0, c, cb))
    ys, states = pl.pallas_call(
        body, name="ssm_fwd", grid=(nc,),
        out_shape=[jax.ShapeDtypeStruct((n_seq, seq, SSM_WIDTH), F32), jax.ShapeDtypeStruct((n_seq, seq, n2), F32)],
        in_specs=[rows(ZS_W, ZS_BLK), const((SSM_WIDTH, n2)), const((1, n2)), const((1, n2)), const((n2, SSM_WIDTH)),
                  const((1, SSM_WIDTH))],
        out_specs=[rows(SSM_WIDTH, 0), rows(n2, 0)],
        scratch_shapes=[pltpu.VMEM((n_seq, lc, n2), F32), pltpu.VMEM((n_seq, 8, n2), F32)],
        compiler_params=_params(("arbitrary",)))(_by_seq(z, n_seq), bmat, a_row, f_row, cmat, d_row)
    return ys.reshape(t, SSM_WIDTH), states.reshape(t, n2)


def _ssm_bwd(z, states, dy, bmat, a_row, f_row, cmat, d_row, dz, n_seq):
    t = z.shape[0]
    seq = t // n_seq
    lc = _ssm_chunk(seq)
    nc = seq // lc
    n2 = 2 * SSM_LANES

    def body(dy_ref, u_ref, s_ref, b_ref, a_ref, f_ref, c_ref, d_ref, _,
             du_ref, db_ref, dc_ref, da_ref, df_ref, dd_ref, g_ref, carry_ref):
        @pl.when(pl.program_id(0) == 0)
        def _():
            for r in (db_ref, dc_ref, da_ref, df_ref, dd_ref, carry_ref):
                r[...] = jnp.zeros_like(r)

        re, im = pl.ds(0, SSM_LANES), pl.ds(SSM_LANES, SSM_LANES)
        for b in range(n_seq):
            dy = dy_ref[b]
            g_ref[b, 0:lc, :] = _dot_nt(dy, c_ref[...])
            g_ref[b, lc:lc + 8, :] = carry_ref[b]
            dc_ref[...] += _dot_tn(s_ref[b], dy)
            dd_ref[...] += jnp.sum(dy * u_ref[b], axis=0, keepdims=True)
        ar, ai = a_ref[:, :SSM_LANES], a_ref[:, SSM_LANES:]

        def step(i, carry):
            r = lc - 1 - i
            out = []
            for b in range(n_seq):
                gr, gi = carry[2 * b], carry[2 * b + 1]
                nr = g_ref[b, pl.ds(r, 1), re] + ar * gr + ai * gi
                ni = g_ref[b, pl.ds(r, 1), im] - ai * gr + ar * gi
                g_ref[b, pl.ds(r, 1), re] = nr
                g_ref[b, pl.ds(r, 1), im] = ni
                out += [nr, ni]
            return tuple(out)

        carry = lax.fori_loop(0, lc, step, tuple(carry_ref[b, 0:1, part] for b in range(n_seq) for part in (re, im)), unroll=8)
        fr, fi = f_ref[:, :SSM_LANES], f_ref[:, SSM_LANES:]
        for b in range(n_seq):
            carry_ref[b, 0:1, re], carry_ref[b, 0:1, im] = carry[2 * b], carry[2 * b + 1]
            dy, u, st = dy_ref[b], u_ref[b], s_ref[b]
            sr, si = st[:, :SSM_LANES], st[:, SSM_LANES:]
            gnr, gni = g_ref[b, pl.ds(1, lc), re], g_ref[b, pl.ds(1, lc), im]
            da_ref[:, :SSM_LANES] += jnp.sum(gnr * sr + gni * si, axis=0, keepdims=True)
            da_ref[:, SSM_LANES:] += jnp.sum(gni * sr - gnr * si, axis=0, keepdims=True)
            gr_all, gi_all = g_ref[b, 0:lc, :SSM_LANES], g_ref[b, 0:lc, SSM_LANES:]
            proj = _dot_nn(u, b_ref[...])
            pr, pi = proj[:, :SSM_LANES], proj[:, SSM_LANES:]
            df_ref[:, :SSM_LANES] += jnp.sum(gr_all * pr + gi_all * pi, axis=0, keepdims=True)
            df_ref[:, SSM_LANES:] += jnp.sum(gi_all * pr - gr_all * pi, axis=0, keepdims=True)
            dproj = jnp.concatenate([fr * gr_all + fi * gi_all, fr * gi_all - fi * gr_all], axis=1).astype(MXU_DTYPE)
            du_ref[b] = (_dot_nt(dproj, b_ref[...]) + d_ref[...] * dy).astype(du_ref.dtype)
            db_ref[...] += _dot_tn(u, dproj)

    const = lambda shape: pl.BlockSpec(shape, lambda c: (0, 0))
    rows = lambda w, cb: pl.BlockSpec((n_seq, lc, w), lambda c: (0, nc - 1 - c, cb))
    by_seq = lambda a: _by_seq(a, n_seq)
    dz, *sums = pl.pallas_call(
        body, name="ssm_bwd", grid=(nc,),
        out_shape=[jax.ShapeDtypeStruct((n_seq, seq, Z_WIDTH), dz.dtype), jax.ShapeDtypeStruct((SSM_WIDTH, n2), F32),
                   jax.ShapeDtypeStruct((n2, SSM_WIDTH), F32), jax.ShapeDtypeStruct((1, n2), F32),
                   jax.ShapeDtypeStruct((1, n2), F32), jax.ShapeDtypeStruct((1, SSM_WIDTH), F32)],
        in_specs=[rows(SSM_WIDTH, 0), rows(ZS_W, ZS_BLK), rows(n2, 0), const((SSM_WIDTH, n2)), const((1, n2)),
                  const((1, n2)), const((n2, SSM_WIDTH)), const((1, SSM_WIDTH)), _ANY],
        out_specs=[rows(ZS_W, ZS_BLK), const((SSM_WIDTH, n2)), const((n2, SSM_WIDTH)), const((1, n2)), const((1, n2)),
                   const((1, SSM_WIDTH))],
        input_output_aliases={8: 0},
        scratch_shapes=[pltpu.VMEM((n_seq, lc + 8, n2), F32), pltpu.VMEM((n_seq, 8, n2), F32)],
        compiler_params=_params(("arbitrary",)))(by_seq(dy), by_seq(z), by_seq(states), bmat, a_row, f_row, cmat, d_row, by_seq(dz))
    return (dz.reshape(t, Z_WIDTH), *sums)


def _conv_chunk(t):
    return _pick(t, (512, 256, 128))


def _glu(c):
    return c[:, :CONV_WIDTH] * jax.nn.sigmoid(c[:, CONV_WIDTH:])


def _conv_specs(lc, nc):
    per = lc // CONV_HALO
    return [pl.BlockSpec((lc, ZC_W), lambda b, c: (b * nc + c, ZC_BLK)),
            pl.BlockSpec((CONV_HALO, ZC_W), lambda b, c: (jnp.maximum((b * nc + c) * per - 1, 0), ZC_BLK))]


def _conv_fill(c_ref, cp_ref, ue_ref, lc):
    ue_ref[0:CONV_HALO, :] = jnp.where(pl.program_id(1) > 0, _glu(cp_ref[...]), 0.0)
    ue_ref[CONV_HALO:CONV_HALO + lc, :] = _glu(c_ref[...])


CONV_ROWS = 64
SUBLANES = 8
CONV_SHIFT_ROWS = CONV_HALO - SUBLANES


def _shifted_copies(src_ref, sh_ref, lc):
    for b in range(1, SUBLANES):
        sh_ref[b - 1, :, :] = src_ref[pl.ds(b, lc + CONV_SHIFT_ROWS), :]


def _tap_rows(src_ref, sh_ref, offset, r0):
    b = offset % SUBLANES
    rows = pl.ds(r0 + offset - b, CONV_ROWS)
    return src_ref[rows, :] if b == 0 else sh_ref[b - 1, rows, :]


def _conv_apply(ue_ref, ush_ref, w_ref, b_ref, o_ref, lc):
    for r0 in range(0, lc, CONV_ROWS):
        acc = jnp.zeros((CONV_ROWS, CONV_WIDTH), F32) + b_ref[...]
        for k in range(CONV_K):
            acc = acc + w_ref[k:k + 1, :] * _tap_rows(ue_ref, ush_ref, k + CONV_HALO - CONV_K + 1, r0)
        o_ref[r0:r0 + CONV_ROWS, :] = acc


def _conv_fwd(z, dw_w, dw_b, n_seq):
    t = z.shape[0]
    seq = t // n_seq
    lc = _conv_chunk(seq)
    nc = seq // lc

    def body(c_ref, cp_ref, w_ref, b_ref, o_ref, ue_ref, ush_ref):
        _conv_fill(c_ref, cp_ref, ue_ref, lc)
        _shifted_copies(ue_ref, ush_ref, lc)
        _conv_apply(ue_ref, ush_ref, w_ref, b_ref, o_ref, lc)

    const = lambda a: pl.BlockSpec(a.shape, lambda b, c: (0, 0))
    return pl.pallas_call(
        body, name="conv_fwd", grid=(n_seq, nc), out_shape=jax.ShapeDtypeStruct((t, CONV_WIDTH), F32),
        in_specs=_conv_specs(lc, nc) + [const(dw_w), const(dw_b)],
        out_specs=pl.BlockSpec((lc, CONV_WIDTH), lambda b, c: (b * nc + c, 0)),
        scratch_shapes=[pltpu.VMEM((CONV_HALO + lc, CONV_WIDTH), F32),
                        pltpu.VMEM((SUBLANES - 1, lc + CONV_SHIFT_ROWS, CONV_WIDTH), F32)],
        compiler_params=_params(("parallel", "parallel")))(z, z, dw_w, dw_b)


def _conv_bwd_taps(z, dv, dw_w, dz, n_seq):
    t = z.shape[0]
    seq = t // n_seq
    lc = _conv_chunk(seq)
    nc = seq // lc
    per = lc // CONV_HALO
    n_halo = t // CONV_HALO

    def body(c_ref, cp_ref, dv_ref, dvn_ref, w_ref, _, dc_ref, dw_ref, ue_ref, dve_ref, ush_ref, dsh_ref):
        @pl.when((pl.program_id(0) == 0) & (pl.program_id(1) == 0))
        def _():
            dw_ref[...] = jnp.zeros_like(dw_ref)

        _conv_fill(c_ref, cp_ref, ue_ref, lc)
        dv = dv_ref[...]
        dve_ref[0:lc, :] = dv
        dve_ref[lc:lc + CONV_HALO, :] = jnp.where(pl.program_id(1) < nc - 1, dvn_ref[...], 0.0)
        _shifted_copies(ue_ref, ush_ref, lc)
        _shifted_copies(dve_ref, dsh_ref, lc)
        dw_ref[CONV_K:CONV_K + 1, :] += jnp.sum(dv, axis=0, keepdims=True)
        for r0 in range(0, lc, CONV_ROWS):
            rows = pl.ds(r0, CONV_ROWS)
            dv_rows = dv_ref[rows, :]
            du = jnp.zeros((CONV_ROWS, CONV_WIDTH), F32)
            for k in range(CONV_K):
                du = du + w_ref[k:k + 1, :] * _tap_rows(dve_ref, dsh_ref, CONV_K - 1 - k, r0)
                taps = _tap_rows(ue_ref, ush_ref, k + CONV_HALO - CONV_K + 1, r0)
                dw_ref[k:k + 1, :] += jnp.sum(dv_rows * taps, axis=0, keepdims=True)
            c = c_ref[rows, :]
            a, sg = c[:, :CONV_WIDTH], jax.nn.sigmoid(c[:, CONV_WIDTH:])
            dc_ref[rows, :CONV_WIDTH] = (du * sg).astype(dc_ref.dtype)
            dc_ref[rows, CONV_WIDTH:] = (du * a * sg * (1.0 - sg)).astype(dc_ref.dtype)

    return pl.pallas_call(
        body, name="conv_bwd_taps", grid=(n_seq, nc),
        out_shape=[jax.ShapeDtypeStruct(dz.shape, dz.dtype), jax.ShapeDtypeStruct((CONV_HALO, CONV_WIDTH), F32)],
        in_specs=_conv_specs(lc, nc) + [
            pl.BlockSpec((lc, CONV_WIDTH), lambda b, c: (b * nc + c, 0)),
            pl.BlockSpec((CONV_HALO, CONV_WIDTH), lambda b, c: (jnp.minimum((b * nc + c + 1) * per, n_halo - 1), 0)),
            pl.BlockSpec(dw_w.shape, lambda b, c: (0, 0)), _ANY],
        out_specs=[pl.BlockSpec((lc, ZC_W), lambda b, c: (b * nc + c, ZC_BLK)),
                   pl.BlockSpec((CONV_HALO, CONV_WIDTH), lambda b, c: (0, 0))],
        input_output_aliases={5: 0},
        scratch_shapes=[pltpu.VMEM((CONV_HALO + lc, CONV_WIDTH), F32), pltpu.VMEM((lc + CONV_HALO, CONV_WIDTH), F32)]
        + [pltpu.VMEM((SUBLANES - 1, lc + CONV_SHIFT_ROWS, CONV_WIDTH), F32)] * 2,
        compiler_params=_params(("arbitrary", "arbitrary")))(z, z, dv, dv, dw_w, dz)


def _row(v):
    return v.reshape(1, -1)


def _ssm_mats(b_re, b_im, c_re, c_im):
    eye = jnp.eye(SSM_GROUPS, dtype=bool)
    bm = jnp.stack([b_re, b_im]).transpose(1, 3, 0, 2)[:, :, :, None, :]
    bmat = jnp.where(eye[:, None, None, :, None], bm, 0.0).reshape(SSM_WIDTH, 2 * SSM_LANES)
    cm = jnp.stack([c_re, -c_im]).transpose(0, 1, 3, 2)[:, :, :, None, :]
    cmat = jnp.where(eye[None, :, None, :, None], cm, 0.0).reshape(2 * SSM_LANES, SSM_WIDTH)
    return bmat.astype(MXU_DTYPE), cmat.astype(MXU_DTYPE)


def _ssm_mats_t(dbmat, dcmat):
    eye = jnp.eye(SSM_GROUPS, dtype=bool)
    db = dbmat.reshape(SSM_GROUPS, SSM_GROUP, 2, SSM_GROUPS, SSM_STATE)
    db = jnp.sum(jnp.where(eye[:, None, None, :, None], db, 0.0), axis=3).transpose(2, 0, 3, 1)
    dc = dcmat.reshape(2, SSM_GROUPS, SSM_STATE, SSM_GROUPS, SSM_GROUP)
    dc = jnp.sum(jnp.where(eye[None, :, None, :, None], dc, 0.0), axis=3).transpose(0, 1, 3, 2)
    return db[0], db[1], dc[0], -dc[1]


def _layer_fwd(x, p, w, get_ffn_weights, sp, ctab, stab, n_seq):
    z, h = _in_proj(x, sp["mix_norm_g"], w["w_in"])
    ya, lse = _attention_fwd(z, ctab, stab, sp["attn_sinks"], n_seq)
    ys, states = _ssm_fwd(z, sp["bmat"], sp["a_row"], sp["f_row"], sp["cmat"], sp["ssm_d"], n_seq)
    v = _conv_fwd(z, w["conv_dw_w"], sp["conv_dw_b"], n_seq)
    merge_consts = [w["w_attn_out"], w["w_ssm_glu"], sp["b_ssm_glu"], sp["conv_norm_g"], sp["conv_norm_b"], w["w_conv_out"],
                    sp["b_gate"], w["w_mix_out"], sp["ffn_norm_g"]]
    (x1, hf), _ = _token_call("merge", lambda *a: (list(_merge_tile(*_f32s(a))), []), 512,
                              [_whole(x), _whole(ya), _whole(ys), _whole(v), (z, ZG_W, 0)], merge_consts,
                              [(x.shape[1], F32), (x.shape[1], MXU_DTYPE)], [])
    w = dict(w, **get_ffn_weights(x1))
    gate, up, act = _ffn_in_act(hf, w["w_ffn_in"])
    x2 = _matmul_add("mm_ffn_out", act, w["w_ffn_out"], x1)
    d = x.shape[1]
    (x3, ple_pre, ple_e), _ = _token_call(
        "ple", lambda *a: (list(_ple_tile(*a)), []), 512, [_whole(x2), _whole(p)],
        [w["w_ple_in"], sp["ple_norm_g"], w["w_ple_gate"]], [(d, F32), (d, MXU_DTYPE), (d, MXU_DTYPE)], [])
    saved = dict(h=h, z=z, ya=ya, lse=lse, ys=ys, states=states, v=v, x1=x1, hf=hf, gate=gate, up=up, act=act, x2=x2, p=p,
                 ple_pre=ple_pre, ple_e=ple_e, x=x)
    return x3, saved, w


def _layer_bwd(dx3, sv, w, sp, on_grads, ctab, stab, n_seq):
    d = dx3.shape[1]
    gw, gs = {}, {}

    def ple_bwd(*a):
        dx2, accs = _ple_bwd_tile(*_f32s(a))
        return [dx2, dx2], accs

    (dx2, dffn), (gw["w_ple_in"], gs["ple_norm_g"], gw["w_ple_gate"]) = _token_call(
        "ple_bwd", ple_bwd, 512, [_whole(sv["x2"]), _whole(sv["p"]), _whole(sv["ple_pre"]), _whole(sv["ple_e"]), _whole(dx3)],
        [w["w_ple_in"], sp["ple_norm_g"], w["w_ple_gate"]], [(d, F32), (d, MXU_DTYPE)],
        [w["w_ple_in"].shape, (1, d), w["w_ple_gate"].shape])

    dgate, dup = _ffn_mid_bwd(dffn, w["w_ffn_out"], sv["gate"], sv["up"])
    gw["w_ffn_out"] = _matmul_tn("mm_ffn_out_dw", sv["act"], dffn)
    dhf = _ffn_in_dx(dgate, dup, w["w_ffn_in"])
    gw["w_ffn_in"] = jnp.concatenate([_matmul_tn("mm_ffn_gate_dw", sv["hf"], dgate), _matmul_tn("mm_ffn_up_dw", sv["hf"], dup)],
                                     axis=1)

    token = on_grads("ffn", gw)

    def merge_bwd(*a):
        return _merge_bwd_tile(*_f32s(a))

    b_gate = sp["b_gate"] if token is None else sp["b_gate"] + token[0:1, 0:1]
    merge_consts = [w["w_attn_out"], w["w_ssm_glu"], sp["b_ssm_glu"], sp["conv_norm_g"], sp["conv_norm_b"], w["w_conv_out"],
                    b_gate, w["w_mix_out"], sp["ffn_norm_g"]]
    dz = lax.empty(sv["z"].shape, MXU_DTYPE)
    (dx_res, dya, dys, dv, dz), macc = _token_call(
        "merge_bwd", merge_bwd, 256,
        [_whole(sv["x1"]), _whole(sv["ya"]), _whole(sv["ys"]), _whole(sv["v"]), (sv["z"], ZG_W, 0), _whole(dx2), _whole(dhf)],
        merge_consts, [(d, F32), (Q_WIDTH, MXU_DTYPE), (SSM_WIDTH, F32), (CONV_WIDTH, F32)],
        [c.shape for c in merge_consts], into=(dz, ZG_W, 0))
    (gw["w_attn_out"], gw["w_ssm_glu"], gs["b_ssm_glu"], gs["conv_norm_g"], gs["conv_norm_b"], gw["w_conv_out"], gs["b_gate"],
     gw["w_mix_out"], gs["ffn_norm_g"]) = macc

    dz, dw_taps = _conv_bwd_taps(sv["z"], dv, w["conv_dw_w"], dz, n_seq)
    gw["conv_dw_w"], gs["conv_dw_b"] = dw_taps[:CONV_K], dw_taps[CONV_K:]

    dz, gs["bmat"], gs["cmat"], gs["a_row"], gs["f_row"], gs["ssm_d"] = _ssm_bwd(
        sv["z"], sv["states"], dys, sp["bmat"], sp["a_row"], sp["f_row"], sp["cmat"], sp["ssm_d"], dz, n_seq)

    dz, dkv, dkvp, gs["attn_sinks"] = _attention_bwd(sv["z"], ctab, stab, sp["attn_sinks"], sv["ya"], sv["lse"], dya, dz, n_seq)
    dz = _kv_combine(dkv, dkvp, dz, n_seq)
    gw["w_in"] = _matmul_tn("mm_in_dw", sv["h"], dz)
    token = on_grads("mix", gw)
    g_in = sp["mix_norm_g"] if token is None else sp["mix_norm_g"] + token[0:1, 0:1]
    dx, gs["mix_norm_g"] = _in_proj_bwd(dz, w["w_in"], sv["x"], dx_res, g_in)
    return dx, gw, gs


def _loss_and_grad(x, target, g):
    def fn(x, tgt, g):
        def f(x, g):
            err = _rms(x, g) - tgt
            return 0.5 * jnp.mean(err * err, axis=-1, keepdims=True)

        per_token, vjp = jax.vjp(f, x, g)
        dx, dg = vjp(jnp.ones_like(per_token))
        return [dx], [jnp.sum(per_token, axis=0, keepdims=True), dg]

    (dx,), (loss, dg) = _token_call("loss", fn, 512, [_whole(x), _whole(target)], [g], [(x.shape[1], F32)],
                                    [(8, LANES), (1, x.shape[1])])
    return loss[0, 0], dx, dg


def _mesh_place():
    return lax.axis_index("x"), lax.axis_index("y"), lax.axis_index("c")


def _flip(v, bit):
    return 1 - v if bit else v


_MESH = pl.DeviceIdType.MESH


def _all_gather(name, xs):
    n = len(xs)

    def body(*refs):
        x_refs, out_refs = refs[:n], refs[n:2 * n]
        send_sems, recv_sems, local_sems = refs[2 * n:]
        mx, my, mc = _mesh_place()
        me, sibling = (mx, my, mc), (mx, my, 1 - mc)
        chips = [(1 - mx, my), (mx, 1 - my), (1 - mx, 1 - my)]

        def slot(a, px, py, pc):
            return out_refs[a].at[4 * px + 2 * py + pc]

        def copy(a, k, block, to, src=None):
            return pltpu.make_async_remote_copy(
                src_ref=slot(a, *block) if src is None else src, dst_ref=slot(a, *block), send_sem=send_sems.at[7 * a + k],
                recv_sem=recv_sems.at[7 * a + k], device_id=to, device_id_type=_MESH)

        mine = [pltpu.make_async_copy(x_refs[a], slot(a, *me), local_sems.at[a]) for a in range(n)]
        for cp in mine:
            cp.start()
        first = [copy(a, 0, me, sibling, src=x_refs[a]) for a in range(n)]
        first += [copy(a, 1 + j, me, (*chip, mc), src=x_refs[a]) for j, chip in enumerate(chips) for a in range(n)]
        for cp in first:
            cp.start()
        passed = []
        for j, chip in enumerate(chips):
            for a in range(n):
                copy(a, 1 + j, (*chip, mc), me).wait_recv()
                passed.append(copy(a, 4 + j, (*chip, mc), sibling))
                passed[-1].start()
        for a in range(n):
            copy(a, 0, sibling, me).wait_recv()
            for j, chip in enumerate(chips):
                copy(a, 4 + j, (*chip, 1 - mc), me).wait_recv()
        for cp in first + passed:
            cp.wait_send()
        for cp in mine:
            cp.wait()

    return pl.pallas_call(
        body, name=name, out_shape=[jax.ShapeDtypeStruct((N_DEV,) + x.shape, x.dtype) for x in xs], in_specs=[_ANY] * n,
        out_specs=[_ANY] * n,
        scratch_shapes=[pltpu.SemaphoreType.DMA((7 * n,)), pltpu.SemaphoreType.DMA((7 * n,)), pltpu.SemaphoreType.DMA((n,))])(*xs)


def _direct_copies(kind, src_refs, land_refs, send_sems, recv_sems, local_sems):
    mx, my, mc = _mesh_place()
    me = 4 * mx + 2 * my + mc
    n = len(src_refs)
    own = [pltpu.make_async_copy(src_refs[a] if kind == "gather" else src_refs[a].at[me], land_refs[a].at[me], local_sems.at[a])
           for a in range(n)]
    copies = []
    for rel in range(1, N_DEV):
        px, py, pc = _flip(mx, rel & 4), _flip(my, rel & 2), _flip(mc, rel & 1)
        for a in range(n):
            src = src_refs[a] if kind == "gather" else src_refs[a].at[4 * px + 2 * py + pc]
            copies.append(pltpu.make_async_remote_copy(
                src_ref=src, dst_ref=land_refs[a].at[me], send_sem=send_sems.at[7 * a + rel - 1],
                recv_sem=recv_sems.at[7 * a + rel - 1], device_id=(px, py, pc), device_id_type=_MESH))
    return copies, own


_HBM = pl.BlockSpec(memory_space=pltpu.HBM)
_SEM = pl.BlockSpec(memory_space=pltpu.SEMAPHORE)
_DATAFLOW = pltpu.SideEffectType.DATAFLOW_SIDE_EFFECTING


def _exchange_start(name, kind, groups):
    sizes = [len(g) for g in groups]
    srcs = [s for g in groups for s in g]
    lands = [lax.empty(((N_DEV,) + s.shape) if kind == "gather" else s.shape, s.dtype) for s in srcs]
    n, n_g = len(srcs), len(groups)
    first = [sum(sizes[:g]) for g in range(n_g)]

    def body(*refs):
        src_refs, land_refs, sems = refs[:n], refs[n:2 * n], refs[2 * n:2 * n + 3 * n_g]
        for g in range(n_g):
            span = slice(first[g], first[g] + sizes[g])
            copies, own = _direct_copies(kind, src_refs[span], land_refs[span], *sems[3 * g:3 * g + 3])
            for cp in own + copies:
                cp.start()
        refs[-1][...] = jnp.zeros_like(refs[-1])

    hbm = lambda a: pltpu.with_memory_space_constraint(a, pltpu.HBM)
    sem_shapes = [pltpu.SemaphoreType.DMA((k * m,)) for m in sizes for k in (7, 7, 1)]
    out = pl.pallas_call(
        body, name=name,
        out_shape=sem_shapes + [pltpu.HBM(a.shape, a.dtype) for a in srcs + lands] + [jax.ShapeDtypeStruct((8, LANES), F32)],
        in_specs=[_HBM] * (2 * n), out_specs=[_SEM] * (3 * n_g) + [_HBM] * (2 * n) + [pl.BlockSpec(memory_space=pltpu.VMEM)],
        input_output_aliases={i: 3 * n_g + i for i in range(2 * n)},
        compiler_params=pltpu.CompilerParams(has_side_effects=_DATAFLOW))(*[hbm(a) for a in srcs + lands])
    sems, arrays = out[:3 * n_g], out[3 * n_g:-1]
    started = [(kind, (*sems[3 * g:3 * g + 3], *arrays[first[g]:first[g] + sizes[g]],
                       *arrays[n + first[g]:n + first[g] + sizes[g]])) for g in range(n_g)]
    return started, out[-1]


def _exchange_wait(name, started, after):
    kind, (send_sems, recv_sems, local_sems, *arrays) = started
    n = len(arrays) // 2

    def body(*refs):
        src_refs, land_refs = refs[:n], refs[n:2 * n]
        copies, own = _direct_copies(kind, src_refs, land_refs, *refs[2 * n:2 * n + 3])
        for cp in copies + own:
            cp.wait()

    out = pl.pallas_call(
        body, name=name, out_shape=[pltpu.HBM(a.shape, a.dtype) for a in arrays],
        in_specs=[_HBM] * (2 * n) + [_SEM] * 3 + [_ANY], out_specs=[_HBM] * (2 * n),
        input_output_aliases={i: i for i in range(2 * n)},
        compiler_params=pltpu.CompilerParams(has_side_effects=_DATAFLOW))(*arrays, send_sems, recv_sems, local_sems, after)
    return out[n:]


def _adamw_math(g, w, m, v):
    m2 = ADAM_B1 * m + (1.0 - ADAM_B1) * g
    v2 = ADAM_B2 * v + (1.0 - ADAM_B2) * jnp.square(g)
    m_hat = m2 / (1.0 - ADAM_B1 ** ADAM_STEP)
    v_hat = v2 / (1.0 - ADAM_B2 ** ADAM_STEP)
    return g, -ADAM_LR * (m_hat / (jnp.sqrt(v_hat) + ADAM_EPS) + ADAM_WD * w), m2, v2


def _sum_blocks(ref):
    g = ref[0].astype(F32)
    for j in range(1, N_DEV):
        g = g + ref[j].astype(F32)
    return g


def _adamw_flat(name, parts, w, m, v):
    r = w.shape[0]
    tile = _pick(r, (1024, 512, 256, 128, 8))

    def body(p_ref, w_ref, m_ref, v_ref, *o_refs):
        for o, val in zip(o_refs, _adamw_math(_sum_blocks(p_ref), w_ref[...], m_ref[...], v_ref[...]), strict=True):
            o[...] = val

    flat = pl.BlockSpec((tile, LANES), lambda i: (i, 0))
    return pl.pallas_call(
        body, name=name, grid=(r // tile,), out_shape=[jax.ShapeDtypeStruct((r, LANES), F32)] * 4,
        in_specs=[pl.BlockSpec((N_DEV, tile, LANES), lambda i: (0, i, 0)), flat, flat, flat], out_specs=[flat] * 4,
        compiler_params=_params(("parallel",)))(parts, w, m, v)


def _adamw_cols(name, landed, base, stride, w, m, v):
    depth, rows, cs = w.shape
    n_slab = -(-cs // LANES)
    tr = _pick(rows, (SLAB_TILE,))

    def body(*refs):
        slabs, (w_ref, m_ref, v_ref), o_refs = refs[:n_slab], refs[n_slab:n_slab + 3], refs[n_slab + 3:]
        g = jnp.concatenate([_sum_blocks(s)[:, :min(LANES, cs - LANES * k)] for k, s in enumerate(slabs)], axis=1)
        for o, val in zip(o_refs, _adamw_math(g, w_ref[...], m_ref[...], v_ref[...]), strict=True):
            o[...] = val

    slab = lambda k: pl.BlockSpec((N_DEV, tr, LANES), lambda l, i: (0, (l * stride + base + k * rows) // tr + i, 0))
    nat = pl.BlockSpec((None, tr, cs), lambda l, i: (l, i, 0))
    return pl.pallas_call(
        body, name=name, grid=(depth, rows // tr), out_shape=[jax.ShapeDtypeStruct(w.shape, F32)] * 4,
        in_specs=[slab(k) for k in range(n_slab)] + [nat] * 3, out_specs=[nat] * 4,
        compiler_params=_params(("parallel", "parallel")))(*[landed] * n_slab, w, m, v)


def _adamw_rows(name, landed, base, stride, w, m, v):
    depth, rs, width = w.shape
    tr = math.gcd(rs, base, stride)

    def body(p_ref, w_ref, m_ref, v_ref, *o_refs):
        for o, val in zip(o_refs, _adamw_math(_sum_blocks(p_ref), w_ref[...], m_ref[...], v_ref[...]), strict=True):
            o[...] = val

    nat = pl.BlockSpec((None, tr, width), lambda l, i: (l, i, 0))
    return pl.pallas_call(
        body, name=name, grid=(depth, rs // tr), out_shape=[jax.ShapeDtypeStruct(w.shape, F32)] * 4,
        in_specs=[pl.BlockSpec((N_DEV, tr, width), lambda l, i: (0, (l * stride + base) // tr + i, 0)), nat, nat, nat],
        out_specs=[nat] * 4, compiler_params=_params(("parallel", "parallel")))(landed, w, m, v)


def _adamw_conv(landed, base, stride, w, m, v):
    depth, taps, cs = w.shape

    def body(p_ref, w_ref, m_ref, v_ref, *o_refs):
        g = _sum_blocks(p_ref)[:taps, :cs]
        for o, val in zip(o_refs, _adamw_math(g, w_ref[...], m_ref[...], v_ref[...]), strict=True):
            o[...] = val

    nat = pl.BlockSpec((None, taps, cs), lambda l: (l, 0, 0))
    return pl.pallas_call(
        body, name="adamw_conv", grid=(depth,), out_shape=[jax.ShapeDtypeStruct(w.shape, F32)] * 4,
        in_specs=[pl.BlockSpec((N_DEV, CONV_HALO, LANES), lambda l: (0, (l * stride + base) // CONV_HALO, 0)), nat, nat, nat],
        out_specs=[nat] * 4, compiler_params=_params(("parallel",)))(landed, w, m, v)


def _unshard_cols(name, gathered, start, rows, cs, shift=0):
    n_slab = -(-cs // LANES)
    total = N_DEV * cs
    tr = _pick(rows, (SLAB_TILE,))

    def body(*refs):
        slabs, o_ref = refs[:n_slab], refs[n_slab]
        for j in range(N_DEV):
            for k, s in enumerate(slabs):
                for src, dst, width in _wrapped(j * cs + LANES * k - shift, min(LANES, cs - LANES * k), total):
                    o_ref[:, dst:dst + width] = s[j, :, src:src + width]

    slab = lambda k: pl.BlockSpec((N_DEV, tr, LANES), lambda i: (0, (start + k * rows) // tr + i, 0))
    return pl.pallas_call(
        body, name=name, grid=(rows // tr,), out_shape=jax.ShapeDtypeStruct((rows, total), gathered.dtype),
        in_specs=[slab(k) for k in range(n_slab)], out_specs=pl.BlockSpec((tr, total), lambda i: (i, 0)),
        compiler_params=_params(("parallel",)))(*[gathered] * n_slab)


def _shard_cols(name, full, cs, shift=0):
    rows, total = full.shape
    n_slab = -(-cs // LANES)
    tr = _pick(rows, (SLAB_TILE,))

    def body(f_ref, o_ref):
        for j in range(N_DEV):
            for k in range(n_slab):
                used = min(LANES, cs - LANES * k)
                for src, dst, width in _wrapped(j * cs + LANES * k - shift, used, total):
                    o_ref[j, k, :, src:src + width] = f_ref[:, dst:dst + width].astype(o_ref.dtype)
                if used < LANES:
                    o_ref[j, k, :, used:] = jnp.zeros((tr, LANES - used), o_ref.dtype)

    out = pl.pallas_call(
        body, name=name, grid=(rows // tr,), out_shape=jax.ShapeDtypeStruct((N_DEV, n_slab, rows, LANES), BF16),
        in_specs=[pl.BlockSpec((tr, total), lambda i: (i, 0))],
        out_specs=pl.BlockSpec((N_DEV, n_slab, tr, LANES), lambda i: (0, 0, i, 0)),
        compiler_params=_params(("parallel",)))(full)
    return out.reshape(N_DEV, n_slab * rows, LANES)


def _wrapped(pos, width, total):
    pos %= total
    if pos + width <= total:
        return [(0, pos, width)]
    head = total - pos
    return [(0, pos, head), (head, 0, width - head)]


CONV_W_PIECES = 3


def _pad_to(n, align):
    return -(-n // align) * align


def _layout(group):
    col_names, row_names, with_conv = GROUPS[group]
    dims = {name: (rows, cols) for name, rows, cols, _ in SHARDED}
    col, off = {}, 0
    for name in col_names:
        rows, cols = dims[name]
        cs = cols // N_DEV
        col[name] = (off, rows, cs)
        off += -(-cs // LANES) * rows
    conv_base = off
    col_rows = _pad_to(off + with_conv * CONV_W_PIECES * CONV_HALO, SLAB_TILE)
    return col, conv_base, col_rows, {name: dims[name][0] // N_DEV for name in row_names}


def _slabs(shard, fill):
    rows, cs = shard.shape
    parts = []
    for k in range(-(-cs // LANES)):
        part = shard[:, LANES * k:min(LANES * (k + 1), cs)]
        parts.append(jnp.pad(part, ((0, 0), (0, LANES - part.shape[1])), constant_values=fill))
    return jnp.concatenate(parts, axis=0)


def _concat_padded(pieces, total, axis):
    used = sum(p.shape[axis] for p in pieces)
    if total > used:
        shape = list(pieces[0].shape)
        shape[axis] = total - used
        pieces = pieces + [jnp.zeros(shape, pieces[0].dtype)]
    return jnp.concatenate(pieces, axis=axis)


def _split3(a):
    hi = a.astype(BF16)
    r1 = a - hi.astype(F32)
    mid = r1.astype(BF16)
    return hi, mid, (r1 - mid.astype(F32)).astype(BF16)


def _pack_small(arrs, lead=()):
    flat = jnp.concatenate([a.reshape(lead + (-1,)) for a in arrs], axis=-1)
    total = _pad_to(flat.shape[-1], 512 * LANES)
    flat = jnp.pad(flat, [(0, 0)] * len(lead) + [(0, total - flat.shape[-1])])
    return flat.reshape(lead + (total // LANES, LANES))


def _unpack_small(flat, shapes):
    flat = flat.reshape(-1)
    res, off = [], 0
    for s in shapes:
        n = int(np.prod(s))
        res.append(flat[off:off + n].reshape(s))
        off += n
    return res


def _small_rows(a, depth):
    n16 = depth * SSM_GROUPS
    a_re, a_im, f_re, f_im = _ssm_coeffs(a["ssm_lambda_re"].reshape(n16, SSM_STATE), a["ssm_lambda_im"].reshape(n16, SSM_STATE),
                                         a["ssm_log_dt"].reshape(n16, 1))
    rows = []
    for l in range(depth):
        sp = {k: _row(a[k][l]) for k in ("mix_norm_g", "b_gate", "attn_sinks", "ssm_d", "b_ssm_glu", "conv_dw_b",
                                         "conv_norm_g", "conv_norm_b", "ffn_norm_g", "ple_norm_g")}
        g = slice(l * SSM_GROUPS, (l + 1) * SSM_GROUPS)
        sp["a_row"] = jnp.concatenate([a_re[g].reshape(1, -1), a_im[g].reshape(1, -1)], axis=1)
        sp["f_row"] = jnp.concatenate([f_re[g].reshape(1, -1), f_im[g].reshape(1, -1)], axis=1)
        sp["bmat"], sp["cmat"] = _ssm_mats(a["ssm_b_re"][l], a["ssm_b_im"][l], a["ssm_c_re"][l], a["ssm_c_im"][l])
        rows.append(sp)
    return rows


def _local_step(a, get_weights, on_grads, depth):
    n_seq, seq, d = a["x"].shape
    t = n_seq * seq
    inv = ROPE_THETA ** (-jnp.arange(0, ROPE_DIM, 2, dtype=F32) / ROPE_DIM)
    lane = np.arange(LANES) % HEAD_DIM
    inv_lane = jnp.where(lane < ROPE_DIM, jnp.tile(inv, LANES // (ROPE_DIM // 2)), 0.0).reshape(1, LANES)
    ctab, stab = _rope_tables(a["positions"].reshape(t), inv_lane)
    small = _small_rows(a, depth)

    x = a["x"].reshape(t, d)
    saved, weights = [], []
    for l in range(depth):
        x, sv, w = _layer_fwd(x, a["p"][l].reshape(t, -1), get_weights(l, "mix", x),
                              functools.partial(get_weights, l, "ffn"), small[l], ctab, stab, n_seq)
        saved.append(sv)
        weights.append(w)
    loss, dx, d_final = _loss_and_grad(x, a["loss_target"].reshape(t, d), _row(a["final_norm_g"]))
    gws, gss = [None] * depth, [None] * depth
    for l in reversed(range(depth)):
        dx, gws[l], gss[l] = _layer_bwd(dx, saved[l], weights[l], small[l], functools.partial(on_grads, l), ctab, stab, n_seq)

    n16 = depth * SSM_GROUPS
    halves = lambda k, h: jnp.concatenate([gss[l][k][:, h * SSM_LANES:(h + 1) * SSM_LANES].reshape(SSM_GROUPS, SSM_STATE)
                                           for l in range(depth)], axis=0)
    dlr, dli, ddt = _ssm_coeffs_bwd(a["ssm_lambda_re"].reshape(n16, SSM_STATE), a["ssm_lambda_im"].reshape(n16, SSM_STATE),
                                    a["ssm_log_dt"].reshape(n16, 1),
                                    (halves("a_row", 0), halves("a_row", 1), halves("f_row", 0), halves("f_row", 1)))
    bc = [_ssm_mats_t(gss[l]["bmat"], gss[l]["cmat"]) for l in range(depth)]
    gsmall = {k: jnp.stack([gss[l][k].reshape(a[k].shape[1:]) for l in range(depth)])
              for k in ("mix_norm_g", "b_gate", "attn_sinks", "ssm_d", "b_ssm_glu", "conv_dw_b", "conv_norm_g", "conv_norm_b",
                        "ffn_norm_g", "ple_norm_g")}
    gsmall["ssm_lambda_re"] = dlr.reshape(a["ssm_lambda_re"].shape)
    gsmall["ssm_lambda_im"] = dli.reshape(a["ssm_lambda_im"].shape)
    gsmall["ssm_log_dt"] = ddt.reshape(a["ssm_log_dt"].shape)
    for i, k in enumerate(("ssm_b_re", "ssm_b_im", "ssm_c_re", "ssm_c_im")):
        gsmall[k] = jnp.stack([bc[l][i] for l in range(depth)])
    gsmall["final_norm_g"] = d_final.reshape(a["final_norm_g"].shape)
    return loss, dx.reshape(n_seq, seq, d), gws, gsmall


def kernel(x, p, positions, mix_norm_g, w_in, b_gate, attn_sinks, w_attn_out, ssm_lambda_re, ssm_lambda_im, ssm_log_dt, ssm_b_re, ssm_b_im, ssm_c_re, ssm_c_im, ssm_d, w_ssm_glu, b_ssm_glu, conv_dw_w, conv_dw_b, conv_norm_g, conv_norm_b, w_conv_out, w_mix_out, ffn_norm_g, w_ffn_in, w_ffn_out, w_ple_in, ple_norm_g, w_ple_gate, final_norm_g, loss_target, m_mix_norm_g, m_w_in, m_b_gate, m_attn_sinks, m_w_attn_out, m_ssm_lambda_re, m_ssm_lambda_im, m_ssm_log_dt, m_ssm_b_re, m_ssm_b_im, m_ssm_c_re, m_ssm_c_im, m_ssm_d, m_w_ssm_glu, m_b_ssm_glu, m_conv_dw_w, m_conv_dw_b, m_conv_norm_g, m_conv_norm_b, m_w_conv_out, m_w_mix_out, m_ffn_norm_g, m_w_ffn_in, m_w_ffn_out, m_w_ple_in, m_ple_norm_g, m_w_ple_gate, m_final_norm_g, v_mix_norm_g, v_w_in, v_b_gate, v_attn_sinks, v_w_attn_out, v_ssm_lambda_re, v_ssm_lambda_im, v_ssm_log_dt, v_ssm_b_re, v_ssm_b_im, v_ssm_c_re, v_ssm_c_im, v_ssm_d, v_w_ssm_glu, v_b_ssm_glu, v_conv_dw_w, v_conv_dw_b, v_conv_norm_g, v_conv_norm_b, v_w_conv_out, v_w_mix_out, v_ffn_norm_g, v_w_ffn_in, v_w_ffn_out, v_w_ple_in, v_ple_norm_g, v_w_ple_gate, v_final_norm_g):
    a = dict(locals())
    depth = w_in.shape[0]
    layouts = {group: _layout(group) for group in GROUPS}
    shift = {"w_in": Z_SPLIT}
    conv_pad = ((0, 0), (0, CONV_HALO - CONV_K), (0, LANES - CONV_WIDTH // N_DEV))

    def packed_weights(l, group, fill):
        col, _, col_rows, row = layouts[group]
        pieces = [_slabs(a[name][l].astype(BF16), fill) for name in col]
        if GROUPS[group][2]:
            pieces.append(jnp.pad(jnp.stack(_split3(a["conv_dw_w"][l])), conv_pad).reshape(-1, LANES))
        return [_concat_padded(pieces, col_rows, 0)] + [a[name][l].astype(BF16) for name in row]

    gathers, tokens, fill = [], [], jnp.zeros((), BF16)
    for l in range(depth):
        started, token = _exchange_start(f"gather_start_{l}", "gather", [packed_weights(l, group, fill) for group in GROUPS])
        gathers.append(dict(zip(GROUPS, started, strict=True)))
        tokens.append(token[0:1, 0:1])
        fill = token[0, 0].astype(BF16)

    def get_weights(l, group, after):
        col, conv_base, _, row = layouts[group]
        slab8, *rows8 = _exchange_wait(f"gather_wait_{group}_{l}", gathers[l][group], after)
        w = {name: _unshard_cols("unshard_" + name, slab8, base, rows, cs, shift.get(name, 0))
             for name, (base, rows, cs) in col.items()}
        for (name, rs), gathered in zip(row.items(), rows8, strict=True):
            w[name] = gathered.reshape(N_DEV * rs, -1)
        if GROUPS[group][2]:
            conv = slab8[:, conv_base:conv_base + CONV_W_PIECES * CONV_HALO]
            conv = conv.reshape(N_DEV, CONV_W_PIECES, CONV_HALO, LANES)[:, :, :CONV_K, :CONV_WIDTH // N_DEV].astype(F32)
            w["conv_dw_w"] = jnp.sum(conv, axis=1).transpose(1, 0, 2).reshape(CONV_K, CONV_WIDTH)
        return w

    scatters = {}

    def on_grads(l, group, gw):
        col, _, col_rows, row = layouts[group]
        pieces = [_shard_cols("shard_" + name, gw[name], cs, shift.get(name, 0)) for name, (_, _, cs) in col.items()]
        if GROUPS[group][2]:
            conv = gw["conv_dw_w"].reshape(CONV_K, N_DEV, CONV_WIDTH // N_DEV).transpose(1, 0, 2).astype(BF16)
            pieces.append(jnp.pad(jnp.pad(conv, conv_pad), ((0, 0), (0, (CONV_W_PIECES - 1) * CONV_HALO), (0, 0))))
        by_shard = [gw[name].astype(BF16).reshape(N_DEV, rs, -1) for name, rs in row.items()]
        (scatters[l, group],), token = _exchange_start(
            f"grads_start_{group}_{l}", "scatter", [[_concat_padded(pieces, col_rows, 1)] + by_shard])
        return token

    local = dict(a, mix_norm_g=a["mix_norm_g"] + sum(tokens))
    loss, grad_x, _, gsmall = _local_step(local, get_weights, on_grads, depth)
    loss = lax.psum(loss, ("x", "y", "c"))

    shapes = [a[k].shape for k in REPLICATED]
    parts, = _all_gather("gather_small_grads", [_pack_small([gsmall[k] for k in REPLICATED])])
    small_state = [_pack_small([a[pre + k] for k in REPLICATED]) for pre in ("", "m_", "v_")]
    small_flat = _adamw_flat("adamw_replicated", parts, *small_state)
    small = [dict(zip(REPLICATED, _unpack_small(o, shapes), strict=True)) for o in small_flat]

    state = lambda name: (a[name], a["m_" + name], a["v_" + name])
    big, after = {}, small_flat[1]
    for group in ("ffn", "mix"):
        col, conv_base, col_rows, row = layouts[group]
        landed = [_exchange_wait(f"grads_wait_{group}_{l}", scatters[l, group], after) for l in range(depth)]
        landed_slab = jnp.concatenate([arrays[0] for arrays in landed], axis=1)
        big.update({name: _adamw_cols("adamw_" + name, landed_slab, base, col_rows, *state(name)) for name, (base, _, _) in col.items()})
        for i, (name, rs) in enumerate(row.items()):
            landed_rows = jnp.concatenate([arrays[1 + i] for arrays in landed], axis=1)
            big[name] = _adamw_rows("adamw_" + name, landed_rows, 0, rs, *state(name))
        if GROUPS[group][2]:
            big["conv_dw_w"] = _adamw_conv(landed_slab, conv_base, col_rows, *state("conv_dw_w"))
        after = big[next(iter(col))][1]

    def result(kind, name):
        if name in REPLICATED:
            return small[kind][name]
        return big[name][kind]

    return (loss, grad_x, *[result(kind, n) for kind in range(4) for n in WEIGHT_ORDER])
```

```python
import functools
import math

import numpy as np
import jax
import jax.numpy as jnp
from jax import lax
from jax.experimental import pallas as pl
from jax.experimental.pallas import tpu as pltpu

F32 = jnp.float32
BF16 = jnp.bfloat16
MXU_DTYPE = jnp.bfloat16
VMEM_LIMIT_BYTES = 56 * 2 ** 20
N_DEV = 8
LANES = 128

HEAD_DIM = 64
N_Q_HEADS = 8
N_KV_HEADS = 2
GQA_GROUP = 4
BLOCK = 128
ROPE_THETA = 500000.0
ROPE_DIM = 16
Q_WIDTH = 512
KV_WIDTH = 128
SSM_WIDTH = 256
SSM_GROUP = 16
SSM_GROUPS = 16
SSM_STATE = 64
SSM_LANES = SSM_GROUPS * SSM_STATE
CONV_WIDTH = 256
CONV_K = 31
CONV_HALO = 32
EPS = 1e-6
NEG_INF = -1e30
ADAM_LR, ADAM_B1, ADAM_B2, ADAM_EPS, ADAM_WD, ADAM_STEP = 0.001, 0.9, 0.999, 1e-08, 0.01, 10

ZG_W, ZQ_W, ZKV_W, ZS_W, ZC_W = 3072, 512, 256, 256, 512
ZQ_BLK, ZKV_BLK, ZS_BLK, ZC_BLK = 3072 // 512, 3584 // 256, 3840 // 256, 4096 // 512
Z_WIDTH = 4608
Z_SPLIT = 1536

SHARDED = (("w_in", 1024, 4608, 1), ("w_attn_out", 512, 1024, 1), ("w_ssm_glu", 256, 2048, 1),
           ("conv_dw_w", 31, 256, 1), ("w_conv_out", 256, 1024, 1), ("w_mix_out", 1024, 1024, 0),
           ("w_ffn_in", 1024, 5632, 1), ("w_ffn_out", 2816, 1024, 0), ("w_ple_in", 256, 1024, 1),
           ("w_ple_gate", 1024, 1024, 0))
GROUPS = {"mix": (("w_in", "w_attn_out", "w_ssm_glu", "w_conv_out"), ("w_mix_out",), True),
          "ffn": (("w_ffn_in", "w_ple_in"), ("w_ffn_out", "w_ple_gate"), False)}
SLAB_TILE = 256
FLAT_ROW_ALIGN = 1024
REPLICATED = ("mix_norm_g", "b_gate", "attn_sinks", "ssm_lambda_re", "ssm_lambda_im", "ssm_log_dt", "ssm_b_re",
              "ssm_b_im", "ssm_c_re", "ssm_c_im", "ssm_d", "b_ssm_glu", "conv_dw_b", "conv_norm_g", "conv_norm_b",
              "ffn_norm_g", "ple_norm_g", "final_norm_g")
WEIGHT_ORDER = ("mix_norm_g", "w_in", "b_gate", "attn_sinks", "w_attn_out", "ssm_lambda_re", "ssm_lambda_im",
                "ssm_log_dt", "ssm_b_re", "ssm_b_im", "ssm_c_re", "ssm_c_im", "ssm_d", "w_ssm_glu", "b_ssm_glu",
                "conv_dw_w", "conv_dw_b", "conv_norm_g", "conv_norm_b", "w_conv_out", "w_mix_out", "ffn_norm_g",
                "w_ffn_in", "w_ffn_out", "w_ple_in", "ple_norm_g", "w_ple_gate", "final_norm_g")


_ANY = pl.BlockSpec(memory_space=pl.ANY)


def _params(sem=None):
    return pltpu.CompilerParams(dimension_semantics=sem, vmem_limit_bytes=VMEM_LIMIT_BYTES)


def _pick(n, cands):
    for c in cands:
        if n % c == 0:
            return c
    return n


def _dot(a, b, dims):
    return lax.dot_general(a.astype(MXU_DTYPE), b.astype(MXU_DTYPE), (dims, ((), ())), preferred_element_type=F32)


def _dot_nn(a, b):
    return _dot(a, b, ((1,), (0,)))


def _dot_nt(a, b):
    return _dot(a, b, ((1,), (1,)))


def _dot_tn(a, b):
    return _dot(a, b, ((0,), (0,)))


@jax.custom_vjp
def _mm(x, w):
    return _dot_nn(x, w)


def _mm_f(x, w):
    return _dot_nn(x, w), (x, w)


def _mm_b(res, dy):
    x, w = res
    return _dot_nt(dy, w).astype(x.dtype), _dot_tn(x, dy).astype(w.dtype)


_mm.defvjp(_mm_f, _mm_b)


def _rms(x, g):
    return x * lax.rsqrt(jnp.mean(x * x, axis=-1, keepdims=True) + EPS) * g


ROW_TILES = (1024, 512, 256, 128)
COL_TILES = (1536, 1408, 1024, 512, 256, 128)


def _matmul_add(name, a, b, residual):
    t, k = a.shape
    n = b.shape[1]
    tm, tn = _pick(t, ROW_TILES), _pick(n, COL_TILES)

    def body(a_ref, b_ref, r_ref, o_ref):
        o_ref[...] = r_ref[...] + _dot_nn(a_ref[...], b_ref[...])

    tile = pl.BlockSpec((tm, tn), lambda i, j: (i, j))
    return pl.pallas_call(
        body, name=name, grid=(t // tm, n // tn), out_shape=jax.ShapeDtypeStruct((t, n), F32),
        in_specs=[pl.BlockSpec((tm, k), lambda i, j: (i, 0)), pl.BlockSpec((k, tn), lambda i, j: (0, j)), tile],
        out_specs=tile, compiler_params=_params(("parallel", "parallel")))(a, b, residual)


def _matmul_tn(name, a, b):
    t, m = a.shape
    n = b.shape[1]
    tm, tn, tt = _pick(m, COL_TILES[1:]), _pick(n, COL_TILES), _pick(t, ROW_TILES)

    def body(a_ref, b_ref, o_ref):
        @pl.when(pl.program_id(2) == 0)
        def _():
            o_ref[...] = jnp.zeros_like(o_ref)

        o_ref[...] += _dot_tn(a_ref[...], b_ref[...])

    return pl.pallas_call(
        body, name=name, grid=(m // tm, n // tn, t // tt), out_shape=jax.ShapeDtypeStruct((m, n), F32),
        in_specs=[pl.BlockSpec((tt, tm), lambda i, j, s: (s, i)), pl.BlockSpec((tt, tn), lambda i, j, s: (s, j))],
        out_specs=pl.BlockSpec((tm, tn), lambda i, j, s: (i, j)),
        compiler_params=_params(("parallel", "parallel", "arbitrary")))(a, b)


def _two_parts(n):
    cut = n // (2 * LANES) * LANES
    return [slice(0, n)] if cut == 0 else [slice(0, cut), slice(cut, n)]


def _in_proj(x, g, w):
    t, d = x.shape
    n = w.shape[1]
    tm, tn = _pick(t, ROW_TILES), _pick(n, COL_TILES)

    def body(x_ref, g_ref, w_ref, z_ref, h_ref):
        @pl.when(pl.program_id(1) == 0)
        def _():
            h_ref[...] = _rms(x_ref[...], g_ref[...]).astype(h_ref.dtype)

        z_ref[...] = _dot_nn(h_ref[...], w_ref[...])

    return pl.pallas_call(
        body, name="in_proj", grid=(t // tm, n // tn),
        out_shape=[jax.ShapeDtypeStruct((t, n), F32), jax.ShapeDtypeStruct((t, d), MXU_DTYPE)],
        in_specs=[pl.BlockSpec((tm, d), lambda i, j: (i, 0)), pl.BlockSpec((1, d), lambda i, j: (0, 0)),
                  pl.BlockSpec((d, tn), lambda i, j: (0, j))],
        out_specs=[pl.BlockSpec((tm, tn), lambda i, j: (i, j)), pl.BlockSpec((tm, d), lambda i, j: (i, 0))],
        compiler_params=_params(("parallel", "arbitrary")))(x, g, w)


def _in_proj_bwd(dz, w, x, dx_res, g):
    t, d = x.shape
    tm = _pick(t, ROW_TILES[1:])

    def body(dz_ref, w_ref, x_ref, r_ref, g_ref, dx_ref, dg_ref):
        _, vjp = jax.vjp(_norm_in_tile, x_ref[...], g_ref[...])
        dx, dg = vjp(_dot_nt(dz_ref[...], w_ref[...]))
        dx_ref[...] = dx + r_ref[...]

        @pl.when(pl.program_id(0) == 0)
        def _():
            dg_ref[...] = jnp.zeros_like(dg_ref)

        dg_ref[...] += dg

    rows = lambda width: pl.BlockSpec((tm, width), lambda i: (i, 0))
    whole = lambda a: pl.BlockSpec(a.shape, lambda i: (0, 0))
    return pl.pallas_call(
        body, name="in_proj_bwd", grid=(t // tm,), out_shape=[jax.ShapeDtypeStruct((t, d), F32), jax.ShapeDtypeStruct((1, d), F32)],
        in_specs=[rows(dz.shape[1]), whole(w), rows(d), rows(d), whole(g)], out_specs=[rows(d), whole(g)],
        compiler_params=_params(("arbitrary",)))(dz, w, x, dx_res, g)


def _ffn_in_act(hf, w_fi):
    t, k = hf.shape
    f = w_fi.shape[1] // 2
    tm, tf = _pick(t, ROW_TILES[1:]), _pick(f, COL_TILES)
    nf = f // tf

    def body(a_ref, wg_ref, wu_ref, g_ref, u_ref, act_ref):
        a = a_ref[...]
        for cols in _two_parts(tf):
            g, u = _dot_nn(a, wg_ref[:, cols]), _dot_nn(a, wu_ref[:, cols])
            g_ref[:, cols] = g.astype(g_ref.dtype)
            u_ref[:, cols] = u.astype(u_ref.dtype)
            act_ref[:, cols] = (jax.nn.silu(g) * u).astype(act_ref.dtype)

    out = pl.BlockSpec((tm, tf), lambda i, j: (i, j))
    return pl.pallas_call(
        body, name="ffn_in_act", grid=(t // tm, nf), out_shape=[jax.ShapeDtypeStruct((t, f), MXU_DTYPE)] * 3,
        in_specs=[pl.BlockSpec((tm, k), lambda i, j: (i, 0)), pl.BlockSpec((k, tf), lambda i, j: (0, j)),
                  pl.BlockSpec((k, tf), lambda i, j: (0, j + nf))],
        out_specs=[out, out, out], compiler_params=_params(("parallel", "parallel")))(hf, w_fi, w_fi)


def _ffn_mid_bwd(dffn, w_fo, gate, up):
    t, d = dffn.shape
    f = w_fo.shape[0]
    tm, tf = _pick(t, ROW_TILES[1:]), _pick(f, COL_TILES)

    def body(a_ref, w_ref, g_ref, u_ref, dg_ref, du_ref):
        a = a_ref[...]
        for cols in _two_parts(tf):
            dact = _dot_nt(a, w_ref[cols, :])
            g, u = g_ref[:, cols].astype(F32), u_ref[:, cols].astype(F32)
            sg = jax.nn.sigmoid(g)
            dg_ref[:, cols] = (dact * u * sg * (1.0 + g * (1.0 - sg))).astype(dg_ref.dtype)
            du_ref[:, cols] = (dact * g * sg).astype(du_ref.dtype)

    blk = pl.BlockSpec((tm, tf), lambda i, j: (i, j))
    return pl.pallas_call(
        body, name="ffn_mid_bwd", grid=(t // tm, f // tf), out_shape=[jax.ShapeDtypeStruct((t, f), MXU_DTYPE)] * 2,
        in_specs=[pl.BlockSpec((tm, d), lambda i, j: (i, 0)), pl.BlockSpec((tf, d), lambda i, j: (j, 0)), blk, blk],
        out_specs=[blk, blk], compiler_params=_params(("parallel", "parallel")))(dffn, w_fo, gate, up)


def _ffn_in_dx(dgate, dup, w_fi):
    t, f = dgate.shape
    d = w_fi.shape[0]
    tm = _pick(t, ROW_TILES[1:])

    def body(g_ref, u_ref, w_ref, o_ref):
        o_ref[...] = _dot_nt(g_ref[...], w_ref[:, :f]) + _dot_nt(u_ref[...], w_ref[:, f:])

    blk = pl.BlockSpec((tm, f), lambda i: (i, 0))
    return pl.pallas_call(
        body, name="ffn_in_dx", grid=(t // tm,), out_shape=jax.ShapeDtypeStruct((t, d), F32),
        in_specs=[blk, blk, pl.BlockSpec(w_fi.shape, lambda i: (0, 0))], out_specs=pl.BlockSpec((tm, d), lambda i: (i, 0)),
        compiler_params=_params(("parallel",)))(dgate, dup, w_fi)


def _token_call(name, fn, tile, tok_ins, consts, tok_outs, acc_outs, into=None):
    n_rows = tok_ins[0][0].shape[0]
    tile = min(tile, n_rows)
    n_ti, n_c = len(tok_ins), len(consts)
    n_in = n_ti + n_c + (into is not None)
    n_to = len(tok_outs) + (into is not None)

    def body(*refs):
        ins = [r[...] for r in refs[:n_ti + n_c]]
        outs, accs = fn(*ins)
        for r, v in zip(refs[n_in:n_in + n_to], outs, strict=True):
            r[...] = v.astype(r.dtype)
        first = pl.program_id(0) == 0
        for r, v in zip(refs[n_in + n_to:], accs, strict=True):
            @pl.when(first)
            def _(r=r):
                r[...] = jnp.zeros_like(r)

            r[...] += jnp.broadcast_to(v, r.shape).astype(F32)

    in_specs = [pl.BlockSpec((tile, w), functools.partial(lambda i, c, r: (i + r, c), c=cb, r=(first[0] // tile if first else 0)))
                for _, w, cb, *first in tok_ins]
    in_specs += [pl.BlockSpec(c.shape, lambda i: (0, 0)) for c in consts]
    out_shape = [jax.ShapeDtypeStruct((n_rows, w), dt) for w, dt in tok_outs]
    out_specs = [pl.BlockSpec((tile, w), lambda i: (i, 0)) for w, _ in tok_outs]
    operands = [spec[0] for spec in tok_ins] + list(consts)
    aliases = {}
    if into is not None:
        target, width, col_block = into
        in_specs.append(_ANY)
        operands.append(target)
        out_shape.append(jax.ShapeDtypeStruct(target.shape, target.dtype))
        out_specs.append(pl.BlockSpec((tile, width), lambda i: (i, col_block)))
        aliases = {n_in - 1: n_to - 1}
    out_shape += [jax.ShapeDtypeStruct(s, F32) for s in acc_outs]
    out_specs += [pl.BlockSpec(s, lambda i: (0, 0)) for s in acc_outs]
    res = pl.pallas_call(
        body, name=name, grid=(n_rows // tile,), out_shape=out_shape, in_specs=in_specs, out_specs=out_specs,
        input_output_aliases=aliases, compiler_params=_params(("arbitrary",)))(*operands)
    return res[:n_to], res[n_to:]


def _whole(a):
    return (a, a.shape[1], 0)


def _norm_in_tile(x, g):
    return _rms(x, g)


def _conv_post_tile(v, g, b):
    mu = jnp.mean(v, axis=-1, keepdims=True)
    var = jnp.mean(jnp.square(v - mu), axis=-1, keepdims=True)
    return jax.nn.silu((v - mu) * lax.rsqrt(var + EPS) * g + b)


def _branches_tile(ya, ys, v, gin, w_ao, w_sg, b_sg, ln_g, ln_b, w_co, b_gate):
    d = w_ao.shape[1]
    y_attn = _mm(ya, w_ao)
    pre = _mm(jax.nn.gelu(ys), w_sg) + b_sg
    y_ssm = pre[:, :d] * jax.nn.sigmoid(pre[:, d:])
    y_conv = _mm(_conv_post_tile(v, ln_g, ln_b), w_co)
    gates = jax.nn.sigmoid(gin + b_gate)
    return gates[:, :d] * y_attn + gates[:, d:2 * d] * y_ssm + gates[:, 2 * d:] * y_conv


def _merge_tile(x, ya, ys, v, gin, w_ao, w_sg, b_sg, ln_g, ln_b, w_co, b_gate, w_mo, g_ffn):
    x1 = x + _mm(_branches_tile(ya, ys, v, gin, w_ao, w_sg, b_sg, ln_g, ln_b, w_co, b_gate), w_mo)
    return x1, _rms(x1, g_ffn)


def _merge_bwd_tile(x1, ya, ys, v, gin, dx1, dhf, w_ao, w_sg, b_sg, ln_g, ln_b, w_co, b_gate, w_mo, g_ffn):
    _, norm_vjp = jax.vjp(_rms, x1, g_ffn)
    dx1_norm, dg_ffn = norm_vjp(dhf)
    dx1 = dx1 + dx1_norm
    merged, branch_vjp = jax.vjp(_branches_tile, ya, ys, v, gin, w_ao, w_sg, b_sg, ln_g, ln_b, w_co, b_gate)
    grads = branch_vjp(_dot_nt(dx1, w_mo))
    return [dx1, *grads[:4]], [*grads[4:], _dot_tn(merged, dx1), dg_ffn]


def _ple_tile(x2, p, w_pi, g_ple, w_pg):
    pre, e = _mm(_rms(x2, g_ple), w_pg), _mm(p, w_pi)
    return x2 + jax.nn.sigmoid(pre) * e, pre, e


def _ple_bwd_tile(x2, p, pre, e, dx3, w_pi, g_ple, w_pg):
    sig = jax.nn.sigmoid(pre)
    dpre = dx3 * e * sig * (1.0 - sig)
    hn, norm_vjp = jax.vjp(_rms, x2, g_ple)
    dx2_norm, dg_ple = norm_vjp(_dot_nt(dpre, w_pg))
    return dx3 + dx2_norm, [_dot_tn(p, dx3 * sig), dg_ple, _dot_tn(hn, dpre)]


def _f32s(vals):
    return [v.astype(F32) for v in vals]


def _rope_tables(positions, inv_lane):
    def fn(pos, inv):
        ang = pos.astype(F32) * inv
        j = lax.broadcasted_iota(jnp.int32, ang.shape, 1) % HEAD_DIM
        c = jnp.where(j < ROPE_DIM, jnp.cos(ang), 1.0)
        s = jnp.sin(ang)
        s = jnp.where(j < ROPE_DIM // 2, -s, jnp.where(j < ROPE_DIM, s, 0.0))
        return [c, s], []

    (c, s), _ = _token_call("rope_tables", fn, 1024, [_whole(positions.reshape(-1, 1))], [inv_lane],
                            [(LANES, F32), (LANES, F32)], [])
    return c, s


def _swap_halves(t):
    n = t.shape[1]
    j = lax.broadcasted_iota(jnp.int32, t.shape, 1) % HEAD_DIM
    lower = pltpu.roll(t, n - ROPE_DIM // 2, 1)
    upper = jnp.where(j < ROPE_DIM, pltpu.roll(t, ROPE_DIM // 2, 1), 0.0)
    return jnp.where(j < ROPE_DIM // 2, lower, upper)


def _rope(t, c, s):
    return t * c + _swap_halves(t) * s


def _rope_t(dt, c, s):
    return dt * c + _swap_halves(dt * s)


def _tile4(a):
    return jnp.concatenate([a] * (Q_WIDTH // LANES), axis=1)


def _attn_mask(n):
    qi = lax.broadcasted_iota(jnp.int32, (GQA_GROUP * BLOCK, 2 * BLOCK), 0) % BLOCK
    kj = lax.broadcasted_iota(jnp.int32, (GQA_GROUP * BLOCK, 2 * BLOCK), 1)
    dist = qi + BLOCK - kj
    return (dist >= 0) & (dist < BLOCK) & ((n > 0) | (kj >= BLOCK))


def _attn_specs(n_seq):
    own = lambda w, blk: pl.BlockSpec((n_seq, BLOCK, w), lambda n: (0, n, blk))
    prev = lambda w, blk: pl.BlockSpec((n_seq, BLOCK, w), lambda n: (0, jnp.maximum(n - 1, 0), blk))
    return [own(ZQ_W, ZQ_BLK), own(ZKV_W, ZKV_BLK), prev(ZKV_W, ZKV_BLK), own(LANES, 0), own(LANES, 0), prev(LANES, 0),
            prev(LANES, 0), pl.BlockSpec((1, N_Q_HEADS), lambda n: (0, 0))]


def _by_seq(a, n_seq):
    return a.reshape(n_seq, a.shape[0] // n_seq, a.shape[1])


ATTN_SCALE = HEAD_DIM ** -0.5


def _stack_heads(t, kh):
    return jnp.concatenate([t[:, (kh * GQA_GROUP + g) * HEAD_DIM:(kh * GQA_GROUP + g + 1) * HEAD_DIM]
                            for g in range(GQA_GROUP)], axis=0)


def _stack_sinks(sink, kh):
    return jnp.concatenate([jnp.broadcast_to(sink[:, kh * GQA_GROUP + g:kh * GQA_GROUP + g + 1], (BLOCK, 1))
                            for g in range(GQA_GROUP)], axis=0)


def _attn_band(b, q_ref, kv_ref, kvp_ref, c_ref, s_ref, cp_ref, sp_ref):
    c, s = c_ref[b], s_ref[b]
    q = _rope(q_ref[b], _tile4(c), _tile4(s)) * ATTN_SCALE
    kv, kvp = kv_ref[b], kvp_ref[b]
    k = _rope(kv[:, :KV_WIDTH], c, s)
    kp = _rope(kvp[:, :KV_WIDTH], cp_ref[b], sp_ref[b])
    kb = jnp.concatenate([kp, k], axis=0)
    vb = jnp.concatenate([kvp[:, KV_WIDTH:], kv[:, KV_WIDTH:]], axis=0)
    return q, kb, vb


def _attention_fwd(z, ctab, stab, sinks, n_seq):
    t = z.shape[0]
    seq = t // n_seq

    def body(q_ref, kv_ref, kvp_ref, c_ref, s_ref, cp_ref, sp_ref, sink_ref, o_ref, lse_ref):
        mask = _attn_mask(pl.program_id(0))
        sink = sink_ref[...]
        lane = lax.broadcasted_iota(jnp.int32, (BLOCK, N_Q_HEADS), 1)
        for b in range(n_seq):
            q, kb, vb = _attn_band(b, q_ref, kv_ref, kvp_ref, c_ref, s_ref, cp_ref, sp_ref)
            lse_all = jnp.zeros((BLOCK, N_Q_HEADS), F32)
            for kh in range(N_KV_HEADS):
                sc = jnp.where(mask, _dot_nt(_stack_heads(q, kh), kb[:, kh * HEAD_DIM:(kh + 1) * HEAD_DIM]), NEG_INF)
                sk = _stack_sinks(sink, kh)
                m = jnp.maximum(jnp.max(sc, axis=-1, keepdims=True), sk)
                pr = jnp.exp(sc - m)
                den = jnp.sum(pr, axis=-1, keepdims=True) + jnp.exp(sk - m)
                out = _dot_nn(pr * (1.0 / den), vb[:, kh * HEAD_DIM:(kh + 1) * HEAD_DIM])
                lse = m + jnp.log(den)
                for g in range(GQA_GROUP):
                    h = kh * GQA_GROUP + g
                    o_ref[b, :, h * HEAD_DIM:(h + 1) * HEAD_DIM] = out[g * BLOCK:(g + 1) * BLOCK].astype(o_ref.dtype)
                    lse_all = jnp.where(lane == h, lse[g * BLOCK:(g + 1) * BLOCK], lse_all)
            lse_ref[b] = lse_all

    rows = lambda w: pl.BlockSpec((n_seq, BLOCK, w), lambda n: (0, n, 0))
    z3, c3, s3 = _by_seq(z, n_seq), _by_seq(ctab, n_seq), _by_seq(stab, n_seq)
    ya, lse = pl.pallas_call(
        body, name="attn_fwd", grid=(seq // BLOCK,),
        out_shape=[jax.ShapeDtypeStruct((n_seq, seq, Q_WIDTH), MXU_DTYPE), jax.ShapeDtypeStruct((n_seq, seq, N_Q_HEADS), F32)],
        in_specs=_attn_specs(n_seq), out_specs=[rows(Q_WIDTH), rows(N_Q_HEADS)],
        compiler_params=_params(("parallel",)))(z3, z3, z3, c3, s3, c3, s3, sinks)
    return ya.reshape(t, Q_WIDTH), lse.reshape(t, N_Q_HEADS)


def _attention_bwd(z, ctab, stab, sinks, ya, lse, dya, dz, n_seq):
    t = z.shape[0]
    seq = t // n_seq

    def body(q_ref, kv_ref, kvp_ref, c_ref, s_ref, cp_ref, sp_ref, sink_ref, o_ref, lse_ref, do_ref, _,
             dq_ref, dkv_ref, dkvp_ref, dsink_ref):
        mask = _attn_mask(pl.program_id(0))
        sink = sink_ref[...]
        lane = lax.broadcasted_iota(jnp.int32, (1, N_Q_HEADS), 1)
        dsink = jnp.zeros((1, N_Q_HEADS), F32)
        for b in range(n_seq):
            q, kb, vb = _attn_band(b, q_ref, kv_ref, kvp_ref, c_ref, s_ref, cp_ref, sp_ref)
            lse_all = lse_ref[b]
            o = o_ref[b].astype(F32)
            do = do_ref[b].astype(F32)
            dq_parts = []
            dk_parts, dv_parts = [], []
            for kh in range(N_KV_HEADS):
                kbh = kb[:, kh * HEAD_DIM:(kh + 1) * HEAD_DIM]
                vbh = vb[:, kh * HEAD_DIM:(kh + 1) * HEAD_DIM]
                qs, dos = _stack_heads(q, kh), _stack_heads(do, kh)
                lse = jnp.concatenate([lse_all[:, kh * GQA_GROUP + g:kh * GQA_GROUP + g + 1] for g in range(GQA_GROUP)], axis=0)
                pr = jnp.exp(jnp.where(mask, _dot_nt(qs, kbh), NEG_INF) - lse)
                delta = jnp.sum(dos * _stack_heads(o, kh), axis=-1, keepdims=True)
                ds = pr * (_dot_nt(dos, vbh) - delta)
                dqs = _dot_nn(ds, kbh)
                dq_parts += [dqs[g * BLOCK:(g + 1) * BLOCK] for g in range(GQA_GROUP)]
                dk_parts.append(_dot_tn(ds, qs))
                dv_parts.append(_dot_tn(pr, dos))
                dsk = jnp.exp(_stack_sinks(sink, kh) - lse) * delta
                for g in range(GQA_GROUP):
                    dsink = dsink + jnp.where(lane == kh * GQA_GROUP + g, -jnp.sum(dsk[g * BLOCK:(g + 1) * BLOCK]), 0.0)
            c, s = c_ref[b], s_ref[b]
            dq_ref[b] = _rope_t(jnp.concatenate(dq_parts, axis=1) * ATTN_SCALE, _tile4(c), _tile4(s)).astype(dq_ref.dtype)
            dk = jnp.concatenate(dk_parts, axis=1)
            dv = jnp.concatenate(dv_parts, axis=1)
            dkv_ref[b, :, :KV_WIDTH] = _rope_t(dk[BLOCK:], c, s)
            dkv_ref[b, :, KV_WIDTH:] = dv[BLOCK:]
            dkvp_ref[b, :, :KV_WIDTH] = _rope_t(dk[:BLOCK], cp_ref[b], sp_ref[b])
            dkvp_ref[b, :, KV_WIDTH:] = dv[:BLOCK]

        @pl.when(pl.program_id(0) == 0)
        def _():
            dsink_ref[...] = jnp.zeros_like(dsink_ref)

        dsink_ref[...] += dsink

    rows = lambda w: pl.BlockSpec((n_seq, BLOCK, w), lambda n: (0, n, 0))
    by_seq = lambda a: _by_seq(a, n_seq)
    z3, c3, s3 = by_seq(z), by_seq(ctab), by_seq(stab)
    dz, dkv, dkvp, dsink = pl.pallas_call(
        body, name="attn_bwd", grid=(seq // BLOCK,),
        out_shape=[jax.ShapeDtypeStruct((n_seq, seq, Z_WIDTH), dz.dtype), jax.ShapeDtypeStruct((n_seq, seq, ZKV_W), F32),
                   jax.ShapeDtypeStruct((n_seq, seq, ZKV_W), F32), jax.ShapeDtypeStruct((1, N_Q_HEADS), F32)],
        in_specs=_attn_specs(n_seq) + [rows(Q_WIDTH), rows(N_Q_HEADS), rows(Q_WIDTH), _ANY],
        out_specs=[pl.BlockSpec((n_seq, BLOCK, ZQ_W), lambda n: (0, n, ZQ_BLK)), rows(ZKV_W), rows(ZKV_W),
                   pl.BlockSpec((1, N_Q_HEADS), lambda n: (0, 0))],
        input_output_aliases={11: 0},
        compiler_params=_params(("arbitrary",)))(z3, z3, z3, c3, s3, c3, s3, sinks, by_seq(ya), by_seq(lse), by_seq(dya), by_seq(dz))
    return dz.reshape(t, Z_WIDTH), dkv.reshape(t, ZKV_W), dkvp.reshape(t, ZKV_W), dsink


def _kv_combine(dkv, dkvp, dz, n_seq):
    t = dkv.shape[0]
    seq = t // n_seq
    rows = _pick(seq, (512, 256, 128))
    nt, per = seq // rows, rows // BLOCK
    n_blocks = t // BLOCK

    def body(dkv_ref, dkvp_ref, dkvn_ref, _, o_ref):
        nxt = jnp.where(pl.program_id(1) == nt - 1, 0.0, dkvn_ref[...])
        shifted = nxt if per == 1 else jnp.concatenate([dkvp_ref[BLOCK:, :], nxt], axis=0)
        o_ref[...] = (dkv_ref[...] + shifted).astype(o_ref.dtype)

    tile = pl.BlockSpec((rows, ZKV_W), lambda b, i: (b * nt + i, 0))
    return pl.pallas_call(
        body, name="kv_combine", grid=(n_seq, nt), out_shape=jax.ShapeDtypeStruct(dz.shape, dz.dtype),
        in_specs=[tile, tile,
                  pl.BlockSpec((BLOCK, ZKV_W), lambda b, i: (jnp.minimum((b * nt + i + 1) * per, n_blocks - 1), 0)), _ANY],
        out_specs=pl.BlockSpec((rows, ZKV_W), lambda b, i: (b * nt + i, ZKV_BLK)), input_output_aliases={3: 0},
        compiler_params=_params(("parallel", "parallel")))(dkv, dkvp, dkvp, dz)


def _ssm_coeff_tile(lam_re, lam_im, log_dt):
    lr = jnp.minimum(lam_re, -1e-4)
    dt = jnp.exp(log_dt)
    mag = jnp.exp(lr * dt)
    a_re = mag * jnp.cos(lam_im * dt)
    a_im = mag * jnp.sin(lam_im * dt)
    den = lr * lr + lam_im * lam_im
    x_re = a_re - 1.0
    f_re = (x_re * lr + a_im * lam_im) / den
    f_im = (a_im * lr - x_re * lam_im) / den
    return a_re, a_im, f_re, f_im


def _ssm_coeffs(lam_re, lam_im, log_dt):
    def body(lr_ref, li_ref, dt_ref, *o_refs):
        for r, v in zip(o_refs, _ssm_coeff_tile(lr_ref[...], li_ref[...], dt_ref[...]), strict=True):
            r[...] = v

    return pl.pallas_call(body, name="ssm_coeffs", out_shape=[jax.ShapeDtypeStruct(lam_re.shape, F32)] * 4)(
        lam_re, lam_im, log_dt)


def _ssm_coeffs_bwd(lam_re, lam_im, log_dt, cts):
    def body(lr_ref, li_ref, dt_ref, c0, c1, c2, c3, dlr_ref, dli_ref, ddt_ref):
        _, vjp = jax.vjp(_ssm_coeff_tile, lr_ref[...], li_ref[...], dt_ref[...])
        dlr, dli, ddt = vjp((c0[...], c1[...], c2[...], c3[...]))
        dlr_ref[...] = dlr
        dli_ref[...] = dli
        ddt_ref[...] = ddt

    return pl.pallas_call(
        body, name="ssm_coeffs_bwd",
        out_shape=[jax.ShapeDtypeStruct(lam_re.shape, F32)] * 2 + [jax.ShapeDtypeStruct(log_dt.shape, F32)])(
        lam_re, lam_im, log_dt, *cts)


def _ssm_chunk(t):
    return _pick(t, (256, 128))


def _ssm_fwd(z, bmat, a_row, f_row, cmat, d_row, n_seq):
    t = z.shape[0]
    seq = t // n_seq
    lc = _ssm_chunk(seq)
    nc = seq // lc
    n2 = 2 * SSM_LANES

    re, im = pl.ds(0, SSM_LANES), pl.ds(SSM_LANES, SSM_LANES)

    def body(u_ref, b_ref, a_ref, f_ref, c_ref, d_ref, y_ref, s_ref, bu_ref, st_ref):
        @pl.when(pl.program_id(0) == 0)
        def _():
            st_ref[...] = jnp.zeros_like(st_ref)

        fr, fi = f_ref[:, :SSM_LANES], f_ref[:, SSM_LANES:]
        for b in range(n_seq):
            proj = _dot_nn(u_ref[b], b_ref[...])
            pr, pi = proj[:, :SSM_LANES], proj[:, SSM_LANES:]
            bu_ref[b, :, :SSM_LANES] = fr * pr - fi * pi
            bu_ref[b, :, SSM_LANES:] = fr * pi + fi * pr
        ar, ai = a_ref[:, :SSM_LANES], a_ref[:, SSM_LANES:]

        def step(i, carry):
            out = []
            for b in range(n_seq):
                sr, si = carry[2 * b], carry[2 * b + 1]
                nr = ar * sr - ai * si + bu_ref[b, pl.ds(i, 1), re]
                ni = ar * si + ai * sr + bu_ref[b, pl.ds(i, 1), im]
                s_ref[b, pl.ds(i, 1), re] = nr
                s_ref[b, pl.ds(i, 1), im] = ni
                out += [nr, ni]
            return tuple(out)

        carry = lax.fori_loop(0, lc, step, tuple(st_ref[b, 0:1, part] for b in range(n_seq) for part in (re, im)), unroll=8)
        for b in range(n_seq):
            st_ref[b, 0:1, re], st_ref[b, 0:1, im] = carry[2 * b], carry[2 * b + 1]
            y_ref[b] = _dot_nn(s_ref[b], c_ref[...]) + d_ref[...] * u_ref[b]

    const = lambda shape: pl.BlockSpec(shape, lambda c: (0, 0))
    rows = lambda w, cb: pl.BlockSpec((n_seq, lc, w), lambda c: (0, c, cb))
    ys, states = pl.pallas_call(
        body, name="ssm_fwd", grid=(nc,),
        out_shape=[jax.ShapeDtypeStruct((n_seq, seq, SSM_WIDTH), F32), jax.ShapeDtypeStruct((n_seq, seq, n2), F32)],
        in_specs=[rows(ZS_W, ZS_BLK), const((SSM_WIDTH, n2)), const((1, n2)), const((1, n2)), const((n2, SSM_WIDTH)),
                  const((1, SSM_WIDTH))],
        out_specs=[rows(SSM_WIDTH, 0), rows(n2, 0)],
        scratch_shapes=[pltpu.VMEM((n_seq, lc, n2), F32), pltpu.VMEM((n_seq, 8, n2), F32)],
        compiler_params=_params(("arbitrary",)))(_by_seq(z, n_seq), bmat, a_row, f_row, cmat, d_row)
    return ys.reshape(t, SSM_WIDTH), states.reshape(t, n2)


def _ssm_bwd(z, states, dy, bmat, a_row, f_row, cmat, d_row, dz, n_seq):
    t = z.shape[0]
    seq = t // n_seq
    lc = _ssm_chunk(seq)
    nc = seq // lc
    n2 = 2 * SSM_LANES

    def body(dy_ref, u_ref, s_ref, b_ref, a_ref, f_ref, c_ref, d_ref, _,
             du_ref, db_ref, dc_ref, da_ref, df_ref, dd_ref, g_ref, carry_ref):
        @pl.when(pl.program_id(0) == 0)
        def _():
            for r in (db_ref, dc_ref, da_ref, df_ref, dd_ref, carry_ref):
                r[...] = jnp.zeros_like(r)

        re, im = pl.ds(0, SSM_LANES), pl.ds(SSM_LANES, SSM_LANES)
        for b in range(n_seq):
            dy = dy_ref[b]
            g_ref[b, 0:lc, :] = _dot_nt(dy, c_ref[...])
            g_ref[b, lc:lc + 8, :] = carry_ref[b]
            dc_ref[...] += _dot_tn(s_ref[b], dy)
            dd_ref[...] += jnp.sum(dy * u_ref[b], axis=0, keepdims=True)
        ar, ai = a_ref[:, :SSM_LANES], a_ref[:, SSM_LANES:]

        def step(i, carry):
            r = lc - 1 - i
            out = []
            for b in range(n_seq):
                gr, gi = carry[2 * b], carry[2 * b + 1]
                nr = g_ref[b, pl.ds(r, 1), re] + ar * gr + ai * gi
                ni = g_ref[b, pl.ds(r, 1), im] - ai * gr + ar * gi
                g_ref[b, pl.ds(r, 1), re] = nr
                g_ref[b, pl.ds(r, 1), im] = ni
                out += [nr, ni]
            return tuple(out)

        carry = lax.fori_loop(0, lc, step, tuple(carry_ref[b, 0:1, part] for b in range(n_seq) for part in (re, im)), unroll=8)
        fr, fi = f_ref[:, :SSM_LANES], f_ref[:, SSM_LANES:]
        for b in range(n_seq):
            carry_ref[b, 0:1, re], carry_ref[b, 0:1, im] = carry[2 * b], carry[2 * b + 1]
            dy, u, st = dy_ref[b], u_ref[b], s_ref[b]
            sr, si = st[:, :SSM_LANES], st[:, SSM_LANES:]
            gnr, gni = g_ref[b, pl.ds(1, lc), re], g_ref[b, pl.ds(1, lc), im]
            da_ref[:, :SSM_LANES] += jnp.sum(gnr * sr + gni * si, axis=0, keepdims=True)
            da_ref[:, SSM_LANES:] += jnp.sum(gni * sr - gnr * si, axis=0, keepdims=True)
            gr_all, gi_all = g_ref[b, 0:lc, :SSM_LANES], g_ref[b, 0:lc, SSM_LANES:]
            proj = _dot_nn(u, b_ref[...])
            pr, pi = proj[:, :SSM_LANES], proj[:, SSM_LANES:]
            df_ref[:, :SSM_LANES] += jnp.sum(gr_all * pr + gi_all * pi, axis=0, keepdims=True)
            df_ref[:, SSM_LANES:] += jnp.sum(gi_all * pr - gr_all * pi, axis=0, keepdims=True)
            dproj = jnp.concatenate([fr * gr_all + fi * gi_all, fr * gi_all - fi * gr_all], axis=1).astype(MXU_DTYPE)
            du_ref[b] = (_dot_nt(dproj, b_ref[...]) + d_ref[...] * dy).astype(du_ref.dtype)
            db_ref[...] += _dot_tn(u, dproj)

    const = lambda shape: pl.BlockSpec(shape, lambda c: (0, 0))
    rows = lambda w, cb: pl.BlockSpec((n_seq, lc, w), lambda c: (0, nc - 1 - c, cb))
    by_seq = lambda a: _by_seq(a, n_seq)
    dz, *sums = pl.pallas_call(
        body, name="ssm_bwd", grid=(nc,),
        out_shape=[jax.ShapeDtypeStruct((n_seq, seq, Z_WIDTH), dz.dtype), jax.ShapeDtypeStruct((SSM_WIDTH, n2), F32),
                   jax.ShapeDtypeStruct((n2, SSM_WIDTH), F32), jax.ShapeDtypeStruct((1, n2), F32),
                   jax.ShapeDtypeStruct((1, n2), F32), jax.ShapeDtypeStruct((1, SSM_WIDTH), F32)],
        in_specs=[rows(SSM_WIDTH, 0), rows(ZS_W, ZS_BLK), rows(n2, 0), const((SSM_WIDTH, n2)), const((1, n2)),
                  const((1, n2)), const((n2, SSM_WIDTH)), const((1, SSM_WIDTH)), _ANY],
        out_specs=[rows(ZS_W, ZS_BLK), const((SSM_WIDTH, n2)), const((n2, SSM_WIDTH)), const((1, n2)), const((1, n2)),
                   const((1, SSM_WIDTH))],
        input_output_aliases={8: 0},
        scratch_shapes=[pltpu.VMEM((n_seq, lc + 8, n2), F32), pltpu.VMEM((n_seq, 8, n2), F32)],
        compiler_params=_params(("arbitrary",)))(by_seq(dy), by_seq(z), by_seq(states), bmat, a_row, f_row, cmat, d_row, by_seq(dz))
    return (dz.reshape(t, Z_WIDTH), *sums)


def _conv_chunk(t):
    return _pick(t, (512, 256, 128))


def _glu(c):
    return c[:, :CONV_WIDTH] * jax.nn.sigmoid(c[:, CONV_WIDTH:])


def _conv_specs(lc, nc):
    per = lc // CONV_HALO
    return [pl.BlockSpec((lc, ZC_W), lambda b, c: (b * nc + c, ZC_BLK)),
            pl.BlockSpec((CONV_HALO, ZC_W), lambda b, c: (jnp.maximum((b * nc + c) * per - 1, 0), ZC_BLK))]


def _conv_fill(c_ref, cp_ref, ue_ref, lc):
    ue_ref[0:CONV_HALO, :] = jnp.where(pl.program_id(1) > 0, _glu(cp_ref[...]), 0.0)
    ue_ref[CONV_HALO:CONV_HALO + lc, :] = _glu(c_ref[...])


CONV_ROWS = 64
SUBLANES = 8
CONV_SHIFT_ROWS = CONV_HALO - SUBLANES


def _shifted_copies(src_ref, sh_ref, lc):
    for b in range(1, SUBLANES):
        sh_ref[b - 1, :, :] = src_ref[pl.ds(b, lc + CONV_SHIFT_ROWS), :]


def _tap_rows(src_ref, sh_ref, offset, r0):
    b = offset % SUBLANES
    rows = pl.ds(r0 + offset - b, CONV_ROWS)
    return src_ref[rows, :] if b == 0 else sh_ref[b - 1, rows, :]


def _conv_apply(ue_ref, ush_ref, w_ref, b_ref, o_ref, lc):
    for r0 in range(0, lc, CONV_ROWS):
        acc = jnp.zeros((CONV_ROWS, CONV_WIDTH), F32) + b_ref[...]
        for k in range(CONV_K):
            acc = acc + w_ref[k:k + 1, :] * _tap_rows(ue_ref, ush_ref, k + CONV_HALO - CONV_K + 1, r0)
        o_ref[r0:r0 + CONV_ROWS, :] = acc


def _conv_fwd(z, dw_w, dw_b, n_seq):
    t = z.shape[0]
    seq = t // n_seq
    lc = _conv_chunk(seq)
    nc = seq // lc

    def body(c_ref, cp_ref, w_ref, b_ref, o_ref, ue_ref, ush_ref):
        _conv_fill(c_ref, cp_ref, ue_ref, lc)
        _shifted_copies(ue_ref, ush_ref, lc)
        _conv_apply(ue_ref, ush_ref, w_ref, b_ref, o_ref, lc)

    const = lambda a: pl.BlockSpec(a.shape, lambda b, c: (0, 0))
    return pl.pallas_call(
        body, name="conv_fwd", grid=(n_seq, nc), out_shape=jax.ShapeDtypeStruct((t, CONV_WIDTH), F32),
        in_specs=_conv_specs(lc, nc) + [const(dw_w), const(dw_b)],
        out_specs=pl.BlockSpec((lc, CONV_WIDTH), lambda b, c: (b * nc + c, 0)),
        scratch_shapes=[pltpu.VMEM((CONV_HALO + lc, CONV_WIDTH), F32),
                        pltpu.VMEM((SUBLANES - 1, lc + CONV_SHIFT_ROWS, CONV_WIDTH), F32)],
        compiler_params=_params(("parallel", "parallel")))(z, z, dw_w, dw_b)


def _conv_bwd_taps(z, dv, dw_w, dz, n_seq):
    t = z.shape[0]
    seq = t // n_seq
    lc = _conv_chunk(seq)
    nc = seq // lc
    per = lc // CONV_HALO
    n_halo = t // CONV_HALO

    def body(c_ref, cp_ref, dv_ref, dvn_ref, w_ref, _, dc_ref, dw_ref, ue_ref, dve_ref, ush_ref, dsh_ref):
        @pl.when((pl.program_id(0) == 0) & (pl.program_id(1) == 0))
        def _():
            dw_ref[...] = jnp.zeros_like(dw_ref)

        _conv_fill(c_ref, cp_ref, ue_ref, lc)
        dv = dv_ref[...]
        dve_ref[0:lc, :] = dv
        dve_ref[lc:lc + CONV_HALO, :] = jnp.where(pl.program_id(1) < nc - 1, dvn_ref[...], 0.0)
        _shifted_copies(ue_ref, ush_ref, lc)
        _shifted_copies(dve_ref, dsh_ref, lc)
        dw_ref[CONV_K:CONV_K + 1, :] += jnp.sum(dv, axis=0, keepdims=True)
        for r0 in range(0, lc, CONV_ROWS):
            rows = pl.ds(r0, CONV_ROWS)
            dv_rows = dv_ref[rows, :]
            du = jnp.zeros((CONV_ROWS, CONV_WIDTH), F32)
            for k in range(CONV_K):
                du = du + w_ref[k:k + 1, :] * _tap_rows(dve_ref, dsh_ref, CONV_K - 1 - k, r0)
                taps = _tap_rows(ue_ref, ush_ref, k + CONV_HALO - CONV_K + 1, r0)
                dw_ref[k:k + 1, :] += jnp.sum(dv_rows * taps, axis=0, keepdims=True)
            c = c_ref[rows, :]
            a, sg = c[:, :CONV_WIDTH], jax.nn.sigmoid(c[:, CONV_WIDTH:])
            dc_ref[rows, :CONV_WIDTH] = (du * sg).astype(dc_ref.dtype)
            dc_ref[rows, CONV_WIDTH:] = (du * a * sg * (1.0 - sg)).astype(dc_ref.dtype)

    return pl.pallas_call(
        body, name="conv_bwd_taps", grid=(n_seq, nc),
        out_shape=[jax.ShapeDtypeStruct(dz.shape, dz.dtype), jax.ShapeDtypeStruct((CONV_HALO, CONV_WIDTH), F32)],
        in_specs=_conv_specs(lc, nc) + [
            pl.BlockSpec((lc, CONV_WIDTH), lambda b, c: (b * nc + c, 0)),
            pl.BlockSpec((CONV_HALO, CONV_WIDTH), lambda b, c: (jnp.minimum((b * nc + c + 1) * per, n_halo - 1), 0)),
            pl.BlockSpec(dw_w.shape, lambda b, c: (0, 0)), _ANY],
        out_specs=[pl.BlockSpec((lc, ZC_W), lambda b, c: (b * nc + c, ZC_BLK)),
                   pl.BlockSpec((CONV_HALO, CONV_WIDTH), lambda b, c: (0, 0))],
        input_output_aliases={5: 0},
        scratch_shapes=[pltpu.VMEM((CONV_HALO + lc, CONV_WIDTH), F32), pltpu.VMEM((lc + CONV_HALO, CONV_WIDTH), F32)]
        + [pltpu.VMEM((SUBLANES - 1, lc + CONV_SHIFT_ROWS, CONV_WIDTH), F32)] * 2,
        compiler_params=_params(("arbitrary", "arbitrary")))(z, z, dv, dv, dw_w, dz)


def _row(v):
    return v.reshape(1, -1)


def _ssm_mats(b_re, b_im, c_re, c_im):
    eye = jnp.eye(SSM_GROUPS, dtype=bool)
    bm = jnp.stack([b_re, b_im]).transpose(1, 3, 0, 2)[:, :, :, None, :]
    bmat = jnp.where(eye[:, None, None, :, None], bm, 0.0).reshape(SSM_WIDTH, 2 * SSM_LANES)
    cm = jnp.stack([c_re, -c_im]).transpose(0, 1, 3, 2)[:, :, :, None, :]
    cmat = jnp.where(eye[None, :, None, :, None], cm, 0.0).reshape(2 * SSM_LANES, SSM_WIDTH)
    return bmat.astype(MXU_DTYPE), cmat.astype(MXU_DTYPE)


def _ssm_mats_t(dbmat, dcmat):
    eye = jnp.eye(SSM_GROUPS, dtype=bool)
    db = dbmat.reshape(SSM_GROUPS, SSM_GROUP, 2, SSM_GROUPS, SSM_STATE)
    db = jnp.sum(jnp.where(eye[:, None, None, :, None], db, 0.0), axis=3).transpose(2, 0, 3, 1)
    dc = dcmat.reshape(2, SSM_GROUPS, SSM_STATE, SSM_GROUPS, SSM_GROUP)
    dc = jnp.sum(jnp.where(eye[None, :, None, :, None], dc, 0.0), axis=3).transpose(0, 1, 3, 2)
    return db[0], db[1], dc[0], -dc[1]


def _layer_fwd(x, p, w, get_ffn_weights, sp, ctab, stab, n_seq):
    z, h = _in_proj(x, sp["mix_norm_g"], w["w_in"])
    ya, lse = _attention_fwd(z, ctab, stab, sp["attn_sinks"], n_seq)
    ys, states = _ssm_fwd(z, sp["bmat"], sp["a_row"], sp["f_row"], sp["cmat"], sp["ssm_d"], n_seq)
    v = _conv_fwd(z, w["conv_dw_w"], sp["conv_dw_b"], n_seq)
    merge_consts = [w["w_attn_out"], w["w_ssm_glu"], sp["b_ssm_glu"], sp["conv_norm_g"], sp["conv_norm_b"], w["w_conv_out"],
                    sp["b_gate"], w["w_mix_out"], sp["ffn_norm_g"]]
    (x1, hf), _ = _token_call("merge", lambda *a: (list(_merge_tile(*_f32s(a))), []), 512,
                              [_whole(x), _whole(ya), _whole(ys), _whole(v), (z, ZG_W, 0)], merge_consts,
                              [(x.shape[1], F32), (x.shape[1], MXU_DTYPE)], [])
    w = dict(w, **get_ffn_weights(x1))
    gate, up, act = _ffn_in_act(hf, w["w_ffn_in"])
    x2 = _matmul_add("mm_ffn_out", act, w["w_ffn_out"], x1)
    d = x.shape[1]
    (x3, ple_pre, ple_e), _ = _token_call(
        "ple", lambda *a: (list(_ple_tile(*a)), []), 512, [_whole(x2), p],
        [w["w_ple_in"], sp["ple_norm_g"], w["w_ple_gate"]], [(d, F32), (d, MXU_DTYPE), (d, MXU_DTYPE)], [])
    saved = dict(h=h, z=z, ya=ya, lse=lse, ys=ys, states=states, v=v, x1=x1, hf=hf, gate=gate, up=up, act=act, x2=x2, p=p,
                 ple_pre=ple_pre, ple_e=ple_e, x=x)
    return x3, saved, w


def _layer_bwd(dx3, sv, w, sp, on_grads, ctab, stab, n_seq):
    d = dx3.shape[1]
    gw, gs = {}, {}

    def ple_bwd(*a):
        dx2, accs = _ple_bwd_tile(*_f32s(a))
        return [dx2, dx2], accs

    (dx2, dffn), (gw["w_ple_in"], gs["ple_norm_g"], gw["w_ple_gate"]) = _token_call(
        "ple_bwd", ple_bwd, 512, [_whole(sv["x2"]), sv["p"], _whole(sv["ple_pre"]), _whole(sv["ple_e"]), _whole(dx3)],
        [w["w_ple_in"], sp["ple_norm_g"], w["w_ple_gate"]], [(d, F32), (d, MXU_DTYPE)],
        [w["w_ple_in"].shape, (1, d), w["w_ple_gate"].shape])

    dgate, dup = _ffn_mid_bwd(dffn, w["w_ffn_out"], sv["gate"], sv["up"])
    gw["w_ffn_out"] = _matmul_tn("mm_ffn_out_dw", sv["act"], dffn)
    dhf = _ffn_in_dx(dgate, dup, w["w_ffn_in"])
    gw["w_ffn_in"] = jnp.concatenate([_matmul_tn("mm_ffn_gate_dw", sv["hf"], dgate), _matmul_tn("mm_ffn_up_dw", sv["hf"], dup)],
                                     axis=1)

    token = on_grads("ffn", gw)

    def merge_bwd(*a):
        return _merge_bwd_tile(*_f32s(a))

    b_gate = sp["b_gate"] if token is None else sp["b_gate"] + token[0:1, 0:1]
    merge_consts = [w["w_attn_out"], w["w_ssm_glu"], sp["b_ssm_glu"], sp["conv_norm_g"], sp["conv_norm_b"], w["w_conv_out"],
                    b_gate, w["w_mix_out"], sp["ffn_norm_g"]]
    dz = lax.empty(sv["z"].shape, MXU_DTYPE)
    (dx_res, dya, dys, dv, dz), macc = _token_call(
        "merge_bwd", merge_bwd, 256,
        [_whole(sv["x1"]), _whole(sv["ya"]), _whole(sv["ys"]), _whole(sv["v"]), (sv["z"], ZG_W, 0), _whole(dx2), _whole(dhf)],
        merge_consts, [(d, F32), (Q_WIDTH, MXU_DTYPE), (SSM_WIDTH, F32), (CONV_WIDTH, F32)],
        [c.shape for c in merge_consts], into=(dz, ZG_W, 0))
    (gw["w_attn_out"], gw["w_ssm_glu"], gs["b_ssm_glu"], gs["conv_norm_g"], gs["conv_norm_b"], gw["w_conv_out"], gs["b_gate"],
     gw["w_mix_out"], gs["ffn_norm_g"]) = macc

    dz, dw_taps = _conv_bwd_taps(sv["z"], dv, w["conv_dw_w"], dz, n_seq)
    gw["conv_dw_w"], gs["conv_dw_b"] = dw_taps[:CONV_K], dw_taps[CONV_K:]

    dz, gs["bmat"], gs["cmat"], gs["a_row"], gs["f_row"], gs["ssm_d"] = _ssm_bwd(
        sv["z"], sv["states"], dys, sp["bmat"], sp["a_row"], sp["f_row"], sp["cmat"], sp["ssm_d"], dz, n_seq)

    dz, dkv, dkvp, gs["attn_sinks"] = _attention_bwd(sv["z"], ctab, stab, sp["attn_sinks"], sv["ya"], sv["lse"], dya, dz, n_seq)
    dz = _kv_combine(dkv, dkvp, dz, n_seq)
    gw["w_in"] = _matmul_tn("mm_in_dw", sv["h"], dz)
    token = on_grads("mix", gw)
    g_in = sp["mix_norm_g"] if token is None else sp["mix_norm_g"] + token[0:1, 0:1]
    dx, gs["mix_norm_g"] = _in_proj_bwd(dz, w["w_in"], sv["x"], dx_res, g_in)
    return dx, gw, gs


def _loss_and_grad(x, target, g):
    def fn(x, tgt, g):
        def f(x, g):
            err = _rms(x, g) - tgt
            return 0.5 * jnp.mean(err * err, axis=-1, keepdims=True)

        per_token, vjp = jax.vjp(f, x, g)
        dx, dg = vjp(jnp.ones_like(per_token))
        return [dx], [jnp.sum(per_token, axis=0, keepdims=True), dg]

    (dx,), (loss, dg) = _token_call("loss", fn, 512, [_whole(x), _whole(target)], [g], [(x.shape[1], F32)],
                                    [(8, LANES), (1, x.shape[1])])
    return loss[0, 0], dx, dg


def _mesh_place():
    return lax.axis_index("x"), lax.axis_index("y"), lax.axis_index("c")


def _flip(v, bit):
    return 1 - v if bit else v


_MESH = pl.DeviceIdType.MESH


def _all_gather(name, xs):
    n = len(xs)

    def body(*refs):
        x_refs, out_refs = refs[:n], refs[n:2 * n]
        send_sems, recv_sems, local_sems = refs[2 * n:]
        mx, my, mc = _mesh_place()
        me, sibling = (mx, my, mc), (mx, my, 1 - mc)
        chips = [(1 - mx, my), (mx, 1 - my), (1 - mx, 1 - my)]

        def slot(a, px, py, pc):
            return out_refs[a].at[4 * px + 2 * py + pc]

        def copy(a, k, block, to, src=None):
            return pltpu.make_async_remote_copy(
                src_ref=slot(a, *block) if src is None else src, dst_ref=slot(a, *block), send_sem=send_sems.at[7 * a + k],
                recv_sem=recv_sems.at[7 * a + k], device_id=to, device_id_type=_MESH)

        mine = [pltpu.make_async_copy(x_refs[a], slot(a, *me), local_sems.at[a]) for a in range(n)]
        for cp in mine:
            cp.start()
        first = [copy(a, 0, me, sibling, src=x_refs[a]) for a in range(n)]
        first += [copy(a, 1 + j, me, (*chip, mc), src=x_refs[a]) for j, chip in enumerate(chips) for a in range(n)]
        for cp in first:
            cp.start()
        passed = []
        for j, chip in enumerate(chips):
            for a in range(n):
                copy(a, 1 + j, (*chip, mc), me).wait_recv()
                passed.append(copy(a, 4 + j, (*chip, mc), sibling))
                passed[-1].start()
        for a in range(n):
            copy(a, 0, sibling, me).wait_recv()
            for j, chip in enumerate(chips):
                copy(a, 4 + j, (*chip, 1 - mc), me).wait_recv()
        for cp in first + passed:
            cp.wait_send()
        for cp in mine:
            cp.wait()

    return pl.pallas_call(
        body, name=name, out_shape=[jax.ShapeDtypeStruct((N_DEV,) + x.shape, x.dtype) for x in xs], in_specs=[_ANY] * n,
        out_specs=[_ANY] * n,
        scratch_shapes=[pltpu.SemaphoreType.DMA((7 * n,)), pltpu.SemaphoreType.DMA((7 * n,)), pltpu.SemaphoreType.DMA((n,))])(*xs)


def _direct_copies(kind, src_refs, land_refs, send_sems, recv_sems, local_sems):
    mx, my, mc = _mesh_place()
    me = 4 * mx + 2 * my + mc
    n = len(src_refs)
    own = [pltpu.make_async_copy(src_refs[a] if kind == "gather" else src_refs[a].at[me], land_refs[a].at[me], local_sems.at[a])
           for a in range(n)]
    copies = []
    for rel in range(1, N_DEV):
        px, py, pc = _flip(mx, rel & 4), _flip(my, rel & 2), _flip(mc, rel & 1)
        for a in range(n):
            src = src_refs[a] if kind == "gather" else src_refs[a].at[4 * px + 2 * py + pc]
            copies.append(pltpu.make_async_remote_copy(
                src_ref=src, dst_ref=land_refs[a].at[me], send_sem=send_sems.at[7 * a + rel - 1],
                recv_sem=recv_sems.at[7 * a + rel - 1], device_id=(px, py, pc), device_id_type=_MESH))
    return copies, own


_HBM = pl.BlockSpec(memory_space=pltpu.HBM)
_SEM = pl.BlockSpec(memory_space=pltpu.SEMAPHORE)
_DATAFLOW = pltpu.SideEffectType.DATAFLOW_SIDE_EFFECTING


def _exchange_start(name, kind, groups):
    sizes = [len(g) for g in groups]
    srcs = [s for g in groups for s in g]
    lands = [lax.empty(((N_DEV,) + s.shape) if kind == "gather" else s.shape, s.dtype) for s in srcs]
    n, n_g = len(srcs), len(groups)
    first = [sum(sizes[:g]) for g in range(n_g)]

    def body(*refs):
        src_refs, land_refs, sems = refs[:n], refs[n:2 * n], refs[2 * n:2 * n + 3 * n_g]
        for g in range(n_g):
            span = slice(first[g], first[g] + sizes[g])
            copies, own = _direct_copies(kind, src_refs[span], land_refs[span], *sems[3 * g:3 * g + 3])
            for cp in own + copies:
                cp.start()
        refs[-1][...] = jnp.zeros_like(refs[-1])

    hbm = lambda a: pltpu.with_memory_space_constraint(a, pltpu.HBM)
    sem_shapes = [pltpu.SemaphoreType.DMA((k * m,)) for m in sizes for k in (7, 7, 1)]
    out = pl.pallas_call(
        body, name=name,
        out_shape=sem_shapes + [pltpu.HBM(a.shape, a.dtype) for a in srcs + lands] + [jax.ShapeDtypeStruct((8, LANES), F32)],
        in_specs=[_HBM] * (2 * n), out_specs=[_SEM] * (3 * n_g) + [_HBM] * (2 * n) + [pl.BlockSpec(memory_space=pltpu.VMEM)],
        input_output_aliases={i: 3 * n_g + i for i in range(2 * n)},
        compiler_params=pltpu.CompilerParams(has_side_effects=_DATAFLOW))(*[hbm(a) for a in srcs + lands])
    sems, arrays = out[:3 * n_g], out[3 * n_g:-1]
    started = [(kind, (*sems[3 * g:3 * g + 3], *arrays[first[g]:first[g] + sizes[g]],
                       *arrays[n + first[g]:n + first[g] + sizes[g]])) for g in range(n_g)]
    return started, out[-1]


def _exchange_wait(name, started, after):
    kind, (send_sems, recv_sems, local_sems, *arrays) = started
    n = len(arrays) // 2

    def body(*refs):
        src_refs, land_refs = refs[:n], refs[n:2 * n]
        copies, own = _direct_copies(kind, src_refs, land_refs, *refs[2 * n:2 * n + 3])
        for cp in copies + own:
            cp.wait()

    out = pl.pallas_call(
        body, name=name, out_shape=[pltpu.HBM(a.shape, a.dtype) for a in arrays],
        in_specs=[_HBM] * (2 * n) + [_SEM] * 3 + [_ANY], out_specs=[_HBM] * (2 * n),
        input_output_aliases={i: i for i in range(2 * n)},
        compiler_params=pltpu.CompilerParams(has_side_effects=_DATAFLOW))(*arrays, send_sems, recv_sems, local_sems, after)
    return out[n:]


def _adamw_math(g, w, m, v):
    m2 = ADAM_B1 * m + (1.0 - ADAM_B1) * g
    v2 = ADAM_B2 * v + (1.0 - ADAM_B2) * jnp.square(g)
    m_hat = m2 / (1.0 - ADAM_B1 ** ADAM_STEP)
    v_hat = v2 / (1.0 - ADAM_B2 ** ADAM_STEP)
    return g, -ADAM_LR * (m_hat / (jnp.sqrt(v_hat) + ADAM_EPS) + ADAM_WD * w), m2, v2


def _sum_blocks(ref):
    g = ref[0].astype(F32)
    for j in range(1, N_DEV):
        g = g + ref[j].astype(F32)
    return g


def _adamw_flat(name, parts, w, m, v):
    r = w.shape[0]
    tile = _pick(r, (1024, 512, 256, 128, 8))

    def body(p_ref, w_ref, m_ref, v_ref, *o_refs):
        for o, val in zip(o_refs, _adamw_math(_sum_blocks(p_ref), w_ref[...], m_ref[...], v_ref[...]), strict=True):
            o[...] = val

    flat = pl.BlockSpec((tile, LANES), lambda i: (i, 0))
    return pl.pallas_call(
        body, name=name, grid=(r // tile,), out_shape=[jax.ShapeDtypeStruct((r, LANES), F32)] * 4,
        in_specs=[pl.BlockSpec((N_DEV, tile, LANES), lambda i: (0, i, 0)), flat, flat, flat], out_specs=[flat] * 4,
        compiler_params=_params(("parallel",)))(parts, w, m, v)


def _adamw_cols(name, landed, base, stride, w, m, v):
    depth, rows, cs = w.shape
    n_slab = -(-cs // LANES)
    tr = _pick(rows, (SLAB_TILE,))

    def body(*refs):
        slabs, (w_ref, m_ref, v_ref), o_refs = refs[:n_slab], refs[n_slab:n_slab + 3], refs[n_slab + 3:]
        g = jnp.concatenate([_sum_blocks(s)[:, :min(LANES, cs - LANES * k)] for k, s in enumerate(slabs)], axis=1)
        for o, val in zip(o_refs, _adamw_math(g, w_ref[...], m_ref[...], v_ref[...]), strict=True):
            o[...] = val

    slab = lambda k: pl.BlockSpec((N_DEV, tr, LANES), lambda l, i: (0, (l * stride + base + k * rows) // tr + i, 0))
    nat = pl.BlockSpec((None, tr, cs), lambda l, i: (l, i, 0))
    return pl.pallas_call(
        body, name=name, grid=(depth, rows // tr), out_shape=[jax.ShapeDtypeStruct(w.shape, F32)] * 4,
        in_specs=[slab(k) for k in range(n_slab)] + [nat] * 3, out_specs=[nat] * 4,
        compiler_params=_params(("parallel", "parallel")))(*[landed] * n_slab, w, m, v)


def _adamw_rows(name, landed, base, stride, w, m, v):
    depth, rs, width = w.shape
    tr = math.gcd(rs, base, stride)

    def body(p_ref, w_ref, m_ref, v_ref, *o_refs):
        for o, val in zip(o_refs, _adamw_math(_sum_blocks(p_ref), w_ref[...], m_ref[...], v_ref[...]), strict=True):
            o[...] = val

    nat = pl.BlockSpec((None, tr, width), lambda l, i: (l, i, 0))
    return pl.pallas_call(
        body, name=name, grid=(depth, rs // tr), out_shape=[jax.ShapeDtypeStruct(w.shape, F32)] * 4,
        in_specs=[pl.BlockSpec((N_DEV, tr, width), lambda l, i: (0, (l * stride + base) // tr + i, 0)), nat, nat, nat],
        out_specs=[nat] * 4, compiler_params=_params(("parallel", "parallel")))(landed, w, m, v)


def _adamw_conv(landed, base, stride, w, m, v):
    depth, taps, cs = w.shape

    def body(p_ref, w_ref, m_ref, v_ref, *o_refs):
        g = _sum_blocks(p_ref)[:taps, :cs]
        for o, val in zip(o_refs, _adamw_math(g, w_ref[...], m_ref[...], v_ref[...]), strict=True):
            o[...] = val

    nat = pl.BlockSpec((None, taps, cs), lambda l: (l, 0, 0))
    return pl.pallas_call(
        body, name="adamw_conv", grid=(depth,), out_shape=[jax.ShapeDtypeStruct(w.shape, F32)] * 4,
        in_specs=[pl.BlockSpec((N_DEV, CONV_HALO, LANES), lambda l: (0, (l * stride + base) // CONV_HALO, 0)), nat, nat, nat],
        out_specs=[nat] * 4, compiler_params=_params(("parallel",)))(landed, w, m, v)


def _unshard_cols(name, gathered, start, rows, cs, shift=0):
    n_slab = -(-cs // LANES)
    total = N_DEV * cs
    tr = _pick(rows, (SLAB_TILE,))

    def body(*refs):
        slabs, o_ref = refs[:n_slab], refs[n_slab]
        for j in range(N_DEV):
            for k, s in enumerate(slabs):
                for src, dst, width in _wrapped(j * cs + LANES * k - shift, min(LANES, cs - LANES * k), total):
                    o_ref[:, dst:dst + width] = s[j, :, src:src + width]

    slab = lambda k: pl.BlockSpec((N_DEV, tr, LANES), lambda i: (0, (start + k * rows) // tr + i, 0))
    return pl.pallas_call(
        body, name=name, grid=(rows // tr,), out_shape=jax.ShapeDtypeStruct((rows, total), gathered.dtype),
        in_specs=[slab(k) for k in range(n_slab)], out_specs=pl.BlockSpec((tr, total), lambda i: (i, 0)),
        compiler_params=_params(("parallel",)))(*[gathered] * n_slab)


def _shard_cols(name, full, cs, shift=0):
    rows, total = full.shape
    n_slab = -(-cs // LANES)
    tr = _pick(rows, (SLAB_TILE,))

    def body(f_ref, o_ref):
        for j in range(N_DEV):
            for k in range(n_slab):
                used = min(LANES, cs - LANES * k)
                for src, dst, width in _wrapped(j * cs + LANES * k - shift, used, total):
                    o_ref[j, k, :, src:src + width] = f_ref[:, dst:dst + width].astype(o_ref.dtype)
                if used < LANES:
                    o_ref[j, k, :, used:] = jnp.zeros((tr, LANES - used), o_ref.dtype)

    out = pl.pallas_call(
        body, name=name, grid=(rows // tr,), out_shape=jax.ShapeDtypeStruct((N_DEV, n_slab, rows, LANES), BF16),
        in_specs=[pl.BlockSpec((tr, total), lambda i: (i, 0))],
        out_specs=pl.BlockSpec((N_DEV, n_slab, tr, LANES), lambda i: (0, 0, i, 0)),
        compiler_params=_params(("parallel",)))(full)
    return out.reshape(N_DEV, n_slab * rows, LANES)


def _wrapped(pos, width, total):
    pos %= total
    if pos + width <= total:
        return [(0, pos, width)]
    head = total - pos
    return [(0, pos, head), (head, 0, width - head)]


CONV_W_PIECES = 3


def _pad_to(n, align):
    return -(-n // align) * align


def _layout(group):
    col_names, row_names, with_conv = GROUPS[group]
    dims = {name: (rows, cols) for name, rows, cols, _ in SHARDED}
    col, off = {}, 0
    for name in col_names:
        rows, cols = dims[name]
        cs = cols // N_DEV
        col[name] = (off, rows, cs)
        off += -(-cs // LANES) * rows
    conv_base = off
    col_rows = _pad_to(off + with_conv * CONV_W_PIECES * CONV_HALO, SLAB_TILE)
    return col, conv_base, col_rows, {name: dims[name][0] // N_DEV for name in row_names}


def _slabs(shard, fill):
    rows, cs = shard.shape
    parts = []
    for k in range(-(-cs // LANES)):
        part = shard[:, LANES * k:min(LANES * (k + 1), cs)]
        parts.append(jnp.pad(part, ((0, 0), (0, LANES - part.shape[1])), constant_values=fill))
    return jnp.concatenate(parts, axis=0)


def _concat_padded(pieces, total, axis):
    used = sum(p.shape[axis] for p in pieces)
    if total > used:
        shape = list(pieces[0].shape)
        shape[axis] = total - used
        pieces = pieces + [jnp.zeros(shape, pieces[0].dtype)]
    return jnp.concatenate(pieces, axis=axis)


def _split3(a):
    hi = a.astype(BF16)
    r1 = a - hi.astype(F32)
    mid = r1.astype(BF16)
    return hi, mid, (r1 - mid.astype(F32)).astype(BF16)


def _pack_small(arrs, lead=()):
    flat = jnp.concatenate([a.reshape(lead + (-1,)) for a in arrs], axis=-1)
    total = _pad_to(flat.shape[-1], 512 * LANES)
    flat = jnp.pad(flat, [(0, 0)] * len(lead) + [(0, total - flat.shape[-1])])
    return flat.reshape(lead + (total // LANES, LANES))


def _unpack_small(flat, shapes):
    flat = flat.reshape(-1)
    res, off = [], 0
    for s in shapes:
        n = int(np.prod(s))
        res.append(flat[off:off + n].reshape(s))
        off += n
    return res


def _small_rows(a, depth):
    n16 = depth * SSM_GROUPS
    a_re, a_im, f_re, f_im = _ssm_coeffs(a["ssm_lambda_re"].reshape(n16, SSM_STATE), a["ssm_lambda_im"].reshape(n16, SSM_STATE),
                                         a["ssm_log_dt"].reshape(n16, 1))
    rows = []
    for l in range(depth):
        sp = {k: _row(a[k][l]) for k in ("mix_norm_g", "b_gate", "attn_sinks", "ssm_d", "b_ssm_glu", "conv_dw_b",
                                         "conv_norm_g", "conv_norm_b", "ffn_norm_g", "ple_norm_g")}
        g = slice(l * SSM_GROUPS, (l + 1) * SSM_GROUPS)
        sp["a_row"] = jnp.concatenate([a_re[g].reshape(1, -1), a_im[g].reshape(1, -1)], axis=1)
        sp["f_row"] = jnp.concatenate([f_re[g].reshape(1, -1), f_im[g].reshape(1, -1)], axis=1)
        sp["bmat"], sp["cmat"] = _ssm_mats(a["ssm_b_re"][l], a["ssm_b_im"][l], a["ssm_c_re"][l], a["ssm_c_im"][l])
        rows.append(sp)
    return rows


def _local_step(a, get_weights, on_grads, depth):
    n_seq, seq, d = a["x"].shape
    t = n_seq * seq
    inv = ROPE_THETA ** (-jnp.arange(0, ROPE_DIM, 2, dtype=F32) / ROPE_DIM)
    lane = np.arange(LANES) % HEAD_DIM
    inv_lane = jnp.where(lane < ROPE_DIM, jnp.tile(inv, LANES // (ROPE_DIM // 2)), 0.0).reshape(1, LANES)
    ctab, stab = _rope_tables(a["positions"].reshape(t), inv_lane)
    small = _small_rows(a, depth)

    x = a["x"].reshape(t, d)
    saved, weights = [], []
    for l in range(depth):
        p_all = a["p"].reshape(depth * t, -1)
        x, sv, w = _layer_fwd(x, (p_all, p_all.shape[1], 0, l * t), get_weights(l, "mix", x),
                              functools.partial(get_weights, l, "ffn"), small[l], ctab, stab, n_seq)
        saved.append(sv)
        weights.append(w)
    loss, dx, d_final = _loss_and_grad(x, a["loss_target"].reshape(t, d), _row(a["final_norm_g"]))
    gws, gss = [None] * depth, [None] * depth
    for l in reversed(range(depth)):
        dx, gws[l], gss[l] = _layer_bwd(dx, saved[l], weights[l], small[l], functools.partial(on_grads, l), ctab, stab, n_seq)

    n16 = depth * SSM_GROUPS
    halves = lambda k, h: jnp.concatenate([gss[l][k][:, h * SSM_LANES:(h + 1) * SSM_LANES].reshape(SSM_GROUPS, SSM_STATE)
                                           for l in range(depth)], axis=0)
    dlr, dli, ddt = _ssm_coeffs_bwd(a["ssm_lambda_re"].reshape(n16, SSM_STATE), a["ssm_lambda_im"].reshape(n16, SSM_STATE),
                                    a["ssm_log_dt"].reshape(n16, 1),
                                    (halves("a_row", 0), halves("a_row", 1), halves("f_row", 0), halves("f_row", 1)))
    bc = [_ssm_mats_t(gss[l]["bmat"], gss[l]["cmat"]) for l in range(depth)]
    gsmall = {k: jnp.stack([gss[l][k].reshape(a[k].shape[1:]) for l in range(depth)])
              for k in ("mix_norm_g", "b_gate", "attn_sinks", "ssm_d", "b_ssm_glu", "conv_dw_b", "conv_norm_g", "conv_norm_b",
                        "ffn_norm_g", "ple_norm_g")}
    gsmall["ssm_lambda_re"] = dlr.reshape(a["ssm_lambda_re"].shape)
    gsmall["ssm_lambda_im"] = dli.reshape(a["ssm_lambda_im"].shape)
    gsmall["ssm_log_dt"] = ddt.reshape(a["ssm_log_dt"].shape)
    for i, k in enumerate(("ssm_b_re", "ssm_b_im", "ssm_c_re", "ssm_c_im")):
        gsmall[k] = jnp.stack([bc[l][i] for l in range(depth)])
    gsmall["final_norm_g"] = d_final.reshape(a["final_norm_g"].shape)
    return loss, dx.reshape(n_seq, seq, d), gws, gsmall


def kernel(x, p, positions, mix_norm_g, w_in, b_gate, attn_sinks, w_attn_out, ssm_lambda_re, ssm_lambda_im, ssm_log_dt, ssm_b_re, ssm_b_im, ssm_c_re, ssm_c_im, ssm_d, w_ssm_glu, b_ssm_glu, conv_dw_w, conv_dw_b, conv_norm_g, conv_norm_b, w_conv_out, w_mix_out, ffn_norm_g, w_ffn_in, w_ffn_out, w_ple_in, ple_norm_g, w_ple_gate, final_norm_g, loss_target, m_mix_norm_g, m_w_in, m_b_gate, m_attn_sinks, m_w_attn_out, m_ssm_lambda_re, m_ssm_lambda_im, m_ssm_log_dt, m_ssm_b_re, m_ssm_b_im, m_ssm_c_re, m_ssm_c_im, m_ssm_d, m_w_ssm_glu, m_b_ssm_glu, m_conv_dw_w, m_conv_dw_b, m_conv_norm_g, m_conv_norm_b, m_w_conv_out, m_w_mix_out, m_ffn_norm_g, m_w_ffn_in, m_w_ffn_out, m_w_ple_in, m_ple_norm_g, m_w_ple_gate, m_final_norm_g, v_mix_norm_g, v_w_in, v_b_gate, v_attn_sinks, v_w_attn_out, v_ssm_lambda_re, v_ssm_lambda_im, v_ssm_log_dt, v_ssm_b_re, v_ssm_b_im, v_ssm_c_re, v_ssm_c_im, v_ssm_d, v_w_ssm_glu, v_b_ssm_glu, v_conv_dw_w, v_conv_dw_b, v_conv_norm_g, v_conv_norm_b, v_w_conv_out, v_w_mix_out, v_ffn_norm_g, v_w_ffn_in, v_w_ffn_out, v_w_ple_in, v_ple_norm_g, v_w_ple_gate, v_final_norm_g):
    a = dict(locals())
    depth = w_in.shape[0]
    layouts = {group: _layout(group) for group in GROUPS}
    shift = {"w_in": Z_SPLIT}
    conv_pad = ((0, 0), (0, CONV_HALO - CONV_K), (0, LANES - CONV_WIDTH // N_DEV))

    def packed_weights(l, group, fill):
        col, _, col_rows, row = layouts[group]
        pieces = [_slabs(a[name][l].astype(BF16), fill) for name in col]
        if GROUPS[group][2]:
            pieces.append(jnp.pad(jnp.stack(_split3(a["conv_dw_w"][l])), conv_pad).reshape(-1, LANES))
        return [_concat_padded(pieces, col_rows, 0)] + [a[name][l].astype(BF16) for name in row]

    gathers, tokens, fill = [], [], jnp.zeros((), BF16)
    for l in range(depth):
        started, token = _exchange_start(f"gather_start_{l}", "gather", [packed_weights(l, group, fill) for group in GROUPS])
        gathers.append(dict(zip(GROUPS, started, strict=True)))
        tokens.append(token[0:1, 0:1])
        fill = token[0, 0].astype(BF16)

    def get_weights(l, group, after):
        col, conv_base, _, row = layouts[group]
        slab8, *rows8 = _exchange_wait(f"gather_wait_{group}_{l}", gathers[l][group], after)
        w = {name: _unshard_cols("unshard_" + name, slab8, base, rows, cs, shift.get(name, 0))
             for name, (base, rows, cs) in col.items()}
        for (name, rs), gathered in zip(row.items(), rows8, strict=True):
            w[name] = gathered.reshape(N_DEV * rs, -1)
        if GROUPS[group][2]:
            conv = slab8[:, conv_base:conv_base + CONV_W_PIECES * CONV_HALO]
            conv = conv.reshape(N_DEV, CONV_W_PIECES, CONV_HALO, LANES)[:, :, :CONV_K, :CONV_WIDTH // N_DEV].astype(F32)
            w["conv_dw_w"] = jnp.sum(conv, axis=1).transpose(1, 0, 2).reshape(CONV_K, CONV_WIDTH)
        return w

    scatters = {}

    def on_grads(l, group, gw):
        col, _, col_rows, row = layouts[group]
        pieces = [_shard_cols("shard_" + name, gw[name], cs, shift.get(name, 0)) for name, (_, _, cs) in col.items()]
        if GROUPS[group][2]:
            conv = gw["conv_dw_w"].reshape(CONV_K, N_DEV, CONV_WIDTH // N_DEV).transpose(1, 0, 2).astype(BF16)
            pieces.append(jnp.pad(jnp.pad(conv, conv_pad), ((0, 0), (0, (CONV_W_PIECES - 1) * CONV_HALO), (0, 0))))
        by_shard = [gw[name].astype(BF16).reshape(N_DEV, rs, -1) for name, rs in row.items()]
        (scatters[l, group],), token = _exchange_start(
            f"grads_start_{group}_{l}", "scatter", [[_concat_padded(pieces, col_rows, 1)] + by_shard])
        return token

    local = dict(a, mix_norm_g=a["mix_norm_g"] + sum(tokens))
    loss, grad_x, _, gsmall = _local_step(local, get_weights, on_grads, depth)
    loss = lax.psum(loss, ("x", "y", "c"))

    shapes = [a[k].shape for k in REPLICATED]
    parts, = _all_gather("gather_small_grads", [_pack_small([gsmall[k] for k in REPLICATED])])
    small_state = [_pack_small([a[pre + k] for k in REPLICATED]) for pre in ("", "m_", "v_")]
    small_flat = _adamw_flat("adamw_replicated", parts, *small_state)
    small = [dict(zip(REPLICATED, _unpack_small(o, shapes), strict=True)) for o in small_flat]

    state = lambda name: (a[name], a["m_" + name], a["v_" + name])
    big, after = {}, small_flat[1]
    for group in ("ffn", "mix"):
        col, conv_base, col_rows, row = layouts[group]
        landed = [_exchange_wait(f"grads_wait_{group}_{l}", scatters[l, group], after) for l in range(depth)]
        landed_slab = jnp.concatenate([arrays[0] for arrays in landed], axis=1)
        big.update({name: _adamw_cols("adamw_" + name, landed_slab, base, col_rows, *state(name)) for name, (base, _, _) in col.items()})
        for i, (name, rs) in enumerate(row.items()):
            landed_rows = jnp.concatenate([arrays[1 + i] for arrays in landed], axis=1)
            big[name] = _adamw_rows("adamw_" + name, landed_rows, 0, rs, *state(name))
        if GROUPS[group][2]:
            big["conv_dw_w"] = _adamw_conv(landed_slab, conv_base, col_rows, *state("conv_dw_w"))
        after = big[next(iter(col))][1]

    def result(kind, name):
        if name in REPLICATED:
            return small[kind][name]
        return big[name][kind]

    return (loss, grad_x, *[result(kind, n) for kind in range(4) for n in WEIGHT_ORDER])
```

```python
import functools
import math

import numpy as np
import jax
import jax.numpy as jnp
from jax import lax
from jax.experimental import pallas as pl
from jax.experimental.pallas import tpu as pltpu

F32 = jnp.float32
BF16 = jnp.bfloat16
MXU_DTYPE = jnp.bfloat16
VMEM_LIMIT_BYTES = 56 * 2 ** 20
N_DEV = 8
LANES = 128

HEAD_DIM = 64
N_Q_HEADS = 8
N_KV_HEADS = 2
GQA_GROUP = 4
BLOCK = 128
ROPE_THETA = 500000.0
ROPE_DIM = 16
Q_WIDTH = 512
KV_WIDTH = 128
SSM_WIDTH = 256
SSM_GROUP = 16
SSM_GROUPS = 16
SSM_STATE = 64
SSM_LANES = SSM_GROUPS * SSM_STATE
CONV_WIDTH = 256
CONV_K = 31
CONV_HALO = 32
EPS = 1e-6
NEG_INF = -1e30
ADAM_LR, ADAM_B1, ADAM_B2, ADAM_EPS, ADAM_WD, ADAM_STEP = 0.001, 0.9, 0.999, 1e-08, 0.01, 10

ZG_W, ZQ_W, ZKV_W, ZS_W, ZC_W = 3072, 512, 256, 256, 512
ZQ_BLK, ZKV_BLK, ZS_BLK, ZC_BLK = 3072 // 512, 3584 // 256, 3840 // 256, 4096 // 512
Z_WIDTH = 4608
Z_SPLIT = 1536

SHARDED = (("w_in", 1024, 4608, 1), ("w_attn_out", 512, 1024, 1), ("w_ssm_glu", 256, 2048, 1),
           ("conv_dw_w", 31, 256, 1), ("w_conv_out", 256, 1024, 1), ("w_mix_out", 1024, 1024, 0),
           ("w_ffn_in", 1024, 5632, 1), ("w_ffn_out", 2816, 1024, 0), ("w_ple_in", 256, 1024, 1),
           ("w_ple_gate", 1024, 1024, 0))
GROUPS = {"mix": (("w_in", "w_attn_out", "w_ssm_glu", "w_conv_out"), ("w_mix_out",), True),
          "ffn": (("w_ffn_in", "w_ple_in"), ("w_ffn_out", "w_ple_gate"), False)}
SLAB_TILE = 256
FLAT_ROW_ALIGN = 1024
REPLICATED = ("mix_norm_g", "b_gate", "attn_sinks", "ssm_lambda_re", "ssm_lambda_im", "ssm_log_dt", "ssm_b_re",
              "ssm_b_im", "ssm_c_re", "ssm_c_im", "ssm_d", "b_ssm_glu", "conv_dw_b", "conv_norm_g", "conv_norm_b",
              "ffn_norm_g", "ple_norm_g", "final_norm_g")
WEIGHT_ORDER = ("mix_norm_g", "w_in", "b_gate", "attn_sinks", "w_attn_out", "ssm_lambda_re", "ssm_lambda_im",
                "ssm_log_dt", "ssm_b_re", "ssm_b_im", "ssm_c_re", "ssm_c_im", "ssm_d", "w_ssm_glu", "b_ssm_glu",
                "conv_dw_w", "conv_dw_b", "conv_norm_g", "conv_norm_b", "w_conv_out", "w_mix_out", "ffn_norm_g",
                "w_ffn_in", "w_ffn_out", "w_ple_in", "ple_norm_g", "w_ple_gate", "final_norm_g")


_ANY = pl.BlockSpec(memory_space=pl.ANY)


def _params(sem=None):
    return pltpu.CompilerParams(dimension_semantics=sem, vmem_limit_bytes=VMEM_LIMIT_BYTES)


def _pick(n, cands):
    for c in cands:
        if n % c == 0:
            return c
    return n


def _dot(a, b, dims):
    return lax.dot_general(a.astype(MXU_DTYPE), b.astype(MXU_DTYPE), (dims, ((), ())), preferred_element_type=F32)


def _dot_nn(a, b):
    return _dot(a, b, ((1,), (0,)))


def _dot_nt(a, b):
    return _dot(a, b, ((1,), (1,)))


def _dot_tn(a, b):
    return _dot(a, b, ((0,), (0,)))


@jax.custom_vjp
def _mm(x, w):
    return _dot_nn(x, w)


def _mm_f(x, w):
    return _dot_nn(x, w), (x, w)


def _mm_b(res, dy):
    x, w = res
    return _dot_nt(dy, w).astype(x.dtype), _dot_tn(x, dy).astype(w.dtype)


_mm.defvjp(_mm_f, _mm_b)


def _rms(x, g):
    return x * lax.rsqrt(jnp.mean(x * x, axis=-1, keepdims=True) + EPS) * g


ROW_TILES = (1024, 512, 256, 128)
COL_TILES = (1536, 1408, 1024, 512, 256, 128)


def _matmul_add(name, a, b, residual):
    t, k = a.shape
    n = b.shape[1]
    tm, tn = _pick(t, ROW_TILES), _pick(n, COL_TILES)

    def body(a_ref, b_ref, r_ref, o_ref):
        o_ref[...] = r_ref[...] + _dot_nn(a_ref[...], b_ref[...])

    tile = pl.BlockSpec((tm, tn), lambda i, j: (i, j))
    return pl.pallas_call(
        body, name=name, grid=(t // tm, n // tn), out_shape=jax.ShapeDtypeStruct((t, n), F32),
        in_specs=[pl.BlockSpec((tm, k), lambda i, j: (i, 0)), pl.BlockSpec((k, tn), lambda i, j: (0, j)), tile],
        out_specs=tile, compiler_params=_params(("parallel", "parallel")))(a, b, residual)


def _matmul_tn(name, a, b):
    t, m = a.shape
    n = b.shape[1]
    tm, tn, tt = _pick(m, COL_TILES[1:]), _pick(n, COL_TILES), _pick(t, ROW_TILES)

    def body(a_ref, b_ref, o_ref):
        @pl.when(pl.program_id(2) == 0)
        def _():
            o_ref[...] = jnp.zeros_like(o_ref)

        o_ref[...] += _dot_tn(a_ref[...], b_ref[...])

    return pl.pallas_call(
        body, name=name, grid=(m // tm, n // tn, t // tt), out_shape=jax.ShapeDtypeStruct((m, n), F32),
        in_specs=[pl.BlockSpec((tt, tm), lambda i, j, s: (s, i)), pl.BlockSpec((tt, tn), lambda i, j, s: (s, j))],
        out_specs=pl.BlockSpec((tm, tn), lambda i, j, s: (i, j)),
        compiler_params=_params(("parallel", "parallel", "arbitrary")))(a, b)


def _two_parts(n):
    cut = n // (2 * LANES) * LANES
    return [slice(0, n)] if cut == 0 else [slice(0, cut), slice(cut, n)]


def _in_proj(x, g, w):
    t, d = x.shape
    n = w.shape[1]
    tm, tn = _pick(t, ROW_TILES), _pick(n, COL_TILES)

    def body(x_ref, g_ref, w_ref, z_ref, h_ref):
        @pl.when(pl.program_id(1) == 0)
        def _():
            h_ref[...] = _rms(x_ref[...], g_ref[...]).astype(h_ref.dtype)

        z_ref[...] = _dot_nn(h_ref[...], w_ref[...])

    return pl.pallas_call(
        body, name="in_proj", grid=(t // tm, n // tn),
        out_shape=[jax.ShapeDtypeStruct((t, n), F32), jax.ShapeDtypeStruct((t, d), MXU_DTYPE)],
        in_specs=[pl.BlockSpec((tm, d), lambda i, j: (i, 0)), pl.BlockSpec((1, d), lambda i, j: (0, 0)),
                  pl.BlockSpec((d, tn), lambda i, j: (0, j))],
        out_specs=[pl.BlockSpec((tm, tn), lambda i, j: (i, j)), pl.BlockSpec((tm, d), lambda i, j: (i, 0))],
        compiler_params=_params(("parallel", "arbitrary")))(x, g, w)


def _in_proj_bwd(dz, w, x, dx_res, g):
    t, d = x.shape
    tm = _pick(t, ROW_TILES[1:])

    def body(dz_ref, w_ref, x_ref, r_ref, g_ref, dx_ref, dg_ref):
        _, vjp = jax.vjp(_norm_in_tile, x_ref[...], g_ref[...])
        dx, dg = vjp(_dot_nt(dz_ref[...], w_ref[...]))
        dx_ref[...] = dx + r_ref[...]

        @pl.when(pl.program_id(0) == 0)
        def _():
            dg_ref[...] = jnp.zeros_like(dg_ref)

        dg_ref[...] += dg

    rows = lambda width: pl.BlockSpec((tm, width), lambda i: (i, 0))
    whole = lambda a: pl.BlockSpec(a.shape, lambda i: (0, 0))
    return pl.pallas_call(
        body, name="in_proj_bwd", grid=(t // tm,), out_shape=[jax.ShapeDtypeStruct((t, d), F32), jax.ShapeDtypeStruct((1, d), F32)],
        in_specs=[rows(dz.shape[1]), whole(w), rows(d), rows(d), whole(g)], out_specs=[rows(d), whole(g)],
        compiler_params=_params(("arbitrary",)))(dz, w, x, dx_res, g)


def _ffn_in_act(hf, w_fi):
    t, k = hf.shape
    f = w_fi.shape[1] // 2
    tm, tf = _pick(t, ROW_TILES[1:]), _pick(f, COL_TILES)
    nf = f // tf

    def body(a_ref, wg_ref, wu_ref, g_ref, u_ref, act_ref):
        a = a_ref[...]
        for cols in _two_parts(tf):
            g, u = _dot_nn(a, wg_ref[:, cols]), _dot_nn(a, wu_ref[:, cols])
            g_ref[:, cols] = g.astype(g_ref.dtype)
            u_ref[:, cols] = u.astype(u_ref.dtype)
            act_ref[:, cols] = (jax.nn.silu(g) * u).astype(act_ref.dtype)

    out = pl.BlockSpec((tm, tf), lambda i, j: (i, j))
    return pl.pallas_call(
        body, name="ffn_in_act", grid=(t // tm, nf), out_shape=[jax.ShapeDtypeStruct((t, f), MXU_DTYPE)] * 3,
        in_specs=[pl.BlockSpec((tm, k), lambda i, j: (i, 0)), pl.BlockSpec((k, tf), lambda i, j: (0, j)),
                  pl.BlockSpec((k, tf), lambda i, j: (0, j + nf))],
        out_specs=[out, out, out], compiler_params=_params(("parallel", "parallel")))(hf, w_fi, w_fi)


def _ffn_mid_bwd(dffn, w_fo, gate, up):
    t, d = dffn.shape
    f = w_fo.shape[0]
    tm, tf = _pick(t, ROW_TILES[1:]), _pick(f, COL_TILES)

    def body(a_ref, w_ref, g_ref, u_ref, dg_ref, du_ref):
        a = a_ref[...]
        for cols in _two_parts(tf):
            dact = _dot_nt(a, w_ref[cols, :])
            g, u = g_ref[:, cols].astype(F32), u_ref[:, cols].astype(F32)
            sg = jax.nn.sigmoid(g)
            silu = g * sg
            dg_ref[:, cols] = (dact * u * (sg * (1.0 - silu) + silu)).astype(dg_ref.dtype)
            du_ref[:, cols] = (dact * silu).astype(du_ref.dtype)

    blk = pl.BlockSpec((tm, tf), lambda i, j: (i, j))
    return pl.pallas_call(
        body, name="ffn_mid_bwd", grid=(t // tm, f // tf), out_shape=[jax.ShapeDtypeStruct((t, f), MXU_DTYPE)] * 2,
        in_specs=[pl.BlockSpec((tm, d), lambda i, j: (i, 0)), pl.BlockSpec((tf, d), lambda i, j: (j, 0)), blk, blk],
        out_specs=[blk, blk], compiler_params=_params(("parallel", "parallel")))(dffn, w_fo, gate, up)


def _ffn_in_dx(dgate, dup, w_fi):
    t, f = dgate.shape
    d = w_fi.shape[0]
    tm = _pick(t, ROW_TILES[1:])

    def body(g_ref, u_ref, w_ref, o_ref):
        o_ref[...] = _dot_nt(g_ref[...], w_ref[:, :f]) + _dot_nt(u_ref[...], w_ref[:, f:])

    blk = pl.BlockSpec((tm, f), lambda i: (i, 0))
    return pl.pallas_call(
        body, name="ffn_in_dx", grid=(t // tm,), out_shape=jax.ShapeDtypeStruct((t, d), F32),
        in_specs=[blk, blk, pl.BlockSpec(w_fi.shape, lambda i: (0, 0))], out_specs=pl.BlockSpec((tm, d), lambda i: (i, 0)),
        compiler_params=_params(("parallel",)))(dgate, dup, w_fi)


def _token_call(name, fn, tile, tok_ins, consts, tok_outs, acc_outs, into=None):
    n_rows = tok_ins[0][0].shape[0]
    tile = min(tile, n_rows)
    n_ti, n_c = len(tok_ins), len(consts)
    n_in = n_ti + n_c + (into is not None)
    n_to = len(tok_outs) + (into is not None)

    def body(*refs):
        ins = [r[...] for r in refs[:n_ti + n_c]]
        outs, accs = fn(*ins)
        for r, v in zip(refs[n_in:n_in + n_to], outs, strict=True):
            r[...] = v.astype(r.dtype)
        first = pl.program_id(0) == 0
        for r, v in zip(refs[n_in + n_to:], accs, strict=True):
            @pl.when(first)
            def _(r=r):
                r[...] = jnp.zeros_like(r)

            r[...] += jnp.broadcast_to(v, r.shape).astype(F32)

    in_specs = [pl.BlockSpec((tile, w), functools.partial(lambda i, c, r: (i + r, c), c=cb, r=(first[0] // tile if first else 0)))
                for _, w, cb, *first in tok_ins]
    in_specs += [pl.BlockSpec(c.shape, lambda i: (0, 0)) for c in consts]
    out_shape = [jax.ShapeDtypeStruct((n_rows, w), dt) for w, dt in tok_outs]
    out_specs = [pl.BlockSpec((tile, w), lambda i: (i, 0)) for w, _ in tok_outs]
    operands = [spec[0] for spec in tok_ins] + list(consts)
    aliases = {}
    if into is not None:
        target, width, col_block = into
        in_specs.append(_ANY)
        operands.append(target)
        out_shape.append(jax.ShapeDtypeStruct(target.shape, target.dtype))
        out_specs.append(pl.BlockSpec((tile, width), lambda i: (i, col_block)))
        aliases = {n_in - 1: n_to - 1}
    out_shape += [jax.ShapeDtypeStruct(s, F32) for s in acc_outs]
    out_specs += [pl.BlockSpec(s, lambda i: (0, 0)) for s in acc_outs]
    res = pl.pallas_call(
        body, name=name, grid=(n_rows // tile,), out_shape=out_shape, in_specs=in_specs, out_specs=out_specs,
        input_output_aliases=aliases, compiler_params=_params(("arbitrary",)))(*operands)
    return res[:n_to], res[n_to:]


def _whole(a):
    return (a, a.shape[1], 0)


def _norm_in_tile(x, g):
    return _rms(x, g)


def _conv_post_tile(v, g, b):
    mu = jnp.mean(v, axis=-1, keepdims=True)
    var = jnp.mean(jnp.square(v - mu), axis=-1, keepdims=True)
    return jax.nn.silu((v - mu) * lax.rsqrt(var + EPS) * g + b)


def _branches_tile(ya, ys, v, gin, w_ao, w_sg, b_sg, ln_g, ln_b, w_co, b_gate):
    d = w_ao.shape[1]
    y_attn = _mm(ya, w_ao)
    pre = _mm(jax.nn.gelu(ys), w_sg) + b_sg
    y_ssm = pre[:, :d] * jax.nn.sigmoid(pre[:, d:])
    y_conv = _mm(_conv_post_tile(v, ln_g, ln_b), w_co)
    gates = jax.nn.sigmoid(gin + b_gate)
    return gates[:, :d] * y_attn + gates[:, d:2 * d] * y_ssm + gates[:, 2 * d:] * y_conv


def _merge_tile(x, ya, ys, v, gin, w_ao, w_sg, b_sg, ln_g, ln_b, w_co, b_gate, w_mo, g_ffn):
    x1 = x + _mm(_branches_tile(ya, ys, v, gin, w_ao, w_sg, b_sg, ln_g, ln_b, w_co, b_gate), w_mo)
    return x1, _rms(x1, g_ffn)


def _merge_bwd_tile(x1, ya, ys, v, gin, dx1, dhf, w_ao, w_sg, b_sg, ln_g, ln_b, w_co, b_gate, w_mo, g_ffn):
    _, norm_vjp = jax.vjp(_rms, x1, g_ffn)
    dx1_norm, dg_ffn = norm_vjp(dhf)
    dx1 = dx1 + dx1_norm
    merged, branch_vjp = jax.vjp(_branches_tile, ya, ys, v, gin, w_ao, w_sg, b_sg, ln_g, ln_b, w_co, b_gate)
    grads = branch_vjp(_dot_nt(dx1, w_mo))
    return [dx1, *grads[:4]], [*grads[4:], _dot_tn(merged, dx1), dg_ffn]


def _ple_tile(x2, p, w_pi, g_ple, w_pg):
    pre, e = _mm(_rms(x2, g_ple), w_pg), _mm(p, w_pi)
    return x2 + jax.nn.sigmoid(pre) * e, pre, e


def _ple_bwd_tile(x2, p, pre, e, dx3, w_pi, g_ple, w_pg):
    sig = jax.nn.sigmoid(pre)
    dpre = dx3 * e * sig * (1.0 - sig)
    hn, norm_vjp = jax.vjp(_rms, x2, g_ple)
    dx2_norm, dg_ple = norm_vjp(_dot_nt(dpre, w_pg))
    return dx3 + dx2_norm, [_dot_tn(p, dx3 * sig), dg_ple, _dot_tn(hn, dpre)]


def _f32s(vals):
    return [v.astype(F32) for v in vals]


def _rope_tables(positions, inv_lane):
    def fn(pos, inv):
        ang = pos.astype(F32) * inv
        j = lax.broadcasted_iota(jnp.int32, ang.shape, 1) % HEAD_DIM
        c = jnp.where(j < ROPE_DIM, jnp.cos(ang), 1.0)
        s = jnp.sin(ang)
        s = jnp.where(j < ROPE_DIM // 2, -s, jnp.where(j < ROPE_DIM, s, 0.0))
        return [c, s], []

    (c, s), _ = _token_call("rope_tables", fn, 1024, [_whole(positions.reshape(-1, 1))], [inv_lane],
                            [(LANES, F32), (LANES, F32)], [])
    return c, s


def _swap_halves(t):
    n = t.shape[1]
    j = lax.broadcasted_iota(jnp.int32, t.shape, 1) % HEAD_DIM
    lower = pltpu.roll(t, n - ROPE_DIM // 2, 1)
    upper = jnp.where(j < ROPE_DIM, pltpu.roll(t, ROPE_DIM // 2, 1), 0.0)
    return jnp.where(j < ROPE_DIM // 2, lower, upper)


def _rope(t, c, s):
    return t * c + _swap_halves(t) * s


def _rope_t(dt, c, s):
    return dt * c + _swap_halves(dt * s)


def _tile4(a):
    return jnp.concatenate([a] * (Q_WIDTH // LANES), axis=1)


def _attn_mask(n):
    qi = lax.broadcasted_iota(jnp.int32, (GQA_GROUP * BLOCK, 2 * BLOCK), 0) % BLOCK
    kj = lax.broadcasted_iota(jnp.int32, (GQA_GROUP * BLOCK, 2 * BLOCK), 1)
    dist = qi + BLOCK - kj
    return (dist >= 0) & (dist < BLOCK) & ((n > 0) | (kj >= BLOCK))


def _attn_specs(n_seq):
    own = lambda w, blk: pl.BlockSpec((n_seq, BLOCK, w), lambda n: (0, n, blk))
    prev = lambda w, blk: pl.BlockSpec((n_seq, BLOCK, w), lambda n: (0, jnp.maximum(n - 1, 0), blk))
    return [own(ZQ_W, ZQ_BLK), own(ZKV_W, ZKV_BLK), prev(ZKV_W, ZKV_BLK), own(LANES, 0), own(LANES, 0), prev(LANES, 0),
            prev(LANES, 0), pl.BlockSpec((1, N_Q_HEADS), lambda n: (0, 0))]


def _by_seq(a, n_seq):
    return a.reshape(n_seq, a.shape[0] // n_seq, a.shape[1])


ATTN_SCALE = HEAD_DIM ** -0.5


def _stack_heads(t, kh):
    return jnp.concatenate([t[:, (kh * GQA_GROUP + g) * HEAD_DIM:(kh * GQA_GROUP + g + 1) * HEAD_DIM]
                            for g in range(GQA_GROUP)], axis=0)


def _stack_sinks(sink, kh):
    return jnp.concatenate([jnp.broadcast_to(sink[:, kh * GQA_GROUP + g:kh * GQA_GROUP + g + 1], (BLOCK, 1))
                            for g in range(GQA_GROUP)], axis=0)


def _attn_band(b, q_ref, kv_ref, kvp_ref, c_ref, s_ref, cp_ref, sp_ref):
    c, s = c_ref[b], s_ref[b]
    q = _rope(q_ref[b], _tile4(c), _tile4(s)) * ATTN_SCALE
    kv, kvp = kv_ref[b], kvp_ref[b]
    k = _rope(kv[:, :KV_WIDTH], c, s)
    kp = _rope(kvp[:, :KV_WIDTH], cp_ref[b], sp_ref[b])
    kb = jnp.concatenate([kp, k], axis=0)
    vb = jnp.concatenate([kvp[:, KV_WIDTH:], kv[:, KV_WIDTH:]], axis=0)
    return q, kb, vb


def _attention_fwd(z, ctab, stab, sinks, n_seq):
    t = z.shape[0]
    seq = t // n_seq

    def body(q_ref, kv_ref, kvp_ref, c_ref, s_ref, cp_ref, sp_ref, sink_ref, o_ref, lse_ref):
        mask = _attn_mask(pl.program_id(0))
        sink = sink_ref[...]
        lane = lax.broadcasted_iota(jnp.int32, (BLOCK, N_Q_HEADS), 1)
        for b in range(n_seq):
            q, kb, vb = _attn_band(b, q_ref, kv_ref, kvp_ref, c_ref, s_ref, cp_ref, sp_ref)
            lse_all = jnp.zeros((BLOCK, N_Q_HEADS), F32)
            for kh in range(N_KV_HEADS):
                sc = jnp.where(mask, _dot_nt(_stack_heads(q, kh), kb[:, kh * HEAD_DIM:(kh + 1) * HEAD_DIM]), NEG_INF)
                sk = _stack_sinks(sink, kh)
                m = jnp.maximum(jnp.max(sc, axis=-1, keepdims=True), sk)
                pr = jnp.exp(sc - m)
                den = jnp.sum(pr, axis=-1, keepdims=True) + jnp.exp(sk - m)
                out = _dot_nn(pr * (1.0 / den), vb[:, kh * HEAD_DIM:(kh + 1) * HEAD_DIM])
                lse = m + jnp.log(den)
                for g in range(GQA_GROUP):
                    h = kh * GQA_GROUP + g
                    o_ref[b, :, h * HEAD_DIM:(h + 1) * HEAD_DIM] = out[g * BLOCK:(g + 1) * BLOCK].astype(o_ref.dtype)
                    lse_all = jnp.where(lane == h, lse[g * BLOCK:(g + 1) * BLOCK], lse_all)
            lse_ref[b] = lse_all

    rows = lambda w: pl.BlockSpec((n_seq, BLOCK, w), lambda n: (0, n, 0))
    z3, c3, s3 = _by_seq(z, n_seq), _by_seq(ctab, n_seq), _by_seq(stab, n_seq)
    ya, lse = pl.pallas_call(
        body, name="attn_fwd", grid=(seq // BLOCK,),
        out_shape=[jax.ShapeDtypeStruct((n_seq, seq, Q_WIDTH), MXU_DTYPE), jax.ShapeDtypeStruct((n_seq, seq, N_Q_HEADS), F32)],
        in_specs=_attn_specs(n_seq), out_specs=[rows(Q_WIDTH), rows(N_Q_HEADS)],
        compiler_params=_params(("parallel",)))(z3, z3, z3, c3, s3, c3, s3, sinks)
    return ya.reshape(t, Q_WIDTH), lse.reshape(t, N_Q_HEADS)


def _attention_bwd(z, ctab, stab, sinks, ya, lse, dya, dz, n_seq):
    t = z.shape[0]
    seq = t // n_seq

    def body(q_ref, kv_ref, kvp_ref, c_ref, s_ref, cp_ref, sp_ref, sink_ref, o_ref, lse_ref, do_ref, _,
             dq_ref, dkv_ref, dkvp_ref, dsink_ref):
        mask = _attn_mask(pl.program_id(0))
        sink = sink_ref[...]
        lane = lax.broadcasted_iota(jnp.int32, (1, N_Q_HEADS), 1)
        dsink = jnp.zeros((1, N_Q_HEADS), F32)
        for b in range(n_seq):
            q, kb, vb = _attn_band(b, q_ref, kv_ref, kvp_ref, c_ref, s_ref, cp_ref, sp_ref)
            lse_all = lse_ref[b]
            o = o_ref[b].astype(F32)
            do = do_ref[b].astype(F32)
            dq_parts = []
            dk_parts, dv_parts = [], []
            for kh in range(N_KV_HEADS):
                kbh = kb[:, kh * HEAD_DIM:(kh + 1) * HEAD_DIM]
                vbh = vb[:, kh * HEAD_DIM:(kh + 1) * HEAD_DIM]
                qs, dos = _stack_heads(q, kh), _stack_heads(do, kh)
                lse = jnp.concatenate([lse_all[:, kh * GQA_GROUP + g:kh * GQA_GROUP + g + 1] for g in range(GQA_GROUP)], axis=0)
                pr = jnp.exp(jnp.where(mask, _dot_nt(qs, kbh), NEG_INF) - lse)
                delta = jnp.sum(dos * _stack_heads(o, kh), axis=-1, keepdims=True)
                ds = pr * (_dot_nt(dos, vbh) - delta)
                dqs = _dot_nn(ds, kbh)
                dq_parts += [dqs[g * BLOCK:(g + 1) * BLOCK] for g in range(GQA_GROUP)]
                dk_parts.append(_dot_tn(ds, qs))
                dv_parts.append(_dot_tn(pr, dos))
                dsk = jnp.exp(_stack_sinks(sink, kh) - lse) * delta
                for g in range(GQA_GROUP):
                    dsink = dsink + jnp.where(lane == kh * GQA_GROUP + g, -jnp.sum(dsk[g * BLOCK:(g + 1) * BLOCK]), 0.0)
            c, s = c_ref[b], s_ref[b]
            dq_ref[b] = _rope_t(jnp.concatenate(dq_parts, axis=1) * ATTN_SCALE, _tile4(c), _tile4(s)).astype(dq_ref.dtype)
            dk = jnp.concatenate(dk_parts, axis=1)
            dv = jnp.concatenate(dv_parts, axis=1)
            dkv_ref[b, :, :KV_WIDTH] = _rope_t(dk[BLOCK:], c, s)
            dkv_ref[b, :, KV_WIDTH:] = dv[BLOCK:]
            dkvp_ref[b, :, :KV_WIDTH] = _rope_t(dk[:BLOCK], cp_ref[b], sp_ref[b])
            dkvp_ref[b, :, KV_WIDTH:] = dv[:BLOCK]

        @pl.when(pl.program_id(0) == 0)
        def _():
            dsink_ref[...] = jnp.zeros_like(dsink_ref)

        dsink_ref[...] += dsink

    rows = lambda w: pl.BlockSpec((n_seq, BLOCK, w), lambda n: (0, n, 0))
    by_seq = lambda a: _by_seq(a, n_seq)
    z3, c3, s3 = by_seq(z), by_seq(ctab), by_seq(stab)
    dz, dkv, dkvp, dsink = pl.pallas_call(
        body, name="attn_bwd", grid=(seq // BLOCK,),
        out_shape=[jax.ShapeDtypeStruct((n_seq, seq, Z_WIDTH), dz.dtype), jax.ShapeDtypeStruct((n_seq, seq, ZKV_W), F32),
                   jax.ShapeDtypeStruct((n_seq, seq, ZKV_W), F32), jax.ShapeDtypeStruct((1, N_Q_HEADS), F32)],
        in_specs=_attn_specs(n_seq) + [rows(Q_WIDTH), rows(N_Q_HEADS), rows(Q_WIDTH), _ANY],
        out_specs=[pl.BlockSpec((n_seq, BLOCK, ZQ_W), lambda n: (0, n, ZQ_BLK)), rows(ZKV_W), rows(ZKV_W),
                   pl.BlockSpec((1, N_Q_HEADS), lambda n: (0, 0))],
        input_output_aliases={11: 0},
        compiler_params=_params(("arbitrary",)))(z3, z3, z3, c3, s3, c3, s3, sinks, by_seq(ya), by_seq(lse), by_seq(dya), by_seq(dz))
    return dz.reshape(t, Z_WIDTH), dkv.reshape(t, ZKV_W), dkvp.reshape(t, ZKV_W), dsink


def _kv_combine(dkv, dkvp, dz, n_seq):
    t = dkv.shape[0]
    seq = t // n_seq
    rows = _pick(seq, (512, 256, 128))
    nt, per = seq // rows, rows // BLOCK
    n_blocks = t // BLOCK

    def body(dkv_ref, dkvp_ref, dkvn_ref, _, o_ref):
        nxt = jnp.where(pl.program_id(1) == nt - 1, 0.0, dkvn_ref[...])
        shifted = nxt if per == 1 else jnp.concatenate([dkvp_ref[BLOCK:, :], nxt], axis=0)
        o_ref[...] = (dkv_ref[...] + shifted).astype(o_ref.dtype)

    tile = pl.BlockSpec((rows, ZKV_W), lambda b, i: (b * nt + i, 0))
    return pl.pallas_call(
        body, name="kv_combine", grid=(n_seq, nt), out_shape=jax.ShapeDtypeStruct(dz.shape, dz.dtype),
        in_specs=[tile, tile,
                  pl.BlockSpec((BLOCK, ZKV_W), lambda b, i: (jnp.minimum((b * nt + i + 1) * per, n_blocks - 1), 0)), _ANY],
        out_specs=pl.BlockSpec((rows, ZKV_W), lambda b, i: (b * nt + i, ZKV_BLK)), input_output_aliases={3: 0},
        compiler_params=_params(("parallel", "parallel")))(dkv, dkvp, dkvp, dz)


def _ssm_coeff_tile(lam_re, lam_im, log_dt):
    lr = jnp.minimum(lam_re, -1e-4)
    dt = jnp.exp(log_dt)
    mag = jnp.exp(lr * dt)
    a_re = mag * jnp.cos(lam_im * dt)
    a_im = mag * jnp.sin(lam_im * dt)
    den = lr * lr + lam_im * lam_im
    x_re = a_re - 1.0
    f_re = (x_re * lr + a_im * lam_im) / den
    f_im = (a_im * lr - x_re * lam_im) / den
    return a_re, a_im, f_re, f_im


def _ssm_coeffs(lam_re, lam_im, log_dt):
    def body(lr_ref, li_ref, dt_ref, *o_refs):
        for r, v in zip(o_refs, _ssm_coeff_tile(lr_ref[...], li_ref[...], dt_ref[...]), strict=True):
            r[...] = v

    return pl.pallas_call(body, name="ssm_coeffs", out_shape=[jax.ShapeDtypeStruct(lam_re.shape, F32)] * 4)(
        lam_re, lam_im, log_dt)


def _ssm_coeffs_bwd(lam_re, lam_im, log_dt, cts):
    def body(lr_ref, li_ref, dt_ref, c0, c1, c2, c3, dlr_ref, dli_ref, ddt_ref):
        _, vjp = jax.vjp(_ssm_coeff_tile, lr_ref[...], li_ref[...], dt_ref[...])
        dlr, dli, ddt = vjp((c0[...], c1[...], c2[...], c3[...]))
        dlr_ref[...] = dlr
        dli_ref[...] = dli
        ddt_ref[...] = ddt

    return pl.pallas_call(
        body, name="ssm_coeffs_bwd",
        out_shape=[jax.ShapeDtypeStruct(lam_re.shape, F32)] * 2 + [jax.ShapeDtypeStruct(log_dt.shape, F32)])(
        lam_re, lam_im, log_dt, *cts)


def _ssm_chunk(t):
    return _pick(t, (256, 128))


def _ssm_fwd_chunk(t):
    return _pick(t, (512, 256, 128))


def _ssm_fwd(z, bmat, a_row, f_row, cmat, d_row, n_seq):
    t = z.shape[0]
    seq = t // n_seq
    lc = _ssm_fwd_chunk(seq)
    nc = seq // lc
    n2 = 2 * SSM_LANES

    re, im = pl.ds(0, SSM_LANES), pl.ds(SSM_LANES, SSM_LANES)

    def body(u_ref, b_ref, a_ref, f_ref, c_ref, d_ref, y_ref, s_ref, bu_ref, st_ref):
        @pl.when(pl.program_id(0) == 0)
        def _():
            st_ref[...] = jnp.zeros_like(st_ref)

        fr, fi = f_ref[:, :SSM_LANES], f_ref[:, SSM_LANES:]
        for b in range(n_seq):
            proj = _dot_nn(u_ref[b], b_ref[...])
            pr, pi = proj[:, :SSM_LANES], proj[:, SSM_LANES:]
            bu_ref[b, :, :SSM_LANES] = fr * pr - fi * pi
            bu_ref[b, :, SSM_LANES:] = fr * pi + fi * pr
        ar, ai = a_ref[:, :SSM_LANES], a_ref[:, SSM_LANES:]

        def step(i, carry):
            out = []
            for b in range(n_seq):
                sr, si = carry[2 * b], carry[2 * b + 1]
                nr = ar * sr - ai * si + bu_ref[b, pl.ds(i, 1), re]
                ni = ar * si + ai * sr + bu_ref[b, pl.ds(i, 1), im]
                s_ref[b, pl.ds(i, 1), re] = nr
                s_ref[b, pl.ds(i, 1), im] = ni
                out += [nr, ni]
            return tuple(out)

        carry = lax.fori_loop(0, lc, step, tuple(st_ref[b, 0:1, part] for b in range(n_seq) for part in (re, im)), unroll=8)
        for b in range(n_seq):
            st_ref[b, 0:1, re], st_ref[b, 0:1, im] = carry[2 * b], carry[2 * b + 1]
            y_ref[b] = _dot_nn(s_ref[b], c_ref[...]) + d_ref[...] * u_ref[b]

    const = lambda shape: pl.BlockSpec(shape, lambda c: (0, 0))
    rows = lambda w, cb: pl.BlockSpec((n_seq, lc, w), lambda c: (0, c, cb))
    ys, states = pl.pallas_call(
        body, name="ssm_fwd", grid=(nc,),
        out_shape=[jax.ShapeDtypeStruct((n_seq, seq, SSM_WIDTH), F32), jax.ShapeDtypeStruct((n_seq, seq, n2), F32)],
        in_specs=[rows(ZS_W, ZS_BLK), const((SSM_WIDTH, n2)), const((1, n2)), const((1, n2)), const((n2, SSM_WIDTH)),
                  const((1, SSM_WIDTH))],
        out_specs=[rows(SSM_WIDTH, 0), rows(n2, 0)],
        scratch_shapes=[pltpu.VMEM((n_seq, lc, n2), F32), pltpu.VMEM((n_seq, 8, n2), F32)],
        compiler_params=_params(("arbitrary",)))(_by_seq(z, n_seq), bmat, a_row, f_row, cmat, d_row)
    return ys.reshape(t, SSM_WIDTH), states.reshape(t, n2)


def _ssm_bwd(z, states, dy, bmat, a_row, f_row, cmat, d_row, dz, n_seq):
    t = z.shape[0]
    seq = t // n_seq
    lc = _ssm_chunk(seq)
    nc = seq // lc
    n2 = 2 * SSM_LANES

    def body(dy_ref, u_ref, s_ref, b_ref, a_ref, f_ref, c_ref, d_ref, _,
             du_ref, db_ref, dc_ref, da_ref, df_ref, dd_ref, g_ref, carry_ref):
        @pl.when(pl.program_id(0) == 0)
        def _():
            for r in (db_ref, dc_ref, da_ref, df_ref, dd_ref, carry_ref):
                r[...] = jnp.zeros_like(r)

        re, im = pl.ds(0, SSM_LANES), pl.ds(SSM_LANES, SSM_LANES)
        for b in range(n_seq):
            dy = dy_ref[b]
            g_ref[b, 0:lc, :] = _dot_nt(dy, c_ref[...])
            g_ref[b, lc:lc + 8, :] = carry_ref[b]
            dc_ref[...] += _dot_tn(s_ref[b], dy)
            dd_ref[...] += jnp.sum(dy * u_ref[b], axis=0, keepdims=True)
        ar, ai = a_ref[:, :SSM_LANES], a_ref[:, SSM_LANES:]

        def step(i, carry):
            r = lc - 1 - i
            out = []
            for b in range(n_seq):
                gr, gi = carry[2 * b], carry[2 * b + 1]
                nr = g_ref[b, pl.ds(r, 1), re] + ar * gr + ai * gi
                ni = g_ref[b, pl.ds(r, 1), im] - ai * gr + ar * gi
                g_ref[b, pl.ds(r, 1), re] = nr
                g_ref[b, pl.ds(r, 1), im] = ni
                out += [nr, ni]
            return tuple(out)

        carry = lax.fori_loop(0, lc, step, tuple(carry_ref[b, 0:1, part] for b in range(n_seq) for part in (re, im)), unroll=8)
        fr, fi = f_ref[:, :SSM_LANES], f_ref[:, SSM_LANES:]
        for b in range(n_seq):
            carry_ref[b, 0:1, re], carry_ref[b, 0:1, im] = carry[2 * b], carry[2 * b + 1]
            dy, u, st = dy_ref[b], u_ref[b], s_ref[b]
            sr, si = st[:, :SSM_LANES], st[:, SSM_LANES:]
            gnr, gni = g_ref[b, pl.ds(1, lc), re], g_ref[b, pl.ds(1, lc), im]
            da_ref[:, :SSM_LANES] += jnp.sum(gnr * sr + gni * si, axis=0, keepdims=True)
            da_ref[:, SSM_LANES:] += jnp.sum(gni * sr - gnr * si, axis=0, keepdims=True)
            gr_all, gi_all = g_ref[b, 0:lc, :SSM_LANES], g_ref[b, 0:lc, SSM_LANES:]
            proj = _dot_nn(u, b_ref[...])
            pr, pi = proj[:, :SSM_LANES], proj[:, SSM_LANES:]
            df_ref[:, :SSM_LANES] += jnp.sum(gr_all * pr + gi_all * pi, axis=0, keepdims=True)
            df_ref[:, SSM_LANES:] += jnp.sum(gi_all * pr - gr_all * pi, axis=0, keepdims=True)
            dproj = jnp.concatenate([fr * gr_all + fi * gi_all, fr * gi_all - fi * gr_all], axis=1).astype(MXU_DTYPE)
            du_ref[b] = (_dot_nt(dproj, b_ref[...]) + d_ref[...] * dy).astype(du_ref.dtype)
            db_ref[...] += _dot_tn(u, dproj)

    const = lambda shape: pl.BlockSpec(shape, lambda c: (0, 0))
    rows = lambda w, cb: pl.BlockSpec((n_seq, lc, w), lambda c: (0, nc - 1 - c, cb))
    by_seq = lambda a: _by_seq(a, n_seq)
    dz, *sums = pl.pallas_call(
        body, name="ssm_bwd", grid=(nc,),
        out_shape=[jax.ShapeDtypeStruct((n_seq, seq, Z_WIDTH), dz.dtype), jax.ShapeDtypeStruct((SSM_WIDTH, n2), F32),
                   jax.ShapeDtypeStruct((n2, SSM_WIDTH), F32), jax.ShapeDtypeStruct((1, n2), F32),
                   jax.ShapeDtypeStruct((1, n2), F32), jax.ShapeDtypeStruct((1, SSM_WIDTH), F32)],
        in_specs=[rows(SSM_WIDTH, 0), rows(ZS_W, ZS_BLK), rows(n2, 0), const((SSM_WIDTH, n2)), const((1, n2)),
                  const((1, n2)), const((n2, SSM_WIDTH)), const((1, SSM_WIDTH)), _ANY],
        out_specs=[rows(ZS_W, ZS_BLK), const((SSM_WIDTH, n2)), const((n2, SSM_WIDTH)), const((1, n2)), const((1, n2)),
                   const((1, SSM_WIDTH))],
        input_output_aliases={8: 0},
        scratch_shapes=[pltpu.VMEM((n_seq, lc + 8, n2), F32), pltpu.VMEM((n_seq, 8, n2), F32)],
        compiler_params=_params(("arbitrary",)))(by_seq(dy), by_seq(z), by_seq(states), bmat, a_row, f_row, cmat, d_row, by_seq(dz))
    return (dz.reshape(t, Z_WIDTH), *sums)


def _conv_chunk(t):
    return _pick(t, (512, 256, 128))


def _glu(c):
    return c[:, :CONV_WIDTH] * jax.nn.sigmoid(c[:, CONV_WIDTH:])


def _conv_specs(lc, nc):
    per = lc // CONV_HALO
    return [pl.BlockSpec((lc, ZC_W), lambda b, c: (b * nc + c, ZC_BLK)),
            pl.BlockSpec((CONV_HALO, ZC_W), lambda b, c: (jnp.maximum((b * nc + c) * per - 1, 0), ZC_BLK))]


def _conv_fill(c_ref, cp_ref, ue_ref, lc):
    ue_ref[0:CONV_HALO, :] = jnp.where(pl.program_id(1) > 0, _glu(cp_ref[...]), 0.0)
    ue_ref[CONV_HALO:CONV_HALO + lc, :] = _glu(c_ref[...])


CONV_ROWS = 64
SUBLANES = 8
CONV_SHIFT_ROWS = CONV_HALO - SUBLANES


def _shifted_copies(src_ref, sh_ref, lc):
    for b in range(1, SUBLANES):
        sh_ref[b - 1, :, :] = src_ref[pl.ds(b, lc + CONV_SHIFT_ROWS), :]


def _tap_rows(src_ref, sh_ref, offset, r0):
    b = offset % SUBLANES
    rows = pl.ds(r0 + offset - b, CONV_ROWS)
    return src_ref[rows, :] if b == 0 else sh_ref[b - 1, rows, :]


def _conv_apply(ue_ref, ush_ref, w_ref, b_ref, o_ref, lc):
    for r0 in range(0, lc, CONV_ROWS):
        acc = jnp.zeros((CONV_ROWS, CONV_WIDTH), F32) + b_ref[...]
        for k in range(CONV_K):
            acc = acc + w_ref[k:k + 1, :] * _tap_rows(ue_ref, ush_ref, k + CONV_HALO - CONV_K + 1, r0)
        o_ref[r0:r0 + CONV_ROWS, :] = acc


def _conv_fwd(z, dw_w, dw_b, n_seq):
    t = z.shape[0]
    seq = t // n_seq
    lc = _conv_chunk(seq)
    nc = seq // lc

    def body(c_ref, cp_ref, w_ref, b_ref, o_ref, ue_ref, ush_ref):
        _conv_fill(c_ref, cp_ref, ue_ref, lc)
        _shifted_copies(ue_ref, ush_ref, lc)
        _conv_apply(ue_ref, ush_ref, w_ref, b_ref, o_ref, lc)

    const = lambda a: pl.BlockSpec(a.shape, lambda b, c: (0, 0))
    return pl.pallas_call(
        body, name="conv_fwd", grid=(n_seq, nc), out_shape=jax.ShapeDtypeStruct((t, CONV_WIDTH), F32),
        in_specs=_conv_specs(lc, nc) + [const(dw_w), const(dw_b)],
        out_specs=pl.BlockSpec((lc, CONV_WIDTH), lambda b, c: (b * nc + c, 0)),
        scratch_shapes=[pltpu.VMEM((CONV_HALO + lc, CONV_WIDTH), F32),
                        pltpu.VMEM((SUBLANES - 1, lc + CONV_SHIFT_ROWS, CONV_WIDTH), F32)],
        compiler_params=_params(("parallel", "parallel")))(z, z, dw_w, dw_b)


def _conv_bwd_taps(z, dv, dw_w, dz, n_seq):
    t = z.shape[0]
    seq = t // n_seq
    lc = _conv_chunk(seq)
    nc = seq // lc
    per = lc // CONV_HALO
    n_halo = t // CONV_HALO

    def body(c_ref, cp_ref, dv_ref, dvn_ref, w_ref, _, dc_ref, dw_ref, ue_ref, dve_ref, ush_ref, dsh_ref):
        @pl.when((pl.program_id(0) == 0) & (pl.program_id(1) == 0))
        def _():
            dw_ref[...] = jnp.zeros_like(dw_ref)

        _conv_fill(c_ref, cp_ref, ue_ref, lc)
        dv = dv_ref[...]
        dve_ref[0:lc, :] = dv
        dve_ref[lc:lc + CONV_HALO, :] = jnp.where(pl.program_id(1) < nc - 1, dvn_ref[...], 0.0)
        _shifted_copies(ue_ref, ush_ref, lc)
        _shifted_copies(dve_ref, dsh_ref, lc)
        dw_ref[CONV_K:CONV_K + 1, :] += jnp.sum(dv, axis=0, keepdims=True)
        for r0 in range(0, lc, CONV_ROWS):
            rows = pl.ds(r0, CONV_ROWS)
            dv_rows = dv_ref[rows, :]
            du = jnp.zeros((CONV_ROWS, CONV_WIDTH), F32)
            for k in range(CONV_K):
                du = du + w_ref[k:k + 1, :] * _tap_rows(dve_ref, dsh_ref, CONV_K - 1 - k, r0)
                taps = _tap_rows(ue_ref, ush_ref, k + CONV_HALO - CONV_K + 1, r0)
                dw_ref[k:k + 1, :] += jnp.sum(dv_rows * taps, axis=0, keepdims=True)
            c = c_ref[rows, :]
            a, sg = c[:, :CONV_WIDTH], jax.nn.sigmoid(c[:, CONV_WIDTH:])
            dc_ref[rows, :CONV_WIDTH] = (du * sg).astype(dc_ref.dtype)
            dc_ref[rows, CONV_WIDTH:] = (du * a * sg * (1.0 - sg)).astype(dc_ref.dtype)

    return pl.pallas_call(
        body, name="conv_bwd_taps", grid=(n_seq, nc),
        out_shape=[jax.ShapeDtypeStruct(dz.shape, dz.dtype), jax.ShapeDtypeStruct((CONV_HALO, CONV_WIDTH), F32)],
        in_specs=_conv_specs(lc, nc) + [
            pl.BlockSpec((lc, CONV_WIDTH), lambda b, c: (b * nc + c, 0)),
            pl.BlockSpec((CONV_HALO, CONV_WIDTH), lambda b, c: (jnp.minimum((b * nc + c + 1) * per, n_halo - 1), 0)),
            pl.BlockSpec(dw_w.shape, lambda b, c: (0, 0)), _ANY],
        out_specs=[pl.BlockSpec((lc, ZC_W), lambda b, c: (b * nc + c, ZC_BLK)),
                   pl.BlockSpec((CONV_HALO, CONV_WIDTH), lambda b, c: (0, 0))],
        input_output_aliases={5: 0},
        scratch_shapes=[pltpu.VMEM((CONV_HALO + lc, CONV_WIDTH), F32), pltpu.VMEM((lc + CONV_HALO, CONV_WIDTH), F32)]
        + [pltpu.VMEM((SUBLANES - 1, lc + CONV_SHIFT_ROWS, CONV_WIDTH), F32)] * 2,
        compiler_params=_params(("arbitrary", "arbitrary")))(z, z, dv, dv, dw_w, dz)


def _row(v):
    return v.reshape(1, -1)


def _ssm_mats(b_re, b_im, c_re, c_im):
    eye = jnp.eye(SSM_GROUPS, dtype=bool)
    bm = jnp.stack([b_re, b_im]).transpose(1, 3, 0, 2)[:, :, :, None, :]
    bmat = jnp.where(eye[:, None, None, :, None], bm, 0.0).reshape(SSM_WIDTH, 2 * SSM_LANES)
    cm = jnp.stack([c_re, -c_im]).transpose(0, 1, 3, 2)[:, :, :, None, :]
    cmat = jnp.where(eye[None, :, None, :, None], cm, 0.0).reshape(2 * SSM_LANES, SSM_WIDTH)
    return bmat.astype(MXU_DTYPE), cmat.astype(MXU_DTYPE)


def _ssm_mats_t(dbmat, dcmat):
    eye = jnp.eye(SSM_GROUPS, dtype=bool)
    db = dbmat.reshape(SSM_GROUPS, SSM_GROUP, 2, SSM_GROUPS, SSM_STATE)
    db = jnp.sum(jnp.where(eye[:, None, None, :, None], db, 0.0), axis=3).transpose(2, 0, 3, 1)
    dc = dcmat.reshape(2, SSM_GROUPS, SSM_STATE, SSM_GROUPS, SSM_GROUP)
    dc = jnp.sum(jnp.where(eye[None, :, None, :, None], dc, 0.0), axis=3).transpose(0, 1, 3, 2)
    return db[0], db[1], dc[0], -dc[1]


def _layer_fwd(x, p, w, get_ffn_weights, sp, ctab, stab, n_seq):
    z, h = _in_proj(x, sp["mix_norm_g"], w["w_in"])
    ya, lse = _attention_fwd(z, ctab, stab, sp["attn_sinks"], n_seq)
    ys, states = _ssm_fwd(z, sp["bmat"], sp["a_row"], sp["f_row"], sp["cmat"], sp["ssm_d"], n_seq)
    v = _conv_fwd(z, w["conv_dw_w"], sp["conv_dw_b"], n_seq)
    merge_consts = [w["w_attn_out"], w["w_ssm_glu"], sp["b_ssm_glu"], sp["conv_norm_g"], sp["conv_norm_b"], w["w_conv_out"],
                    sp["b_gate"], w["w_mix_out"], sp["ffn_norm_g"]]
    (x1, hf), _ = _token_call("merge", lambda *a: (list(_merge_tile(*_f32s(a))), []), 512,
                              [_whole(x), _whole(ya), _whole(ys), _whole(v), (z, ZG_W, 0)], merge_consts,
                              [(x.shape[1], F32), (x.shape[1], MXU_DTYPE)], [])
    w = dict(w, **get_ffn_weights(x1))
    gate, up, act = _ffn_in_act(hf, w["w_ffn_in"])
    x2 = _matmul_add("mm_ffn_out", act, w["w_ffn_out"], x1)
    d = x.shape[1]
    (x3, ple_pre, ple_e), _ = _token_call(
        "ple", lambda *a: (list(_ple_tile(*a)), []), 512, [_whole(x2), p],
        [w["w_ple_in"], sp["ple_norm_g"], w["w_ple_gate"]], [(d, F32), (d, MXU_DTYPE), (d, MXU_DTYPE)], [])
    saved = dict(h=h, z=z, ya=ya, lse=lse, ys=ys, states=states, v=v, x1=x1, hf=hf, gate=gate, up=up, act=act, x2=x2, p=p,
                 ple_pre=ple_pre, ple_e=ple_e, x=x)
    return x3, saved, w


def _layer_bwd(dx3, sv, w, sp, on_grads, ctab, stab, n_seq):
    d = dx3.shape[1]
    gw, gs = {}, {}

    def ple_bwd(*a):
        dx2, accs = _ple_bwd_tile(*_f32s(a))
        return [dx2, dx2], accs

    (dx2, dffn), (gw["w_ple_in"], gs["ple_norm_g"], gw["w_ple_gate"]) = _token_call(
        "ple_bwd", ple_bwd, 512, [_whole(sv["x2"]), sv["p"], _whole(sv["ple_pre"]), _whole(sv["ple_e"]), _whole(dx3)],
        [w["w_ple_in"], sp["ple_norm_g"], w["w_ple_gate"]], [(d, F32), (d, MXU_DTYPE)],
        [w["w_ple_in"].shape, (1, d), w["w_ple_gate"].shape])

    dgate, dup = _ffn_mid_bwd(dffn, w["w_ffn_out"], sv["gate"], sv["up"])
    gw["w_ffn_out"] = _matmul_tn("mm_ffn_out_dw", sv["act"], dffn)
    dhf = _ffn_in_dx(dgate, dup, w["w_ffn_in"])
    gw["w_ffn_in"] = jnp.concatenate([_matmul_tn("mm_ffn_gate_dw", sv["hf"], dgate), _matmul_tn("mm_ffn_up_dw", sv["hf"], dup)],
                                     axis=1)

    token = on_grads("ffn", gw)

    def merge_bwd(*a):
        return _merge_bwd_tile(*_f32s(a))

    b_gate = sp["b_gate"] if token is None else sp["b_gate"] + token[0:1, 0:1]
    merge_consts = [w["w_attn_out"], w["w_ssm_glu"], sp["b_ssm_glu"], sp["conv_norm_g"], sp["conv_norm_b"], w["w_conv_out"],
                    b_gate, w["w_mix_out"], sp["ffn_norm_g"]]
    dz = lax.empty(sv["z"].shape, MXU_DTYPE)
    (dx_res, dya, dys, dv, dz), macc = _token_call(
        "merge_bwd", merge_bwd, 256,
        [_whole(sv["x1"]), _whole(sv["ya"]), _whole(sv["ys"]), _whole(sv["v"]), (sv["z"], ZG_W, 0), _whole(dx2), _whole(dhf)],
        merge_consts, [(d, F32), (Q_WIDTH, MXU_DTYPE), (SSM_WIDTH, F32), (CONV_WIDTH, F32)],
        [c.shape for c in merge_consts], into=(dz, ZG_W, 0))
    (gw["w_attn_out"], gw["w_ssm_glu"], gs["b_ssm_glu"], gs["conv_norm_g"], gs["conv_norm_b"], gw["w_conv_out"], gs["b_gate"],
     gw["w_mix_out"], gs["ffn_norm_g"]) = macc

    dz, dw_taps = _conv_bwd_taps(sv["z"], dv, w["conv_dw_w"], dz, n_seq)
    gw["conv_dw_w"], gs["conv_dw_b"] = dw_taps[:CONV_K], dw_taps[CONV_K:]

    dz, gs["bmat"], gs["cmat"], gs["a_row"], gs["f_row"], gs["ssm_d"] = _ssm_bwd(
        sv["z"], sv["states"], dys, sp["bmat"], sp["a_row"], sp["f_row"], sp["cmat"], sp["ssm_d"], dz, n_seq)

    dz, dkv, dkvp, gs["attn_sinks"] = _attention_bwd(sv["z"], ctab, stab, sp["attn_sinks"], sv["ya"], sv["lse"], dya, dz, n_seq)
    dz = _kv_combine(dkv, dkvp, dz, n_seq)
    gw["w_in"] = _matmul_tn("mm_in_dw", sv["h"], dz)
    token = on_grads("mix", gw)
    g_in = sp["mix_norm_g"] if token is None else sp["mix_norm_g"] + token[0:1, 0:1]
    dx, gs["mix_norm_g"] = _in_proj_bwd(dz, w["w_in"], sv["x"], dx_res, g_in)
    return dx, gw, gs


def _loss_and_grad(x, target, g):
    def fn(x, tgt, g):
        def f(x, g):
            err = _rms(x, g) - tgt
            return 0.5 * jnp.mean(err * err, axis=-1, keepdims=True)

        per_token, vjp = jax.vjp(f, x, g)
        dx, dg = vjp(jnp.ones_like(per_token))
        return [dx], [jnp.sum(per_token, axis=0, keepdims=True), dg]

    (dx,), (loss, dg) = _token_call("loss", fn, 512, [_whole(x), _whole(target)], [g], [(x.shape[1], F32)],
                                    [(8, LANES), (1, x.shape[1])])
    return loss[0, 0], dx, dg


def _mesh_place():
    return lax.axis_index("x"), lax.axis_index("y"), lax.axis_index("c")


def _flip(v, bit):
    return 1 - v if bit else v


_MESH = pl.DeviceIdType.MESH


def _all_gather(name, xs):
    n = len(xs)

    def body(*refs):
        x_refs, out_refs = refs[:n], refs[n:2 * n]
        send_sems, recv_sems, local_sems = refs[2 * n:]
        mx, my, mc = _mesh_place()
        me, sibling = (mx, my, mc), (mx, my, 1 - mc)
        chips = [(1 - mx, my), (mx, 1 - my), (1 - mx, 1 - my)]

        def slot(a, px, py, pc):
            return out_refs[a].at[4 * px + 2 * py + pc]

        def copy(a, k, block, to, src=None):
            return pltpu.make_async_remote_copy(
                src_ref=slot(a, *block) if src is None else src, dst_ref=slot(a, *block), send_sem=send_sems.at[7 * a + k],
                recv_sem=recv_sems.at[7 * a + k], device_id=to, device_id_type=_MESH)

        mine = [pltpu.make_async_copy(x_refs[a], slot(a, *me), local_sems.at[a]) for a in range(n)]
        for cp in mine:
            cp.start()
        first = [copy(a, 0, me, sibling, src=x_refs[a]) for a in range(n)]
        first += [copy(a, 1 + j, me, (*chip, mc), src=x_refs[a]) for j, chip in enumerate(chips) for a in range(n)]
        for cp in first:
            cp.start()
        passed = []
        for j, chip in enumerate(chips):
            for a in range(n):
                copy(a, 1 + j, (*chip, mc), me).wait_recv()
                passed.append(copy(a, 4 + j, (*chip, mc), sibling))
                passed[-1].start()
        for a in range(n):
            copy(a, 0, sibling, me).wait_recv()
            for j, chip in enumerate(chips):
                copy(a, 4 + j, (*chip, 1 - mc), me).wait_recv()
        for cp in first + passed:
            cp.wait_send()
        for cp in mine:
            cp.wait()

    return pl.pallas_call(
        body, name=name, out_shape=[jax.ShapeDtypeStruct((N_DEV,) + x.shape, x.dtype) for x in xs], in_specs=[_ANY] * n,
        out_specs=[_ANY] * n,
        scratch_shapes=[pltpu.SemaphoreType.DMA((7 * n,)), pltpu.SemaphoreType.DMA((7 * n,)), pltpu.SemaphoreType.DMA((n,))])(*xs)


def _direct_copies(kind, src_refs, land_refs, send_sems, recv_sems, local_sems):
    mx, my, mc = _mesh_place()
    me = 4 * mx + 2 * my + mc
    n = len(src_refs)
    own = [pltpu.make_async_copy(src_refs[a] if kind == "gather" else src_refs[a].at[me], land_refs[a].at[me], local_sems.at[a])
           for a in range(n)]
    copies = []
    for rel in range(1, N_DEV):
        px, py, pc = _flip(mx, rel & 4), _flip(my, rel & 2), _flip(mc, rel & 1)
        for a in range(n):
            src = src_refs[a] if kind == "gather" else src_refs[a].at[4 * px + 2 * py + pc]
            copies.append(pltpu.make_async_remote_copy(
                src_ref=src, dst_ref=land_refs[a].at[me], send_sem=send_sems.at[7 * a + rel - 1],
                recv_sem=recv_sems.at[7 * a + rel - 1], device_id=(px, py, pc), device_id_type=_MESH))
    return copies, own


_HBM = pl.BlockSpec(memory_space=pltpu.HBM)
_SEM = pl.BlockSpec(memory_space=pltpu.SEMAPHORE)
_DATAFLOW = pltpu.SideEffectType.DATAFLOW_SIDE_EFFECTING


def _exchange_start(name, kind, groups):
    sizes = [len(g) for g in groups]
    srcs = [s for g in groups for s in g]
    lands = [lax.empty(((N_DEV,) + s.shape) if kind == "gather" else s.shape, s.dtype) for s in srcs]
    n, n_g = len(srcs), len(groups)
    first = [sum(sizes[:g]) for g in range(n_g)]

    def body(*refs):
        src_refs, land_refs, sems = refs[:n], refs[n:2 * n], refs[2 * n:2 * n + 3 * n_g]
        for g in range(n_g):
            span = slice(first[g], first[g] + sizes[g])
            copies, own = _direct_copies(kind, src_refs[span], land_refs[span], *sems[3 * g:3 * g + 3])
            for cp in own + copies:
                cp.start()
        refs[-1][...] = jnp.zeros_like(refs[-1])

    hbm = lambda a: pltpu.with_memory_space_constraint(a, pltpu.HBM)
    sem_shapes = [pltpu.SemaphoreType.DMA((k * m,)) for m in sizes for k in (7, 7, 1)]
    out = pl.pallas_call(
        body, name=name,
        out_shape=sem_shapes + [pltpu.HBM(a.shape, a.dtype) for a in srcs + lands] + [jax.ShapeDtypeStruct((8, LANES), F32)],
        in_specs=[_HBM] * (2 * n), out_specs=[_SEM] * (3 * n_g) + [_HBM] * (2 * n) + [pl.BlockSpec(memory_space=pltpu.VMEM)],
        input_output_aliases={i: 3 * n_g + i for i in range(2 * n)},
        compiler_params=pltpu.CompilerParams(has_side_effects=_DATAFLOW))(*[hbm(a) for a in srcs + lands])
    sems, arrays = out[:3 * n_g], out[3 * n_g:-1]
    started = [(kind, (*sems[3 * g:3 * g + 3], *arrays[first[g]:first[g] + sizes[g]],
                       *arrays[n + first[g]:n + first[g] + sizes[g]])) for g in range(n_g)]
    return started, out[-1]


def _exchange_wait(name, started, after):
    kind, (send_sems, recv_sems, local_sems, *arrays) = started
    n = len(arrays) // 2

    def body(*refs):
        src_refs, land_refs = refs[:n], refs[n:2 * n]
        copies, own = _direct_copies(kind, src_refs, land_refs, *refs[2 * n:2 * n + 3])
        for cp in copies + own:
            cp.wait()

    out = pl.pallas_call(
        body, name=name, out_shape=[pltpu.HBM(a.shape, a.dtype) for a in arrays],
        in_specs=[_HBM] * (2 * n) + [_SEM] * 3 + [_ANY], out_specs=[_HBM] * (2 * n),
        input_output_aliases={i: i for i in range(2 * n)},
        compiler_params=pltpu.CompilerParams(has_side_effects=_DATAFLOW))(*arrays, send_sems, recv_sems, local_sems, after)
    return out[n:]


def _adamw_math(g, w, m, v):
    m2 = ADAM_B1 * m + (1.0 - ADAM_B1) * g
    v2 = ADAM_B2 * v + (1.0 - ADAM_B2) * jnp.square(g)
    m_hat = m2 / (1.0 - ADAM_B1 ** ADAM_STEP)
    v_hat = v2 / (1.0 - ADAM_B2 ** ADAM_STEP)
    return g, -ADAM_LR * (m_hat / (jnp.sqrt(v_hat) + ADAM_EPS) + ADAM_WD * w), m2, v2


def _sum_blocks(ref):
    g = ref[0].astype(F32)
    for j in range(1, N_DEV):
        g = g + ref[j].astype(F32)
    return g


def _adamw_flat(name, parts, w, m, v):
    r = w.shape[0]
    tile = _pick(r, (1024, 512, 256, 128, 8))

    def body(p_ref, w_ref, m_ref, v_ref, *o_refs):
        for o, val in zip(o_refs, _adamw_math(_sum_blocks(p_ref), w_ref[...], m_ref[...], v_ref[...]), strict=True):
            o[...] = val

    flat = pl.BlockSpec((tile, LANES), lambda i: (i, 0))
    return pl.pallas_call(
        body, name=name, grid=(r // tile,), out_shape=[jax.ShapeDtypeStruct((r, LANES), F32)] * 4,
        in_specs=[pl.BlockSpec((N_DEV, tile, LANES), lambda i: (0, i, 0)), flat, flat, flat], out_specs=[flat] * 4,
        compiler_params=_params(("parallel",)))(parts, w, m, v)


def _adamw_cols(name, landed, base, stride, w, m, v):
    depth, rows, cs = w.shape
    n_slab = -(-cs // LANES)
    tr = _pick(rows, (SLAB_TILE,))

    def body(*refs):
        slabs, (w_ref, m_ref, v_ref), o_refs = refs[:n_slab], refs[n_slab:n_slab + 3], refs[n_slab + 3:]
        g = jnp.concatenate([_sum_blocks(s)[:, :min(LANES, cs - LANES * k)] for k, s in enumerate(slabs)], axis=1)
        for o, val in zip(o_refs, _adamw_math(g, w_ref[...], m_ref[...], v_ref[...]), strict=True):
            o[...] = val

    slab = lambda k: pl.BlockSpec((N_DEV, tr, LANES), lambda l, i: (0, (l * stride + base + k * rows) // tr + i, 0))
    nat = pl.BlockSpec((None, tr, cs), lambda l, i: (l, i, 0))
    return pl.pallas_call(
        body, name=name, grid=(depth, rows // tr), out_shape=[jax.ShapeDtypeStruct(w.shape, F32)] * 4,
        in_specs=[slab(k) for k in range(n_slab)] + [nat] * 3, out_specs=[nat] * 4,
        compiler_params=_params(("parallel", "parallel")))(*[landed] * n_slab, w, m, v)


def _adamw_rows(name, landed, base, stride, w, m, v):
    depth, rs, width = w.shape
    tr = math.gcd(rs, base, stride)

    def body(p_ref, w_ref, m_ref, v_ref, *o_refs):
        for o, val in zip(o_refs, _adamw_math(_sum_blocks(p_ref), w_ref[...], m_ref[...], v_ref[...]), strict=True):
            o[...] = val

    nat = pl.BlockSpec((None, tr, width), lambda l, i: (l, i, 0))
    return pl.pallas_call(
        body, name=name, grid=(depth, rs // tr), out_shape=[jax.ShapeDtypeStruct(w.shape, F32)] * 4,
        in_specs=[pl.BlockSpec((N_DEV, tr, width), lambda l, i: (0, (l * stride + base) // tr + i, 0)), nat, nat, nat],
        out_specs=[nat] * 4, compiler_params=_params(("parallel", "parallel")))(landed, w, m, v)


def _adamw_conv(landed, base, stride, w, m, v):
    depth, taps, cs = w.shape

    def body(p_ref, w_ref, m_ref, v_ref, *o_refs):
        g = _sum_blocks(p_ref)[:taps, :cs]
        for o, val in zip(o_refs, _adamw_math(g, w_ref[...], m_ref[...], v_ref[...]), strict=True):
            o[...] = val

    nat = pl.BlockSpec((None, taps, cs), lambda l: (l, 0, 0))
    return pl.pallas_call(
        body, name="adamw_conv", grid=(depth,), out_shape=[jax.ShapeDtypeStruct(w.shape, F32)] * 4,
        in_specs=[pl.BlockSpec((N_DEV, CONV_HALO, LANES), lambda l: (0, (l * stride + base) // CONV_HALO, 0)), nat, nat, nat],
        out_specs=[nat] * 4, compiler_params=_params(("parallel",)))(landed, w, m, v)


def _unshard_cols(name, gathered, start, rows, cs, shift=0):
    n_slab = -(-cs // LANES)
    total = N_DEV * cs
    tr = _pick(rows, (SLAB_TILE,))

    def body(*refs):
        slabs, o_ref = refs[:n_slab], refs[n_slab]
        for j in range(N_DEV):
            for k, s in enumerate(slabs):
                for src, dst, width in _wrapped(j * cs + LANES * k - shift, min(LANES, cs - LANES * k), total):
                    o_ref[:, dst:dst + width] = s[j, :, src:src + width]

    slab = lambda k: pl.BlockSpec((N_DEV, tr, LANES), lambda i: (0, (start + k * rows) // tr + i, 0))
    return pl.pallas_call(
        body, name=name, grid=(rows // tr,), out_shape=jax.ShapeDtypeStruct((rows, total), gathered.dtype),
        in_specs=[slab(k) for k in range(n_slab)], out_specs=pl.BlockSpec((tr, total), lambda i: (i, 0)),
        compiler_params=_params(("parallel",)))(*[gathered] * n_slab)


def _shard_cols(name, full, cs, shift=0):
    rows, total = full.shape
    n_slab = -(-cs // LANES)
    tr = _pick(rows, (SLAB_TILE,))

    def body(f_ref, o_ref):
        for j in range(N_DEV):
            for k in range(n_slab):
                used = min(LANES, cs - LANES * k)
                for src, dst, width in _wrapped(j * cs + LANES * k - shift, used, total):
                    o_ref[j, k, :, src:src + width] = f_ref[:, dst:dst + width].astype(o_ref.dtype)
                if used < LANES:
                    o_ref[j, k, :, used:] = jnp.zeros((tr, LANES - used), o_ref.dtype)

    out = pl.pallas_call(
        body, name=name, grid=(rows // tr,), out_shape=jax.ShapeDtypeStruct((N_DEV, n_slab, rows, LANES), BF16),
        in_specs=[pl.BlockSpec((tr, total), lambda i: (i, 0))],
        out_specs=pl.BlockSpec((N_DEV, n_slab, tr, LANES), lambda i: (0, 0, i, 0)),
        compiler_params=_params(("parallel",)))(full)
    return out.reshape(N_DEV, n_slab * rows, LANES)


def _wrapped(pos, width, total):
    pos %= total
    if pos + width <= total:
        return [(0, pos, width)]
    head = total - pos
    return [(0, pos, head), (head, 0, width - head)]


CONV_W_PIECES = 3


def _pad_to(n, align):
    return -(-n // align) * align


def _layout(group):
    col_names, row_names, with_conv = GROUPS[group]
    dims = {name: (rows, cols) for name, rows, cols, _ in SHARDED}
    col, off = {}, 0
    for name in col_names:
        rows, cols = dims[name]
        cs = cols // N_DEV
        col[name] = (off, rows, cs)
        off += -(-cs // LANES) * rows
    conv_base = off
    col_rows = _pad_to(off + with_conv * CONV_W_PIECES * CONV_HALO, SLAB_TILE)
    return col, conv_base, col_rows, {name: dims[name][0] // N_DEV for name in row_names}


def _slabs(shard, fill):
    rows, cs = shard.shape
    parts = []
    for k in range(-(-cs // LANES)):
        part = shard[:, LANES * k:min(LANES * (k + 1), cs)]
        parts.append(jnp.pad(part, ((0, 0), (0, LANES - part.shape[1])), constant_values=fill))
    return jnp.concatenate(parts, axis=0)


def _concat_padded(pieces, total, axis):
    used = sum(p.shape[axis] for p in pieces)
    if total > used:
        shape = list(pieces[0].shape)
        shape[axis] = total - used
        pieces = pieces + [jnp.zeros(shape, pieces[0].dtype)]
    return jnp.concatenate(pieces, axis=axis)


def _split3(a):
    hi = a.astype(BF16)
    r1 = a - hi.astype(F32)
    mid = r1.astype(BF16)
    return hi, mid, (r1 - mid.astype(F32)).astype(BF16)


def _pack_small(arrs, lead=()):
    flat = jnp.concatenate([a.reshape(lead + (-1,)) for a in arrs], axis=-1)
    total = _pad_to(flat.shape[-1], 512 * LANES)
    flat = jnp.pad(flat, [(0, 0)] * len(lead) + [(0, total - flat.shape[-1])])
    return flat.reshape(lead + (total // LANES, LANES))


def _unpack_small(flat, shapes):
    flat = flat.reshape(-1)
    res, off = [], 0
    for s in shapes:
        n = int(np.prod(s))
        res.append(flat[off:off + n].reshape(s))
        off += n
    return res


def _small_rows(a, depth):
    n16 = depth * SSM_GROUPS
    a_re, a_im, f_re, f_im = _ssm_coeffs(a["ssm_lambda_re"].reshape(n16, SSM_STATE), a["ssm_lambda_im"].reshape(n16, SSM_STATE),
                                         a["ssm_log_dt"].reshape(n16, 1))
    rows = []
    for l in range(depth):
        sp = {k: _row(a[k][l]) for k in ("mix_norm_g", "b_gate", "attn_sinks", "ssm_d", "b_ssm_glu", "conv_dw_b",
                                         "conv_norm_g", "conv_norm_b", "ffn_norm_g", "ple_norm_g")}
        g = slice(l * SSM_GROUPS, (l + 1) * SSM_GROUPS)
        sp["a_row"] = jnp.concatenate([a_re[g].reshape(1, -1), a_im[g].reshape(1, -1)], axis=1)
        sp["f_row"] = jnp.concatenate([f_re[g].reshape(1, -1), f_im[g].reshape(1, -1)], axis=1)
        sp["bmat"], sp["cmat"] = _ssm_mats(a["ssm_b_re"][l], a["ssm_b_im"][l], a["ssm_c_re"][l], a["ssm_c_im"][l])
        rows.append(sp)
    return rows


def _local_step(a, get_weights, on_grads, depth):
    n_seq, seq, d = a["x"].shape
    t = n_seq * seq
    inv = ROPE_THETA ** (-jnp.arange(0, ROPE_DIM, 2, dtype=F32) / ROPE_DIM)
    lane = np.arange(LANES) % HEAD_DIM
    inv_lane = jnp.where(lane < ROPE_DIM, jnp.tile(inv, LANES // (ROPE_DIM // 2)), 0.0).reshape(1, LANES)
    ctab, stab = _rope_tables(a["positions"].reshape(t), inv_lane)
    small = _small_rows(a, depth)

    x = a["x"].reshape(t, d)
    saved, weights = [], []
    for l in range(depth):
        p_all = a["p"].reshape(depth * t, -1)
        x, sv, w = _layer_fwd(x, (p_all, p_all.shape[1], 0, l * t), get_weights(l, "mix", x),
                              functools.partial(get_weights, l, "ffn"), small[l], ctab, stab, n_seq)
        saved.append(sv)
        weights.append(w)
    loss, dx, d_final = _loss_and_grad(x, a["loss_target"].reshape(t, d), _row(a["final_norm_g"]))
    gws, gss = [None] * depth, [None] * depth
    for l in reversed(range(depth)):
        dx, gws[l], gss[l] = _layer_bwd(dx, saved[l], weights[l], small[l], functools.partial(on_grads, l), ctab, stab, n_seq)

    n16 = depth * SSM_GROUPS
    halves = lambda k, h: jnp.concatenate([gss[l][k][:, h * SSM_LANES:(h + 1) * SSM_LANES].reshape(SSM_GROUPS, SSM_STATE)
                                           for l in range(depth)], axis=0)
    dlr, dli, ddt = _ssm_coeffs_bwd(a["ssm_lambda_re"].reshape(n16, SSM_STATE), a["ssm_lambda_im"].reshape(n16, SSM_STATE),
                                    a["ssm_log_dt"].reshape(n16, 1),
                                    (halves("a_row", 0), halves("a_row", 1), halves("f_row", 0), halves("f_row", 1)))
    bc = [_ssm_mats_t(gss[l]["bmat"], gss[l]["cmat"]) for l in range(depth)]
    gsmall = {k: jnp.stack([gss[l][k].reshape(a[k].shape[1:]) for l in range(depth)])
              for k in ("mix_norm_g", "b_gate", "attn_sinks", "ssm_d", "b_ssm_glu", "conv_dw_b", "conv_norm_g", "conv_norm_b",
                        "ffn_norm_g", "ple_norm_g")}
    gsmall["ssm_lambda_re"] = dlr.reshape(a["ssm_lambda_re"].shape)
    gsmall["ssm_lambda_im"] = dli.reshape(a["ssm_lambda_im"].shape)
    gsmall["ssm_log_dt"] = ddt.reshape(a["ssm_log_dt"].shape)
    for i, k in enumerate(("ssm_b_re", "ssm_b_im", "ssm_c_re", "ssm_c_im")):
        gsmall[k] = jnp.stack([bc[l][i] for l in range(depth)])
    gsmall["final_norm_g"] = d_final.reshape(a["final_norm_g"].shape)
    return loss, dx.reshape(n_seq, seq, d), gws, gsmall


def kernel(x, p, positions, mix_norm_g, w_in, b_gate, attn_sinks, w_attn_out, ssm_lambda_re, ssm_lambda_im, ssm_log_dt, ssm_b_re, ssm_b_im, ssm_c_re, ssm_c_im, ssm_d, w_ssm_glu, b_ssm_glu, conv_dw_w, conv_dw_b, conv_norm_g, conv_norm_b, w_conv_out, w_mix_out, ffn_norm_g, w_ffn_in, w_ffn_out, w_ple_in, ple_norm_g, w_ple_gate, final_norm_g, loss_target, m_mix_norm_g, m_w_in, m_b_gate, m_attn_sinks, m_w_attn_out, m_ssm_lambda_re, m_ssm_lambda_im, m_ssm_log_dt, m_ssm_b_re, m_ssm_b_im, m_ssm_c_re, m_ssm_c_im, m_ssm_d, m_w_ssm_glu, m_b_ssm_glu, m_conv_dw_w, m_conv_dw_b, m_conv_norm_g, m_conv_norm_b, m_w_conv_out, m_w_mix_out, m_ffn_norm_g, m_w_ffn_in, m_w_ffn_out, m_w_ple_in, m_ple_norm_g, m_w_ple_gate, m_final_norm_g, v_mix_norm_g, v_w_in, v_b_gate, v_attn_sinks, v_w_attn_out, v_ssm_lambda_re, v_ssm_lambda_im, v_ssm_log_dt, v_ssm_b_re, v_ssm_b_im, v_ssm_c_re, v_ssm_c_im, v_ssm_d, v_w_ssm_glu, v_b_ssm_glu, v_conv_dw_w, v_conv_dw_b, v_conv_norm_g, v_conv_norm_b, v_w_conv_out, v_w_mix_out, v_ffn_norm_g, v_w_ffn_in, v_w_ffn_out, v_w_ple_in, v_ple_norm_g, v_w_ple_gate, v_final_norm_g):
    a = dict(locals())
    depth = w_in.shape[0]
    layouts = {group: _layout(group) for group in GROUPS}
    shift = {"w_in": Z_SPLIT}
    conv_pad = ((0, 0), (0, CONV_HALO - CONV_K), (0, LANES - CONV_WIDTH // N_DEV))

    def packed_weights(l, group, fill):
        col, _, col_rows, row = layouts[group]
        pieces = [_slabs(a[name][l].astype(BF16), fill) for name in col]
        if GROUPS[group][2]:
            pieces.append(jnp.pad(jnp.stack(_split3(a["conv_dw_w"][l])), conv_pad).reshape(-1, LANES))
        return [_concat_padded(pieces, col_rows, 0)] + [a[name][l].astype(BF16) for name in row]

    gathers, tokens, fill = [], [], jnp.zeros((), BF16)
    for l in range(depth):
        started, token = _exchange_start(f"gather_start_{l}", "gather", [packed_weights(l, group, fill) for group in GROUPS])
        gathers.append(dict(zip(GROUPS, started, strict=True)))
        tokens.append(token[0:1, 0:1])
        fill = token[0, 0].astype(BF16)

    def get_weights(l, group, after):
        col, conv_base, _, row = layouts[group]
        slab8, *rows8 = _exchange_wait(f"gather_wait_{group}_{l}", gathers[l][group], after)
        w = {name: _unshard_cols("unshard_" + name, slab8, base, rows, cs, shift.get(name, 0))
             for name, (base, rows, cs) in col.items()}
        for (name, rs), gathered in zip(row.items(), rows8, strict=True):
            w[name] = gathered.reshape(N_DEV * rs, -1)
        if GROUPS[group][2]:
            conv = slab8[:, conv_base:conv_base + CONV_W_PIECES * CONV_HALO]
            conv = conv.reshape(N_DEV, CONV_W_PIECES, CONV_HALO, LANES)[:, :, :CONV_K, :CONV_WIDTH // N_DEV].astype(F32)
            w["conv_dw_w"] = jnp.sum(conv, axis=1).transpose(1, 0, 2).reshape(CONV_K, CONV_WIDTH)
        return w

    scatters = {}

    def on_grads(l, group, gw):
        col, _, col_rows, row = layouts[group]
        pieces = [_shard_cols("shard_" + name, gw[name], cs, shift.get(name, 0)) for name, (_, _, cs) in col.items()]
        if GROUPS[group][2]:
            conv = gw["conv_dw_w"].reshape(CONV_K, N_DEV, CONV_WIDTH // N_DEV).transpose(1, 0, 2).astype(BF16)
            pieces.append(jnp.pad(jnp.pad(conv, conv_pad), ((0, 0), (0, (CONV_W_PIECES - 1) * CONV_HALO), (0, 0))))
        by_shard = [gw[name].astype(BF16).reshape(N_DEV, rs, -1) for name, rs in row.items()]
        (scatters[l, group],), token = _exchange_start(
            f"grads_start_{group}_{l}", "scatter", [[_concat_padded(pieces, col_rows, 1)] + by_shard])
        return token

    local = dict(a, mix_norm_g=a["mix_norm_g"] + sum(tokens))
    loss, grad_x, _, gsmall = _local_step(local, get_weights, on_grads, depth)
    loss = lax.psum(loss, ("x", "y", "c"))

    shapes = [a[k].shape for k in REPLICATED]
    parts, = _all_gather("gather_small_grads", [_pack_small([gsmall[k] for k in REPLICATED])])
    small_state = [_pack_small([a[pre + k] for k in REPLICATED]) for pre in ("", "m_", "v_")]
    small_flat = _adamw_flat("adamw_replicated", parts, *small_state)
    small = [dict(zip(REPLICATED, _unpack_small(o, shapes), strict=True)) for o in small_flat]

    state = lambda name: (a[name], a["m_" + name], a["v_" + name])
    big, after = {}, small_flat[1]
    for group in ("ffn", "mix"):
        col, conv_base, col_rows, row = layouts[group]
        landed = [_exchange_wait(f"grads_wait_{group}_{l}", scatters[l, group], after) for l in range(depth)]
        landed_slab = jnp.concatenate([arrays[0] for arrays in landed], axis=1)
        big.update({name: _adamw_cols("adamw_" + name, landed_slab, base, col_rows, *state(name)) for name, (base, _, _) in col.items()})
        for i, (name, rs) in enumerate(row.items()):
            landed_rows = jnp.concatenate([arrays[1 + i] for arrays in landed], axis=1)
            big[name] = _adamw_rows("adamw_" + name, landed_rows, 0, rs, *state(name))
        if GROUPS[group][2]:
            big["conv_dw_w"] = _adamw_conv(landed_slab, conv_base, col_rows, *state("conv_dw_w"))
        after = big[next(iter(col))][1]

    def result(kind, name):
        if name in REPLICATED:
            return small[kind][name]
        return big[name][kind]

    return (loss, grad_x, *[result(kind, n) for kind in range(4) for n in WEIGHT_ORDER])
```
